```python
import math
import jax, jax.numpy as jnp
from jax import lax
import numpy as np

D_MODEL = 1024
BATCH = 8
SEQ = 4096
DEPTH = 1

M_HEADS = 4
M_HEAD_DIM = 128
M_WIDTH = M_HEADS * M_HEAD_DIM
M_CHUNK = 64
A_HEADS = 8
A_NOPE = 64
A_ROPE = 32
A_V = 64
A_WIDTH = A_HEADS * A_V
Q_RANK = 384
KV_RANK = 256
ROPE_THETA = 10000.0
Q_BLOCK = 128
D_MIX = M_WIDTH + A_WIDTH
D_FF = ((8 * D_MODEL // 3 + 255) // 256) * 256
EPS = 1e-6
IN_SPLITS = (M_WIDTH, M_WIDTH, M_WIDTH, M_WIDTH, M_HEADS, M_HEADS, Q_RANK, KV_RANK, A_ROPE)
D_IN = sum(IN_SPLITS)

kernel_name = "hymba_mlstm_mla_swiglu"


def rmsnorm(x, w):
    xf = x.astype(jnp.float32)
    y = xf * lax.rsqrt(jnp.mean(xf * xf, axis=-1, keepdims=True) + EPS)
    return (y * w.astype(jnp.float32)).astype(x.dtype)


def apply_rope(x, positions):
    half = A_ROPE // 2
    inv_freq = ROPE_THETA ** (-jnp.arange(half, dtype=jnp.float32) / half)
    ang = positions.astype(jnp.float32)[:, :, None, None] * inv_freq
    cos, sin = jnp.cos(ang), jnp.sin(ang)
    xf = x.astype(jnp.float32)
    x1, x2 = xf[..., :half], xf[..., half:]
    return jnp.concatenate([x1 * cos - x2 * sin, x2 * cos + x1 * sin], axis=-1).astype(x.dtype)


def mlstm_chunkwise(q, k, v, i_raw, f_raw):
    B, S, H, Dh = q.shape
    L = M_CHUNK
    NC = S // L
    f32 = jnp.float32

    def to_chunks(t):
        return t.astype(f32).reshape(B, NC, L, H, -1).transpose(0, 3, 1, 2, 4)

    def gate_chunks(t):
        return t.astype(f32).reshape(B, NC, L, H).transpose(0, 3, 1, 2)

    qc = to_chunks(q) * (Dh ** -0.5)
    kc = to_chunks(k)
    vc = to_chunks(v)
    logi = gate_chunks(i_raw)
    logf = jax.nn.log_sigmoid(gate_chunks(f_raw))
    b = jnp.cumsum(logf, axis=-1)
    g = b[..., -1]

    a = g[..., None] - b + logi
    m_loc = jnp.max(a, axis=-1)
    w = jnp.exp(a - m_loc[..., None])
    C_loc = jnp.einsum('bhcl,bhcld,bhcle->bhcde', w, kc, vc)
    n_loc = jnp.einsum('bhcl,bhcld->bhcd', w, kc)

    def step(carry, xs):
        C, n, m = carry
        g_c, m_l, C_l, n_l = xs
        m_new = jnp.maximum(g_c + m, m_l)
        s_old = jnp.exp(g_c + m - m_new)
        s_loc = jnp.exp(m_l - m_new)
        C_new = s_old[..., None, None] * C + s_loc[..., None, None] * C_l
        n_new = s_old[..., None] * n + s_loc[..., None] * n_l
        return (C_new, n_new, m_new), (C, n, m)

    init = (jnp.zeros((B, H, Dh, vc.shape[-1]), f32),
            jnp.zeros((B, H, Dh), f32),
            jnp.zeros((B, H), f32))
    xs = (jnp.moveaxis(g, 2, 0), jnp.moveaxis(m_loc, 2, 0),
          jnp.moveaxis(C_loc, 2, 0), jnp.moveaxis(n_loc, 2, 0))
    _, (C_prev, n_prev, m_prev) = lax.scan(step, init, xs)
    C_prev = jnp.moveaxis(C_prev, 0, 2)
    n_prev = jnp.moveaxis(n_prev, 0, 2)
    m_prev = jnp.moveaxis(m_prev, 0, 2)

    causal = jnp.tril(jnp.ones((L, L), dtype=bool))
    d_log = b[..., :, None] - b[..., None, :] + logi[..., None, :]
    d_log = jnp.where(causal, d_log, -jnp.inf)
    inter_log = b + m_prev[..., None]
    m_t = jnp.maximum(jnp.max(d_log, axis=-1), inter_log)
    p = jnp.exp(d_log - m_t[..., None]) * jnp.einsum('bhcld,bhcsd->bhcls', qc, kc)
    s_inter = jnp.exp(inter_log - m_t)
    num = (jnp.einsum('bhcls,bhcse->bhcle', p, vc)
           + s_inter[..., None] * jnp.einsum('bhcld,bhcde->bhcle', qc, C_prev))
    den = jnp.sum(p, axis=-1) + s_inter * jnp.einsum('bhcld,bhcd->bhcl', qc, n_prev)
    den = jnp.maximum(jnp.abs(den), jnp.exp(-m_t))
    h = num / den[..., None]
    return h.transpose(0, 2, 3, 1, 4).reshape(B, S, H, -1).astype(q.dtype)


def mla_attention(q_nope, q_rope, k_nope, k_rope, v):
    B, S, H, _ = q_nope.shape
    NQ = S // Q_BLOCK
    scale = (A_NOPE + A_ROPE) ** -0.5
    qn = q_nope.reshape(B, NQ, Q_BLOCK, H, A_NOPE).swapaxes(0, 1)
    qr = q_rope.reshape(B, NQ, Q_BLOCK, H, A_ROPE).swapaxes(0, 1)
    kr = k_rope[:, :, 0, :]
    k_pos = jnp.arange(S)

    def block(args):
        idx, qn_b, qr_b = args
        s = (jnp.einsum('bqhd,bkhd->bhqk', qn_b, k_nope)
             + jnp.einsum('bqhd,bkd->bhqk', qr_b, kr)).astype(jnp.float32) * scale
        q_pos = idx * Q_BLOCK + jnp.arange(Q_BLOCK)
        s = jnp.where(k_pos[None, :] <= q_pos[:, None], s, -jnp.inf)
        p = jax.nn.softmax(s, axis=-1).astype(v.dtype)
        return jnp.einsum('bhqk,bkhd->bqhd', p, v)

    out = lax.map(block, (jnp.arange(NQ), qn, qr))
    return out.swapaxes(0, 1).reshape(B, S, H * A_V)


def _fwd_setup_inputs(seed: int = 0) -> dict:
    key = jax.random.key(seed)
    ks = jax.random.split(key, 20)
    f32 = jnp.float32

    def nrm(k, shape, scale):
        return jax.random.normal(k, shape, f32) * scale

    def gain(k, shape):
        return 1.0 + 0.01 * jax.random.normal(k, shape, f32)

    x = jax.random.normal(ks[0], (BATCH, SEQ, D_MODEL), f32)
    offsets = jax.random.randint(ks[1], (BATCH, 1), 0, 1024, dtype=jnp.int32)
    positions = (jnp.arange(SEQ, dtype=jnp.int32)[None, :] + offsets).astype(jnp.int32)
    i_bias = 0.1 * jax.random.normal(ks[2], (DEPTH, M_HEADS), f32)
    f_bias = jnp.linspace(3.0, 6.0, M_HEADS, dtype=f32)[None, :] + 0.1 * jax.random.normal(ks[3], (DEPTH, M_HEADS), f32)
    return {
        "x": x,
        "positions": positions,
        "attn_norm_w": gain(ks[4], (DEPTH, D_MODEL)),
        "w_in": nrm(ks[5], (DEPTH, D_MODEL, D_IN), D_MODEL ** -0.5),
        "b_gates": jnp.concatenate([i_bias, f_bias], axis=-1),
        "mlstm_norm_w": gain(ks[6], (DEPTH, M_HEADS, M_HEAD_DIM)),
        "q_a_norm_w": gain(ks[7], (DEPTH, Q_RANK)),
        "w_q_b": nrm(ks[8], (DEPTH, Q_RANK, A_HEADS * (A_NOPE + A_ROPE)), Q_RANK ** -0.5),
        "kv_a_norm_w": gain(ks[9], (DEPTH, KV_RANK)),
        "w_kv_b": nrm(ks[10], (DEPTH, KV_RANK, A_HEADS * (A_NOPE + A_V)), KV_RANK ** -0.5),
        "w_out": nrm(ks[11], (DEPTH, D_MIX, D_MODEL), D_MIX ** -0.5),
        "ffn_norm_w": gain(ks[12], (DEPTH, D_MODEL)),
        "w_gate": nrm(ks[13], (DEPTH, D_MODEL, D_FF), D_MODEL ** -0.5),
        "w_up": nrm(ks[14], (DEPTH, D_MODEL, D_FF), D_MODEL ** -0.5),
        "w_down": nrm(ks[15], (DEPTH, D_FF, D_MODEL), D_FF ** -0.5),
        "final_norm_w": gain(ks[16], (D_MODEL,)),
    }


def _fwd_reference(x, positions, attn_norm_w, w_in, b_gates, mlstm_norm_w, q_a_norm_w, w_q_b,
              kv_a_norm_w, w_kv_b, w_out, ffn_norm_w, w_gate, w_up, w_down, final_norm_w):
    B, S = x.shape[0], x.shape[1]
    split_points = np.cumsum(np.array(IN_SPLITS))[:-1].tolist()
    h = x
    for l in range(DEPTH):
        u = rmsnorm(h, attn_norm_w[l])
        proj = u @ w_in[l]
        mq, mk, mv, mo, mi, mf, qa, kva, kr = jnp.split(proj, split_points, axis=-1)

        mi = mi + b_gates[l, :M_HEADS]
        mf = mf + b_gates[l, M_HEADS:]
        to_heads = lambda t: t.reshape(B, S, M_HEADS, M_HEAD_DIM)
        hm = mlstm_chunkwise(to_heads(mq), to_heads(mk), to_heads(mv), mi, mf)
        hm = rmsnorm(hm, mlstm_norm_w[l]).reshape(B, S, M_WIDTH)
        hm = hm * jax.nn.sigmoid(mo)

        cq = (rmsnorm(qa, q_a_norm_w[l]) @ w_q_b[l]).reshape(B, S, A_HEADS, A_NOPE + A_ROPE)
        q_nope, q_rope = cq[..., :A_NOPE], apply_rope(cq[..., A_NOPE:], positions)
        ckv = (rmsnorm(kva, kv_a_norm_w[l]) @ w_kv_b[l]).reshape(B, S, A_HEADS, A_NOPE + A_V)
        k_nope, v = ckv[..., :A_NOPE], ckv[..., A_NOPE:]
        k_rope = apply_rope(kr[:, :, None, :], positions)
        ha = mla_attention(q_nope, q_rope, k_nope, k_rope, v)

        h = h + jnp.concatenate([hm, ha], axis=-1) @ w_out[l]

        u = rmsnorm(h, ffn_norm_w[l])
        h = h + (jax.nn.silu(u @ w_gate[l]) * (u @ w_up[l])) @ w_down[l]
    return rmsnorm(h, final_norm_w)


import jax as _jax
import jax.numpy as _jnp

TWIN_FORMAT = 'train_step'
FWD_PARAMS = ['x', 'positions', 'attn_norm_w', 'w_in', 'b_gates', 'mlstm_norm_w', 'q_a_norm_w', 'w_q_b', 'kv_a_norm_w', 'w_kv_b', 'w_out', 'ffn_norm_w', 'w_gate', 'w_up', 'w_down', 'final_norm_w']
TWIN_WEIGHTS = ['attn_norm_w', 'w_in', 'b_gates', 'mlstm_norm_w', 'q_a_norm_w', 'w_q_b', 'kv_a_norm_w', 'w_kv_b', 'w_out', 'ffn_norm_w', 'w_gate', 'w_up', 'w_down', 'final_norm_w']
TWIN_DIFF_INPUT = 'x'
TWIN_INPUTS = ['x', 'positions', 'attn_norm_w', 'w_in', 'b_gates', 'mlstm_norm_w', 'q_a_norm_w', 'w_q_b', 'kv_a_norm_w', 'w_kv_b', 'w_out', 'ffn_norm_w', 'w_gate', 'w_up', 'w_down', 'final_norm_w', 'loss_target', 'm_attn_norm_w', 'm_w_in', 'm_b_gates', 'm_mlstm_norm_w', 'm_q_a_norm_w', 'm_w_q_b', 'm_kv_a_norm_w', 'm_w_kv_b', 'm_w_out', 'm_ffn_norm_w', 'm_w_gate', 'm_w_up', 'm_w_down', 'm_final_norm_w', 'v_attn_norm_w', 'v_w_in', 'v_b_gates', 'v_mlstm_norm_w', 'v_q_a_norm_w', 'v_w_q_b', 'v_kv_a_norm_w', 'v_w_kv_b', 'v_w_out', 'v_ffn_norm_w', 'v_w_gate', 'v_w_up', 'v_w_down', 'v_final_norm_w']
TWIN_OUTPUTS = ['loss', 'grad_x', 'grad_attn_norm_w', 'grad_w_in', 'grad_b_gates', 'grad_mlstm_norm_w', 'grad_q_a_norm_w', 'grad_w_q_b', 'grad_kv_a_norm_w', 'grad_w_kv_b', 'grad_w_out', 'grad_ffn_norm_w', 'grad_w_gate', 'grad_w_up', 'grad_w_down', 'grad_final_norm_w', 'delta_attn_norm_w', 'delta_w_in', 'delta_b_gates', 'delta_mlstm_norm_w', 'delta_q_a_norm_w', 'delta_w_q_b', 'delta_kv_a_norm_w', 'delta_w_kv_b', 'delta_w_out', 'delta_ffn_norm_w', 'delta_w_gate', 'delta_w_up', 'delta_w_down', 'delta_final_norm_w', 'new_m_attn_norm_w', 'new_m_w_in', 'new_m_b_gates', 'new_m_mlstm_norm_w', 'new_m_q_a_norm_w', 'new_m_w_q_b', 'new_m_kv_a_norm_w', 'new_m_w_kv_b', 'new_m_w_out', 'new_m_ffn_norm_w', 'new_m_w_gate', 'new_m_w_up', 'new_m_w_down', 'new_m_final_norm_w', 'new_v_attn_norm_w', 'new_v_w_in', 'new_v_b_gates', 'new_v_mlstm_norm_w', 'new_v_q_a_norm_w', 'new_v_w_q_b', 'new_v_kv_a_norm_w', 'new_v_w_kv_b', 'new_v_w_out', 'new_v_ffn_norm_w', 'new_v_w_gate', 'new_v_w_up', 'new_v_w_down', 'new_v_final_norm_w']
TWIN_LEAF_KINDS = {'loss': 'loss', 'grad_x': 'grad_x', 'grad_attn_norm_w': 'grad_w', 'grad_w_in': 'grad_w', 'grad_b_gates': 'grad_w', 'grad_mlstm_norm_w': 'grad_w', 'grad_q_a_norm_w': 'grad_w', 'grad_w_q_b': 'grad_w', 'grad_kv_a_norm_w': 'grad_w', 'grad_w_kv_b': 'grad_w', 'grad_w_out': 'grad_w', 'grad_ffn_norm_w': 'grad_w', 'grad_w_gate': 'grad_w', 'grad_w_up': 'grad_w', 'grad_w_down': 'grad_w', 'grad_final_norm_w': 'grad_w', 'delta_attn_norm_w': 'delta_w', 'delta_w_in': 'delta_w', 'delta_b_gates': 'delta_w', 'delta_mlstm_norm_w': 'delta_w', 'delta_q_a_norm_w': 'delta_w', 'delta_w_q_b': 'delta_w', 'delta_kv_a_norm_w': 'delta_w', 'delta_w_kv_b': 'delta_w', 'delta_w_out': 'delta_w', 'delta_ffn_norm_w': 'delta_w', 'delta_w_gate': 'delta_w', 'delta_w_up': 'delta_w', 'delta_w_down': 'delta_w', 'delta_final_norm_w': 'delta_w', 'new_m_attn_norm_w': 'new_m', 'new_m_w_in': 'new_m', 'new_m_b_gates': 'new_m', 'new_m_mlstm_norm_w': 'new_m', 'new_m_q_a_norm_w': 'new_m', 'new_m_w_q_b': 'new_m', 'new_m_kv_a_norm_w': 'new_m', 'new_m_w_kv_b': 'new_m', 'new_m_w_out': 'new_m', 'new_m_ffn_norm_w': 'new_m', 'new_m_w_gate': 'new_m', 'new_m_w_up': 'new_m', 'new_m_w_down': 'new_m', 'new_m_final_norm_w': 'new_m', 'new_v_attn_norm_w': 'new_v', 'new_v_w_in': 'new_v', 'new_v_b_gates': 'new_v', 'new_v_mlstm_norm_w': 'new_v', 'new_v_q_a_norm_w': 'new_v', 'new_v_w_q_b': 'new_v', 'new_v_kv_a_norm_w': 'new_v', 'new_v_w_kv_b': 'new_v', 'new_v_w_out': 'new_v', 'new_v_ffn_norm_w': 'new_v', 'new_v_w_gate': 'new_v', 'new_v_w_up': 'new_v', 'new_v_w_down': 'new_v', 'new_v_final_norm_w': 'new_v'}


def _forward(args):
    return _fwd_reference(*[args[k] for k in FWD_PARAMS])


def _output_shape():
    out = _jax.eval_shape(lambda: _forward(_fwd_setup_inputs(0)))
    return out.shape, out.dtype

N_MICROBATCH = 1
ADAM_LR = 0.001
ADAM_B1 = 0.9
ADAM_B2 = 0.999
ADAM_EPS = 1e-08
ADAM_WD = 0.01
ADAM_STEP = 10
PER_EXAMPLE_BATCH_AXIS = {'x': 0, 'positions': 0, 'loss_target': 0}
SHARED_INPUTS = []
_WEIGHT_DTYPES = {'attn_norm_w': _jnp.float32, 'w_in': _jnp.float32, 'b_gates': _jnp.float32, 'mlstm_norm_w': _jnp.float32, 'q_a_norm_w': _jnp.float32, 'w_q_b': _jnp.float32, 'kv_a_norm_w': _jnp.float32, 'w_kv_b': _jnp.float32, 'w_out': _jnp.float32, 'ffn_norm_w': _jnp.float32, 'w_gate': _jnp.float32, 'w_up': _jnp.float32, 'w_down': _jnp.float32, 'final_norm_w': _jnp.float32}
MOMENT_SCALE = {'attn_norm_w': 1.562115e-01, 'w_in': 9.234314e-02, 'b_gates': 4.611587e-01, 'mlstm_norm_w': 1.026056e-01, 'q_a_norm_w': 3.701196e-02, 'w_q_b': 2.736136e-02, 'kv_a_norm_w': 6.617911e-02, 'w_kv_b': 3.376909e-02, 'w_out': 7.532547e-02, 'ffn_norm_w': 1.413313e-01, 'w_gate': 5.388129e-02, 'w_up': 5.209110e-02, 'w_down': 8.602592e-02, 'final_norm_w': 3.199067e+01}


def _to_microbatches(a, axis):
    t = _jnp.moveaxis(a, axis, 0)
    t = t.reshape((N_MICROBATCH, t.shape[0] // N_MICROBATCH) + t.shape[1:])
    return _jnp.moveaxis(t, 1, axis + 1)


def setup_inputs(seed: int = 0) -> dict:
    inp = _fwd_setup_inputs(seed)
    key = _jax.random.fold_in(_jax.random.key(seed), 7919)
    shape, _ = _output_shape()
    out = dict(inp)
    out["loss_target"] = _jax.random.normal(_jax.random.fold_in(key, 0), shape, _jnp.float32)
    for i, name in enumerate(TWIN_WEIGHTS):
        w = inp[name].astype(_jnp.float32)
        if MOMENT_SCALE is None:
            s = _jnp.sqrt(_jnp.mean(_jnp.square(w)) + 1e-30)
        else:
            s = MOMENT_SCALE[name]
        km, kv = _jax.random.split(_jax.random.fold_in(key, i + 1))
        out[name] = w
        out["m_" + name] = s * _jax.random.normal(km, w.shape, _jnp.float32)
        out["v_" + name] = (s * s) * _jax.random.uniform(kv, w.shape, _jnp.float32, 0.5, 1.5)
    if N_MICROBATCH > 1:
        for name, axis in PER_EXAMPLE_BATCH_AXIS.items():
            out[name] = _to_microbatches(out[name], axis)
    return {'x': out['x'], 'positions': out['positions'], 'attn_norm_w': out['attn_norm_w'], 'w_in': out['w_in'], 'b_gates': out['b_gates'], 'mlstm_norm_w': out['mlstm_norm_w'], 'q_a_norm_w': out['q_a_norm_w'], 'w_q_b': out['w_q_b'], 'kv_a_norm_w': out['kv_a_norm_w'], 'w_kv_b': out['w_kv_b'], 'w_out': out['w_out'], 'ffn_norm_w': out['ffn_norm_w'], 'w_gate': out['w_gate'], 'w_up': out['w_up'], 'w_down': out['w_down'], 'final_norm_w': out['final_norm_w'], 'loss_target': out['loss_target'], 'm_attn_norm_w': out['m_attn_norm_w'], 'm_w_in': out['m_w_in'], 'm_b_gates': out['m_b_gates'], 'm_mlstm_norm_w': out['m_mlstm_norm_w'], 'm_q_a_norm_w': out['m_q_a_norm_w'], 'm_w_q_b': out['m_w_q_b'], 'm_kv_a_norm_w': out['m_kv_a_norm_w'], 'm_w_kv_b': out['m_w_kv_b'], 'm_w_out': out['m_w_out'], 'm_ffn_norm_w': out['m_ffn_norm_w'], 'm_w_gate': out['m_w_gate'], 'm_w_up': out['m_w_up'], 'm_w_down': out['m_w_down'], 'm_final_norm_w': out['m_final_norm_w'], 'v_attn_norm_w': out['v_attn_norm_w'], 'v_w_in': out['v_w_in'], 'v_b_gates': out['v_b_gates'], 'v_mlstm_norm_w': out['v_mlstm_norm_w'], 'v_q_a_norm_w': out['v_q_a_norm_w'], 'v_w_q_b': out['v_w_q_b'], 'v_kv_a_norm_w': out['v_kv_a_norm_w'], 'v_w_kv_b': out['v_w_kv_b'], 'v_w_out': out['v_w_out'], 'v_ffn_norm_w': out['v_ffn_norm_w'], 'v_w_gate': out['v_w_gate'], 'v_w_up': out['v_w_up'], 'v_w_down': out['v_w_down'], 'v_final_norm_w': out['v_final_norm_w']}


def _loss(weights, diff, rest, loss_target):
    with _jax.named_scope("forward"):
        args = {**rest, TWIN_DIFF_INPUT: diff, **{k: w.astype(_WEIGHT_DTYPES[k]) for k, w in weights.items()}}
        y = _forward(args)
    with _jax.named_scope("loss_head"):
        err = _jnp.square(y.astype(_jnp.float32) - loss_target)
        return 0.5 * _jnp.sum(_jnp.mean(err, axis=-1)) if err.ndim else 0.5 * err


def _adamw(w, g, m, v):
    m = ADAM_B1 * m + (1.0 - ADAM_B1) * g
    v = ADAM_B2 * v + (1.0 - ADAM_B2) * _jnp.square(g)
    m_hat = m / (1.0 - ADAM_B1 ** ADAM_STEP)
    v_hat = v / (1.0 - ADAM_B2 ** ADAM_STEP)
    delta = -ADAM_LR * (m_hat / (_jnp.sqrt(v_hat) + ADAM_EPS) + ADAM_WD * w)
    return delta, m, v


def reference(x, positions, attn_norm_w, w_in, b_gates, mlstm_norm_w, q_a_norm_w, w_q_b, kv_a_norm_w, w_kv_b, w_out, ffn_norm_w, w_gate, w_up, w_down, final_norm_w, loss_target, m_attn_norm_w, m_w_in, m_b_gates, m_mlstm_norm_w, m_q_a_norm_w, m_w_q_b, m_kv_a_norm_w, m_w_kv_b, m_w_out, m_ffn_norm_w, m_w_gate, m_w_up, m_w_down, m_final_norm_w, v_attn_norm_w, v_w_in, v_b_gates, v_mlstm_norm_w, v_q_a_norm_w, v_w_q_b, v_kv_a_norm_w, v_w_kv_b, v_w_out, v_ffn_norm_w, v_w_gate, v_w_up, v_w_down, v_final_norm_w):
    given = dict(x=x, positions=positions, attn_norm_w=attn_norm_w, w_in=w_in, b_gates=b_gates, mlstm_norm_w=mlstm_norm_w, q_a_norm_w=q_a_norm_w, w_q_b=w_q_b, kv_a_norm_w=kv_a_norm_w, w_kv_b=w_kv_b, w_out=w_out, ffn_norm_w=ffn_norm_w, w_gate=w_gate, w_up=w_up, w_down=w_down, final_norm_w=final_norm_w, loss_target=loss_target, m_attn_norm_w=m_attn_norm_w, m_w_in=m_w_in, m_b_gates=m_b_gates, m_mlstm_norm_w=m_mlstm_norm_w, m_q_a_norm_w=m_q_a_norm_w, m_w_q_b=m_w_q_b, m_kv_a_norm_w=m_kv_a_norm_w, m_w_kv_b=m_w_kv_b, m_w_out=m_w_out, m_ffn_norm_w=m_ffn_norm_w, m_w_gate=m_w_gate, m_w_up=m_w_up, m_w_down=m_w_down, m_final_norm_w=m_final_norm_w, v_attn_norm_w=v_attn_norm_w, v_w_in=v_w_in, v_b_gates=v_b_gates, v_mlstm_norm_w=v_mlstm_norm_w, v_q_a_norm_w=v_q_a_norm_w, v_w_q_b=v_w_q_b, v_kv_a_norm_w=v_kv_a_norm_w, v_w_kv_b=v_w_kv_b, v_w_out=v_w_out, v_ffn_norm_w=v_ffn_norm_w, v_w_gate=v_w_gate, v_w_up=v_w_up, v_w_down=v_w_down, v_final_norm_w=v_final_norm_w)
    weights = {n: given[n] for n in TWIN_WEIGHTS}
    shared = {n: given[n] for n in SHARED_INPUTS}
    per_example = {n: given[n] for n in ['x', 'positions']}
    grad_fn = _jax.value_and_grad(_loss, argnums=(0, 1))

    def one_microbatch(ex, loss_target):
        ex = dict(ex)
        diff = ex.pop(TWIN_DIFF_INPUT)
        return grad_fn(weights, diff, {**shared, **ex}, loss_target)

    if N_MICROBATCH == 1:
        loss, (grad_w, grad_x) = one_microbatch(per_example, given["loss_target"])
    else:
        def body(carry, xs):
            loss_sum, grad_sum = carry
            l_k, (gw_k, gx_k) = one_microbatch(xs[0], xs[1])
            with _jax.named_scope("update"):
                return (loss_sum + l_k, _jax.tree.map(_jnp.add, grad_sum, gw_k)), gx_k

        init = (_jnp.zeros((), _jnp.float32), _jax.tree.map(_jnp.zeros_like, weights))
        (loss, grad_w), grad_x = _jax.lax.scan(body, init, (per_example, given["loss_target"]))
    with _jax.named_scope("update"):
        delta_w, new_m, new_v = {}, {}, {}
        for n in TWIN_WEIGHTS:
            delta_w[n], new_m[n], new_v[n] = _adamw(weights[n], grad_w[n], given["m_" + n], given["v_" + n])
    return (loss, grad_x, *[grad_w[n] for n in TWIN_WEIGHTS], *[delta_w[n] for n in TWIN_WEIGHTS],
            *[new_m[n] for n in TWIN_WEIGHTS], *[new_v[n] for n in TWIN_WEIGHTS])
```

```python
import functools
import math

import jax
import jax.numpy as jnp
from jax import lax
from jax.experimental import pallas as pl
from jax.experimental.pallas import tpu as pltpu

F32 = jnp.float32
BF16 = jnp.bfloat16

D_MODEL = 1024
M_HEADS = 4
M_HEAD_DIM = 128
M_WIDTH = M_HEADS * M_HEAD_DIM
M_CHUNK = 64
A_HEADS = 8
A_NOPE = 64
A_ROPE = 32
A_V = 64
A_WIDTH = A_HEADS * A_V
A_QK_PAD = 128
Q_RANK = 384
KV_RANK = 256
ROPE_THETA = 10000.0
D_FF = 2816
D_IN = 2728
EPS = 1e-6
ATTN_SCALE = (A_NOPE + A_ROPE) ** -0.5
M_SCALE = M_HEAD_DIM ** -0.5

ADAM_LR = 0.001
ADAM_B1 = 0.9
ADAM_B2 = 0.999
ADAM_EPS = 1e-08
ADAM_WD = 0.01
ADAM_STEP = 10

VMEM_LIMIT_BYTES = 56 * 1024 * 1024
LANES = 128
NEG_INF = float("-inf")


def _cparams(*sem):
    return pltpu.CompilerParams(dimension_semantics=sem if sem else None, vmem_limit_bytes=VMEM_LIMIT_BYTES)


_DIMS = {"nn": (((1,), (0,)), ((), ())), "nt": (((1,), (1,)), ((), ())), "tn": (((0,), (0,)), ((), ()))}


def _matmul(a, b, mode, out_dtype, name, tm=512, tn=512, residual=None):
    if mode == "nn":
        (m, k), (k2, n) = a.shape, b.shape
    elif mode == "nt":
        (m, k), (n, k2) = a.shape, b.shape
    else:
        (k, m), (k2, n) = a.shape, b.shape
    assert k == k2, (a.shape, b.shape, mode)
    tm, tn = min(tm, m), min(tn, n)
    assert m % tm == 0 and n % tn == 0, (m, n, tm, tn)
    a_spec = pl.BlockSpec((k, tm), lambda i, j: (0, i)) if mode == "tn" else pl.BlockSpec((tm, k), lambda i, j: (i, 0))
    b_spec = pl.BlockSpec((tn, k), lambda i, j: (j, 0)) if mode == "nt" else pl.BlockSpec((k, tn), lambda i, j: (0, j))
    o_spec = pl.BlockSpec((tm, tn), lambda i, j: (i, j))
    dims = _DIMS[mode]

    def body(a_ref, b_ref, *rest):
        o_ref = rest[-1]
        acc = lax.dot_general(a_ref[...].astype(BF16), b_ref[...].astype(BF16), dims, preferred_element_type=F32)
        if residual is not None:
            acc = acc + rest[0][...].astype(F32)
        o_ref[...] = acc.astype(o_ref.dtype)

    ins = [a, b] + ([residual] if residual is not None else [])
    in_specs = [a_spec, b_spec] + ([o_spec] if residual is not None else [])
    return pl.pallas_call(
        body, name=name, grid=(m // tm, n // tn), in_specs=in_specs, out_specs=o_spec,
        out_shape=jax.ShapeDtypeStruct((m, n), out_dtype), compiler_params=_cparams("parallel", "parallel"),
    )(*ins)


def _rmsnorm_fwd(x, w, name, tm=512):
    t, d = x.shape

    def body(x_ref, w_ref, o_ref):
        xf = x_ref[...].astype(F32)
        r = lax.rsqrt(jnp.mean(xf * xf, axis=-1, keepdims=True) + EPS)
        o_ref[...] = (xf * r * w_ref[...]).astype(o_ref.dtype)

    return pl.pallas_call(
        body, name=name, grid=(t // tm,),
        in_specs=[pl.BlockSpec((tm, d), lambda i: (i, 0)), pl.BlockSpec((1, d), lambda i: (0, 0))],
        out_specs=pl.BlockSpec((tm, d), lambda i: (i, 0)),
        out_shape=jax.ShapeDtypeStruct((t, d), BF16), compiler_params=_cparams("parallel"),
    )(x, w)


def _rmsnorm_bwd(dy, x, w, name, out_dtype, residual=None, tm=512):
    t, d = x.shape

    def body(dy_ref, x_ref, w_ref, *rest):
        dx_ref, dw_ref = rest[-2], rest[-1]
        xf = x_ref[...].astype(F32)
        r = lax.rsqrt(jnp.mean(xf * xf, axis=-1, keepdims=True) + EPS)
        xh = xf * r
        dyf = dy_ref[...].astype(F32)
        dyh = dyf * w_ref[...]
        dx = r * (dyh - xh * jnp.mean(dyh * xh, axis=-1, keepdims=True))
        if residual is not None:
            dx = dx + rest[0][...].astype(F32)
        dx_ref[...] = dx.astype(dx_ref.dtype)
        part = jnp.sum(dyf * xh, axis=0, keepdims=True)

        @pl.when(pl.program_id(0) == 0)
        def _():
            dw_ref[...] = part

        @pl.when(pl.program_id(0) > 0)
        def _():
            dw_ref[...] += part

    row = pl.BlockSpec((tm, d), lambda i: (i, 0))
    vec = pl.BlockSpec((1, d), lambda i: (0, 0))
    ins = [dy, x, w] + ([residual] if residual is not None else [])
    return pl.pallas_call(
        body, name=name, grid=(t // tm,), in_specs=[row, row, vec] + ([row] if residual is not None else []),
        out_specs=[row, vec],
        out_shape=[jax.ShapeDtypeStruct((t, d), out_dtype), jax.ShapeDtypeStruct((1, d), F32)],
        compiler_params=_cparams("arbitrary"),
    )(*ins)


def _final_loss(h, target, w, name, tm=512):
    t, d = h.shape

    def body(h_ref, t_ref, w_ref, dh_ref, loss_ref, dw_ref):
        xf = h_ref[...]
        r = lax.rsqrt(jnp.mean(xf * xf, axis=-1, keepdims=True) + EPS)
        xh = xf * r
        err = xh * w_ref[...] - t_ref[...]
        part_loss = 0.5 * jnp.sum(jnp.sum(err * err, axis=-1, keepdims=True), axis=0, keepdims=True) * (1.0 / d)
        dy = err * (1.0 / d)
        dyh = dy * w_ref[...]
        dh_ref[...] = r * (dyh - xh * jnp.mean(dyh * xh, axis=-1, keepdims=True))
        part_dw = jnp.sum(dy * xh, axis=0, keepdims=True)
        part_loss = jnp.broadcast_to(part_loss, (1, LANES))

        @pl.when(pl.program_id(0) == 0)
        def _():
            dw_ref[...] = part_dw
            loss_ref[...] = part_loss

        @pl.when(pl.program_id(0) > 0)
        def _():
            dw_ref[...] += part_dw
            loss_ref[...] += part_loss

    row = pl.BlockSpec((tm, d), lambda i: (i, 0))
    vec = pl.BlockSpec((1, d), lambda i: (0, 0))
    return pl.pallas_call(
        body, name=name, grid=(t // tm,), in_specs=[row, row, vec],
        out_specs=[row, pl.BlockSpec((1, LANES), lambda i: (0, 0)), vec],
        out_shape=[jax.ShapeDtypeStruct((t, d), F32), jax.ShapeDtypeStruct((1, LANES), F32),
                   jax.ShapeDtypeStruct((1, d), F32)],
        compiler_params=_cparams("arbitrary"),
    )(h, target, w)


def _swiglu_fwd(g, u, name, tm=512, tn=1408):
    t, n = g.shape

    def body(g_ref, u_ref, o_ref):
        gf = g_ref[...].astype(F32)
        o_ref[...] = (gf * jax.nn.sigmoid(gf) * u_ref[...].astype(F32)).astype(o_ref.dtype)

    blk = pl.BlockSpec((tm, tn), lambda i, j: (i, j))
    return pl.pallas_call(
        body, name=name, grid=(t // tm, n // tn), in_specs=[blk, blk], out_specs=blk,
        out_shape=jax.ShapeDtypeStruct((t, n), BF16), compiler_params=_cparams("parallel", "parallel"),
    )(g, u)


def _swiglu_bwd(g, u, dact, name, tm=512, tn=1408):
    t, n = g.shape

    def body(g_ref, u_ref, d_ref, dg_ref, du_ref):
        gf = g_ref[...].astype(F32)
        uf = u_ref[...].astype(F32)
        df = d_ref[...].astype(F32)
        s = jax.nn.sigmoid(gf)
        dg_ref[...] = (df * uf * s * (1.0 + gf * (1.0 - s))).astype(dg_ref.dtype)
        du_ref[...] = (df * gf * s).astype(du_ref.dtype)

    blk = pl.BlockSpec((tm, tn), lambda i, j: (i, j))
    out = jax.ShapeDtypeStruct((t, n), BF16)
    return pl.pallas_call(
        body, name=name, grid=(t // tm, n // tn), in_specs=[blk, blk, blk], out_specs=[blk, blk],
        out_shape=[out, out], compiler_params=_cparams("parallel", "parallel"),
    )(g, u, dact)


M_BLOCK = 512
M_CHUNKS_PER_BLOCK = M_BLOCK // M_CHUNK


def _mlstm_chunk_fwd(q, k, v, i_col, f_col, c_prev, n_prev, m_prev):
    L = M_CHUNK
    tt = lax.broadcasted_iota(jnp.int32, (L, L), 0)
    ss = lax.broadcasted_iota(jnp.int32, (L, L), 1)
    eye = tt == ss
    causal = ss <= tt

    def to_row(col):
        return jnp.sum(jnp.where(eye, col, 0.0), axis=0, keepdims=True)

    lf_col = jnp.minimum(f_col, 0.0) - jnp.log(1.0 + jnp.exp(-jnp.abs(f_col)))
    lf_row = to_row(lf_col)
    li_row = to_row(i_col)
    b_col = jnp.sum(jnp.where(causal, lf_row, 0.0), axis=1, keepdims=True)
    b_row = jnp.sum(jnp.where(tt <= ss, lf_col, 0.0), axis=0, keepdims=True)
    g = jnp.sum(lf_col, axis=0, keepdims=True)
    a_row = g - b_row + li_row
    a_col = g - b_col + i_col
    m_loc = jnp.max(a_row, axis=1, keepdims=True)
    d_log = jnp.where(causal, b_col - b_row + li_row, NEG_INF)
    inter = b_col + m_prev
    m_t = jnp.maximum(jnp.max(d_log, axis=1, keepdims=True), inter)
    dmat = jnp.exp(d_log - m_t)
    amat = lax.dot_general(q, k, _DIMS["nt"], preferred_element_type=F32) * M_SCALE
    p = dmat * amat
    sig = jnp.exp(inter - m_t)
    c_prev_b = c_prev.astype(BF16)
    qc = jnp.dot(q, c_prev_b, preferred_element_type=F32) * M_SCALE
    num = jnp.dot(p.astype(BF16), v, preferred_element_type=F32) + sig * qc
    qn = jnp.sum(q.astype(F32) * n_prev, axis=1, keepdims=True) * M_SCALE
    den_raw = jnp.sum(p, axis=1, keepdims=True) + sig * qn
    floor = jnp.exp(-m_t)
    den = jnp.maximum(jnp.abs(den_raw), floor)
    h = num / den
    m_new = jnp.maximum(g + m_prev, m_loc)
    w_col = jnp.exp(a_col - m_new)
    alpha = jnp.exp(g + m_prev - m_new)
    return dict(eye=eye, causal=causal, tt=tt, ss=ss, f_col=f_col, dmat=dmat, amat=amat, p=p, sig=sig, qc=qc,
                qn=qn, num=num, den_raw=den_raw, floor=floor, den=den, h=h, m_new=m_new, w_col=w_col, alpha=alpha,
                c_prev_b=c_prev_b)


def _head_gates(gc, hd):
    return gc[:, hd:hd + 1], gc[:, M_HEADS + hd:M_HEADS + hd + 1]


def _mlstm_fwd(m4, gates, bias, norm_w, name):
    t = m4.shape[0]
    nblk = t // M_BLOCK
    nc = t // M_CHUNK
    L, dh = M_CHUNK, M_HEAD_DIM

    def body(m4_ref, g_ref, b_ref, w_ref, hm_ref, cp_ref, nm_ref, c_s, n_s, m_s):
        @pl.when(pl.program_id(0) == 0)
        def _():
            c_s[...] = jnp.zeros_like(c_s)
            n_s[...] = jnp.zeros_like(n_s)
            m_s[...] = jnp.zeros_like(m_s)

        row8 = lax.broadcasted_iota(jnp.int32, (8, dh), 0)

        def chunk(c, carry):
            rows = pl.ds(pl.multiple_of(c * L, L), L)
            gc = g_ref[rows, :] + b_ref[...]
            for hd in range(M_HEADS):
                q = m4_ref[rows, hd * dh:(hd + 1) * dh]
                k = m4_ref[rows, M_WIDTH + hd * dh:M_WIDTH + (hd + 1) * dh]
                v = m4_ref[rows, 2 * M_WIDTH + hd * dh:2 * M_WIDTH + (hd + 1) * dh]
                o = m4_ref[rows, 3 * M_WIDTH + hd * dh:3 * M_WIDTH + (hd + 1) * dh].astype(F32)
                i_col, f_col = _head_gates(gc, hd)
                c_prev = c_s[hd]
                n_prev = n_s[hd, 0:1, :]
                m_prev = m_s[hd, 0:1, 0:1]
                r = _mlstm_chunk_fwd(q, k, v, i_col, f_col, c_prev, n_prev, m_prev)
                h = r["h"]
                hn = h * lax.rsqrt(jnp.mean(h * h, axis=-1, keepdims=True) + EPS) * w_ref[hd:hd + 1, :]
                hm_ref[rows, hd * dh:(hd + 1) * dh] = (hn * jax.nn.sigmoid(o)).astype(hm_ref.dtype)
                cp_ref[hd, c] = r["c_prev_b"]
                nm_ref[hd, c] = jnp.where(row8 == 0, n_prev, jnp.where(row8 == 1, m_prev, 0.0))
                kw = k.astype(F32) * r["w_col"]
                c_s[hd] = r["alpha"] * c_prev + lax.dot_general(kw.astype(BF16), v, _DIMS["tn"],
                                                                preferred_element_type=F32)
                n_s[hd, 0:1, :] = r["alpha"] * n_prev + jnp.sum(kw, axis=0, keepdims=True)
                m_s[hd] = jnp.broadcast_to(r["m_new"], (8, dh))
            return carry

        lax.fori_loop(0, M_CHUNKS_PER_BLOCK, chunk, 0)

    return pl.pallas_call(
        body, name=name, grid=(nblk,),
        in_specs=[pl.BlockSpec((M_BLOCK, 4 * M_WIDTH), lambda i: (i, 0)),
                  pl.BlockSpec((M_BLOCK, LANES), lambda i: (i, 0)),
                  pl.BlockSpec((1, LANES), lambda i: (0, 0)),
                  pl.BlockSpec((M_HEADS, dh), lambda i: (0, 0))],
        out_specs=[pl.BlockSpec((M_BLOCK, M_WIDTH), lambda i: (i, 0)),
                   pl.BlockSpec((M_HEADS, M_CHUNKS_PER_BLOCK, dh, dh), lambda i: (0, i, 0, 0)),
                   pl.BlockSpec((M_HEADS, M_CHUNKS_PER_BLOCK, 8, dh), lambda i: (0, i, 0, 0))],
        out_shape=[jax.ShapeDtypeStruct((t, M_WIDTH), BF16),
                   jax.ShapeDtypeStruct((M_HEADS, nc, dh, dh), BF16),
                   jax.ShapeDtypeStruct((M_HEADS, nc, 8, dh), F32)],
        scratch_shapes=[pltpu.VMEM((M_HEADS, dh, dh), F32), pltpu.VMEM((M_HEADS, 8, dh), F32),
                        pltpu.VMEM((M_HEADS, 8, dh), F32)],
        compiler_params=_cparams("arbitrary"),
    )(m4, gates, bias, norm_w)


def _mlstm_bwd(m4, gates, bias, norm_w, c_prev_all, nm_all, dhm, name):
    t = m4.shape[0]
    nblk = t // M_BLOCK
    L, dh = M_CHUNK, M_HEAD_DIM

    def body(m4_ref, g_ref, b_ref, w_ref, cp_ref, nm_ref, dhm_ref, dm4_ref, dg_ref, small_ref, dc_s, dn_s):
        @pl.when(pl.program_id(0) == 0)
        def _():
            dc_s[...] = jnp.zeros_like(dc_s)
            dn_s[...] = jnp.zeros_like(dn_s)
            small_ref[...] = jnp.zeros_like(small_ref)

        lane = lax.broadcasted_iota(jnp.int32, (L, LANES), 1)
        row8 = lax.broadcasted_iota(jnp.int32, (8, LANES), 0)

        def chunk(ci, carry):
            c = M_CHUNKS_PER_BLOCK - 1 - ci
            rows = pl.ds(pl.multiple_of(c * L, L), L)
            gc = g_ref[rows, :] + b_ref[...]
            dg_tile = jnp.zeros((L, LANES), F32)
            small = jnp.zeros((8, LANES), F32)
            for hd in range(M_HEADS):
                q = m4_ref[rows, hd * dh:(hd + 1) * dh]
                k = m4_ref[rows, M_WIDTH + hd * dh:M_WIDTH + (hd + 1) * dh]
                v = m4_ref[rows, 2 * M_WIDTH + hd * dh:2 * M_WIDTH + (hd + 1) * dh]
                o = m4_ref[rows, 3 * M_WIDTH + hd * dh:3 * M_WIDTH + (hd + 1) * dh].astype(F32)
                i_col, f_col = _head_gates(gc, hd)
                c_prev = cp_ref[hd, c].astype(F32)
                nm = nm_ref[hd, c]
                n_prev = nm[0:1, :]
                m_prev = nm[1:2, 0:1]
                r = _mlstm_chunk_fwd(q, k, v, i_col, f_col, c_prev, n_prev, m_prev)
                eye, tt, ss = r["eye"], r["tt"], r["ss"]
                h, den, sig, p = r["h"], r["den"], r["sig"], r["p"]
                w_col, alpha, c_prev_b = r["w_col"], r["alpha"], r["c_prev_b"]

                def to_row(col):
                    return jnp.sum(jnp.where(eye, col, 0.0), axis=0, keepdims=True)

                def to_col(row):
                    return jnp.sum(jnp.where(eye, row, 0.0), axis=1, keepdims=True)

                w_h = w_ref[hd:hd + 1, :]
                rn = lax.rsqrt(jnp.mean(h * h, axis=-1, keepdims=True) + EPS)
                hh = h * rn
                og = jax.nn.sigmoid(o)
                dhm_c = dhm_ref[rows, hd * dh:(hd + 1) * dh].astype(F32)
                dhn = dhm_c * og
                d_o = dhm_c * hh * w_h * og * (1.0 - og)
                small = small + jnp.where(row8 == hd, jnp.sum(dhn * hh, axis=0, keepdims=True), 0.0)
                dhh = dhn * w_h
                dh_ = rn * (dhh - hh * jnp.mean(dhh * hh, axis=-1, keepdims=True))

                dnum = dh_ / den
                dden = -jnp.sum(dh_ * h, axis=-1, keepdims=True) / den
                dden_raw = jnp.where(jnp.abs(r["den_raw"]) >= r["floor"], dden * jnp.sign(r["den_raw"]), 0.0)
                dnum_b = dnum.astype(BF16)
                dp = lax.dot_general(dnum_b, v, _DIMS["nt"], preferred_element_type=F32) + dden_raw
                p_b = p.astype(BF16)
                dc = dc_s[hd]
                dn = dn_s[hd, 0:1, :]
                dc_b = dc.astype(BF16)
                g2 = sig * dnum
                g2_b = g2.astype(BF16)
                sd = sig * dden_raw
                da_mat = (dp * r["dmat"]).astype(BF16)
                dqs = (lax.dot_general(g2_b, c_prev_b, _DIMS["nt"], preferred_element_type=F32)
                       + sd * n_prev + jnp.dot(da_mat, k, preferred_element_type=F32))
                r_mat = lax.dot_general(v, dc_b, _DIMS["nt"], preferred_element_type=F32) + dn
                kf = k.astype(F32)
                d_k = (lax.dot_general(da_mat, q, _DIMS["tn"], preferred_element_type=F32) * M_SCALE + w_col * r_mat)
                d_v = (lax.dot_general(p_b, dnum_b, _DIMS["tn"], preferred_element_type=F32)
                       + w_col * jnp.dot(k, dc_b, preferred_element_type=F32))
                d_omega = jnp.sum(kf * r_mat, axis=-1, keepdims=True)
                da_col = d_omega * w_col
                dsig = jnp.sum(dnum * r["qc"], axis=-1, keepdims=True) + dden_raw * r["qn"]
                pp = dp * p
                r1 = jnp.sum(pp, axis=1, keepdims=True)
                c1_col = to_col(jnp.sum(pp, axis=0, keepdims=True))
                d_alpha = (jnp.sum(jnp.sum(dc * c_prev, axis=1, keepdims=True), axis=0, keepdims=True)
                           + jnp.sum(dn * n_prev, axis=1, keepdims=True))
                dgl = jnp.sum(da_col, axis=0, keepdims=True) + d_alpha * alpha
                db_col = r1 + dsig * sig - c1_col - da_col
                dli_col = c1_col + da_col
                db_row = to_row(db_col)
                dlf_col = jnp.sum(jnp.where(ss >= tt, db_row, 0.0), axis=1, keepdims=True) + dgl
                df_col = dlf_col * jax.nn.sigmoid(-r["f_col"])

                dc_s[hd] = alpha * dc + lax.dot_general(q, g2_b, _DIMS["tn"], preferred_element_type=F32) * M_SCALE
                dn_s[hd, 0:1, :] = alpha * dn + jnp.sum(sd * q.astype(F32), axis=0, keepdims=True) * M_SCALE

                dm4_ref[rows, hd * dh:(hd + 1) * dh] = (dqs * M_SCALE).astype(dm4_ref.dtype)
                dm4_ref[rows, M_WIDTH + hd * dh:M_WIDTH + (hd + 1) * dh] = d_k.astype(dm4_ref.dtype)
                dm4_ref[rows, 2 * M_WIDTH + hd * dh:2 * M_WIDTH + (hd + 1) * dh] = d_v.astype(dm4_ref.dtype)
                dm4_ref[rows, 3 * M_WIDTH + hd * dh:3 * M_WIDTH + (hd + 1) * dh] = d_o.astype(dm4_ref.dtype)
                dg_tile = dg_tile + jnp.where(lane == hd, dli_col, 0.0) + jnp.where(lane == M_HEADS + hd, df_col, 0.0)
            dg_ref[rows, :] = dg_tile
            small = small + jnp.where(row8 == M_HEADS, jnp.sum(dg_tile, axis=0, keepdims=True), 0.0)
            small_ref[...] += small
            return carry

        lax.fori_loop(0, M_CHUNKS_PER_BLOCK, chunk, 0)

    rev = lambda i: nblk - 1 - i
    return pl.pallas_call(
        body, name=name, grid=(nblk,),
        in_specs=[pl.BlockSpec((M_BLOCK, 4 * M_WIDTH), lambda i: (rev(i), 0)),
                  pl.BlockSpec((M_BLOCK, LANES), lambda i: (rev(i), 0)),
                  pl.BlockSpec((1, LANES), lambda i: (0, 0)),
                  pl.BlockSpec((M_HEADS, dh), lambda i: (0, 0)),
                  pl.BlockSpec((M_HEADS, M_CHUNKS_PER_BLOCK, dh, dh), lambda i: (0, rev(i), 0, 0)),
                  pl.BlockSpec((M_HEADS, M_CHUNKS_PER_BLOCK, 8, dh), lambda i: (0, rev(i), 0, 0)),
                  pl.BlockSpec((M_BLOCK, M_WIDTH), lambda i: (rev(i), 0))],
        out_specs=[pl.BlockSpec((M_BLOCK, 4 * M_WIDTH), lambda i: (rev(i), 0)),
                   pl.BlockSpec((M_BLOCK, LANES), lambda i: (rev(i), 0)),
                   pl.BlockSpec((8, LANES), lambda i: (0, 0))],
        out_shape=[jax.ShapeDtypeStruct((t, 4 * M_WIDTH), BF16), jax.ShapeDtypeStruct((t, LANES), F32),
                   jax.ShapeDtypeStruct((8, LANES), F32)],
        scratch_shapes=[pltpu.VMEM((M_HEADS, dh, dh), F32), pltpu.VMEM((M_HEADS, 8, dh), F32)],
        compiler_params=_cparams("arbitrary"),
    )(m4, gates, bias, norm_w, c_prev_all, nm_all, dhm)


def _rope_tables(pos_col, width):
    lane = lax.broadcasted_iota(jnp.int32, (1, width), 1) % A_QK_PAD
    half = A_ROPE // 2
    first = (lane >= A_NOPE) & (lane < A_NOPE + half)
    second = (lane >= A_NOPE + half) & (lane < A_NOPE + A_ROPE)
    idx = jnp.where(first, lane - A_NOPE, lane - A_NOPE - half).astype(F32)
    inv_freq = jnp.exp(idx * (-math.log(ROPE_THETA) / half))
    ang = pos_col.astype(F32) * inv_freq
    cos, sin = jnp.cos(ang), jnp.sin(ang)
    rot = first | second
    return jnp.where(rot, cos, 1.0), jnp.where(first, -sin, 0.0), jnp.where(second, sin, 0.0)


def _rope_apply(vv, cosf, s1, s2):
    half = A_ROPE // 2
    w = vv.shape[-1]
    return vv * cosf + pltpu.roll(vv, w - half, 1) * s1 + pltpu.roll(vv, half, 1) * s2


def _rope_apply_t(dd, cosf, s1, s2):
    half = A_ROPE // 2
    w = dd.shape[-1]
    return dd * cosf + pltpu.roll(dd * s1, half, 1) + pltpu.roll(dd * s2, w - half, 1)


def _rope_fwd(qpre, kpre, gates, pos, name, tm=512):
    t, w = qpre.shape

    def body(q_ref, k_ref, g_ref, p_ref, qo_ref, ko_ref):
        pos_col = p_ref[...]
        cosf, s1, s2 = _rope_tables(pos_col, w)
        qo_ref[...] = (_rope_apply(q_ref[...].astype(F32), cosf, s1, s2) * ATTN_SCALE).astype(qo_ref.dtype)
        lane = lax.broadcasted_iota(jnp.int32, (1, LANES), 1)
        kr = jnp.where((lane >= A_NOPE) & (lane < A_NOPE + A_ROPE), g_ref[...], 0.0)
        kr = _rope_apply(kr, cosf[:, :LANES], s1[:, :LANES], s2[:, :LANES])
        ko_ref[...] = (k_ref[...].astype(F32) + jnp.tile(kr, (1, A_HEADS))).astype(ko_ref.dtype)

    row = pl.BlockSpec((tm, w), lambda i: (i, 0))
    out = jax.ShapeDtypeStruct((t, w), BF16)
    return pl.pallas_call(
        body, name=name, grid=(t // tm,),
        in_specs=[row, row, pl.BlockSpec((tm, LANES), lambda i: (i, 0)), pl.BlockSpec((tm, 1), lambda i: (i, 0))],
        out_specs=[row, row], out_shape=[out, out], compiler_params=_cparams("parallel"),
    )(qpre, kpre, gates, pos)


def _rope_bwd(dq, dk, dgates, pos, name, tm=512):
    t, w = dq.shape

    def body(dq_ref, dk_ref, dg_ref, p_ref, dqo_ref, dgo_ref):
        cosf, s1, s2 = _rope_tables(p_ref[...], w)
        dqo_ref[...] = (_rope_apply_t(dq_ref[...], cosf, s1, s2) * ATTN_SCALE).astype(dqo_ref.dtype)
        dkf = dk_ref[...].astype(F32)
        acc = dkf[:, :LANES]
        for hd in range(1, A_HEADS):
            acc = acc + dkf[:, hd * LANES:(hd + 1) * LANES]
        lane = lax.broadcasted_iota(jnp.int32, (1, LANES), 1)
        dkr = _rope_apply_t(acc, cosf[:, :LANES], s1[:, :LANES], s2[:, :LANES])
        dkr = jnp.where((lane >= A_NOPE) & (lane < A_NOPE + A_ROPE), dkr, 0.0)
        dgo_ref[...] = (dg_ref[...] + dkr).astype(dgo_ref.dtype)

    row = pl.BlockSpec((tm, w), lambda i: (i, 0))
    nar = pl.BlockSpec((tm, LANES), lambda i: (i, 0))
    return pl.pallas_call(
        body, name=name, grid=(t // tm,),
        in_specs=[row, row, nar, pl.BlockSpec((tm, 1), lambda i: (i, 0))],
        out_specs=[row, nar],
        out_shape=[jax.ShapeDtypeStruct((t, w), BF16), jax.ShapeDtypeStruct((t, LANES), BF16)],
        compiler_params=_cparams("parallel"),
    )(dq, dk, dgates, pos)


A_TQ = 256
A_TK = 256


def _flash_fwd(q, k, v, name):
    t = q.shape[0]
    tq, tk = A_TQ, A_TK
    npairs = A_HEADS // 2

    def body(q_ref, k_ref, v_ref, o_ref, lse_ref):
        i = pl.program_id(1)
        lane = lax.broadcasted_iota(jnp.int32, (tq, LANES), 1)
        qpos = i * tq + lax.broadcasted_iota(jnp.int32, (tq, tk), 0)
        kofs = lax.broadcasted_iota(jnp.int32, (tq, tk), 1)
        outs, lses = [], []
        for hh in range(2):
            qh = q_ref[:, hh * LANES:(hh + 1) * LANES]

            def step(kb, carry, masked):
                m, l, acc = carry
                rows = pl.ds(pl.multiple_of(kb * tk, tk), tk)
                kh = k_ref[rows, hh * LANES:(hh + 1) * LANES]
                s = lax.dot_general(qh, kh, _DIMS["nt"], preferred_element_type=F32)
                if masked:
                    s = jnp.where(kb * tk + kofs <= qpos, s, NEG_INF)
                m_new = jnp.maximum(m, jnp.max(s, axis=1, keepdims=True))
                p = jnp.exp(s - m_new)
                a = jnp.exp(m - m_new)
                l = a * l + jnp.sum(p, axis=1, keepdims=True)
                acc = a * acc + jnp.dot(p.astype(BF16), v_ref[rows, :], preferred_element_type=F32)
                return m_new, l, acc

            carry = (jnp.full((tq, 1), NEG_INF, F32), jnp.zeros((tq, 1), F32), jnp.zeros((tq, LANES), F32))
            n_full = (i * tq) // tk
            carry = lax.fori_loop(0, n_full, functools.partial(step, masked=False), carry)
            for d in range(tq // tk):
                carry = step(n_full + d, carry, True)
            m, l, acc = carry
            outs.append(acc / l)
            lses.append(m + jnp.log(l))
        o_ref[...] = jnp.where(lane < A_V, outs[0], outs[1]).astype(o_ref.dtype)
        lse_tile = jnp.where(lane == 0, lses[0], jnp.where(lane == 1, lses[1], 0.0))
        lse_ref[0, 0] = jnp.transpose(lse_tile)[0:8, :]

    return pl.pallas_call(
        body, name=name, grid=(npairs, t // tq),
        in_specs=[pl.BlockSpec((tq, 2 * LANES), lambda j, i: (i, j)),
                  pl.BlockSpec((t, 2 * LANES), lambda j, i: (0, j)),
                  pl.BlockSpec((t, LANES), lambda j, i: (0, j))],
        out_specs=[pl.BlockSpec((tq, LANES), lambda j, i: (i, j)),
                   pl.BlockSpec((1, 1, 8, tq), lambda j, i: (j, i, 0, 0))],
        out_shape=[jax.ShapeDtypeStruct((t, A_WIDTH), BF16), jax.ShapeDtypeStruct((npairs, t // tq, 8, tq), F32)],
        compiler_params=_cparams("parallel", "parallel"),
    )(q, k, v)


def _flash_bwd(q, k, v, o, do, lse, name):
    t = q.shape[0]
    tq, tk = A_TQ, A_TK
    assert tq == tk
    npairs = A_HEADS // 2
    nq = t // tq

    def body(q_ref, k_ref, v_ref, o_ref, do_ref, lse_ref, dq_ref, dk_ref, dv_ref, delta_s):
        kb = pl.program_id(1)
        lane = lax.broadcasted_iota(jnp.int32, (tq, LANES), 1)

        @pl.when(kb == 0)
        def _():
            dq_ref[...] = jnp.zeros_like(dq_ref)

            def fill(qi, carry):
                rows = pl.ds(pl.multiple_of(qi * tq, tq), tq)
                prod_t = jnp.transpose(do_ref[rows, :].astype(F32) * o_ref[rows, :].astype(F32))
                d0 = jnp.sum(prod_t[:A_V], axis=0, keepdims=True)
                d1 = jnp.sum(prod_t[A_V:], axis=0, keepdims=True)
                row8 = lax.broadcasted_iota(jnp.int32, (8, tq), 0)
                delta_s[qi] = jnp.where(row8 == 0, d0, jnp.where(row8 == 1, d1, 0.0))
                return carry

            lax.fori_loop(0, nq, fill, 0)

        kpos = kb * tk + lax.broadcasted_iota(jnp.int32, (tk, tq), 0)
        qofs = lax.broadcasted_iota(jnp.int32, (tk, tq), 1)
        vb = v_ref[...]
        dv_acc = jnp.zeros((tk, LANES), F32)
        dk_out = []
        for hh in range(2):
            kh = k_ref[:, hh * LANES:(hh + 1) * LANES]
            head_lanes = (lane < A_V) if hh == 0 else (lane >= A_V)

            def step(qi, carry, masked):
                dk_acc, dv_a = carry
                rows = pl.ds(pl.multiple_of(qi * tq, tq), tq)
                qh = q_ref[rows, hh * LANES:(hh + 1) * LANES]
                s_t = lax.dot_general(kh, qh, _DIMS["nt"], preferred_element_type=F32)
                p_t = jnp.exp(s_t - lse_ref[0, qi, hh:hh + 1, :])
                if masked:
                    p_t = jnp.where(qi * tq + qofs >= kpos, p_t, 0.0)
                dom = jnp.where(head_lanes, do_ref[rows, :], jnp.zeros((), BF16))
                dp_t = lax.dot_general(vb, dom, _DIMS["nt"], preferred_element_type=F32)
                ds_t = (p_t * (dp_t - delta_s[qi, hh:hh + 1, :])).astype(BF16)
                dv_a = dv_a + jnp.dot(p_t.astype(BF16), dom, preferred_element_type=F32)
                dk_acc = dk_acc + jnp.dot(ds_t, qh, preferred_element_type=F32)
                dq_ref[rows, hh * LANES:(hh + 1) * LANES] += lax.dot_general(ds_t, kh, _DIMS["tn"],
                                                                             preferred_element_type=F32)
                return dk_acc, dv_a

            carry = (jnp.zeros((tk, LANES), F32), dv_acc)
            carry = step(kb, carry, True)
            carry = lax.fori_loop(kb + 1, nq, functools.partial(step, masked=False), carry)
            dk_out.append(carry[0])
            dv_acc = carry[1]
        dk_ref[:, 0:LANES] = dk_out[0].astype(dk_ref.dtype)
        dk_ref[:, LANES:2 * LANES] = dk_out[1].astype(dk_ref.dtype)
        dv_ref[...] = dv_acc.astype(dv_ref.dtype)

    return pl.pallas_call(
        body, name=name, grid=(npairs, t // tk),
        in_specs=[pl.BlockSpec((t, 2 * LANES), lambda j, kb: (0, j)),
                  pl.BlockSpec((tk, 2 * LANES), lambda j, kb: (kb, j)),
                  pl.BlockSpec((tk, LANES), lambda j, kb: (kb, j)),
                  pl.BlockSpec((t, LANES), lambda j, kb: (0, j)),
                  pl.BlockSpec((t, LANES), lambda j, kb: (0, j)),
                  pl.BlockSpec((1, nq, 8, tq), lambda j, kb: (j, 0, 0, 0))],
        out_specs=[pl.BlockSpec((t, 2 * LANES), lambda j, kb: (0, j)),
                   pl.BlockSpec((tk, 2 * LANES), lambda j, kb: (kb, j)),
                   pl.BlockSpec((tk, LANES), lambda j, kb: (kb, j))],
        out_shape=[jax.ShapeDtypeStruct((t, A_HEADS * LANES), F32), jax.ShapeDtypeStruct((t, A_HEADS * LANES), BF16),
                   jax.ShapeDtypeStruct((t, A_WIDTH), BF16)],
        scratch_shapes=[pltpu.VMEM((nq, 8, tq), F32)],
        compiler_params=_cparams("parallel", "arbitrary"),
    )(q, k, v, o, do, lse)


_SPLITS = (M_WIDTH, M_WIDTH, M_WIDTH, M_WIDTH, M_HEADS, M_HEADS, Q_RANK, KV_RANK, A_ROPE)
_OFFS = tuple(sum(_SPLITS[:i]) for i in range(len(_SPLITS) + 1))
_GATE_BLOCK_KR = A_NOPE


def _split_w_in(w_in):
    d = w_in.shape[0]
    w_m4 = w_in[:, :_OFFS[4]]
    w_a2 = w_in[:, _OFFS[6]:_OFFS[8]]
    z = lambda n: jnp.zeros((d, n), w_in.dtype)
    w_g = jnp.concatenate([w_in[:, _OFFS[4]:_OFFS[6]], z(_GATE_BLOCK_KR - 2 * M_HEADS), w_in[:, _OFFS[8]:_OFFS[9]],
                           z(LANES - _GATE_BLOCK_KR - A_ROPE)], axis=1)
    return w_m4, w_a2, w_g


def _merge_w_in_grad(g_m4, g_a2, g_g):
    return jnp.concatenate([g_m4, g_g[:, :2 * M_HEADS], g_a2, g_g[:, _GATE_BLOCK_KR:_GATE_BLOCK_KR + A_ROPE]], axis=1)


def _pad_w_q_b(w):
    r = w.shape[0]
    w3 = w.reshape(r, A_HEADS, A_NOPE + A_ROPE)
    return jnp.pad(w3, ((0, 0), (0, 0), (0, A_QK_PAD - A_NOPE - A_ROPE))).reshape(r, A_HEADS * A_QK_PAD)


def _unpad_w_q_b_grad(g):
    r = g.shape[0]
    return g.reshape(r, A_HEADS, A_QK_PAD)[:, :, :A_NOPE + A_ROPE].reshape(r, A_HEADS * (A_NOPE + A_ROPE))


def _split_w_kv_b(w):
    r = w.shape[0]
    w3 = w.reshape(r, A_HEADS, A_NOPE + A_V)
    wk = jnp.pad(w3[:, :, :A_NOPE], ((0, 0), (0, 0), (0, A_QK_PAD - A_NOPE))).reshape(r, A_HEADS * A_QK_PAD)
    wv = w3[:, :, A_NOPE:].reshape(r, A_WIDTH)
    return wk, wv


def _merge_w_kv_b_grad(gk, gv):
    r = gk.shape[0]
    gk3 = gk.reshape(r, A_HEADS, A_QK_PAD)[:, :, :A_NOPE]
    gv3 = gv.reshape(r, A_HEADS, A_V)
    return jnp.concatenate([gk3, gv3], axis=2).reshape(r, A_HEADS * (A_NOPE + A_V))


def _local_step(x, pos, target, nw, w):
    w_m4, w_a2, w_g = _split_w_in(w["w_in"])
    w_qb = _pad_w_q_b(w["w_q_b"])
    w_k, w_v = _split_w_kv_b(w["w_kv_b"])
    w_out_m, w_out_a = w["w_out"][:M_WIDTH], w["w_out"][M_WIDTH:]
    bias = jnp.pad(nw["b_gates"], ((0, 0), (0, LANES - 2 * M_HEADS)))
    mnorm = nw["mlstm_norm_w"].reshape(M_HEADS, M_HEAD_DIM)

    u1 = _rmsnorm_fwd(x, nw["attn_norm_w"], "attn_norm")
    m4 = _matmul(u1, w_m4, "nn", BF16, "proj_mlstm")
    a2 = _matmul(u1, w_a2, "nn", BF16, "proj_latents", tn=640)
    gates = _matmul(u1, w_g, "nn", F32, "proj_gates")
    hm, c_prev_all, nm_all = _mlstm_fwd(m4, gates, bias, mnorm, "mlstm_fwd")
    qa, kva = a2[:, :Q_RANK], a2[:, Q_RANK:]
    qa_n = _rmsnorm_fwd(qa, nw["q_a_norm_w"], "q_a_norm")
    kv_n = _rmsnorm_fwd(kva, nw["kv_a_norm_w"], "kv_a_norm")
    qpre = _matmul(qa_n, w_qb, "nn", F32, "q_up")
    kpre = _matmul(kv_n, w_k, "nn", F32, "k_up")
    vv = _matmul(kv_n, w_v, "nn", BF16, "v_up")
    qr, kr = _rope_fwd(qpre, kpre, gates, pos, "rope_fwd")
    ha, lse = _flash_fwd(qr, kr, vv, "attn_fwd")
    h1 = _matmul(hm, w_out_m, "nn", F32, "out_proj_m", residual=x)
    h1 = _matmul(ha, w_out_a, "nn", F32, "out_proj_a", residual=h1)
    u2 = _rmsnorm_fwd(h1, nw["ffn_norm_w"], "ffn_norm")
    gg = _matmul(u2, w["w_gate"], "nn", BF16, "ffn_gate", tn=1408)
    uu = _matmul(u2, w["w_up"], "nn", BF16, "ffn_up", tn=1408)
    act = _swiglu_fwd(gg, uu, "swiglu_fwd")
    h2 = _matmul(act, w["w_down"], "nn", F32, "ffn_down", residual=h1)

    dh2, loss, g_final = _final_loss(h2, target, nw["final_norm_w"].reshape(1, D_MODEL), "final_loss")
    grads = {"final_norm_w": g_final.reshape(D_MODEL)}
    dact = _matmul(dh2, w["w_down"], "nt", BF16, "d_act", tn=1408)
    grads["w_down"] = _matmul(act, dh2, "tn", F32, "g_w_down", tm=256)
    dgg, duu = _swiglu_bwd(gg, uu, dact, "swiglu_bwd")
    du2 = _matmul(dgg, w["w_gate"], "nt", F32, "d_u2_gate")
    du2 = _matmul(duu, w["w_up"], "nt", F32, "d_u2_up", residual=du2)
    grads["w_gate"] = _matmul(u2, dgg, "tn", F32, "g_w_gate", tn=1408)
    grads["w_up"] = _matmul(u2, duu, "tn", F32, "g_w_up", tn=1408)
    dh1, grads["ffn_norm_w"] = _rmsnorm_bwd(du2, h1, nw["ffn_norm_w"], "ffn_norm_bwd", F32, residual=dh2)
    dhm = _matmul(dh1, w_out_m, "nt", BF16, "d_hm")
    dha = _matmul(dh1, w_out_a, "nt", BF16, "d_ha")
    grads["w_out"] = jnp.concatenate([_matmul(hm, dh1, "tn", F32, "g_w_out_m"),
                                      _matmul(ha, dh1, "tn", F32, "g_w_out_a")], axis=0)

    dqr, dkr, dvv = _flash_bwd(qr, kr, vv, ha, dha, lse, "attn_bwd")
    dm4, dgates, small = _mlstm_bwd(m4, gates, bias, mnorm, c_prev_all, nm_all, dhm, "mlstm_bwd")
    grads["mlstm_norm_w"] = small[:M_HEADS].reshape(1, M_HEADS, M_HEAD_DIM)
    grads["b_gates"] = small[M_HEADS:M_HEADS + 1, :2 * M_HEADS]
    dqpre, dgk = _rope_bwd(dqr, dkr, dgates, pos, "rope_bwd")
    dqa_n = _matmul(dqpre, w_qb, "nt", BF16, "d_qa_n")
    grads["w_q_b"] = _unpad_w_q_b_grad(_matmul(qa_n, dqpre, "tn", F32, "g_w_q_b", tm=384))
    dkv_n = _matmul(dkr, w_k, "nt", F32, "d_kv_n_k")
    dkv_n = _matmul(dvv, w_v, "nt", BF16, "d_kv_n_v", residual=dkv_n)
    grads["w_kv_b"] = _merge_w_kv_b_grad(_matmul(kv_n, dkr, "tn", F32, "g_w_k", tm=256),
                                         _matmul(kv_n, dvv, "tn", F32, "g_w_v", tm=256))
    dqa, grads["q_a_norm_w"] = _rmsnorm_bwd(dqa_n, qa, nw["q_a_norm_w"], "q_a_norm_bwd", BF16)
    dkva, grads["kv_a_norm_w"] = _rmsnorm_bwd(dkv_n, kva, nw["kv_a_norm_w"], "kv_a_norm_bwd", BF16)
    da2 = jnp.concatenate([dqa, dkva], axis=1)
    du1 = _matmul(dm4, w_m4, "nt", F32, "d_u1_m4")
    du1 = _matmul(da2, w_a2, "nt", F32, "d_u1_a2", residual=du1)
    du1 = _matmul(dgk, w_g, "nt", F32, "d_u1_g", residual=du1)
    grads["w_in"] = _merge_w_in_grad(_matmul(u1, dm4, "tn", F32, "g_w_m4"),
                                     _matmul(u1, da2, "tn", F32, "g_w_a2", tn=640),
                                     _matmul(u1, dgk, "tn", F32, "g_w_g"))
    grad_x, grads["attn_norm_w"] = _rmsnorm_bwd(du1, x, nw["attn_norm_w"], "attn_norm_bwd", F32, residual=dh1)
    return loss, grad_x, grads


MESH = pl.DeviceIdType.MESH
N_CHIPS = 4
BIG = ("w_in", "w_q_b", "w_kv_b", "w_out", "w_gate", "w_up", "w_down")
ROW_SHARDED = ("w_out", "w_down")
SMALL = ("attn_norm_w", "b_gates", "mlstm_norm_w", "q_a_norm_w", "kv_a_norm_w", "ffn_norm_w", "final_norm_w")
HBM_SPEC = pl.BlockSpec(memory_space=pltpu.HBM)


def _place():
    x, y, c = lax.axis_index("x"), lax.axis_index("y"), lax.axis_index("c")
    others = [(1 - x, y), (x, 1 - y), (1 - x, 1 - y)]
    return x, y, c, others


def _gather_weights(shards):
    n = len(shards)

    def body(*refs):
        ins, outs = refs[:n], refs[n:2 * n]
        send_sems, recv_sems, local_sems = refs[2 * n:]
        x, y, c, others = _place()
        me = 2 * x + y
        sibling = (x, y, 1 - c)

        def half_rows(w, core):
            rh = ins[w].shape[0] // 2
            return pl.ds(core * rh, rh)

        def copy(w, k, slab, core, to, src=None):
            dst = outs[w].at[slab, half_rows(w, core)]
            return pltpu.make_async_remote_copy(
                src_ref=dst if src is None else src, dst_ref=dst, send_sem=send_sems.at[w * 6 + k],
                recv_sem=recv_sems.at[w * 6 + k], device_id=to, device_id_type=MESH)

        started = []
        for w in range(n):
            mine = pltpu.make_async_copy(ins[w], outs[w].at[me], local_sems.at[w])
            mine.start()
            started.append(mine)
        sends = []
        for w in range(n):
            for j, chip in enumerate(others):
                cp = copy(w, j, me, c, (*chip, c), src=ins[w].at[half_rows(w, c)])
                cp.start()
                sends.append(cp)
        for w in range(n):
            for j, (ox, oy) in enumerate(others):
                slab = 2 * ox + oy
                copy(w, j, slab, c, (x, y, c)).wait_recv()
                fwd = copy(w, 3 + j, slab, c, sibling)
                fwd.start()
                sends.append(fwd)
        for w in range(n):
            for j, (ox, oy) in enumerate(others):
                copy(w, 3 + j, 2 * ox + oy, 1 - c, (x, y, c)).wait_recv()
        for cp in sends:
            cp.wait_send()
        for cp in started:
            cp.wait()

    return pl.pallas_call(
        body, name="gather_weights", in_specs=[HBM_SPEC] * n, out_specs=[HBM_SPEC] * n,
        out_shape=[jax.ShapeDtypeStruct((N_CHIPS,) + s.shape, s.dtype) for s in shards],
        scratch_shapes=[pltpu.SemaphoreType.DMA((6 * n,)), pltpu.SemaphoreType.DMA((6 * n,)),
                        pltpu.SemaphoreType.DMA((n,))],
    )(*shards)


def _exchange(arrays, out_shapes, plan, name):
    n = len(arrays)

    def body(*refs):
        ins, outs = refs[:n], refs[n:2 * n]
        send_sems, recv_sems, local_sems = refs[2 * n:]
        local, remote = plan(ins, outs, _place())
        loc = [pltpu.make_async_copy(s, d, local_sems.at[k]) for k, (s, d) in enumerate(local)]
        rem = [pltpu.make_async_remote_copy(src_ref=s, dst_ref=d, send_sem=send_sems.at[k], recv_sem=recv_sems.at[k],
                                            device_id=to, device_id_type=MESH) for k, (s, d, to) in enumerate(remote)]
        for cp in loc + rem:
            cp.start()
        for cp in rem:
            cp.wait_recv()
        for cp in rem:
            cp.wait_send()
        for cp in loc:
            cp.wait()

    n_remote = {"sibling_halves": n, "chip_partials": 3 * n, "sibling_result": n}[name]
    return pl.pallas_call(
        body, name=name, in_specs=[HBM_SPEC] * n, out_specs=[HBM_SPEC] * n, out_shape=out_shapes,
        scratch_shapes=[pltpu.SemaphoreType.DMA((n_remote,)), pltpu.SemaphoreType.DMA((n_remote,)),
                        pltpu.SemaphoreType.DMA((n,))],
    )(*arrays)


def _plan_sibling_halves(ins, outs, place):
    x, y, c, _ = place
    return [], [(ins[w], outs[w], (x, y, 1 - c)) for w in range(len(ins))]


def _plan_chip_partials(ins, outs, place):
    x, y, c, others = place
    remote = []
    for w in range(len(ins)):
        for j, (ox, oy) in enumerate(others):
            remote.append((ins[w].at[2 * ox + oy], outs[w].at[j], (ox, oy, c)))
    return [], remote


def _plan_sibling_result(ins, outs, place):
    x, y, c, _ = place
    n = len(ins)
    return ([(ins[w], outs[w].at[c]) for w in range(n)],
            [(ins[w], outs[w].at[c], (x, y, 1 - c)) for w in range(n)])


def _row_tile(rows, limit=512):
    for tr in range(min(limit, rows), 0, -1):
        if rows % tr == 0 and tr % 16 == 0:
            return tr
    raise ValueError(rows)


def _elementwise(fn, ins, out_dtypes, name):
    shape = ins[0].shape
    c = shape[-1]
    rows = math.prod(shape[:-1])
    tr = _row_tile(rows)
    ins2 = [a.reshape(rows, c) for a in ins]
    n_in = len(ins)

    def body(*refs):
        res = fn(*[r[...] for r in refs[:n_in]])
        for r, val in zip(refs[n_in:], res):
            r[...] = val.astype(r.dtype)

    blk = pl.BlockSpec((tr, c), lambda i: (i, 0))
    outs = pl.pallas_call(
        body, name=name, grid=(rows // tr,), in_specs=[blk] * n_in, out_specs=[blk] * len(out_dtypes),
        out_shape=[jax.ShapeDtypeStruct((rows, c), d) for d in out_dtypes], compiler_params=_cparams("parallel"),
    )(*ins2)
    return [o.reshape(shape) for o in outs]


def _adamw_math(w, g, m, v):
    m = ADAM_B1 * m + (1.0 - ADAM_B1) * g
    v = ADAM_B2 * v + (1.0 - ADAM_B2) * (g * g)
    m_hat = m / (1.0 - ADAM_B1 ** ADAM_STEP)
    v_hat = v / (1.0 - ADAM_B2 ** ADAM_STEP)
    delta = -ADAM_LR * (m_hat / (jnp.sqrt(v_hat) + ADAM_EPS) + ADAM_WD * w)
    return delta, m, v


def _to_shard_halves(g, row_sharded):
    if row_sharded:
        rs = g.shape[0] // N_CHIPS
        return g.reshape(N_CHIPS, 2, rs // 2, g.shape[1]).transpose(1, 0, 2, 3)
    cs = g.shape[1] // N_CHIPS
    return g.reshape(2, g.shape[0] // 2, N_CHIPS, cs).transpose(0, 2, 1, 3)


def _reduce_big_grads(grads):
    c = lax.axis_index("c")
    me = 2 * lax.axis_index("x") + lax.axis_index("y")
    halves = [_to_shard_halves(grads[n], n in ROW_SHARDED) for n in BIG]
    keep = [lax.dynamic_index_in_dim(h, c, 0, keepdims=False) for h in halves]
    send = [lax.dynamic_index_in_dim(h, 1 - c, 0, keepdims=False).astype(BF16) for h in halves]
    got = _exchange(send, [jax.ShapeDtypeStruct(s.shape, BF16) for s in send], _plan_sibling_halves, "sibling_halves")
    part = [_elementwise(lambda a, b: (a + b.astype(F32),), [k, r], [BF16], "chip_partial_%s" % n)[0]
            for n, k, r in zip(BIG, keep, got)]
    got3 = _exchange(part, [jax.ShapeDtypeStruct((3,) + p.shape[1:], BF16) for p in part], _plan_chip_partials,
                     "chip_partials")
    final = []
    for n, k, r, g3 in zip(BIG, keep, got, got3):
        own_k = lax.dynamic_index_in_dim(k, me, 0, keepdims=False)
        own_r = lax.dynamic_index_in_dim(r, me, 0, keepdims=False)
        final.append(_elementwise(
            lambda a, b, p0, p1, p2: (a + b.astype(F32) + p0.astype(F32) + p1.astype(F32) + p2.astype(F32),),
            [own_k, own_r, g3[0], g3[1], g3[2]], [F32], "shard_sum_%s" % n)[0])
    both = _exchange(final, [jax.ShapeDtypeStruct((2,) + f.shape, F32) for f in final], _plan_sibling_result,
                     "sibling_result")
    return {n: b.reshape(2 * b.shape[1], b.shape[2]) for n, b in zip(BIG, both)}


SMALL_ROWS = 8
SMALL_LAYOUT = {"attn_norm_w": (0, 0, 1024), "ffn_norm_w": (1, 0, 1024), "final_norm_w": (2, 0, 1024),
                "q_a_norm_w": (3, 0, 384), "kv_a_norm_w": (3, 384, 256), "mlstm_norm_w": (4, 0, 512),
                "b_gates": (4, 512, 8)}
LOSS_SLOT = (5, 0)


def _pack_small(vals, loss=None):
    tile = jnp.zeros((SMALL_ROWS, D_MODEL), F32)
    for n, (r, c0, width) in SMALL_LAYOUT.items():
        tile = tile.at[r, c0:c0 + width].set(vals[n].reshape(width).astype(F32))
    if loss is not None:
        tile = tile.at[LOSS_SLOT[0], LOSS_SLOT[1]].set(loss)
    return tile


def _unpack_small(tile, shapes):
    return {n: tile[r, c0:c0 + width].reshape(shapes[n]) for n, (r, c0, width) in SMALL_LAYOUT.items()}


def _small_allreduce_adamw(g_tile, w_tile, m_tile, v_tile):
    def body(g_ref, w_ref, m_ref, v_ref, gsum_ref, d_ref, mo_ref, vo_ref, slots, send_sems, recv_sems):
        x, y, c, _ = _place()
        me = 4 * x + 2 * y + c
        slots[me] = g_ref[...]
        copies = []
        for k in range(1, 8):
            dx, dy, dc = (k >> 2) & 1, (k >> 1) & 1, k & 1
            to = (x ^ dx, y ^ dy, c ^ dc)
            cp = pltpu.make_async_remote_copy(src_ref=g_ref, dst_ref=slots.at[me], send_sem=send_sems.at[k - 1],
                                              recv_sem=recv_sems.at[k - 1], device_id=to, device_id_type=MESH)
            cp.start()
            copies.append(cp)
        for k in range(1, 8):
            src = me ^ k
            pltpu.make_async_remote_copy(src_ref=g_ref, dst_ref=slots.at[src], send_sem=send_sems.at[k - 1],
                                         recv_sem=recv_sems.at[k - 1], device_id=(x, y, c),
                                         device_id_type=MESH).wait_recv()
        for cp in copies:
            cp.wait_send()
        total = slots[0]
        for d in range(1, 8):
            total = total + slots[d]
        gsum_ref[...] = total
        delta, m_new, v_new = _adamw_math(w_ref[...], total, m_ref[...], v_ref[...])
        d_ref[...] = delta
        mo_ref[...] = m_new
        vo_ref[...] = v_new

    vm = pl.BlockSpec(memory_space=pltpu.VMEM)
    tile = jax.ShapeDtypeStruct((SMALL_ROWS, D_MODEL), F32)
    return pl.pallas_call(
        body, name="small_allreduce_adamw", in_specs=[vm] * 4, out_specs=[vm] * 4, out_shape=[tile] * 4,
        scratch_shapes=[pltpu.VMEM((8, SMALL_ROWS, D_MODEL), F32), pltpu.SemaphoreType.DMA((7,)),
                        pltpu.SemaphoreType.DMA((7,))],
    )(g_tile, w_tile, m_tile, v_tile)


def kernel(x, positions, attn_norm_w, w_in, b_gates, mlstm_norm_w, q_a_norm_w, w_q_b, kv_a_norm_w, w_kv_b, w_out, ffn_norm_w, w_gate, w_up, w_down, final_norm_w, loss_target, m_attn_norm_w, m_w_in, m_b_gates, m_mlstm_norm_w, m_q_a_norm_w, m_w_q_b, m_kv_a_norm_w, m_w_kv_b, m_w_out, m_ffn_norm_w, m_w_gate, m_w_up, m_w_down, m_final_norm_w, v_attn_norm_w, v_w_in, v_b_gates, v_mlstm_norm_w, v_q_a_norm_w, v_w_q_b, v_kv_a_norm_w, v_w_kv_b, v_w_out, v_ffn_norm_w, v_w_gate, v_w_up, v_w_down, v_final_norm_w):
    names = ("attn_norm_w", "w_in", "b_gates", "mlstm_norm_w", "q_a_norm_w", "w_q_b", "kv_a_norm_w", "w_kv_b", "w_out",
             "ffn_norm_w", "w_gate", "w_up", "w_down", "final_norm_w")
    wts = dict(zip(names, (attn_norm_w, w_in, b_gates, mlstm_norm_w, q_a_norm_w, w_q_b, kv_a_norm_w, w_kv_b, w_out,
                           ffn_norm_w, w_gate, w_up, w_down, final_norm_w)))
    mom = dict(zip(names, (m_attn_norm_w, m_w_in, m_b_gates, m_mlstm_norm_w, m_q_a_norm_w, m_w_q_b, m_kv_a_norm_w,
                           m_w_kv_b, m_w_out, m_ffn_norm_w, m_w_gate, m_w_up, m_w_down, m_final_norm_w)))
    vel = dict(zip(names, (v_attn_norm_w, v_w_in, v_b_gates, v_mlstm_norm_w, v_q_a_norm_w, v_w_q_b, v_kv_a_norm_w,
                           v_w_kv_b, v_w_out, v_ffn_norm_w, v_w_gate, v_w_up, v_w_down, v_final_norm_w)))
    t = x.shape[1]

    gathered = _gather_weights([wts[n][0].astype(BF16) for n in BIG])
    full = {}
    for n, g in zip(BIG, gathered):
        full[n] = g.reshape(N_CHIPS * g.shape[1], g.shape[2]) if n in ROW_SHARDED else jnp.concatenate(list(g), axis=1)

    nw = {n: wts[n] for n in SMALL}
    loss, grad_x, grads = _local_step(x[0], positions.reshape(t, 1), loss_target[0], nw, full)

    big = _reduce_big_grads(grads)
    outs_g, outs_d, outs_m, outs_v = {}, {}, {}, {}
    for n in BIG:
        d, m_new, v_new = _elementwise(_adamw_math, [wts[n][0], big[n], mom[n][0], vel[n][0]], [F32, F32, F32],
                                       "adamw_%s" % n)
        outs_g[n], outs_d[n], outs_m[n], outs_v[n] = big[n][None], d[None], m_new[None], v_new[None]
    shapes = {n: wts[n].shape for n in SMALL}
    g_tile = _pack_small({n: grads[n] for n in SMALL}, loss=loss[0, 0])
    gsum, d_tile, m_tile, v_tile = _small_allreduce_adamw(g_tile, _pack_small(wts), _pack_small(mom), _pack_small(vel))
    outs_g.update(_unpack_small(gsum, shapes))
    outs_d.update(_unpack_small(d_tile, shapes))
    outs_m.update(_unpack_small(m_tile, shapes))
    outs_v.update(_unpack_small(v_tile, shapes))
    total_loss = gsum[LOSS_SLOT[0], LOSS_SLOT[1]]
    return (total_loss, grad_x[None], *[outs_g[n] for n in names], *[outs_d[n] for n in names],
            *[outs_m[n] for n in names], *[outs_v[n] for n in names])
```

```python
import functools
import math

import jax
import jax.numpy as jnp
from jax import lax
from jax.experimental import pallas as pl
from jax.experimental.pallas import tpu as pltpu

F32 = jnp.float32
BF16 = jnp.bfloat16

D_MODEL = 1024
M_HEADS = 4
M_HEAD_DIM = 128
M_WIDTH = M_HEADS * M_HEAD_DIM
M_CHUNK = 64
A_HEADS = 8
A_NOPE = 64
A_ROPE = 32
A_V = 64
A_WIDTH = A_HEADS * A_V
A_QK_PAD = 128
Q_RANK = 384
KV_RANK = 256
ROPE_THETA = 10000.0
D_FF = 2816
D_IN = 2728
EPS = 1e-6
ATTN_SCALE = (A_NOPE + A_ROPE) ** -0.5
M_SCALE = M_HEAD_DIM ** -0.5

ADAM_LR = 0.001
ADAM_B1 = 0.9
ADAM_B2 = 0.999
ADAM_EPS = 1e-08
ADAM_WD = 0.01
ADAM_STEP = 10

VMEM_LIMIT_BYTES = 56 * 1024 * 1024
LANES = 128
NEG_INF = float("-inf")


def _cparams(*sem):
    return pltpu.CompilerParams(dimension_semantics=sem if sem else None, vmem_limit_bytes=VMEM_LIMIT_BYTES)


_DIMS = {"nn": (((1,), (0,)), ((), ())), "nt": (((1,), (1,)), ((), ())), "tn": (((0,), (0,)), ((), ()))}


def _matmul(a, b, mode, out_dtype, name, tm=512, tn=512, residual=None):
    if mode == "nn":
        (m, k), (k2, n) = a.shape, b.shape
    elif mode == "nt":
        (m, k), (n, k2) = a.shape, b.shape
    else:
        (k, m), (k2, n) = a.shape, b.shape
    assert k == k2, (a.shape, b.shape, mode)
    tm, tn = min(tm, m), min(tn, n)
    assert m % tm == 0 and n % tn == 0, (m, n, tm, tn)
    a_spec = pl.BlockSpec((k, tm), lambda i, j: (0, i)) if mode == "tn" else pl.BlockSpec((tm, k), lambda i, j: (i, 0))
    b_spec = pl.BlockSpec((tn, k), lambda i, j: (j, 0)) if mode == "nt" else pl.BlockSpec((k, tn), lambda i, j: (0, j))
    o_spec = pl.BlockSpec((tm, tn), lambda i, j: (i, j))
    dims = _DIMS[mode]

    def body(a_ref, b_ref, *rest):
        o_ref = rest[-1]
        acc = lax.dot_general(a_ref[...].astype(BF16), b_ref[...].astype(BF16), dims, preferred_element_type=F32)
        if residual is not None:
            acc = acc + rest[0][...].astype(F32)
        o_ref[...] = acc.astype(o_ref.dtype)

    ins = [a, b] + ([residual] if residual is not None else [])
    in_specs = [a_spec, b_spec] + ([o_spec] if residual is not None else [])
    return pl.pallas_call(
        body, name=name, grid=(m // tm, n // tn), in_specs=in_specs, out_specs=o_spec,
        out_shape=jax.ShapeDtypeStruct((m, n), out_dtype), compiler_params=_cparams("parallel", "parallel"),
    )(*ins)


def _rmsnorm_fwd(x, w, name, tm=512):
    t, d = x.shape

    def body(x_ref, w_ref, o_ref):
        xf = x_ref[...].astype(F32)
        r = lax.rsqrt(jnp.mean(xf * xf, axis=-1, keepdims=True) + EPS)
        o_ref[...] = (xf * r * w_ref[...]).astype(o_ref.dtype)

    return pl.pallas_call(
        body, name=name, grid=(t // tm,),
        in_specs=[pl.BlockSpec((tm, d), lambda i: (i, 0)), pl.BlockSpec((1, d), lambda i: (0, 0))],
        out_specs=pl.BlockSpec((tm, d), lambda i: (i, 0)),
        out_shape=jax.ShapeDtypeStruct((t, d), BF16), compiler_params=_cparams("parallel"),
    )(x, w)


def _rmsnorm_bwd(dy, x, w, name, out_dtype, residual=None, tm=512):
    t, d = x.shape

    def body(dy_ref, x_ref, w_ref, *rest):
        dx_ref, dw_ref = rest[-2], rest[-1]
        xf = x_ref[...].astype(F32)
        r = lax.rsqrt(jnp.mean(xf * xf, axis=-1, keepdims=True) + EPS)
        xh = xf * r
        dyf = dy_ref[...].astype(F32)
        dyh = dyf * w_ref[...]
        dx = r * (dyh - xh * jnp.mean(dyh * xh, axis=-1, keepdims=True))
        if residual is not None:
            dx = dx + rest[0][...].astype(F32)
        dx_ref[...] = dx.astype(dx_ref.dtype)
        part = jnp.sum(dyf * xh, axis=0, keepdims=True)

        @pl.when(pl.program_id(0) == 0)
        def _():
            dw_ref[...] = part

        @pl.when(pl.program_id(0) > 0)
        def _():
            dw_ref[...] += part

    row = pl.BlockSpec((tm, d), lambda i: (i, 0))
    vec = pl.BlockSpec((1, d), lambda i: (0, 0))
    ins = [dy, x, w] + ([residual] if residual is not None else [])
    return pl.pallas_call(
        body, name=name, grid=(t // tm,), in_specs=[row, row, vec] + ([row] if residual is not None else []),
        out_specs=[row, vec],
        out_shape=[jax.ShapeDtypeStruct((t, d), out_dtype), jax.ShapeDtypeStruct((1, d), F32)],
        compiler_params=_cparams("arbitrary"),
    )(*ins)


def _final_loss(h, target, w, name, tm=512):
    t, d = h.shape

    def body(h_ref, t_ref, w_ref, dh_ref, loss_ref, dw_ref):
        xf = h_ref[...]
        r = lax.rsqrt(jnp.mean(xf * xf, axis=-1, keepdims=True) + EPS)
        xh = xf * r
        err = xh * w_ref[...] - t_ref[...]
        part_loss = 0.5 * jnp.sum(jnp.sum(err * err, axis=-1, keepdims=True), axis=0, keepdims=True) * (1.0 / d)
        dy = err * (1.0 / d)
        dyh = dy * w_ref[...]
        dh_ref[...] = r * (dyh - xh * jnp.mean(dyh * xh, axis=-1, keepdims=True))
        part_dw = jnp.sum(dy * xh, axis=0, keepdims=True)
        part_loss = jnp.broadcast_to(part_loss, (1, LANES))

        @pl.when(pl.program_id(0) == 0)
        def _():
            dw_ref[...] = part_dw
            loss_ref[...] = part_loss

        @pl.when(pl.program_id(0) > 0)
        def _():
            dw_ref[...] += part_dw
            loss_ref[...] += part_loss

    row = pl.BlockSpec((tm, d), lambda i: (i, 0))
    vec = pl.BlockSpec((1, d), lambda i: (0, 0))
    return pl.pallas_call(
        body, name=name, grid=(t // tm,), in_specs=[row, row, vec],
        out_specs=[row, pl.BlockSpec((1, LANES), lambda i: (0, 0)), vec],
        out_shape=[jax.ShapeDtypeStruct((t, d), F32), jax.ShapeDtypeStruct((1, LANES), F32),
                   jax.ShapeDtypeStruct((1, d), F32)],
        compiler_params=_cparams("arbitrary"),
    )(h, target, w)


def _swiglu_fwd(g, u, name, tm=512, tn=1408):
    t, n = g.shape

    def body(g_ref, u_ref, o_ref):
        gf = g_ref[...].astype(F32)
        o_ref[...] = (gf * jax.nn.sigmoid(gf) * u_ref[...].astype(F32)).astype(o_ref.dtype)

    blk = pl.BlockSpec((tm, tn), lambda i, j: (i, j))
    return pl.pallas_call(
        body, name=name, grid=(t // tm, n // tn), in_specs=[blk, blk], out_specs=blk,
        out_shape=jax.ShapeDtypeStruct((t, n), BF16), compiler_params=_cparams("parallel", "parallel"),
    )(g, u)


def _swiglu_bwd(g, u, dact, name, tm=512, tn=1408):
    t, n = g.shape

    def body(g_ref, u_ref, d_ref, dg_ref, du_ref):
        gf = g_ref[...].astype(F32)
        uf = u_ref[...].astype(F32)
        df = d_ref[...].astype(F32)
        s = jax.nn.sigmoid(gf)
        dg_ref[...] = (df * uf * s * (1.0 + gf * (1.0 - s))).astype(dg_ref.dtype)
        du_ref[...] = (df * gf * s).astype(du_ref.dtype)

    blk = pl.BlockSpec((tm, tn), lambda i, j: (i, j))
    out = jax.ShapeDtypeStruct((t, n), BF16)
    return pl.pallas_call(
        body, name=name, grid=(t // tm, n // tn), in_specs=[blk, blk, blk], out_specs=[blk, blk],
        out_shape=[out, out], compiler_params=_cparams("parallel", "parallel"),
    )(g, u, dact)


M_BLOCK = 512
M_CHUNKS_PER_BLOCK = M_BLOCK // M_CHUNK


def _mlstm_chunk_fwd(q, k, v, i_col, f_col, c_prev, n_prev, m_prev):
    L = M_CHUNK
    tt = lax.broadcasted_iota(jnp.int32, (L, L), 0)
    ss = lax.broadcasted_iota(jnp.int32, (L, L), 1)
    eye = tt == ss
    causal = ss <= tt

    def to_row(col):
        return jnp.sum(jnp.where(eye, col, 0.0), axis=0, keepdims=True)

    lf_col = jnp.minimum(f_col, 0.0) - jnp.log(1.0 + jnp.exp(-jnp.abs(f_col)))
    lf_row = to_row(lf_col)
    li_row = to_row(i_col)
    b_col = jnp.sum(jnp.where(causal, lf_row, 0.0), axis=1, keepdims=True)
    b_row = jnp.sum(jnp.where(tt <= ss, lf_col, 0.0), axis=0, keepdims=True)
    g = jnp.sum(lf_col, axis=0, keepdims=True)
    a_row = g - b_row + li_row
    a_col = g - b_col + i_col
    m_loc = jnp.max(a_row, axis=1, keepdims=True)
    d_log = jnp.where(causal, b_col - b_row + li_row, NEG_INF)
    inter = b_col + m_prev
    m_t = jnp.maximum(jnp.max(d_log, axis=1, keepdims=True), inter)
    dmat = jnp.exp(d_log - m_t)
    amat = lax.dot_general(q, k, _DIMS["nt"], preferred_element_type=F32) * M_SCALE
    p = dmat * amat
    sig = jnp.exp(inter - m_t)
    c_prev_b = c_prev.astype(BF16)
    qc = jnp.dot(q, c_prev_b, preferred_element_type=F32) * M_SCALE
    num = jnp.dot(p.astype(BF16), v, preferred_element_type=F32) + sig * qc
    qn = jnp.sum(q.astype(F32) * n_prev, axis=1, keepdims=True) * M_SCALE
    den_raw = jnp.sum(p, axis=1, keepdims=True) + sig * qn
    floor = jnp.exp(-m_t)
    den = jnp.maximum(jnp.abs(den_raw), floor)
    h = num / den
    m_new = jnp.maximum(g + m_prev, m_loc)
    w_col = jnp.exp(a_col - m_new)
    alpha = jnp.exp(g + m_prev - m_new)
    return dict(eye=eye, causal=causal, tt=tt, ss=ss, f_col=f_col, dmat=dmat, amat=amat, p=p, sig=sig, qc=qc,
                qn=qn, num=num, den_raw=den_raw, floor=floor, den=den, h=h, m_new=m_new, w_col=w_col, alpha=alpha,
                c_prev_b=c_prev_b)


def _head_gates(gc, hd):
    return gc[:, hd:hd + 1], gc[:, M_HEADS + hd:M_HEADS + hd + 1]


def _mlstm_fwd(m4, gates, bias, norm_w, name):
    t = m4.shape[0]
    nblk = t // M_BLOCK
    nc = t // M_CHUNK
    L, dh = M_CHUNK, M_HEAD_DIM

    def body(m4_ref, g_ref, b_ref, w_ref, hm_ref, cp_ref, nm_ref, c_s, n_s, m_s):
        @pl.when(pl.program_id(0) == 0)
        def _():
            c_s[...] = jnp.zeros_like(c_s)
            n_s[...] = jnp.zeros_like(n_s)
            m_s[...] = jnp.zeros_like(m_s)

        row8 = lax.broadcasted_iota(jnp.int32, (8, dh), 0)

        def chunk(c, carry):
            rows = pl.ds(pl.multiple_of(c * L, L), L)
            gc = g_ref[rows, :] + b_ref[...]
            for hd in range(M_HEADS):
                q = m4_ref[rows, hd * dh:(hd + 1) * dh]
                k = m4_ref[rows, M_WIDTH + hd * dh:M_WIDTH + (hd + 1) * dh]
                v = m4_ref[rows, 2 * M_WIDTH + hd * dh:2 * M_WIDTH + (hd + 1) * dh]
                o = m4_ref[rows, 3 * M_WIDTH + hd * dh:3 * M_WIDTH + (hd + 1) * dh].astype(F32)
                i_col, f_col = _head_gates(gc, hd)
                c_prev = c_s[hd]
                n_prev = n_s[hd, 0:1, :]
                m_prev = m_s[hd, 0:1, 0:1]
                r = _mlstm_chunk_fwd(q, k, v, i_col, f_col, c_prev, n_prev, m_prev)
                h = r["h"]
                hn = h * lax.rsqrt(jnp.mean(h * h, axis=-1, keepdims=True) + EPS) * w_ref[hd:hd + 1, :]
                hm_ref[rows, hd * dh:(hd + 1) * dh] = (hn * jax.nn.sigmoid(o)).astype(hm_ref.dtype)
                cp_ref[hd, c] = r["c_prev_b"]
                nm_ref[hd, c] = jnp.where(row8 == 0, n_prev, jnp.where(row8 == 1, m_prev, 0.0))
                kw = k.astype(F32) * r["w_col"]
                c_s[hd] = r["alpha"] * c_prev + lax.dot_general(kw.astype(BF16), v, _DIMS["tn"],
                                                                preferred_element_type=F32)
                n_s[hd, 0:1, :] = r["alpha"] * n_prev + jnp.sum(kw, axis=0, keepdims=True)
                m_s[hd] = jnp.broadcast_to(r["m_new"], (8, dh))
            return carry

        lax.fori_loop(0, M_CHUNKS_PER_BLOCK, chunk, 0)

    return pl.pallas_call(
        body, name=name, grid=(nblk,),
        in_specs=[pl.BlockSpec((M_BLOCK, 4 * M_WIDTH), lambda i: (i, 0)),
                  pl.BlockSpec((M_BLOCK, LANES), lambda i: (i, 0)),
                  pl.BlockSpec((1, LANES), lambda i: (0, 0)),
                  pl.BlockSpec((M_HEADS, dh), lambda i: (0, 0))],
        out_specs=[pl.BlockSpec((M_BLOCK, M_WIDTH), lambda i: (i, 0)),
                   pl.BlockSpec((M_HEADS, M_CHUNKS_PER_BLOCK, dh, dh), lambda i: (0, i, 0, 0)),
                   pl.BlockSpec((M_HEADS, M_CHUNKS_PER_BLOCK, 8, dh), lambda i: (0, i, 0, 0))],
        out_shape=[jax.ShapeDtypeStruct((t, M_WIDTH), BF16),
                   jax.ShapeDtypeStruct((M_HEADS, nc, dh, dh), BF16),
                   jax.ShapeDtypeStruct((M_HEADS, nc, 8, dh), F32)],
        scratch_shapes=[pltpu.VMEM((M_HEADS, dh, dh), F32), pltpu.VMEM((M_HEADS, 8, dh), F32),
                        pltpu.VMEM((M_HEADS, 8, dh), F32)],
        compiler_params=_cparams("arbitrary"),
    )(m4, gates, bias, norm_w)


def _mlstm_bwd(m4, gates, bias, norm_w, c_prev_all, nm_all, dhm, name):
    t = m4.shape[0]
    nblk = t // M_BLOCK
    L, dh = M_CHUNK, M_HEAD_DIM

    def body(m4_ref, g_ref, b_ref, w_ref, cp_ref, nm_ref, dhm_ref, dm4_ref, dg_ref, small_ref, dc_s, dn_s):
        @pl.when(pl.program_id(0) == 0)
        def _():
            dc_s[...] = jnp.zeros_like(dc_s)
            dn_s[...] = jnp.zeros_like(dn_s)
            small_ref[...] = jnp.zeros_like(small_ref)

        lane = lax.broadcasted_iota(jnp.int32, (L, LANES), 1)
        row8 = lax.broadcasted_iota(jnp.int32, (8, LANES), 0)

        def chunk(ci, carry):
            c = M_CHUNKS_PER_BLOCK - 1 - ci
            rows = pl.ds(pl.multiple_of(c * L, L), L)
            gc = g_ref[rows, :] + b_ref[...]
            dg_tile = jnp.zeros((L, LANES), F32)
            small = jnp.zeros((8, LANES), F32)
            for hd in range(M_HEADS):
                q = m4_ref[rows, hd * dh:(hd + 1) * dh]
                k = m4_ref[rows, M_WIDTH + hd * dh:M_WIDTH + (hd + 1) * dh]
                v = m4_ref[rows, 2 * M_WIDTH + hd * dh:2 * M_WIDTH + (hd + 1) * dh]
                o = m4_ref[rows, 3 * M_WIDTH + hd * dh:3 * M_WIDTH + (hd + 1) * dh].astype(F32)
                i_col, f_col = _head_gates(gc, hd)
                c_prev = cp_ref[hd, c].astype(F32)
                nm = nm_ref[hd, c]
                n_prev = nm[0:1, :]
                m_prev = nm[1:2, 0:1]
                r = _mlstm_chunk_fwd(q, k, v, i_col, f_col, c_prev, n_prev, m_prev)
                eye, tt, ss = r["eye"], r["tt"], r["ss"]
                h, den, sig, p = r["h"], r["den"], r["sig"], r["p"]
                w_col, alpha, c_prev_b = r["w_col"], r["alpha"], r["c_prev_b"]

                def to_row(col):
                    return jnp.sum(jnp.where(eye, col, 0.0), axis=0, keepdims=True)

                def to_col(row):
                    return jnp.sum(jnp.where(eye, row, 0.0), axis=1, keepdims=True)

                w_h = w_ref[hd:hd + 1, :]
                rn = lax.rsqrt(jnp.mean(h * h, axis=-1, keepdims=True) + EPS)
                hh = h * rn
                og = jax.nn.sigmoid(o)
                dhm_c = dhm_ref[rows, hd * dh:(hd + 1) * dh].astype(F32)
                dhn = dhm_c * og
                d_o = dhm_c * hh * w_h * og * (1.0 - og)
                small = small + jnp.where(row8 == hd, jnp.sum(dhn * hh, axis=0, keepdims=True), 0.0)
                dhh = dhn * w_h
                dh_ = rn * (dhh - hh * jnp.mean(dhh * hh, axis=-1, keepdims=True))

                dnum = dh_ / den
                dden = -jnp.sum(dh_ * h, axis=-1, keepdims=True) / den
                dden_raw = jnp.where(jnp.abs(r["den_raw"]) >= r["floor"], dden * jnp.sign(r["den_raw"]), 0.0)
                dnum_b = dnum.astype(BF16)
                dp = lax.dot_general(dnum_b, v, _DIMS["nt"], preferred_element_type=F32) + dden_raw
                p_b = p.astype(BF16)
                dc = dc_s[hd]
                dn = dn_s[hd, 0:1, :]
                dc_b = dc.astype(BF16)
                g2 = sig * dnum
                g2_b = g2.astype(BF16)
                sd = sig * dden_raw
                da_mat = (dp * r["dmat"]).astype(BF16)
                dqs = (lax.dot_general(g2_b, c_prev_b, _DIMS["nt"], preferred_element_type=F32)
                       + sd * n_prev + jnp.dot(da_mat, k, preferred_element_type=F32))
                r_mat = lax.dot_general(v, dc_b, _DIMS["nt"], preferred_element_type=F32) + dn
                kf = k.astype(F32)
                d_k = (lax.dot_general(da_mat, q, _DIMS["tn"], preferred_element_type=F32) * M_SCALE + w_col * r_mat)
                d_v = (lax.dot_general(p_b, dnum_b, _DIMS["tn"], preferred_element_type=F32)
                       + w_col * jnp.dot(k, dc_b, preferred_element_type=F32))
                d_omega = jnp.sum(kf * r_mat, axis=-1, keepdims=True)
                da_col = d_omega * w_col
                dsig = jnp.sum(dnum * r["qc"], axis=-1, keepdims=True) + dden_raw * r["qn"]
                pp = dp * p
                r1 = jnp.sum(pp, axis=1, keepdims=True)
                c1_col = to_col(jnp.sum(pp, axis=0, keepdims=True))
                d_alpha = (jnp.sum(jnp.sum(dc * c_prev, axis=1, keepdims=True), axis=0, keepdims=True)
                           + jnp.sum(dn * n_prev, axis=1, keepdims=True))
                dgl = jnp.sum(da_col, axis=0, keepdims=True) + d_alpha * alpha
                db_col = r1 + dsig * sig - c1_col - da_col
                dli_col = c1_col + da_col
                db_row = to_row(db_col)
                dlf_col = jnp.sum(jnp.where(ss >= tt, db_row, 0.0), axis=1, keepdims=True) + dgl
                df_col = dlf_col * jax.nn.sigmoid(-r["f_col"])

                dc_s[hd] = alpha * dc + lax.dot_general(q, g2_b, _DIMS["tn"], preferred_element_type=F32) * M_SCALE
                dn_s[hd, 0:1, :] = alpha * dn + jnp.sum(sd * q.astype(F32), axis=0, keepdims=True) * M_SCALE

                dm4_ref[rows, hd * dh:(hd + 1) * dh] = (dqs * M_SCALE).astype(dm4_ref.dtype)
                dm4_ref[rows, M_WIDTH + hd * dh:M_WIDTH + (hd + 1) * dh] = d_k.astype(dm4_ref.dtype)
                dm4_ref[rows, 2 * M_WIDTH + hd * dh:2 * M_WIDTH + (hd + 1) * dh] = d_v.astype(dm4_ref.dtype)
                dm4_ref[rows, 3 * M_WIDTH + hd * dh:3 * M_WIDTH + (hd + 1) * dh] = d_o.astype(dm4_ref.dtype)
                dg_tile = dg_tile + jnp.where(lane == hd, dli_col, 0.0) + jnp.where(lane == M_HEADS + hd, df_col, 0.0)
            dg_ref[rows, :] = dg_tile
            small = small + jnp.where(row8 == M_HEADS, jnp.sum(dg_tile, axis=0, keepdims=True), 0.0)
            small_ref[...] += small
            return carry

        lax.fori_loop(0, M_CHUNKS_PER_BLOCK, chunk, 0)

    rev = lambda i: nblk - 1 - i
    return pl.pallas_call(
        body, name=name, grid=(nblk,),
        in_specs=[pl.BlockSpec((M_BLOCK, 4 * M_WIDTH), lambda i: (rev(i), 0)),
                  pl.BlockSpec((M_BLOCK, LANES), lambda i: (rev(i), 0)),
                  pl.BlockSpec((1, LANES), lambda i: (0, 0)),
                  pl.BlockSpec((M_HEADS, dh), lambda i: (0, 0)),
                  pl.BlockSpec((M_HEADS, M_CHUNKS_PER_BLOCK, dh, dh), lambda i: (0, rev(i), 0, 0)),
                  pl.BlockSpec((M_HEADS, M_CHUNKS_PER_BLOCK, 8, dh), lambda i: (0, rev(i), 0, 0)),
                  pl.BlockSpec((M_BLOCK, M_WIDTH), lambda i: (rev(i), 0))],
        out_specs=[pl.BlockSpec((M_BLOCK, 4 * M_WIDTH), lambda i: (rev(i), 0)),
                   pl.BlockSpec((M_BLOCK, LANES), lambda i: (rev(i), 0)),
                   pl.BlockSpec((8, LANES), lambda i: (0, 0))],
        out_shape=[jax.ShapeDtypeStruct((t, 4 * M_WIDTH), BF16), jax.ShapeDtypeStruct((t, LANES), F32),
                   jax.ShapeDtypeStruct((8, LANES), F32)],
        scratch_shapes=[pltpu.VMEM((M_HEADS, dh, dh), F32), pltpu.VMEM((M_HEADS, 8, dh), F32)],
        compiler_params=_cparams("arbitrary"),
    )(m4, gates, bias, norm_w, c_prev_all, nm_all, dhm)


def _rope_tables(pos_col, width):
    lane = lax.broadcasted_iota(jnp.int32, (1, width), 1) % A_QK_PAD
    half = A_ROPE // 2
    first = (lane >= A_NOPE) & (lane < A_NOPE + half)
    second = (lane >= A_NOPE + half) & (lane < A_NOPE + A_ROPE)
    idx = jnp.where(first, lane - A_NOPE, lane - A_NOPE - half).astype(F32)
    inv_freq = jnp.exp(idx * (-math.log(ROPE_THETA) / half))
    ang = pos_col.astype(F32) * inv_freq
    cos, sin = jnp.cos(ang), jnp.sin(ang)
    rot = first | second
    return jnp.where(rot, cos, 1.0), jnp.where(first, -sin, 0.0), jnp.where(second, sin, 0.0)


def _rope_apply(vv, cosf, s1, s2):
    half = A_ROPE // 2
    w = vv.shape[-1]
    return vv * cosf + pltpu.roll(vv, w - half, 1) * s1 + pltpu.roll(vv, half, 1) * s2


def _rope_apply_t(dd, cosf, s1, s2):
    half = A_ROPE // 2
    w = dd.shape[-1]
    return dd * cosf + pltpu.roll(dd * s1, half, 1) + pltpu.roll(dd * s2, w - half, 1)


A_BIAS_LANE_K = A_NOPE + A_ROPE
A_BIAS_LANE_V = A_V


def _hi_lo(val):
    hi = val.astype(BF16)
    return hi, (val - hi.astype(F32)).astype(BF16)


def _rope_fwd(qpre, kpre, vpre, gates, pos, name, tm=512):
    t, w = qpre.shape

    def body(q_ref, k_ref, v_ref, g_ref, p_ref, qo_ref, ko_ref, vo_ref):
        cosf, s1, s2 = _rope_tables(p_ref[...], LANES)
        lane = lax.broadcasted_iota(jnp.int32, (1, LANES), 1)
        kr = jnp.where((lane >= A_NOPE) & (lane < A_NOPE + A_ROPE), g_ref[...], 0.0)
        kr = _rope_apply(kr, cosf, s1, s2)
        kr = jnp.where((lane == A_BIAS_LANE_K) | (lane == A_BIAS_LANE_K + 1), 1.0, kr)
        v_one = (lane == A_BIAS_LANE_V) | (lane == A_BIAS_LANE_V + 1)
        for hd in range(A_HEADS):
            sl = slice(hd * LANES, (hd + 1) * LANES)
            qo_ref[:, sl] = (_rope_apply(q_ref[:, sl], cosf, s1, s2) * ATTN_SCALE).astype(qo_ref.dtype)
            ko_ref[:, sl] = (k_ref[:, sl] + kr).astype(ko_ref.dtype)
            vo_ref[:, sl] = jnp.where(v_one, jnp.ones((), BF16), v_ref[:, sl])

    row = pl.BlockSpec((tm, w), lambda i: (i, 0))
    out = jax.ShapeDtypeStruct((t, w), BF16)
    return pl.pallas_call(
        body, name=name, grid=(t // tm,),
        in_specs=[row, row, row, pl.BlockSpec((tm, LANES), lambda i: (i, 0)), pl.BlockSpec((tm, 1), lambda i: (i, 0))],
        out_specs=[row, row, row], out_shape=[out, out, out], compiler_params=_cparams("parallel"),
    )(qpre, kpre, vpre, gates, pos)


def _rope_bwd(dq, dk, dgates, pos, name, tm=512):
    t, w = dq.shape

    def body(dq_ref, dk_ref, dg_ref, p_ref, dqo_ref, dgo_ref):
        cosf, s1, s2 = _rope_tables(p_ref[...], LANES)
        acc = jnp.zeros((tm, LANES), F32)
        for hd in range(A_HEADS):
            sl = slice(hd * LANES, (hd + 1) * LANES)
            dqo_ref[:, sl] = (_rope_apply_t(dq_ref[:, sl], cosf, s1, s2) * ATTN_SCALE).astype(dqo_ref.dtype)
            acc = acc + dk_ref[:, sl].astype(F32)
        lane = lax.broadcasted_iota(jnp.int32, (1, LANES), 1)
        dkr = _rope_apply_t(acc, cosf, s1, s2)
        dkr = jnp.where((lane >= A_NOPE) & (lane < A_NOPE + A_ROPE), dkr, 0.0)
        dgo_ref[...] = (dg_ref[...] + dkr).astype(dgo_ref.dtype)

    row = pl.BlockSpec((tm, w), lambda i: (i, 0))
    nar = pl.BlockSpec((tm, LANES), lambda i: (i, 0))
    return pl.pallas_call(
        body, name=name, grid=(t // tm,),
        in_specs=[row, row, nar, pl.BlockSpec((tm, 1), lambda i: (i, 0))],
        out_specs=[row, nar],
        out_shape=[jax.ShapeDtypeStruct((t, w), BF16), jax.ShapeDtypeStruct((t, LANES), BF16)],
        compiler_params=_cparams("parallel"),
    )(dq, dk, dgates, pos)


A_TQ = 512
A_TK = 512


A_HEADS_PER_STEP = 2


def _flash_fwd(q, k, v, name):
    t = q.shape[0]
    tq, tk = A_TQ, A_TK
    nh = A_HEADS_PER_STEP
    wblk = nh * LANES

    def body(q_ref, k_ref, v_ref, o_ref, qb_ref):
        i = pl.program_id(1)
        lane = lax.broadcasted_iota(jnp.int32, (tq, LANES), 1)
        qpos = i * tq + lax.broadcasted_iota(jnp.int32, (tq, tk), 0)
        kofs = lax.broadcasted_iota(jnp.int32, (tq, tk), 1)
        qs = [q_ref[:, hh * LANES:(hh + 1) * LANES] for hh in range(nh)]

        def step(kb, carry, masked):
            rows = pl.ds(pl.multiple_of(kb * tk, tk), tk)
            new = []
            for hh in range(nh):
                m, acc = carry[hh]
                kh = k_ref[rows, hh * LANES:(hh + 1) * LANES]
                s = lax.dot_general(qs[hh], kh, _DIMS["nt"], preferred_element_type=F32)
                if masked:
                    s = jnp.where(kb * tk + kofs <= qpos, s, NEG_INF)
                m_new = jnp.maximum(m, jnp.max(s, axis=1, keepdims=True))
                p = jnp.exp(s - m_new)
                acc = jnp.exp(m - m_new) * acc + jnp.dot(p.astype(BF16), v_ref[rows, hh * LANES:(hh + 1) * LANES],
                                                          preferred_element_type=F32)
                new.append((m_new, acc))
            return tuple(new)

        carry = tuple((jnp.full((tq, 1), NEG_INF, F32), jnp.zeros((tq, LANES), F32)) for _ in range(nh))
        n_full = (i * tq) // tk
        carry = lax.fori_loop(0, n_full, functools.partial(step, masked=False), carry)
        for d in range(tq // tk):
            carry = step(n_full + d, carry, True)
        for hh in range(nh):
            m, acc = carry[hh]
            l = acc[:, A_BIAS_LANE_V:A_BIAS_LANE_V + 1]
            o_ref[:, hh * LANES:(hh + 1) * LANES] = (acc / l).astype(o_ref.dtype)
            hi, lo = _hi_lo(-(m + jnp.log(l)))
            qb_ref[:, hh * LANES:(hh + 1) * LANES] = jnp.where(
                lane == A_BIAS_LANE_K, hi, jnp.where(lane == A_BIAS_LANE_K + 1, lo, qs[hh]))

    blk = pl.BlockSpec((tq, wblk), lambda j, i: (i, j))
    res = pl.BlockSpec((t, wblk), lambda j, i: (0, j))
    out = jax.ShapeDtypeStruct((t, A_HEADS * LANES), BF16)
    return pl.pallas_call(
        body, name=name, grid=(A_HEADS // nh, t // tq), in_specs=[blk, res, res], out_specs=[blk, blk],
        out_shape=[out, out], compiler_params=_cparams("parallel", "parallel"),
    )(q, k, v)


def _attn_delta(do, o, name, tm=512):
    t, w = do.shape

    def body(do_ref, o_ref, out_ref):
        lane = lax.broadcasted_iota(jnp.int32, (tm, LANES), 1)
        for hd in range(A_HEADS):
            sl = slice(hd * LANES, (hd + 1) * LANES)
            d = do_ref[:, sl]
            delta = jnp.sum(jnp.where(lane < A_V, d.astype(F32) * o_ref[:, sl].astype(F32), 0.0), axis=1, keepdims=True)
            hi, lo = _hi_lo(-delta)
            out_ref[:, sl] = jnp.where(lane == A_BIAS_LANE_V, hi, jnp.where(lane == A_BIAS_LANE_V + 1, lo, d))

    row = pl.BlockSpec((tm, w), lambda i: (i, 0))
    return pl.pallas_call(
        body, name=name, grid=(t // tm,), in_specs=[row, row], out_specs=row,
        out_shape=jax.ShapeDtypeStruct((t, w), BF16), compiler_params=_cparams("parallel"),
    )(do, o)


def _flash_bwd(qb, k, v, doe, name):
    t = qb.shape[0]
    tq, tk = A_TQ, A_TK
    assert tq == tk
    nh = A_HEADS_PER_STEP
    wblk = nh * LANES
    nq = t // tq

    def body(q_ref, k_ref, v_ref, do_ref, dq_ref, dk_ref, dv_ref):
        kb = pl.program_id(1)

        @pl.when(kb == 0)
        def _():
            dq_ref[...] = jnp.zeros_like(dq_ref)

        kpos = kb * tk + lax.broadcasted_iota(jnp.int32, (tk, tq), 0)
        qofs = lax.broadcasted_iota(jnp.int32, (tk, tq), 1)
        ks = [k_ref[:, hh * LANES:(hh + 1) * LANES] for hh in range(nh)]
        vs = [v_ref[:, hh * LANES:(hh + 1) * LANES] for hh in range(nh)]

        def step(qi, carry, masked):
            rows = pl.ds(pl.multiple_of(qi * tq, tq), tq)
            new = []
            for hh in range(nh):
                dk_acc, dv_acc = carry[hh]
                qh = q_ref[rows, hh * LANES:(hh + 1) * LANES]
                doh = do_ref[rows, hh * LANES:(hh + 1) * LANES]
                p_t = jnp.exp(lax.dot_general(ks[hh], qh, _DIMS["nt"], preferred_element_type=F32))
                if masked:
                    p_t = jnp.where(qi * tq + qofs >= kpos, p_t, 0.0)
                ds_t = (p_t * lax.dot_general(vs[hh], doh, _DIMS["nt"], preferred_element_type=F32)).astype(BF16)
                dv_acc = dv_acc + jnp.dot(p_t.astype(BF16), doh, preferred_element_type=F32)
                dk_acc = dk_acc + jnp.dot(ds_t, qh, preferred_element_type=F32)
                dq_ref[rows, hh * LANES:(hh + 1) * LANES] += lax.dot_general(ds_t, ks[hh], _DIMS["tn"],
                                                                             preferred_element_type=F32)
                new.append((dk_acc, dv_acc))
            return tuple(new)

        carry = tuple((jnp.zeros((tk, LANES), F32), jnp.zeros((tk, LANES), F32)) for _ in range(nh))
        carry = step(kb, carry, True)
        carry = lax.fori_loop(kb + 1, nq, functools.partial(step, masked=False), carry)
        for hh in range(nh):
            dk_ref[:, hh * LANES:(hh + 1) * LANES] = carry[hh][0].astype(dk_ref.dtype)
            dv_ref[:, hh * LANES:(hh + 1) * LANES] = carry[hh][1].astype(dv_ref.dtype)

    res = pl.BlockSpec((t, wblk), lambda j, kb: (0, j))
    blk = pl.BlockSpec((tk, wblk), lambda j, kb: (kb, j))
    wide = (t, A_HEADS * LANES)
    return pl.pallas_call(
        body, name=name, grid=(A_HEADS // nh, t // tk), in_specs=[res, blk, blk, res], out_specs=[res, blk, blk],
        out_shape=[jax.ShapeDtypeStruct(wide, F32), jax.ShapeDtypeStruct(wide, BF16), jax.ShapeDtypeStruct(wide, BF16)],
        compiler_params=_cparams("parallel", "arbitrary"),
    )(qb, k, v, doe)


_SPLITS = (M_WIDTH, M_WIDTH, M_WIDTH, M_WIDTH, M_HEADS, M_HEADS, Q_RANK, KV_RANK, A_ROPE)
_OFFS = tuple(sum(_SPLITS[:i]) for i in range(len(_SPLITS) + 1))
_GATE_BLOCK_KR = A_NOPE


def _split_w_in(w_in):
    d = w_in.shape[0]
    w_m4 = w_in[:, :_OFFS[4]]
    w_a2 = w_in[:, _OFFS[6]:_OFFS[8]]
    z = lambda n: jnp.zeros((d, n), w_in.dtype)
    w_g = jnp.concatenate([w_in[:, _OFFS[4]:_OFFS[6]], z(_GATE_BLOCK_KR - 2 * M_HEADS), w_in[:, _OFFS[8]:_OFFS[9]],
                           z(LANES - _GATE_BLOCK_KR - A_ROPE)], axis=1)
    return w_m4, w_a2, w_g


def _merge_w_in_grad(g_m4, g_a2, g_g):
    return jnp.concatenate([g_m4, g_g[:, :2 * M_HEADS], g_a2, g_g[:, _GATE_BLOCK_KR:_GATE_BLOCK_KR + A_ROPE]], axis=1)


def _pad_w_q_b(w):
    r = w.shape[0]
    w3 = w.reshape(r, A_HEADS, A_NOPE + A_ROPE)
    return jnp.pad(w3, ((0, 0), (0, 0), (0, A_QK_PAD - A_NOPE - A_ROPE))).reshape(r, A_HEADS * A_QK_PAD)


def _unpad_w_q_b_grad(g):
    r = g.shape[0]
    return g.reshape(r, A_HEADS, A_QK_PAD)[:, :, :A_NOPE + A_ROPE].reshape(r, A_HEADS * (A_NOPE + A_ROPE))


def _split_w_kv_b(w):
    r = w.shape[0]
    w3 = w.reshape(r, A_HEADS, A_NOPE + A_V)
    pad = lambda part: jnp.pad(part, ((0, 0), (0, 0), (0, LANES - part.shape[2]))).reshape(r, A_HEADS * LANES)
    return pad(w3[:, :, :A_NOPE]), pad(w3[:, :, A_NOPE:])


def _merge_w_kv_b_grad(gk, gv):
    r = gk.shape[0]
    gk3 = gk.reshape(r, A_HEADS, LANES)[:, :, :A_NOPE]
    gv3 = gv.reshape(r, A_HEADS, LANES)[:, :, :A_V]
    return jnp.concatenate([gk3, gv3], axis=2).reshape(r, A_HEADS * (A_NOPE + A_V))


def _pad_head_rows(w):
    w3 = w.reshape(A_HEADS, A_V, w.shape[1])
    return jnp.pad(w3, ((0, 0), (0, LANES - A_V), (0, 0))).reshape(A_HEADS * LANES, w.shape[1])


def _unpad_head_rows(g):
    return g.reshape(A_HEADS, LANES, g.shape[1])[:, :A_V].reshape(A_HEADS * A_V, g.shape[1])


def _local_step(x, pos, target, nw, w):
    w_m4, w_a2, w_g = _split_w_in(w["w_in"])
    w_qb = _pad_w_q_b(w["w_q_b"])
    w_k, w_v = _split_w_kv_b(w["w_kv_b"])
    w_out_m, w_out_a = w["w_out"][:M_WIDTH], _pad_head_rows(w["w_out"][M_WIDTH:])
    bias = jnp.pad(nw["b_gates"], ((0, 0), (0, LANES - 2 * M_HEADS)))
    mnorm = nw["mlstm_norm_w"].reshape(M_HEADS, M_HEAD_DIM)

    u1 = _rmsnorm_fwd(x, nw["attn_norm_w"], "attn_norm")
    m4 = _matmul(u1, w_m4, "nn", BF16, "proj_mlstm")
    a2 = _matmul(u1, w_a2, "nn", BF16, "proj_latents", tn=640)
    gates = _matmul(u1, w_g, "nn", F32, "proj_gates")
    hm, c_prev_all, nm_all = _mlstm_fwd(m4, gates, bias, mnorm, "mlstm_fwd")
    qa, kva = a2[:, :Q_RANK], a2[:, Q_RANK:]
    qa_n = _rmsnorm_fwd(qa, nw["q_a_norm_w"], "q_a_norm")
    kv_n = _rmsnorm_fwd(kva, nw["kv_a_norm_w"], "kv_a_norm")
    qpre = _matmul(qa_n, w_qb, "nn", F32, "q_up")
    kpre = _matmul(kv_n, w_k, "nn", F32, "k_up")
    vpre = _matmul(kv_n, w_v, "nn", BF16, "v_up")
    qr, kr, vv = _rope_fwd(qpre, kpre, vpre, gates, pos, "rope_fwd")
    ha, qb = _flash_fwd(qr, kr, vv, "attn_fwd")
    h1 = _matmul(hm, w_out_m, "nn", F32, "out_proj_m", residual=x)
    h1 = _matmul(ha, w_out_a, "nn", F32, "out_proj_a", residual=h1)
    u2 = _rmsnorm_fwd(h1, nw["ffn_norm_w"], "ffn_norm")
    gg = _matmul(u2, w["w_gate"], "nn", BF16, "ffn_gate", tn=1408)
    uu = _matmul(u2, w["w_up"], "nn", BF16, "ffn_up", tn=1408)
    act = _swiglu_fwd(gg, uu, "swiglu_fwd")
    h2 = _matmul(act, w["w_down"], "nn", F32, "ffn_down", residual=h1)

    dh2, loss, g_final = _final_loss(h2, target, nw["final_norm_w"].reshape(1, D_MODEL), "final_loss")
    grads = {"final_norm_w": g_final.reshape(D_MODEL)}
    dact = _matmul(dh2, w["w_down"], "nt", BF16, "d_act", tn=1408)
    grads["w_down"] = _matmul(act, dh2, "tn", F32, "g_w_down", tm=256)
    dgg, duu = _swiglu_bwd(gg, uu, dact, "swiglu_bwd")
    du2 = _matmul(dgg, w["w_gate"], "nt", F32, "d_u2_gate")
    du2 = _matmul(duu, w["w_up"], "nt", F32, "d_u2_up", residual=du2)
    grads["w_gate"] = _matmul(u2, dgg, "tn", F32, "g_w_gate", tn=1408)
    grads["w_up"] = _matmul(u2, duu, "tn", F32, "g_w_up", tn=1408)
    dh1, grads["ffn_norm_w"] = _rmsnorm_bwd(du2, h1, nw["ffn_norm_w"], "ffn_norm_bwd", F32, residual=dh2)
    dhm = _matmul(dh1, w_out_m, "nt", BF16, "d_hm")
    dha = _matmul(dh1, w_out_a, "nt", BF16, "d_ha")
    grads["w_out"] = jnp.concatenate([_matmul(hm, dh1, "tn", F32, "g_w_out_m"),
                                      _unpad_head_rows(_matmul(ha, dh1, "tn", F32, "g_w_out_a"))], axis=0)

    dqr, dkr, dvv = _flash_bwd(qb, kr, vv, _attn_delta(dha, ha, "attn_delta"), "attn_bwd")
    dm4, dgates, small = _mlstm_bwd(m4, gates, bias, mnorm, c_prev_all, nm_all, dhm, "mlstm_bwd")
    grads["mlstm_norm_w"] = small[:M_HEADS].reshape(1, M_HEADS, M_HEAD_DIM)
    grads["b_gates"] = small[M_HEADS:M_HEADS + 1, :2 * M_HEADS]
    dqpre, dgk = _rope_bwd(dqr, dkr, dgates, pos, "rope_bwd")
    dqa_n = _matmul(dqpre, w_qb, "nt", BF16, "d_qa_n")
    grads["w_q_b"] = _unpad_w_q_b_grad(_matmul(qa_n, dqpre, "tn", F32, "g_w_q_b", tm=384))
    dkv_n = _matmul(dkr, w_k, "nt", F32, "d_kv_n_k")
    dkv_n = _matmul(dvv, w_v, "nt", BF16, "d_kv_n_v", residual=dkv_n)
    grads["w_kv_b"] = _merge_w_kv_b_grad(_matmul(kv_n, dkr, "tn", F32, "g_w_k", tm=256),
                                         _matmul(kv_n, dvv, "tn", F32, "g_w_v", tm=256))
    dqa, grads["q_a_norm_w"] = _rmsnorm_bwd(dqa_n, qa, nw["q_a_norm_w"], "q_a_norm_bwd", BF16)
    dkva, grads["kv_a_norm_w"] = _rmsnorm_bwd(dkv_n, kva, nw["kv_a_norm_w"], "kv_a_norm_bwd", BF16)
    da2 = jnp.concatenate([dqa, dkva], axis=1)
    du1 = _matmul(dm4, w_m4, "nt", F32, "d_u1_m4")
    du1 = _matmul(da2, w_a2, "nt", F32, "d_u1_a2", residual=du1)
    du1 = _matmul(dgk, w_g, "nt", F32, "d_u1_g", residual=du1)
    grads["w_in"] = _merge_w_in_grad(_matmul(u1, dm4, "tn", F32, "g_w_m4"),
                                     _matmul(u1, da2, "tn", F32, "g_w_a2", tn=640),
                                     _matmul(u1, dgk, "tn", F32, "g_w_g"))
    grad_x, grads["attn_norm_w"] = _rmsnorm_bwd(du1, x, nw["attn_norm_w"], "attn_norm_bwd", F32, residual=dh1)
    return loss, grad_x, grads


MESH = pl.DeviceIdType.MESH
N_CHIPS = 4
BIG = ("w_in", "w_q_b", "w_kv_b", "w_out", "w_gate", "w_up", "w_down")
ROW_SHARDED = ("w_out", "w_down")
SMALL = ("attn_norm_w", "b_gates", "mlstm_norm_w", "q_a_norm_w", "kv_a_norm_w", "ffn_norm_w", "final_norm_w")
HBM_SPEC = pl.BlockSpec(memory_space=pltpu.HBM)


def _place():
    x, y, c = lax.axis_index("x"), lax.axis_index("y"), lax.axis_index("c")
    others = [(1 - x, y), (x, 1 - y), (1 - x, 1 - y)]
    return x, y, c, others


def _gather_weights(shards):
    n = len(shards)

    def body(*refs):
        ins, outs = refs[:n], refs[n:2 * n]
        send_sems, recv_sems = refs[2 * n:]
        x, y, c, others = _place()
        me = 2 * x + y
        sibling = (x, y, 1 - c)

        def half_rows(w, core):
            rh = ins[w].shape[0] // 2
            return pl.ds(core * rh, rh)

        def copy(w, k, slab, core, to, src=None):
            dst = outs[w].at[slab, half_rows(w, core)]
            return pltpu.make_async_remote_copy(
                src_ref=dst if src is None else src, dst_ref=dst, send_sem=send_sems.at[w * 6 + k],
                recv_sem=recv_sems.at[w * 6 + k], device_id=to, device_id_type=MESH)

        sends = []
        for w in range(n):
            for j, chip in enumerate(others):
                cp = copy(w, j, me, c, (*chip, c), src=ins[w].at[half_rows(w, c)])
                cp.start()
                sends.append(cp)
        for w in range(n):
            for j, (ox, oy) in enumerate(others):
                slab = 2 * ox + oy
                copy(w, j, slab, c, (x, y, c)).wait_recv()
                fwd = copy(w, 3 + j, slab, c, sibling)
                fwd.start()
                sends.append(fwd)
        for w in range(n):
            for j, (ox, oy) in enumerate(others):
                copy(w, 3 + j, 2 * ox + oy, 1 - c, (x, y, c)).wait_recv()
        for cp in sends:
            cp.wait_send()

    gathered = pl.pallas_call(
        body, name="gather_weights", in_specs=[HBM_SPEC] * n, out_specs=[HBM_SPEC] * n,
        out_shape=[jax.ShapeDtypeStruct((N_CHIPS,) + s.shape, s.dtype) for s in shards],
        scratch_shapes=[pltpu.SemaphoreType.DMA((6 * n,)), pltpu.SemaphoreType.DMA((6 * n,))],
    )(*shards)
    me = 2 * lax.axis_index("x") + lax.axis_index("y")
    return [lax.dynamic_update_slice(g, s[None], (me, 0, 0)) for g, s in zip(gathered, shards)]


def _exchange(arrays, out_shapes, plan, copies_per_array, name):
    n = len(arrays)

    def body(*refs):
        ins, outs = refs[:n], refs[n:2 * n]
        send_sems, recv_sems = refs[2 * n:]
        rem = [pltpu.make_async_remote_copy(src_ref=s, dst_ref=d, send_sem=send_sems.at[k], recv_sem=recv_sems.at[k],
                                            device_id=to, device_id_type=MESH)
               for k, (s, d, to) in enumerate(plan(ins, outs, _place()))]
        for cp in rem:
            cp.start()
        for cp in rem:
            cp.wait_recv()
        for cp in rem:
            cp.wait_send()

    return pl.pallas_call(
        body, name=name, in_specs=[HBM_SPEC] * n, out_specs=[HBM_SPEC] * n, out_shape=out_shapes,
        scratch_shapes=[pltpu.SemaphoreType.DMA((copies_per_array * n,)),
                        pltpu.SemaphoreType.DMA((copies_per_array * n,))],
    )(*arrays)


def _plan_to_sibling(ins, outs, place):
    x, y, c, _ = place
    return [(ins[w], outs[w], (x, y, 1 - c)) for w in range(len(ins))]


def _plan_chip_partials(ins, outs, place):
    x, y, c, others = place
    return [(ins[w].at[2 * ox + oy], outs[w].at[j], (ox, oy, c))
            for w in range(len(ins)) for j, (ox, oy) in enumerate(others)]


def _row_tile(rows, limit=512):
    for tr in range(min(limit, rows), 0, -1):
        if rows % tr == 0 and tr % 16 == 0:
            return tr
    raise ValueError(rows)


def _elementwise(fn, ins, out_dtypes, name):
    shape = ins[0].shape
    c = shape[-1]
    rows = math.prod(shape[:-1])
    tr = _row_tile(rows)
    ins2 = [a.reshape(rows, c) for a in ins]
    n_in = len(ins)

    def body(*refs):
        res = fn(*[r[...] for r in refs[:n_in]])
        for r, val in zip(refs[n_in:], res):
            r[...] = val.astype(r.dtype)

    blk = pl.BlockSpec((tr, c), lambda i: (i, 0))
    outs = pl.pallas_call(
        body, name=name, grid=(rows // tr,), in_specs=[blk] * n_in, out_specs=[blk] * len(out_dtypes),
        out_shape=[jax.ShapeDtypeStruct((rows, c), d) for d in out_dtypes], compiler_params=_cparams("parallel"),
    )(*ins2)
    return [o.reshape(shape) for o in outs]


def _adamw_math(w, g, m, v):
    m = ADAM_B1 * m + (1.0 - ADAM_B1) * g
    v = ADAM_B2 * v + (1.0 - ADAM_B2) * (g * g)
    m_hat = m / (1.0 - ADAM_B1 ** ADAM_STEP)
    v_hat = v / (1.0 - ADAM_B2 ** ADAM_STEP)
    delta = -ADAM_LR * (m_hat / (jnp.sqrt(v_hat) + ADAM_EPS) + ADAM_WD * w)
    return delta, m, v


def _to_shard_halves(g, row_sharded):
    if row_sharded:
        rs = g.shape[0] // N_CHIPS
        return g.reshape(N_CHIPS, 2, rs // 2, g.shape[1]).transpose(1, 0, 2, 3)
    cs = g.shape[1] // N_CHIPS
    return g.reshape(2, g.shape[0] // 2, N_CHIPS, cs).transpose(0, 2, 1, 3)


def _reduce_big_grads(grads):
    c = lax.axis_index("c")
    me = 2 * lax.axis_index("x") + lax.axis_index("y")
    halves = [_to_shard_halves(grads[n], n in ROW_SHARDED) for n in BIG]
    keep = [lax.dynamic_index_in_dim(h, c, 0, keepdims=False) for h in halves]
    send = [lax.dynamic_index_in_dim(h, 1 - c, 0, keepdims=False).astype(BF16) for h in halves]
    got = _exchange(send, [jax.ShapeDtypeStruct(s.shape, BF16) for s in send], _plan_to_sibling, 1, "sibling_halves")
    part = [_elementwise(lambda a, b: (a + b.astype(F32),), [k, r], [BF16], "chip_partial_%s" % n)[0]
            for n, k, r in zip(BIG, keep, got)]
    got3 = _exchange(part, [jax.ShapeDtypeStruct((3,) + p.shape[1:], BF16) for p in part], _plan_chip_partials, 3,
                     "chip_partials")
    final = []
    for n, k, r, g3 in zip(BIG, keep, got, got3):
        own_k = lax.dynamic_index_in_dim(k, me, 0, keepdims=False)
        own_r = lax.dynamic_index_in_dim(r, me, 0, keepdims=False)
        final.append(_elementwise(
            lambda a, b, p0, p1, p2: (a + b.astype(F32) + p0.astype(F32) + p1.astype(F32) + p2.astype(F32),),
            [own_k, own_r, g3[0], g3[1], g3[2]], [F32], "shard_sum_%s" % n)[0])
    other = _exchange(final, [jax.ShapeDtypeStruct(f.shape, F32) for f in final], _plan_to_sibling, 1,
                      "sibling_result")
    return {n: lax.select(c == 0, jnp.concatenate([f, o], axis=0), jnp.concatenate([o, f], axis=0))
            for n, f, o in zip(BIG, final, other)}


SMALL_ROWS = 8
SMALL_LAYOUT = {"attn_norm_w": (0, 0, 1024), "ffn_norm_w": (1, 0, 1024), "final_norm_w": (2, 0, 1024),
                "q_a_norm_w": (3, 0, 384), "kv_a_norm_w": (3, 384, 256), "mlstm_norm_w": (4, 0, 512),
                "b_gates": (4, 512, 8)}
LOSS_SLOT = (5, 0)


def _pack_small(vals, loss=None):
    tile = jnp.zeros((SMALL_ROWS, D_MODEL), F32)
    for n, (r, c0, width) in SMALL_LAYOUT.items():
        tile = tile.at[r, c0:c0 + width].set(vals[n].reshape(width).astype(F32))
    if loss is not None:
        tile = tile.at[LOSS_SLOT[0], LOSS_SLOT[1]].set(loss)
    return tile


def _unpack_small(tile, shapes):
    return {n: tile[r, c0:c0 + width].reshape(shapes[n]) for n, (r, c0, width) in SMALL_LAYOUT.items()}


def _small_allreduce_adamw(g_tile, w_tile, m_tile, v_tile):
    def body(g_ref, w_ref, m_ref, v_ref, gsum_ref, d_ref, mo_ref, vo_ref, slots, send_sems, recv_sems):
        x, y, c, _ = _place()
        me = 4 * x + 2 * y + c
        slots[me] = g_ref[...]
        copies = []
        for k in range(1, 8):
            dx, dy, dc = (k >> 2) & 1, (k >> 1) & 1, k & 1
            to = (x ^ dx, y ^ dy, c ^ dc)
            cp = pltpu.make_async_remote_copy(src_ref=g_ref, dst_ref=slots.at[me], send_sem=send_sems.at[k - 1],
                                              recv_sem=recv_sems.at[k - 1], device_id=to, device_id_type=MESH)
            cp.start()
            copies.append(cp)
        for k in range(1, 8):
            src = me ^ k
            pltpu.make_async_remote_copy(src_ref=g_ref, dst_ref=slots.at[src], send_sem=send_sems.at[k - 1],
                                         recv_sem=recv_sems.at[k - 1], device_id=(x, y, c),
                                         device_id_type=MESH).wait_recv()
        for cp in copies:
            cp.wait_send()
        total = slots[0]
        for d in range(1, 8):
            total = total + slots[d]
        gsum_ref[...] = total
        delta, m_new, v_new = _adamw_math(w_ref[...], total, m_ref[...], v_ref[...])
        d_ref[...] = delta
        mo_ref[...] = m_new
        vo_ref[...] = v_new

    vm = pl.BlockSpec(memory_space=pltpu.VMEM)
    tile = jax.ShapeDtypeStruct((SMALL_ROWS, D_MODEL), F32)
    return pl.pallas_call(
        body, name="small_allreduce_adamw", in_specs=[vm] * 4, out_specs=[vm] * 4, out_shape=[tile] * 4,
        scratch_shapes=[pltpu.VMEM((8, SMALL_ROWS, D_MODEL), F32), pltpu.SemaphoreType.DMA((7,)),
                        pltpu.SemaphoreType.DMA((7,))],
    )(g_tile, w_tile, m_tile, v_tile)


def kernel(x, positions, attn_norm_w, w_in, b_gates, mlstm_norm_w, q_a_norm_w, w_q_b, kv_a_norm_w, w_kv_b, w_out, ffn_norm_w, w_gate, w_up, w_down, final_norm_w, loss_target, m_attn_norm_w, m_w_in, m_b_gates, m_mlstm_norm_w, m_q_a_norm_w, m_w_q_b, m_kv_a_norm_w, m_w_kv_b, m_w_out, m_ffn_norm_w, m_w_gate, m_w_up, m_w_down, m_final_norm_w, v_attn_norm_w, v_w_in, v_b_gates, v_mlstm_norm_w, v_q_a_norm_w, v_w_q_b, v_kv_a_norm_w, v_w_kv_b, v_w_out, v_ffn_norm_w, v_w_gate, v_w_up, v_w_down, v_final_norm_w):
    names = ("attn_norm_w", "w_in", "b_gates", "mlstm_norm_w", "q_a_norm_w", "w_q_b", "kv_a_norm_w", "w_kv_b", "w_out",
             "ffn_norm_w", "w_gate", "w_up", "w_down", "final_norm_w")
    wts = dict(zip(names, (attn_norm_w, w_in, b_gates, mlstm_norm_w, q_a_norm_w, w_q_b, kv_a_norm_w, w_kv_b, w_out,
                           ffn_norm_w, w_gate, w_up, w_down, final_norm_w)))
    mom = dict(zip(names, (m_attn_norm_w, m_w_in, m_b_gates, m_mlstm_norm_w, m_q_a_norm_w, m_w_q_b, m_kv_a_norm_w,
                           m_w_kv_b, m_w_out, m_ffn_norm_w, m_w_gate, m_w_up, m_w_down, m_final_norm_w)))
    vel = dict(zip(names, (v_attn_norm_w, v_w_in, v_b_gates, v_mlstm_norm_w, v_q_a_norm_w, v_w_q_b, v_kv_a_norm_w,
                           v_w_kv_b, v_w_out, v_ffn_norm_w, v_w_gate, v_w_up, v_w_down, v_final_norm_w)))
    t = x.shape[1]

    gathered = _gather_weights([wts[n][0].astype(BF16) for n in BIG])
    full = {}
    for n, g in zip(BIG, gathered):
        full[n] = g.reshape(N_CHIPS * g.shape[1], g.shape[2]) if n in ROW_SHARDED else jnp.concatenate(list(g), axis=1)

    nw = {n: wts[n] for n in SMALL}
    loss, grad_x, grads = _local_step(x[0], positions.reshape(t, 1), loss_target[0], nw, full)

    big = _reduce_big_grads(grads)
    outs_g, outs_d, outs_m, outs_v = {}, {}, {}, {}
    for n in BIG:
        d, m_new, v_new = _elementwise(_adamw_math, [wts[n][0], big[n], mom[n][0], vel[n][0]], [F32, F32, F32],
                                       "adamw_%s" % n)
        outs_g[n], outs_d[n], outs_m[n], outs_v[n] = big[n][None], d[None], m_new[None], v_new[None]
    shapes = {n: wts[n].shape for n in SMALL}
    g_tile = _pack_small({n: grads[n] for n in SMALL}, loss=loss[0, 0])
    gsum, d_tile, m_tile, v_tile = _small_allreduce_adamw(g_tile, _pack_small(wts), _pack_small(mom), _pack_small(vel))
    outs_g.update(_unpack_small(gsum, shapes))
    outs_d.update(_unpack_small(d_tile, shapes))
    outs_m.update(_unpack_small(m_tile, shapes))
    outs_v.update(_unpack_small(v_tile, shapes))
    total_loss = gsum[LOSS_SLOT[0], LOSS_SLOT[1]]
    return (total_loss, grad_x[None], *[outs_g[n] for n in names], *[outs_d[n] for n in names],
            *[outs_m[n] for n in names], *[outs_v[n] for n in names])
```

```python
import functools
import math

import jax
import jax.numpy as jnp
from jax import lax
from jax.experimental import pallas as pl
from jax.experimental.pallas import tpu as pltpu

F32 = jnp.float32
BF16 = jnp.bfloat16

D_MODEL = 1024
M_HEADS = 4
M_HEAD_DIM = 128
M_WIDTH = M_HEADS * M_HEAD_DIM
M_CHUNK = 64
A_HEADS = 8
A_NOPE = 64
A_ROPE = 32
A_V = 64
A_WIDTH = A_HEADS * A_V
A_QK_PAD = 128
Q_RANK = 384
KV_RANK = 256
ROPE_THETA = 10000.0
D_FF = 2816
D_IN = 2728
EPS = 1e-6
ATTN_SCALE = (A_NOPE + A_ROPE) ** -0.5
M_SCALE = M_HEAD_DIM ** -0.5

ADAM_LR = 0.001
ADAM_B1 = 0.9
ADAM_B2 = 0.999
ADAM_EPS = 1e-08
ADAM_WD = 0.01
ADAM_STEP = 10

VMEM_LIMIT_BYTES = 56 * 1024 * 1024
LANES = 128
NEG_INF = float("-inf")


def _cparams(*sem):
    return pltpu.CompilerParams(dimension_semantics=sem if sem else None, vmem_limit_bytes=VMEM_LIMIT_BYTES)


_DIMS = {"nn": (((1,), (0,)), ((), ())), "nt": (((1,), (1,)), ((), ())), "tn": (((0,), (0,)), ((), ()))}


def _matmul(a, b, mode, out_dtype, name, tm=512, tn=512, residual=None, b_rows=None, lead_scale=None):
    b_shape = b.shape if b_rows is None else (b_rows, b.shape[1])
    if mode == "nn":
        (m, k), (k2, n) = a.shape, b_shape
    elif mode == "nt":
        (m, k), (n, k2) = a.shape, b_shape
    else:
        (k, m), (k2, n) = a.shape, b_shape
    assert k == k2, (a.shape, b.shape, mode)
    tm, tn = min(tm, m), min(tn, n)
    assert m % tm == 0 and n % tn == 0, (m, n, tm, tn)
    a_spec = pl.BlockSpec((k, tm), lambda i, j: (0, i)) if mode == "tn" else pl.BlockSpec((tm, k), lambda i, j: (i, 0))
    b_spec = pl.BlockSpec((tn, k), lambda i, j: (j, 0)) if mode == "nt" else pl.BlockSpec((k, tn), lambda i, j: (0, j))
    o_spec = pl.BlockSpec((tm, tn), lambda i, j: (i, j))
    dims = _DIMS[mode]

    def body(a_ref, b_ref, *rest):
        o_ref = rest[-1]
        acc = lax.dot_general(a_ref[...].astype(BF16), b_ref[...].astype(BF16), dims, preferred_element_type=F32)
        if lead_scale is not None:
            assert lead_scale[0] % tn == 0
            acc = acc * jnp.where(pl.program_id(1) < lead_scale[0] // tn, lead_scale[1], 1.0)
        if residual is not None:
            acc = acc + rest[0][...].astype(F32)
        o_ref[...] = acc.astype(o_ref.dtype)

    ins = [a, b] + ([residual] if residual is not None else [])
    in_specs = [a_spec, b_spec] + ([o_spec] if residual is not None else [])
    return pl.pallas_call(
        body, name=name, grid=(m // tm, n // tn), in_specs=in_specs, out_specs=o_spec,
        out_shape=jax.ShapeDtypeStruct((m, n), out_dtype), compiler_params=_cparams("parallel", "parallel"),
    )(*ins)


def _rmsnorm_fwd(x, w, name, tm=512):
    t, d = x.shape

    def body(x_ref, w_ref, o_ref):
        xf = x_ref[...].astype(F32)
        r = lax.rsqrt(jnp.mean(xf * xf, axis=-1, keepdims=True) + EPS)
        o_ref[...] = (xf * r * w_ref[...]).astype(o_ref.dtype)

    return pl.pallas_call(
        body, name=name, grid=(t // tm,),
        in_specs=[pl.BlockSpec((tm, d), lambda i: (i, 0)), pl.BlockSpec((1, d), lambda i: (0, 0))],
        out_specs=pl.BlockSpec((tm, d), lambda i: (i, 0)),
        out_shape=jax.ShapeDtypeStruct((t, d), BF16), compiler_params=_cparams("parallel"),
    )(x, w)


def _rmsnorm_bwd(dy, x, w, name, out_dtype, residual=None, also_bf16=False, tm=512):
    t, d = x.shape

    def body(dy_ref, x_ref, w_ref, *rest):
        dx_ref, dw_ref = rest[-3 if also_bf16 else -2], rest[-1]
        xf = x_ref[...].astype(F32)
        r = lax.rsqrt(jnp.mean(xf * xf, axis=-1, keepdims=True) + EPS)
        xh = xf * r
        dyf = dy_ref[...].astype(F32)
        dyh = dyf * w_ref[...]
        dx = r * (dyh - xh * jnp.mean(dyh * xh, axis=-1, keepdims=True))
        if residual is not None:
            dx = dx + rest[0][...].astype(F32)
        dx_ref[...] = dx.astype(dx_ref.dtype)
        if also_bf16:
            rest[-2][...] = dx.astype(BF16)
        part = jnp.sum(dyf * xh, axis=0, keepdims=True)

        @pl.when(pl.program_id(0) == 0)
        def _():
            dw_ref[...] = part

        @pl.when(pl.program_id(0) > 0)
        def _():
            dw_ref[...] += part

    row = pl.BlockSpec((tm, d), lambda i: (i, 0))
    vec = pl.BlockSpec((1, d), lambda i: (0, 0))
    ins = [dy, x, w] + ([residual] if residual is not None else [])
    extra = [jax.ShapeDtypeStruct((t, d), BF16)] if also_bf16 else []
    return pl.pallas_call(
        body, name=name, grid=(t // tm,), in_specs=[row, row, vec] + ([row] if residual is not None else []),
        out_specs=[row] + [row] * len(extra) + [vec],
        out_shape=[jax.ShapeDtypeStruct((t, d), out_dtype)] + extra + [jax.ShapeDtypeStruct((1, d), F32)],
        compiler_params=_cparams("arbitrary"),
    )(*ins)


def _final_loss(h, target, w, name, tm=512):
    t, d = h.shape

    def body(h_ref, t_ref, w_ref, dh_ref, dhb_ref, loss_ref, dw_ref):
        xf = h_ref[...]
        r = lax.rsqrt(jnp.mean(xf * xf, axis=-1, keepdims=True) + EPS)
        xh = xf * r
        err = xh * w_ref[...] - t_ref[...]
        part_loss = 0.5 * jnp.sum(jnp.sum(err * err, axis=-1, keepdims=True), axis=0, keepdims=True) * (1.0 / d)
        dy = err * (1.0 / d)
        dyh = dy * w_ref[...]
        dh = r * (dyh - xh * jnp.mean(dyh * xh, axis=-1, keepdims=True))
        dh_ref[...] = dh
        dhb_ref[...] = dh.astype(BF16)
        part_dw = jnp.sum(dy * xh, axis=0, keepdims=True)
        part_loss = jnp.broadcast_to(part_loss, (1, LANES))

        @pl.when(pl.program_id(0) == 0)
        def _():
            dw_ref[...] = part_dw
            loss_ref[...] = part_loss

        @pl.when(pl.program_id(0) > 0)
        def _():
            dw_ref[...] += part_dw
            loss_ref[...] += part_loss

    row = pl.BlockSpec((tm, d), lambda i: (i, 0))
    vec = pl.BlockSpec((1, d), lambda i: (0, 0))
    return pl.pallas_call(
        body, name=name, grid=(t // tm,), in_specs=[row, row, vec],
        out_specs=[row, row, pl.BlockSpec((1, LANES), lambda i: (0, 0)), vec],
        out_shape=[jax.ShapeDtypeStruct((t, d), F32), jax.ShapeDtypeStruct((t, d), BF16),
                   jax.ShapeDtypeStruct((1, LANES), F32), jax.ShapeDtypeStruct((1, d), F32)],
        compiler_params=_cparams("arbitrary"),
    )(h, target, w)


def _swiglu_fwd(g, u, name, tm=512, tn=1408):
    t, n = g.shape

    def body(g_ref, u_ref, o_ref):
        gf = g_ref[...].astype(F32)
        o_ref[...] = (gf * jax.nn.sigmoid(gf) * u_ref[...].astype(F32)).astype(o_ref.dtype)

    blk = pl.BlockSpec((tm, tn), lambda i, j: (i, j))
    return pl.pallas_call(
        body, name=name, grid=(t // tm, n // tn), in_specs=[blk, blk], out_specs=blk,
        out_shape=jax.ShapeDtypeStruct((t, n), BF16), compiler_params=_cparams("parallel", "parallel"),
    )(g, u)


def _swiglu_bwd(g, u, dact, name, tm=512, tn=1408):
    t, n = g.shape

    def body(g_ref, u_ref, d_ref, dg_ref, du_ref):
        gf = g_ref[...].astype(F32)
        uf = u_ref[...].astype(F32)
        df = d_ref[...].astype(F32)
        s = jax.nn.sigmoid(gf)
        dg_ref[...] = (df * uf * s * (1.0 + gf * (1.0 - s))).astype(dg_ref.dtype)
        du_ref[...] = (df * gf * s).astype(du_ref.dtype)

    blk = pl.BlockSpec((tm, tn), lambda i, j: (i, j))
    out = jax.ShapeDtypeStruct((t, n), BF16)
    return pl.pallas_call(
        body, name=name, grid=(t // tm, n // tn), in_specs=[blk, blk, blk], out_specs=[blk, blk],
        out_shape=[out, out], compiler_params=_cparams("parallel", "parallel"),
    )(g, u, dact)


M_BLOCK = 512
M_CHUNKS_PER_BLOCK = M_BLOCK // M_CHUNK


def _mlstm_chunk_fwd(q, k, v, i_col, f_col, c_prev, n_prev, m_prev):
    L = M_CHUNK
    tt = lax.broadcasted_iota(jnp.int32, (L, L), 0)
    ss = lax.broadcasted_iota(jnp.int32, (L, L), 1)
    eye = tt == ss
    causal = ss <= tt

    def to_row(col):
        return jnp.sum(jnp.where(eye, col, 0.0), axis=0, keepdims=True)

    lf_col = jnp.minimum(f_col, 0.0) - jnp.log(1.0 + jnp.exp(-jnp.abs(f_col)))
    lf_row = to_row(lf_col)
    li_row = to_row(i_col)
    b_col = jnp.sum(jnp.where(causal, lf_row, 0.0), axis=1, keepdims=True)
    b_row = jnp.sum(jnp.where(tt <= ss, lf_col, 0.0), axis=0, keepdims=True)
    g = jnp.sum(lf_col, axis=0, keepdims=True)
    a_row = g - b_row + li_row
    a_col = g - b_col + i_col
    m_loc = jnp.max(a_row, axis=1, keepdims=True)
    d_log = jnp.where(causal, b_col - b_row + li_row, NEG_INF)
    inter = b_col + m_prev
    m_t = jnp.maximum(jnp.max(d_log, axis=1, keepdims=True), inter)
    dmat = jnp.exp(d_log - m_t)
    amat = lax.dot_general(q, k, _DIMS["nt"], preferred_element_type=F32)
    p = dmat * amat
    sig = jnp.exp(inter - m_t)
    c_prev_b = c_prev.astype(BF16)
    qc = jnp.dot(q, c_prev_b, preferred_element_type=F32)
    num = jnp.dot(p.astype(BF16), v, preferred_element_type=F32) + sig * qc
    qn = jnp.sum(q.astype(F32) * n_prev, axis=1, keepdims=True)
    den_raw = jnp.sum(p, axis=1, keepdims=True) + sig * qn
    floor = jnp.exp(-m_t)
    den = jnp.maximum(jnp.abs(den_raw), floor)
    h = num / den
    m_new = jnp.maximum(g + m_prev, m_loc)
    w_col = jnp.exp(a_col - m_new)
    alpha = jnp.exp(g + m_prev - m_new)
    return dict(eye=eye, causal=causal, tt=tt, ss=ss, f_col=f_col, dmat=dmat, amat=amat, p=p, sig=sig, qc=qc,
                qn=qn, num=num, den_raw=den_raw, floor=floor, den=den, h=h, m_new=m_new, w_col=w_col, alpha=alpha,
                c_prev_b=c_prev_b)


def _head_gates(gc, hd):
    return gc[:, hd:hd + 1], gc[:, M_HEADS + hd:M_HEADS + hd + 1]


def _mlstm_fwd(m4, gates, bias, norm_w, name):
    t = m4.shape[0]
    nblk = t // M_BLOCK
    nc = t // M_CHUNK
    L, dh = M_CHUNK, M_HEAD_DIM

    def body(m4_ref, g_ref, b_ref, w_ref, hm_ref, cp_ref, nm_ref, c_s, n_s, m_s):
        @pl.when(pl.program_id(0) == 0)
        def _():
            c_s[...] = jnp.zeros_like(c_s)
            n_s[...] = jnp.zeros_like(n_s)
            m_s[...] = jnp.zeros_like(m_s)

        row8 = lax.broadcasted_iota(jnp.int32, (8, dh), 0)

        def chunk(c, carry):
            rows = pl.ds(pl.multiple_of(c * L, L), L)
            gc = g_ref[rows, :] + b_ref[...]
            for hd in range(M_HEADS):
                q = m4_ref[rows, hd * dh:(hd + 1) * dh]
                k = m4_ref[rows, M_WIDTH + hd * dh:M_WIDTH + (hd + 1) * dh]
                v = m4_ref[rows, 2 * M_WIDTH + hd * dh:2 * M_WIDTH + (hd + 1) * dh]
                o = m4_ref[rows, 3 * M_WIDTH + hd * dh:3 * M_WIDTH + (hd + 1) * dh].astype(F32)
                i_col, f_col = _head_gates(gc, hd)
                c_prev = c_s[hd]
                n_prev = n_s[hd, 0:1, :]
                m_prev = m_s[hd, 0:1, 0:1]
                r = _mlstm_chunk_fwd(q, k, v, i_col, f_col, c_prev, n_prev, m_prev)
                h = r["h"]
                hn = h * lax.rsqrt(jnp.mean(h * h, axis=-1, keepdims=True) + EPS) * w_ref[hd:hd + 1, :]
                hm_ref[rows, hd * dh:(hd + 1) * dh] = (hn * jax.nn.sigmoid(o)).astype(hm_ref.dtype)
                cp_ref[hd, c] = r["c_prev_b"]
                nm_ref[hd, c] = jnp.where(row8 == 0, n_prev, jnp.where(row8 == 1, m_prev, 0.0))
                kw = k.astype(F32) * r["w_col"]
                c_s[hd] = r["alpha"] * c_prev + lax.dot_general(kw.astype(BF16), v, _DIMS["tn"],
                                                                preferred_element_type=F32)
                n_s[hd, 0:1, :] = r["alpha"] * n_prev + jnp.sum(kw, axis=0, keepdims=True)
                m_s[hd] = jnp.broadcast_to(r["m_new"], (8, dh))
            return carry

        lax.fori_loop(0, M_CHUNKS_PER_BLOCK, chunk, 0)

    return pl.pallas_call(
        body, name=name, grid=(nblk,),
        in_specs=[pl.BlockSpec((M_BLOCK, 4 * M_WIDTH), lambda i: (i, 0)),
                  pl.BlockSpec((M_BLOCK, LANES), lambda i: (i, 0)),
                  pl.BlockSpec((1, LANES), lambda i: (0, 0)),
                  pl.BlockSpec((M_HEADS, dh), lambda i: (0, 0))],
        out_specs=[pl.BlockSpec((M_BLOCK, M_WIDTH), lambda i: (i, 0)),
                   pl.BlockSpec((M_HEADS, M_CHUNKS_PER_BLOCK, dh, dh), lambda i: (0, i, 0, 0)),
                   pl.BlockSpec((M_HEADS, M_CHUNKS_PER_BLOCK, 8, dh), lambda i: (0, i, 0, 0))],
        out_shape=[jax.ShapeDtypeStruct((t, M_WIDTH), BF16),
                   jax.ShapeDtypeStruct((M_HEADS, nc, dh, dh), BF16),
                   jax.ShapeDtypeStruct((M_HEADS, nc, 8, dh), F32)],
        scratch_shapes=[pltpu.VMEM((M_HEADS, dh, dh), F32), pltpu.VMEM((M_HEADS, 8, dh), F32),
                        pltpu.VMEM((M_HEADS, 8, dh), F32)],
        compiler_params=_cparams("arbitrary"),
    )(m4, gates, bias, norm_w)


def _mlstm_bwd(m4, gates, bias, norm_w, c_prev_all, nm_all, dhm, name):
    t = m4.shape[0]
    nblk = t // M_BLOCK
    L, dh = M_CHUNK, M_HEAD_DIM

    def body(m4_ref, g_ref, b_ref, w_ref, cp_ref, nm_ref, dhm_ref, dm4_ref, dg_ref, small_ref, dc_s, dn_s):
        @pl.when(pl.program_id(0) == 0)
        def _():
            dc_s[...] = jnp.zeros_like(dc_s)
            dn_s[...] = jnp.zeros_like(dn_s)
            small_ref[...] = jnp.zeros_like(small_ref)

        lane = lax.broadcasted_iota(jnp.int32, (L, LANES), 1)
        row8 = lax.broadcasted_iota(jnp.int32, (8, LANES), 0)

        def chunk(ci, carry):
            c = M_CHUNKS_PER_BLOCK - 1 - ci
            rows = pl.ds(pl.multiple_of(c * L, L), L)
            gc = g_ref[rows, :] + b_ref[...]
            dg_tile = jnp.zeros((L, LANES), F32)
            small = jnp.zeros((8, LANES), F32)
            for hd in range(M_HEADS):
                q = m4_ref[rows, hd * dh:(hd + 1) * dh]
                k = m4_ref[rows, M_WIDTH + hd * dh:M_WIDTH + (hd + 1) * dh]
                v = m4_ref[rows, 2 * M_WIDTH + hd * dh:2 * M_WIDTH + (hd + 1) * dh]
                o = m4_ref[rows, 3 * M_WIDTH + hd * dh:3 * M_WIDTH + (hd + 1) * dh].astype(F32)
                i_col, f_col = _head_gates(gc, hd)
                c_prev = cp_ref[hd, c].astype(F32)
                nm = nm_ref[hd, c]
                n_prev = nm[0:1, :]
                m_prev = nm[1:2, 0:1]
                r = _mlstm_chunk_fwd(q, k, v, i_col, f_col, c_prev, n_prev, m_prev)
                eye, tt, ss = r["eye"], r["tt"], r["ss"]
                h, den, sig, p = r["h"], r["den"], r["sig"], r["p"]
                w_col, alpha, c_prev_b = r["w_col"], r["alpha"], r["c_prev_b"]

                def to_row(col):
                    return jnp.sum(jnp.where(eye, col, 0.0), axis=0, keepdims=True)

                def to_col(row):
                    return jnp.sum(jnp.where(eye, row, 0.0), axis=1, keepdims=True)

                w_h = w_ref[hd:hd + 1, :]
                rn = lax.rsqrt(jnp.mean(h * h, axis=-1, keepdims=True) + EPS)
                hh = h * rn
                og = jax.nn.sigmoid(o)
                dhm_c = dhm_ref[rows, hd * dh:(hd + 1) * dh].astype(F32)
                dhn = dhm_c * og
                d_o = dhm_c * hh * w_h * og * (1.0 - og)
                small = small + jnp.where(row8 == hd, jnp.sum(dhn * hh, axis=0, keepdims=True), 0.0)
                dhh = dhn * w_h
                dh_ = rn * (dhh - hh * jnp.mean(dhh * hh, axis=-1, keepdims=True))

                dnum = dh_ / den
                dden = -jnp.sum(dh_ * h, axis=-1, keepdims=True) / den
                dden_raw = jnp.where(jnp.abs(r["den_raw"]) >= r["floor"], dden * jnp.sign(r["den_raw"]), 0.0)
                dnum_b = dnum.astype(BF16)
                dp = lax.dot_general(dnum_b, v, _DIMS["nt"], preferred_element_type=F32) + dden_raw
                p_b = p.astype(BF16)
                dc = dc_s[hd]
                dn = dn_s[hd, 0:1, :]
                dc_b = dc.astype(BF16)
                g2 = sig * dnum
                g2_b = g2.astype(BF16)
                sd = sig * dden_raw
                da_mat = (dp * r["dmat"]).astype(BF16)
                dqs = (lax.dot_general(g2_b, c_prev_b, _DIMS["nt"], preferred_element_type=F32)
                       + sd * n_prev + jnp.dot(da_mat, k, preferred_element_type=F32))
                r_mat = lax.dot_general(v, dc_b, _DIMS["nt"], preferred_element_type=F32) + dn
                kf = k.astype(F32)
                d_k = lax.dot_general(da_mat, q, _DIMS["tn"], preferred_element_type=F32) + w_col * r_mat
                d_v = (lax.dot_general(p_b, dnum_b, _DIMS["tn"], preferred_element_type=F32)
                       + w_col * jnp.dot(k, dc_b, preferred_element_type=F32))
                d_omega = jnp.sum(kf * r_mat, axis=-1, keepdims=True)
                da_col = d_omega * w_col
                dsig = jnp.sum(dnum * r["qc"], axis=-1, keepdims=True) + dden_raw * r["qn"]
                pp = dp * p
                r1 = jnp.sum(pp, axis=1, keepdims=True)
                c1_col = to_col(jnp.sum(pp, axis=0, keepdims=True))
                d_alpha = (jnp.sum(jnp.sum(dc * c_prev, axis=1, keepdims=True), axis=0, keepdims=True)
                           + jnp.sum(dn * n_prev, axis=1, keepdims=True))
                dgl = jnp.sum(da_col, axis=0, keepdims=True) + d_alpha * alpha
                db_col = r1 + dsig * sig - c1_col - da_col
                dli_col = c1_col + da_col
                db_row = to_row(db_col)
                dlf_col = jnp.sum(jnp.where(ss >= tt, db_row, 0.0), axis=1, keepdims=True) + dgl
                df_col = dlf_col * jax.nn.sigmoid(-r["f_col"])

                dc_s[hd] = alpha * dc + lax.dot_general(q, g2_b, _DIMS["tn"], preferred_element_type=F32)
                dn_s[hd, 0:1, :] = alpha * dn + jnp.sum(sd * q.astype(F32), axis=0, keepdims=True)

                dm4_ref[rows, hd * dh:(hd + 1) * dh] = (dqs * M_SCALE).astype(dm4_ref.dtype)
                dm4_ref[rows, M_WIDTH + hd * dh:M_WIDTH + (hd + 1) * dh] = d_k.astype(dm4_ref.dtype)
                dm4_ref[rows, 2 * M_WIDTH + hd * dh:2 * M_WIDTH + (hd + 1) * dh] = d_v.astype(dm4_ref.dtype)
                dm4_ref[rows, 3 * M_WIDTH + hd * dh:3 * M_WIDTH + (hd + 1) * dh] = d_o.astype(dm4_ref.dtype)
                dg_tile = dg_tile + jnp.where(lane == hd, dli_col, 0.0) + jnp.where(lane == M_HEADS + hd, df_col, 0.0)
            dg_ref[rows, :] = dg_tile
            small = small + jnp.where(row8 == M_HEADS, jnp.sum(dg_tile, axis=0, keepdims=True), 0.0)
            small_ref[...] += small
            return carry

        lax.fori_loop(0, M_CHUNKS_PER_BLOCK, chunk, 0)

    rev = lambda i: nblk - 1 - i
    return pl.pallas_call(
        body, name=name, grid=(nblk,),
        in_specs=[pl.BlockSpec((M_BLOCK, 4 * M_WIDTH), lambda i: (rev(i), 0)),
                  pl.BlockSpec((M_BLOCK, LANES), lambda i: (rev(i), 0)),
                  pl.BlockSpec((1, LANES), lambda i: (0, 0)),
                  pl.BlockSpec((M_HEADS, dh), lambda i: (0, 0)),
                  pl.BlockSpec((M_HEADS, M_CHUNKS_PER_BLOCK, dh, dh), lambda i: (0, rev(i), 0, 0)),
                  pl.BlockSpec((M_HEADS, M_CHUNKS_PER_BLOCK, 8, dh), lambda i: (0, rev(i), 0, 0)),
                  pl.BlockSpec((M_BLOCK, M_WIDTH), lambda i: (rev(i), 0))],
        out_specs=[pl.BlockSpec((M_BLOCK, 4 * M_WIDTH), lambda i: (rev(i), 0)),
                   pl.BlockSpec((M_BLOCK, LANES), lambda i: (rev(i), 0)),
                   pl.BlockSpec((8, LANES), lambda i: (0, 0))],
        out_shape=[jax.ShapeDtypeStruct((t, 4 * M_WIDTH), BF16), jax.ShapeDtypeStruct((t, LANES), F32),
                   jax.ShapeDtypeStruct((8, LANES), F32)],
        scratch_shapes=[pltpu.VMEM((M_HEADS, dh, dh), F32), pltpu.VMEM((M_HEADS, 8, dh), F32)],
        compiler_params=_cparams("arbitrary"),
    )(m4, gates, bias, norm_w, c_prev_all, nm_all, dhm)


def _rope_tables(pos_col, width):
    lane = lax.broadcasted_iota(jnp.int32, (1, width), 1) % A_QK_PAD
    half = A_ROPE // 2
    first = (lane >= A_NOPE) & (lane < A_NOPE + half)
    second = (lane >= A_NOPE + half) & (lane < A_NOPE + A_ROPE)
    idx = jnp.where(first, lane - A_NOPE, lane - A_NOPE - half).astype(F32)
    inv_freq = jnp.exp(idx * (-math.log(ROPE_THETA) / half))
    ang = pos_col.astype(F32) * inv_freq
    cos, sin = jnp.cos(ang), jnp.sin(ang)
    rot = first | second
    return jnp.where(rot, cos, 1.0), jnp.where(first, -sin, 0.0), jnp.where(second, sin, 0.0)


def _rope_apply(vv, cosf, s1, s2):
    half = A_ROPE // 2
    w = vv.shape[-1]
    return vv * cosf + pltpu.roll(vv, w - half, 1) * s1 + pltpu.roll(vv, half, 1) * s2


def _rope_apply_t(dd, cosf, s1, s2):
    half = A_ROPE // 2
    w = dd.shape[-1]
    return dd * cosf + pltpu.roll(dd * s1, half, 1) + pltpu.roll(dd * s2, w - half, 1)


A_BIAS_LANE_K = A_NOPE + A_ROPE
A_BIAS_LANE_V = A_V


def _hi_lo(val):
    hi = val.astype(BF16)
    return hi, (val - hi.astype(F32)).astype(BF16)


def _rope_fwd(qpre, kpre, vpre, gates, pos, name, tm=512):
    t, w = qpre.shape

    def body(q_ref, k_ref, v_ref, g_ref, p_ref, qo_ref, ko_ref, vo_ref):
        cosf, s1, s2 = _rope_tables(p_ref[...], LANES)
        lane = lax.broadcasted_iota(jnp.int32, (1, LANES), 1)
        kr = jnp.where((lane >= A_NOPE) & (lane < A_NOPE + A_ROPE), g_ref[...], 0.0)
        kr = _rope_apply(kr, cosf, s1, s2)
        kr = jnp.where((lane == A_BIAS_LANE_K) | (lane == A_BIAS_LANE_K + 1), 1.0, kr)
        v_one = (lane == A_BIAS_LANE_V) | (lane == A_BIAS_LANE_V + 1)
        for hd in range(A_HEADS):
            sl = slice(hd * LANES, (hd + 1) * LANES)
            qo_ref[:, sl] = (_rope_apply(q_ref[:, sl], cosf, s1, s2) * ATTN_SCALE).astype(qo_ref.dtype)
            ko_ref[:, sl] = (k_ref[:, sl] + kr).astype(ko_ref.dtype)
            vo_ref[:, sl] = jnp.where(v_one, jnp.ones((), BF16), v_ref[:, sl])

    row = pl.BlockSpec((tm, w), lambda i: (i, 0))
    out = jax.ShapeDtypeStruct((t, w), BF16)
    return pl.pallas_call(
        body, name=name, grid=(t // tm,),
        in_specs=[row, row, row, pl.BlockSpec((tm, LANES), lambda i: (i, 0)), pl.BlockSpec((tm, 1), lambda i: (i, 0))],
        out_specs=[row, row, row], out_shape=[out, out, out], compiler_params=_cparams("parallel"),
    )(qpre, kpre, vpre, gates, pos)


def _rope_bwd(dq, dk, dgates, pos, name, tm=512):
    t, w = dq.shape

    def body(dq_ref, dk_ref, dg_ref, p_ref, dqo_ref, dgo_ref):
        cosf, s1, s2 = _rope_tables(p_ref[...], LANES)
        acc = jnp.zeros((tm, LANES), F32)
        for hd in range(A_HEADS):
            sl = slice(hd * LANES, (hd + 1) * LANES)
            dqo_ref[:, sl] = (_rope_apply_t(dq_ref[:, sl], cosf, s1, s2) * ATTN_SCALE).astype(dqo_ref.dtype)
            acc = acc + dk_ref[:, sl].astype(F32)
        lane = lax.broadcasted_iota(jnp.int32, (1, LANES), 1)
        dkr = _rope_apply_t(acc, cosf, s1, s2)
        dkr = jnp.where((lane >= A_NOPE) & (lane < A_NOPE + A_ROPE), dkr, 0.0)
        dgo_ref[...] = (dg_ref[...] + dkr).astype(dgo_ref.dtype)

    row = pl.BlockSpec((tm, w), lambda i: (i, 0))
    nar = pl.BlockSpec((tm, LANES), lambda i: (i, 0))
    return pl.pallas_call(
        body, name=name, grid=(t // tm,),
        in_specs=[row, row, nar, pl.BlockSpec((tm, 1), lambda i: (i, 0))],
        out_specs=[row, nar],
        out_shape=[jax.ShapeDtypeStruct((t, w), BF16), jax.ShapeDtypeStruct((t, LANES), BF16)],
        compiler_params=_cparams("parallel"),
    )(dq, dk, dgates, pos)


A_TQ = 512
A_TK = 512


A_HEADS_PER_STEP = 2


def _flash_fwd(q, k, v, name):
    t = q.shape[0]
    tq, tk = A_TQ, A_TK
    nh = A_HEADS_PER_STEP
    wblk = nh * LANES

    def body(q_ref, k_ref, v_ref, o_ref, qb_ref):
        i = pl.program_id(1)
        lane = lax.broadcasted_iota(jnp.int32, (tq, LANES), 1)
        qpos = i * tq + lax.broadcasted_iota(jnp.int32, (tq, tk), 0)
        kofs = lax.broadcasted_iota(jnp.int32, (tq, tk), 1)
        qs = [q_ref[:, hh * LANES:(hh + 1) * LANES] for hh in range(nh)]

        def step(kb, carry, masked):
            rows = pl.ds(pl.multiple_of(kb * tk, tk), tk)
            new = []
            for hh in range(nh):
                m, acc = carry[hh]
                kh = k_ref[rows, hh * LANES:(hh + 1) * LANES]
                s = lax.dot_general(qs[hh], kh, _DIMS["nt"], preferred_element_type=F32)
                if masked:
                    s = jnp.where(kb * tk + kofs <= qpos, s, NEG_INF)
                m_new = jnp.maximum(m, jnp.max(s, axis=1, keepdims=True))
                p = jnp.exp(s - m_new)
                acc = jnp.exp(m - m_new) * acc + jnp.dot(p.astype(BF16), v_ref[rows, hh * LANES:(hh + 1) * LANES],
                                                          preferred_element_type=F32)
                new.append((m_new, acc))
            return tuple(new)

        carry = tuple((jnp.full((tq, 1), NEG_INF, F32), jnp.zeros((tq, LANES), F32)) for _ in range(nh))
        n_full = (i * tq) // tk
        carry = lax.fori_loop(0, n_full, functools.partial(step, masked=False), carry)
        for d in range(tq // tk):
            carry = step(n_full + d, carry, True)
        for hh in range(nh):
            m, acc = carry[hh]
            l = acc[:, A_BIAS_LANE_V:A_BIAS_LANE_V + 1]
            o_ref[:, hh * LANES:(hh + 1) * LANES] = (acc / l).astype(o_ref.dtype)
            hi, lo = _hi_lo(-(m + jnp.log(l)))
            qb_ref[:, hh * LANES:(hh + 1) * LANES] = jnp.where(
                lane == A_BIAS_LANE_K, hi, jnp.where(lane == A_BIAS_LANE_K + 1, lo, qs[hh]))

    blk = pl.BlockSpec((tq, wblk), lambda j, i: (i, j))
    res = pl.BlockSpec((t, wblk), lambda j, i: (0, j))
    out = jax.ShapeDtypeStruct((t, A_HEADS * LANES), BF16)
    return pl.pallas_call(
        body, name=name, grid=(A_HEADS // nh, t // tq), in_specs=[blk, res, res], out_specs=[blk, blk],
        out_shape=[out, out], compiler_params=_cparams("parallel", "parallel"),
    )(q, k, v)


def _attn_delta(do, o, name, tm=512):
    t, w = do.shape

    def body(do_ref, o_ref, out_ref):
        lane = lax.broadcasted_iota(jnp.int32, (tm, LANES), 1)
        for hd in range(A_HEADS):
            sl = slice(hd * LANES, (hd + 1) * LANES)
            d = do_ref[:, sl]
            delta = jnp.sum(jnp.where(lane < A_V, d.astype(F32) * o_ref[:, sl].astype(F32), 0.0), axis=1, keepdims=True)
            hi, lo = _hi_lo(-delta)
            out_ref[:, sl] = jnp.where(lane == A_BIAS_LANE_V, hi, jnp.where(lane == A_BIAS_LANE_V + 1, lo, d))

    row = pl.BlockSpec((tm, w), lambda i: (i, 0))
    return pl.pallas_call(
        body, name=name, grid=(t // tm,), in_specs=[row, row], out_specs=row,
        out_shape=jax.ShapeDtypeStruct((t, w), BF16), compiler_params=_cparams("parallel"),
    )(do, o)


def _flash_bwd(qb, k, v, doe, name):
    t = qb.shape[0]
    tq, tk = A_TQ, A_TK
    assert tq == tk
    nh = A_HEADS_PER_STEP
    wblk = nh * LANES
    nq = t // tq

    def body(q_ref, k_ref, v_ref, do_ref, dq_ref, dk_ref, dv_ref):
        kb = pl.program_id(1)

        @pl.when(kb == 0)
        def _():
            dq_ref[...] = jnp.zeros_like(dq_ref)

        kpos = kb * tk + lax.broadcasted_iota(jnp.int32, (tk, tq), 0)
        qofs = lax.broadcasted_iota(jnp.int32, (tk, tq), 1)
        ks = [k_ref[:, hh * LANES:(hh + 1) * LANES] for hh in range(nh)]
        vs = [v_ref[:, hh * LANES:(hh + 1) * LANES] for hh in range(nh)]

        def step(qi, carry, masked):
            rows = pl.ds(pl.multiple_of(qi * tq, tq), tq)
            new = []
            for hh in range(nh):
                dk_acc, dv_acc = carry[hh]
                qh = q_ref[rows, hh * LANES:(hh + 1) * LANES]
                doh = do_ref[rows, hh * LANES:(hh + 1) * LANES]
                p_t = jnp.exp(lax.dot_general(ks[hh], qh, _DIMS["nt"], preferred_element_type=F32))
                if masked:
                    p_t = jnp.where(qi * tq + qofs >= kpos, p_t, 0.0)
                ds_t = (p_t * lax.dot_general(vs[hh], doh, _DIMS["nt"], preferred_element_type=F32)).astype(BF16)
                dv_acc = dv_acc + jnp.dot(p_t.astype(BF16), doh, preferred_element_type=F32)
                dk_acc = dk_acc + jnp.dot(ds_t, qh, preferred_element_type=F32)
                dq_ref[rows, hh * LANES:(hh + 1) * LANES] += lax.dot_general(ds_t, ks[hh], _DIMS["tn"],
                                                                             preferred_element_type=F32)
                new.append((dk_acc, dv_acc))
            return tuple(new)

        carry = tuple((jnp.zeros((tk, LANES), F32), jnp.zeros((tk, LANES), F32)) for _ in range(nh))
        carry = step(kb, carry, True)
        carry = lax.fori_loop(kb + 1, nq, functools.partial(step, masked=False), carry)
        for hh in range(nh):
            dk_ref[:, hh * LANES:(hh + 1) * LANES] = carry[hh][0].astype(dk_ref.dtype)
            dv_ref[:, hh * LANES:(hh + 1) * LANES] = carry[hh][1].astype(dv_ref.dtype)

    res = pl.BlockSpec((t, wblk), lambda j, kb: (0, j))
    blk = pl.BlockSpec((tk, wblk), lambda j, kb: (kb, j))
    wide = (t, A_HEADS * LANES)
    return pl.pallas_call(
        body, name=name, grid=(A_HEADS // nh, t // tk), in_specs=[res, blk, blk, res], out_specs=[res, blk, blk],
        out_shape=[jax.ShapeDtypeStruct(wide, F32), jax.ShapeDtypeStruct(wide, BF16), jax.ShapeDtypeStruct(wide, BF16)],
        compiler_params=_cparams("parallel", "arbitrary"),
    )(qb, k, v, doe)


_SPLITS = (M_WIDTH, M_WIDTH, M_WIDTH, M_WIDTH, M_HEADS, M_HEADS, Q_RANK, KV_RANK, A_ROPE)
_OFFS = tuple(sum(_SPLITS[:i]) for i in range(len(_SPLITS) + 1))
_GATE_BLOCK_KR = A_NOPE


def _split_w_in_t(wt):
    z = lambda n: jnp.zeros((n, wt.shape[1]), wt.dtype)
    w_g = jnp.concatenate([wt[_OFFS[4]:_OFFS[6]], z(_GATE_BLOCK_KR - 2 * M_HEADS), wt[_OFFS[8]:_OFFS[9]],
                           z(LANES - _GATE_BLOCK_KR - A_ROPE)], axis=0)
    return wt[_OFFS[6]:_OFFS[8]], w_g


def _merge_w_in_grad_t(g_m4, g_a2, g_g):
    return jnp.concatenate([g_m4, g_g[:2 * M_HEADS], g_a2, g_g[_GATE_BLOCK_KR:_GATE_BLOCK_KR + A_ROPE]], axis=0)


def _pad_heads(w, used):
    w3 = w.reshape(A_HEADS, used, w.shape[1])
    return jnp.pad(w3, ((0, 0), (0, LANES - used), (0, 0))).reshape(A_HEADS * LANES, w.shape[1])


def _unpad_heads(g, used):
    return g.reshape(A_HEADS, LANES, g.shape[1])[:, :used].reshape(A_HEADS * used, g.shape[1])


def _split_w_kv_b_t(wt):
    w3 = wt.reshape(A_HEADS, A_NOPE + A_V, wt.shape[1])
    pad = lambda part: jnp.pad(part, ((0, 0), (0, LANES - part.shape[1]), (0, 0))).reshape(A_HEADS * LANES, wt.shape[1])
    return pad(w3[:, :A_NOPE]), pad(w3[:, A_NOPE:])


def _merge_w_kv_b_grad_t(gk, gv):
    gk3 = gk.reshape(A_HEADS, LANES, gk.shape[1])[:, :A_NOPE]
    gv3 = gv.reshape(A_HEADS, LANES, gv.shape[1])[:, :A_V]
    return jnp.concatenate([gk3, gv3], axis=1).reshape(A_HEADS * (A_NOPE + A_V), gk.shape[1])


def _local_step(x, pos, target, nw, w):
    w_in = w["w_in"]
    w_a2, w_g = _split_w_in_t(w_in)
    w_qb = _pad_heads(w["w_q_b"], A_NOPE + A_ROPE)
    w_k, w_v = _split_w_kv_b_t(w["w_kv_b"])
    w_out_m, w_out_a = w["w_out"][:M_WIDTH], _pad_heads(w["w_out"][M_WIDTH:], A_V)
    w_gate, w_up, w_down = w["w_gate"], w["w_up"], w["w_down"]
    bias = jnp.pad(nw["b_gates"], ((0, 0), (0, LANES - 2 * M_HEADS)))
    mnorm = nw["mlstm_norm_w"].reshape(M_HEADS, M_HEAD_DIM)
    n_m4 = 4 * M_WIDTH

    u1 = _rmsnorm_fwd(x, nw["attn_norm_w"], "attn_norm")
    m4 = _matmul(u1, w_in, "nt", BF16, "proj_mlstm", b_rows=n_m4, lead_scale=(M_WIDTH, M_SCALE))
    a2 = _matmul(u1, w_a2, "nt", BF16, "proj_latents", tn=640)
    gates = _matmul(u1, w_g, "nt", F32, "proj_gates")
    hm, c_prev_all, nm_all = _mlstm_fwd(m4, gates, bias, mnorm, "mlstm_fwd")
    qa, kva = a2[:, :Q_RANK], a2[:, Q_RANK:]
    qa_n = _rmsnorm_fwd(qa, nw["q_a_norm_w"], "q_a_norm")
    kv_n = _rmsnorm_fwd(kva, nw["kv_a_norm_w"], "kv_a_norm")
    qpre = _matmul(qa_n, w_qb, "nt", F32, "q_up")
    kpre = _matmul(kv_n, w_k, "nt", F32, "k_up")
    vpre = _matmul(kv_n, w_v, "nt", BF16, "v_up")
    qr, kr, vv = _rope_fwd(qpre, kpre, vpre, gates, pos, "rope_fwd")
    ha, qb = _flash_fwd(qr, kr, vv, "attn_fwd")
    h1 = _matmul(hm, w_out_m, "nn", F32, "out_proj_m", residual=x)
    h1 = _matmul(ha, w_out_a, "nn", F32, "out_proj_a", residual=h1)
    u2 = _rmsnorm_fwd(h1, nw["ffn_norm_w"], "ffn_norm")
    gg = _matmul(u2, w_gate, "nt", BF16, "ffn_gate", tn=1408)
    uu = _matmul(u2, w_up, "nt", BF16, "ffn_up", tn=1408)
    act = _swiglu_fwd(gg, uu, "swiglu_fwd")
    h2 = _matmul(act, w_down, "nn", F32, "ffn_down", residual=h1)

    dh2, dh2_b, loss, g_final = _final_loss(h2, target, nw["final_norm_w"].reshape(1, D_MODEL), "final_loss")
    grads = {"final_norm_w": g_final.reshape(D_MODEL)}
    dact = _matmul(dh2_b, w_down, "nt", BF16, "d_act", tn=1408)
    grads["w_down"] = _matmul(act, dh2_b, "tn", F32, "g_w_down", tm=256, tn=D_MODEL)
    dgg, duu = _swiglu_bwd(gg, uu, dact, "swiglu_bwd")
    du2 = _matmul(dgg, w_gate, "nn", F32, "d_u2_gate")
    du2 = _matmul(duu, w_up, "nn", F32, "d_u2_up", residual=du2)
    grads["w_gate"] = _matmul(dgg, u2, "tn", F32, "g_w_gate", tm=256, tn=D_MODEL)
    grads["w_up"] = _matmul(duu, u2, "tn", F32, "g_w_up", tm=256, tn=D_MODEL)
    dh1, dh1_b, grads["ffn_norm_w"] = _rmsnorm_bwd(du2, h1, nw["ffn_norm_w"], "ffn_norm_bwd", F32, residual=dh2,
                                                   also_bf16=True)
    dhm = _matmul(dh1_b, w_out_m, "nt", BF16, "d_hm")
    dha = _matmul(dh1_b, w_out_a, "nt", BF16, "d_ha")
    grads["w_out"] = jnp.concatenate([_matmul(hm, dh1_b, "tn", F32, "g_w_out_m", tn=D_MODEL),
                                      _unpad_heads(_matmul(ha, dh1_b, "tn", F32, "g_w_out_a", tn=D_MODEL), A_V)], axis=0)

    dqr, dkr, dvv = _flash_bwd(qb, kr, vv, _attn_delta(dha, ha, "attn_delta"), "attn_bwd")
    dm4, dgates, small = _mlstm_bwd(m4, gates, bias, mnorm, c_prev_all, nm_all, dhm, "mlstm_bwd")
    grads["mlstm_norm_w"] = small[:M_HEADS].reshape(1, M_HEADS, M_HEAD_DIM)
    grads["b_gates"] = small[M_HEADS:M_HEADS + 1, :2 * M_HEADS]
    dqpre, dgk = _rope_bwd(dqr, dkr, dgates, pos, "rope_bwd")
    dqa_n = _matmul(dqpre, w_qb, "nn", BF16, "d_qa_n")
    grads["w_q_b"] = _unpad_heads(_matmul(dqpre, qa_n, "tn", F32, "g_w_q_b"), A_NOPE + A_ROPE)
    dkv_n = _matmul(dkr, w_k, "nn", F32, "d_kv_n_k")
    dkv_n = _matmul(dvv, w_v, "nn", BF16, "d_kv_n_v", residual=dkv_n)
    grads["w_kv_b"] = _merge_w_kv_b_grad_t(_matmul(dkr, kv_n, "tn", F32, "g_w_k"),
                                           _matmul(dvv, kv_n, "tn", F32, "g_w_v"))
    dqa, grads["q_a_norm_w"] = _rmsnorm_bwd(dqa_n, qa, nw["q_a_norm_w"], "q_a_norm_bwd", BF16)
    dkva, grads["kv_a_norm_w"] = _rmsnorm_bwd(dkv_n, kva, nw["kv_a_norm_w"], "kv_a_norm_bwd", BF16)
    da2 = jnp.concatenate([dqa, dkva], axis=1)
    du1 = _matmul(dm4, w_in, "nn", F32, "d_u1_m4", b_rows=n_m4)
    du1 = _matmul(da2, w_a2, "nn", F32, "d_u1_a2", residual=du1)
    du1 = _matmul(dgk, w_g, "nn", F32, "d_u1_g", residual=du1)
    grads["w_in"] = _merge_w_in_grad_t(_matmul(dm4, u1, "tn", F32, "g_w_m4", tn=D_MODEL),
                                       _matmul(da2, u1, "tn", F32, "g_w_a2", tm=640, tn=D_MODEL),
                                       _matmul(dgk, u1, "tn", F32, "g_w_g", tn=D_MODEL))
    grad_x, grads["attn_norm_w"] = _rmsnorm_bwd(du1, x, nw["attn_norm_w"], "attn_norm_bwd", F32, residual=dh1)
    return loss, grad_x, grads


MESH = pl.DeviceIdType.MESH
N_CHIPS = 4
BIG = ("w_in", "w_q_b", "w_kv_b", "w_out", "w_gate", "w_up", "w_down")
TRANSPOSED = ("w_in", "w_q_b", "w_kv_b", "w_gate", "w_up")
LANE_HALVED = ("w_in", "w_out", "w_gate", "w_up", "w_down")
SMALL = ("attn_norm_w", "b_gates", "mlstm_norm_w", "q_a_norm_w", "kv_a_norm_w", "ffn_norm_w", "final_norm_w")
HBM_SPEC = pl.BlockSpec(memory_space=pltpu.HBM)


def _stored(name, a):
    return a[0].T if name in TRANSPOSED else a[0]


def _unstored(name, a):
    return (a.T if name in TRANSPOSED else a)[None]


def _half_shape(name, shape):
    rs, cs = shape
    return (rs, cs // 2) if name in LANE_HALVED else (rs // 2, cs)


def _half(ref, name, h, *lead):
    rs, cs = ref.shape[-2:]
    if name in LANE_HALVED:
        return ref.at[(*lead, slice(None), pl.ds(h * (cs // 2), cs // 2))]
    return ref.at[(*lead, pl.ds(h * (rs // 2), rs // 2), slice(None))]


def _place():
    x, y, c = lax.axis_index("x"), lax.axis_index("y"), lax.axis_index("c")
    others = [(1 - x, y), (x, 1 - y), (1 - x, 1 - y)]
    return x, y, c, others


def _gather_weights(shards):
    n = len(shards)

    def body(*refs):
        ins, outs = refs[:n], refs[n:2 * n]
        send_sems, recv_sems = refs[2 * n:]
        x, y, c, others = _place()
        me = 2 * x + y
        sibling = (x, y, 1 - c)

        def copy(w, k, slab, core, to, src=None):
            dst = _half(outs[w], BIG[w], core, slab)
            return pltpu.make_async_remote_copy(
                src_ref=dst if src is None else src, dst_ref=dst, send_sem=send_sems.at[w * 6 + k],
                recv_sem=recv_sems.at[w * 6 + k], device_id=to, device_id_type=MESH)

        sends = []
        for w in range(n):
            for j, chip in enumerate(others):
                cp = copy(w, j, me, c, (*chip, c), src=_half(ins[w], BIG[w], c))
                cp.start()
                sends.append(cp)
        for w in range(n):
            for j, (ox, oy) in enumerate(others):
                slab = 2 * ox + oy
                copy(w, j, slab, c, (x, y, c)).wait_recv()
                fwd = copy(w, 3 + j, slab, c, sibling)
                fwd.start()
                sends.append(fwd)
        for w in range(n):
            for j, (ox, oy) in enumerate(others):
                copy(w, 3 + j, 2 * ox + oy, 1 - c, (x, y, c)).wait_recv()
        for cp in sends:
            cp.wait_send()

    gathered = pl.pallas_call(
        body, name="gather_weights", in_specs=[HBM_SPEC] * n, out_specs=[HBM_SPEC] * n,
        out_shape=[jax.ShapeDtypeStruct((N_CHIPS,) + s.shape, s.dtype) for s in shards],
        scratch_shapes=[pltpu.SemaphoreType.DMA((6 * n,)), pltpu.SemaphoreType.DMA((6 * n,))],
    )(*shards)
    me = 2 * lax.axis_index("x") + lax.axis_index("y")
    return [lax.dynamic_update_slice(g, s[None], (me, 0, 0)) for g, s in zip(gathered, shards)]


def _exchange(arrays, out_shapes, plan, copies_per_array, name):
    n = len(arrays)

    def body(*refs):
        ins, outs = refs[:n], refs[n:2 * n]
        send_sems, recv_sems = refs[2 * n:]
        rem = [pltpu.make_async_remote_copy(src_ref=s, dst_ref=d, send_sem=send_sems.at[k], recv_sem=recv_sems.at[k],
                                            device_id=to, device_id_type=MESH)
               for k, (s, d, to) in enumerate(plan(ins, outs, _place()))]
        for cp in rem:
            cp.start()
        for cp in rem:
            cp.wait_recv()
        for cp in rem:
            cp.wait_send()

    return pl.pallas_call(
        body, name=name, in_specs=[HBM_SPEC] * n, out_specs=[HBM_SPEC] * n, out_shape=out_shapes,
        scratch_shapes=[pltpu.SemaphoreType.DMA((copies_per_array * n,)),
                        pltpu.SemaphoreType.DMA((copies_per_array * n,))],
    )(*arrays)


def _plan_to_sibling(ins, outs, place):
    x, y, c, _ = place
    return [(ins[w], outs[w], (x, y, 1 - c)) for w in range(len(ins))]


def _plan_sibling_halves(ins, outs, place):
    x, y, c, _ = place
    return [(_half(ins[w], BIG[w], 1 - c, slice(None)), outs[w], (x, y, 1 - c)) for w in range(len(ins))]


def _plan_chip_partials(ins, outs, place):
    x, y, c, others = place
    return [(ins[w].at[2 * ox + oy], outs[w].at[j], (ox, oy, c))
            for w in range(len(ins)) for j, (ox, oy) in enumerate(others)]


def _chip_partial(g3, got, name, core):
    hs = tuple(got.shape[1:])
    if name in LANE_HALVED:
        g_in, g_spec = g3, pl.BlockSpec((1,) + hs, lambda s, c_ref: (s, 0, c_ref[0]))
    else:
        g_in, g_spec = g3.reshape((N_CHIPS, 2) + hs), pl.BlockSpec((1, 1) + hs, lambda s, c_ref: (s, c_ref[0], 0, 0))
    blk = pl.BlockSpec((1,) + hs, lambda s, c_ref: (s, 0, 0))

    def body(c_ref, g_ref, r_ref, o_ref):
        o_ref[0] = (g_ref[(0,) * (len(g_ref.shape) - 2)] + r_ref[0].astype(F32)).astype(o_ref.dtype)

    return pl.pallas_call(
        body, name="chip_partial_%s" % name,
        grid_spec=pltpu.PrefetchScalarGridSpec(num_scalar_prefetch=1, grid=(N_CHIPS,), in_specs=[g_spec, blk],
                                               out_specs=blk),
        out_shape=jax.ShapeDtypeStruct((N_CHIPS,) + hs, BF16), compiler_params=_cparams("parallel"),
    )(core, g_in, got)


def _shard_sum(part, got3, name, me):
    hs = tuple(part.shape[1:])

    def body(me_ref, p_ref, a_ref, b_ref, c_ref, o_ref):
        o_ref[...] = ((p_ref[0].astype(F32) + a_ref[0].astype(F32)) + b_ref[0].astype(F32)) + c_ref[0].astype(F32)

    slot = lambda j: pl.BlockSpec((1,) + hs, lambda i, me_ref: (j, 0, 0))
    return pl.pallas_call(
        body, name="shard_sum_%s" % name,
        grid_spec=pltpu.PrefetchScalarGridSpec(
            num_scalar_prefetch=1, grid=(1,),
            in_specs=[pl.BlockSpec((1,) + hs, lambda i, me_ref: (me_ref[0], 0, 0)), slot(0), slot(1), slot(2)],
            out_specs=pl.BlockSpec(hs, lambda i, me_ref: (0, 0))),
        out_shape=jax.ShapeDtypeStruct(hs, F32), compiler_params=_cparams("arbitrary"),
    )(me, part, got3, got3, got3)


def _adamw_halves(w, m, v, mine, other, name, core):
    hs = tuple(mine.shape)
    full = pl.BlockSpec(hs, (lambda h, c_ref: (0, h)) if name in LANE_HALVED else (lambda h, c_ref: (h, 0)))
    half = pl.BlockSpec(hs, lambda h, c_ref: (0, 0))

    def body(c_ref, w_ref, m_ref, v_ref, a_ref, b_ref, g_ref, d_ref, mo_ref, vo_ref):
        g = jnp.where(pl.program_id(0) == c_ref[0], a_ref[...], b_ref[...])
        delta, m_new, v_new = _adamw_math(w_ref[...], g, m_ref[...], v_ref[...])
        g_ref[...] = g
        d_ref[...] = delta
        mo_ref[...] = m_new
        vo_ref[...] = v_new

    out = jax.ShapeDtypeStruct(w.shape, F32)
    return pl.pallas_call(
        body, name="adamw_%s" % name,
        grid_spec=pltpu.PrefetchScalarGridSpec(num_scalar_prefetch=1, grid=(2,), in_specs=[full, full, full, half, half],
                                               out_specs=[full] * 4),
        out_shape=[out] * 4, compiler_params=_cparams("parallel"),
    )(core, w, m, v, mine, other)


def _adamw_math(w, g, m, v):
    m = ADAM_B1 * m + (1.0 - ADAM_B1) * g
    v = ADAM_B2 * v + (1.0 - ADAM_B2) * (g * g)
    m_hat = m / (1.0 - ADAM_B1 ** ADAM_STEP)
    v_hat = v / (1.0 - ADAM_B2 ** ADAM_STEP)
    delta = -ADAM_LR * (m_hat / (jnp.sqrt(v_hat) + ADAM_EPS) + ADAM_WD * w)
    return delta, m, v


def _reduce_big_grads(grads):
    core = lax.axis_index("c").astype(jnp.int32).reshape(1)
    me = (2 * lax.axis_index("x") + lax.axis_index("y")).astype(jnp.int32).reshape(1)
    g3 = [grads[n].reshape(N_CHIPS, grads[n].shape[0] // N_CHIPS, grads[n].shape[1]) for n in BIG]
    hs = [_half_shape(n, g.shape[1:]) for n, g in zip(BIG, g3)]
    bf = lambda lead: [jax.ShapeDtypeStruct(lead + h, BF16) for h in hs]
    got = _exchange([g.astype(BF16) for g in g3], bf((N_CHIPS,)), _plan_sibling_halves, 1, "sibling_halves")
    part = [_chip_partial(g, r, n, core) for n, g, r in zip(BIG, g3, got)]
    got3 = _exchange(part, bf((3,)), _plan_chip_partials, 3, "chip_partials")
    mine = [_shard_sum(p, r3, n, me) for n, p, r3 in zip(BIG, part, got3)]
    other = _exchange(mine, [jax.ShapeDtypeStruct(h, F32) for h in hs], _plan_to_sibling, 1, "sibling_result")
    return {n: (a, b) for n, a, b in zip(BIG, mine, other)}, core


SMALL_ROWS = 8
SMALL_LAYOUT = {"attn_norm_w": (0, 0, 1024), "ffn_norm_w": (1, 0, 1024), "final_norm_w": (2, 0, 1024),
                "q_a_norm_w": (3, 0, 384), "kv_a_norm_w": (3, 384, 256), "mlstm_norm_w": (4, 0, 512),
                "b_gates": (4, 512, 8)}
LOSS_SLOT = (5, 0)


def _pack_small(vals, loss=None):
    tile = jnp.zeros((SMALL_ROWS, D_MODEL), F32)
    for n, (r, c0, width) in SMALL_LAYOUT.items():
        tile = tile.at[r, c0:c0 + width].set(vals[n].reshape(width).astype(F32))
    if loss is not None:
        tile = tile.at[LOSS_SLOT[0], LOSS_SLOT[1]].set(loss)
    return tile


def _unpack_small(tile, shapes):
    return {n: tile[r, c0:c0 + width].reshape(shapes[n]) for n, (r, c0, width) in SMALL_LAYOUT.items()}


def _small_allreduce_adamw(g_tile, w_tile, m_tile, v_tile):
    def body(g_ref, w_ref, m_ref, v_ref, gsum_ref, d_ref, mo_ref, vo_ref, slots, send_sems, recv_sems):
        x, y, c, _ = _place()
        me = 4 * x + 2 * y + c
        slots[me] = g_ref[...]
        copies = []
        for k in range(1, 8):
            dx, dy, dc = (k >> 2) & 1, (k >> 1) & 1, k & 1
            to = (x ^ dx, y ^ dy, c ^ dc)
            cp = pltpu.make_async_remote_copy(src_ref=g_ref, dst_ref=slots.at[me], send_sem=send_sems.at[k - 1],
                                              recv_sem=recv_sems.at[k - 1], device_id=to, device_id_type=MESH)
            cp.start()
            copies.append(cp)
        for k in range(1, 8):
            src = me ^ k
            pltpu.make_async_remote_copy(src_ref=g_ref, dst_ref=slots.at[src], send_sem=send_sems.at[k - 1],
                                         recv_sem=recv_sems.at[k - 1], device_id=(x, y, c),
                                         device_id_type=MESH).wait_recv()
        for cp in copies:
            cp.wait_send()
        total = slots[0]
        for d in range(1, 8):
            total = total + slots[d]
        gsum_ref[...] = total
        delta, m_new, v_new = _adamw_math(w_ref[...], total, m_ref[...], v_ref[...])
        d_ref[...] = delta
        mo_ref[...] = m_new
        vo_ref[...] = v_new

    vm = pl.BlockSpec(memory_space=pltpu.VMEM)
    tile = jax.ShapeDtypeStruct((SMALL_ROWS, D_MODEL), F32)
    return pl.pallas_call(
        body, name="small_allreduce_adamw", in_specs=[vm] * 4, out_specs=[vm] * 4, out_shape=[tile] * 4,
        scratch_shapes=[pltpu.VMEM((8, SMALL_ROWS, D_MODEL), F32), pltpu.SemaphoreType.DMA((7,)),
                        pltpu.SemaphoreType.DMA((7,))],
    )(g_tile, w_tile, m_tile, v_tile)


def kernel(x, positions, attn_norm_w, w_in, b_gates, mlstm_norm_w, q_a_norm_w, w_q_b, kv_a_norm_w, w_kv_b, w_out, ffn_norm_w, w_gate, w_up, w_down, final_norm_w, loss_target, m_attn_norm_w, m_w_in, m_b_gates, m_mlstm_norm_w, m_q_a_norm_w, m_w_q_b, m_kv_a_norm_w, m_w_kv_b, m_w_out, m_ffn_norm_w, m_w_gate, m_w_up, m_w_down, m_final_norm_w, v_attn_norm_w, v_w_in, v_b_gates, v_mlstm_norm_w, v_q_a_norm_w, v_w_q_b, v_kv_a_norm_w, v_w_kv_b, v_w_out, v_ffn_norm_w, v_w_gate, v_w_up, v_w_down, v_final_norm_w):
    names = ("attn_norm_w", "w_in", "b_gates", "mlstm_norm_w", "q_a_norm_w", "w_q_b", "kv_a_norm_w", "w_kv_b", "w_out",
             "ffn_norm_w", "w_gate", "w_up", "w_down", "final_norm_w")
    wts = dict(zip(names, (attn_norm_w, w_in, b_gates, mlstm_norm_w, q_a_norm_w, w_q_b, kv_a_norm_w, w_kv_b, w_out,
                           ffn_norm_w, w_gate, w_up, w_down, final_norm_w)))
    mom = dict(zip(names, (m_attn_norm_w, m_w_in, m_b_gates, m_mlstm_norm_w, m_q_a_norm_w, m_w_q_b, m_kv_a_norm_w,
                           m_w_kv_b, m_w_out, m_ffn_norm_w, m_w_gate, m_w_up, m_w_down, m_final_norm_w)))
    vel = dict(zip(names, (v_attn_norm_w, v_w_in, v_b_gates, v_mlstm_norm_w, v_q_a_norm_w, v_w_q_b, v_kv_a_norm_w,
                           v_w_kv_b, v_w_out, v_ffn_norm_w, v_w_gate, v_w_up, v_w_down, v_final_norm_w)))
    t = x.shape[1]

    gathered = _gather_weights([_stored(n, wts[n]).astype(BF16) for n in BIG])
    full = {n: g.reshape(N_CHIPS * g.shape[1], g.shape[2]) for n, g in zip(BIG, gathered)}

    nw = {n: wts[n] for n in SMALL}
    loss, grad_x, grads = _local_step(x[0], positions.reshape(t, 1), loss_target[0], nw, full)

    halves, core = _reduce_big_grads(grads)
    outs_g, outs_d, outs_m, outs_v = {}, {}, {}, {}
    for n in BIG:
        res = _adamw_halves(_stored(n, wts[n]), _stored(n, mom[n]), _stored(n, vel[n]), *halves[n], n, core)
        outs_g[n], outs_d[n], outs_m[n], outs_v[n] = [_unstored(n, r) for r in res]
    shapes = {n: wts[n].shape for n in SMALL}
    g_tile = _pack_small({n: grads[n] for n in SMALL}, loss=loss[0, 0])
    gsum, d_tile, m_tile, v_tile = _small_allreduce_adamw(g_tile, _pack_small(wts), _pack_small(mom), _pack_small(vel))
    outs_g.update(_unpack_small(gsum, shapes))
    outs_d.update(_unpack_small(d_tile, shapes))
    outs_m.update(_unpack_small(m_tile, shapes))
    outs_v.update(_unpack_small(v_tile, shapes))
    total_loss = gsum[LOSS_SLOT[0], LOSS_SLOT[1]]
    return (total_loss, grad_x[None], *[outs_g[n] for n in names], *[outs_d[n] for n in names],
            *[outs_m[n] for n in names], *[outs_v[n] for n in names])
```

```python
import functools
import math

import jax
import jax.numpy as jnp
from jax import lax
from jax.experimental import pallas as pl
from jax.experimental.pallas import tpu as pltpu
from jax.experimental.pallas import tpu_sc as plsc

F32 = jnp.float32
BF16 = jnp.bfloat16

D_MODEL = 1024
M_HEADS = 4
M_HEAD_DIM = 128
M_WIDTH = M_HEADS * M_HEAD_DIM
M_CHUNK = 64
A_HEADS = 8
A_NOPE = 64
A_ROPE = 32
A_V = 64
A_WIDTH = A_HEADS * A_V
A_QK_PAD = 128
Q_RANK = 384
KV_RANK = 256
ROPE_THETA = 10000.0
D_FF = 2816
D_IN = 2728
EPS = 1e-6
ATTN_SCALE = (A_NOPE + A_ROPE) ** -0.5
M_SCALE = M_HEAD_DIM ** -0.5

ADAM_LR = 0.001
ADAM_B1 = 0.9
ADAM_B2 = 0.999
ADAM_EPS = 1e-08
ADAM_WD = 0.01
ADAM_STEP = 10

VMEM_LIMIT_BYTES = 56 * 1024 * 1024
LANES = 128
NEG_INF = float("-inf")


def _cparams(*sem):
    return pltpu.CompilerParams(dimension_semantics=sem if sem else None, vmem_limit_bytes=VMEM_LIMIT_BYTES)


_DIMS = {"nn": (((1,), (0,)), ((), ())), "nt": (((1,), (1,)), ((), ())), "tn": (((0,), (0,)), ((), ()))}


def _matmul(a, b, mode, out_dtype, name, tm=None, tn=1024, residual=None, b_rows=None, lead_scale=None):
    b_shape = b.shape if b_rows is None else (b_rows, b.shape[1])
    if mode == "nn":
        (m, k), (k2, n) = a.shape, b_shape
    elif mode == "nt":
        (m, k), (n, k2) = a.shape, b_shape
    else:
        (k, m), (k2, n) = a.shape, b_shape
    assert k == k2, (a.shape, b.shape, mode)
    if tm is None:
        tm = 512 if mode == "tn" else 1024
    tm, tn = min(tm, m), min(tn, n)
    assert m % tm == 0 and n % tn == 0, (m, n, tm, tn)
    a_spec = pl.BlockSpec((k, tm), lambda i, j: (0, i)) if mode == "tn" else pl.BlockSpec((tm, k), lambda i, j: (i, 0))
    b_spec = pl.BlockSpec((tn, k), lambda i, j: (j, 0)) if mode == "nt" else pl.BlockSpec((k, tn), lambda i, j: (0, j))
    o_spec = pl.BlockSpec((tm, tn), lambda i, j: (i, j))
    dims = _DIMS[mode]

    def body(a_ref, b_ref, *rest):
        o_ref = rest[-1]
        acc = lax.dot_general(a_ref[...].astype(BF16), b_ref[...].astype(BF16), dims, preferred_element_type=F32)
        if lead_scale is not None:
            assert lead_scale[0] % tn == 0
            acc = acc * jnp.where(pl.program_id(1) < lead_scale[0] // tn, lead_scale[1], 1.0)
        if residual is not None:
            acc = acc + rest[0][...].astype(F32)
        o_ref[...] = acc.astype(o_ref.dtype)

    ins = [a, b] + ([residual] if residual is not None else [])
    in_specs = [a_spec, b_spec] + ([o_spec] if residual is not None else [])
    return pl.pallas_call(
        body, name=name, grid=(m // tm, n // tn), in_specs=in_specs, out_specs=o_spec,
        out_shape=jax.ShapeDtypeStruct((m, n), out_dtype), compiler_params=_cparams("parallel", "parallel"),
    )(*ins)


def _rmsnorm_fwd(x, w, name, tm=512):
    t, d = x.shape

    def body(x_ref, w_ref, o_ref):
        xf = x_ref[...].astype(F32)
        r = lax.rsqrt(jnp.mean(xf * xf, axis=-1, keepdims=True) + EPS)
        o_ref[...] = (xf * r * w_ref[...]).astype(o_ref.dtype)

    return pl.pallas_call(
        body, name=name, grid=(t // tm,),
        in_specs=[pl.BlockSpec((tm, d), lambda i: (i, 0)), pl.BlockSpec((1, d), lambda i: (0, 0))],
        out_specs=pl.BlockSpec((tm, d), lambda i: (i, 0)),
        out_shape=jax.ShapeDtypeStruct((t, d), BF16), compiler_params=_cparams("parallel"),
    )(x, w)


def _rmsnorm_bwd(dy, x, w, name, out_dtype, residual=None, also_bf16=False, tm=512):
    t, d = x.shape

    def body(dy_ref, x_ref, w_ref, *rest):
        dx_ref, dw_ref = rest[-3 if also_bf16 else -2], rest[-1]
        xf = x_ref[...].astype(F32)
        r = lax.rsqrt(jnp.mean(xf * xf, axis=-1, keepdims=True) + EPS)
        xh = xf * r
        dyf = dy_ref[...].astype(F32)
        dyh = dyf * w_ref[...]
        dx = r * (dyh - xh * jnp.mean(dyh * xh, axis=-1, keepdims=True))
        if residual is not None:
            dx = dx + rest[0][...].astype(F32)
        dx_ref[...] = dx.astype(dx_ref.dtype)
        if also_bf16:
            rest[-2][...] = dx.astype(BF16)
        part = jnp.sum(dyf * xh, axis=0, keepdims=True)

        @pl.when(pl.program_id(0) == 0)
        def _():
            dw_ref[...] = part

        @pl.when(pl.program_id(0) > 0)
        def _():
            dw_ref[...] += part

    row = pl.BlockSpec((tm, d), lambda i: (i, 0))
    vec = pl.BlockSpec((1, d), lambda i: (0, 0))
    ins = [dy, x, w] + ([residual] if residual is not None else [])
    extra = [jax.ShapeDtypeStruct((t, d), BF16)] if also_bf16 else []
    return pl.pallas_call(
        body, name=name, grid=(t // tm,), in_specs=[row, row, vec] + ([row] if residual is not None else []),
        out_specs=[row] + [row] * len(extra) + [vec],
        out_shape=[jax.ShapeDtypeStruct((t, d), out_dtype)] + extra + [jax.ShapeDtypeStruct((1, d), F32)],
        compiler_params=_cparams("arbitrary"),
    )(*ins)


def _final_loss(h, target, w, name, tm=512):
    t, d = h.shape

    def body(h_ref, t_ref, w_ref, dh_ref, dhb_ref, loss_ref, dw_ref):
        xf = h_ref[...]
        r = lax.rsqrt(jnp.mean(xf * xf, axis=-1, keepdims=True) + EPS)
        xh = xf * r
        err = xh * w_ref[...] - t_ref[...]
        part_loss = 0.5 * jnp.sum(jnp.sum(err * err, axis=-1, keepdims=True), axis=0, keepdims=True) * (1.0 / d)
        dy = err * (1.0 / d)
        dyh = dy * w_ref[...]
        dh = r * (dyh - xh * jnp.mean(dyh * xh, axis=-1, keepdims=True))
        dh_ref[...] = dh
        dhb_ref[...] = dh.astype(BF16)
        part_dw = jnp.sum(dy * xh, axis=0, keepdims=True)
        part_loss = jnp.broadcast_to(part_loss, (1, LANES))

        @pl.when(pl.program_id(0) == 0)
        def _():
            dw_ref[...] = part_dw
            loss_ref[...] = part_loss

        @pl.when(pl.program_id(0) > 0)
        def _():
            dw_ref[...] += part_dw
            loss_ref[...] += part_loss

    row = pl.BlockSpec((tm, d), lambda i: (i, 0))
    vec = pl.BlockSpec((1, d), lambda i: (0, 0))
    return pl.pallas_call(
        body, name=name, grid=(t // tm,), in_specs=[row, row, vec],
        out_specs=[row, row, pl.BlockSpec((1, LANES), lambda i: (0, 0)), vec],
        out_shape=[jax.ShapeDtypeStruct((t, d), F32), jax.ShapeDtypeStruct((t, d), BF16),
                   jax.ShapeDtypeStruct((1, LANES), F32), jax.ShapeDtypeStruct((1, d), F32)],
        compiler_params=_cparams("arbitrary"),
    )(h, target, w)


def _swiglu_fwd(g, u, name, tm=512, tn=1408):
    t, n = g.shape

    def body(g_ref, u_ref, o_ref):
        gf = g_ref[...].astype(F32)
        o_ref[...] = (gf * jax.nn.sigmoid(gf) * u_ref[...].astype(F32)).astype(o_ref.dtype)

    blk = pl.BlockSpec((tm, tn), lambda i, j: (i, j))
    return pl.pallas_call(
        body, name=name, grid=(t // tm, n // tn), in_specs=[blk, blk], out_specs=blk,
        out_shape=jax.ShapeDtypeStruct((t, n), BF16), compiler_params=_cparams("parallel", "parallel"),
    )(g, u)


def _swiglu_bwd(g, u, dact, name, tm=512, tn=1408):
    t, n = g.shape

    def body(g_ref, u_ref, d_ref, dg_ref, du_ref):
        gf = g_ref[...].astype(F32)
        uf = u_ref[...].astype(F32)
        df = d_ref[...].astype(F32)
        s = jax.nn.sigmoid(gf)
        dg_ref[...] = (df * uf * s * (1.0 + gf * (1.0 - s))).astype(dg_ref.dtype)
        du_ref[...] = (df * gf * s).astype(du_ref.dtype)

    blk = pl.BlockSpec((tm, tn), lambda i, j: (i, j))
    out = jax.ShapeDtypeStruct((t, n), BF16)
    return pl.pallas_call(
        body, name=name, grid=(t // tm, n // tn), in_specs=[blk, blk, blk], out_specs=[blk, blk],
        out_shape=[out, out], compiler_params=_cparams("parallel", "parallel"),
    )(g, u, dact)


M_BLOCK = 512
M_CHUNKS_PER_BLOCK = M_BLOCK // M_CHUNK


def _mlstm_chunk_fwd(q, k, v, i_col, f_col, c_prev, n_prev, m_prev):
    L = M_CHUNK
    tt = lax.broadcasted_iota(jnp.int32, (L, L), 0)
    ss = lax.broadcasted_iota(jnp.int32, (L, L), 1)
    eye = tt == ss
    causal = ss <= tt

    def to_row(col):
        return jnp.sum(jnp.where(eye, col, 0.0), axis=0, keepdims=True)

    lf_col = jnp.minimum(f_col, 0.0) - jnp.log(1.0 + jnp.exp(-jnp.abs(f_col)))
    lf_row = to_row(lf_col)
    li_row = to_row(i_col)
    b_col = jnp.sum(jnp.where(causal, lf_row, 0.0), axis=1, keepdims=True)
    b_row = jnp.sum(jnp.where(tt <= ss, lf_col, 0.0), axis=0, keepdims=True)
    g = jnp.sum(lf_col, axis=0, keepdims=True)
    a_row = g - b_row + li_row
    a_col = g - b_col + i_col
    m_loc = jnp.max(a_row, axis=1, keepdims=True)
    d_log = jnp.where(causal, b_col - b_row + li_row, NEG_INF)
    inter = b_col + m_prev
    m_t = jnp.maximum(jnp.max(d_log, axis=1, keepdims=True), inter)
    dmat = jnp.exp(d_log - m_t)
    amat = lax.dot_general(q, k, _DIMS["nt"], preferred_element_type=F32)
    p = dmat * amat
    sig = jnp.exp(inter - m_t)
    c_prev_b = c_prev.astype(BF16)
    qc = jnp.dot(q, c_prev_b, preferred_element_type=F32)
    num = jnp.dot(p.astype(BF16), v, preferred_element_type=F32) + sig * qc
    qn = jnp.sum(q.astype(F32) * n_prev, axis=1, keepdims=True)
    den_raw = jnp.sum(p, axis=1, keepdims=True) + sig * qn
    floor = jnp.exp(-m_t)
    den = jnp.maximum(jnp.abs(den_raw), floor)
    h = num / den
    m_new = jnp.maximum(g + m_prev, m_loc)
    w_col = jnp.exp(a_col - m_new)
    alpha = jnp.exp(g + m_prev - m_new)
    return dict(eye=eye, causal=causal, tt=tt, ss=ss, f_col=f_col, dmat=dmat, amat=amat, p=p, sig=sig, qc=qc,
                qn=qn, num=num, den_raw=den_raw, floor=floor, den=den, h=h, m_new=m_new, w_col=w_col, alpha=alpha,
                c_prev_b=c_prev_b)


def _head_gates(gc, hd):
    return gc[:, hd:hd + 1], gc[:, M_HEADS + hd:M_HEADS + hd + 1]


def _mlstm_fwd(m4, gates, bias, norm_w, name):
    t = m4.shape[0]
    nblk = t // M_BLOCK
    nc = t // M_CHUNK
    L, dh = M_CHUNK, M_HEAD_DIM

    def body(m4_ref, g_ref, b_ref, w_ref, hm_ref, cp_ref, nm_ref, c_s, n_s, m_s):
        @pl.when(pl.program_id(0) == 0)
        def _():
            c_s[...] = jnp.zeros_like(c_s)
            n_s[...] = jnp.zeros_like(n_s)
            m_s[...] = jnp.zeros_like(m_s)

        row8 = lax.broadcasted_iota(jnp.int32, (8, dh), 0)

        def chunk(c, carry):
            rows = pl.ds(pl.multiple_of(c * L, L), L)
            gc = g_ref[rows, :] + b_ref[...]
            for hd in range(M_HEADS):
                q = m4_ref[rows, hd * dh:(hd + 1) * dh]
                k = m4_ref[rows, M_WIDTH + hd * dh:M_WIDTH + (hd + 1) * dh]
                v = m4_ref[rows, 2 * M_WIDTH + hd * dh:2 * M_WIDTH + (hd + 1) * dh]
                o = m4_ref[rows, 3 * M_WIDTH + hd * dh:3 * M_WIDTH + (hd + 1) * dh].astype(F32)
                i_col, f_col = _head_gates(gc, hd)
                c_prev = c_s[hd]
                n_prev = n_s[hd, 0:1, :]
                m_prev = m_s[hd, 0:1, 0:1]
                r = _mlstm_chunk_fwd(q, k, v, i_col, f_col, c_prev, n_prev, m_prev)
                h = r["h"]
                hn = h * lax.rsqrt(jnp.mean(h * h, axis=-1, keepdims=True) + EPS) * w_ref[hd:hd + 1, :]
                hm_ref[rows, hd * dh:(hd + 1) * dh] = (hn * jax.nn.sigmoid(o)).astype(hm_ref.dtype)
                cp_ref[hd, c] = r["c_prev_b"]
                nm_ref[hd, c] = jnp.where(row8 == 0, n_prev, jnp.where(row8 == 1, m_prev, 0.0))
                kw = k.astype(F32) * r["w_col"]
                c_s[hd] = r["alpha"] * c_prev + lax.dot_general(kw.astype(BF16), v, _DIMS["tn"],
                                                                preferred_element_type=F32)
                n_s[hd, 0:1, :] = r["alpha"] * n_prev + jnp.sum(kw, axis=0, keepdims=True)
                m_s[hd] = jnp.broadcast_to(r["m_new"], (8, dh))
            return carry

        lax.fori_loop(0, M_CHUNKS_PER_BLOCK, chunk, 0)

    return pl.pallas_call(
        body, name=name, grid=(nblk,),
        in_specs=[pl.BlockSpec((M_BLOCK, 4 * M_WIDTH), lambda i: (i, 0)),
                  pl.BlockSpec((M_BLOCK, LANES), lambda i: (i, 0)),
                  pl.BlockSpec((1, LANES), lambda i: (0, 0)),
                  pl.BlockSpec((M_HEADS, dh), lambda i: (0, 0))],
        out_specs=[pl.BlockSpec((M_BLOCK, M_WIDTH), lambda i: (i, 0)),
                   pl.BlockSpec((M_HEADS, M_CHUNKS_PER_BLOCK, dh, dh), lambda i: (0, i, 0, 0)),
                   pl.BlockSpec((M_HEADS, M_CHUNKS_PER_BLOCK, 8, dh), lambda i: (0, i, 0, 0))],
        out_shape=[jax.ShapeDtypeStruct((t, M_WIDTH), BF16),
                   jax.ShapeDtypeStruct((M_HEADS, nc, dh, dh), BF16),
                   jax.ShapeDtypeStruct((M_HEADS, nc, 8, dh), F32)],
        scratch_shapes=[pltpu.VMEM((M_HEADS, dh, dh), F32), pltpu.VMEM((M_HEADS, 8, dh), F32),
                        pltpu.VMEM((M_HEADS, 8, dh), F32)],
        compiler_params=_cparams("arbitrary"),
    )(m4, gates, bias, norm_w)


def _mlstm_bwd(m4, gates, bias, norm_w, c_prev_all, nm_all, dhm, name):
    t = m4.shape[0]
    nblk = t // M_BLOCK
    L, dh = M_CHUNK, M_HEAD_DIM

    def body(m4_ref, g_ref, b_ref, w_ref, cp_ref, nm_ref, dhm_ref, dm4_ref, dg_ref, small_ref, dc_s, dn_s):
        @pl.when(pl.program_id(0) == 0)
        def _():
            dc_s[...] = jnp.zeros_like(dc_s)
            dn_s[...] = jnp.zeros_like(dn_s)
            small_ref[...] = jnp.zeros_like(small_ref)

        lane = lax.broadcasted_iota(jnp.int32, (L, LANES), 1)
        row8 = lax.broadcasted_iota(jnp.int32, (8, LANES), 0)

        def chunk(ci, carry):
            c = M_CHUNKS_PER_BLOCK - 1 - ci
            rows = pl.ds(pl.multiple_of(c * L, L), L)
            gc = g_ref[rows, :] + b_ref[...]
            dg_tile = jnp.zeros((L, LANES), F32)
            small = jnp.zeros((8, LANES), F32)
            for hd in range(M_HEADS):
                q = m4_ref[rows, hd * dh:(hd + 1) * dh]
                k = m4_ref[rows, M_WIDTH + hd * dh:M_WIDTH + (hd + 1) * dh]
                v = m4_ref[rows, 2 * M_WIDTH + hd * dh:2 * M_WIDTH + (hd + 1) * dh]
                o = m4_ref[rows, 3 * M_WIDTH + hd * dh:3 * M_WIDTH + (hd + 1) * dh].astype(F32)
                i_col, f_col = _head_gates(gc, hd)
                c_prev = cp_ref[hd, c].astype(F32)
                nm = nm_ref[hd, c]
                n_prev = nm[0:1, :]
                m_prev = nm[1:2, 0:1]
                r = _mlstm_chunk_fwd(q, k, v, i_col, f_col, c_prev, n_prev, m_prev)
                eye, tt, ss = r["eye"], r["tt"], r["ss"]
                h, den, sig, p = r["h"], r["den"], r["sig"], r["p"]
                w_col, alpha, c_prev_b = r["w_col"], r["alpha"], r["c_prev_b"]

                def to_row(col):
                    return jnp.sum(jnp.where(eye, col, 0.0), axis=0, keepdims=True)

                def to_col(row):
                    return jnp.sum(jnp.where(eye, row, 0.0), axis=1, keepdims=True)

                w_h = w_ref[hd:hd + 1, :]
                rn = lax.rsqrt(jnp.mean(h * h, axis=-1, keepdims=True) + EPS)
                hh = h * rn
                og = jax.nn.sigmoid(o)
                dhm_c = dhm_ref[rows, hd * dh:(hd + 1) * dh].astype(F32)
                dhn = dhm_c * og
                d_o = dhm_c * hh * w_h * og * (1.0 - og)
                small = small + jnp.where(row8 == hd, jnp.sum(dhn * hh, axis=0, keepdims=True), 0.0)
                dhh = dhn * w_h
                dh_ = rn * (dhh - hh * jnp.mean(dhh * hh, axis=-1, keepdims=True))

                dnum = dh_ / den
                dden = -jnp.sum(dh_ * h, axis=-1, keepdims=True) / den
                dden_raw = jnp.where(jnp.abs(r["den_raw"]) >= r["floor"], dden * jnp.sign(r["den_raw"]), 0.0)
                dnum_b = dnum.astype(BF16)
                dp = lax.dot_general(dnum_b, v, _DIMS["nt"], preferred_element_type=F32) + dden_raw
                p_b = p.astype(BF16)
                dc = dc_s[hd]
                dn = dn_s[hd, 0:1, :]
                dc_b = dc.astype(BF16)
                g2 = sig * dnum
                g2_b = g2.astype(BF16)
                sd = sig * dden_raw
                da_mat = (dp * r["dmat"]).astype(BF16)
                dqs = (lax.dot_general(g2_b, c_prev_b, _DIMS["nt"], preferred_element_type=F32)
                       + sd * n_prev + jnp.dot(da_mat, k, preferred_element_type=F32))
                r_mat = lax.dot_general(v, dc_b, _DIMS["nt"], preferred_element_type=F32) + dn
                kf = k.astype(F32)
                d_k = lax.dot_general(da_mat, q, _DIMS["tn"], preferred_element_type=F32) + w_col * r_mat
                d_v = (lax.dot_general(p_b, dnum_b, _DIMS["tn"], preferred_element_type=F32)
                       + w_col * jnp.dot(k, dc_b, preferred_element_type=F32))
                d_omega = jnp.sum(kf * r_mat, axis=-1, keepdims=True)
                da_col = d_omega * w_col
                dsig = jnp.sum(dnum * r["qc"], axis=-1, keepdims=True) + dden_raw * r["qn"]
                pp = dp * p
                r1 = jnp.sum(pp, axis=1, keepdims=True)
                c1_col = to_col(jnp.sum(pp, axis=0, keepdims=True))
                d_alpha = (jnp.sum(jnp.sum(dc * c_prev, axis=1, keepdims=True), axis=0, keepdims=True)
                           + jnp.sum(dn * n_prev, axis=1, keepdims=True))
                dgl = jnp.sum(da_col, axis=0, keepdims=True) + d_alpha * alpha
                db_col = r1 + dsig * sig - c1_col - da_col
                dli_col = c1_col + da_col
                db_row = to_row(db_col)
                dlf_col = jnp.sum(jnp.where(ss >= tt, db_row, 0.0), axis=1, keepdims=True) + dgl
                df_col = dlf_col * jax.nn.sigmoid(-r["f_col"])

                dc_s[hd] = alpha * dc + lax.dot_general(q, g2_b, _DIMS["tn"], preferred_element_type=F32)
                dn_s[hd, 0:1, :] = alpha * dn + jnp.sum(sd * q.astype(F32), axis=0, keepdims=True)

                dm4_ref[rows, hd * dh:(hd + 1) * dh] = (dqs * M_SCALE).astype(dm4_ref.dtype)
                dm4_ref[rows, M_WIDTH + hd * dh:M_WIDTH + (hd + 1) * dh] = d_k.astype(dm4_ref.dtype)
                dm4_ref[rows, 2 * M_WIDTH + hd * dh:2 * M_WIDTH + (hd + 1) * dh] = d_v.astype(dm4_ref.dtype)
                dm4_ref[rows, 3 * M_WIDTH + hd * dh:3 * M_WIDTH + (hd + 1) * dh] = d_o.astype(dm4_ref.dtype)
                dg_tile = dg_tile + jnp.where(lane == hd, dli_col, 0.0) + jnp.where(lane == M_HEADS + hd, df_col, 0.0)
            dg_ref[rows, :] = dg_tile
            small = small + jnp.where(row8 == M_HEADS, jnp.sum(dg_tile, axis=0, keepdims=True), 0.0)
            small_ref[...] += small
            return carry

        lax.fori_loop(0, M_CHUNKS_PER_BLOCK, chunk, 0)

    rev = lambda i: nblk - 1 - i
    return pl.pallas_call(
        body, name=name, grid=(nblk,),
        in_specs=[pl.BlockSpec((M_BLOCK, 4 * M_WIDTH), lambda i: (rev(i), 0)),
                  pl.BlockSpec((M_BLOCK, LANES), lambda i: (rev(i), 0)),
                  pl.BlockSpec((1, LANES), lambda i: (0, 0)),
                  pl.BlockSpec((M_HEADS, dh), lambda i: (0, 0)),
                  pl.BlockSpec((M_HEADS, M_CHUNKS_PER_BLOCK, dh, dh), lambda i: (0, rev(i), 0, 0)),
                  pl.BlockSpec((M_HEADS, M_CHUNKS_PER_BLOCK, 8, dh), lambda i: (0, rev(i), 0, 0)),
                  pl.BlockSpec((M_BLOCK, M_WIDTH), lambda i: (rev(i), 0))],
        out_specs=[pl.BlockSpec((M_BLOCK, 4 * M_WIDTH), lambda i: (rev(i), 0)),
                   pl.BlockSpec((M_BLOCK, LANES), lambda i: (rev(i), 0)),
                   pl.BlockSpec((8, LANES), lambda i: (0, 0))],
        out_shape=[jax.ShapeDtypeStruct((t, 4 * M_WIDTH), BF16), jax.ShapeDtypeStruct((t, LANES), F32),
                   jax.ShapeDtypeStruct((8, LANES), F32)],
        scratch_shapes=[pltpu.VMEM((M_HEADS, dh, dh), F32), pltpu.VMEM((M_HEADS, 8, dh), F32)],
        compiler_params=_cparams("arbitrary"),
    )(m4, gates, bias, norm_w, c_prev_all, nm_all, dhm)


def _rope_tables(pos_col, width):
    lane = lax.broadcasted_iota(jnp.int32, (1, width), 1) % A_QK_PAD
    half = A_ROPE // 2
    first = (lane >= A_NOPE) & (lane < A_NOPE + half)
    second = (lane >= A_NOPE + half) & (lane < A_NOPE + A_ROPE)
    idx = jnp.where(first, lane - A_NOPE, lane - A_NOPE - half).astype(F32)
    inv_freq = jnp.exp(idx * (-math.log(ROPE_THETA) / half))
    ang = pos_col.astype(F32) * inv_freq
    cos, sin = jnp.cos(ang), jnp.sin(ang)
    rot = first | second
    return jnp.where(rot, cos, 1.0), jnp.where(first, -sin, 0.0), jnp.where(second, sin, 0.0)


def _rope_apply(vv, cosf, s1, s2):
    half = A_ROPE // 2
    w = vv.shape[-1]
    return vv * cosf + pltpu.roll(vv, w - half, 1) * s1 + pltpu.roll(vv, half, 1) * s2


def _rope_apply_t(dd, cosf, s1, s2):
    half = A_ROPE // 2
    w = dd.shape[-1]
    return dd * cosf + pltpu.roll(dd * s1, half, 1) + pltpu.roll(dd * s2, w - half, 1)


A_BIAS_LANE_K = A_NOPE + A_ROPE
A_BIAS_LANE_V = A_V


def _hi_lo(val):
    hi = val.astype(BF16)
    return hi, (val - hi.astype(F32)).astype(BF16)


def _rope_fwd(qpre, kpre, vpre, gates, pos, name, tm=512):
    t, w = qpre.shape

    def body(q_ref, k_ref, v_ref, g_ref, p_ref, qo_ref, ko_ref, vo_ref):
        cosf, s1, s2 = _rope_tables(p_ref[...], LANES)
        lane = lax.broadcasted_iota(jnp.int32, (1, LANES), 1)
        kr = jnp.where((lane >= A_NOPE) & (lane < A_NOPE + A_ROPE), g_ref[...], 0.0)
        kr = _rope_apply(kr, cosf, s1, s2)
        kr = jnp.where((lane == A_BIAS_LANE_K) | (lane == A_BIAS_LANE_K + 1), 1.0, kr)
        v_one = (lane == A_BIAS_LANE_V) | (lane == A_BIAS_LANE_V + 1)
        for hd in range(A_HEADS):
            sl = slice(hd * LANES, (hd + 1) * LANES)
            qo_ref[:, sl] = (_rope_apply(q_ref[:, sl], cosf, s1, s2) * ATTN_SCALE).astype(qo_ref.dtype)
            ko_ref[:, sl] = (k_ref[:, sl] + kr).astype(ko_ref.dtype)
            vo_ref[:, sl] = jnp.where(v_one, jnp.ones((), BF16), v_ref[:, sl])

    row = pl.BlockSpec((tm, w), lambda i: (i, 0))
    out = jax.ShapeDtypeStruct((t, w), BF16)
    return pl.pallas_call(
        body, name=name, grid=(t // tm,),
        in_specs=[row, row, row, pl.BlockSpec((tm, LANES), lambda i: (i, 0)), pl.BlockSpec((tm, 1), lambda i: (i, 0))],
        out_specs=[row, row, row], out_shape=[out, out, out], compiler_params=_cparams("parallel"),
    )(qpre, kpre, vpre, gates, pos)


def _rope_bwd(dq, dk, dgates, pos, name, tm=512):
    t, w = dq.shape

    def body(dq_ref, dk_ref, dg_ref, p_ref, dqo_ref, dgo_ref):
        cosf, s1, s2 = _rope_tables(p_ref[...], LANES)
        acc = jnp.zeros((tm, LANES), F32)
        for hd in range(A_HEADS):
            sl = slice(hd * LANES, (hd + 1) * LANES)
            dqo_ref[:, sl] = (_rope_apply_t(dq_ref[:, sl], cosf, s1, s2) * ATTN_SCALE).astype(dqo_ref.dtype)
            acc = acc + dk_ref[:, sl].astype(F32)
        lane = lax.broadcasted_iota(jnp.int32, (1, LANES), 1)
        dkr = _rope_apply_t(acc, cosf, s1, s2)
        dkr = jnp.where((lane >= A_NOPE) & (lane < A_NOPE + A_ROPE), dkr, 0.0)
        dgo_ref[...] = (dg_ref[...] + dkr).astype(dgo_ref.dtype)

    row = pl.BlockSpec((tm, w), lambda i: (i, 0))
    nar = pl.BlockSpec((tm, LANES), lambda i: (i, 0))
    return pl.pallas_call(
        body, name=name, grid=(t // tm,),
        in_specs=[row, row, nar, pl.BlockSpec((tm, 1), lambda i: (i, 0))],
        out_specs=[row, nar],
        out_shape=[jax.ShapeDtypeStruct((t, w), BF16), jax.ShapeDtypeStruct((t, LANES), BF16)],
        compiler_params=_cparams("parallel"),
    )(dq, dk, dgates, pos)


A_TQ = 512
A_TK = 512


A_HEADS_PER_STEP = 2


def _flash_fwd(q, k, v, name):
    t = q.shape[0]
    tq, tk = A_TQ, A_TK
    nh = A_HEADS_PER_STEP
    wblk = nh * LANES

    def body(q_ref, k_ref, v_ref, o_ref, qb_ref):
        i = pl.program_id(1)
        lane = lax.broadcasted_iota(jnp.int32, (tq, LANES), 1)
        qpos = i * tq + lax.broadcasted_iota(jnp.int32, (tq, tk), 0)
        kofs = lax.broadcasted_iota(jnp.int32, (tq, tk), 1)
        qs = [q_ref[:, hh * LANES:(hh + 1) * LANES] for hh in range(nh)]

        def step(kb, carry, masked):
            rows = pl.ds(pl.multiple_of(kb * tk, tk), tk)
            new = []
            for hh in range(nh):
                m, acc = carry[hh]
                kh = k_ref[rows, hh * LANES:(hh + 1) * LANES]
                s = lax.dot_general(qs[hh], kh, _DIMS["nt"], preferred_element_type=F32)
                if masked:
                    s = jnp.where(kb * tk + kofs <= qpos, s, NEG_INF)
                m_new = jnp.maximum(m, jnp.max(s, axis=1, keepdims=True))
                p = jnp.exp(s - m_new)
                acc = jnp.exp(m - m_new) * acc + jnp.dot(p.astype(BF16), v_ref[rows, hh * LANES:(hh + 1) * LANES],
                                                          preferred_element_type=F32)
                new.append((m_new, acc))
            return tuple(new)

        carry = tuple((jnp.full((tq, 1), NEG_INF, F32), jnp.zeros((tq, LANES), F32)) for _ in range(nh))
        n_full = (i * tq) // tk
        carry = lax.fori_loop(0, n_full, functools.partial(step, masked=False), carry)
        for d in range(tq // tk):
            carry = step(n_full + d, carry, True)
        for hh in range(nh):
            m, acc = carry[hh]
            l = acc[:, A_BIAS_LANE_V:A_BIAS_LANE_V + 1]
            o_ref[:, hh * LANES:(hh + 1) * LANES] = (acc / l).astype(o_ref.dtype)
            hi, lo = _hi_lo(-(m + jnp.log(l)))
            qb_ref[:, hh * LANES:(hh + 1) * LANES] = jnp.where(
                lane == A_BIAS_LANE_K, hi, jnp.where(lane == A_BIAS_LANE_K + 1, lo, qs[hh]))

    blk = pl.BlockSpec((tq, wblk), lambda j, i: (i, j))
    res = pl.BlockSpec((t, wblk), lambda j, i: (0, j))
    out = jax.ShapeDtypeStruct((t, A_HEADS * LANES), BF16)
    return pl.pallas_call(
        body, name=name, grid=(A_HEADS // nh, t // tq), in_specs=[blk, res, res], out_specs=[blk, blk],
        out_shape=[out, out], compiler_params=_cparams("parallel", "parallel"),
    )(q, k, v)


def _attn_delta(do, o, name, tm=512):
    t, w = do.shape

    def body(do_ref, o_ref, out_ref):
        lane = lax.broadcasted_iota(jnp.int32, (tm, LANES), 1)
        for hd in range(A_HEADS):
            sl = slice(hd * LANES, (hd + 1) * LANES)
            d = do_ref[:, sl]
            delta = jnp.sum(jnp.where(lane < A_V, d.astype(F32) * o_ref[:, sl].astype(F32), 0.0), axis=1, keepdims=True)
            hi, lo = _hi_lo(-delta)
            out_ref[:, sl] = jnp.where(lane == A_BIAS_LANE_V, hi, jnp.where(lane == A_BIAS_LANE_V + 1, lo, d))

    row = pl.BlockSpec((tm, w), lambda i: (i, 0))
    return pl.pallas_call(
        body, name=name, grid=(t // tm,), in_specs=[row, row], out_specs=row,
        out_shape=jax.ShapeDtypeStruct((t, w), BF16), compiler_params=_cparams("parallel"),
    )(do, o)


def _flash_bwd(qb, k, v, doe, name):
    t = qb.shape[0]
    tq, tk = A_TQ, A_TK
    assert tq == tk
    nh = A_HEADS_PER_STEP
    wblk = nh * LANES
    nq = t // tq

    def body(q_ref, k_ref, v_ref, do_ref, dq_ref, dk_ref, dv_ref):
        kb = pl.program_id(1)

        @pl.when(kb == 0)
        def _():
            dq_ref[...] = jnp.zeros_like(dq_ref)

        kpos = kb * tk + lax.broadcasted_iota(jnp.int32, (tk, tq), 0)
        qofs = lax.broadcasted_iota(jnp.int32, (tk, tq), 1)
        ks = [k_ref[:, hh * LANES:(hh + 1) * LANES] for hh in range(nh)]
        vs = [v_ref[:, hh * LANES:(hh + 1) * LANES] for hh in range(nh)]

        def step(qi, carry, masked):
            rows = pl.ds(pl.multiple_of(qi * tq, tq), tq)
            new = []
            for hh in range(nh):
                dk_acc, dv_acc = carry[hh]
                qh = q_ref[rows, hh * LANES:(hh + 1) * LANES]
                doh = do_ref[rows, hh * LANES:(hh + 1) * LANES]
                p_t = jnp.exp(lax.dot_general(ks[hh], qh, _DIMS["nt"], preferred_element_type=F32))
                if masked:
                    p_t = jnp.where(qi * tq + qofs >= kpos, p_t, 0.0)
                ds_t = (p_t * lax.dot_general(vs[hh], doh, _DIMS["nt"], preferred_element_type=F32)).astype(BF16)
                dv_acc = dv_acc + jnp.dot(p_t.astype(BF16), doh, preferred_element_type=F32)
                dk_acc = dk_acc + jnp.dot(ds_t, qh, preferred_element_type=F32)
                dq_ref[rows, hh * LANES:(hh + 1) * LANES] += lax.dot_general(ds_t, ks[hh], _DIMS["tn"],
                                                                             preferred_element_type=F32)
                new.append((dk_acc, dv_acc))
            return tuple(new)

        carry = tuple((jnp.zeros((tk, LANES), F32), jnp.zeros((tk, LANES), F32)) for _ in range(nh))
        carry = step(kb, carry, True)
        carry = lax.fori_loop(kb + 1, nq, functools.partial(step, masked=False), carry)
        for hh in range(nh):
            dk_ref[:, hh * LANES:(hh + 1) * LANES] = carry[hh][0].astype(dk_ref.dtype)
            dv_ref[:, hh * LANES:(hh + 1) * LANES] = carry[hh][1].astype(dv_ref.dtype)

    res = pl.BlockSpec((t, wblk), lambda j, kb: (0, j))
    blk = pl.BlockSpec((tk, wblk), lambda j, kb: (kb, j))
    wide = (t, A_HEADS * LANES)
    return pl.pallas_call(
        body, name=name, grid=(A_HEADS // nh, t // tk), in_specs=[res, blk, blk, res], out_specs=[res, blk, blk],
        out_shape=[jax.ShapeDtypeStruct(wide, F32), jax.ShapeDtypeStruct(wide, BF16), jax.ShapeDtypeStruct(wide, BF16)],
        compiler_params=_cparams("parallel", "arbitrary"),
    )(qb, k, v, doe)


_SPLITS = (M_WIDTH, M_WIDTH, M_WIDTH, M_WIDTH, M_HEADS, M_HEADS, Q_RANK, KV_RANK, A_ROPE)
_OFFS = tuple(sum(_SPLITS[:i]) for i in range(len(_SPLITS) + 1))
_GATE_BLOCK_KR = A_NOPE


def _split_w_in_t(wt):
    z = lambda n: jnp.zeros((n, wt.shape[1]), wt.dtype)
    w_g = jnp.concatenate([wt[_OFFS[4]:_OFFS[6]], z(_GATE_BLOCK_KR - 2 * M_HEADS), wt[_OFFS[8]:_OFFS[9]],
                           z(LANES - _GATE_BLOCK_KR - A_ROPE)], axis=0)
    return wt[_OFFS[6]:_OFFS[8]], w_g


def _merge_w_in_grad_t(g_m4, g_a2, g_g):
    return jnp.concatenate([g_m4, g_g[:2 * M_HEADS], g_a2, g_g[_GATE_BLOCK_KR:_GATE_BLOCK_KR + A_ROPE]], axis=0)


def _pad_heads(w, used):
    w3 = w.reshape(A_HEADS, used, w.shape[1])
    return jnp.pad(w3, ((0, 0), (0, LANES - used), (0, 0))).reshape(A_HEADS * LANES, w.shape[1])


def _unpad_heads(g, used):
    return g.reshape(A_HEADS, LANES, g.shape[1])[:, :used].reshape(A_HEADS * used, g.shape[1])


def _split_w_kv_b_t(wt):
    w3 = wt.reshape(A_HEADS, A_NOPE + A_V, wt.shape[1])
    pad = lambda part: jnp.pad(part, ((0, 0), (0, LANES - part.shape[1]), (0, 0))).reshape(A_HEADS * LANES, wt.shape[1])
    return pad(w3[:, :A_NOPE]), pad(w3[:, A_NOPE:])


def _merge_w_kv_b_grad_t(gk, gv):
    gk3 = gk.reshape(A_HEADS, LANES, gk.shape[1])[:, :A_NOPE]
    gv3 = gv.reshape(A_HEADS, LANES, gv.shape[1])[:, :A_V]
    return jnp.concatenate([gk3, gv3], axis=1).reshape(A_HEADS * (A_NOPE + A_V), gk.shape[1])


def _local_step(x, pos, target, nw, w):
    w_in = w["w_in"]
    w_a2, w_g = _split_w_in_t(w_in)
    w_qb = _pad_heads(w["w_q_b"], A_NOPE + A_ROPE)
    w_k, w_v = _split_w_kv_b_t(w["w_kv_b"])
    w_out_m, w_out_a = w["w_out"][:M_WIDTH], _pad_heads(w["w_out"][M_WIDTH:], A_V)
    w_gate, w_up, w_down = w["w_gate"], w["w_up"], w["w_down"]
    bias = jnp.pad(nw["b_gates"], ((0, 0), (0, LANES - 2 * M_HEADS)))
    mnorm = nw["mlstm_norm_w"].reshape(M_HEADS, M_HEAD_DIM)
    n_m4 = 4 * M_WIDTH

    u1 = _rmsnorm_fwd(x, nw["attn_norm_w"], "attn_norm")
    m4 = _matmul(u1, w_in, "nt", BF16, "proj_mlstm", tn=M_WIDTH, b_rows=n_m4, lead_scale=(M_WIDTH, M_SCALE))
    a2 = _matmul(u1, w_a2, "nt", BF16, "proj_latents", tn=640)
    gates = _matmul(u1, w_g, "nt", F32, "proj_gates")
    hm, c_prev_all, nm_all = _mlstm_fwd(m4, gates, bias, mnorm, "mlstm_fwd")
    qa, kva = a2[:, :Q_RANK], a2[:, Q_RANK:]
    qa_n = _rmsnorm_fwd(qa, nw["q_a_norm_w"], "q_a_norm")
    kv_n = _rmsnorm_fwd(kva, nw["kv_a_norm_w"], "kv_a_norm")
    qpre = _matmul(qa_n, w_qb, "nt", F32, "q_up")
    kpre = _matmul(kv_n, w_k, "nt", F32, "k_up")
    vpre = _matmul(kv_n, w_v, "nt", BF16, "v_up")
    qr, kr, vv = _rope_fwd(qpre, kpre, vpre, gates, pos, "rope_fwd")
    ha, qb = _flash_fwd(qr, kr, vv, "attn_fwd")
    h1 = _matmul(hm, w_out_m, "nn", F32, "out_proj_m", residual=x)
    h1 = _matmul(ha, w_out_a, "nn", F32, "out_proj_a", residual=h1)
    u2 = _rmsnorm_fwd(h1, nw["ffn_norm_w"], "ffn_norm")
    gg = _matmul(u2, w_gate, "nt", BF16, "ffn_gate", tn=1408)
    uu = _matmul(u2, w_up, "nt", BF16, "ffn_up", tn=1408)
    act = _swiglu_fwd(gg, uu, "swiglu_fwd")
    h2 = _matmul(act, w_down, "nn", F32, "ffn_down", residual=h1)

    dh2, dh2_b, loss, g_final = _final_loss(h2, target, nw["final_norm_w"].reshape(1, D_MODEL), "final_loss")
    grads = {"final_norm_w": g_final.reshape(D_MODEL)}
    dact = _matmul(dh2_b, w_down, "nt", BF16, "d_act", tn=1408)
    grads["w_down"] = _matmul(act, dh2_b, "tn", F32, "g_w_down", tm=256, tn=D_MODEL)
    dgg, duu = _swiglu_bwd(gg, uu, dact, "swiglu_bwd")
    du2 = _matmul(dgg, w_gate, "nn", F32, "d_u2_gate")
    du2 = _matmul(duu, w_up, "nn", F32, "d_u2_up", residual=du2)
    grads["w_gate"] = _matmul(dgg, u2, "tn", F32, "g_w_gate", tm=256, tn=D_MODEL)
    grads["w_up"] = _matmul(duu, u2, "tn", F32, "g_w_up", tm=256, tn=D_MODEL)
    dh1, dh1_b, grads["ffn_norm_w"] = _rmsnorm_bwd(du2, h1, nw["ffn_norm_w"], "ffn_norm_bwd", F32, residual=dh2,
                                                   also_bf16=True)
    dhm = _matmul(dh1_b, w_out_m, "nt", BF16, "d_hm")
    dha = _matmul(dh1_b, w_out_a, "nt", BF16, "d_ha")
    grads["w_out"] = jnp.concatenate([_matmul(hm, dh1_b, "tn", F32, "g_w_out_m", tn=D_MODEL),
                                      _unpad_heads(_matmul(ha, dh1_b, "tn", F32, "g_w_out_a", tn=D_MODEL), A_V)], axis=0)

    dqr, dkr, dvv = _flash_bwd(qb, kr, vv, _attn_delta(dha, ha, "attn_delta"), "attn_bwd")
    dm4, dgates, small = _mlstm_bwd(m4, gates, bias, mnorm, c_prev_all, nm_all, dhm, "mlstm_bwd")
    grads["mlstm_norm_w"] = small[:M_HEADS].reshape(1, M_HEADS, M_HEAD_DIM)
    grads["b_gates"] = small[M_HEADS:M_HEADS + 1, :2 * M_HEADS]
    dqpre, dgk = _rope_bwd(dqr, dkr, dgates, pos, "rope_bwd")
    dqa_n = _matmul(dqpre, w_qb, "nn", BF16, "d_qa_n")
    grads["w_q_b"] = _unpad_heads(_matmul(dqpre, qa_n, "tn", F32, "g_w_q_b"), A_NOPE + A_ROPE)
    dkv_n = _matmul(dkr, w_k, "nn", F32, "d_kv_n_k")
    dkv_n = _matmul(dvv, w_v, "nn", BF16, "d_kv_n_v", residual=dkv_n)
    grads["w_kv_b"] = _merge_w_kv_b_grad_t(_matmul(dkr, kv_n, "tn", F32, "g_w_k"),
                                           _matmul(dvv, kv_n, "tn", F32, "g_w_v"))
    dqa, grads["q_a_norm_w"] = _rmsnorm_bwd(dqa_n, qa, nw["q_a_norm_w"], "q_a_norm_bwd", BF16)
    dkva, grads["kv_a_norm_w"] = _rmsnorm_bwd(dkv_n, kva, nw["kv_a_norm_w"], "kv_a_norm_bwd", BF16)
    da2 = jnp.concatenate([dqa, dkva], axis=1)
    du1 = _matmul(dm4, w_in, "nn", F32, "d_u1_m4", b_rows=n_m4)
    du1 = _matmul(da2, w_a2, "nn", F32, "d_u1_a2", residual=du1)
    du1 = _matmul(dgk, w_g, "nn", F32, "d_u1_g", residual=du1)
    grads["w_in"] = _merge_w_in_grad_t(_matmul(dm4, u1, "tn", F32, "g_w_m4", tn=D_MODEL),
                                       _matmul(da2, u1, "tn", F32, "g_w_a2", tm=640, tn=D_MODEL),
                                       _matmul(dgk, u1, "tn", F32, "g_w_g", tn=D_MODEL))
    grad_x, grads["attn_norm_w"] = _rmsnorm_bwd(du1, x, nw["attn_norm_w"], "attn_norm_bwd", F32, residual=dh1)
    return loss, grad_x, grads


MESH = pl.DeviceIdType.MESH
N_CHIPS = 4
EARLY = ("w_in", "w_q_b", "w_kv_b")
LATE = ("w_out", "w_gate", "w_up", "w_down")
BIG = EARLY + LATE
TRANSPOSED = ("w_in", "w_q_b", "w_kv_b", "w_gate", "w_up")
LANE_HALVED = ("w_in", "w_out", "w_gate", "w_up", "w_down")
SMALL = ("attn_norm_w", "b_gates", "mlstm_norm_w", "q_a_norm_w", "kv_a_norm_w", "ffn_norm_w", "final_norm_w")
HBM_SPEC = pl.BlockSpec(memory_space=pltpu.HBM)


def _stored(name, a):
    return a[0].T if name in TRANSPOSED else a[0]


def _unstored(name, a):
    return (a.T if name in TRANSPOSED else a)[None]


def _half_shape(name, shape):
    rs, cs = shape
    return (rs, cs // 2) if name in LANE_HALVED else (rs // 2, cs)


def _half(ref, name, h, *lead):
    rs, cs = ref.shape[-2:]
    if name in LANE_HALVED:
        return ref.at[(*lead, slice(None), pl.ds(h * (cs // 2), cs // 2))]
    return ref.at[(*lead, pl.ds(h * (rs // 2), rs // 2), slice(None))]


def _place():
    x, y, c = lax.axis_index("x"), lax.axis_index("y"), lax.axis_index("c")
    others = [(1 - x, y), (x, 1 - y), (1 - x, 1 - y)]
    return x, y, c, others


def _gather_weights(names, shards):
    n = len(shards)

    def body(*refs):
        _gather_body(names, refs[:n], refs[n:2 * n], *refs[2 * n:])

    gathered = pl.pallas_call(
        body, name="gather_weights", in_specs=[HBM_SPEC] * n, out_specs=[HBM_SPEC] * n,
        out_shape=[jax.ShapeDtypeStruct((N_CHIPS,) + s.shape, s.dtype) for s in shards],
        scratch_shapes=[pltpu.SemaphoreType.DMA((6 * n,)), pltpu.SemaphoreType.DMA((6 * n,))],
    )(*shards)
    return _with_own_slab(gathered, shards)


def _with_own_slab(gathered, shards):
    me = 2 * lax.axis_index("x") + lax.axis_index("y")
    return [lax.dynamic_update_slice(g, s[None], (me, 0, 0)) for g, s in zip(gathered, shards)]


def _gather_body(names, ins, outs, send_sems, recv_sems):
    x, y, c, others = _place()
    me = 2 * x + y
    sibling = (x, y, 1 - c)

    def copy(w, k, slab, core, to, src=None):
        dst = _half(outs[w], names[w], core, slab)
        return pltpu.make_async_remote_copy(
            src_ref=dst if src is None else src, dst_ref=dst, send_sem=send_sems.at[w * 6 + k],
            recv_sem=recv_sems.at[w * 6 + k], device_id=to, device_id_type=MESH)

    sends = []
    for w in range(len(names)):
        for j, chip in enumerate(others):
            cp = copy(w, j, me, c, (*chip, c), src=_half(ins[w], names[w], c))
            cp.start()
            sends.append(cp)
    for w in range(len(names)):
        for j, (ox, oy) in enumerate(others):
            slab = 2 * ox + oy
            copy(w, j, slab, c, (x, y, c)).wait_recv()
            fwd = copy(w, 3 + j, slab, c, sibling)
            fwd.start()
            sends.append(fwd)
    for w in range(len(names)):
        for j, (ox, oy) in enumerate(others):
            copy(w, 3 + j, 2 * ox + oy, 1 - c, (x, y, c)).wait_recv()
    for cp in sends:
        cp.wait_send()


GATHER_LATE_COLLECTIVE_ID = 1


def _gather_weights_async(names, shards):
    n = len(shards)
    src = [jax.new_ref(s, memory_space=pltpu.MemorySpace.HBM) for s in shards]
    dst = [jax.empty_ref(jax.ShapeDtypeStruct((N_CHIPS,) + s.shape, s.dtype), memory_space=pltpu.MemorySpace.HBM)
           for s in shards]

    @pl.kernel(mesh=plsc.ScalarSubcoreMesh(axis_name="sequencer", num_cores=1), name="gather_weights_async",
               scratch_types=(pltpu.SemaphoreType.DMA((6 * n,)), pltpu.SemaphoreType.DMA((6 * n,))),
               compiler_params=pltpu.CompilerParams(collective_id=GATHER_LATE_COLLECTIVE_ID))
    def launch(send_sems, recv_sems):
        x, y, c, others = _place()
        peers = [(ox, oy, c) for ox, oy in others] + [(x, y, 1 - c)]
        barrier = pltpu.get_barrier_semaphore()
        for peer in peers:
            pl.semaphore_signal(barrier, inc=1, device_id=peer, device_id_type=MESH)
        pl.semaphore_wait(barrier, len(peers))
        _gather_body(names, src, dst, send_sems, recv_sems)

    launch()
    return _with_own_slab([d[...] for d in dst], shards)


def _exchange(arrays, out_shapes, plan, copies_per_array, name):
    n = len(arrays)

    def body(*refs):
        ins, outs = refs[:n], refs[n:2 * n]
        send_sems, recv_sems = refs[2 * n:]
        rem = [pltpu.make_async_remote_copy(src_ref=s, dst_ref=d, send_sem=send_sems.at[k], recv_sem=recv_sems.at[k],
                                            device_id=to, device_id_type=MESH)
               for k, (s, d, to) in enumerate(plan(ins, outs, _place()))]
        for cp in rem:
            cp.start()
        for cp in rem:
            cp.wait_recv()
        for cp in rem:
            cp.wait_send()

    return pl.pallas_call(
        body, name=name, in_specs=[HBM_SPEC] * n, out_specs=[HBM_SPEC] * n, out_shape=out_shapes,
        scratch_shapes=[pltpu.SemaphoreType.DMA((copies_per_array * n,)),
                        pltpu.SemaphoreType.DMA((copies_per_array * n,))],
    )(*arrays)


def _plan_to_sibling(ins, outs, place):
    x, y, c, _ = place
    return [(ins[w], outs[w], (x, y, 1 - c)) for w in range(len(ins))]


def _plan_sibling_halves(ins, outs, place):
    x, y, c, _ = place
    return [(_half(ins[w], BIG[w], 1 - c, slice(None)), outs[w], (x, y, 1 - c)) for w in range(len(ins))]


def _plan_chip_partials(ins, outs, place):
    x, y, c, others = place
    return [(ins[w].at[2 * ox + oy], outs[w].at[j], (ox, oy, c))
            for w in range(len(ins)) for j, (ox, oy) in enumerate(others)]


def _chip_partial(g3, got, name, core):
    hs = tuple(got.shape[1:])
    if name in LANE_HALVED:
        g_in, g_spec = g3, pl.BlockSpec((1,) + hs, lambda s, c_ref: (s, 0, c_ref[0]))
    else:
        g_in, g_spec = g3.reshape((N_CHIPS, 2) + hs), pl.BlockSpec((1, 1) + hs, lambda s, c_ref: (s, c_ref[0], 0, 0))
    blk = pl.BlockSpec((1,) + hs, lambda s, c_ref: (s, 0, 0))

    def body(c_ref, g_ref, r_ref, o_ref):
        o_ref[0] = (g_ref[(0,) * (len(g_ref.shape) - 2)] + r_ref[0].astype(F32)).astype(o_ref.dtype)

    return pl.pallas_call(
        body, name="chip_partial_%s" % name,
        grid_spec=pltpu.PrefetchScalarGridSpec(num_scalar_prefetch=1, grid=(N_CHIPS,), in_specs=[g_spec, blk],
                                               out_specs=blk),
        out_shape=jax.ShapeDtypeStruct((N_CHIPS,) + hs, BF16), compiler_params=_cparams("parallel"),
    )(core, g_in, got)


def _shard_sum(part, got3, name, me):
    hs = tuple(part.shape[1:])

    def body(me_ref, p_ref, a_ref, b_ref, c_ref, o_ref):
        o_ref[...] = ((p_ref[0].astype(F32) + a_ref[0].astype(F32)) + b_ref[0].astype(F32)) + c_ref[0].astype(F32)

    slot = lambda j: pl.BlockSpec((1,) + hs, lambda i, me_ref: (j, 0, 0))
    return pl.pallas_call(
        body, name="shard_sum_%s" % name,
        grid_spec=pltpu.PrefetchScalarGridSpec(
            num_scalar_prefetch=1, grid=(1,),
            in_specs=[pl.BlockSpec((1,) + hs, lambda i, me_ref: (me_ref[0], 0, 0)), slot(0), slot(1), slot(2)],
            out_specs=pl.BlockSpec(hs, lambda i, me_ref: (0, 0))),
        out_shape=jax.ShapeDtypeStruct(hs, F32), compiler_params=_cparams("arbitrary"),
    )(me, part, got3, got3, got3)


def _adamw_halves(w, m, v, mine, other, name, core):
    hs = tuple(mine.shape)
    full = pl.BlockSpec(hs, (lambda h, c_ref: (0, h)) if name in LANE_HALVED else (lambda h, c_ref: (h, 0)))
    half = pl.BlockSpec(hs, lambda h, c_ref: (0, 0))

    def body(c_ref, w_ref, m_ref, v_ref, a_ref, b_ref, g_ref, d_ref, mo_ref, vo_ref):
        g = jnp.where(pl.program_id(0) == c_ref[0], a_ref[...], b_ref[...])
        delta, m_new, v_new = _adamw_math(w_ref[...], g, m_ref[...], v_ref[...])
        g_ref[...] = g
        d_ref[...] = delta
        mo_ref[...] = m_new
        vo_ref[...] = v_new

    out = jax.ShapeDtypeStruct(w.shape, F32)
    return pl.pallas_call(
        body, name="adamw_%s" % name,
        grid_spec=pltpu.PrefetchScalarGridSpec(num_scalar_prefetch=1, grid=(2,), in_specs=[full, full, full, half, half],
                                               out_specs=[full] * 4),
        out_shape=[out] * 4, compiler_params=_cparams("parallel"),
    )(core, w, m, v, mine, other)


def _adamw_math(w, g, m, v):
    m = ADAM_B1 * m + (1.0 - ADAM_B1) * g
    v = ADAM_B2 * v + (1.0 - ADAM_B2) * (g * g)
    m_hat = m / (1.0 - ADAM_B1 ** ADAM_STEP)
    v_hat = v / (1.0 - ADAM_B2 ** ADAM_STEP)
    delta = -ADAM_LR * (m_hat / (jnp.sqrt(v_hat) + ADAM_EPS) + ADAM_WD * w)
    return delta, m, v


def _reduce_big_grads(grads):
    core = lax.axis_index("c").astype(jnp.int32).reshape(1)
    me = (2 * lax.axis_index("x") + lax.axis_index("y")).astype(jnp.int32).reshape(1)
    g3 = [grads[n].reshape(N_CHIPS, grads[n].shape[0] // N_CHIPS, grads[n].shape[1]) for n in BIG]
    hs = [_half_shape(n, g.shape[1:]) for n, g in zip(BIG, g3)]
    bf = lambda lead: [jax.ShapeDtypeStruct(lead + h, BF16) for h in hs]
    got = _exchange([g.astype(BF16) for g in g3], bf((N_CHIPS,)), _plan_sibling_halves, 1, "sibling_halves")
    part = [_chip_partial(g, r, n, core) for n, g, r in zip(BIG, g3, got)]
    got3 = _exchange(part, bf((3,)), _plan_chip_partials, 3, "chip_partials")
    mine = [_shard_sum(p, r3, n, me) for n, p, r3 in zip(BIG, part, got3)]
    other = _exchange(mine, [jax.ShapeDtypeStruct(h, F32) for h in hs], _plan_to_sibling, 1, "sibling_result")
    return {n: (a, b) for n, a, b in zip(BIG, mine, other)}, core


SMALL_ROWS = 8
SMALL_LAYOUT = {"attn_norm_w": (0, 0, 1024), "ffn_norm_w": (1, 0, 1024), "final_norm_w": (2, 0, 1024),
                "q_a_norm_w": (3, 0, 384), "kv_a_norm_w": (3, 384, 256), "mlstm_norm_w": (4, 0, 512),
                "b_gates": (4, 512, 8)}
LOSS_SLOT = (5, 0)


def _pack_small(vals, loss=None):
    tile = jnp.zeros((SMALL_ROWS, D_MODEL), F32)
    for n, (r, c0, width) in SMALL_LAYOUT.items():
        tile = tile.at[r, c0:c0 + width].set(vals[n].reshape(width).astype(F32))
    if loss is not None:
        tile = tile.at[LOSS_SLOT[0], LOSS_SLOT[1]].set(loss)
    return tile


def _unpack_small(tile, shapes):
    return {n: tile[r, c0:c0 + width].reshape(shapes[n]) for n, (r, c0, width) in SMALL_LAYOUT.items()}


def _small_allreduce_adamw(g_tile, w_tile, m_tile, v_tile):
    def body(g_ref, w_ref, m_ref, v_ref, gsum_ref, d_ref, mo_ref, vo_ref, slots, send_sems, recv_sems):
        x, y, c, _ = _place()
        me = 4 * x + 2 * y + c
        slots[me] = g_ref[...]
        copies = []
        for k in range(1, 8):
            dx, dy, dc = (k >> 2) & 1, (k >> 1) & 1, k & 1
            to = (x ^ dx, y ^ dy, c ^ dc)
            cp = pltpu.make_async_remote_copy(src_ref=g_ref, dst_ref=slots.at[me], send_sem=send_sems.at[k - 1],
                                              recv_sem=recv_sems.at[k - 1], device_id=to, device_id_type=MESH)
            cp.start()
            copies.append(cp)
        for k in range(1, 8):
            src = me ^ k
            pltpu.make_async_remote_copy(src_ref=g_ref, dst_ref=slots.at[src], send_sem=send_sems.at[k - 1],
                                         recv_sem=recv_sems.at[k - 1], device_id=(x, y, c),
                                         device_id_type=MESH).wait_recv()
        for cp in copies:
            cp.wait_send()
        total = slots[0]
        for d in range(1, 8):
            total = total + slots[d]
        gsum_ref[...] = total
        delta, m_new, v_new = _adamw_math(w_ref[...], total, m_ref[...], v_ref[...])
        d_ref[...] = delta
        mo_ref[...] = m_new
        vo_ref[...] = v_new

    vm = pl.BlockSpec(memory_space=pltpu.VMEM)
    tile = jax.ShapeDtypeStruct((SMALL_ROWS, D_MODEL), F32)
    return pl.pallas_call(
        body, name="small_allreduce_adamw", in_specs=[vm] * 4, out_specs=[vm] * 4, out_shape=[tile] * 4,
        scratch_shapes=[pltpu.VMEM((8, SMALL_ROWS, D_MODEL), F32), pltpu.SemaphoreType.DMA((7,)),
                        pltpu.SemaphoreType.DMA((7,))],
    )(g_tile, w_tile, m_tile, v_tile)


def kernel(x, positions, attn_norm_w, w_in, b_gates, mlstm_norm_w, q_a_norm_w, w_q_b, kv_a_norm_w, w_kv_b, w_out, ffn_norm_w, w_gate, w_up, w_down, final_norm_w, loss_target, m_attn_norm_w, m_w_in, m_b_gates, m_mlstm_norm_w, m_q_a_norm_w, m_w_q_b, m_kv_a_norm_w, m_w_kv_b, m_w_out, m_ffn_norm_w, m_w_gate, m_w_up, m_w_down, m_final_norm_w, v_attn_norm_w, v_w_in, v_b_gates, v_mlstm_norm_w, v_q_a_norm_w, v_w_q_b, v_kv_a_norm_w, v_w_kv_b, v_w_out, v_ffn_norm_w, v_w_gate, v_w_up, v_w_down, v_final_norm_w):
    names = ("attn_norm_w", "w_in", "b_gates", "mlstm_norm_w", "q_a_norm_w", "w_q_b", "kv_a_norm_w", "w_kv_b", "w_out",
             "ffn_norm_w", "w_gate", "w_up", "w_down", "final_norm_w")
    wts = dict(zip(names, (attn_norm_w, w_in, b_gates, mlstm_norm_w, q_a_norm_w, w_q_b, kv_a_norm_w, w_kv_b, w_out,
                           ffn_norm_w, w_gate, w_up, w_down, final_norm_w)))
    mom = dict(zip(names, (m_attn_norm_w, m_w_in, m_b_gates, m_mlstm_norm_w, m_q_a_norm_w, m_w_q_b, m_kv_a_norm_w,
                           m_w_kv_b, m_w_out, m_ffn_norm_w, m_w_gate, m_w_up, m_w_down, m_final_norm_w)))
    vel = dict(zip(names, (v_attn_norm_w, v_w_in, v_b_gates, v_mlstm_norm_w, v_q_a_norm_w, v_w_q_b, v_kv_a_norm_w,
                           v_w_kv_b, v_w_out, v_ffn_norm_w, v_w_gate, v_w_up, v_w_down, v_final_norm_w)))
    t = x.shape[1]

    shards = {n: _stored(n, wts[n]).astype(BF16) for n in BIG}
    early = _gather_weights(EARLY, [shards[n] for n in EARLY])
    early, late_in = lax.optimization_barrier((early, [shards[n] for n in LATE]))
    late = _gather_weights_async(LATE, late_in)
    full = {n: g.reshape(N_CHIPS * g.shape[1], g.shape[2]) for n, g in zip(EARLY + LATE, early + late)}

    nw = {n: wts[n] for n in SMALL}
    loss, grad_x, grads = _local_step(x[0], positions.reshape(t, 1), loss_target[0], nw, full)

    halves, core = _reduce_big_grads(grads)
    outs_g, outs_d, outs_m, outs_v = {}, {}, {}, {}
    for n in BIG:
        res = _adamw_halves(_stored(n, wts[n]), _stored(n, mom[n]), _stored(n, vel[n]), *halves[n], n, core)
        outs_g[n], outs_d[n], outs_m[n], outs_v[n] = [_unstored(n, r) for r in res]
    shapes = {n: wts[n].shape for n in SMALL}
    g_tile = _pack_small({n: grads[n] for n in SMALL}, loss=loss[0, 0])
    gsum, d_tile, m_tile, v_tile = _small_allreduce_adamw(g_tile, _pack_small(wts), _pack_small(mom), _pack_small(vel))
    outs_g.update(_unpack_small(gsum, shapes))
    outs_d.update(_unpack_small(d_tile, shapes))
    outs_m.update(_unpack_small(m_tile, shapes))
    outs_v.update(_unpack_small(v_tile, shapes))
    total_loss = gsum[LOSS_SLOT[0], LOSS_SLOT[1]]
    return (total_loss, grad_x[None], *[outs_g[n] for n in names], *[outs_d[n] for n in names],
            *[outs_m[n] for n in names], *[outs_v[n] for n in names])
```

```python
import functools
import math

import jax
import jax.numpy as jnp
from jax import lax
from jax.experimental import pallas as pl
from jax.experimental.pallas import tpu as pltpu
from jax.experimental.pallas import tpu_sc as plsc

F32 = jnp.float32
BF16 = jnp.bfloat16

D_MODEL = 1024
M_HEADS = 4
M_HEAD_DIM = 128
M_WIDTH = M_HEADS * M_HEAD_DIM
M_CHUNK = 64
A_HEADS = 8
A_NOPE = 64
A_ROPE = 32
A_V = 64
A_WIDTH = A_HEADS * A_V
A_QK_PAD = 128
Q_RANK = 384
KV_RANK = 256
ROPE_THETA = 10000.0
D_FF = 2816
D_IN = 2728
EPS = 1e-6
ATTN_SCALE = (A_NOPE + A_ROPE) ** -0.5
M_SCALE = M_HEAD_DIM ** -0.5

ADAM_LR = 0.001
ADAM_B1 = 0.9
ADAM_B2 = 0.999
ADAM_EPS = 1e-08
ADAM_WD = 0.01
ADAM_STEP = 10

VMEM_LIMIT_BYTES = 56 * 1024 * 1024
LANES = 128
NEG_INF = float("-inf")


def _cparams(*sem):
    return pltpu.CompilerParams(dimension_semantics=sem if sem else None, vmem_limit_bytes=VMEM_LIMIT_BYTES)


_DIMS = {"nn": (((1,), (0,)), ((), ())), "nt": (((1,), (1,)), ((), ())), "tn": (((0,), (0,)), ((), ()))}


def _matmul(a, b, mode, out_dtype, name, tm=None, tn=1024, residual=None, b_rows=None, lead_scale=None):
    b_shape = b.shape if b_rows is None else (b_rows, b.shape[1])
    if mode == "nn":
        (m, k), (k2, n) = a.shape, b_shape
    elif mode == "nt":
        (m, k), (n, k2) = a.shape, b_shape
    else:
        (k, m), (k2, n) = a.shape, b_shape
    assert k == k2, (a.shape, b.shape, mode)
    if tm is None:
        tm = 512 if mode == "tn" else 1024
    tm, tn = min(tm, m), min(tn, n)
    assert m % tm == 0 and n % tn == 0, (m, n, tm, tn)
    a_spec = pl.BlockSpec((k, tm), lambda i, j: (0, i)) if mode == "tn" else pl.BlockSpec((tm, k), lambda i, j: (i, 0))
    b_spec = pl.BlockSpec((tn, k), lambda i, j: (j, 0)) if mode == "nt" else pl.BlockSpec((k, tn), lambda i, j: (0, j))
    o_spec = pl.BlockSpec((tm, tn), lambda i, j: (i, j))
    dims = _DIMS[mode]

    def body(a_ref, b_ref, *rest):
        o_ref = rest[-1]
        acc = lax.dot_general(a_ref[...].astype(BF16), b_ref[...].astype(BF16), dims, preferred_element_type=F32)
        if lead_scale is not None:
            assert lead_scale[0] % tn == 0
            acc = acc * jnp.where(pl.program_id(1) < lead_scale[0] // tn, lead_scale[1], 1.0)
        if residual is not None:
            acc = acc + rest[0][...].astype(F32)
        o_ref[...] = acc.astype(o_ref.dtype)

    ins = [a, b] + ([residual] if residual is not None else [])
    in_specs = [a_spec, b_spec] + ([o_spec] if residual is not None else [])
    return pl.pallas_call(
        body, name=name, grid=(m // tm, n // tn), in_specs=in_specs, out_specs=o_spec,
        out_shape=jax.ShapeDtypeStruct((m, n), out_dtype), compiler_params=_cparams("parallel", "parallel"),
    )(*ins)


def _rmsnorm_fwd(x, w, name, tm=512):
    t, d = x.shape

    def body(x_ref, w_ref, o_ref):
        xf = x_ref[...].astype(F32)
        r = lax.rsqrt(jnp.mean(xf * xf, axis=-1, keepdims=True) + EPS)
        o_ref[...] = (xf * r * w_ref[...]).astype(o_ref.dtype)

    return pl.pallas_call(
        body, name=name, grid=(t // tm,),
        in_specs=[pl.BlockSpec((tm, d), lambda i: (i, 0)), pl.BlockSpec((1, d), lambda i: (0, 0))],
        out_specs=pl.BlockSpec((tm, d), lambda i: (i, 0)),
        out_shape=jax.ShapeDtypeStruct((t, d), BF16), compiler_params=_cparams("parallel"),
    )(x, w)


def _rmsnorm_bwd(dy, x, w, name, out_dtype, residual=None, also_bf16=False, tm=512):
    t, d = x.shape

    def body(dy_ref, x_ref, w_ref, *rest):
        dx_ref, dw_ref = rest[-3 if also_bf16 else -2], rest[-1]
        xf = x_ref[...].astype(F32)
        r = lax.rsqrt(jnp.mean(xf * xf, axis=-1, keepdims=True) + EPS)
        xh = xf * r
        dyf = dy_ref[...].astype(F32)
        dyh = dyf * w_ref[...]
        dx = r * (dyh - xh * jnp.mean(dyh * xh, axis=-1, keepdims=True))
        if residual is not None:
            dx = dx + rest[0][...].astype(F32)
        dx_ref[...] = dx.astype(dx_ref.dtype)
        if also_bf16:
            rest[-2][...] = dx.astype(BF16)
        part = jnp.sum(dyf * xh, axis=0, keepdims=True)

        @pl.when(pl.program_id(0) == 0)
        def _():
            dw_ref[...] = part

        @pl.when(pl.program_id(0) > 0)
        def _():
            dw_ref[...] += part

    row = pl.BlockSpec((tm, d), lambda i: (i, 0))
    vec = pl.BlockSpec((1, d), lambda i: (0, 0))
    ins = [dy, x, w] + ([residual] if residual is not None else [])
    extra = [jax.ShapeDtypeStruct((t, d), BF16)] if also_bf16 else []
    return pl.pallas_call(
        body, name=name, grid=(t // tm,), in_specs=[row, row, vec] + ([row] if residual is not None else []),
        out_specs=[row] + [row] * len(extra) + [vec],
        out_shape=[jax.ShapeDtypeStruct((t, d), out_dtype)] + extra + [jax.ShapeDtypeStruct((1, d), F32)],
        compiler_params=_cparams("arbitrary"),
    )(*ins)


def _final_loss(h, target, w, name, tm=512):
    t, d = h.shape

    def body(h_ref, t_ref, w_ref, dh_ref, dhb_ref, loss_ref, dw_ref):
        xf = h_ref[...]
        r = lax.rsqrt(jnp.mean(xf * xf, axis=-1, keepdims=True) + EPS)
        xh = xf * r
        err = xh * w_ref[...] - t_ref[...]
        part_loss = 0.5 * jnp.sum(jnp.sum(err * err, axis=-1, keepdims=True), axis=0, keepdims=True) * (1.0 / d)
        dy = err * (1.0 / d)
        dyh = dy * w_ref[...]
        dh = r * (dyh - xh * jnp.mean(dyh * xh, axis=-1, keepdims=True))
        dh_ref[...] = dh
        dhb_ref[...] = dh.astype(BF16)
        part_dw = jnp.sum(dy * xh, axis=0, keepdims=True)
        part_loss = jnp.broadcast_to(part_loss, (1, LANES))

        @pl.when(pl.program_id(0) == 0)
        def _():
            dw_ref[...] = part_dw
            loss_ref[...] = part_loss

        @pl.when(pl.program_id(0) > 0)
        def _():
            dw_ref[...] += part_dw
            loss_ref[...] += part_loss

    row = pl.BlockSpec((tm, d), lambda i: (i, 0))
    vec = pl.BlockSpec((1, d), lambda i: (0, 0))
    return pl.pallas_call(
        body, name=name, grid=(t // tm,), in_specs=[row, row, vec],
        out_specs=[row, row, pl.BlockSpec((1, LANES), lambda i: (0, 0)), vec],
        out_shape=[jax.ShapeDtypeStruct((t, d), F32), jax.ShapeDtypeStruct((t, d), BF16),
                   jax.ShapeDtypeStruct((1, LANES), F32), jax.ShapeDtypeStruct((1, d), F32)],
        compiler_params=_cparams("arbitrary"),
    )(h, target, w)


def _swiglu_fwd(g, u, name, tm=512, tn=1408):
    t, n = g.shape

    def body(g_ref, u_ref, o_ref):
        gf = g_ref[...].astype(F32)
        o_ref[...] = (gf * jax.nn.sigmoid(gf) * u_ref[...].astype(F32)).astype(o_ref.dtype)

    blk = pl.BlockSpec((tm, tn), lambda i, j: (i, j))
    return pl.pallas_call(
        body, name=name, grid=(t // tm, n // tn), in_specs=[blk, blk], out_specs=blk,
        out_shape=jax.ShapeDtypeStruct((t, n), BF16), compiler_params=_cparams("parallel", "parallel"),
    )(g, u)


def _swiglu_bwd(g, u, dact, name, tm=512, tn=1408):
    t, n = g.shape

    def body(g_ref, u_ref, d_ref, dg_ref, du_ref):
        gf = g_ref[...].astype(F32)
        uf = u_ref[...].astype(F32)
        df = d_ref[...].astype(F32)
        s = jax.nn.sigmoid(gf)
        dg_ref[...] = (df * uf * s * (1.0 + gf * (1.0 - s))).astype(dg_ref.dtype)
        du_ref[...] = (df * gf * s).astype(du_ref.dtype)

    blk = pl.BlockSpec((tm, tn), lambda i, j: (i, j))
    out = jax.ShapeDtypeStruct((t, n), BF16)
    return pl.pallas_call(
        body, name=name, grid=(t // tm, n // tn), in_specs=[blk, blk, blk], out_specs=[blk, blk],
        out_shape=[out, out], compiler_params=_cparams("parallel", "parallel"),
    )(g, u, dact)


M_BLOCK = 512
M_CHUNKS_PER_BLOCK = M_BLOCK // M_CHUNK


def _mlstm_chunk_fwd(q, k, v, i_col, f_col, c_prev, n_prev, m_prev):
    L = M_CHUNK
    tt = lax.broadcasted_iota(jnp.int32, (L, L), 0)
    ss = lax.broadcasted_iota(jnp.int32, (L, L), 1)
    eye = tt == ss
    causal = ss <= tt

    def to_row(col):
        return jnp.sum(jnp.where(eye, col, 0.0), axis=0, keepdims=True)

    lf_col = jnp.minimum(f_col, 0.0) - jnp.log(1.0 + jnp.exp(-jnp.abs(f_col)))
    lf_row = to_row(lf_col)
    li_row = to_row(i_col)
    b_col = jnp.sum(jnp.where(causal, lf_row, 0.0), axis=1, keepdims=True)
    b_row = jnp.sum(jnp.where(tt <= ss, lf_col, 0.0), axis=0, keepdims=True)
    g = jnp.sum(lf_col, axis=0, keepdims=True)
    a_row = g - b_row + li_row
    a_col = g - b_col + i_col
    m_loc = jnp.max(a_row, axis=1, keepdims=True)
    d_log = jnp.where(causal, b_col - b_row + li_row, NEG_INF)
    inter = b_col + m_prev
    m_t = jnp.maximum(jnp.max(d_log, axis=1, keepdims=True), inter)
    dmat = jnp.exp(d_log - m_t)
    amat = lax.dot_general(q, k, _DIMS["nt"], preferred_element_type=F32)
    p = dmat * amat
    sig = jnp.exp(inter - m_t)
    c_prev_b = c_prev.astype(BF16)
    qc = jnp.dot(q, c_prev_b, preferred_element_type=F32)
    num = jnp.dot(p.astype(BF16), v, preferred_element_type=F32) + sig * qc
    qn = jnp.sum(q.astype(F32) * n_prev, axis=1, keepdims=True)
    den_raw = jnp.sum(p, axis=1, keepdims=True) + sig * qn
    floor = jnp.exp(-m_t)
    den = jnp.maximum(jnp.abs(den_raw), floor)
    h = num / den
    m_new = jnp.maximum(g + m_prev, m_loc)
    w_col = jnp.exp(a_col - m_new)
    alpha = jnp.exp(g + m_prev - m_new)
    return dict(eye=eye, causal=causal, tt=tt, ss=ss, f_col=f_col, dmat=dmat, amat=amat, p=p, sig=sig, qc=qc,
                qn=qn, num=num, den_raw=den_raw, floor=floor, den=den, h=h, m_new=m_new, w_col=w_col, alpha=alpha,
                c_prev_b=c_prev_b)


def _head_gates(gc, hd):
    return gc[:, hd:hd + 1], gc[:, M_HEADS + hd:M_HEADS + hd + 1]


def _mlstm_fwd(m4, gates, bias, norm_w, name):
    t = m4.shape[0]
    nblk = t // M_BLOCK
    nc = t // M_CHUNK
    L, dh = M_CHUNK, M_HEAD_DIM

    def body(m4_ref, g_ref, b_ref, w_ref, hm_ref, cp_ref, nm_ref, c_s, n_s, m_s):
        @pl.when(pl.program_id(0) == 0)
        def _():
            c_s[...] = jnp.zeros_like(c_s)
            n_s[...] = jnp.zeros_like(n_s)
            m_s[...] = jnp.zeros_like(m_s)

        row8 = lax.broadcasted_iota(jnp.int32, (8, dh), 0)

        def chunk(c, carry):
            rows = pl.ds(pl.multiple_of(c * L, L), L)
            gc = g_ref[rows, :] + b_ref[...]
            for hd in range(M_HEADS):
                q = m4_ref[rows, hd * dh:(hd + 1) * dh]
                k = m4_ref[rows, M_WIDTH + hd * dh:M_WIDTH + (hd + 1) * dh]
                v = m4_ref[rows, 2 * M_WIDTH + hd * dh:2 * M_WIDTH + (hd + 1) * dh]
                o = m4_ref[rows, 3 * M_WIDTH + hd * dh:3 * M_WIDTH + (hd + 1) * dh].astype(F32)
                i_col, f_col = _head_gates(gc, hd)
                c_prev = c_s[hd]
                n_prev = n_s[hd, 0:1, :]
                m_prev = m_s[hd, 0:1, 0:1]
                r = _mlstm_chunk_fwd(q, k, v, i_col, f_col, c_prev, n_prev, m_prev)
                h = r["h"]
                hn = h * lax.rsqrt(jnp.mean(h * h, axis=-1, keepdims=True) + EPS) * w_ref[hd:hd + 1, :]
                hm_ref[rows, hd * dh:(hd + 1) * dh] = (hn * jax.nn.sigmoid(o)).astype(hm_ref.dtype)
                cp_ref[hd, c] = r["c_prev_b"]
                nm_ref[hd, c] = jnp.where(row8 == 0, n_prev, jnp.where(row8 == 1, m_prev, 0.0))
                kw = k.astype(F32) * r["w_col"]
                c_s[hd] = r["alpha"] * c_prev + lax.dot_general(kw.astype(BF16), v, _DIMS["tn"],
                                                                preferred_element_type=F32)
                n_s[hd, 0:1, :] = r["alpha"] * n_prev + jnp.sum(kw, axis=0, keepdims=True)
                m_s[hd] = jnp.broadcast_to(r["m_new"], (8, dh))
            return carry

        lax.fori_loop(0, M_CHUNKS_PER_BLOCK, chunk, 0)

    return pl.pallas_call(
        body, name=name, grid=(nblk,),
        in_specs=[pl.BlockSpec((M_BLOCK, 4 * M_WIDTH), lambda i: (i, 0)),
                  pl.BlockSpec((M_BLOCK, LANES), lambda i: (i, 0)),
                  pl.BlockSpec((1, LANES), lambda i: (0, 0)),
                  pl.BlockSpec((M_HEADS, dh), lambda i: (0, 0))],
        out_specs=[pl.BlockSpec((M_BLOCK, M_WIDTH), lambda i: (i, 0)),
                   pl.BlockSpec((M_HEADS, M_CHUNKS_PER_BLOCK, dh, dh), lambda i: (0, i, 0, 0)),
                   pl.BlockSpec((M_HEADS, M_CHUNKS_PER_BLOCK, 8, dh), lambda i: (0, i, 0, 0))],
        out_shape=[jax.ShapeDtypeStruct((t, M_WIDTH), BF16),
                   jax.ShapeDtypeStruct((M_HEADS, nc, dh, dh), BF16),
                   jax.ShapeDtypeStruct((M_HEADS, nc, 8, dh), F32)],
        scratch_shapes=[pltpu.VMEM((M_HEADS, dh, dh), F32), pltpu.VMEM((M_HEADS, 8, dh), F32),
                        pltpu.VMEM((M_HEADS, 8, dh), F32)],
        compiler_params=_cparams("arbitrary"),
    )(m4, gates, bias, norm_w)


def _mlstm_bwd(m4, gates, bias, norm_w, c_prev_all, nm_all, dhm, name):
    t = m4.shape[0]
    nblk = t // M_BLOCK
    L, dh = M_CHUNK, M_HEAD_DIM

    def body(m4_ref, g_ref, b_ref, w_ref, cp_ref, nm_ref, dhm_ref, dm4_ref, dg_ref, small_ref, dc_s, dn_s):
        @pl.when(pl.program_id(0) == 0)
        def _():
            dc_s[...] = jnp.zeros_like(dc_s)
            dn_s[...] = jnp.zeros_like(dn_s)
            small_ref[...] = jnp.zeros_like(small_ref)

        lane = lax.broadcasted_iota(jnp.int32, (L, LANES), 1)
        row8 = lax.broadcasted_iota(jnp.int32, (8, LANES), 0)

        def chunk(ci, carry):
            c = M_CHUNKS_PER_BLOCK - 1 - ci
            rows = pl.ds(pl.multiple_of(c * L, L), L)
            gc = g_ref[rows, :] + b_ref[...]
            dg_tile = jnp.zeros((L, LANES), F32)
            small = jnp.zeros((8, LANES), F32)
            for hd in range(M_HEADS):
                q = m4_ref[rows, hd * dh:(hd + 1) * dh]
                k = m4_ref[rows, M_WIDTH + hd * dh:M_WIDTH + (hd + 1) * dh]
                v = m4_ref[rows, 2 * M_WIDTH + hd * dh:2 * M_WIDTH + (hd + 1) * dh]
                o = m4_ref[rows, 3 * M_WIDTH + hd * dh:3 * M_WIDTH + (hd + 1) * dh].astype(F32)
                i_col, f_col = _head_gates(gc, hd)
                c_prev = cp_ref[hd, c].astype(F32)
                nm = nm_ref[hd, c]
                n_prev = nm[0:1, :]
                m_prev = nm[1:2, 0:1]
                r = _mlstm_chunk_fwd(q, k, v, i_col, f_col, c_prev, n_prev, m_prev)
                eye, tt, ss = r["eye"], r["tt"], r["ss"]
                h, den, sig, p = r["h"], r["den"], r["sig"], r["p"]
                w_col, alpha, c_prev_b = r["w_col"], r["alpha"], r["c_prev_b"]

                def to_row(col):
                    return jnp.sum(jnp.where(eye, col, 0.0), axis=0, keepdims=True)

                def to_col(row):
                    return jnp.sum(jnp.where(eye, row, 0.0), axis=1, keepdims=True)

                w_h = w_ref[hd:hd + 1, :]
                rn = lax.rsqrt(jnp.mean(h * h, axis=-1, keepdims=True) + EPS)
                hh = h * rn
                og = jax.nn.sigmoid(o)
                dhm_c = dhm_ref[rows, hd * dh:(hd + 1) * dh].astype(F32)
                dhn = dhm_c * og
                d_o = dhm_c * hh * w_h * og * (1.0 - og)
                small = small + jnp.where(row8 == hd, jnp.sum(dhn * hh, axis=0, keepdims=True), 0.0)
                dhh = dhn * w_h
                dh_ = rn * (dhh - hh * jnp.mean(dhh * hh, axis=-1, keepdims=True))

                dnum = dh_ / den
                dden = -jnp.sum(dh_ * h, axis=-1, keepdims=True) / den
                dden_raw = jnp.where(jnp.abs(r["den_raw"]) >= r["floor"], dden * jnp.sign(r["den_raw"]), 0.0)
                dnum_b = dnum.astype(BF16)
                dp = lax.dot_general(dnum_b, v, _DIMS["nt"], preferred_element_type=F32) + dden_raw
                p_b = p.astype(BF16)
                dc = dc_s[hd]
                dn = dn_s[hd, 0:1, :]
                dc_b = dc.astype(BF16)
                g2 = sig * dnum
                g2_b = g2.astype(BF16)
                sd = sig * dden_raw
                da_mat = (dp * r["dmat"]).astype(BF16)
                dqs = (lax.dot_general(g2_b, c_prev_b, _DIMS["nt"], preferred_element_type=F32)
                       + sd * n_prev + jnp.dot(da_mat, k, preferred_element_type=F32))
                r_mat = lax.dot_general(v, dc_b, _DIMS["nt"], preferred_element_type=F32) + dn
                kf = k.astype(F32)
                d_k = lax.dot_general(da_mat, q, _DIMS["tn"], preferred_element_type=F32) + w_col * r_mat
                d_v = (lax.dot_general(p_b, dnum_b, _DIMS["tn"], preferred_element_type=F32)
                       + w_col * jnp.dot(k, dc_b, preferred_element_type=F32))
                d_omega = jnp.sum(kf * r_mat, axis=-1, keepdims=True)
                da_col = d_omega * w_col
                dsig = jnp.sum(dnum * r["qc"], axis=-1, keepdims=True) + dden_raw * r["qn"]
                pp = dp * p
                r1 = jnp.sum(pp, axis=1, keepdims=True)
                c1_col = to_col(jnp.sum(pp, axis=0, keepdims=True))
                d_alpha = (jnp.sum(jnp.sum(dc * c_prev, axis=1, keepdims=True), axis=0, keepdims=True)
                           + jnp.sum(dn * n_prev, axis=1, keepdims=True))
                dgl = jnp.sum(da_col, axis=0, keepdims=True) + d_alpha * alpha
                db_col = r1 + dsig * sig - c1_col - da_col
                dli_col = c1_col + da_col
                db_row = to_row(db_col)
                dlf_col = jnp.sum(jnp.where(ss >= tt, db_row, 0.0), axis=1, keepdims=True) + dgl
                df_col = dlf_col * jax.nn.sigmoid(-r["f_col"])

                dc_s[hd] = alpha * dc + lax.dot_general(q, g2_b, _DIMS["tn"], preferred_element_type=F32)
                dn_s[hd, 0:1, :] = alpha * dn + jnp.sum(sd * q.astype(F32), axis=0, keepdims=True)

                dm4_ref[rows, hd * dh:(hd + 1) * dh] = (dqs * M_SCALE).astype(dm4_ref.dtype)
                dm4_ref[rows, M_WIDTH + hd * dh:M_WIDTH + (hd + 1) * dh] = d_k.astype(dm4_ref.dtype)
                dm4_ref[rows, 2 * M_WIDTH + hd * dh:2 * M_WIDTH + (hd + 1) * dh] = d_v.astype(dm4_ref.dtype)
                dm4_ref[rows, 3 * M_WIDTH + hd * dh:3 * M_WIDTH + (hd + 1) * dh] = d_o.astype(dm4_ref.dtype)
                dg_tile = dg_tile + jnp.where(lane == hd, dli_col, 0.0) + jnp.where(lane == M_HEADS + hd, df_col, 0.0)
            dg_ref[rows, :] = dg_tile
            small = small + jnp.where(row8 == M_HEADS, jnp.sum(dg_tile, axis=0, keepdims=True), 0.0)
            small_ref[...] += small
            return carry

        lax.fori_loop(0, M_CHUNKS_PER_BLOCK, chunk, 0)

    rev = lambda i: nblk - 1 - i
    return pl.pallas_call(
        body, name=name, grid=(nblk,),
        in_specs=[pl.BlockSpec((M_BLOCK, 4 * M_WIDTH), lambda i: (rev(i), 0)),
                  pl.BlockSpec((M_BLOCK, LANES), lambda i: (rev(i), 0)),
                  pl.BlockSpec((1, LANES), lambda i: (0, 0)),
                  pl.BlockSpec((M_HEADS, dh), lambda i: (0, 0)),
                  pl.BlockSpec((M_HEADS, M_CHUNKS_PER_BLOCK, dh, dh), lambda i: (0, rev(i), 0, 0)),
                  pl.BlockSpec((M_HEADS, M_CHUNKS_PER_BLOCK, 8, dh), lambda i: (0, rev(i), 0, 0)),
                  pl.BlockSpec((M_BLOCK, M_WIDTH), lambda i: (rev(i), 0))],
        out_specs=[pl.BlockSpec((M_BLOCK, 4 * M_WIDTH), lambda i: (rev(i), 0)),
                   pl.BlockSpec((M_BLOCK, LANES), lambda i: (rev(i), 0)),
                   pl.BlockSpec((8, LANES), lambda i: (0, 0))],
        out_shape=[jax.ShapeDtypeStruct((t, 4 * M_WIDTH), BF16), jax.ShapeDtypeStruct((t, LANES), F32),
                   jax.ShapeDtypeStruct((8, LANES), F32)],
        scratch_shapes=[pltpu.VMEM((M_HEADS, dh, dh), F32), pltpu.VMEM((M_HEADS, 8, dh), F32)],
        compiler_params=_cparams("arbitrary"),
    )(m4, gates, bias, norm_w, c_prev_all, nm_all, dhm)


def _rope_tables(pos_col, width):
    lane = lax.broadcasted_iota(jnp.int32, (1, width), 1) % A_QK_PAD
    half = A_ROPE // 2
    first = (lane >= A_NOPE) & (lane < A_NOPE + half)
    second = (lane >= A_NOPE + half) & (lane < A_NOPE + A_ROPE)
    idx = jnp.where(first, lane - A_NOPE, lane - A_NOPE - half).astype(F32)
    inv_freq = jnp.exp(idx * (-math.log(ROPE_THETA) / half))
    ang = pos_col.astype(F32) * inv_freq
    cos, sin = jnp.cos(ang), jnp.sin(ang)
    rot = first | second
    return jnp.where(rot, cos, 1.0), jnp.where(first, -sin, 0.0), jnp.where(second, sin, 0.0)


def _rope_apply(vv, cosf, s1, s2):
    half = A_ROPE // 2
    w = vv.shape[-1]
    return vv * cosf + pltpu.roll(vv, w - half, 1) * s1 + pltpu.roll(vv, half, 1) * s2


def _rope_apply_t(dd, cosf, s1, s2):
    half = A_ROPE // 2
    w = dd.shape[-1]
    return dd * cosf + pltpu.roll(dd * s1, half, 1) + pltpu.roll(dd * s2, w - half, 1)


A_BIAS_LANE_K = A_NOPE + A_ROPE
A_BIAS_LANE_V = A_V


def _hi_lo(val):
    hi = val.astype(BF16)
    return hi, (val - hi.astype(F32)).astype(BF16)


def _rope_fwd(qpre, kpre, vpre, gates, pos, name, tm=512):
    t, w = qpre.shape

    def body(q_ref, k_ref, v_ref, g_ref, p_ref, qo_ref, ko_ref, vo_ref):
        cosf, s1, s2 = _rope_tables(p_ref[...], LANES)
        lane = lax.broadcasted_iota(jnp.int32, (1, LANES), 1)
        kr = jnp.where((lane >= A_NOPE) & (lane < A_NOPE + A_ROPE), g_ref[...], 0.0)
        kr = _rope_apply(kr, cosf, s1, s2)
        kr = jnp.where((lane == A_BIAS_LANE_K) | (lane == A_BIAS_LANE_K + 1), 1.0, kr)
        v_one = (lane == A_BIAS_LANE_V) | (lane == A_BIAS_LANE_V + 1)
        for hd in range(A_HEADS):
            sl = slice(hd * LANES, (hd + 1) * LANES)
            qo_ref[:, sl] = (_rope_apply(q_ref[:, sl], cosf, s1, s2) * ATTN_SCALE).astype(qo_ref.dtype)
            ko_ref[:, sl] = (k_ref[:, sl] + kr).astype(ko_ref.dtype)
            vo_ref[:, sl] = jnp.where(v_one, jnp.ones((), BF16), v_ref[:, sl])

    row = pl.BlockSpec((tm, w), lambda i: (i, 0))
    out = jax.ShapeDtypeStruct((t, w), BF16)
    return pl.pallas_call(
        body, name=name, grid=(t // tm,),
        in_specs=[row, row, row, pl.BlockSpec((tm, LANES), lambda i: (i, 0)), pl.BlockSpec((tm, 1), lambda i: (i, 0))],
        out_specs=[row, row, row], out_shape=[out, out, out], compiler_params=_cparams("parallel"),
    )(qpre, kpre, vpre, gates, pos)


def _rope_bwd(dq, dk, dgates, pos, name, tm=512):
    t, w = dq.shape

    def body(dq_ref, dk_ref, dg_ref, p_ref, dqo_ref, dgo_ref):
        cosf, s1, s2 = _rope_tables(p_ref[...], LANES)
        acc = jnp.zeros((tm, LANES), F32)
        for hd in range(A_HEADS):
            sl = slice(hd * LANES, (hd + 1) * LANES)
            dqo_ref[:, sl] = (_rope_apply_t(dq_ref[:, sl], cosf, s1, s2) * ATTN_SCALE).astype(dqo_ref.dtype)
            acc = acc + dk_ref[:, sl].astype(F32)
        lane = lax.broadcasted_iota(jnp.int32, (1, LANES), 1)
        dkr = _rope_apply_t(acc, cosf, s1, s2)
        dkr = jnp.where((lane >= A_NOPE) & (lane < A_NOPE + A_ROPE), dkr, 0.0)
        dgo_ref[...] = (dg_ref[...] + dkr).astype(dgo_ref.dtype)

    row = pl.BlockSpec((tm, w), lambda i: (i, 0))
    nar = pl.BlockSpec((tm, LANES), lambda i: (i, 0))
    return pl.pallas_call(
        body, name=name, grid=(t // tm,),
        in_specs=[row, row, nar, pl.BlockSpec((tm, 1), lambda i: (i, 0))],
        out_specs=[row, nar],
        out_shape=[jax.ShapeDtypeStruct((t, w), BF16), jax.ShapeDtypeStruct((t, LANES), BF16)],
        compiler_params=_cparams("parallel"),
    )(dq, dk, dgates, pos)


A_TQ = 512
A_TK = 512


A_HEADS_PER_STEP = 2


def _flash_fwd(q, k, v, name):
    t = q.shape[0]
    tq, tk = A_TQ, A_TK
    nh = A_HEADS_PER_STEP
    wblk = nh * LANES

    def body(q_ref, k_ref, v_ref, o_ref, qb_ref):
        i = pl.program_id(1)
        lane = lax.broadcasted_iota(jnp.int32, (tq, LANES), 1)
        qpos = i * tq + lax.broadcasted_iota(jnp.int32, (tq, tk), 0)
        kofs = lax.broadcasted_iota(jnp.int32, (tq, tk), 1)
        qs = [q_ref[:, hh * LANES:(hh + 1) * LANES] for hh in range(nh)]

        def step(kb, carry, masked):
            rows = pl.ds(pl.multiple_of(kb * tk, tk), tk)
            new = []
            for hh in range(nh):
                m, acc = carry[hh]
                kh = k_ref[rows, hh * LANES:(hh + 1) * LANES]
                s = lax.dot_general(qs[hh], kh, _DIMS["nt"], preferred_element_type=F32)
                if masked:
                    s = jnp.where(kb * tk + kofs <= qpos, s, NEG_INF)
                m_new = jnp.maximum(m, jnp.max(s, axis=1, keepdims=True))
                p = jnp.exp(s - m_new)
                acc = jnp.exp(m - m_new) * acc + jnp.dot(p.astype(BF16), v_ref[rows, hh * LANES:(hh + 1) * LANES],
                                                          preferred_element_type=F32)
                new.append((m_new, acc))
            return tuple(new)

        carry = tuple((jnp.full((tq, 1), NEG_INF, F32), jnp.zeros((tq, LANES), F32)) for _ in range(nh))
        n_full = (i * tq) // tk
        carry = lax.fori_loop(0, n_full, functools.partial(step, masked=False), carry)
        for d in range(tq // tk):
            carry = step(n_full + d, carry, True)
        for hh in range(nh):
            m, acc = carry[hh]
            l = acc[:, A_BIAS_LANE_V:A_BIAS_LANE_V + 1]
            o_ref[:, hh * LANES:(hh + 1) * LANES] = (acc / l).astype(o_ref.dtype)
            hi, lo = _hi_lo(-(m + jnp.log(l)))
            qb_ref[:, hh * LANES:(hh + 1) * LANES] = jnp.where(
                lane == A_BIAS_LANE_K, hi, jnp.where(lane == A_BIAS_LANE_K + 1, lo, qs[hh]))

    blk = pl.BlockSpec((tq, wblk), lambda j, i: (i, j))
    res = pl.BlockSpec((t, wblk), lambda j, i: (0, j))
    out = jax.ShapeDtypeStruct((t, A_HEADS * LANES), BF16)
    return pl.pallas_call(
        body, name=name, grid=(A_HEADS // nh, t // tq), in_specs=[blk, res, res], out_specs=[blk, blk],
        out_shape=[out, out], compiler_params=_cparams("parallel", "parallel"),
    )(q, k, v)


def _attn_delta(do, o, name, tm=512):
    t, w = do.shape

    def body(do_ref, o_ref, out_ref):
        lane = lax.broadcasted_iota(jnp.int32, (tm, LANES), 1)
        for hd in range(A_HEADS):
            sl = slice(hd * LANES, (hd + 1) * LANES)
            d = do_ref[:, sl]
            delta = jnp.sum(jnp.where(lane < A_V, d.astype(F32) * o_ref[:, sl].astype(F32), 0.0), axis=1, keepdims=True)
            hi, lo = _hi_lo(-delta)
            out_ref[:, sl] = jnp.where(lane == A_BIAS_LANE_V, hi, jnp.where(lane == A_BIAS_LANE_V + 1, lo, d))

    row = pl.BlockSpec((tm, w), lambda i: (i, 0))
    return pl.pallas_call(
        body, name=name, grid=(t // tm,), in_specs=[row, row], out_specs=row,
        out_shape=jax.ShapeDtypeStruct((t, w), BF16), compiler_params=_cparams("parallel"),
    )(do, o)


def _flash_bwd(qb, k, v, doe, name):
    t = qb.shape[0]
    tq, tk = A_TQ, A_TK
    assert tq == tk
    nh = A_HEADS_PER_STEP
    wblk = nh * LANES
    nq = t // tq

    def body(q_ref, k_ref, v_ref, do_ref, dq_ref, dk_ref, dv_ref):
        kb = pl.program_id(1)

        @pl.when(kb == 0)
        def _():
            dq_ref[...] = jnp.zeros_like(dq_ref)

        kpos = kb * tk + lax.broadcasted_iota(jnp.int32, (tk, tq), 0)
        qofs = lax.broadcasted_iota(jnp.int32, (tk, tq), 1)
        ks = [k_ref[:, hh * LANES:(hh + 1) * LANES] for hh in range(nh)]
        vs = [v_ref[:, hh * LANES:(hh + 1) * LANES] for hh in range(nh)]

        def step(qi, carry, masked):
            rows = pl.ds(pl.multiple_of(qi * tq, tq), tq)
            new = []
            for hh in range(nh):
                dk_acc, dv_acc = carry[hh]
                qh = q_ref[rows, hh * LANES:(hh + 1) * LANES]
                doh = do_ref[rows, hh * LANES:(hh + 1) * LANES]
                p_t = jnp.exp(lax.dot_general(ks[hh], qh, _DIMS["nt"], preferred_element_type=F32))
                if masked:
                    p_t = jnp.where(qi * tq + qofs >= kpos, p_t, 0.0)
                ds_t = (p_t * lax.dot_general(vs[hh], doh, _DIMS["nt"], preferred_element_type=F32)).astype(BF16)
                dv_acc = dv_acc + jnp.dot(p_t.astype(BF16), doh, preferred_element_type=F32)
                dk_acc = dk_acc + jnp.dot(ds_t, qh, preferred_element_type=F32)
                dq_ref[rows, hh * LANES:(hh + 1) * LANES] += lax.dot_general(ds_t, ks[hh], _DIMS["tn"],
                                                                             preferred_element_type=F32)
                new.append((dk_acc, dv_acc))
            return tuple(new)

        carry = tuple((jnp.zeros((tk, LANES), F32), jnp.zeros((tk, LANES), F32)) for _ in range(nh))
        carry = step(kb, carry, True)
        carry = lax.fori_loop(kb + 1, nq, functools.partial(step, masked=False), carry)
        for hh in range(nh):
            dk_ref[:, hh * LANES:(hh + 1) * LANES] = carry[hh][0].astype(dk_ref.dtype)
            dv_ref[:, hh * LANES:(hh + 1) * LANES] = carry[hh][1].astype(dv_ref.dtype)

    res = pl.BlockSpec((t, wblk), lambda j, kb: (0, j))
    blk = pl.BlockSpec((tk, wblk), lambda j, kb: (kb, j))
    wide = (t, A_HEADS * LANES)
    return pl.pallas_call(
        body, name=name, grid=(A_HEADS // nh, t // tk), in_specs=[res, blk, blk, res], out_specs=[res, blk, blk],
        out_shape=[jax.ShapeDtypeStruct(wide, F32), jax.ShapeDtypeStruct(wide, BF16), jax.ShapeDtypeStruct(wide, BF16)],
        compiler_params=_cparams("parallel", "arbitrary"),
    )(qb, k, v, doe)


_SPLITS = (M_WIDTH, M_WIDTH, M_WIDTH, M_WIDTH, M_HEADS, M_HEADS, Q_RANK, KV_RANK, A_ROPE)
_OFFS = tuple(sum(_SPLITS[:i]) for i in range(len(_SPLITS) + 1))
_GATE_BLOCK_KR = A_NOPE


def _split_w_in_t(wt):
    z = lambda n: jnp.zeros((n, wt.shape[1]), wt.dtype)
    w_g = jnp.concatenate([wt[_OFFS[4]:_OFFS[6]], z(_GATE_BLOCK_KR - 2 * M_HEADS), wt[_OFFS[8]:_OFFS[9]],
                           z(LANES - _GATE_BLOCK_KR - A_ROPE)], axis=0)
    return wt[_OFFS[6]:_OFFS[8]], w_g


def _merge_w_in_grad_t(g_m4, g_a2, g_g):
    return jnp.concatenate([g_m4, g_g[:2 * M_HEADS], g_a2, g_g[_GATE_BLOCK_KR:_GATE_BLOCK_KR + A_ROPE]], axis=0)


def _pad_heads(w, used):
    w3 = w.reshape(A_HEADS, used, w.shape[1])
    return jnp.pad(w3, ((0, 0), (0, LANES - used), (0, 0))).reshape(A_HEADS * LANES, w.shape[1])


def _unpad_heads(g, used):
    return g.reshape(A_HEADS, LANES, g.shape[1])[:, :used].reshape(A_HEADS * used, g.shape[1])


def _split_w_kv_b_t(wt):
    w3 = wt.reshape(A_HEADS, A_NOPE + A_V, wt.shape[1])
    pad = lambda part: jnp.pad(part, ((0, 0), (0, LANES - part.shape[1]), (0, 0))).reshape(A_HEADS * LANES, wt.shape[1])
    return pad(w3[:, :A_NOPE]), pad(w3[:, A_NOPE:])


def _merge_w_kv_b_grad_t(gk, gv):
    gk3 = gk.reshape(A_HEADS, LANES, gk.shape[1])[:, :A_NOPE]
    gv3 = gv.reshape(A_HEADS, LANES, gv.shape[1])[:, :A_V]
    return jnp.concatenate([gk3, gv3], axis=1).reshape(A_HEADS * (A_NOPE + A_V), gk.shape[1])


def _local_step(x, pos, target, nw, w):
    w_in = w["w_in"]
    w_a2, w_g = _split_w_in_t(w_in)
    w_qb = _pad_heads(w["w_q_b"], A_NOPE + A_ROPE)
    w_k, w_v = _split_w_kv_b_t(w["w_kv_b"])
    w_out_m, w_out_a = w["w_out"][:M_WIDTH], _pad_heads(w["w_out"][M_WIDTH:], A_V)
    w_gate, w_up, w_down = w["w_gate"], w["w_up"], w["w_down"]
    bias = jnp.pad(nw["b_gates"], ((0, 0), (0, LANES - 2 * M_HEADS)))
    mnorm = nw["mlstm_norm_w"].reshape(M_HEADS, M_HEAD_DIM)
    n_m4 = 4 * M_WIDTH

    u1 = _rmsnorm_fwd(x, nw["attn_norm_w"], "attn_norm")
    m4 = _matmul(u1, w_in, "nt", BF16, "proj_mlstm", tn=M_WIDTH, b_rows=n_m4, lead_scale=(M_WIDTH, M_SCALE))
    a2 = _matmul(u1, w_a2, "nt", BF16, "proj_latents", tn=640)
    gates = _matmul(u1, w_g, "nt", F32, "proj_gates")
    hm, c_prev_all, nm_all = _mlstm_fwd(m4, gates, bias, mnorm, "mlstm_fwd")
    qa, kva = a2[:, :Q_RANK], a2[:, Q_RANK:]
    qa_n = _rmsnorm_fwd(qa, nw["q_a_norm_w"], "q_a_norm")
    kv_n = _rmsnorm_fwd(kva, nw["kv_a_norm_w"], "kv_a_norm")
    qpre = _matmul(qa_n, w_qb, "nt", F32, "q_up")
    kpre = _matmul(kv_n, w_k, "nt", F32, "k_up")
    vpre = _matmul(kv_n, w_v, "nt", BF16, "v_up")
    qr, kr, vv = _rope_fwd(qpre, kpre, vpre, gates, pos, "rope_fwd")
    ha, qb = _flash_fwd(qr, kr, vv, "attn_fwd")
    h1 = _matmul(hm, w_out_m, "nn", F32, "out_proj_m", residual=x)
    h1 = _matmul(ha, w_out_a, "nn", F32, "out_proj_a", residual=h1)
    u2 = _rmsnorm_fwd(h1, nw["ffn_norm_w"], "ffn_norm")
    gg = _matmul(u2, w_gate, "nt", BF16, "ffn_gate", tn=1408)
    uu = _matmul(u2, w_up, "nt", BF16, "ffn_up", tn=1408)
    act = _swiglu_fwd(gg, uu, "swiglu_fwd")
    h2 = _matmul(act, w_down, "nn", F32, "ffn_down", residual=h1)

    dh2, dh2_b, loss, g_final = _final_loss(h2, target, nw["final_norm_w"].reshape(1, D_MODEL), "final_loss")
    grads = {"final_norm_w": g_final.reshape(D_MODEL)}
    dact = _matmul(dh2_b, w_down, "nt", BF16, "d_act", tn=1408)
    grads["w_down"] = _matmul(act, dh2_b, "tn", F32, "g_w_down", tm=256, tn=D_MODEL)
    dgg, duu = _swiglu_bwd(gg, uu, dact, "swiglu_bwd")
    du2 = _matmul(dgg, w_gate, "nn", F32, "d_u2_gate")
    du2 = _matmul(duu, w_up, "nn", F32, "d_u2_up", residual=du2)
    grads["w_gate"] = _matmul(dgg, u2, "tn", F32, "g_w_gate", tm=256, tn=D_MODEL)
    grads["w_up"] = _matmul(duu, u2, "tn", F32, "g_w_up", tm=256, tn=D_MODEL)
    dh1, dh1_b, grads["ffn_norm_w"] = _rmsnorm_bwd(du2, h1, nw["ffn_norm_w"], "ffn_norm_bwd", F32, residual=dh2,
                                                   also_bf16=True)
    dhm = _matmul(dh1_b, w_out_m, "nt", BF16, "d_hm")
    dha = _matmul(dh1_b, w_out_a, "nt", BF16, "d_ha")
    grads["w_out"] = jnp.concatenate([_matmul(hm, dh1_b, "tn", F32, "g_w_out_m", tn=D_MODEL),
                                      _unpad_heads(_matmul(ha, dh1_b, "tn", F32, "g_w_out_a", tn=D_MODEL), A_V)], axis=0)

    late_part = _chip_partials_of(LATE, grads, "late")
    late_part, dha = lax.optimization_barrier((late_part, dha))
    late_got3 = _exchange_partials_async(late_part)

    dqr, dkr, dvv = _flash_bwd(qb, kr, vv, _attn_delta(dha, ha, "attn_delta"), "attn_bwd")
    dm4, dgates, small = _mlstm_bwd(m4, gates, bias, mnorm, c_prev_all, nm_all, dhm, "mlstm_bwd")
    grads["mlstm_norm_w"] = small[:M_HEADS].reshape(1, M_HEADS, M_HEAD_DIM)
    grads["b_gates"] = small[M_HEADS:M_HEADS + 1, :2 * M_HEADS]
    dqpre, dgk = _rope_bwd(dqr, dkr, dgates, pos, "rope_bwd")
    dqa_n = _matmul(dqpre, w_qb, "nn", BF16, "d_qa_n")
    grads["w_q_b"] = _unpad_heads(_matmul(dqpre, qa_n, "tn", F32, "g_w_q_b"), A_NOPE + A_ROPE)
    dkv_n = _matmul(dkr, w_k, "nn", F32, "d_kv_n_k")
    dkv_n = _matmul(dvv, w_v, "nn", BF16, "d_kv_n_v", residual=dkv_n)
    grads["w_kv_b"] = _merge_w_kv_b_grad_t(_matmul(dkr, kv_n, "tn", F32, "g_w_k"),
                                           _matmul(dvv, kv_n, "tn", F32, "g_w_v"))
    dqa, grads["q_a_norm_w"] = _rmsnorm_bwd(dqa_n, qa, nw["q_a_norm_w"], "q_a_norm_bwd", BF16)
    dkva, grads["kv_a_norm_w"] = _rmsnorm_bwd(dkv_n, kva, nw["kv_a_norm_w"], "kv_a_norm_bwd", BF16)
    da2 = jnp.concatenate([dqa, dkva], axis=1)
    du1 = _matmul(dm4, w_in, "nn", F32, "d_u1_m4", b_rows=n_m4)
    du1 = _matmul(da2, w_a2, "nn", F32, "d_u1_a2", residual=du1)
    du1 = _matmul(dgk, w_g, "nn", F32, "d_u1_g", residual=du1)
    grads["w_in"] = _merge_w_in_grad_t(_matmul(dm4, u1, "tn", F32, "g_w_m4", tn=D_MODEL),
                                       _matmul(da2, u1, "tn", F32, "g_w_a2", tm=640, tn=D_MODEL),
                                       _matmul(dgk, u1, "tn", F32, "g_w_g", tn=D_MODEL))
    grad_x, grads["attn_norm_w"] = _rmsnorm_bwd(du1, x, nw["attn_norm_w"], "attn_norm_bwd", F32, residual=dh1)
    return loss, grad_x, grads, (late_part, late_got3)


MESH = pl.DeviceIdType.MESH
N_CHIPS = 4
EARLY = ("w_in", "w_q_b", "w_kv_b")
LATE = ("w_out", "w_gate", "w_up", "w_down")
BIG = EARLY + LATE
TRANSPOSED = ("w_in", "w_q_b", "w_kv_b", "w_gate", "w_up")
LANE_HALVED = ("w_in", "w_out", "w_gate", "w_up", "w_down")
SMALL = ("attn_norm_w", "b_gates", "mlstm_norm_w", "q_a_norm_w", "kv_a_norm_w", "ffn_norm_w", "final_norm_w")
HBM_SPEC = pl.BlockSpec(memory_space=pltpu.HBM)


def _stored(name, a):
    return a[0].T if name in TRANSPOSED else a[0]


def _unstored(name, a):
    return (a.T if name in TRANSPOSED else a)[None]


def _half_shape(name, shape):
    rs, cs = shape
    return (rs, cs // 2) if name in LANE_HALVED else (rs // 2, cs)


def _half(ref, name, h, *lead):
    rs, cs = ref.shape[-2:]
    if name in LANE_HALVED:
        return ref.at[(*lead, slice(None), pl.ds(h * (cs // 2), cs // 2))]
    return ref.at[(*lead, pl.ds(h * (rs // 2), rs // 2), slice(None))]


def _place():
    x, y, c = lax.axis_index("x"), lax.axis_index("y"), lax.axis_index("c")
    others = [(1 - x, y), (x, 1 - y), (1 - x, 1 - y)]
    return x, y, c, others


def _gather_weights(names, shards):
    n = len(shards)

    def body(*refs):
        _gather_body(names, refs[:n], refs[n:2 * n], *refs[2 * n:])

    gathered = pl.pallas_call(
        body, name="gather_weights", in_specs=[HBM_SPEC] * n, out_specs=[HBM_SPEC] * n,
        out_shape=[jax.ShapeDtypeStruct((N_CHIPS,) + s.shape, s.dtype) for s in shards],
        scratch_shapes=[pltpu.SemaphoreType.DMA((6 * n,)), pltpu.SemaphoreType.DMA((6 * n,))],
    )(*shards)
    return _with_own_slab(gathered, shards)


def _with_own_slab(gathered, shards):
    me = 2 * lax.axis_index("x") + lax.axis_index("y")
    return [lax.dynamic_update_slice(g, s[None], (me, 0, 0)) for g, s in zip(gathered, shards)]


def _gather_body(names, ins, outs, send_sems, recv_sems):
    x, y, c, others = _place()
    me = 2 * x + y
    sibling = (x, y, 1 - c)

    def copy(w, k, slab, core, to, src=None):
        dst = _half(outs[w], names[w], core, slab)
        return pltpu.make_async_remote_copy(
            src_ref=dst if src is None else src, dst_ref=dst, send_sem=send_sems.at[w * 6 + k],
            recv_sem=recv_sems.at[w * 6 + k], device_id=to, device_id_type=MESH)

    sends = []
    for w in range(len(names)):
        for j, chip in enumerate(others):
            cp = copy(w, j, me, c, (*chip, c), src=_half(ins[w], names[w], c))
            cp.start()
            sends.append(cp)
    for w in range(len(names)):
        for j, (ox, oy) in enumerate(others):
            slab = 2 * ox + oy
            copy(w, j, slab, c, (x, y, c)).wait_recv()
            fwd = copy(w, 3 + j, slab, c, sibling)
            fwd.start()
            sends.append(fwd)
    for w in range(len(names)):
        for j, (ox, oy) in enumerate(others):
            copy(w, 3 + j, 2 * ox + oy, 1 - c, (x, y, c)).wait_recv()
    for cp in sends:
        cp.wait_send()


GATHER_LATE_COLLECTIVE_ID = 1


def _gather_weights_async(names, shards):
    n = len(shards)
    src = [jax.new_ref(s, memory_space=pltpu.MemorySpace.HBM) for s in shards]
    dst = [jax.empty_ref(jax.ShapeDtypeStruct((N_CHIPS,) + s.shape, s.dtype), memory_space=pltpu.MemorySpace.HBM)
           for s in shards]

    @pl.kernel(mesh=plsc.ScalarSubcoreMesh(axis_name="sequencer", num_cores=1), name="gather_weights_async",
               scratch_types=(pltpu.SemaphoreType.DMA((6 * n,)), pltpu.SemaphoreType.DMA((6 * n,))),
               compiler_params=pltpu.CompilerParams(collective_id=GATHER_LATE_COLLECTIVE_ID))
    def launch(send_sems, recv_sems):
        x, y, c, others = _place()
        peers = [(ox, oy, c) for ox, oy in others] + [(x, y, 1 - c)]
        barrier = pltpu.get_barrier_semaphore()
        for peer in peers:
            pl.semaphore_signal(barrier, inc=1, device_id=peer, device_id_type=MESH)
        pl.semaphore_wait(barrier, len(peers))
        _gather_body(names, src, dst, send_sems, recv_sems)

    launch()
    return _with_own_slab([d[...] for d in dst], shards)


def _exchange(arrays, out_shapes, plan, copies_per_array, name):
    n = len(arrays)

    def body(*refs):
        ins, outs = refs[:n], refs[n:2 * n]
        send_sems, recv_sems = refs[2 * n:]
        rem = [pltpu.make_async_remote_copy(src_ref=s, dst_ref=d, send_sem=send_sems.at[k], recv_sem=recv_sems.at[k],
                                            device_id=to, device_id_type=MESH)
               for k, (s, d, to) in enumerate(plan(ins, outs, _place()))]
        for cp in rem:
            cp.start()
        for cp in rem:
            cp.wait_recv()
        for cp in rem:
            cp.wait_send()

    return pl.pallas_call(
        body, name=name, in_specs=[HBM_SPEC] * n, out_specs=[HBM_SPEC] * n, out_shape=out_shapes,
        scratch_shapes=[pltpu.SemaphoreType.DMA((copies_per_array * n,)),
                        pltpu.SemaphoreType.DMA((copies_per_array * n,))],
    )(*arrays)


def _plan_to_sibling(ins, outs, place):
    x, y, c, _ = place
    return [(ins[w], outs[w], (x, y, 1 - c)) for w in range(len(ins))]


def _plan_sibling_halves(names, ins, outs, place):
    x, y, c, _ = place
    return [(_half(ins[w], names[w], 1 - c, slice(None)), outs[w], (x, y, 1 - c)) for w in range(len(ins))]


def _plan_chip_partials(ins, outs, place):
    x, y, c, others = place
    return [(ins[w].at[2 * ox + oy], outs[w].at[j], (ox, oy, c))
            for w in range(len(ins)) for j, (ox, oy) in enumerate(others)]


def _chip_partial(g3, got, name, core):
    hs = tuple(got.shape[1:])
    if name in LANE_HALVED:
        g_in, g_spec = g3, pl.BlockSpec((1,) + hs, lambda s, c_ref: (s, 0, c_ref[0]))
    else:
        g_in, g_spec = g3.reshape((N_CHIPS, 2) + hs), pl.BlockSpec((1, 1) + hs, lambda s, c_ref: (s, c_ref[0], 0, 0))
    blk = pl.BlockSpec((1,) + hs, lambda s, c_ref: (s, 0, 0))

    def body(c_ref, g_ref, r_ref, o_ref):
        o_ref[0] = (g_ref[(0,) * (len(g_ref.shape) - 2)] + r_ref[0].astype(F32)).astype(o_ref.dtype)

    return pl.pallas_call(
        body, name="chip_partial_%s" % name,
        grid_spec=pltpu.PrefetchScalarGridSpec(num_scalar_prefetch=1, grid=(N_CHIPS,), in_specs=[g_spec, blk],
                                               out_specs=blk),
        out_shape=jax.ShapeDtypeStruct((N_CHIPS,) + hs, BF16), compiler_params=_cparams("parallel"),
    )(core, g_in, got)


def _shard_sum(part, got3, name, me):
    hs = tuple(part.shape[1:])

    def body(me_ref, p_ref, a_ref, b_ref, c_ref, o_ref):
        o_ref[...] = ((p_ref[0].astype(F32) + a_ref[0].astype(F32)) + b_ref[0].astype(F32)) + c_ref[0].astype(F32)

    slot = lambda j: pl.BlockSpec((1,) + hs, lambda i, me_ref: (j, 0, 0))
    return pl.pallas_call(
        body, name="shard_sum_%s" % name,
        grid_spec=pltpu.PrefetchScalarGridSpec(
            num_scalar_prefetch=1, grid=(1,),
            in_specs=[pl.BlockSpec((1,) + hs, lambda i, me_ref: (me_ref[0], 0, 0)), slot(0), slot(1), slot(2)],
            out_specs=pl.BlockSpec(hs, lambda i, me_ref: (0, 0))),
        out_shape=jax.ShapeDtypeStruct(hs, F32), compiler_params=_cparams("arbitrary"),
    )(me, part, got3, got3, got3)


def _adamw_halves(w, m, v, mine, other, name, core):
    hs = tuple(mine.shape)
    full = pl.BlockSpec(hs, (lambda h, c_ref: (0, h)) if name in LANE_HALVED else (lambda h, c_ref: (h, 0)))
    half = pl.BlockSpec(hs, lambda h, c_ref: (0, 0))

    def body(c_ref, w_ref, m_ref, v_ref, a_ref, b_ref, g_ref, d_ref, mo_ref, vo_ref):
        g = jnp.where(pl.program_id(0) == c_ref[0], a_ref[...], b_ref[...])
        delta, m_new, v_new = _adamw_math(w_ref[...], g, m_ref[...], v_ref[...])
        g_ref[...] = g
        d_ref[...] = delta
        mo_ref[...] = m_new
        vo_ref[...] = v_new

    out = jax.ShapeDtypeStruct(w.shape, F32)
    return pl.pallas_call(
        body, name="adamw_%s" % name,
        grid_spec=pltpu.PrefetchScalarGridSpec(num_scalar_prefetch=1, grid=(2,), in_specs=[full, full, full, half, half],
                                               out_specs=[full] * 4),
        out_shape=[out] * 4, compiler_params=_cparams("parallel"),
    )(core, w, m, v, mine, other)


def _adamw_math(w, g, m, v):
    m = ADAM_B1 * m + (1.0 - ADAM_B1) * g
    v = ADAM_B2 * v + (1.0 - ADAM_B2) * (g * g)
    m_hat = m / (1.0 - ADAM_B1 ** ADAM_STEP)
    v_hat = v / (1.0 - ADAM_B2 ** ADAM_STEP)
    delta = -ADAM_LR * (m_hat / (jnp.sqrt(v_hat) + ADAM_EPS) + ADAM_WD * w)
    return delta, m, v


def _core_index():
    return lax.axis_index("c").astype(jnp.int32).reshape(1)


def _chip_partials_of(names, grads, tag):
    g3 = [grads[n].reshape(N_CHIPS, grads[n].shape[0] // N_CHIPS, grads[n].shape[1]) for n in names]
    shapes = [jax.ShapeDtypeStruct((N_CHIPS,) + _half_shape(n, g.shape[1:]), BF16) for n, g in zip(names, g3)]
    got = _exchange([g.astype(BF16) for g in g3], shapes, functools.partial(_plan_sibling_halves, names), 1,
                    "sibling_halves_" + tag)
    return [_chip_partial(g, r, n, _core_index()) for n, g, r in zip(names, g3, got)]


REDUCE_LATE_COLLECTIVE_ID = 2


def _exchange_partials_async(part):
    n = len(part)
    src = [jax.new_ref(p, memory_space=pltpu.MemorySpace.HBM) for p in part]
    dst = [jax.empty_ref(jax.ShapeDtypeStruct((3,) + p.shape[1:], p.dtype), memory_space=pltpu.MemorySpace.HBM)
           for p in part]

    @pl.kernel(mesh=plsc.ScalarSubcoreMesh(axis_name="sequencer", num_cores=1), name="chip_partials_async",
               scratch_types=(pltpu.SemaphoreType.DMA((3 * n,)), pltpu.SemaphoreType.DMA((3 * n,))),
               compiler_params=pltpu.CompilerParams(collective_id=REDUCE_LATE_COLLECTIVE_ID))
    def launch(send_sems, recv_sems):
        place = _place()
        x, y, c, others = place
        barrier = pltpu.get_barrier_semaphore()
        for ox, oy in others:
            pl.semaphore_signal(barrier, inc=1, device_id=(ox, oy, c), device_id_type=MESH)
        pl.semaphore_wait(barrier, len(others))
        rem = [pltpu.make_async_remote_copy(src_ref=s, dst_ref=d, send_sem=send_sems.at[k], recv_sem=recv_sems.at[k],
                                            device_id=to, device_id_type=MESH)
               for k, (s, d, to) in enumerate(_plan_chip_partials(src, dst, place))]
        for cp in rem:
            cp.start()
        for cp in rem:
            cp.wait_recv()
        for cp in rem:
            cp.wait_send()

    launch()
    return [d[...] for d in dst]


def _finish_reduce(names, part, got3):
    me = (2 * lax.axis_index("x") + lax.axis_index("y")).astype(jnp.int32).reshape(1)
    mine = [_shard_sum(p, r3, n, me) for n, p, r3 in zip(names, part, got3)]
    other = _exchange(mine, [jax.ShapeDtypeStruct(m.shape, F32) for m in mine], _plan_to_sibling, 1, "sibling_result")
    return {n: (a, b) for n, a, b in zip(names, mine, other)}


SMALL_ROWS = 8
SMALL_LAYOUT = {"attn_norm_w": (0, 0, 1024), "ffn_norm_w": (1, 0, 1024), "final_norm_w": (2, 0, 1024),
                "q_a_norm_w": (3, 0, 384), "kv_a_norm_w": (3, 384, 256), "mlstm_norm_w": (4, 0, 512),
                "b_gates": (4, 512, 8)}
LOSS_SLOT = (5, 0)


def _pack_small(vals, loss=None):
    tile = jnp.zeros((SMALL_ROWS, D_MODEL), F32)
    for n, (r, c0, width) in SMALL_LAYOUT.items():
        tile = tile.at[r, c0:c0 + width].set(vals[n].reshape(width).astype(F32))
    if loss is not None:
        tile = tile.at[LOSS_SLOT[0], LOSS_SLOT[1]].set(loss)
    return tile


def _unpack_small(tile, shapes):
    return {n: tile[r, c0:c0 + width].reshape(shapes[n]) for n, (r, c0, width) in SMALL_LAYOUT.items()}


def _small_allreduce_adamw(g_tile, w_tile, m_tile, v_tile):
    def body(g_ref, w_ref, m_ref, v_ref, gsum_ref, d_ref, mo_ref, vo_ref, slots, send_sems, recv_sems):
        x, y, c, _ = _place()
        me = 4 * x + 2 * y + c
        slots[me] = g_ref[...]
        copies = []
        for k in range(1, 8):
            dx, dy, dc = (k >> 2) & 1, (k >> 1) & 1, k & 1
            to = (x ^ dx, y ^ dy, c ^ dc)
            cp = pltpu.make_async_remote_copy(src_ref=g_ref, dst_ref=slots.at[me], send_sem=send_sems.at[k - 1],
                                              recv_sem=recv_sems.at[k - 1], device_id=to, device_id_type=MESH)
            cp.start()
            copies.append(cp)
        for k in range(1, 8):
            src = me ^ k
            pltpu.make_async_remote_copy(src_ref=g_ref, dst_ref=slots.at[src], send_sem=send_sems.at[k - 1],
                                         recv_sem=recv_sems.at[k - 1], device_id=(x, y, c),
                                         device_id_type=MESH).wait_recv()
        for cp in copies:
            cp.wait_send()
        total = slots[0]
        for d in range(1, 8):
            total = total + slots[d]
        gsum_ref[...] = total
        delta, m_new, v_new = _adamw_math(w_ref[...], total, m_ref[...], v_ref[...])
        d_ref[...] = delta
        mo_ref[...] = m_new
        vo_ref[...] = v_new

    vm = pl.BlockSpec(memory_space=pltpu.VMEM)
    tile = jax.ShapeDtypeStruct((SMALL_ROWS, D_MODEL), F32)
    return pl.pallas_call(
        body, name="small_allreduce_adamw", in_specs=[vm] * 4, out_specs=[vm] * 4, out_shape=[tile] * 4,
        scratch_shapes=[pltpu.VMEM((8, SMALL_ROWS, D_MODEL), F32), pltpu.SemaphoreType.DMA((7,)),
                        pltpu.SemaphoreType.DMA((7,))],
    )(g_tile, w_tile, m_tile, v_tile)


def kernel(x, positions, attn_norm_w, w_in, b_gates, mlstm_norm_w, q_a_norm_w, w_q_b, kv_a_norm_w, w_kv_b, w_out, ffn_norm_w, w_gate, w_up, w_down, final_norm_w, loss_target, m_attn_norm_w, m_w_in, m_b_gates, m_mlstm_norm_w, m_q_a_norm_w, m_w_q_b, m_kv_a_norm_w, m_w_kv_b, m_w_out, m_ffn_norm_w, m_w_gate, m_w_up, m_w_down, m_final_norm_w, v_attn_norm_w, v_w_in, v_b_gates, v_mlstm_norm_w, v_q_a_norm_w, v_w_q_b, v_kv_a_norm_w, v_w_kv_b, v_w_out, v_ffn_norm_w, v_w_gate, v_w_up, v_w_down, v_final_norm_w):
    names = ("attn_norm_w", "w_in", "b_gates", "mlstm_norm_w", "q_a_norm_w", "w_q_b", "kv_a_norm_w", "w_kv_b", "w_out",
             "ffn_norm_w", "w_gate", "w_up", "w_down", "final_norm_w")
    wts = dict(zip(names, (attn_norm_w, w_in, b_gates, mlstm_norm_w, q_a_norm_w, w_q_b, kv_a_norm_w, w_kv_b, w_out,
                           ffn_norm_w, w_gate, w_up, w_down, final_norm_w)))
    mom = dict(zip(names, (m_attn_norm_w, m_w_in, m_b_gates, m_mlstm_norm_w, m_q_a_norm_w, m_w_q_b, m_kv_a_norm_w,
                           m_w_kv_b, m_w_out, m_ffn_norm_w, m_w_gate, m_w_up, m_w_down, m_final_norm_w)))
    vel = dict(zip(names, (v_attn_norm_w, v_w_in, v_b_gates, v_mlstm_norm_w, v_q_a_norm_w, v_w_q_b, v_kv_a_norm_w,
                           v_w_kv_b, v_w_out, v_ffn_norm_w, v_w_gate, v_w_up, v_w_down, v_final_norm_w)))
    t = x.shape[1]

    shards = {n: _stored(n, wts[n]).astype(BF16) for n in BIG}
    early = _gather_weights(EARLY, [shards[n] for n in EARLY])
    early, late_in = lax.optimization_barrier((early, [shards[n] for n in LATE]))
    late = _gather_weights_async(LATE, late_in)
    full = {n: g.reshape(N_CHIPS * g.shape[1], g.shape[2]) for n, g in zip(EARLY + LATE, early + late)}

    nw = {n: wts[n] for n in SMALL}
    loss, grad_x, grads, (late_part, late_got3) = _local_step(x[0], positions.reshape(t, 1), loss_target[0], nw, full)

    early_part = _chip_partials_of(EARLY, grads, "early")
    early_got3 = _exchange(early_part, [jax.ShapeDtypeStruct((3,) + p.shape[1:], BF16) for p in early_part],
                           _plan_chip_partials, 3, "chip_partials_early")
    halves = _finish_reduce(BIG, list(early_part) + list(late_part), list(early_got3) + list(late_got3))
    core = _core_index()
    outs_g, outs_d, outs_m, outs_v = {}, {}, {}, {}
    for n in BIG:
        res = _adamw_halves(_stored(n, wts[n]), _stored(n, mom[n]), _stored(n, vel[n]), *halves[n], n, core)
        outs_g[n], outs_d[n], outs_m[n], outs_v[n] = [_unstored(n, r) for r in res]
    shapes = {n: wts[n].shape for n in SMALL}
    g_tile = _pack_small({n: grads[n] for n in SMALL}, loss=loss[0, 0])
    gsum, d_tile, m_tile, v_tile = _small_allreduce_adamw(g_tile, _pack_small(wts), _pack_small(mom), _pack_small(vel))
    outs_g.update(_unpack_small(gsum, shapes))
    outs_d.update(_unpack_small(d_tile, shapes))
    outs_m.update(_unpack_small(m_tile, shapes))
    outs_v.update(_unpack_small(v_tile, shapes))
    total_loss = gsum[LOSS_SLOT[0], LOSS_SLOT[1]]
    return (total_loss, grad_x[None], *[outs_g[n] for n in names], *[outs_d[n] for n in names],
            *[outs_m[n] for n in names], *[outs_v[n] for n in names])
```

```python
import functools
import math

import jax
import jax.numpy as jnp
from jax import lax
from jax.experimental import pallas as pl
from jax.experimental.pallas import tpu as pltpu
from jax.experimental.pallas import tpu_sc as plsc

F32 = jnp.float32
BF16 = jnp.bfloat16

D_MODEL = 1024
M_HEADS = 4
M_HEAD_DIM = 128
M_WIDTH = M_HEADS * M_HEAD_DIM
M_CHUNK = 64
A_HEADS = 8
A_NOPE = 64
A_ROPE = 32
A_V = 64
A_WIDTH = A_HEADS * A_V
A_QK_PAD = 128
Q_RANK = 384
KV_RANK = 256
ROPE_THETA = 10000.0
D_FF = 2816
D_IN = 2728
EPS = 1e-6
ATTN_SCALE = (A_NOPE + A_ROPE) ** -0.5
M_SCALE = M_HEAD_DIM ** -0.5

ADAM_LR = 0.001
ADAM_B1 = 0.9
ADAM_B2 = 0.999
ADAM_EPS = 1e-08
ADAM_WD = 0.01
ADAM_STEP = 10

VMEM_LIMIT_BYTES = 56 * 1024 * 1024
LANES = 128
NEG_INF = float("-inf")


def _cparams(*sem):
    return pltpu.CompilerParams(dimension_semantics=sem if sem else None, vmem_limit_bytes=VMEM_LIMIT_BYTES)


_DIMS = {"nn": (((1,), (0,)), ((), ())), "nt": (((1,), (1,)), ((), ())), "tn": (((0,), (0,)), ((), ()))}


def _matmul(a, b, mode, out_dtype, name, tm=None, tn=1024, residual=None, b_rows=None, lead_scale=None):
    b_shape = b.shape if b_rows is None else (b_rows, b.shape[1])
    if mode == "nn":
        (m, k), (k2, n) = a.shape, b_shape
    elif mode == "nt":
        (m, k), (n, k2) = a.shape, b_shape
    else:
        (k, m), (k2, n) = a.shape, b_shape
    assert k == k2, (a.shape, b.shape, mode)
    if tm is None:
        tm = 512 if mode == "tn" else 1024
    tm, tn = min(tm, m), min(tn, n)
    assert m % tm == 0 and n % tn == 0, (m, n, tm, tn)
    a_spec = pl.BlockSpec((k, tm), lambda i, j: (0, i)) if mode == "tn" else pl.BlockSpec((tm, k), lambda i, j: (i, 0))
    b_spec = pl.BlockSpec((tn, k), lambda i, j: (j, 0)) if mode == "nt" else pl.BlockSpec((k, tn), lambda i, j: (0, j))
    o_spec = pl.BlockSpec((tm, tn), lambda i, j: (i, j))
    dims = _DIMS[mode]

    def body(a_ref, b_ref, *rest):
        o_ref = rest[-1]
        acc = lax.dot_general(a_ref[...].astype(BF16), b_ref[...].astype(BF16), dims, preferred_element_type=F32)
        if lead_scale is not None:
            assert lead_scale[0] % tn == 0
            acc = acc * jnp.where(pl.program_id(1) < lead_scale[0] // tn, lead_scale[1], 1.0)
        if residual is not None:
            acc = acc + rest[0][...].astype(F32)
        o_ref[...] = acc.astype(o_ref.dtype)

    ins = [a, b] + ([residual] if residual is not None else [])
    in_specs = [a_spec, b_spec] + ([o_spec] if residual is not None else [])
    return pl.pallas_call(
        body, name=name, grid=(m // tm, n // tn), in_specs=in_specs, out_specs=o_spec,
        out_shape=jax.ShapeDtypeStruct((m, n), out_dtype), compiler_params=_cparams("parallel", "parallel"),
    )(*ins)


def _rmsnorm_fwd(x, w, name, tm=512):
    t, d = x.shape

    def body(x_ref, w_ref, o_ref):
        xf = x_ref[...].astype(F32)
        r = lax.rsqrt(jnp.mean(xf * xf, axis=-1, keepdims=True) + EPS)
        o_ref[...] = (xf * r * w_ref[...]).astype(o_ref.dtype)

    return pl.pallas_call(
        body, name=name, grid=(t // tm,),
        in_specs=[pl.BlockSpec((tm, d), lambda i: (i, 0)), pl.BlockSpec((1, d), lambda i: (0, 0))],
        out_specs=pl.BlockSpec((tm, d), lambda i: (i, 0)),
        out_shape=jax.ShapeDtypeStruct((t, d), BF16), compiler_params=_cparams("parallel"),
    )(x, w)


def _rmsnorm_bwd(dy, x, w, name, out_dtype, residual=None, also_bf16=False, tm=512):
    t, d = x.shape

    def body(dy_ref, x_ref, w_ref, *rest):
        dx_ref, dw_ref = rest[-3 if also_bf16 else -2], rest[-1]
        xf = x_ref[...].astype(F32)
        r = lax.rsqrt(jnp.mean(xf * xf, axis=-1, keepdims=True) + EPS)
        xh = xf * r
        dyf = dy_ref[...].astype(F32)
        dyh = dyf * w_ref[...]
        dx = r * (dyh - xh * jnp.mean(dyh * xh, axis=-1, keepdims=True))
        if residual is not None:
            dx = dx + rest[0][...].astype(F32)
        dx_ref[...] = dx.astype(dx_ref.dtype)
        if also_bf16:
            rest[-2][...] = dx.astype(BF16)
        part = jnp.sum(dyf * xh, axis=0, keepdims=True)

        @pl.when(pl.program_id(0) == 0)
        def _():
            dw_ref[...] = part

        @pl.when(pl.program_id(0) > 0)
        def _():
            dw_ref[...] += part

    row = pl.BlockSpec((tm, d), lambda i: (i, 0))
    vec = pl.BlockSpec((1, d), lambda i: (0, 0))
    ins = [dy, x, w] + ([residual] if residual is not None else [])
    extra = [jax.ShapeDtypeStruct((t, d), BF16)] if also_bf16 else []
    return pl.pallas_call(
        body, name=name, grid=(t // tm,), in_specs=[row, row, vec] + ([row] if residual is not None else []),
        out_specs=[row] + [row] * len(extra) + [vec],
        out_shape=[jax.ShapeDtypeStruct((t, d), out_dtype)] + extra + [jax.ShapeDtypeStruct((1, d), F32)],
        compiler_params=_cparams("arbitrary"),
    )(*ins)


def _final_loss(h, target, w, name, tm=512):
    t, d = h.shape

    def body(h_ref, t_ref, w_ref, dh_ref, dhb_ref, loss_ref, dw_ref):
        xf = h_ref[...]
        r = lax.rsqrt(jnp.mean(xf * xf, axis=-1, keepdims=True) + EPS)
        xh = xf * r
        err = xh * w_ref[...] - t_ref[...]
        part_loss = 0.5 * jnp.sum(jnp.sum(err * err, axis=-1, keepdims=True), axis=0, keepdims=True) * (1.0 / d)
        dy = err * (1.0 / d)
        dyh = dy * w_ref[...]
        dh = r * (dyh - xh * jnp.mean(dyh * xh, axis=-1, keepdims=True))
        dh_ref[...] = dh
        dhb_ref[...] = dh.astype(BF16)
        part_dw = jnp.sum(dy * xh, axis=0, keepdims=True)
        part_loss = jnp.broadcast_to(part_loss, (1, LANES))

        @pl.when(pl.program_id(0) == 0)
        def _():
            dw_ref[...] = part_dw
            loss_ref[...] = part_loss

        @pl.when(pl.program_id(0) > 0)
        def _():
            dw_ref[...] += part_dw
            loss_ref[...] += part_loss

    row = pl.BlockSpec((tm, d), lambda i: (i, 0))
    vec = pl.BlockSpec((1, d), lambda i: (0, 0))
    return pl.pallas_call(
        body, name=name, grid=(t // tm,), in_specs=[row, row, vec],
        out_specs=[row, row, pl.BlockSpec((1, LANES), lambda i: (0, 0)), vec],
        out_shape=[jax.ShapeDtypeStruct((t, d), F32), jax.ShapeDtypeStruct((t, d), BF16),
                   jax.ShapeDtypeStruct((1, LANES), F32), jax.ShapeDtypeStruct((1, d), F32)],
        compiler_params=_cparams("arbitrary"),
    )(h, target, w)


def _swiglu_fwd(g, u, name, tm=512, tn=1408):
    t, n = g.shape

    def body(g_ref, u_ref, o_ref):
        gf = g_ref[...].astype(F32)
        o_ref[...] = (gf * jax.nn.sigmoid(gf) * u_ref[...].astype(F32)).astype(o_ref.dtype)

    blk = pl.BlockSpec((tm, tn), lambda i, j: (i, j))
    return pl.pallas_call(
        body, name=name, grid=(t // tm, n // tn), in_specs=[blk, blk], out_specs=blk,
        out_shape=jax.ShapeDtypeStruct((t, n), BF16), compiler_params=_cparams("parallel", "parallel"),
    )(g, u)


def _swiglu_bwd(g, u, dact, name, tm=512, tn=1408):
    t, n = g.shape

    def body(g_ref, u_ref, d_ref, dg_ref, du_ref):
        gf = g_ref[...].astype(F32)
        uf = u_ref[...].astype(F32)
        df = d_ref[...].astype(F32)
        s = jax.nn.sigmoid(gf)
        dg_ref[...] = (df * uf * s * (1.0 + gf * (1.0 - s))).astype(dg_ref.dtype)
        du_ref[...] = (df * gf * s).astype(du_ref.dtype)

    blk = pl.BlockSpec((tm, tn), lambda i, j: (i, j))
    out = jax.ShapeDtypeStruct((t, n), BF16)
    return pl.pallas_call(
        body, name=name, grid=(t // tm, n // tn), in_specs=[blk, blk, blk], out_specs=[blk, blk],
        out_shape=[out, out], compiler_params=_cparams("parallel", "parallel"),
    )(g, u, dact)


M_BLOCK = 512
M_CHUNKS_PER_BLOCK = M_BLOCK // M_CHUNK
M_UNROLL = 2


def _log_sigmoid(z):
    return jnp.minimum(z, 0.0) - jnp.log(1.0 + jnp.exp(-jnp.abs(z)))


def _per_head(fn):
    return jnp.stack([fn(hd) for hd in range(M_HEADS)])


def _mlstm_chunk_fwd(q, k, v, gc, c_prev, n_prev, m_prev):
    L, H = M_CHUNK, M_HEADS
    tt = lax.broadcasted_iota(jnp.int32, (L, L), 0)
    ss = lax.broadcasted_iota(jnp.int32, (L, L), 1)
    eye = tt == ss
    causal = ss <= tt
    gct = jnp.transpose(gc)
    i_col = _per_head(lambda hd: gc[:, hd:hd + 1])
    f_col = _per_head(lambda hd: gc[:, H + hd:H + hd + 1])
    i_row = _per_head(lambda hd: gct[hd:hd + 1, :])
    lf_col = _log_sigmoid(f_col)
    lf_row = _log_sigmoid(_per_head(lambda hd: gct[H + hd:H + hd + 1, :]))
    b_col = jnp.sum(jnp.where(causal, lf_row, 0.0), axis=2, keepdims=True)
    b_row = jnp.sum(jnp.where(tt <= ss, lf_col, 0.0), axis=1, keepdims=True)
    g = jnp.sum(lf_col, axis=1, keepdims=True)
    a_row = g - b_row + i_row
    a_col = g - b_col + i_col
    m_loc = jnp.max(a_row, axis=2, keepdims=True)
    d_log = jnp.where(causal, b_col - b_row + i_row, NEG_INF)
    inter = b_col + m_prev
    m_t = jnp.maximum(jnp.max(d_log, axis=2, keepdims=True), inter)
    dmat = jnp.exp(d_log - m_t)
    amat = _per_head(lambda hd: lax.dot_general(q[hd], k[hd], _DIMS["nt"], preferred_element_type=F32))
    p = dmat * amat
    sig = jnp.exp(inter - m_t)
    c_prev_b = [c.astype(BF16) for c in c_prev]
    qc = _per_head(lambda hd: jnp.dot(q[hd], c_prev_b[hd], preferred_element_type=F32))
    p_b = p.astype(BF16)
    num = _per_head(lambda hd: jnp.dot(p_b[hd], v[hd], preferred_element_type=F32)) + sig * qc
    qf = _per_head(lambda hd: q[hd].astype(F32))
    qn = jnp.sum(qf * n_prev, axis=2, keepdims=True)
    den_raw = jnp.sum(p, axis=2, keepdims=True) + sig * qn
    floor = jnp.exp(-m_t)
    den = jnp.maximum(jnp.abs(den_raw), floor)
    h = num / den
    m_new = jnp.maximum(g + m_prev, m_loc)
    w_col = jnp.exp(a_col - m_new)
    w_row = jnp.exp(a_row - m_new)
    alpha = jnp.exp(g + m_prev - m_new)
    return dict(eye=eye, causal=causal, tt=tt, ss=ss, i_col=i_col, f_col=f_col, dmat=dmat, amat=amat, p=p, p_b=p_b,
                sig=sig, qc=qc, qf=qf, qn=qn, num=num, den_raw=den_raw, floor=floor, den=den, h=h, m_new=m_new,
                w_col=w_col, w_row=w_row, alpha=alpha, c_prev_b=c_prev_b, b_row=b_row, b_col=b_col, m_t=m_t)


def _chunk_major_t(xt):
    f, t = xt.shape
    return xt.reshape(f, t // M_CHUNK, M_CHUNK).transpose(1, 0, 2)


def _mlstm_fwd(m4, kt, gates, bias, norm_w, name):
    t = m4.shape[0]
    nblk = t // M_BLOCK
    nc = t // M_CHUNK
    L, dh = M_CHUNK, M_HEAD_DIM

    def body(m4_ref, kt_ref, g_ref, b_ref, w_ref, hm_ref, cp_ref, nm_ref, c_s, n_s, m_s):
        @pl.when(pl.program_id(0) == 0)
        def _():
            c_s[...] = jnp.zeros_like(c_s)
            n_s[...] = jnp.zeros_like(n_s)
            m_s[...] = jnp.zeros_like(m_s)

        row8 = lax.broadcasted_iota(jnp.int32, (8, dh), 0)

        def chunk(c, carry):
            rows = pl.ds(pl.multiple_of(c * L, L), L)
            gc = g_ref[rows, :] + b_ref[...]
            col = lambda part, hd: m4_ref[rows, part * M_WIDTH + hd * dh:part * M_WIDTH + (hd + 1) * dh]
            q = [col(0, hd) for hd in range(M_HEADS)]
            k = [col(1, hd) for hd in range(M_HEADS)]
            v = [col(2, hd) for hd in range(M_HEADS)]
            o = _per_head(lambda hd: col(3, hd).astype(F32))
            c_prev = [c_s[hd] for hd in range(M_HEADS)]
            n_prev = n_s[:, 0:1, :]
            m_prev = m_s[:, 0:1, 0:1]
            r = _mlstm_chunk_fwd(q, k, v, gc, c_prev, n_prev, m_prev)
            h = r["h"]
            hn = h * lax.rsqrt(jnp.mean(h * h, axis=-1, keepdims=True) + EPS) * w_ref[...]
            hm = (hn * jax.nn.sigmoid(o)).astype(hm_ref.dtype)
            nm = jnp.where(row8 == 0, n_prev, jnp.where(row8 == 1, m_prev, 0.0))
            kf = _per_head(lambda hd: k[hd].astype(F32))
            n_s[:, 0:1, :] = r["alpha"] * n_prev + jnp.sum(kf * r["w_col"], axis=1, keepdims=True)
            m_s[...] = jnp.broadcast_to(r["m_new"], (M_HEADS, 8, dh))
            for hd in range(M_HEADS):
                hm_ref[rows, hd * dh:(hd + 1) * dh] = hm[hd]
                cp_ref[hd, c] = r["c_prev_b"][hd]
                nm_ref[hd, c] = nm[hd]
                kw_t = (kt_ref[c, hd * dh:(hd + 1) * dh, :].astype(F32) * r["w_row"][hd]).astype(BF16)
                c_s[hd] = r["alpha"][hd] * c_prev[hd] + jnp.dot(kw_t, v[hd], preferred_element_type=F32)
            return carry

        lax.fori_loop(0, M_CHUNKS_PER_BLOCK, chunk, 0)

    return pl.pallas_call(
        body, name=name, grid=(nblk,),
        in_specs=[pl.BlockSpec((M_BLOCK, 4 * M_WIDTH), lambda i: (i, 0)),
                  pl.BlockSpec((M_CHUNKS_PER_BLOCK, M_WIDTH, L), lambda i: (i, 0, 0)),
                  pl.BlockSpec((M_BLOCK, LANES), lambda i: (i, 0)),
                  pl.BlockSpec((1, LANES), lambda i: (0, 0)),
                  pl.BlockSpec((M_HEADS, 1, dh), lambda i: (0, 0, 0))],
        out_specs=[pl.BlockSpec((M_BLOCK, M_WIDTH), lambda i: (i, 0)),
                   pl.BlockSpec((M_HEADS, M_CHUNKS_PER_BLOCK, dh, dh), lambda i: (0, i, 0, 0)),
                   pl.BlockSpec((M_HEADS, M_CHUNKS_PER_BLOCK, 8, dh), lambda i: (0, i, 0, 0))],
        out_shape=[jax.ShapeDtypeStruct((t, M_WIDTH), BF16),
                   jax.ShapeDtypeStruct((M_HEADS, nc, dh, dh), BF16),
                   jax.ShapeDtypeStruct((M_HEADS, nc, 8, dh), F32)],
        scratch_shapes=[pltpu.VMEM((M_HEADS, dh, dh), F32), pltpu.VMEM((M_HEADS, 8, dh), F32),
                        pltpu.VMEM((M_HEADS, 8, dh), F32)],
        compiler_params=_cparams("arbitrary"),
    )(m4, kt, gates, bias, norm_w)


def _mlstm_bwd(m4, qt, gates, bias, norm_w, c_prev_all, nm_all, dhm, name):
    t = m4.shape[0]
    nblk = t // M_BLOCK
    L, dh = M_CHUNK, M_HEAD_DIM

    def body(m4_ref, qt_ref, g_ref, b_ref, w_ref, cp_ref, nm_ref, dhm_ref, dm4_ref, dg_ref, small_ref, dc_s, dn_s):
        @pl.when(pl.program_id(0) == 0)
        def _():
            dc_s[...] = jnp.zeros_like(dc_s)
            dn_s[...] = jnp.zeros_like(dn_s)
            small_ref[...] = jnp.zeros_like(small_ref)

        lane = lax.broadcasted_iota(jnp.int32, (L, LANES), 1)
        row8 = lax.broadcasted_iota(jnp.int32, (8, LANES), 0)

        def chunk(ci, carry):
            c = M_CHUNKS_PER_BLOCK - 1 - ci
            rows = pl.ds(pl.multiple_of(c * L, L), L)
            gc = g_ref[rows, :] + b_ref[...]
            heads = range(M_HEADS)
            col = lambda part, hd: m4_ref[rows, part * M_WIDTH + hd * dh:part * M_WIDTH + (hd + 1) * dh]
            q = [col(0, hd) for hd in heads]
            k = [col(1, hd) for hd in heads]
            v = [col(2, hd) for hd in heads]
            o = _per_head(lambda hd: col(3, hd).astype(F32))
            q_t = [qt_ref[c, hd * dh:(hd + 1) * dh, :] for hd in heads]
            c_prev = [cp_ref[hd, c].astype(F32) for hd in heads]
            nm = _per_head(lambda hd: nm_ref[hd, c])
            n_prev = nm[:, 0:1, :]
            m_prev = nm[:, 1:2, 0:1]
            r = _mlstm_chunk_fwd(q, k, v, gc, c_prev, n_prev, m_prev)
            eye, tt, ss = r["eye"], r["tt"], r["ss"]
            h, den, sig, p = r["h"], r["den"], r["sig"], r["p"]
            w_col, alpha, c_prev_b = r["w_col"], r["alpha"], r["c_prev_b"]
            to_row = lambda colv: jnp.sum(jnp.where(eye, colv, 0.0), axis=1, keepdims=True)
            to_col = lambda rowv: jnp.sum(jnp.where(eye, rowv, 0.0), axis=2, keepdims=True)
            mm = lambda fn: _per_head(lambda hd: fn(hd))

            w_h = w_ref[...]
            rn = lax.rsqrt(jnp.mean(h * h, axis=-1, keepdims=True) + EPS)
            hh = h * rn
            og = jax.nn.sigmoid(o)
            dhm_c = _per_head(lambda hd: dhm_ref[rows, hd * dh:(hd + 1) * dh].astype(F32))
            dhn = dhm_c * og
            d_o = dhm_c * hh * w_h * og * (1.0 - og)
            d_norm = jnp.sum(dhn * hh, axis=1, keepdims=True)
            dhh = dhn * w_h
            dh_ = rn * (dhh - hh * jnp.mean(dhh * hh, axis=-1, keepdims=True))

            dnum = dh_ / den
            dden = -jnp.sum(dh_ * h, axis=-1, keepdims=True) / den
            dden_raw = jnp.where(jnp.abs(r["den_raw"]) >= r["floor"], dden * jnp.sign(r["den_raw"]), 0.0)
            dnum_b = dnum.astype(BF16)
            dp = mm(lambda hd: lax.dot_general(dnum_b[hd], v[hd], _DIMS["nt"], preferred_element_type=F32)) + dden_raw
            dc = [dc_s[hd] for hd in heads]
            dn = dn_s[:, 0:1, :]
            dc_b = [d.astype(BF16) for d in dc]
            g2_b = (sig * dnum).astype(BF16)
            sd = sig * dden_raw
            da_mat = (dp * r["dmat"]).astype(BF16)
            dqs = mm(lambda hd: lax.dot_general(g2_b[hd], c_prev_b[hd], _DIMS["nt"], preferred_element_type=F32)
                     + jnp.dot(da_mat[hd], k[hd], preferred_element_type=F32)) + sd * n_prev
            r_mat = mm(lambda hd: lax.dot_general(v[hd], dc_b[hd], _DIMS["nt"], preferred_element_type=F32)) + dn
            kf = _per_head(lambda hd: k[hd].astype(F32))
            dmat_t = jnp.exp(jnp.where(tt <= ss, r["b_row"] - r["b_col"] + r["i_col"], NEG_INF) - to_row(r["m_t"]))
            p_t = (dmat_t * mm(lambda hd: jnp.dot(k[hd], q_t[hd], preferred_element_type=F32))).astype(BF16)
            dp_t = mm(lambda hd: lax.dot_general(v[hd], dnum_b[hd], _DIMS["nt"], preferred_element_type=F32))
            da_t = ((dp_t + to_row(dden_raw)) * dmat_t).astype(BF16)
            d_k = mm(lambda hd: jnp.dot(da_t[hd], q[hd], preferred_element_type=F32)) + w_col * r_mat
            d_v = (mm(lambda hd: jnp.dot(p_t[hd], dnum_b[hd], preferred_element_type=F32))
                   + w_col * mm(lambda hd: jnp.dot(k[hd], dc_b[hd], preferred_element_type=F32)))
            da_col = jnp.sum(kf * r_mat, axis=-1, keepdims=True) * w_col
            dsig = jnp.sum(dnum * r["qc"], axis=-1, keepdims=True) + dden_raw * r["qn"]
            pp = dp * p
            r1 = jnp.sum(pp, axis=2, keepdims=True)
            c1_col = to_col(jnp.sum(pp, axis=1, keepdims=True))
            dcc = _per_head(lambda hd: dc[hd] * c_prev[hd])
            d_alpha = (jnp.sum(jnp.sum(dcc, axis=2, keepdims=True), axis=1, keepdims=True)
                       + jnp.sum(dn * n_prev, axis=2, keepdims=True))
            dgl = jnp.sum(da_col, axis=1, keepdims=True) + d_alpha * alpha
            db_col = r1 + dsig * sig - c1_col - da_col
            dli_col = c1_col + da_col
            dlf_col = jnp.sum(jnp.where(ss >= tt, to_row(db_col), 0.0), axis=2, keepdims=True) + dgl
            df_col = dlf_col * jax.nn.sigmoid(-r["f_col"])

            dn_s[:, 0:1, :] = alpha * dn + jnp.sum(sd * r["qf"], axis=1, keepdims=True)
            dq_out = (dqs * M_SCALE).astype(dm4_ref.dtype)
            dk_out, dv_out, do_out = d_k.astype(dm4_ref.dtype), d_v.astype(dm4_ref.dtype), d_o.astype(dm4_ref.dtype)
            dg_tile = jnp.zeros((L, LANES), F32)
            small = jnp.zeros((8, LANES), F32)
            for hd in heads:
                dc_s[hd] = alpha[hd] * dc[hd] + jnp.dot(q_t[hd], g2_b[hd], preferred_element_type=F32)
                dm4_ref[rows, hd * dh:(hd + 1) * dh] = dq_out[hd]
                dm4_ref[rows, M_WIDTH + hd * dh:M_WIDTH + (hd + 1) * dh] = dk_out[hd]
                dm4_ref[rows, 2 * M_WIDTH + hd * dh:2 * M_WIDTH + (hd + 1) * dh] = dv_out[hd]
                dm4_ref[rows, 3 * M_WIDTH + hd * dh:3 * M_WIDTH + (hd + 1) * dh] = do_out[hd]
                dg_tile = (dg_tile + jnp.where(lane == hd, dli_col[hd], 0.0)
                           + jnp.where(lane == M_HEADS + hd, df_col[hd], 0.0))
                small = small + jnp.where(row8 == hd, d_norm[hd], 0.0)
            dg_ref[rows, :] = dg_tile
            small = small + jnp.where(row8 == M_HEADS, jnp.sum(dg_tile, axis=0, keepdims=True), 0.0)
            small_ref[...] += small
            return carry

        lax.fori_loop(0, M_CHUNKS_PER_BLOCK, chunk, 0, unroll=M_UNROLL)

    rev = lambda i: nblk - 1 - i
    return pl.pallas_call(
        body, name=name, grid=(nblk,),
        in_specs=[pl.BlockSpec((M_BLOCK, 4 * M_WIDTH), lambda i: (rev(i), 0)),
                  pl.BlockSpec((M_CHUNKS_PER_BLOCK, M_WIDTH, L), lambda i: (rev(i), 0, 0)),
                  pl.BlockSpec((M_BLOCK, LANES), lambda i: (rev(i), 0)),
                  pl.BlockSpec((1, LANES), lambda i: (0, 0)),
                  pl.BlockSpec((M_HEADS, 1, dh), lambda i: (0, 0, 0)),
                  pl.BlockSpec((M_HEADS, M_CHUNKS_PER_BLOCK, dh, dh), lambda i: (0, rev(i), 0, 0)),
                  pl.BlockSpec((M_HEADS, M_CHUNKS_PER_BLOCK, 8, dh), lambda i: (0, rev(i), 0, 0)),
                  pl.BlockSpec((M_BLOCK, M_WIDTH), lambda i: (rev(i), 0))],
        out_specs=[pl.BlockSpec((M_BLOCK, 4 * M_WIDTH), lambda i: (rev(i), 0)),
                   pl.BlockSpec((M_BLOCK, LANES), lambda i: (rev(i), 0)),
                   pl.BlockSpec((8, LANES), lambda i: (0, 0))],
        out_shape=[jax.ShapeDtypeStruct((t, 4 * M_WIDTH), BF16), jax.ShapeDtypeStruct((t, LANES), F32),
                   jax.ShapeDtypeStruct((8, LANES), F32)],
        scratch_shapes=[pltpu.VMEM((M_HEADS, dh, dh), F32), pltpu.VMEM((M_HEADS, 8, dh), F32)],
        compiler_params=_cparams("arbitrary"),
    )(m4, qt, gates, bias, norm_w, c_prev_all, nm_all, dhm)


def _rope_tables(pos_col, width):
    lane = lax.broadcasted_iota(jnp.int32, (1, width), 1) % A_QK_PAD
    half = A_ROPE // 2
    first = (lane >= A_NOPE) & (lane < A_NOPE + half)
    second = (lane >= A_NOPE + half) & (lane < A_NOPE + A_ROPE)
    idx = jnp.where(first, lane - A_NOPE, lane - A_NOPE - half).astype(F32)
    inv_freq = jnp.exp(idx * (-math.log(ROPE_THETA) / half))
    ang = pos_col.astype(F32) * inv_freq
    cos, sin = jnp.cos(ang), jnp.sin(ang)
    rot = first | second
    return jnp.where(rot, cos, 1.0), jnp.where(first, -sin, 0.0), jnp.where(second, sin, 0.0)


def _rope_apply(vv, cosf, s1, s2):
    half = A_ROPE // 2
    w = vv.shape[-1]
    return vv * cosf + pltpu.roll(vv, w - half, 1) * s1 + pltpu.roll(vv, half, 1) * s2


def _rope_apply_t(dd, cosf, s1, s2):
    half = A_ROPE // 2
    w = dd.shape[-1]
    return dd * cosf + pltpu.roll(dd * s1, half, 1) + pltpu.roll(dd * s2, w - half, 1)


A_BIAS_LANE_K = A_NOPE + A_ROPE
A_BIAS_LANE_V = A_V


def _hi_lo(val):
    hi = val.astype(BF16)
    return hi, (val - hi.astype(F32)).astype(BF16)


def _rope_fwd(qpre, kpre, vpre, gates, pos, name, tm=512):
    t, w = qpre.shape

    def body(q_ref, k_ref, v_ref, g_ref, p_ref, qo_ref, ko_ref, vo_ref):
        cosf, s1, s2 = _rope_tables(p_ref[...], LANES)
        lane = lax.broadcasted_iota(jnp.int32, (1, LANES), 1)
        kr = jnp.where((lane >= A_NOPE) & (lane < A_NOPE + A_ROPE), g_ref[...], 0.0)
        kr = _rope_apply(kr, cosf, s1, s2)
        kr = jnp.where((lane == A_BIAS_LANE_K) | (lane == A_BIAS_LANE_K + 1), 1.0, kr)
        v_one = (lane == A_BIAS_LANE_V) | (lane == A_BIAS_LANE_V + 1)
        for hd in range(A_HEADS):
            sl = slice(hd * LANES, (hd + 1) * LANES)
            qo_ref[:, sl] = (_rope_apply(q_ref[:, sl], cosf, s1, s2) * ATTN_SCALE).astype(qo_ref.dtype)
            ko_ref[:, sl] = (k_ref[:, sl] + kr).astype(ko_ref.dtype)
            vo_ref[:, sl] = jnp.where(v_one, jnp.ones((), BF16), v_ref[:, sl])

    row = pl.BlockSpec((tm, w), lambda i: (i, 0))
    out = jax.ShapeDtypeStruct((t, w), BF16)
    return pl.pallas_call(
        body, name=name, grid=(t // tm,),
        in_specs=[row, row, row, pl.BlockSpec((tm, LANES), lambda i: (i, 0)), pl.BlockSpec((tm, 1), lambda i: (i, 0))],
        out_specs=[row, row, row], out_shape=[out, out, out], compiler_params=_cparams("parallel"),
    )(qpre, kpre, vpre, gates, pos)


def _rope_bwd(dq, dk, dgates, pos, name, tm=512):
    t, w = dq.shape

    def body(dq_ref, dk_ref, dg_ref, p_ref, dqo_ref, dgo_ref):
        cosf, s1, s2 = _rope_tables(p_ref[...], LANES)
        acc = jnp.zeros((tm, LANES), F32)
        for hd in range(A_HEADS):
            sl = slice(hd * LANES, (hd + 1) * LANES)
            dqo_ref[:, sl] = (_rope_apply_t(dq_ref[:, sl], cosf, s1, s2) * ATTN_SCALE).astype(dqo_ref.dtype)
            acc = acc + dk_ref[:, sl].astype(F32)
        lane = lax.broadcasted_iota(jnp.int32, (1, LANES), 1)
        dkr = _rope_apply_t(acc, cosf, s1, s2)
        dkr = jnp.where((lane >= A_NOPE) & (lane < A_NOPE + A_ROPE), dkr, 0.0)
        dgo_ref[...] = (dg_ref[...] + dkr).astype(dgo_ref.dtype)

    row = pl.BlockSpec((tm, w), lambda i: (i, 0))
    nar = pl.BlockSpec((tm, LANES), lambda i: (i, 0))
    return pl.pallas_call(
        body, name=name, grid=(t // tm,),
        in_specs=[row, row, nar, pl.BlockSpec((tm, 1), lambda i: (i, 0))],
        out_specs=[row, nar],
        out_shape=[jax.ShapeDtypeStruct((t, w), BF16), jax.ShapeDtypeStruct((t, LANES), BF16)],
        compiler_params=_cparams("parallel"),
    )(dq, dk, dgates, pos)


A_TQ = 512
A_TK = 512


A_HEADS_PER_STEP = 2


def _flash_fwd(q, k, v, name):
    t = q.shape[0]
    tq, tk = A_TQ, A_TK
    nh = A_HEADS_PER_STEP
    wblk = nh * LANES

    def body(q_ref, k_ref, v_ref, o_ref, qb_ref):
        i = pl.program_id(1)
        lane = lax.broadcasted_iota(jnp.int32, (tq, LANES), 1)
        qpos = i * tq + lax.broadcasted_iota(jnp.int32, (tq, tk), 0)
        kofs = lax.broadcasted_iota(jnp.int32, (tq, tk), 1)
        qs = [q_ref[:, hh * LANES:(hh + 1) * LANES] for hh in range(nh)]

        def step(kb, carry, masked):
            rows = pl.ds(pl.multiple_of(kb * tk, tk), tk)
            new = []
            for hh in range(nh):
                m, acc = carry[hh]
                kh = k_ref[rows, hh * LANES:(hh + 1) * LANES]
                s = lax.dot_general(qs[hh], kh, _DIMS["nt"], preferred_element_type=F32)
                if masked:
                    s = jnp.where(kb * tk + kofs <= qpos, s, NEG_INF)
                m_new = jnp.maximum(m, jnp.max(s, axis=1, keepdims=True))
                p = jnp.exp(s - m_new)
                acc = jnp.exp(m - m_new) * acc + jnp.dot(p.astype(BF16), v_ref[rows, hh * LANES:(hh + 1) * LANES],
                                                          preferred_element_type=F32)
                new.append((m_new, acc))
            return tuple(new)

        carry = tuple((jnp.full((tq, 1), NEG_INF, F32), jnp.zeros((tq, LANES), F32)) for _ in range(nh))
        n_full = (i * tq) // tk
        carry = lax.fori_loop(0, n_full, functools.partial(step, masked=False), carry)
        for d in range(tq // tk):
            carry = step(n_full + d, carry, True)
        for hh in range(nh):
            m, acc = carry[hh]
            l = acc[:, A_BIAS_LANE_V:A_BIAS_LANE_V + 1]
            o_ref[:, hh * LANES:(hh + 1) * LANES] = (acc / l).astype(o_ref.dtype)
            hi, lo = _hi_lo(-(m + jnp.log(l)))
            qb_ref[:, hh * LANES:(hh + 1) * LANES] = jnp.where(
                lane == A_BIAS_LANE_K, hi, jnp.where(lane == A_BIAS_LANE_K + 1, lo, qs[hh]))

    blk = pl.BlockSpec((tq, wblk), lambda j, i: (i, j))
    res = pl.BlockSpec((t, wblk), lambda j, i: (0, j))
    out = jax.ShapeDtypeStruct((t, A_HEADS * LANES), BF16)
    return pl.pallas_call(
        body, name=name, grid=(A_HEADS // nh, t // tq), in_specs=[blk, res, res], out_specs=[blk, blk],
        out_shape=[out, out], compiler_params=_cparams("parallel", "parallel"),
    )(q, k, v)


def _attn_delta(do, o, name, tm=512):
    t, w = do.shape

    def body(do_ref, o_ref, out_ref):
        lane = lax.broadcasted_iota(jnp.int32, (tm, LANES), 1)
        for hd in range(A_HEADS):
            sl = slice(hd * LANES, (hd + 1) * LANES)
            d = do_ref[:, sl]
            delta = jnp.sum(jnp.where(lane < A_V, d.astype(F32) * o_ref[:, sl].astype(F32), 0.0), axis=1, keepdims=True)
            hi, lo = _hi_lo(-delta)
            out_ref[:, sl] = jnp.where(lane == A_BIAS_LANE_V, hi, jnp.where(lane == A_BIAS_LANE_V + 1, lo, d))

    row = pl.BlockSpec((tm, w), lambda i: (i, 0))
    return pl.pallas_call(
        body, name=name, grid=(t // tm,), in_specs=[row, row], out_specs=row,
        out_shape=jax.ShapeDtypeStruct((t, w), BF16), compiler_params=_cparams("parallel"),
    )(do, o)


def _flash_bwd(qb, k, v, doe, name):
    t = qb.shape[0]
    tq, tk = A_TQ, A_TK
    assert tq == tk
    nh = A_HEADS_PER_STEP
    wblk = nh * LANES
    nq = t // tq

    def body(q_ref, k_ref, v_ref, do_ref, dq_ref, dk_ref, dv_ref):
        kb = pl.program_id(1)

        @pl.when(kb == 0)
        def _():
            dq_ref[...] = jnp.zeros_like(dq_ref)

        kpos = kb * tk + lax.broadcasted_iota(jnp.int32, (tk, tq), 0)
        qofs = lax.broadcasted_iota(jnp.int32, (tk, tq), 1)
        ks = [k_ref[:, hh * LANES:(hh + 1) * LANES] for hh in range(nh)]
        vs = [v_ref[:, hh * LANES:(hh + 1) * LANES] for hh in range(nh)]

        def step(qi, carry, masked):
            rows = pl.ds(pl.multiple_of(qi * tq, tq), tq)
            new = []
            for hh in range(nh):
                dk_acc, dv_acc = carry[hh]
                qh = q_ref[rows, hh * LANES:(hh + 1) * LANES]
                doh = do_ref[rows, hh * LANES:(hh + 1) * LANES]
                p_t = jnp.exp(lax.dot_general(ks[hh], qh, _DIMS["nt"], preferred_element_type=F32))
                if masked:
                    p_t = jnp.where(qi * tq + qofs >= kpos, p_t, 0.0)
                ds_t = (p_t * lax.dot_general(vs[hh], doh, _DIMS["nt"], preferred_element_type=F32)).astype(BF16)
                dv_acc = dv_acc + jnp.dot(p_t.astype(BF16), doh, preferred_element_type=F32)
                dk_acc = dk_acc + jnp.dot(ds_t, qh, preferred_element_type=F32)
                dq_ref[rows, hh * LANES:(hh + 1) * LANES] += lax.dot_general(ds_t, ks[hh], _DIMS["tn"],
                                                                             preferred_element_type=F32)
                new.append((dk_acc, dv_acc))
            return tuple(new)

        carry = tuple((jnp.zeros((tk, LANES), F32), jnp.zeros((tk, LANES), F32)) for _ in range(nh))
        carry = step(kb, carry, True)
        carry = lax.fori_loop(kb + 1, nq, functools.partial(step, masked=False), carry)
        for hh in range(nh):
            dk_ref[:, hh * LANES:(hh + 1) * LANES] = carry[hh][0].astype(dk_ref.dtype)
            dv_ref[:, hh * LANES:(hh + 1) * LANES] = carry[hh][1].astype(dv_ref.dtype)

    res = pl.BlockSpec((t, wblk), lambda j, kb: (0, j))
    blk = pl.BlockSpec((tk, wblk), lambda j, kb: (kb, j))
    wide = (t, A_HEADS * LANES)
    return pl.pallas_call(
        body, name=name, grid=(A_HEADS // nh, t // tk), in_specs=[res, blk, blk, res], out_specs=[res, blk, blk],
        out_shape=[jax.ShapeDtypeStruct(wide, F32), jax.ShapeDtypeStruct(wide, BF16), jax.ShapeDtypeStruct(wide, BF16)],
        compiler_params=_cparams("parallel", "arbitrary"),
    )(qb, k, v, doe)


_SPLITS = (M_WIDTH, M_WIDTH, M_WIDTH, M_WIDTH, M_HEADS, M_HEADS, Q_RANK, KV_RANK, A_ROPE)
_OFFS = tuple(sum(_SPLITS[:i]) for i in range(len(_SPLITS) + 1))
_GATE_BLOCK_KR = A_NOPE


def _split_w_in_t(wt):
    z = lambda n: jnp.zeros((n, wt.shape[1]), wt.dtype)
    w_g = jnp.concatenate([wt[_OFFS[4]:_OFFS[6]], z(_GATE_BLOCK_KR - 2 * M_HEADS), wt[_OFFS[8]:_OFFS[9]],
                           z(LANES - _GATE_BLOCK_KR - A_ROPE)], axis=0)
    return wt[_OFFS[6]:_OFFS[8]], w_g


def _merge_w_in_grad_t(g_m4, g_a2, g_g):
    return jnp.concatenate([g_m4, g_g[:2 * M_HEADS], g_a2, g_g[_GATE_BLOCK_KR:_GATE_BLOCK_KR + A_ROPE]], axis=0)


def _pad_heads(w, used):
    w3 = w.reshape(A_HEADS, used, w.shape[1])
    return jnp.pad(w3, ((0, 0), (0, LANES - used), (0, 0))).reshape(A_HEADS * LANES, w.shape[1])


def _unpad_heads(g, used):
    return g.reshape(A_HEADS, LANES, g.shape[1])[:, :used].reshape(A_HEADS * used, g.shape[1])


def _split_w_kv_b_t(wt):
    w3 = wt.reshape(A_HEADS, A_NOPE + A_V, wt.shape[1])
    pad = lambda part: jnp.pad(part, ((0, 0), (0, LANES - part.shape[1]), (0, 0))).reshape(A_HEADS * LANES, wt.shape[1])
    return pad(w3[:, :A_NOPE]), pad(w3[:, A_NOPE:])


def _merge_w_kv_b_grad_t(gk, gv):
    gk3 = gk.reshape(A_HEADS, LANES, gk.shape[1])[:, :A_NOPE]
    gv3 = gv.reshape(A_HEADS, LANES, gv.shape[1])[:, :A_V]
    return jnp.concatenate([gk3, gv3], axis=1).reshape(A_HEADS * (A_NOPE + A_V), gk.shape[1])


def _local_step(x, pos, target, nw, w):
    w_in = w["w_in"]
    w_a2, w_g = _split_w_in_t(w_in)
    w_qb = _pad_heads(w["w_q_b"], A_NOPE + A_ROPE)
    w_k, w_v = _split_w_kv_b_t(w["w_kv_b"])
    w_out_m, w_out_a = w["w_out"][:M_WIDTH], _pad_heads(w["w_out"][M_WIDTH:], A_V)
    w_gate, w_up, w_down = w["w_gate"], w["w_up"], w["w_down"]
    bias = jnp.pad(nw["b_gates"], ((0, 0), (0, LANES - 2 * M_HEADS)))
    mnorm = nw["mlstm_norm_w"].reshape(M_HEADS, 1, M_HEAD_DIM)
    n_m4 = 4 * M_WIDTH
    t_len = x.shape[0]

    u1 = _rmsnorm_fwd(x, nw["attn_norm_w"], "attn_norm")
    m4 = _matmul(u1, w_in, "nt", BF16, "proj_mlstm", tn=M_WIDTH, b_rows=n_m4, lead_scale=(M_WIDTH, M_SCALE))
    a2 = _matmul(u1, w_a2, "nt", BF16, "proj_latents", tn=640)
    gates = _matmul(u1, w_g, "nt", F32, "proj_gates")
    q_t = _chunk_major_t(_matmul(w_in[:M_WIDTH], u1, "nt", BF16, "proj_q_t", lead_scale=(t_len, M_SCALE)))
    k_t = _chunk_major_t(_matmul(w_in[M_WIDTH:2 * M_WIDTH], u1, "nt", BF16, "proj_k_t"))
    hm, c_prev_all, nm_all = _mlstm_fwd(m4, k_t, gates, bias, mnorm, "mlstm_fwd")
    qa, kva = a2[:, :Q_RANK], a2[:, Q_RANK:]
    qa_n = _rmsnorm_fwd(qa, nw["q_a_norm_w"], "q_a_norm")
    kv_n = _rmsnorm_fwd(kva, nw["kv_a_norm_w"], "kv_a_norm")
    qpre = _matmul(qa_n, w_qb, "nt", F32, "q_up")
    kpre = _matmul(kv_n, w_k, "nt", F32, "k_up")
    vpre = _matmul(kv_n, w_v, "nt", BF16, "v_up")
    qr, kr, vv = _rope_fwd(qpre, kpre, vpre, gates, pos, "rope_fwd")
    ha, qb = _flash_fwd(qr, kr, vv, "attn_fwd")
    h1 = _matmul(hm, w_out_m, "nn", F32, "out_proj_m", residual=x)
    h1 = _matmul(ha, w_out_a, "nn", F32, "out_proj_a", residual=h1)
    u2 = _rmsnorm_fwd(h1, nw["ffn_norm_w"], "ffn_norm")
    gg = _matmul(u2, w_gate, "nt", BF16, "ffn_gate", tn=1408)
    uu = _matmul(u2, w_up, "nt", BF16, "ffn_up", tn=1408)
    act = _swiglu_fwd(gg, uu, "swiglu_fwd")
    h2 = _matmul(act, w_down, "nn", F32, "ffn_down", residual=h1)

    dh2, dh2_b, loss, g_final = _final_loss(h2, target, nw["final_norm_w"].reshape(1, D_MODEL), "final_loss")
    grads = {"final_norm_w": g_final.reshape(D_MODEL)}
    dact = _matmul(dh2_b, w_down, "nt", BF16, "d_act", tn=1408)
    grads["w_down"] = _matmul(act, dh2_b, "tn", F32, "g_w_down", tm=256, tn=D_MODEL)
    dgg, duu = _swiglu_bwd(gg, uu, dact, "swiglu_bwd")
    du2 = _matmul(dgg, w_gate, "nn", F32, "d_u2_gate")
    du2 = _matmul(duu, w_up, "nn", F32, "d_u2_up", residual=du2)
    grads["w_gate"] = _matmul(dgg, u2, "tn", F32, "g_w_gate", tm=256, tn=D_MODEL)
    grads["w_up"] = _matmul(duu, u2, "tn", F32, "g_w_up", tm=256, tn=D_MODEL)
    dh1, dh1_b, grads["ffn_norm_w"] = _rmsnorm_bwd(du2, h1, nw["ffn_norm_w"], "ffn_norm_bwd", F32, residual=dh2,
                                                   also_bf16=True)
    dhm = _matmul(dh1_b, w_out_m, "nt", BF16, "d_hm")
    dha = _matmul(dh1_b, w_out_a, "nt", BF16, "d_ha")
    grads["w_out"] = jnp.concatenate([_matmul(hm, dh1_b, "tn", F32, "g_w_out_m", tn=D_MODEL),
                                      _unpad_heads(_matmul(ha, dh1_b, "tn", F32, "g_w_out_a", tn=D_MODEL), A_V)], axis=0)

    late_part = _chip_partials_of(LATE, grads, "late")
    late_part, dha = lax.optimization_barrier((late_part, dha))
    late_got3 = _exchange_partials_async(late_part)

    dqr, dkr, dvv = _flash_bwd(qb, kr, vv, _attn_delta(dha, ha, "attn_delta"), "attn_bwd")
    dm4, dgates, small = _mlstm_bwd(m4, q_t, gates, bias, mnorm, c_prev_all, nm_all, dhm, "mlstm_bwd")
    grads["mlstm_norm_w"] = small[:M_HEADS].reshape(1, M_HEADS, M_HEAD_DIM)
    grads["b_gates"] = small[M_HEADS:M_HEADS + 1, :2 * M_HEADS]
    dqpre, dgk = _rope_bwd(dqr, dkr, dgates, pos, "rope_bwd")
    dqa_n = _matmul(dqpre, w_qb, "nn", BF16, "d_qa_n")
    grads["w_q_b"] = _unpad_heads(_matmul(dqpre, qa_n, "tn", F32, "g_w_q_b"), A_NOPE + A_ROPE)
    dkv_n = _matmul(dkr, w_k, "nn", F32, "d_kv_n_k")
    dkv_n = _matmul(dvv, w_v, "nn", BF16, "d_kv_n_v", residual=dkv_n)
    grads["w_kv_b"] = _merge_w_kv_b_grad_t(_matmul(dkr, kv_n, "tn", F32, "g_w_k"),
                                           _matmul(dvv, kv_n, "tn", F32, "g_w_v"))
    dqa, grads["q_a_norm_w"] = _rmsnorm_bwd(dqa_n, qa, nw["q_a_norm_w"], "q_a_norm_bwd", BF16)
    dkva, grads["kv_a_norm_w"] = _rmsnorm_bwd(dkv_n, kva, nw["kv_a_norm_w"], "kv_a_norm_bwd", BF16)
    da2 = jnp.concatenate([dqa, dkva], axis=1)
    du1 = _matmul(dm4, w_in, "nn", F32, "d_u1_m4", b_rows=n_m4)
    du1 = _matmul(da2, w_a2, "nn", F32, "d_u1_a2", residual=du1)
    du1 = _matmul(dgk, w_g, "nn", F32, "d_u1_g", residual=du1)
    grads["w_in"] = _merge_w_in_grad_t(_matmul(dm4, u1, "tn", F32, "g_w_m4", tn=D_MODEL),
                                       _matmul(da2, u1, "tn", F32, "g_w_a2", tm=640, tn=D_MODEL),
                                       _matmul(dgk, u1, "tn", F32, "g_w_g", tn=D_MODEL))
    grad_x, grads["attn_norm_w"] = _rmsnorm_bwd(du1, x, nw["attn_norm_w"], "attn_norm_bwd", F32, residual=dh1)
    return loss, grad_x, grads, (late_part, late_got3)


MESH = pl.DeviceIdType.MESH
N_CHIPS = 4
EARLY = ("w_in", "w_q_b", "w_kv_b")
LATE = ("w_out", "w_gate", "w_up", "w_down")
BIG = EARLY + LATE
TRANSPOSED = ("w_in", "w_q_b", "w_kv_b", "w_gate", "w_up")
LANE_HALVED = ("w_in", "w_out", "w_gate", "w_up", "w_down")
SMALL = ("attn_norm_w", "b_gates", "mlstm_norm_w", "q_a_norm_w", "kv_a_norm_w", "ffn_norm_w", "final_norm_w")
HBM_SPEC = pl.BlockSpec(memory_space=pltpu.HBM)


def _stored(name, a):
    return a[0].T if name in TRANSPOSED else a[0]


def _unstored(name, a):
    return (a.T if name in TRANSPOSED else a)[None]


def _half_shape(name, shape):
    rs, cs = shape
    return (rs, cs // 2) if name in LANE_HALVED else (rs // 2, cs)


def _half(ref, name, h, *lead):
    rs, cs = ref.shape[-2:]
    if name in LANE_HALVED:
        return ref.at[(*lead, slice(None), pl.ds(h * (cs // 2), cs // 2))]
    return ref.at[(*lead, pl.ds(h * (rs // 2), rs // 2), slice(None))]


def _place():
    x, y, c = lax.axis_index("x"), lax.axis_index("y"), lax.axis_index("c")
    others = [(1 - x, y), (x, 1 - y), (1 - x, 1 - y)]
    return x, y, c, others


def _gather_weights(names, shards):
    n = len(shards)

    def body(*refs):
        _gather_body(names, refs[:n], refs[n:2 * n], *refs[2 * n:])

    gathered = pl.pallas_call(
        body, name="gather_weights", in_specs=[HBM_SPEC] * n, out_specs=[HBM_SPEC] * n,
        out_shape=[jax.ShapeDtypeStruct((N_CHIPS,) + s.shape, s.dtype) for s in shards],
        scratch_shapes=[pltpu.SemaphoreType.DMA((6 * n,)), pltpu.SemaphoreType.DMA((6 * n,))],
    )(*shards)
    return _with_own_slab(gathered, shards)


def _with_own_slab(gathered, shards):
    me = 2 * lax.axis_index("x") + lax.axis_index("y")
    return [lax.dynamic_update_slice(g, s[None], (me, 0, 0)) for g, s in zip(gathered, shards)]


def _gather_body(names, ins, outs, send_sems, recv_sems):
    x, y, c, others = _place()
    me = 2 * x + y
    sibling = (x, y, 1 - c)

    def copy(w, k, slab, core, to, src=None):
        dst = _half(outs[w], names[w], core, slab)
        return pltpu.make_async_remote_copy(
            src_ref=dst if src is None else src, dst_ref=dst, send_sem=send_sems.at[w * 6 + k],
            recv_sem=recv_sems.at[w * 6 + k], device_id=to, device_id_type=MESH)

    sends = []
    for w in range(len(names)):
        for j, chip in enumerate(others):
            cp = copy(w, j, me, c, (*chip, c), src=_half(ins[w], names[w], c))
            cp.start()
            sends.append(cp)
    for w in range(len(names)):
        for j, (ox, oy) in enumerate(others):
            slab = 2 * ox + oy
            copy(w, j, slab, c, (x, y, c)).wait_recv()
            fwd = copy(w, 3 + j, slab, c, sibling)
            fwd.start()
            sends.append(fwd)
    for w in range(len(names)):
        for j, (ox, oy) in enumerate(others):
            copy(w, 3 + j, 2 * ox + oy, 1 - c, (x, y, c)).wait_recv()
    for cp in sends:
        cp.wait_send()


GATHER_LATE_COLLECTIVE_ID = 1


def _gather_weights_async(names, shards):
    n = len(shards)
    src = [jax.new_ref(s, memory_space=pltpu.MemorySpace.HBM) for s in shards]
    dst = [jax.empty_ref(jax.ShapeDtypeStruct((N_CHIPS,) + s.shape, s.dtype), memory_space=pltpu.MemorySpace.HBM)
           for s in shards]

    @pl.kernel(mesh=plsc.ScalarSubcoreMesh(axis_name="sequencer", num_cores=1), name="gather_weights_async",
               scratch_types=(pltpu.SemaphoreType.DMA((6 * n,)), pltpu.SemaphoreType.DMA((6 * n,))),
               compiler_params=pltpu.CompilerParams(collective_id=GATHER_LATE_COLLECTIVE_ID))
    def launch(send_sems, recv_sems):
        x, y, c, others = _place()
        peers = [(ox, oy, c) for ox, oy in others] + [(x, y, 1 - c)]
        barrier = pltpu.get_barrier_semaphore()
        for peer in peers:
            pl.semaphore_signal(barrier, inc=1, device_id=peer, device_id_type=MESH)
        pl.semaphore_wait(barrier, len(peers))
        _gather_body(names, src, dst, send_sems, recv_sems)

    launch()
    return _with_own_slab([d[...] for d in dst], shards)


def _exchange(arrays, out_shapes, plan, copies_per_array, name):
    n = len(arrays)

    def body(*refs):
        ins, outs = refs[:n], refs[n:2 * n]
        send_sems, recv_sems = refs[2 * n:]
        rem = [pltpu.make_async_remote_copy(src_ref=s, dst_ref=d, send_sem=send_sems.at[k], recv_sem=recv_sems.at[k],
                                            device_id=to, device_id_type=MESH)
               for k, (s, d, to) in enumerate(plan(ins, outs, _place()))]
        for cp in rem:
            cp.start()
        for cp in rem:
            cp.wait_recv()
        for cp in rem:
            cp.wait_send()

    return pl.pallas_call(
        body, name=name, in_specs=[HBM_SPEC] * n, out_specs=[HBM_SPEC] * n, out_shape=out_shapes,
        scratch_shapes=[pltpu.SemaphoreType.DMA((copies_per_array * n,)),
                        pltpu.SemaphoreType.DMA((copies_per_array * n,))],
    )(*arrays)


def _plan_to_sibling(ins, outs, place):
    x, y, c, _ = place
    return [(ins[w], outs[w], (x, y, 1 - c)) for w in range(len(ins))]


def _plan_sibling_halves(names, ins, outs, place):
    x, y, c, _ = place
    return [(_half(ins[w], names[w], 1 - c, slice(None)), outs[w], (x, y, 1 - c)) for w in range(len(ins))]


def _plan_chip_partials(ins, outs, place):
    x, y, c, others = place
    return [(ins[w].at[2 * ox + oy], outs[w].at[j], (ox, oy, c))
            for w in range(len(ins)) for j, (ox, oy) in enumerate(others)]


def _chip_partial(g3, got, name, core):
    hs = tuple(got.shape[1:])
    if name in LANE_HALVED:
        g_in, g_spec = g3, pl.BlockSpec((1,) + hs, lambda s, c_ref: (s, 0, c_ref[0]))
    else:
        g_in, g_spec = g3.reshape((N_CHIPS, 2) + hs), pl.BlockSpec((1, 1) + hs, lambda s, c_ref: (s, c_ref[0], 0, 0))
    blk = pl.BlockSpec((1,) + hs, lambda s, c_ref: (s, 0, 0))

    def body(c_ref, g_ref, r_ref, o_ref):
        o_ref[0] = (g_ref[(0,) * (len(g_ref.shape) - 2)] + r_ref[0].astype(F32)).astype(o_ref.dtype)

    return pl.pallas_call(
        body, name="chip_partial_%s" % name,
        grid_spec=pltpu.PrefetchScalarGridSpec(num_scalar_prefetch=1, grid=(N_CHIPS,), in_specs=[g_spec, blk],
                                               out_specs=blk),
        out_shape=jax.ShapeDtypeStruct((N_CHIPS,) + hs, BF16), compiler_params=_cparams("parallel"),
    )(core, g_in, got)


def _shard_sum(part, got3, name, me):
    hs = tuple(part.shape[1:])

    def body(me_ref, p_ref, a_ref, b_ref, c_ref, o_ref):
        o_ref[...] = ((p_ref[0].astype(F32) + a_ref[0].astype(F32)) + b_ref[0].astype(F32)) + c_ref[0].astype(F32)

    slot = lambda j: pl.BlockSpec((1,) + hs, lambda i, me_ref: (j, 0, 0))
    return pl.pallas_call(
        body, name="shard_sum_%s" % name,
        grid_spec=pltpu.PrefetchScalarGridSpec(
            num_scalar_prefetch=1, grid=(1,),
            in_specs=[pl.BlockSpec((1,) + hs, lambda i, me_ref: (me_ref[0], 0, 0)), slot(0), slot(1), slot(2)],
            out_specs=pl.BlockSpec(hs, lambda i, me_ref: (0, 0))),
        out_shape=jax.ShapeDtypeStruct(hs, F32), compiler_params=_cparams("arbitrary"),
    )(me, part, got3, got3, got3)


def _adamw_halves(w, m, v, mine, other, name, core):
    hs = tuple(mine.shape)
    full = pl.BlockSpec(hs, (lambda h, c_ref: (0, h)) if name in LANE_HALVED else (lambda h, c_ref: (h, 0)))
    half = pl.BlockSpec(hs, lambda h, c_ref: (0, 0))

    def body(c_ref, w_ref, m_ref, v_ref, a_ref, b_ref, g_ref, d_ref, mo_ref, vo_ref):
        g = jnp.where(pl.program_id(0) == c_ref[0], a_ref[...], b_ref[...])
        delta, m_new, v_new = _adamw_math(w_ref[...], g, m_ref[...], v_ref[...])
        g_ref[...] = g
        d_ref[...] = delta
        mo_ref[...] = m_new
        vo_ref[...] = v_new

    out = jax.ShapeDtypeStruct(w.shape, F32)
    return pl.pallas_call(
        body, name="adamw_%s" % name,
        grid_spec=pltpu.PrefetchScalarGridSpec(num_scalar_prefetch=1, grid=(2,), in_specs=[full, full, full, half, half],
                                               out_specs=[full] * 4),
        out_shape=[out] * 4, compiler_params=_cparams("parallel"),
    )(core, w, m, v, mine, other)


def _adamw_math(w, g, m, v):
    m = ADAM_B1 * m + (1.0 - ADAM_B1) * g
    v = ADAM_B2 * v + (1.0 - ADAM_B2) * (g * g)
    m_hat = m / (1.0 - ADAM_B1 ** ADAM_STEP)
    v_hat = v / (1.0 - ADAM_B2 ** ADAM_STEP)
    delta = -ADAM_LR * (m_hat / (jnp.sqrt(v_hat) + ADAM_EPS) + ADAM_WD * w)
    return delta, m, v


def _core_index():
    return lax.axis_index("c").astype(jnp.int32).reshape(1)


def _chip_partials_of(names, grads, tag):
    g3 = [grads[n].reshape(N_CHIPS, grads[n].shape[0] // N_CHIPS, grads[n].shape[1]) for n in names]
    shapes = [jax.ShapeDtypeStruct((N_CHIPS,) + _half_shape(n, g.shape[1:]), BF16) for n, g in zip(names, g3)]
    got = _exchange([g.astype(BF16) for g in g3], shapes, functools.partial(_plan_sibling_halves, names), 1,
                    "sibling_halves_" + tag)
    return [_chip_partial(g, r, n, _core_index()) for n, g, r in zip(names, g3, got)]


REDUCE_LATE_COLLECTIVE_ID = 2


def _exchange_partials_async(part):
    n = len(part)
    src = [jax.new_ref(p, memory_space=pltpu.MemorySpace.HBM) for p in part]
    dst = [jax.empty_ref(jax.ShapeDtypeStruct((3,) + p.shape[1:], p.dtype), memory_space=pltpu.MemorySpace.HBM)
           for p in part]

    @pl.kernel(mesh=plsc.ScalarSubcoreMesh(axis_name="sequencer", num_cores=1), name="chip_partials_async",
               scratch_types=(pltpu.SemaphoreType.DMA((3 * n,)), pltpu.SemaphoreType.DMA((3 * n,))),
               compiler_params=pltpu.CompilerParams(collective_id=REDUCE_LATE_COLLECTIVE_ID))
    def launch(send_sems, recv_sems):
        place = _place()
        x, y, c, others = place
        barrier = pltpu.get_barrier_semaphore()
        for ox, oy in others:
            pl.semaphore_signal(barrier, inc=1, device_id=(ox, oy, c), device_id_type=MESH)
        pl.semaphore_wait(barrier, len(others))
        rem = [pltpu.make_async_remote_copy(src_ref=s, dst_ref=d, send_sem=send_sems.at[k], recv_sem=recv_sems.at[k],
                                            device_id=to, device_id_type=MESH)
               for k, (s, d, to) in enumerate(_plan_chip_partials(src, dst, place))]
        for cp in rem:
            cp.start()
        for cp in rem:
            cp.wait_recv()
        for cp in rem:
            cp.wait_send()

    launch()
    return [d[...] for d in dst]


def _finish_reduce(names, part, got3):
    me = (2 * lax.axis_index("x") + lax.axis_index("y")).astype(jnp.int32).reshape(1)
    mine = [_shard_sum(p, r3, n, me) for n, p, r3 in zip(names, part, got3)]
    other = _exchange(mine, [jax.ShapeDtypeStruct(m.shape, F32) for m in mine], _plan_to_sibling, 1, "sibling_result")
    return {n: (a, b) for n, a, b in zip(names, mine, other)}


SMALL_ROWS = 8
SMALL_LAYOUT = {"attn_norm_w": (0, 0, 1024), "ffn_norm_w": (1, 0, 1024), "final_norm_w": (2, 0, 1024),
                "q_a_norm_w": (3, 0, 384), "kv_a_norm_w": (3, 384, 256), "mlstm_norm_w": (4, 0, 512),
                "b_gates": (4, 512, 8)}
LOSS_SLOT = (5, 0)


def _pack_small(vals, loss=None):
    tile = jnp.zeros((SMALL_ROWS, D_MODEL), F32)
    for n, (r, c0, width) in SMALL_LAYOUT.items():
        tile = tile.at[r, c0:c0 + width].set(vals[n].reshape(width).astype(F32))
    if loss is not None:
        tile = tile.at[LOSS_SLOT[0], LOSS_SLOT[1]].set(loss)
    return tile


def _unpack_small(tile, shapes):
    return {n: tile[r, c0:c0 + width].reshape(shapes[n]) for n, (r, c0, width) in SMALL_LAYOUT.items()}


def _small_allreduce_adamw(g_tile, w_tile, m_tile, v_tile):
    def body(g_ref, w_ref, m_ref, v_ref, gsum_ref, d_ref, mo_ref, vo_ref, slots, send_sems, recv_sems):
        x, y, c, _ = _place()
        me = 4 * x + 2 * y + c
        slots[me] = g_ref[...]
        copies = []
        for k in range(1, 8):
            dx, dy, dc = (k >> 2) & 1, (k >> 1) & 1, k & 1
            to = (x ^ dx, y ^ dy, c ^ dc)
            cp = pltpu.make_async_remote_copy(src_ref=g_ref, dst_ref=slots.at[me], send_sem=send_sems.at[k - 1],
                                              recv_sem=recv_sems.at[k - 1], device_id=to, device_id_type=MESH)
            cp.start()
            copies.append(cp)
        for k in range(1, 8):
            src = me ^ k
            pltpu.make_async_remote_copy(src_ref=g_ref, dst_ref=slots.at[src], send_sem=send_sems.at[k - 1],
                                         recv_sem=recv_sems.at[k - 1], device_id=(x, y, c),
                                         device_id_type=MESH).wait_recv()
        for cp in copies:
            cp.wait_send()
        total = slots[0]
        for d in range(1, 8):
            total = total + slots[d]
        gsum_ref[...] = total
        delta, m_new, v_new = _adamw_math(w_ref[...], total, m_ref[...], v_ref[...])
        d_ref[...] = delta
        mo_ref[...] = m_new
        vo_ref[...] = v_new

    vm = pl.BlockSpec(memory_space=pltpu.VMEM)
    tile = jax.ShapeDtypeStruct((SMALL_ROWS, D_MODEL), F32)
    return pl.pallas_call(
        body, name="small_allreduce_adamw", in_specs=[vm] * 4, out_specs=[vm] * 4, out_shape=[tile] * 4,
        scratch_shapes=[pltpu.VMEM((8, SMALL_ROWS, D_MODEL), F32), pltpu.SemaphoreType.DMA((7,)),
                        pltpu.SemaphoreType.DMA((7,))],
    )(g_tile, w_tile, m_tile, v_tile)


def kernel(x, positions, attn_norm_w, w_in, b_gates, mlstm_norm_w, q_a_norm_w, w_q_b, kv_a_norm_w, w_kv_b, w_out, ffn_norm_w, w_gate, w_up, w_down, final_norm_w, loss_target, m_attn_norm_w, m_w_in, m_b_gates, m_mlstm_norm_w, m_q_a_norm_w, m_w_q_b, m_kv_a_norm_w, m_w_kv_b, m_w_out, m_ffn_norm_w, m_w_gate, m_w_up, m_w_down, m_final_norm_w, v_attn_norm_w, v_w_in, v_b_gates, v_mlstm_norm_w, v_q_a_norm_w, v_w_q_b, v_kv_a_norm_w, v_w_kv_b, v_w_out, v_ffn_norm_w, v_w_gate, v_w_up, v_w_down, v_final_norm_w):
    names = ("attn_norm_w", "w_in", "b_gates", "mlstm_norm_w", "q_a_norm_w", "w_q_b", "kv_a_norm_w", "w_kv_b", "w_out",
             "ffn_norm_w", "w_gate", "w_up", "w_down", "final_norm_w")
    wts = dict(zip(names, (attn_norm_w, w_in, b_gates, mlstm_norm_w, q_a_norm_w, w_q_b, kv_a_norm_w, w_kv_b, w_out,
                           ffn_norm_w, w_gate, w_up, w_down, final_norm_w)))
    mom = dict(zip(names, (m_attn_norm_w, m_w_in, m_b_gates, m_mlstm_norm_w, m_q_a_norm_w, m_w_q_b, m_kv_a_norm_w,
                           m_w_kv_b, m_w_out, m_ffn_norm_w, m_w_gate, m_w_up, m_w_down, m_final_norm_w)))
    vel = dict(zip(names, (v_attn_norm_w, v_w_in, v_b_gates, v_mlstm_norm_w, v_q_a_norm_w, v_w_q_b, v_kv_a_norm_w,
                           v_w_kv_b, v_w_out, v_ffn_norm_w, v_w_gate, v_w_up, v_w_down, v_final_norm_w)))
    t = x.shape[1]

    shards = {n: _stored(n, wts[n]).astype(BF16) for n in BIG}
    early = _gather_weights(EARLY, [shards[n] for n in EARLY])
    early, late_in = lax.optimization_barrier((early, [shards[n] for n in LATE]))
    late = _gather_weights_async(LATE, late_in)
    full = {n: g.reshape(N_CHIPS * g.shape[1], g.shape[2]) for n, g in zip(EARLY + LATE, early + late)}

    nw = {n: wts[n] for n in SMALL}
    loss, grad_x, grads, (late_part, late_got3) = _local_step(x[0], positions.reshape(t, 1), loss_target[0], nw, full)

    early_part = _chip_partials_of(EARLY, grads, "early")
    early_got3 = _exchange(early_part, [jax.ShapeDtypeStruct((3,) + p.shape[1:], BF16) for p in early_part],
                           _plan_chip_partials, 3, "chip_partials_early")
    halves = _finish_reduce(BIG, list(early_part) + list(late_part), list(early_got3) + list(late_got3))
    core = _core_index()
    outs_g, outs_d, outs_m, outs_v = {}, {}, {}, {}
    for n in BIG:
        res = _adamw_halves(_stored(n, wts[n]), _stored(n, mom[n]), _stored(n, vel[n]), *halves[n], n, core)
        outs_g[n], outs_d[n], outs_m[n], outs_v[n] = [_unstored(n, r) for r in res]
    shapes = {n: wts[n].shape for n in SMALL}
    g_tile = _pack_small({n: grads[n] for n in SMALL}, loss=loss[0, 0])
    gsum, d_tile, m_tile, v_tile = _small_allreduce_adamw(g_tile, _pack_small(wts), _pack_small(mom), _pack_small(vel))
    outs_g.update(_unpack_small(gsum, shapes))
    outs_d.update(_unpack_small(d_tile, shapes))
    outs_m.update(_unpack_small(m_tile, shapes))
    outs_v.update(_unpack_small(v_tile, shapes))
    total_loss = gsum[LOSS_SLOT[0], LOSS_SLOT[1]]
    return (total_loss, grad_x[None], *[outs_g[n] for n in names], *[outs_d[n] for n in names],
            *[outs_m[n] for n in names], *[outs_v[n] for n in names])
```

```python
import functools
import math

import jax
import jax.numpy as jnp
from jax import lax
from jax.experimental import pallas as pl
from jax.experimental.pallas import tpu as pltpu
from jax.experimental.pallas import tpu_sc as plsc

F32 = jnp.float32
BF16 = jnp.bfloat16

D_MODEL = 1024
M_HEADS = 4
M_HEAD_DIM = 128
M_WIDTH = M_HEADS * M_HEAD_DIM
M_CHUNK = 64
A_HEADS = 8
A_NOPE = 64
A_ROPE = 32
A_V = 64
A_WIDTH = A_HEADS * A_V
A_QK_PAD = 128
Q_RANK = 384
KV_RANK = 256
ROPE_THETA = 10000.0
D_FF = 2816
D_IN = 2728
EPS = 1e-6
ATTN_SCALE = (A_NOPE + A_ROPE) ** -0.5
M_SCALE = M_HEAD_DIM ** -0.5

ADAM_LR = 0.001
ADAM_B1 = 0.9
ADAM_B2 = 0.999
ADAM_EPS = 1e-08
ADAM_WD = 0.01
ADAM_STEP = 10

VMEM_LIMIT_BYTES = 56 * 1024 * 1024
LANES = 128
NEG_INF = float("-inf")


def _cparams(*sem):
    return pltpu.CompilerParams(dimension_semantics=sem if sem else None, vmem_limit_bytes=VMEM_LIMIT_BYTES)


_DIMS = {"nn": (((1,), (0,)), ((), ())), "nt": (((1,), (1,)), ((), ())), "tn": (((0,), (0,)), ((), ()))}


def _matmul(a, b, mode, out_dtype, name, tm=None, tn=1024, residual=None, b_rows=None, lead_scale=None,
            also_bf16=False):
    a_list = list(a) if isinstance(a, (list, tuple)) else [a]
    b_list = list(b) if isinstance(b, (list, tuple)) else [b]
    rows_list = list(b_rows) if isinstance(b_rows, (list, tuple)) else [b_rows] * len(b_list)
    assert len(a_list) == len(b_list) == len(rows_list)
    dims = _DIMS[mode]
    specs, m, n = [], None, None
    for aa, bb, rr in zip(a_list, b_list, rows_list):
        b_shape = bb.shape if rr is None else (rr, bb.shape[1])
        if mode == "nn":
            (m1, k), (k2, n1) = aa.shape, b_shape
        elif mode == "nt":
            (m1, k), (n1, k2) = aa.shape, b_shape
        else:
            (k, m1), (k2, n1) = aa.shape, b_shape
        assert k == k2 and (m is None or (m, n) == (m1, n1)), (aa.shape, bb.shape, mode)
        m, n = m1, n1
        specs.append(k)
    if tm is None:
        tm = 512 if mode == "tn" else 1024
    tm, tn = min(tm, m), min(tn, n)
    assert m % tm == 0 and n % tn == 0, (m, n, tm, tn)
    in_specs = []
    for k in specs:
        in_specs.append(pl.BlockSpec((k, tm), lambda i, j: (0, i)) if mode == "tn"
                        else pl.BlockSpec((tm, k), lambda i, j: (i, 0)))
        in_specs.append(pl.BlockSpec((tn, k), lambda i, j: (j, 0)) if mode == "nt"
                        else pl.BlockSpec((k, tn), lambda i, j: (0, j)))
    o_spec = pl.BlockSpec((tm, tn), lambda i, j: (i, j))
    n_pairs = len(specs)

    def body(*refs):
        acc = None
        for p in range(n_pairs):
            part = lax.dot_general(refs[2 * p][...].astype(BF16), refs[2 * p + 1][...].astype(BF16), dims,
                                   preferred_element_type=F32)
            acc = part if acc is None else acc + part
        if lead_scale is not None:
            assert lead_scale[0] % tn == 0
            acc = acc * jnp.where(pl.program_id(1) < lead_scale[0] // tn, lead_scale[1], 1.0)
        if residual is not None:
            acc = acc + refs[2 * n_pairs][...].astype(F32)
        outs = refs[2 * n_pairs + (residual is not None):]
        outs[0][...] = acc.astype(outs[0].dtype)
        if also_bf16:
            outs[1][...] = acc.astype(BF16)

    ins = [x for pair in zip(a_list, b_list) for x in pair] + ([residual] if residual is not None else [])
    in_specs = in_specs + ([o_spec] if residual is not None else [])
    out_shape = jax.ShapeDtypeStruct((m, n), out_dtype)
    if also_bf16:
        return pl.pallas_call(
            body, name=name, grid=(m // tm, n // tn), in_specs=in_specs, out_specs=[o_spec, o_spec],
            out_shape=[out_shape, jax.ShapeDtypeStruct((m, n), BF16)],
            compiler_params=_cparams("parallel", "parallel"),
        )(*ins)
    return pl.pallas_call(
        body, name=name, grid=(m // tm, n // tn), in_specs=in_specs, out_specs=o_spec, out_shape=out_shape,
        compiler_params=_cparams("parallel", "parallel"),
    )(*ins)


def _rmsnorm_fwd(x, w, name, tm=512):
    t, d = x.shape

    def body(x_ref, w_ref, o_ref):
        xf = x_ref[...].astype(F32)
        r = lax.rsqrt(jnp.mean(xf * xf, axis=-1, keepdims=True) + EPS)
        o_ref[...] = (xf * r * w_ref[...]).astype(o_ref.dtype)

    return pl.pallas_call(
        body, name=name, grid=(t // tm,),
        in_specs=[pl.BlockSpec((tm, d), lambda i: (i, 0)), pl.BlockSpec((1, d), lambda i: (0, 0))],
        out_specs=pl.BlockSpec((tm, d), lambda i: (i, 0)),
        out_shape=jax.ShapeDtypeStruct((t, d), BF16), compiler_params=_cparams("parallel"),
    )(x, w)


def _rmsnorm_bwd(dy, x, w, name, out_dtype, residual=None, also_bf16=False, tm=512):
    t, d = x.shape

    def body(dy_ref, x_ref, w_ref, *rest):
        dx_ref, dw_ref = rest[-3 if also_bf16 else -2], rest[-1]
        xf = x_ref[...].astype(F32)
        r = lax.rsqrt(jnp.mean(xf * xf, axis=-1, keepdims=True) + EPS)
        xh = xf * r
        dyf = dy_ref[...].astype(F32)
        dyh = dyf * w_ref[...]
        dx = r * (dyh - xh * jnp.mean(dyh * xh, axis=-1, keepdims=True))
        if residual is not None:
            dx = dx + rest[0][...].astype(F32)
        dx_ref[...] = dx.astype(dx_ref.dtype)
        if also_bf16:
            rest[-2][...] = dx.astype(BF16)
        part = jnp.sum(dyf * xh, axis=0, keepdims=True)

        @pl.when(pl.program_id(0) == 0)
        def _():
            dw_ref[...] = part

        @pl.when(pl.program_id(0) > 0)
        def _():
            dw_ref[...] += part

    row = pl.BlockSpec((tm, d), lambda i: (i, 0))
    vec = pl.BlockSpec((1, d), lambda i: (0, 0))
    ins = [dy, x, w] + ([residual] if residual is not None else [])
    extra = [jax.ShapeDtypeStruct((t, d), BF16)] if also_bf16 else []
    return pl.pallas_call(
        body, name=name, grid=(t // tm,), in_specs=[row, row, vec] + ([row] if residual is not None else []),
        out_specs=[row] + [row] * len(extra) + [vec],
        out_shape=[jax.ShapeDtypeStruct((t, d), out_dtype)] + extra + [jax.ShapeDtypeStruct((1, d), F32)],
        compiler_params=_cparams("arbitrary"),
    )(*ins)


def _final_loss(h, target, w, name, tm=512):
    t, d = h.shape

    def body(h_ref, t_ref, w_ref, dh_ref, dhb_ref, loss_ref, dw_ref):
        xf = h_ref[...]
        r = lax.rsqrt(jnp.mean(xf * xf, axis=-1, keepdims=True) + EPS)
        xh = xf * r
        err = xh * w_ref[...] - t_ref[...]
        part_loss = 0.5 * jnp.sum(jnp.sum(err * err, axis=-1, keepdims=True), axis=0, keepdims=True) * (1.0 / d)
        dy = err * (1.0 / d)
        dyh = dy * w_ref[...]
        dh = r * (dyh - xh * jnp.mean(dyh * xh, axis=-1, keepdims=True))
        dh_ref[...] = dh
        dhb_ref[...] = dh.astype(BF16)
        part_dw = jnp.sum(dy * xh, axis=0, keepdims=True)
        part_loss = jnp.broadcast_to(part_loss, (1, LANES))

        @pl.when(pl.program_id(0) == 0)
        def _():
            dw_ref[...] = part_dw
            loss_ref[...] = part_loss

        @pl.when(pl.program_id(0) > 0)
        def _():
            dw_ref[...] += part_dw
            loss_ref[...] += part_loss

    row = pl.BlockSpec((tm, d), lambda i: (i, 0))
    vec = pl.BlockSpec((1, d), lambda i: (0, 0))
    return pl.pallas_call(
        body, name=name, grid=(t // tm,), in_specs=[row, row, vec],
        out_specs=[row, row, pl.BlockSpec((1, LANES), lambda i: (0, 0)), vec],
        out_shape=[jax.ShapeDtypeStruct((t, d), F32), jax.ShapeDtypeStruct((t, d), BF16),
                   jax.ShapeDtypeStruct((1, LANES), F32), jax.ShapeDtypeStruct((1, d), F32)],
        compiler_params=_cparams("arbitrary"),
    )(h, target, w)


FFN_TN = 1408


def _ffn_gate_up(u, w_gate, w_up, name, tm=1024):
    t, d = u.shape
    n = w_gate.shape[0]
    tm = min(tm, t)

    def body(u_ref, wg_ref, wu_ref, g_ref, up_ref, act_ref):
        uu = u_ref[...]
        g = lax.dot_general(uu, wg_ref[...], _DIMS["nt"], preferred_element_type=F32)
        up = lax.dot_general(uu, wu_ref[...], _DIMS["nt"], preferred_element_type=F32)
        g_b, up_b = g.astype(BF16), up.astype(BF16)
        g_ref[...] = g_b
        up_ref[...] = up_b
        gf, uf = g_b.astype(F32), up_b.astype(F32)
        act_ref[...] = (gf * jax.nn.sigmoid(gf) * uf).astype(act_ref.dtype)

    w_spec = pl.BlockSpec((FFN_TN, d), lambda i, j: (j, 0))
    o_spec = pl.BlockSpec((tm, FFN_TN), lambda i, j: (i, j))
    out = jax.ShapeDtypeStruct((t, n), BF16)
    return pl.pallas_call(
        body, name=name, grid=(t // tm, n // FFN_TN),
        in_specs=[pl.BlockSpec((tm, d), lambda i, j: (i, 0)), w_spec, w_spec], out_specs=[o_spec] * 3,
        out_shape=[out] * 3, compiler_params=_cparams("parallel", "parallel"),
    )(u, w_gate, w_up)


def _ffn_down_bwd(dh, w_down, g, up, name, tm=1024):
    t, d = dh.shape
    n = w_down.shape[0]
    tm = min(tm, t)

    def body(dh_ref, w_ref, g_ref, up_ref, dg_ref, du_ref):
        df = lax.dot_general(dh_ref[...], w_ref[...], _DIMS["nt"], preferred_element_type=F32).astype(BF16).astype(F32)
        gf = g_ref[...].astype(F32)
        uf = up_ref[...].astype(F32)
        s = jax.nn.sigmoid(gf)
        dg_ref[...] = (df * uf * s * (1.0 + gf * (1.0 - s))).astype(dg_ref.dtype)
        du_ref[...] = (df * gf * s).astype(du_ref.dtype)

    o_spec = pl.BlockSpec((tm, FFN_TN), lambda i, j: (i, j))
    out = jax.ShapeDtypeStruct((t, n), BF16)
    return pl.pallas_call(
        body, name=name, grid=(t // tm, n // FFN_TN),
        in_specs=[pl.BlockSpec((tm, d), lambda i, j: (i, 0)), pl.BlockSpec((FFN_TN, d), lambda i, j: (j, 0)),
                  o_spec, o_spec],
        out_specs=[o_spec, o_spec], out_shape=[out, out], compiler_params=_cparams("parallel", "parallel"),
    )(dh, w_down, g, up)


M_BLOCK = 512
M_CHUNKS_PER_BLOCK = M_BLOCK // M_CHUNK
M_UNROLL = 2


def _log_sigmoid(z):
    return jnp.minimum(z, 0.0) - jnp.log(1.0 + jnp.exp(-jnp.abs(z)))


def _per_head(fn):
    return jnp.stack([fn(hd) for hd in range(M_HEADS)])


def _mlstm_chunk_fwd(q, k, v, gc, c_prev, n_prev, m_prev):
    L, H = M_CHUNK, M_HEADS
    tt = lax.broadcasted_iota(jnp.int32, (L, L), 0)
    ss = lax.broadcasted_iota(jnp.int32, (L, L), 1)
    eye = tt == ss
    causal = ss <= tt
    gct = jnp.transpose(gc)
    i_col = _per_head(lambda hd: gc[:, hd:hd + 1])
    f_col = _per_head(lambda hd: gc[:, H + hd:H + hd + 1])
    i_row = _per_head(lambda hd: gct[hd:hd + 1, :])
    lf_col = _log_sigmoid(f_col)
    lf_row = _log_sigmoid(_per_head(lambda hd: gct[H + hd:H + hd + 1, :]))
    b_col = jnp.sum(jnp.where(causal, lf_row, 0.0), axis=2, keepdims=True)
    b_row = jnp.sum(jnp.where(tt <= ss, lf_col, 0.0), axis=1, keepdims=True)
    g = jnp.sum(lf_col, axis=1, keepdims=True)
    a_row = g - b_row + i_row
    a_col = g - b_col + i_col
    m_loc = jnp.max(a_row, axis=2, keepdims=True)
    d_log = jnp.where(causal, b_col - b_row + i_row, NEG_INF)
    inter = b_col + m_prev
    m_t = jnp.maximum(jnp.max(d_log, axis=2, keepdims=True), inter)
    dmat = jnp.exp(d_log - m_t)
    amat = _per_head(lambda hd: lax.dot_general(q[hd], k[hd], _DIMS["nt"], preferred_element_type=F32))
    p = dmat * amat
    sig = jnp.exp(inter - m_t)
    c_prev_b = [c.astype(BF16) for c in c_prev]
    qc = _per_head(lambda hd: jnp.dot(q[hd], c_prev_b[hd], preferred_element_type=F32))
    p_b = p.astype(BF16)
    num = _per_head(lambda hd: jnp.dot(p_b[hd], v[hd], preferred_element_type=F32)) + sig * qc
    qf = _per_head(lambda hd: q[hd].astype(F32))
    qn = jnp.sum(qf * n_prev, axis=2, keepdims=True)
    den_raw = jnp.sum(p, axis=2, keepdims=True) + sig * qn
    floor = jnp.exp(-m_t)
    den = jnp.maximum(jnp.abs(den_raw), floor)
    h = num / den
    m_new = jnp.maximum(g + m_prev, m_loc)
    w_col = jnp.exp(a_col - m_new)
    w_row = jnp.exp(a_row - m_new)
    alpha = jnp.exp(g + m_prev - m_new)
    return dict(eye=eye, causal=causal, tt=tt, ss=ss, i_col=i_col, f_col=f_col, dmat=dmat, amat=amat, p=p, p_b=p_b,
                sig=sig, qc=qc, qf=qf, qn=qn, num=num, den_raw=den_raw, floor=floor, den=den, h=h, m_new=m_new,
                w_col=w_col, w_row=w_row, alpha=alpha, c_prev_b=c_prev_b, b_row=b_row, b_col=b_col, m_t=m_t)


def _chunk_major_t(xt):
    f, t = xt.shape
    return xt.reshape(f, t // M_CHUNK, M_CHUNK).transpose(1, 0, 2)


def _mlstm_fwd(m4, kt, gates, bias, norm_w, name):
    t = m4.shape[0]
    nblk = t // M_BLOCK
    nc = t // M_CHUNK
    L, dh = M_CHUNK, M_HEAD_DIM

    def body(m4_ref, kt_ref, g_ref, b_ref, w_ref, hm_ref, cp_ref, nm_ref, c_s, n_s, m_s):
        @pl.when(pl.program_id(0) == 0)
        def _():
            c_s[...] = jnp.zeros_like(c_s)
            n_s[...] = jnp.zeros_like(n_s)
            m_s[...] = jnp.zeros_like(m_s)

        row8 = lax.broadcasted_iota(jnp.int32, (8, dh), 0)

        def chunk(c, carry):
            rows = pl.ds(pl.multiple_of(c * L, L), L)
            gc = g_ref[rows, :] + b_ref[...]
            col = lambda part, hd: m4_ref[rows, part * M_WIDTH + hd * dh:part * M_WIDTH + (hd + 1) * dh]
            q = [col(0, hd) for hd in range(M_HEADS)]
            k = [col(1, hd) for hd in range(M_HEADS)]
            v = [col(2, hd) for hd in range(M_HEADS)]
            o = _per_head(lambda hd: col(3, hd).astype(F32))
            c_prev = [c_s[hd] for hd in range(M_HEADS)]
            n_prev = n_s[:, 0:1, :]
            m_prev = m_s[:, 0:1, 0:1]
            r = _mlstm_chunk_fwd(q, k, v, gc, c_prev, n_prev, m_prev)
            h = r["h"]
            hn = h * lax.rsqrt(jnp.mean(h * h, axis=-1, keepdims=True) + EPS) * w_ref[...]
            hm = (hn * jax.nn.sigmoid(o)).astype(hm_ref.dtype)
            nm = jnp.where(row8 == 0, n_prev, jnp.where(row8 == 1, m_prev, 0.0))
            kf = _per_head(lambda hd: k[hd].astype(F32))
            n_s[:, 0:1, :] = r["alpha"] * n_prev + jnp.sum(kf * r["w_col"], axis=1, keepdims=True)
            m_s[...] = jnp.broadcast_to(r["m_new"], (M_HEADS, 8, dh))
            for hd in range(M_HEADS):
                hm_ref[rows, hd * dh:(hd + 1) * dh] = hm[hd]
                cp_ref[hd, c] = r["c_prev_b"][hd]
                nm_ref[hd, c] = nm[hd]
                kw_t = (kt_ref[c, hd * dh:(hd + 1) * dh, :].astype(F32) * r["w_row"][hd]).astype(BF16)
                c_s[hd] = r["alpha"][hd] * c_prev[hd] + jnp.dot(kw_t, v[hd], preferred_element_type=F32)
            return carry

        lax.fori_loop(0, M_CHUNKS_PER_BLOCK, chunk, 0)

    return pl.pallas_call(
        body, name=name, grid=(nblk,),
        in_specs=[pl.BlockSpec((M_BLOCK, 4 * M_WIDTH), lambda i: (i, 0)),
                  pl.BlockSpec((M_CHUNKS_PER_BLOCK, M_WIDTH, L), lambda i: (i, 0, 0)),
                  pl.BlockSpec((M_BLOCK, LANES), lambda i: (i, 0)),
                  pl.BlockSpec((1, LANES), lambda i: (0, 0)),
                  pl.BlockSpec((M_HEADS, 1, dh), lambda i: (0, 0, 0))],
        out_specs=[pl.BlockSpec((M_BLOCK, M_WIDTH), lambda i: (i, 0)),
                   pl.BlockSpec((M_HEADS, M_CHUNKS_PER_BLOCK, dh, dh), lambda i: (0, i, 0, 0)),
                   pl.BlockSpec((M_HEADS, M_CHUNKS_PER_BLOCK, 8, dh), lambda i: (0, i, 0, 0))],
        out_shape=[jax.ShapeDtypeStruct((t, M_WIDTH), BF16),
                   jax.ShapeDtypeStruct((M_HEADS, nc, dh, dh), BF16),
                   jax.ShapeDtypeStruct((M_HEADS, nc, 8, dh), F32)],
        scratch_shapes=[pltpu.VMEM((M_HEADS, dh, dh), F32), pltpu.VMEM((M_HEADS, 8, dh), F32),
                        pltpu.VMEM((M_HEADS, 8, dh), F32)],
        compiler_params=_cparams("arbitrary"),
    )(m4, kt, gates, bias, norm_w)


def _mlstm_bwd(m4, qt, gates, bias, norm_w, c_prev_all, nm_all, dhm, name):
    t = m4.shape[0]
    nblk = t // M_BLOCK
    L, dh = M_CHUNK, M_HEAD_DIM

    def body(m4_ref, qt_ref, g_ref, b_ref, w_ref, cp_ref, nm_ref, dhm_ref, dm4_ref, dg_ref, small_ref, dc_s, dn_s):
        @pl.when(pl.program_id(0) == 0)
        def _():
            dc_s[...] = jnp.zeros_like(dc_s)
            dn_s[...] = jnp.zeros_like(dn_s)
            small_ref[...] = jnp.zeros_like(small_ref)

        lane = lax.broadcasted_iota(jnp.int32, (L, LANES), 1)
        row8 = lax.broadcasted_iota(jnp.int32, (8, LANES), 0)

        def chunk(ci, carry):
            c = M_CHUNKS_PER_BLOCK - 1 - ci
            rows = pl.ds(pl.multiple_of(c * L, L), L)
            gc = g_ref[rows, :] + b_ref[...]
            heads = range(M_HEADS)
            col = lambda part, hd: m4_ref[rows, part * M_WIDTH + hd * dh:part * M_WIDTH + (hd + 1) * dh]
            q = [col(0, hd) for hd in heads]
            k = [col(1, hd) for hd in heads]
            v = [col(2, hd) for hd in heads]
            o = _per_head(lambda hd: col(3, hd).astype(F32))
            q_t = [qt_ref[c, hd * dh:(hd + 1) * dh, :] for hd in heads]
            c_prev = [cp_ref[hd, c].astype(F32) for hd in heads]
            nm = _per_head(lambda hd: nm_ref[hd, c])
            n_prev = nm[:, 0:1, :]
            m_prev = nm[:, 1:2, 0:1]
            r = _mlstm_chunk_fwd(q, k, v, gc, c_prev, n_prev, m_prev)
            eye, tt, ss = r["eye"], r["tt"], r["ss"]
            h, den, sig, p = r["h"], r["den"], r["sig"], r["p"]
            w_col, alpha, c_prev_b = r["w_col"], r["alpha"], r["c_prev_b"]
            to_row = lambda colv: jnp.sum(jnp.where(eye, colv, 0.0), axis=1, keepdims=True)
            to_col = lambda rowv: jnp.sum(jnp.where(eye, rowv, 0.0), axis=2, keepdims=True)
            mm = lambda fn: _per_head(lambda hd: fn(hd))

            w_h = w_ref[...]
            rn = lax.rsqrt(jnp.mean(h * h, axis=-1, keepdims=True) + EPS)
            hh = h * rn
            og = jax.nn.sigmoid(o)
            dhm_c = _per_head(lambda hd: dhm_ref[rows, hd * dh:(hd + 1) * dh].astype(F32))
            dhn = dhm_c * og
            d_o = dhm_c * hh * w_h * og * (1.0 - og)
            d_norm = jnp.sum(dhn * hh, axis=1, keepdims=True)
            dhh = dhn * w_h
            dh_ = rn * (dhh - hh * jnp.mean(dhh * hh, axis=-1, keepdims=True))

            dnum = dh_ / den
            dden = -jnp.sum(dh_ * h, axis=-1, keepdims=True) / den
            dden_raw = jnp.where(jnp.abs(r["den_raw"]) >= r["floor"], dden * jnp.sign(r["den_raw"]), 0.0)
            dnum_b = dnum.astype(BF16)
            dp = mm(lambda hd: lax.dot_general(dnum_b[hd], v[hd], _DIMS["nt"], preferred_element_type=F32)) + dden_raw
            dc = [dc_s[hd] for hd in heads]
            dn = dn_s[:, 0:1, :]
            dc_b = [d.astype(BF16) for d in dc]
            g2_b = (sig * dnum).astype(BF16)
            sd = sig * dden_raw
            da_mat = (dp * r["dmat"]).astype(BF16)
            dqs = mm(lambda hd: lax.dot_general(g2_b[hd], c_prev_b[hd], _DIMS["nt"], preferred_element_type=F32)
                     + jnp.dot(da_mat[hd], k[hd], preferred_element_type=F32)) + sd * n_prev
            r_mat = mm(lambda hd: lax.dot_general(v[hd], dc_b[hd], _DIMS["nt"], preferred_element_type=F32)) + dn
            kf = _per_head(lambda hd: k[hd].astype(F32))
            dmat_t = jnp.exp(jnp.where(tt <= ss, r["b_row"] - r["b_col"] + r["i_col"], NEG_INF) - to_row(r["m_t"]))
            p_t = (dmat_t * mm(lambda hd: jnp.dot(k[hd], q_t[hd], preferred_element_type=F32))).astype(BF16)
            dp_t = mm(lambda hd: lax.dot_general(v[hd], dnum_b[hd], _DIMS["nt"], preferred_element_type=F32))
            da_t = ((dp_t + to_row(dden_raw)) * dmat_t).astype(BF16)
            d_k = mm(lambda hd: jnp.dot(da_t[hd], q[hd], preferred_element_type=F32)) + w_col * r_mat
            d_v = (mm(lambda hd: jnp.dot(p_t[hd], dnum_b[hd], preferred_element_type=F32))
                   + w_col * mm(lambda hd: jnp.dot(k[hd], dc_b[hd], preferred_element_type=F32)))
            da_col = jnp.sum(kf * r_mat, axis=-1, keepdims=True) * w_col
            dsig = jnp.sum(dnum * r["qc"], axis=-1, keepdims=True) + dden_raw * r["qn"]
            pp = dp * p
            r1 = jnp.sum(pp, axis=2, keepdims=True)
            c1_col = to_col(jnp.sum(pp, axis=1, keepdims=True))
            dcc = _per_head(lambda hd: dc[hd] * c_prev[hd])
            d_alpha = (jnp.sum(jnp.sum(dcc, axis=2, keepdims=True), axis=1, keepdims=True)
                       + jnp.sum(dn * n_prev, axis=2, keepdims=True))
            dgl = jnp.sum(da_col, axis=1, keepdims=True) + d_alpha * alpha
            db_col = r1 + dsig * sig - c1_col - da_col
            dli_col = c1_col + da_col
            dlf_col = jnp.sum(jnp.where(ss >= tt, to_row(db_col), 0.0), axis=2, keepdims=True) + dgl
            df_col = dlf_col * jax.nn.sigmoid(-r["f_col"])

            dn_s[:, 0:1, :] = alpha * dn + jnp.sum(sd * r["qf"], axis=1, keepdims=True)
            dq_out = (dqs * M_SCALE).astype(dm4_ref.dtype)
            dk_out, dv_out, do_out = d_k.astype(dm4_ref.dtype), d_v.astype(dm4_ref.dtype), d_o.astype(dm4_ref.dtype)
            dg_tile = jnp.zeros((L, LANES), F32)
            small = jnp.zeros((8, LANES), F32)
            for hd in heads:
                dc_s[hd] = alpha[hd] * dc[hd] + jnp.dot(q_t[hd], g2_b[hd], preferred_element_type=F32)
                dm4_ref[rows, hd * dh:(hd + 1) * dh] = dq_out[hd]
                dm4_ref[rows, M_WIDTH + hd * dh:M_WIDTH + (hd + 1) * dh] = dk_out[hd]
                dm4_ref[rows, 2 * M_WIDTH + hd * dh:2 * M_WIDTH + (hd + 1) * dh] = dv_out[hd]
                dm4_ref[rows, 3 * M_WIDTH + hd * dh:3 * M_WIDTH + (hd + 1) * dh] = do_out[hd]
                dg_tile = (dg_tile + jnp.where(lane == hd, dli_col[hd], 0.0)
                           + jnp.where(lane == M_HEADS + hd, df_col[hd], 0.0))
                small = small + jnp.where(row8 == hd, d_norm[hd], 0.0)
            dg_ref[rows, :] = dg_tile
            small = small + jnp.where(row8 == M_HEADS, jnp.sum(dg_tile, axis=0, keepdims=True), 0.0)
            small_ref[...] += small
            return carry

        lax.fori_loop(0, M_CHUNKS_PER_BLOCK, chunk, 0, unroll=M_UNROLL)

    rev = lambda i: nblk - 1 - i
    return pl.pallas_call(
        body, name=name, grid=(nblk,),
        in_specs=[pl.BlockSpec((M_BLOCK, 4 * M_WIDTH), lambda i: (rev(i), 0)),
                  pl.BlockSpec((M_CHUNKS_PER_BLOCK, M_WIDTH, L), lambda i: (rev(i), 0, 0)),
                  pl.BlockSpec((M_BLOCK, LANES), lambda i: (rev(i), 0)),
                  pl.BlockSpec((1, LANES), lambda i: (0, 0)),
                  pl.BlockSpec((M_HEADS, 1, dh), lambda i: (0, 0, 0)),
                  pl.BlockSpec((M_HEADS, M_CHUNKS_PER_BLOCK, dh, dh), lambda i: (0, rev(i), 0, 0)),
                  pl.BlockSpec((M_HEADS, M_CHUNKS_PER_BLOCK, 8, dh), lambda i: (0, rev(i), 0, 0)),
                  pl.BlockSpec((M_BLOCK, M_WIDTH), lambda i: (rev(i), 0))],
        out_specs=[pl.BlockSpec((M_BLOCK, 4 * M_WIDTH), lambda i: (rev(i), 0)),
                   pl.BlockSpec((M_BLOCK, LANES), lambda i: (rev(i), 0)),
                   pl.BlockSpec((8, LANES), lambda i: (0, 0))],
        out_shape=[jax.ShapeDtypeStruct((t, 4 * M_WIDTH), BF16), jax.ShapeDtypeStruct((t, LANES), F32),
                   jax.ShapeDtypeStruct((8, LANES), F32)],
        scratch_shapes=[pltpu.VMEM((M_HEADS, dh, dh), F32), pltpu.VMEM((M_HEADS, 8, dh), F32)],
        compiler_params=_cparams("arbitrary"),
    )(m4, qt, gates, bias, norm_w, c_prev_all, nm_all, dhm)


def _rope_tables(pos_col, width):
    lane = lax.broadcasted_iota(jnp.int32, (1, width), 1) % A_QK_PAD
    half = A_ROPE // 2
    first = (lane >= A_NOPE) & (lane < A_NOPE + half)
    second = (lane >= A_NOPE + half) & (lane < A_NOPE + A_ROPE)
    idx = jnp.where(first, lane - A_NOPE, lane - A_NOPE - half).astype(F32)
    inv_freq = jnp.exp(idx * (-math.log(ROPE_THETA) / half))
    ang = pos_col.astype(F32) * inv_freq
    cos, sin = jnp.cos(ang), jnp.sin(ang)
    rot = first | second
    return jnp.where(rot, cos, 1.0), jnp.where(first, -sin, 0.0), jnp.where(second, sin, 0.0)


def _rope_apply(vv, cosf, s1, s2):
    half = A_ROPE // 2
    w = vv.shape[-1]
    return vv * cosf + pltpu.roll(vv, w - half, 1) * s1 + pltpu.roll(vv, half, 1) * s2


def _rope_apply_t(dd, cosf, s1, s2):
    half = A_ROPE // 2
    w = dd.shape[-1]
    return dd * cosf + pltpu.roll(dd * s1, half, 1) + pltpu.roll(dd * s2, w - half, 1)


A_BIAS_LANE_K = A_NOPE + A_ROPE
A_BIAS_LANE_V = A_V


def _hi_lo(val):
    hi = val.astype(BF16)
    return hi, (val - hi.astype(F32)).astype(BF16)


def _rope_fwd(qpre, kpre, vpre, gates, pos, name, tm=512):
    t, w = qpre.shape

    def body(q_ref, k_ref, v_ref, g_ref, p_ref, qo_ref, ko_ref, vo_ref):
        cosf, s1, s2 = _rope_tables(p_ref[...], LANES)
        lane = lax.broadcasted_iota(jnp.int32, (1, LANES), 1)
        kr = jnp.where((lane >= A_NOPE) & (lane < A_NOPE + A_ROPE), g_ref[...], 0.0)
        kr = _rope_apply(kr, cosf, s1, s2)
        kr = jnp.where((lane == A_BIAS_LANE_K) | (lane == A_BIAS_LANE_K + 1), 1.0, kr)
        v_one = (lane == A_BIAS_LANE_V) | (lane == A_BIAS_LANE_V + 1)
        for hd in range(A_HEADS):
            sl = slice(hd * LANES, (hd + 1) * LANES)
            qo_ref[:, sl] = (_rope_apply(q_ref[:, sl], cosf, s1, s2) * ATTN_SCALE).astype(qo_ref.dtype)
            ko_ref[:, sl] = (k_ref[:, sl] + kr).astype(ko_ref.dtype)
            vo_ref[:, sl] = jnp.where(v_one, jnp.ones((), BF16), v_ref[:, sl])

    row = pl.BlockSpec((tm, w), lambda i: (i, 0))
    out = jax.ShapeDtypeStruct((t, w), BF16)
    return pl.pallas_call(
        body, name=name, grid=(t // tm,),
        in_specs=[row, row, row, pl.BlockSpec((tm, LANES), lambda i: (i, 0)), pl.BlockSpec((tm, 1), lambda i: (i, 0))],
        out_specs=[row, row, row], out_shape=[out, out, out], compiler_params=_cparams("parallel"),
    )(qpre, kpre, vpre, gates, pos)


def _rope_bwd(dq, dk, dgates, pos, name, tm=512):
    t, w = dq.shape

    def body(dq_ref, dk_ref, dg_ref, p_ref, dqo_ref, dgo_ref):
        cosf, s1, s2 = _rope_tables(p_ref[...], LANES)
        acc = jnp.zeros((tm, LANES), F32)
        for hd in range(A_HEADS):
            sl = slice(hd * LANES, (hd + 1) * LANES)
            dqo_ref[:, sl] = (_rope_apply_t(dq_ref[:, sl], cosf, s1, s2) * ATTN_SCALE).astype(dqo_ref.dtype)
            acc = acc + dk_ref[:, sl].astype(F32)
        lane = lax.broadcasted_iota(jnp.int32, (1, LANES), 1)
        dkr = _rope_apply_t(acc, cosf, s1, s2)
        dkr = jnp.where((lane >= A_NOPE) & (lane < A_NOPE + A_ROPE), dkr, 0.0)
        dgo_ref[...] = (dg_ref[...] + dkr).astype(dgo_ref.dtype)

    row = pl.BlockSpec((tm, w), lambda i: (i, 0))
    nar = pl.BlockSpec((tm, LANES), lambda i: (i, 0))
    return pl.pallas_call(
        body, name=name, grid=(t // tm,),
        in_specs=[row, row, nar, pl.BlockSpec((tm, 1), lambda i: (i, 0))],
        out_specs=[row, nar],
        out_shape=[jax.ShapeDtypeStruct((t, w), BF16), jax.ShapeDtypeStruct((t, LANES), BF16)],
        compiler_params=_cparams("parallel"),
    )(dq, dk, dgates, pos)


A_TQ = 512
A_TK = 512


A_HEADS_PER_STEP = 2


def _flash_fwd(q, k, v, name):
    t = q.shape[0]
    tq, tk = A_TQ, A_TK
    nh = A_HEADS_PER_STEP
    wblk = nh * LANES

    def body(q_ref, k_ref, v_ref, o_ref, qb_ref):
        i = pl.program_id(1)
        lane = lax.broadcasted_iota(jnp.int32, (tq, LANES), 1)
        qpos = i * tq + lax.broadcasted_iota(jnp.int32, (tq, tk), 0)
        kofs = lax.broadcasted_iota(jnp.int32, (tq, tk), 1)
        qs = [q_ref[:, hh * LANES:(hh + 1) * LANES] for hh in range(nh)]

        def step(kb, carry, masked):
            rows = pl.ds(pl.multiple_of(kb * tk, tk), tk)
            new = []
            for hh in range(nh):
                m, acc = carry[hh]
                kh = k_ref[rows, hh * LANES:(hh + 1) * LANES]
                s = lax.dot_general(qs[hh], kh, _DIMS["nt"], preferred_element_type=F32)
                if masked:
                    s = jnp.where(kb * tk + kofs <= qpos, s, NEG_INF)
                m_new = jnp.maximum(m, jnp.max(s, axis=1, keepdims=True))
                p = jnp.exp(s - m_new)
                acc = jnp.exp(m - m_new) * acc + jnp.dot(p.astype(BF16), v_ref[rows, hh * LANES:(hh + 1) * LANES],
                                                          preferred_element_type=F32)
                new.append((m_new, acc))
            return tuple(new)

        carry = tuple((jnp.full((tq, 1), NEG_INF, F32), jnp.zeros((tq, LANES), F32)) for _ in range(nh))
        n_full = (i * tq) // tk
        carry = lax.fori_loop(0, n_full, functools.partial(step, masked=False), carry)
        for d in range(tq // tk):
            carry = step(n_full + d, carry, True)
        for hh in range(nh):
            m, acc = carry[hh]
            l = acc[:, A_BIAS_LANE_V:A_BIAS_LANE_V + 1]
            o_ref[:, hh * LANES:(hh + 1) * LANES] = (acc / l).astype(o_ref.dtype)
            hi, lo = _hi_lo(-(m + jnp.log(l)))
            qb_ref[:, hh * LANES:(hh + 1) * LANES] = jnp.where(
                lane == A_BIAS_LANE_K, hi, jnp.where(lane == A_BIAS_LANE_K + 1, lo, qs[hh]))

    blk = pl.BlockSpec((tq, wblk), lambda j, i: (i, j))
    res = pl.BlockSpec((t, wblk), lambda j, i: (0, j))
    out = jax.ShapeDtypeStruct((t, A_HEADS * LANES), BF16)
    return pl.pallas_call(
        body, name=name, grid=(A_HEADS // nh, t // tq), in_specs=[blk, res, res], out_specs=[blk, blk],
        out_shape=[out, out], compiler_params=_cparams("parallel", "parallel"),
    )(q, k, v)


def _attn_delta(do, o, name, tm=512):
    t, w = do.shape

    def body(do_ref, o_ref, out_ref):
        lane = lax.broadcasted_iota(jnp.int32, (tm, LANES), 1)
        for hd in range(A_HEADS):
            sl = slice(hd * LANES, (hd + 1) * LANES)
            d = do_ref[:, sl]
            delta = jnp.sum(jnp.where(lane < A_V, d.astype(F32) * o_ref[:, sl].astype(F32), 0.0), axis=1, keepdims=True)
            hi, lo = _hi_lo(-delta)
            out_ref[:, sl] = jnp.where(lane == A_BIAS_LANE_V, hi, jnp.where(lane == A_BIAS_LANE_V + 1, lo, d))

    row = pl.BlockSpec((tm, w), lambda i: (i, 0))
    return pl.pallas_call(
        body, name=name, grid=(t // tm,), in_specs=[row, row], out_specs=row,
        out_shape=jax.ShapeDtypeStruct((t, w), BF16), compiler_params=_cparams("parallel"),
    )(do, o)


def _flash_bwd(qb, k, v, doe, name):
    t = qb.shape[0]
    tq, tk = A_TQ, A_TK
    assert tq == tk
    nh = A_HEADS_PER_STEP
    wblk = nh * LANES
    nq = t // tq

    def body(q_ref, k_ref, v_ref, do_ref, dq_ref, dk_ref, dv_ref):
        kb = pl.program_id(1)

        @pl.when(kb == 0)
        def _():
            dq_ref[...] = jnp.zeros_like(dq_ref)

        kpos = kb * tk + lax.broadcasted_iota(jnp.int32, (tk, tq), 0)
        qofs = lax.broadcasted_iota(jnp.int32, (tk, tq), 1)
        ks = [k_ref[:, hh * LANES:(hh + 1) * LANES] for hh in range(nh)]
        vs = [v_ref[:, hh * LANES:(hh + 1) * LANES] for hh in range(nh)]

        def step(qi, carry, masked):
            rows = pl.ds(pl.multiple_of(qi * tq, tq), tq)
            new = []
            for hh in range(nh):
                dk_acc, dv_acc = carry[hh]
                qh = q_ref[rows, hh * LANES:(hh + 1) * LANES]
                doh = do_ref[rows, hh * LANES:(hh + 1) * LANES]
                p_t = jnp.exp(lax.dot_general(ks[hh], qh, _DIMS["nt"], preferred_element_type=F32))
                if masked:
                    p_t = jnp.where(qi * tq + qofs >= kpos, p_t, 0.0)
                ds_t = (p_t * lax.dot_general(vs[hh], doh, _DIMS["nt"], preferred_element_type=F32)).astype(BF16)
                dv_acc = dv_acc + jnp.dot(p_t.astype(BF16), doh, preferred_element_type=F32)
                dk_acc = dk_acc + jnp.dot(ds_t, qh, preferred_element_type=F32)
                dq_ref[rows, hh * LANES:(hh + 1) * LANES] += lax.dot_general(ds_t, ks[hh], _DIMS["tn"],
                                                                             preferred_element_type=F32)
                new.append((dk_acc, dv_acc))
            return tuple(new)

        carry = tuple((jnp.zeros((tk, LANES), F32), jnp.zeros((tk, LANES), F32)) for _ in range(nh))
        carry = step(kb, carry, True)
        carry = lax.fori_loop(kb + 1, nq, functools.partial(step, masked=False), carry)
        for hh in range(nh):
            dk_ref[:, hh * LANES:(hh + 1) * LANES] = carry[hh][0].astype(dk_ref.dtype)
            dv_ref[:, hh * LANES:(hh + 1) * LANES] = carry[hh][1].astype(dv_ref.dtype)

    res = pl.BlockSpec((t, wblk), lambda j, kb: (0, j))
    blk = pl.BlockSpec((tk, wblk), lambda j, kb: (kb, j))
    wide = (t, A_HEADS * LANES)
    return pl.pallas_call(
        body, name=name, grid=(A_HEADS // nh, t // tk), in_specs=[res, blk, blk, res], out_specs=[res, blk, blk],
        out_shape=[jax.ShapeDtypeStruct(wide, F32), jax.ShapeDtypeStruct(wide, BF16), jax.ShapeDtypeStruct(wide, BF16)],
        compiler_params=_cparams("parallel", "arbitrary"),
    )(qb, k, v, doe)


_SPLITS = (M_WIDTH, M_WIDTH, M_WIDTH, M_WIDTH, M_HEADS, M_HEADS, Q_RANK, KV_RANK, A_ROPE)
_OFFS = tuple(sum(_SPLITS[:i]) for i in range(len(_SPLITS) + 1))
_GATE_BLOCK_KR = A_NOPE


def _split_w_in_t(wt):
    z = lambda n: jnp.zeros((n, wt.shape[1]), wt.dtype)
    w_g = jnp.concatenate([wt[_OFFS[4]:_OFFS[6]], z(_GATE_BLOCK_KR - 2 * M_HEADS), wt[_OFFS[8]:_OFFS[9]],
                           z(LANES - _GATE_BLOCK_KR - A_ROPE)], axis=0)
    return wt[_OFFS[6]:_OFFS[8]], w_g


def _merge_w_in_grad_t(g_m4, g_a2, g_g):
    return jnp.concatenate([g_m4, g_g[:2 * M_HEADS], g_a2, g_g[_GATE_BLOCK_KR:_GATE_BLOCK_KR + A_ROPE]], axis=0)


def _pad_heads(w, used):
    w3 = w.reshape(A_HEADS, used, w.shape[1])
    return jnp.pad(w3, ((0, 0), (0, LANES - used), (0, 0))).reshape(A_HEADS * LANES, w.shape[1])


def _unpad_heads(g, used):
    return g.reshape(A_HEADS, LANES, g.shape[1])[:, :used].reshape(A_HEADS * used, g.shape[1])


def _split_w_kv_b_t(wt):
    w3 = wt.reshape(A_HEADS, A_NOPE + A_V, wt.shape[1])
    pad = lambda part: jnp.pad(part, ((0, 0), (0, LANES - part.shape[1]), (0, 0))).reshape(A_HEADS * LANES, wt.shape[1])
    return pad(w3[:, :A_NOPE]), pad(w3[:, A_NOPE:])


def _merge_w_kv_b_grad_t(gk, gv):
    gk3 = gk.reshape(A_HEADS, LANES, gk.shape[1])[:, :A_NOPE]
    gv3 = gv.reshape(A_HEADS, LANES, gv.shape[1])[:, :A_V]
    return jnp.concatenate([gk3, gv3], axis=1).reshape(A_HEADS * (A_NOPE + A_V), gk.shape[1])


def _local_step(x, pos, target, nw, w, start_late_reduce):
    w_in = w["w_in"]
    w_a2, w_g = _split_w_in_t(w_in)
    w_qb = _pad_heads(w["w_q_b"], A_NOPE + A_ROPE)
    w_k, w_v = _split_w_kv_b_t(w["w_kv_b"])
    w_out_m, w_out_a = w["w_out"][:M_WIDTH], _pad_heads(w["w_out"][M_WIDTH:], A_V)
    w_gate, w_up, w_down = w["w_gate"], w["w_up"], w["w_down"]
    bias = jnp.pad(nw["b_gates"], ((0, 0), (0, LANES - 2 * M_HEADS)))
    mnorm = nw["mlstm_norm_w"].reshape(M_HEADS, 1, M_HEAD_DIM)
    n_m4 = 4 * M_WIDTH
    t_len = x.shape[0]

    u1 = _rmsnorm_fwd(x, nw["attn_norm_w"], "attn_norm")
    m4 = _matmul(u1, w_in, "nt", BF16, "proj_mlstm", tn=M_WIDTH, b_rows=n_m4, lead_scale=(M_WIDTH, M_SCALE))
    a2 = _matmul(u1, w_a2, "nt", BF16, "proj_latents", tn=640)
    gates = _matmul(u1, w_g, "nt", F32, "proj_gates")
    q_t = _chunk_major_t(_matmul(w_in[:M_WIDTH], u1, "nt", BF16, "proj_q_t", lead_scale=(t_len, M_SCALE)))
    k_t = _chunk_major_t(_matmul(w_in[M_WIDTH:2 * M_WIDTH], u1, "nt", BF16, "proj_k_t"))
    hm, c_prev_all, nm_all = _mlstm_fwd(m4, k_t, gates, bias, mnorm, "mlstm_fwd")
    qa, kva = a2[:, :Q_RANK], a2[:, Q_RANK:]
    qa_n = _rmsnorm_fwd(qa, nw["q_a_norm_w"], "q_a_norm")
    kv_n = _rmsnorm_fwd(kva, nw["kv_a_norm_w"], "kv_a_norm")
    qpre = _matmul(qa_n, w_qb, "nt", F32, "q_up")
    kpre = _matmul(kv_n, w_k, "nt", F32, "k_up")
    vpre = _matmul(kv_n, w_v, "nt", BF16, "v_up")
    qr, kr, vv = _rope_fwd(qpre, kpre, vpre, gates, pos, "rope_fwd")
    ha, qb = _flash_fwd(qr, kr, vv, "attn_fwd")
    h1 = _matmul([hm, ha], [w_out_m, w_out_a], "nn", F32, "out_proj", residual=x)
    u2 = _rmsnorm_fwd(h1, nw["ffn_norm_w"], "ffn_norm")
    gg, uu, act = _ffn_gate_up(u2, w_gate, w_up, "ffn_gate_up")
    h2 = _matmul(act, w_down, "nn", F32, "ffn_down", residual=h1)

    dh2, dh2_b, loss, g_final = _final_loss(h2, target, nw["final_norm_w"].reshape(1, D_MODEL), "final_loss")
    grads = {"final_norm_w": g_final.reshape(D_MODEL)}
    grads_b = {}
    dgg, duu = _ffn_down_bwd(dh2_b, w_down, gg, uu, "ffn_down_bwd")
    grads["w_down"], grads_b["w_down"] = _matmul(act, dh2_b, "tn", F32, "g_w_down", tm=256, tn=D_MODEL, also_bf16=True)
    du2 = _matmul([dgg, duu], [w_gate, w_up], "nn", F32, "d_u2", tm=512, tn=512)
    grads["w_gate"], grads_b["w_gate"] = _matmul(dgg, u2, "tn", F32, "g_w_gate", tm=256, tn=D_MODEL, also_bf16=True)
    grads["w_up"], grads_b["w_up"] = _matmul(duu, u2, "tn", F32, "g_w_up", tm=256, tn=D_MODEL, also_bf16=True)
    dh1, dh1_b, grads["ffn_norm_w"] = _rmsnorm_bwd(du2, h1, nw["ffn_norm_w"], "ffn_norm_bwd", F32, residual=dh2,
                                                   also_bf16=True)
    dhm = _matmul(dh1_b, w_out_m, "nt", BF16, "d_hm")
    dha = _matmul(dh1_b, w_out_a, "nt", BF16, "d_ha")
    grads["w_out"] = jnp.concatenate([_matmul(hm, dh1_b, "tn", F32, "g_w_out_m", tn=D_MODEL),
                                      _unpad_heads(_matmul(ha, dh1_b, "tn", F32, "g_w_out_a", tn=D_MODEL), A_V)], axis=0)

    late, dha = start_late_reduce(grads, grads_b, dha)

    dqr, dkr, dvv = _flash_bwd(qb, kr, vv, _attn_delta(dha, ha, "attn_delta"), "attn_bwd")
    dm4, dgates, small = _mlstm_bwd(m4, q_t, gates, bias, mnorm, c_prev_all, nm_all, dhm, "mlstm_bwd")
    grads["mlstm_norm_w"] = small[:M_HEADS].reshape(1, M_HEADS, M_HEAD_DIM)
    grads["b_gates"] = small[M_HEADS:M_HEADS + 1, :2 * M_HEADS]
    dqpre, dgk = _rope_bwd(dqr, dkr, dgates, pos, "rope_bwd")
    dqa_n = _matmul(dqpre, w_qb, "nn", BF16, "d_qa_n")
    grads["w_q_b"] = _unpad_heads(_matmul(dqpre, qa_n, "tn", F32, "g_w_q_b"), A_NOPE + A_ROPE)
    dkv_n = _matmul([dkr, dvv], [w_k, w_v], "nn", BF16, "d_kv_n")
    grads["w_kv_b"] = _merge_w_kv_b_grad_t(_matmul(dkr, kv_n, "tn", F32, "g_w_k"),
                                           _matmul(dvv, kv_n, "tn", F32, "g_w_v"))
    dqa, grads["q_a_norm_w"] = _rmsnorm_bwd(dqa_n, qa, nw["q_a_norm_w"], "q_a_norm_bwd", BF16)
    dkva, grads["kv_a_norm_w"] = _rmsnorm_bwd(dkv_n, kva, nw["kv_a_norm_w"], "kv_a_norm_bwd", BF16)
    da2 = jnp.concatenate([dqa, dkva], axis=1)
    du1 = _matmul([dm4, da2, dgk], [w_in, w_a2, w_g], "nn", F32, "d_u1", b_rows=[n_m4, None, None])
    grads["w_in"] = _merge_w_in_grad_t(_matmul(dm4, u1, "tn", F32, "g_w_m4", tn=D_MODEL),
                                       _matmul(da2, u1, "tn", F32, "g_w_a2", tm=640, tn=D_MODEL),
                                       _matmul(dgk, u1, "tn", F32, "g_w_g", tn=D_MODEL))
    grad_x, grads["attn_norm_w"] = _rmsnorm_bwd(du1, x, nw["attn_norm_w"], "attn_norm_bwd", F32, residual=dh1)
    return loss, grad_x, grads, grads_b, late


MESH = pl.DeviceIdType.MESH
N_CHIPS = 4
EARLY = ("w_in", "w_q_b", "w_kv_b")
LATE = ("w_out", "w_gate", "w_up", "w_down")
BIG = EARLY + LATE
TRANSPOSED = ("w_in", "w_q_b", "w_kv_b", "w_gate", "w_up")
LANE_HALVED = ("w_in", "w_out", "w_gate", "w_up", "w_down")
SMALL = ("attn_norm_w", "b_gates", "mlstm_norm_w", "q_a_norm_w", "kv_a_norm_w", "ffn_norm_w", "final_norm_w")
HBM_SPEC = pl.BlockSpec(memory_space=pltpu.HBM)


def _stored(name, a):
    return a[0].T if name in TRANSPOSED else a[0]


def _unstored(name, a):
    return (a.T if name in TRANSPOSED else a)[None]


def _half_shape(name, shape):
    rs, cs = shape
    return (rs, cs // 2) if name in LANE_HALVED else (rs // 2, cs)


def _half(ref, name, h, *lead):
    rs, cs = ref.shape[-2:]
    if name in LANE_HALVED:
        return ref.at[(*lead, slice(None), pl.ds(h * (cs // 2), cs // 2))]
    return ref.at[(*lead, pl.ds(h * (rs // 2), rs // 2), slice(None))]


def _place():
    x, y, c = lax.axis_index("x"), lax.axis_index("y"), lax.axis_index("c")
    others = [(1 - x, y), (x, 1 - y), (1 - x, 1 - y)]
    return x, y, c, others


def _gather_weights(names, shards):
    n = len(shards)

    def body(*refs):
        _gather_body(names, refs[:n], refs[n:2 * n], *refs[2 * n:])

    gathered = pl.pallas_call(
        body, name="gather_weights", in_specs=[HBM_SPEC] * n, out_specs=[HBM_SPEC] * n,
        out_shape=[jax.ShapeDtypeStruct((N_CHIPS,) + s.shape, s.dtype) for s in shards],
        scratch_shapes=[pltpu.SemaphoreType.DMA((6 * n,)), pltpu.SemaphoreType.DMA((6 * n,))],
    )(*shards)
    return _with_own_slab(gathered, shards)


def _with_own_slab(gathered, shards):
    me = 2 * lax.axis_index("x") + lax.axis_index("y")
    return [lax.dynamic_update_slice(g, s[None], (me, 0, 0)) for g, s in zip(gathered, shards)]


def _gather_body(names, ins, outs, send_sems, recv_sems):
    x, y, c, others = _place()
    me = 2 * x + y
    sibling = (x, y, 1 - c)

    def copy(w, k, slab, core, to, src=None):
        dst = _half(outs[w], names[w], core, slab)
        return pltpu.make_async_remote_copy(
            src_ref=dst if src is None else src, dst_ref=dst, send_sem=send_sems.at[w * 6 + k],
            recv_sem=recv_sems.at[w * 6 + k], device_id=to, device_id_type=MESH)

    sends = []
    for w in range(len(names)):
        for j, chip in enumerate(others):
            cp = copy(w, j, me, c, (*chip, c), src=_half(ins[w], names[w], c))
            cp.start()
            sends.append(cp)
    for w in range(len(names)):
        for j, (ox, oy) in enumerate(others):
            slab = 2 * ox + oy
            copy(w, j, slab, c, (x, y, c)).wait_recv()
            fwd = copy(w, 3 + j, slab, c, sibling)
            fwd.start()
            sends.append(fwd)
    for w in range(len(names)):
        for j, (ox, oy) in enumerate(others):
            copy(w, 3 + j, 2 * ox + oy, 1 - c, (x, y, c)).wait_recv()
    for cp in sends:
        cp.wait_send()


GATHER_LATE_COLLECTIVE_ID = 1


def _gather_weights_async(names, shards):
    n = len(shards)
    src = [jax.new_ref(s, memory_space=pltpu.MemorySpace.HBM) for s in shards]
    dst = [jax.empty_ref(jax.ShapeDtypeStruct((N_CHIPS,) + s.shape, s.dtype), memory_space=pltpu.MemorySpace.HBM)
           for s in shards]

    @pl.kernel(mesh=plsc.ScalarSubcoreMesh(axis_name="sequencer", num_cores=1), name="gather_weights_async",
               scratch_types=(pltpu.SemaphoreType.DMA((6 * n,)), pltpu.SemaphoreType.DMA((6 * n,))),
               compiler_params=pltpu.CompilerParams(collective_id=GATHER_LATE_COLLECTIVE_ID))
    def launch(send_sems, recv_sems):
        x, y, c, others = _place()
        peers = [(ox, oy, c) for ox, oy in others] + [(x, y, 1 - c)]
        barrier = pltpu.get_barrier_semaphore()
        for peer in peers:
            pl.semaphore_signal(barrier, inc=1, device_id=peer, device_id_type=MESH)
        pl.semaphore_wait(barrier, len(peers))
        _gather_body(names, src, dst, send_sems, recv_sems)

    launch()
    return _with_own_slab([d[...] for d in dst], shards)


def _exchange(arrays, out_shapes, plan, copies_per_array, name):
    n = len(arrays)

    def body(*refs):
        ins, outs = refs[:n], refs[n:2 * n]
        send_sems, recv_sems = refs[2 * n:]
        rem = [pltpu.make_async_remote_copy(src_ref=s, dst_ref=d, send_sem=send_sems.at[k], recv_sem=recv_sems.at[k],
                                            device_id=to, device_id_type=MESH)
               for k, (s, d, to) in enumerate(plan(ins, outs, _place()))]
        for cp in rem:
            cp.start()
        for cp in rem:
            cp.wait_recv()
        for cp in rem:
            cp.wait_send()

    return pl.pallas_call(
        body, name=name, in_specs=[HBM_SPEC] * n, out_specs=[HBM_SPEC] * n, out_shape=out_shapes,
        scratch_shapes=[pltpu.SemaphoreType.DMA((copies_per_array * n,)),
                        pltpu.SemaphoreType.DMA((copies_per_array * n,))],
    )(*arrays)


def _plan_to_sibling(ins, outs, place):
    x, y, c, _ = place
    return [(ins[w], outs[w], (x, y, 1 - c)) for w in range(len(ins))]


def _plan_sibling_halves(names, ins, outs, place):
    x, y, c, _ = place
    return [(_half(ins[w], names[w], 1 - c, slice(None)), outs[w], (x, y, 1 - c)) for w in range(len(ins))]


def _plan_chip_partials(ins, outs, place):
    x, y, c, others = place
    return [(ins[w].at[2 * ox + oy], outs[w].at[j], (ox, oy, c))
            for w in range(len(ins)) for j, (ox, oy) in enumerate(others)]


def _chip_partial(g3, got, name, core):
    hs = tuple(got.shape[1:])
    if name in LANE_HALVED:
        g_in, g_spec = g3, pl.BlockSpec((1,) + hs, lambda s, c_ref: (s, 0, c_ref[0]))
    else:
        g_in, g_spec = g3.reshape((N_CHIPS, 2) + hs), pl.BlockSpec((1, 1) + hs, lambda s, c_ref: (s, c_ref[0], 0, 0))
    blk = pl.BlockSpec((1,) + hs, lambda s, c_ref: (s, 0, 0))

    def body(c_ref, g_ref, r_ref, o_ref):
        o_ref[0] = (g_ref[(0,) * (len(g_ref.shape) - 2)] + r_ref[0].astype(F32)).astype(o_ref.dtype)

    return pl.pallas_call(
        body, name="chip_partial_%s" % name,
        grid_spec=pltpu.PrefetchScalarGridSpec(num_scalar_prefetch=1, grid=(N_CHIPS,), in_specs=[g_spec, blk],
                                               out_specs=blk),
        out_shape=jax.ShapeDtypeStruct((N_CHIPS,) + hs, BF16), compiler_params=_cparams("parallel"),
    )(core, g_in, got)


def _shard_sum(part, got3, name, me):
    hs = tuple(part.shape[1:])

    def body(me_ref, p_ref, a_ref, b_ref, c_ref, o_ref):
        o_ref[...] = ((p_ref[0].astype(F32) + a_ref[0].astype(F32)) + b_ref[0].astype(F32)) + c_ref[0].astype(F32)

    slot = lambda j: pl.BlockSpec((1,) + hs, lambda i, me_ref: (j, 0, 0))
    return pl.pallas_call(
        body, name="shard_sum_%s" % name,
        grid_spec=pltpu.PrefetchScalarGridSpec(
            num_scalar_prefetch=1, grid=(1,),
            in_specs=[pl.BlockSpec((1,) + hs, lambda i, me_ref: (me_ref[0], 0, 0)), slot(0), slot(1), slot(2)],
            out_specs=pl.BlockSpec(hs, lambda i, me_ref: (0, 0))),
        out_shape=jax.ShapeDtypeStruct(hs, F32), compiler_params=_cparams("arbitrary"),
    )(me, part, got3, got3, got3)


def _adamw_halves(w, m, v, mine, other, name, core):
    hs = tuple(mine.shape)
    full = pl.BlockSpec(hs, (lambda h, c_ref: (0, h)) if name in LANE_HALVED else (lambda h, c_ref: (h, 0)))
    half = pl.BlockSpec(hs, lambda h, c_ref: (0, 0))

    def body(c_ref, w_ref, m_ref, v_ref, a_ref, b_ref, g_ref, d_ref, mo_ref, vo_ref):
        g = jnp.where(pl.program_id(0) == c_ref[0], a_ref[...], b_ref[...])
        delta, m_new, v_new = _adamw_math(w_ref[...], g, m_ref[...], v_ref[...])
        g_ref[...] = g
        d_ref[...] = delta
        mo_ref[...] = m_new
        vo_ref[...] = v_new

    out = jax.ShapeDtypeStruct(w.shape, F32)
    return pl.pallas_call(
        body, name="adamw_%s" % name,
        grid_spec=pltpu.PrefetchScalarGridSpec(num_scalar_prefetch=1, grid=(2,), in_specs=[full, full, full, half, half],
                                               out_specs=[full] * 4),
        out_shape=[out] * 4, compiler_params=_cparams("parallel"),
    )(core, w, m, v, mine, other)


def _adamw_math(w, g, m, v):
    m = ADAM_B1 * m + (1.0 - ADAM_B1) * g
    v = ADAM_B2 * v + (1.0 - ADAM_B2) * (g * g)
    m_hat = m / (1.0 - ADAM_B1 ** ADAM_STEP)
    v_hat = v / (1.0 - ADAM_B2 ** ADAM_STEP)
    delta = -ADAM_LR * (m_hat / (jnp.sqrt(v_hat) + ADAM_EPS) + ADAM_WD * w)
    return delta, m, v


def _core_index():
    return lax.axis_index("c").astype(jnp.int32).reshape(1)


def _chip_partials_of(names, grads, grads_b, tag):
    by_shard = lambda g: g.reshape(N_CHIPS, g.shape[0] // N_CHIPS, g.shape[1])
    g3 = [by_shard(grads[n]) for n in names]
    g3_b = [by_shard(grads_b[n]) if n in grads_b else g.astype(BF16) for n, g in zip(names, g3)]
    shapes = [jax.ShapeDtypeStruct((N_CHIPS,) + _half_shape(n, g.shape[1:]), BF16) for n, g in zip(names, g3)]
    got = _exchange(g3_b, shapes, functools.partial(_plan_sibling_halves, names), 1, "sibling_halves_" + tag)
    return [_chip_partial(g, r, n, _core_index()) for n, g, r in zip(names, g3, got)]


REDUCE_LATE_COLLECTIVE_ID = 2


def _exchange_partials_async(part):
    n = len(part)
    src = [jax.new_ref(p, memory_space=pltpu.MemorySpace.HBM) for p in part]
    dst = [jax.empty_ref(jax.ShapeDtypeStruct((3,) + p.shape[1:], p.dtype), memory_space=pltpu.MemorySpace.HBM)
           for p in part]

    @pl.kernel(mesh=plsc.ScalarSubcoreMesh(axis_name="sequencer", num_cores=1), name="chip_partials_async",
               scratch_types=(pltpu.SemaphoreType.DMA((3 * n,)), pltpu.SemaphoreType.DMA((3 * n,))),
               compiler_params=pltpu.CompilerParams(collective_id=REDUCE_LATE_COLLECTIVE_ID))
    def launch(send_sems, recv_sems):
        place = _place()
        x, y, c, others = place
        barrier = pltpu.get_barrier_semaphore()
        for ox, oy in others:
            pl.semaphore_signal(barrier, inc=1, device_id=(ox, oy, c), device_id_type=MESH)
        pl.semaphore_wait(barrier, len(others))
        rem = [pltpu.make_async_remote_copy(src_ref=s, dst_ref=d, send_sem=send_sems.at[k], recv_sem=recv_sems.at[k],
                                            device_id=to, device_id_type=MESH)
               for k, (s, d, to) in enumerate(_plan_chip_partials(src, dst, place))]
        for cp in rem:
            cp.start()
        for cp in rem:
            cp.wait_recv()
        for cp in rem:
            cp.wait_send()

    launch()
    return [d[...] for d in dst]


def _finish_reduce(names, part, got3):
    me = (2 * lax.axis_index("x") + lax.axis_index("y")).astype(jnp.int32).reshape(1)
    mine = [_shard_sum(p, r3, n, me) for n, p, r3 in zip(names, part, got3)]
    other = _exchange(mine, [jax.ShapeDtypeStruct(m.shape, F32) for m in mine], _plan_to_sibling, 1, "sibling_result")
    return {n: (a, b) for n, a, b in zip(names, mine, other)}


SMALL_ROWS = 8
SMALL_LAYOUT = {"attn_norm_w": (0, 0, 1024), "ffn_norm_w": (1, 0, 1024), "final_norm_w": (2, 0, 1024),
                "q_a_norm_w": (3, 0, 384), "kv_a_norm_w": (3, 384, 256), "mlstm_norm_w": (4, 0, 512),
                "b_gates": (4, 512, 8)}
LOSS_SLOT = (5, 0)


def _pack_small(vals, loss=None):
    tile = jnp.zeros((SMALL_ROWS, D_MODEL), F32)
    for n, (r, c0, width) in SMALL_LAYOUT.items():
        tile = tile.at[r, c0:c0 + width].set(vals[n].reshape(width).astype(F32))
    if loss is not None:
        tile = tile.at[LOSS_SLOT[0], LOSS_SLOT[1]].set(loss)
    return tile


def _unpack_small(tile, shapes):
    return {n: tile[r, c0:c0 + width].reshape(shapes[n]) for n, (r, c0, width) in SMALL_LAYOUT.items()}


def _small_allreduce_adamw(g_tile, w_tile, m_tile, v_tile):
    def body(g_ref, w_ref, m_ref, v_ref, gsum_ref, d_ref, mo_ref, vo_ref, slots, send_sems, recv_sems):
        x, y, c, _ = _place()
        me = 4 * x + 2 * y + c
        slots[me] = g_ref[...]
        copies = []
        for k in range(1, 8):
            dx, dy, dc = (k >> 2) & 1, (k >> 1) & 1, k & 1
            to = (x ^ dx, y ^ dy, c ^ dc)
            cp = pltpu.make_async_remote_copy(src_ref=g_ref, dst_ref=slots.at[me], send_sem=send_sems.at[k - 1],
                                              recv_sem=recv_sems.at[k - 1], device_id=to, device_id_type=MESH)
            cp.start()
            copies.append(cp)
        for k in range(1, 8):
            src = me ^ k
            pltpu.make_async_remote_copy(src_ref=g_ref, dst_ref=slots.at[src], send_sem=send_sems.at[k - 1],
                                         recv_sem=recv_sems.at[k - 1], device_id=(x, y, c),
                                         device_id_type=MESH).wait_recv()
        for cp in copies:
            cp.wait_send()
        total = slots[0]
        for d in range(1, 8):
            total = total + slots[d]
        gsum_ref[...] = total
        delta, m_new, v_new = _adamw_math(w_ref[...], total, m_ref[...], v_ref[...])
        d_ref[...] = delta
        mo_ref[...] = m_new
        vo_ref[...] = v_new

    vm = pl.BlockSpec(memory_space=pltpu.VMEM)
    tile = jax.ShapeDtypeStruct((SMALL_ROWS, D_MODEL), F32)
    return pl.pallas_call(
        body, name="small_allreduce_adamw", in_specs=[vm] * 4, out_specs=[vm] * 4, out_shape=[tile] * 4,
        scratch_shapes=[pltpu.VMEM((8, SMALL_ROWS, D_MODEL), F32), pltpu.SemaphoreType.DMA((7,)),
                        pltpu.SemaphoreType.DMA((7,))],
    )(g_tile, w_tile, m_tile, v_tile)


def kernel(x, positions, attn_norm_w, w_in, b_gates, mlstm_norm_w, q_a_norm_w, w_q_b, kv_a_norm_w, w_kv_b, w_out, ffn_norm_w, w_gate, w_up, w_down, final_norm_w, loss_target, m_attn_norm_w, m_w_in, m_b_gates, m_mlstm_norm_w, m_q_a_norm_w, m_w_q_b, m_kv_a_norm_w, m_w_kv_b, m_w_out, m_ffn_norm_w, m_w_gate, m_w_up, m_w_down, m_final_norm_w, v_attn_norm_w, v_w_in, v_b_gates, v_mlstm_norm_w, v_q_a_norm_w, v_w_q_b, v_kv_a_norm_w, v_w_kv_b, v_w_out, v_ffn_norm_w, v_w_gate, v_w_up, v_w_down, v_final_norm_w):
    names = ("attn_norm_w", "w_in", "b_gates", "mlstm_norm_w", "q_a_norm_w", "w_q_b", "kv_a_norm_w", "w_kv_b", "w_out",
             "ffn_norm_w", "w_gate", "w_up", "w_down", "final_norm_w")
    wts = dict(zip(names, (attn_norm_w, w_in, b_gates, mlstm_norm_w, q_a_norm_w, w_q_b, kv_a_norm_w, w_kv_b, w_out,
                           ffn_norm_w, w_gate, w_up, w_down, final_norm_w)))
    mom = dict(zip(names, (m_attn_norm_w, m_w_in, m_b_gates, m_mlstm_norm_w, m_q_a_norm_w, m_w_q_b, m_kv_a_norm_w,
                           m_w_kv_b, m_w_out, m_ffn_norm_w, m_w_gate, m_w_up, m_w_down, m_final_norm_w)))
    vel = dict(zip(names, (v_attn_norm_w, v_w_in, v_b_gates, v_mlstm_norm_w, v_q_a_norm_w, v_w_q_b, v_kv_a_norm_w,
                           v_w_kv_b, v_w_out, v_ffn_norm_w, v_w_gate, v_w_up, v_w_down, v_final_norm_w)))
    t = x.shape[1]

    shards = {n: _stored(n, wts[n]).astype(BF16) for n in BIG}
    early = _gather_weights(EARLY, [shards[n] for n in EARLY])
    early, late_in = lax.optimization_barrier((early, [shards[n] for n in LATE]))
    late = _gather_weights_async(LATE, late_in)
    full = {n: g.reshape(N_CHIPS * g.shape[1], g.shape[2]) for n, g in zip(EARLY + LATE, early + late)}

    nw = {n: wts[n] for n in SMALL}
    def start_late_reduce(grads, grads_b, marker):
        part = _chip_partials_of(LATE, grads, grads_b, "late")
        part, marker = lax.optimization_barrier((part, marker))
        return (part, _exchange_partials_async(part)), marker

    loss, grad_x, grads, grads_b, (late_part, late_got3) = _local_step(x[0], positions.reshape(t, 1), loss_target[0],
                                                                       nw, full, start_late_reduce)

    early_part = _chip_partials_of(EARLY, grads, grads_b, "early")
    early_got3 = _exchange(early_part, [jax.ShapeDtypeStruct((3,) + p.shape[1:], BF16) for p in early_part],
                           _plan_chip_partials, 3, "chip_partials_early")
    halves = _finish_reduce(BIG, list(early_part) + list(late_part), list(early_got3) + list(late_got3))
    core = _core_index()
    outs_g, outs_d, outs_m, outs_v = {}, {}, {}, {}
    for n in BIG:
        res = _adamw_halves(_stored(n, wts[n]), _stored(n, mom[n]), _stored(n, vel[n]), *halves[n], n, core)
        outs_g[n], outs_d[n], outs_m[n], outs_v[n] = [_unstored(n, r) for r in res]
    shapes = {n: wts[n].shape for n in SMALL}
    g_tile = _pack_small({n: grads[n] for n in SMALL}, loss=loss[0, 0])
    gsum, d_tile, m_tile, v_tile = _small_allreduce_adamw(g_tile, _pack_small(wts), _pack_small(mom), _pack_small(vel))
    outs_g.update(_unpack_small(gsum, shapes))
    outs_d.update(_unpack_small(d_tile, shapes))
    outs_m.update(_unpack_small(m_tile, shapes))
    outs_v.update(_unpack_small(v_tile, shapes))
    total_loss = gsum[LOSS_SLOT[0], LOSS_SLOT[1]]
    return (total_loss, grad_x[None], *[outs_g[n] for n in names], *[outs_d[n] for n in names],
            *[outs_m[n] for n in names], *[outs_v[n] for n in names])
```

```python
import functools
import math

import jax
import jax.numpy as jnp
from jax import lax
from jax.experimental import pallas as pl
from jax.experimental.pallas import tpu as pltpu
from jax.experimental.pallas import tpu_sc as plsc

F32 = jnp.float32
BF16 = jnp.bfloat16

D_MODEL = 1024
M_HEADS = 4
M_HEAD_DIM = 128
M_WIDTH = M_HEADS * M_HEAD_DIM
M_CHUNK = 64
A_HEADS = 8
A_NOPE = 64
A_ROPE = 32
A_V = 64
A_WIDTH = A_HEADS * A_V
A_QK_PAD = 128
Q_RANK = 384
KV_RANK = 256
ROPE_THETA = 10000.0
D_FF = 2816
D_IN = 2728
EPS = 1e-6
ATTN_SCALE = (A_NOPE + A_ROPE) ** -0.5
M_SCALE = M_HEAD_DIM ** -0.5

ADAM_LR = 0.001
ADAM_B1 = 0.9
ADAM_B2 = 0.999
ADAM_EPS = 1e-08
ADAM_WD = 0.01
ADAM_STEP = 10

VMEM_LIMIT_BYTES = 56 * 1024 * 1024
LANES = 128
NEG_INF = float("-inf")


def _cparams(*sem):
    return pltpu.CompilerParams(dimension_semantics=sem if sem else None, vmem_limit_bytes=VMEM_LIMIT_BYTES)


_DIMS = {"nn": (((1,), (0,)), ((), ())), "nt": (((1,), (1,)), ((), ())), "tn": (((0,), (0,)), ((), ()))}


def _matmul(a, b, mode, out_dtype, name, tm=None, tn=1024, residual=None, b_rows=None, lead_scale=None,
            also_bf16=False):
    a_list = list(a) if isinstance(a, (list, tuple)) else [a]
    b_list = list(b) if isinstance(b, (list, tuple)) else [b]
    rows_list = list(b_rows) if isinstance(b_rows, (list, tuple)) else [b_rows] * len(b_list)
    assert len(a_list) == len(b_list) == len(rows_list)
    dims = _DIMS[mode]
    specs, m, n = [], None, None
    for aa, bb, rr in zip(a_list, b_list, rows_list):
        b_shape = bb.shape if rr is None else (rr, bb.shape[1])
        if mode == "nn":
            (m1, k), (k2, n1) = aa.shape, b_shape
        elif mode == "nt":
            (m1, k), (n1, k2) = aa.shape, b_shape
        else:
            (k, m1), (k2, n1) = aa.shape, b_shape
        assert k == k2 and (m is None or (m, n) == (m1, n1)), (aa.shape, bb.shape, mode)
        m, n = m1, n1
        specs.append(k)
    if tm is None:
        tm = 512 if mode == "tn" else 1024
    tm, tn = min(tm, m), min(tn, n)
    assert m % tm == 0 and n % tn == 0, (m, n, tm, tn)
    in_specs = []
    for k in specs:
        in_specs.append(pl.BlockSpec((k, tm), lambda i, j: (0, i)) if mode == "tn"
                        else pl.BlockSpec((tm, k), lambda i, j: (i, 0)))
        in_specs.append(pl.BlockSpec((tn, k), lambda i, j: (j, 0)) if mode == "nt"
                        else pl.BlockSpec((k, tn), lambda i, j: (0, j)))
    o_spec = pl.BlockSpec((tm, tn), lambda i, j: (i, j))
    n_pairs = len(specs)

    def body(*refs):
        acc = None
        for p in range(n_pairs):
            part = lax.dot_general(refs[2 * p][...].astype(BF16), refs[2 * p + 1][...].astype(BF16), dims,
                                   preferred_element_type=F32)
            acc = part if acc is None else acc + part
        if lead_scale is not None:
            assert lead_scale[0] % tn == 0
            acc = acc * jnp.where(pl.program_id(1) < lead_scale[0] // tn, lead_scale[1], 1.0)
        if residual is not None:
            acc = acc + refs[2 * n_pairs][...].astype(F32)
        outs = refs[2 * n_pairs + (residual is not None):]
        outs[0][...] = acc.astype(outs[0].dtype)
        if also_bf16:
            outs[1][...] = acc.astype(BF16)

    ins = [x for pair in zip(a_list, b_list) for x in pair] + ([residual] if residual is not None else [])
    in_specs = in_specs + ([o_spec] if residual is not None else [])
    out_shape = jax.ShapeDtypeStruct((m, n), out_dtype)
    if also_bf16:
        return pl.pallas_call(
            body, name=name, grid=(m // tm, n // tn), in_specs=in_specs, out_specs=[o_spec, o_spec],
            out_shape=[out_shape, jax.ShapeDtypeStruct((m, n), BF16)],
            compiler_params=_cparams("parallel", "parallel"),
        )(*ins)
    return pl.pallas_call(
        body, name=name, grid=(m // tm, n // tn), in_specs=in_specs, out_specs=o_spec, out_shape=out_shape,
        compiler_params=_cparams("parallel", "parallel"),
    )(*ins)


def _rmsnorm_fwd(x, w, name, tm=512):
    t, d = x.shape

    def body(x_ref, w_ref, o_ref):
        xf = x_ref[...].astype(F32)
        r = lax.rsqrt(jnp.mean(xf * xf, axis=-1, keepdims=True) + EPS)
        o_ref[...] = (xf * r * w_ref[...]).astype(o_ref.dtype)

    return pl.pallas_call(
        body, name=name, grid=(t // tm,),
        in_specs=[pl.BlockSpec((tm, d), lambda i: (i, 0)), pl.BlockSpec((1, d), lambda i: (0, 0))],
        out_specs=pl.BlockSpec((tm, d), lambda i: (i, 0)),
        out_shape=jax.ShapeDtypeStruct((t, d), BF16), compiler_params=_cparams("parallel"),
    )(x, w)


def _rmsnorm_bwd(dy, x, w, name, out_dtype, residual=None, also_bf16=False, tm=512):
    t, d = x.shape

    def body(dy_ref, x_ref, w_ref, *rest):
        dx_ref, dw_ref = rest[-3 if also_bf16 else -2], rest[-1]
        xf = x_ref[...].astype(F32)
        r = lax.rsqrt(jnp.mean(xf * xf, axis=-1, keepdims=True) + EPS)
        xh = xf * r
        dyf = dy_ref[...].astype(F32)
        dyh = dyf * w_ref[...]
        dx = r * (dyh - xh * jnp.mean(dyh * xh, axis=-1, keepdims=True))
        if residual is not None:
            dx = dx + rest[0][...].astype(F32)
        dx_ref[...] = dx.astype(dx_ref.dtype)
        if also_bf16:
            rest[-2][...] = dx.astype(BF16)
        part = jnp.sum(dyf * xh, axis=0, keepdims=True)

        @pl.when(pl.program_id(0) == 0)
        def _():
            dw_ref[...] = part

        @pl.when(pl.program_id(0) > 0)
        def _():
            dw_ref[...] += part

    row = pl.BlockSpec((tm, d), lambda i: (i, 0))
    vec = pl.BlockSpec((1, d), lambda i: (0, 0))
    ins = [dy, x, w] + ([residual] if residual is not None else [])
    extra = [jax.ShapeDtypeStruct((t, d), BF16)] if also_bf16 else []
    return pl.pallas_call(
        body, name=name, grid=(t // tm,), in_specs=[row, row, vec] + ([row] if residual is not None else []),
        out_specs=[row] + [row] * len(extra) + [vec],
        out_shape=[jax.ShapeDtypeStruct((t, d), out_dtype)] + extra + [jax.ShapeDtypeStruct((1, d), F32)],
        compiler_params=_cparams("arbitrary"),
    )(*ins)


def _final_loss(h, target, w, name, tm=512):
    t, d = h.shape

    def body(h_ref, t_ref, w_ref, dh_ref, dhb_ref, loss_ref, dw_ref):
        xf = h_ref[...]
        r = lax.rsqrt(jnp.mean(xf * xf, axis=-1, keepdims=True) + EPS)
        xh = xf * r
        err = xh * w_ref[...] - t_ref[...]
        part_loss = 0.5 * jnp.sum(jnp.sum(err * err, axis=-1, keepdims=True), axis=0, keepdims=True) * (1.0 / d)
        dy = err * (1.0 / d)
        dyh = dy * w_ref[...]
        dh = r * (dyh - xh * jnp.mean(dyh * xh, axis=-1, keepdims=True))
        dh_ref[...] = dh
        dhb_ref[...] = dh.astype(BF16)
        part_dw = jnp.sum(dy * xh, axis=0, keepdims=True)
        part_loss = jnp.broadcast_to(part_loss, (1, LANES))

        @pl.when(pl.program_id(0) == 0)
        def _():
            dw_ref[...] = part_dw
            loss_ref[...] = part_loss

        @pl.when(pl.program_id(0) > 0)
        def _():
            dw_ref[...] += part_dw
            loss_ref[...] += part_loss

    row = pl.BlockSpec((tm, d), lambda i: (i, 0))
    vec = pl.BlockSpec((1, d), lambda i: (0, 0))
    return pl.pallas_call(
        body, name=name, grid=(t // tm,), in_specs=[row, row, vec],
        out_specs=[row, row, pl.BlockSpec((1, LANES), lambda i: (0, 0)), vec],
        out_shape=[jax.ShapeDtypeStruct((t, d), F32), jax.ShapeDtypeStruct((t, d), BF16),
                   jax.ShapeDtypeStruct((1, LANES), F32), jax.ShapeDtypeStruct((1, d), F32)],
        compiler_params=_cparams("arbitrary"),
    )(h, target, w)


FFN_TN = 1408


def _ffn_gate_up(u, w_gate, w_up, name, tm=1024):
    t, d = u.shape
    n = w_gate.shape[0]
    tm = min(tm, t)

    def body(u_ref, wg_ref, wu_ref, g_ref, up_ref, act_ref):
        uu = u_ref[...]
        g = lax.dot_general(uu, wg_ref[...], _DIMS["nt"], preferred_element_type=F32)
        up = lax.dot_general(uu, wu_ref[...], _DIMS["nt"], preferred_element_type=F32)
        g_b, up_b = g.astype(BF16), up.astype(BF16)
        g_ref[...] = g_b
        up_ref[...] = up_b
        gf, uf = g_b.astype(F32), up_b.astype(F32)
        act_ref[...] = (gf * jax.nn.sigmoid(gf) * uf).astype(act_ref.dtype)

    w_spec = pl.BlockSpec((FFN_TN, d), lambda i, j: (j, 0))
    o_spec = pl.BlockSpec((tm, FFN_TN), lambda i, j: (i, j))
    out = jax.ShapeDtypeStruct((t, n), BF16)
    return pl.pallas_call(
        body, name=name, grid=(t // tm, n // FFN_TN),
        in_specs=[pl.BlockSpec((tm, d), lambda i, j: (i, 0)), w_spec, w_spec], out_specs=[o_spec] * 3,
        out_shape=[out] * 3, compiler_params=_cparams("parallel", "parallel"),
    )(u, w_gate, w_up)


def _ffn_down_bwd(dh, w_down, g, up, name, tm=1024):
    t, d = dh.shape
    n = w_down.shape[0]
    tm = min(tm, t)

    def body(dh_ref, w_ref, g_ref, up_ref, dg_ref, du_ref):
        df = lax.dot_general(dh_ref[...], w_ref[...], _DIMS["nt"], preferred_element_type=F32).astype(BF16).astype(F32)
        gf = g_ref[...].astype(F32)
        uf = up_ref[...].astype(F32)
        s = jax.nn.sigmoid(gf)
        dg_ref[...] = (df * uf * s * (1.0 + gf * (1.0 - s))).astype(dg_ref.dtype)
        du_ref[...] = (df * gf * s).astype(du_ref.dtype)

    o_spec = pl.BlockSpec((tm, FFN_TN), lambda i, j: (i, j))
    out = jax.ShapeDtypeStruct((t, n), BF16)
    return pl.pallas_call(
        body, name=name, grid=(t // tm, n // FFN_TN),
        in_specs=[pl.BlockSpec((tm, d), lambda i, j: (i, 0)), pl.BlockSpec((FFN_TN, d), lambda i, j: (j, 0)),
                  o_spec, o_spec],
        out_specs=[o_spec, o_spec], out_shape=[out, out], compiler_params=_cparams("parallel", "parallel"),
    )(dh, w_down, g, up)


M_BLOCK = 512
M_CHUNKS_PER_BLOCK = M_BLOCK // M_CHUNK
M_UNROLL = 2


def _log_sigmoid(z):
    return jnp.minimum(z, 0.0) - jnp.log(1.0 + jnp.exp(-jnp.abs(z)))


def _per_head(fn):
    return jnp.stack([fn(hd) for hd in range(M_HEADS)])


def _mlstm_chunk_fwd(q, k, v, gc, c_prev, n_prev, m_prev):
    L, H = M_CHUNK, M_HEADS
    tt = lax.broadcasted_iota(jnp.int32, (L, L), 0)
    ss = lax.broadcasted_iota(jnp.int32, (L, L), 1)
    eye = tt == ss
    causal = ss <= tt
    gct = jnp.transpose(gc)
    i_col = _per_head(lambda hd: gc[:, hd:hd + 1])
    f_col = _per_head(lambda hd: gc[:, H + hd:H + hd + 1])
    i_row = _per_head(lambda hd: gct[hd:hd + 1, :])
    lf_col = _log_sigmoid(f_col)
    lf_row = _log_sigmoid(_per_head(lambda hd: gct[H + hd:H + hd + 1, :]))
    b_col = jnp.sum(jnp.where(causal, lf_row, 0.0), axis=2, keepdims=True)
    b_row = jnp.sum(jnp.where(tt <= ss, lf_col, 0.0), axis=1, keepdims=True)
    g = jnp.sum(lf_col, axis=1, keepdims=True)
    a_row = g - b_row + i_row
    a_col = g - b_col + i_col
    m_loc = jnp.max(a_row, axis=2, keepdims=True)
    d_log = jnp.where(causal, b_col - b_row + i_row, NEG_INF)
    inter = b_col + m_prev
    m_t = jnp.maximum(jnp.max(d_log, axis=2, keepdims=True), inter)
    dmat = jnp.exp(d_log - m_t)
    amat = _per_head(lambda hd: lax.dot_general(q[hd], k[hd], _DIMS["nt"], preferred_element_type=F32))
    p = dmat * amat
    sig = jnp.exp(inter - m_t)
    c_prev_b = [c.astype(BF16) for c in c_prev]
    qc = _per_head(lambda hd: jnp.dot(q[hd], c_prev_b[hd], preferred_element_type=F32))
    p_b = p.astype(BF16)
    num = _per_head(lambda hd: jnp.dot(p_b[hd], v[hd], preferred_element_type=F32)) + sig * qc
    qf = _per_head(lambda hd: q[hd].astype(F32))
    qn = jnp.sum(qf * n_prev, axis=2, keepdims=True)
    den_raw = jnp.sum(p, axis=2, keepdims=True) + sig * qn
    floor = jnp.exp(-m_t)
    den = jnp.maximum(jnp.abs(den_raw), floor)
    h = num / den
    m_new = jnp.maximum(g + m_prev, m_loc)
    w_col = jnp.exp(a_col - m_new)
    w_row = jnp.exp(a_row - m_new)
    alpha = jnp.exp(g + m_prev - m_new)
    return dict(eye=eye, causal=causal, tt=tt, ss=ss, i_col=i_col, f_col=f_col, dmat=dmat, amat=amat, p=p, p_b=p_b,
                sig=sig, qc=qc, qf=qf, qn=qn, num=num, den_raw=den_raw, floor=floor, den=den, h=h, m_new=m_new,
                w_col=w_col, w_row=w_row, alpha=alpha, c_prev_b=c_prev_b, b_row=b_row, b_col=b_col, m_t=m_t)


def _chunk_major_t(xt):
    f, t = xt.shape
    return xt.reshape(f, t // M_CHUNK, M_CHUNK).transpose(1, 0, 2)


def _mlstm_fwd(m4, kt, gates, bias, norm_w, name):
    t = m4.shape[0]
    nblk = t // M_BLOCK
    nc = t // M_CHUNK
    L, dh = M_CHUNK, M_HEAD_DIM

    def body(m4_ref, kt_ref, g_ref, b_ref, w_ref, hm_ref, cp_ref, nm_ref, c_s, n_s, m_s):
        @pl.when(pl.program_id(0) == 0)
        def _():
            c_s[...] = jnp.zeros_like(c_s)
            n_s[...] = jnp.zeros_like(n_s)
            m_s[...] = jnp.zeros_like(m_s)

        row8 = lax.broadcasted_iota(jnp.int32, (8, dh), 0)

        def chunk(c, carry):
            rows = pl.ds(pl.multiple_of(c * L, L), L)
            gc = g_ref[rows, :] + b_ref[...]
            col = lambda part, hd: m4_ref[rows, part * M_WIDTH + hd * dh:part * M_WIDTH + (hd + 1) * dh]
            q = [col(0, hd) for hd in range(M_HEADS)]
            k = [col(1, hd) for hd in range(M_HEADS)]
            v = [col(2, hd) for hd in range(M_HEADS)]
            o = _per_head(lambda hd: col(3, hd).astype(F32))
            c_prev = [c_s[hd] for hd in range(M_HEADS)]
            n_prev = n_s[:, 0:1, :]
            m_prev = m_s[:, 0:1, 0:1]
            r = _mlstm_chunk_fwd(q, k, v, gc, c_prev, n_prev, m_prev)
            h = r["h"]
            hn = h * lax.rsqrt(jnp.mean(h * h, axis=-1, keepdims=True) + EPS) * w_ref[...]
            hm = (hn * jax.nn.sigmoid(o)).astype(hm_ref.dtype)
            nm = jnp.where(row8 == 0, n_prev, jnp.where(row8 == 1, m_prev, 0.0))
            kf = _per_head(lambda hd: k[hd].astype(F32))
            n_s[:, 0:1, :] = r["alpha"] * n_prev + jnp.sum(kf * r["w_col"], axis=1, keepdims=True)
            m_s[...] = jnp.broadcast_to(r["m_new"], (M_HEADS, 8, dh))
            for hd in range(M_HEADS):
                hm_ref[rows, hd * dh:(hd + 1) * dh] = hm[hd]
                cp_ref[hd, c] = r["c_prev_b"][hd]
                nm_ref[hd, c] = nm[hd]
                kw_t = (kt_ref[c, hd * dh:(hd + 1) * dh, :].astype(F32) * r["w_row"][hd]).astype(BF16)
                c_s[hd] = r["alpha"][hd] * c_prev[hd] + jnp.dot(kw_t, v[hd], preferred_element_type=F32)
            return carry

        lax.fori_loop(0, M_CHUNKS_PER_BLOCK, chunk, 0)

    return pl.pallas_call(
        body, name=name, grid=(nblk,),
        in_specs=[pl.BlockSpec((M_BLOCK, 4 * M_WIDTH), lambda i: (i, 0)),
                  pl.BlockSpec((M_CHUNKS_PER_BLOCK, M_WIDTH, L), lambda i: (i, 0, 0)),
                  pl.BlockSpec((M_BLOCK, LANES), lambda i: (i, 0)),
                  pl.BlockSpec((1, LANES), lambda i: (0, 0)),
                  pl.BlockSpec((M_HEADS, 1, dh), lambda i: (0, 0, 0))],
        out_specs=[pl.BlockSpec((M_BLOCK, M_WIDTH), lambda i: (i, 0)),
                   pl.BlockSpec((M_HEADS, M_CHUNKS_PER_BLOCK, dh, dh), lambda i: (0, i, 0, 0)),
                   pl.BlockSpec((M_HEADS, M_CHUNKS_PER_BLOCK, 8, dh), lambda i: (0, i, 0, 0))],
        out_shape=[jax.ShapeDtypeStruct((t, M_WIDTH), BF16),
                   jax.ShapeDtypeStruct((M_HEADS, nc, dh, dh), BF16),
                   jax.ShapeDtypeStruct((M_HEADS, nc, 8, dh), F32)],
        scratch_shapes=[pltpu.VMEM((M_HEADS, dh, dh), F32), pltpu.VMEM((M_HEADS, 8, dh), F32),
                        pltpu.VMEM((M_HEADS, 8, dh), F32)],
        compiler_params=_cparams("arbitrary"),
    )(m4, kt, gates, bias, norm_w)


def _mlstm_bwd(m4, qt, gates, bias, norm_w, c_prev_all, nm_all, dhm, name):
    t = m4.shape[0]
    nblk = t // M_BLOCK
    L, dh = M_CHUNK, M_HEAD_DIM

    def body(m4_ref, qt_ref, g_ref, b_ref, w_ref, cp_ref, nm_ref, dhm_ref, dm4_ref, dg_ref, small_ref, dc_s, dn_s):
        @pl.when(pl.program_id(0) == 0)
        def _():
            dc_s[...] = jnp.zeros_like(dc_s)
            dn_s[...] = jnp.zeros_like(dn_s)
            small_ref[...] = jnp.zeros_like(small_ref)

        lane = lax.broadcasted_iota(jnp.int32, (L, LANES), 1)
        row8 = lax.broadcasted_iota(jnp.int32, (8, LANES), 0)

        def chunk(ci, carry):
            c = M_CHUNKS_PER_BLOCK - 1 - ci
            rows = pl.ds(pl.multiple_of(c * L, L), L)
            gc = g_ref[rows, :] + b_ref[...]
            heads = range(M_HEADS)
            col = lambda part, hd: m4_ref[rows, part * M_WIDTH + hd * dh:part * M_WIDTH + (hd + 1) * dh]
            q = [col(0, hd) for hd in heads]
            k = [col(1, hd) for hd in heads]
            v = [col(2, hd) for hd in heads]
            o = _per_head(lambda hd: col(3, hd).astype(F32))
            q_t = [qt_ref[c, hd * dh:(hd + 1) * dh, :] for hd in heads]
            c_prev = [cp_ref[hd, c].astype(F32) for hd in heads]
            nm = _per_head(lambda hd: nm_ref[hd, c])
            n_prev = nm[:, 0:1, :]
            m_prev = nm[:, 1:2, 0:1]
            r = _mlstm_chunk_fwd(q, k, v, gc, c_prev, n_prev, m_prev)
            eye, tt, ss = r["eye"], r["tt"], r["ss"]
            h, den, sig, p = r["h"], r["den"], r["sig"], r["p"]
            w_col, alpha, c_prev_b = r["w_col"], r["alpha"], r["c_prev_b"]
            to_row = lambda colv: jnp.sum(jnp.where(eye, colv, 0.0), axis=1, keepdims=True)
            to_col = lambda rowv: jnp.sum(jnp.where(eye, rowv, 0.0), axis=2, keepdims=True)
            mm = lambda fn: _per_head(lambda hd: fn(hd))

            w_h = w_ref[...]
            rn = lax.rsqrt(jnp.mean(h * h, axis=-1, keepdims=True) + EPS)
            hh = h * rn
            og = jax.nn.sigmoid(o)
            dhm_c = _per_head(lambda hd: dhm_ref[rows, hd * dh:(hd + 1) * dh].astype(F32))
            dhn = dhm_c * og
            d_o = dhm_c * hh * w_h * og * (1.0 - og)
            d_norm = jnp.sum(dhn * hh, axis=1, keepdims=True)
            dhh = dhn * w_h
            dh_ = rn * (dhh - hh * jnp.mean(dhh * hh, axis=-1, keepdims=True))

            dnum = dh_ / den
            dden = -jnp.sum(dh_ * h, axis=-1, keepdims=True) / den
            dden_raw = jnp.where(jnp.abs(r["den_raw"]) >= r["floor"], dden * jnp.sign(r["den_raw"]), 0.0)
            dnum_b = dnum.astype(BF16)
            dp = mm(lambda hd: lax.dot_general(dnum_b[hd], v[hd], _DIMS["nt"], preferred_element_type=F32)) + dden_raw
            dc = [dc_s[hd] for hd in heads]
            dn = dn_s[:, 0:1, :]
            dc_b = [d.astype(BF16) for d in dc]
            g2_b = (sig * dnum).astype(BF16)
            sd = sig * dden_raw
            da_mat = (dp * r["dmat"]).astype(BF16)
            dqs = mm(lambda hd: lax.dot_general(g2_b[hd], c_prev_b[hd], _DIMS["nt"], preferred_element_type=F32)
                     + jnp.dot(da_mat[hd], k[hd], preferred_element_type=F32)) + sd * n_prev
            r_mat = mm(lambda hd: lax.dot_general(v[hd], dc_b[hd], _DIMS["nt"], preferred_element_type=F32)) + dn
            kf = _per_head(lambda hd: k[hd].astype(F32))
            dmat_t = jnp.exp(jnp.where(tt <= ss, r["b_row"] - r["b_col"] + r["i_col"], NEG_INF) - to_row(r["m_t"]))
            p_t = (dmat_t * mm(lambda hd: jnp.dot(k[hd], q_t[hd], preferred_element_type=F32))).astype(BF16)
            dp_t = mm(lambda hd: lax.dot_general(v[hd], dnum_b[hd], _DIMS["nt"], preferred_element_type=F32))
            da_t = ((dp_t + to_row(dden_raw)) * dmat_t).astype(BF16)
            d_k = mm(lambda hd: jnp.dot(da_t[hd], q[hd], preferred_element_type=F32)) + w_col * r_mat
            d_v = (mm(lambda hd: jnp.dot(p_t[hd], dnum_b[hd], preferred_element_type=F32))
                   + w_col * mm(lambda hd: jnp.dot(k[hd], dc_b[hd], preferred_element_type=F32)))
            da_col = jnp.sum(kf * r_mat, axis=-1, keepdims=True) * w_col
            dsig = jnp.sum(dnum * r["qc"], axis=-1, keepdims=True) + dden_raw * r["qn"]
            pp = dp * p
            r1 = jnp.sum(pp, axis=2, keepdims=True)
            c1_col = to_col(jnp.sum(pp, axis=1, keepdims=True))
            dcc = _per_head(lambda hd: dc[hd] * c_prev[hd])
            d_alpha = (jnp.sum(jnp.sum(dcc, axis=2, keepdims=True), axis=1, keepdims=True)
                       + jnp.sum(dn * n_prev, axis=2, keepdims=True))
            dgl = jnp.sum(da_col, axis=1, keepdims=True) + d_alpha * alpha
            db_col = r1 + dsig * sig - c1_col - da_col
            dli_col = c1_col + da_col
            dlf_col = jnp.sum(jnp.where(ss >= tt, to_row(db_col), 0.0), axis=2, keepdims=True) + dgl
            df_col = dlf_col * jax.nn.sigmoid(-r["f_col"])

            dn_s[:, 0:1, :] = alpha * dn + jnp.sum(sd * r["qf"], axis=1, keepdims=True)
            dq_out = (dqs * M_SCALE).astype(dm4_ref.dtype)
            dk_out, dv_out, do_out = d_k.astype(dm4_ref.dtype), d_v.astype(dm4_ref.dtype), d_o.astype(dm4_ref.dtype)
            dg_tile = jnp.zeros((L, LANES), F32)
            small = jnp.zeros((8, LANES), F32)
            for hd in heads:
                dc_s[hd] = alpha[hd] * dc[hd] + jnp.dot(q_t[hd], g2_b[hd], preferred_element_type=F32)
                dm4_ref[rows, hd * dh:(hd + 1) * dh] = dq_out[hd]
                dm4_ref[rows, M_WIDTH + hd * dh:M_WIDTH + (hd + 1) * dh] = dk_out[hd]
                dm4_ref[rows, 2 * M_WIDTH + hd * dh:2 * M_WIDTH + (hd + 1) * dh] = dv_out[hd]
                dm4_ref[rows, 3 * M_WIDTH + hd * dh:3 * M_WIDTH + (hd + 1) * dh] = do_out[hd]
                dg_tile = (dg_tile + jnp.where(lane == hd, dli_col[hd], 0.0)
                           + jnp.where(lane == M_HEADS + hd, df_col[hd], 0.0))
                small = small + jnp.where(row8 == hd, d_norm[hd], 0.0)
            dg_ref[rows, :] = dg_tile
            small = small + jnp.where(row8 == M_HEADS, jnp.sum(dg_tile, axis=0, keepdims=True), 0.0)
            small_ref[...] += small
            return carry

        lax.fori_loop(0, M_CHUNKS_PER_BLOCK, chunk, 0, unroll=M_UNROLL)

    rev = lambda i: nblk - 1 - i
    return pl.pallas_call(
        body, name=name, grid=(nblk,),
        in_specs=[pl.BlockSpec((M_BLOCK, 4 * M_WIDTH), lambda i: (rev(i), 0)),
                  pl.BlockSpec((M_CHUNKS_PER_BLOCK, M_WIDTH, L), lambda i: (rev(i), 0, 0)),
                  pl.BlockSpec((M_BLOCK, LANES), lambda i: (rev(i), 0)),
                  pl.BlockSpec((1, LANES), lambda i: (0, 0)),
                  pl.BlockSpec((M_HEADS, 1, dh), lambda i: (0, 0, 0)),
                  pl.BlockSpec((M_HEADS, M_CHUNKS_PER_BLOCK, dh, dh), lambda i: (0, rev(i), 0, 0)),
                  pl.BlockSpec((M_HEADS, M_CHUNKS_PER_BLOCK, 8, dh), lambda i: (0, rev(i), 0, 0)),
                  pl.BlockSpec((M_BLOCK, M_WIDTH), lambda i: (rev(i), 0))],
        out_specs=[pl.BlockSpec((M_BLOCK, 4 * M_WIDTH), lambda i: (rev(i), 0)),
                   pl.BlockSpec((M_BLOCK, LANES), lambda i: (rev(i), 0)),
                   pl.BlockSpec((8, LANES), lambda i: (0, 0))],
        out_shape=[jax.ShapeDtypeStruct((t, 4 * M_WIDTH), BF16), jax.ShapeDtypeStruct((t, LANES), F32),
                   jax.ShapeDtypeStruct((8, LANES), F32)],
        scratch_shapes=[pltpu.VMEM((M_HEADS, dh, dh), F32), pltpu.VMEM((M_HEADS, 8, dh), F32)],
        compiler_params=_cparams("arbitrary"),
    )(m4, qt, gates, bias, norm_w, c_prev_all, nm_all, dhm)


def _rope_tables(pos_col, width):
    lane = lax.broadcasted_iota(jnp.int32, (1, width), 1) % A_QK_PAD
    half = A_ROPE // 2
    first = (lane >= A_NOPE) & (lane < A_NOPE + half)
    second = (lane >= A_NOPE + half) & (lane < A_NOPE + A_ROPE)
    idx = jnp.where(first, lane - A_NOPE, lane - A_NOPE - half).astype(F32)
    inv_freq = jnp.exp(idx * (-math.log(ROPE_THETA) / half))
    ang = pos_col.astype(F32) * inv_freq
    cos, sin = jnp.cos(ang), jnp.sin(ang)
    rot = first | second
    return jnp.where(rot, cos, 1.0), jnp.where(first, -sin, 0.0), jnp.where(second, sin, 0.0)


def _rope_apply(vv, cosf, s1, s2):
    half = A_ROPE // 2
    w = vv.shape[-1]
    return vv * cosf + pltpu.roll(vv, w - half, 1) * s1 + pltpu.roll(vv, half, 1) * s2


def _rope_apply_t(dd, cosf, s1, s2):
    half = A_ROPE // 2
    w = dd.shape[-1]
    return dd * cosf + pltpu.roll(dd * s1, half, 1) + pltpu.roll(dd * s2, w - half, 1)


A_BIAS_LANE_K = A_NOPE + A_ROPE
A_BIAS_LANE_V = A_V


def _hi_lo(val):
    hi = val.astype(BF16)
    return hi, (val - hi.astype(F32)).astype(BF16)


def _qkv_up_rope(qa_n, kv_n, w_qb, w_k, w_v, gates, pos, name, tm=512):
    t = qa_n.shape[0]
    w = A_HEADS * LANES

    def body(qa_ref, kv_ref, wq_ref, wk_ref, wv_ref, g_ref, p_ref, qo_ref, ko_ref, vo_ref):
        cosf, s1, s2 = _rope_tables(p_ref[...], LANES)
        lane = lax.broadcasted_iota(jnp.int32, (1, LANES), 1)
        kr = jnp.where((lane >= A_NOPE) & (lane < A_NOPE + A_ROPE), g_ref[...], 0.0)
        kr = _rope_apply(kr, cosf, s1, s2)
        kr = jnp.where((lane == A_BIAS_LANE_K) | (lane == A_BIAS_LANE_K + 1), 1.0, kr)
        v_one = (lane == A_BIAS_LANE_V) | (lane == A_BIAS_LANE_V + 1)
        qpre = lax.dot_general(qa_ref[...], wq_ref[...], _DIMS["nt"], preferred_element_type=F32)
        kv = kv_ref[...]
        kpre = lax.dot_general(kv, wk_ref[...], _DIMS["nt"], preferred_element_type=F32)
        vpre = lax.dot_general(kv, wv_ref[...], _DIMS["nt"], preferred_element_type=F32)
        for hd in range(A_HEADS):
            sl = slice(hd * LANES, (hd + 1) * LANES)
            qo_ref[:, sl] = (_rope_apply(qpre[:, sl], cosf, s1, s2) * ATTN_SCALE).astype(qo_ref.dtype)
            ko_ref[:, sl] = (kpre[:, sl] + kr).astype(ko_ref.dtype)
            vo_ref[:, sl] = jnp.where(v_one, 1.0, vpre[:, sl]).astype(vo_ref.dtype)

    rows = lambda width: pl.BlockSpec((tm, width), lambda i: (i, 0))
    whole = lambda arr: pl.BlockSpec(arr.shape, lambda i: (0, 0))
    out = jax.ShapeDtypeStruct((t, w), BF16)
    return pl.pallas_call(
        body, name=name, grid=(t // tm,),
        in_specs=[rows(qa_n.shape[1]), rows(kv_n.shape[1]), whole(w_qb), whole(w_k), whole(w_v), rows(LANES), rows(1)],
        out_specs=[rows(w)] * 3, out_shape=[out] * 3, compiler_params=_cparams("parallel"),
    )(qa_n, kv_n, w_qb, w_k, w_v, gates, pos)


def _rope_bwd(dq, dk, dgates, pos, name, tm=512):
    t, w = dq.shape

    def body(dq_ref, dk_ref, dg_ref, p_ref, dqo_ref, dgo_ref):
        cosf, s1, s2 = _rope_tables(p_ref[...], LANES)
        acc = jnp.zeros((tm, LANES), F32)
        for hd in range(A_HEADS):
            sl = slice(hd * LANES, (hd + 1) * LANES)
            dqo_ref[:, sl] = (_rope_apply_t(dq_ref[:, sl], cosf, s1, s2) * ATTN_SCALE).astype(dqo_ref.dtype)
            acc = acc + dk_ref[:, sl].astype(F32)
        lane = lax.broadcasted_iota(jnp.int32, (1, LANES), 1)
        dkr = _rope_apply_t(acc, cosf, s1, s2)
        dkr = jnp.where((lane >= A_NOPE) & (lane < A_NOPE + A_ROPE), dkr, 0.0)
        dgo_ref[...] = (dg_ref[...] + dkr).astype(dgo_ref.dtype)

    row = pl.BlockSpec((tm, w), lambda i: (i, 0))
    nar = pl.BlockSpec((tm, LANES), lambda i: (i, 0))
    return pl.pallas_call(
        body, name=name, grid=(t // tm,),
        in_specs=[row, row, nar, pl.BlockSpec((tm, 1), lambda i: (i, 0))],
        out_specs=[row, nar],
        out_shape=[jax.ShapeDtypeStruct((t, w), BF16), jax.ShapeDtypeStruct((t, LANES), BF16)],
        compiler_params=_cparams("parallel"),
    )(dq, dk, dgates, pos)


A_TQ = 512
A_TK = 512


A_HEADS_PER_STEP = 2
A_HEADS_PER_STEP_FWD = 4


def _flash_fwd(q, k, v, name):
    t = q.shape[0]
    tq, tk = A_TQ, A_TK
    nh = A_HEADS_PER_STEP_FWD
    wblk = nh * LANES

    def body(q_ref, k_ref, v_ref, o_ref, qb_ref):
        i = pl.program_id(1)
        lane = lax.broadcasted_iota(jnp.int32, (tq, LANES), 1)
        qpos = i * tq + lax.broadcasted_iota(jnp.int32, (tq, tk), 0)
        kofs = lax.broadcasted_iota(jnp.int32, (tq, tk), 1)
        qs = [q_ref[:, hh * LANES:(hh + 1) * LANES] for hh in range(nh)]

        def step(kb, carry, masked):
            rows = pl.ds(pl.multiple_of(kb * tk, tk), tk)
            new = []
            for hh in range(nh):
                m, acc = carry[hh]
                kh = k_ref[rows, hh * LANES:(hh + 1) * LANES]
                s = lax.dot_general(qs[hh], kh, _DIMS["nt"], preferred_element_type=F32)
                if masked:
                    s = jnp.where(kb * tk + kofs <= qpos, s, NEG_INF)
                m_new = jnp.maximum(m, jnp.max(s, axis=1, keepdims=True))
                p = jnp.exp(s - m_new)
                acc = jnp.exp(m - m_new) * acc + jnp.dot(p.astype(BF16), v_ref[rows, hh * LANES:(hh + 1) * LANES],
                                                          preferred_element_type=F32)
                new.append((m_new, acc))
            return tuple(new)

        carry = tuple((jnp.full((tq, 1), NEG_INF, F32), jnp.zeros((tq, LANES), F32)) for _ in range(nh))
        n_full = (i * tq) // tk
        carry = lax.fori_loop(0, n_full, functools.partial(step, masked=False), carry)
        for d in range(tq // tk):
            carry = step(n_full + d, carry, True)
        for hh in range(nh):
            m, acc = carry[hh]
            l = acc[:, A_BIAS_LANE_V:A_BIAS_LANE_V + 1]
            o_ref[:, hh * LANES:(hh + 1) * LANES] = (acc / l).astype(o_ref.dtype)
            hi, lo = _hi_lo(-(m + jnp.log(l)))
            qb_ref[:, hh * LANES:(hh + 1) * LANES] = jnp.where(
                lane == A_BIAS_LANE_K, hi, jnp.where(lane == A_BIAS_LANE_K + 1, lo, qs[hh]))

    blk = pl.BlockSpec((tq, wblk), lambda j, i: (i, j))
    res = pl.BlockSpec((t, wblk), lambda j, i: (0, j))
    out = jax.ShapeDtypeStruct((t, A_HEADS * LANES), BF16)
    return pl.pallas_call(
        body, name=name, grid=(A_HEADS // nh, t // tq), in_specs=[blk, res, res], out_specs=[blk, blk],
        out_shape=[out, out], compiler_params=_cparams("parallel", "parallel"),
    )(q, k, v)


def _attn_delta(do, o, name, tm=512):
    t, w = do.shape

    def body(do_ref, o_ref, out_ref):
        lane = lax.broadcasted_iota(jnp.int32, (tm, LANES), 1)
        for hd in range(A_HEADS):
            sl = slice(hd * LANES, (hd + 1) * LANES)
            d = do_ref[:, sl]
            delta = jnp.sum(jnp.where(lane < A_V, d.astype(F32) * o_ref[:, sl].astype(F32), 0.0), axis=1, keepdims=True)
            hi, lo = _hi_lo(-delta)
            out_ref[:, sl] = jnp.where(lane == A_BIAS_LANE_V, hi, jnp.where(lane == A_BIAS_LANE_V + 1, lo, d))

    row = pl.BlockSpec((tm, w), lambda i: (i, 0))
    return pl.pallas_call(
        body, name=name, grid=(t // tm,), in_specs=[row, row], out_specs=row,
        out_shape=jax.ShapeDtypeStruct((t, w), BF16), compiler_params=_cparams("parallel"),
    )(do, o)


def _flash_bwd(qb, k, v, doe, name):
    t = qb.shape[0]
    tq, tk = A_TQ, A_TK
    assert tq == tk
    nh = A_HEADS_PER_STEP
    wblk = nh * LANES
    nq = t // tq

    def body(q_ref, k_ref, v_ref, do_ref, dq_ref, dk_ref, dv_ref):
        kb = pl.program_id(1)

        @pl.when(kb == 0)
        def _():
            dq_ref[...] = jnp.zeros_like(dq_ref)

        kpos = kb * tk + lax.broadcasted_iota(jnp.int32, (tk, tq), 0)
        qofs = lax.broadcasted_iota(jnp.int32, (tk, tq), 1)
        ks = [k_ref[:, hh * LANES:(hh + 1) * LANES] for hh in range(nh)]
        vs = [v_ref[:, hh * LANES:(hh + 1) * LANES] for hh in range(nh)]

        def step(qi, carry, masked):
            rows = pl.ds(pl.multiple_of(qi * tq, tq), tq)
            new = []
            for hh in range(nh):
                dk_acc, dv_acc = carry[hh]
                qh = q_ref[rows, hh * LANES:(hh + 1) * LANES]
                doh = do_ref[rows, hh * LANES:(hh + 1) * LANES]
                p_t = jnp.exp(lax.dot_general(ks[hh], qh, _DIMS["nt"], preferred_element_type=F32))
                if masked:
                    p_t = jnp.where(qi * tq + qofs >= kpos, p_t, 0.0)
                ds_t = (p_t * lax.dot_general(vs[hh], doh, _DIMS["nt"], preferred_element_type=F32)).astype(BF16)
                dv_acc = dv_acc + jnp.dot(p_t.astype(BF16), doh, preferred_element_type=F32)
                dk_acc = dk_acc + jnp.dot(ds_t, qh, preferred_element_type=F32)
                dq_ref[rows, hh * LANES:(hh + 1) * LANES] += lax.dot_general(ds_t, ks[hh], _DIMS["tn"],
                                                                             preferred_element_type=F32)
                new.append((dk_acc, dv_acc))
            return tuple(new)

        carry = tuple((jnp.zeros((tk, LANES), F32), jnp.zeros((tk, LANES), F32)) for _ in range(nh))
        carry = step(kb, carry, True)
        carry = lax.fori_loop(kb + 1, nq, functools.partial(step, masked=False), carry)
        for hh in range(nh):
            dk_ref[:, hh * LANES:(hh + 1) * LANES] = carry[hh][0].astype(dk_ref.dtype)
            dv_ref[:, hh * LANES:(hh + 1) * LANES] = carry[hh][1].astype(dv_ref.dtype)

    res = pl.BlockSpec((t, wblk), lambda j, kb: (0, j))
    blk = pl.BlockSpec((tk, wblk), lambda j, kb: (kb, j))
    wide = (t, A_HEADS * LANES)
    return pl.pallas_call(
        body, name=name, grid=(A_HEADS // nh, t // tk), in_specs=[res, blk, blk, res], out_specs=[res, blk, blk],
        out_shape=[jax.ShapeDtypeStruct(wide, F32), jax.ShapeDtypeStruct(wide, BF16), jax.ShapeDtypeStruct(wide, BF16)],
        compiler_params=_cparams("parallel", "arbitrary"),
    )(qb, k, v, doe)


_SPLITS = (M_WIDTH, M_WIDTH, M_WIDTH, M_WIDTH, M_HEADS, M_HEADS, Q_RANK, KV_RANK, A_ROPE)
_OFFS = tuple(sum(_SPLITS[:i]) for i in range(len(_SPLITS) + 1))
_GATE_BLOCK_KR = A_NOPE


def _split_w_in_t(wt):
    z = lambda n: jnp.zeros((n, wt.shape[1]), wt.dtype)
    w_g = jnp.concatenate([wt[_OFFS[4]:_OFFS[6]], z(_GATE_BLOCK_KR - 2 * M_HEADS), wt[_OFFS[8]:_OFFS[9]],
                           z(LANES - _GATE_BLOCK_KR - A_ROPE)], axis=0)
    return wt[_OFFS[6]:_OFFS[8]], w_g


def _merge_w_in_grad_t(g_m4, g_a2, g_g):
    return jnp.concatenate([g_m4, g_g[:2 * M_HEADS], g_a2, g_g[_GATE_BLOCK_KR:_GATE_BLOCK_KR + A_ROPE]], axis=0)


def _pad_heads(w, used):
    w3 = w.reshape(A_HEADS, used, w.shape[1])
    return jnp.pad(w3, ((0, 0), (0, LANES - used), (0, 0))).reshape(A_HEADS * LANES, w.shape[1])


def _unpad_heads(g, used):
    return g.reshape(A_HEADS, LANES, g.shape[1])[:, :used].reshape(A_HEADS * used, g.shape[1])


def _split_w_kv_b_t(wt):
    w3 = wt.reshape(A_HEADS, A_NOPE + A_V, wt.shape[1])
    pad = lambda part: jnp.pad(part, ((0, 0), (0, LANES - part.shape[1]), (0, 0))).reshape(A_HEADS * LANES, wt.shape[1])
    return pad(w3[:, :A_NOPE]), pad(w3[:, A_NOPE:])


def _merge_w_kv_b_grad_t(gk, gv):
    gk3 = gk.reshape(A_HEADS, LANES, gk.shape[1])[:, :A_NOPE]
    gv3 = gv.reshape(A_HEADS, LANES, gv.shape[1])[:, :A_V]
    return jnp.concatenate([gk3, gv3], axis=1).reshape(A_HEADS * (A_NOPE + A_V), gk.shape[1])


def _local_step(x, pos, target, nw, w, start_late_reduce):
    w_in = w["w_in"]
    w_a2, w_g = _split_w_in_t(w_in)
    w_qb = _pad_heads(w["w_q_b"], A_NOPE + A_ROPE)
    w_k, w_v = _split_w_kv_b_t(w["w_kv_b"])
    w_out_m, w_out_a = w["w_out"][:M_WIDTH], _pad_heads(w["w_out"][M_WIDTH:], A_V)
    w_gate, w_up, w_down = w["w_gate"], w["w_up"], w["w_down"]
    bias = jnp.pad(nw["b_gates"], ((0, 0), (0, LANES - 2 * M_HEADS)))
    mnorm = nw["mlstm_norm_w"].reshape(M_HEADS, 1, M_HEAD_DIM)
    n_m4 = 4 * M_WIDTH
    t_len = x.shape[0]

    u1 = _rmsnorm_fwd(x, nw["attn_norm_w"], "attn_norm")
    m4 = _matmul(u1, w_in, "nt", BF16, "proj_mlstm", tn=M_WIDTH, b_rows=n_m4, lead_scale=(M_WIDTH, M_SCALE))
    a2 = _matmul(u1, w_a2, "nt", BF16, "proj_latents", tn=640)
    gates = _matmul(u1, w_g, "nt", F32, "proj_gates")
    q_t = _chunk_major_t(_matmul(w_in[:M_WIDTH], u1, "nt", BF16, "proj_q_t", lead_scale=(t_len, M_SCALE)))
    k_t = _chunk_major_t(_matmul(w_in[M_WIDTH:2 * M_WIDTH], u1, "nt", BF16, "proj_k_t"))
    hm, c_prev_all, nm_all = _mlstm_fwd(m4, k_t, gates, bias, mnorm, "mlstm_fwd")
    qa, kva = a2[:, :Q_RANK], a2[:, Q_RANK:]
    qa_n = _rmsnorm_fwd(qa, nw["q_a_norm_w"], "q_a_norm")
    kv_n = _rmsnorm_fwd(kva, nw["kv_a_norm_w"], "kv_a_norm")
    qr, kr, vv = _qkv_up_rope(qa_n, kv_n, w_qb, w_k, w_v, gates, pos, "qkv_up_rope")
    ha, qb = _flash_fwd(qr, kr, vv, "attn_fwd")
    h1 = _matmul([hm, ha], [w_out_m, w_out_a], "nn", F32, "out_proj", residual=x)
    u2 = _rmsnorm_fwd(h1, nw["ffn_norm_w"], "ffn_norm")
    gg, uu, act = _ffn_gate_up(u2, w_gate, w_up, "ffn_gate_up")
    h2 = _matmul(act, w_down, "nn", F32, "ffn_down", residual=h1)

    dh2, dh2_b, loss, g_final = _final_loss(h2, target, nw["final_norm_w"].reshape(1, D_MODEL), "final_loss")
    grads = {"final_norm_w": g_final.reshape(D_MODEL)}
    grads_b = {}
    dgg, duu = _ffn_down_bwd(dh2_b, w_down, gg, uu, "ffn_down_bwd")
    grads["w_down"], grads_b["w_down"] = _matmul(act, dh2_b, "tn", F32, "g_w_down", tm=256, tn=D_MODEL, also_bf16=True)
    du2 = _matmul([dgg, duu], [w_gate, w_up], "nn", F32, "d_u2", tm=1024, tn=512)
    grads["w_gate"], grads_b["w_gate"] = _matmul(dgg, u2, "tn", F32, "g_w_gate", tm=256, tn=D_MODEL, also_bf16=True)
    grads["w_up"], grads_b["w_up"] = _matmul(duu, u2, "tn", F32, "g_w_up", tm=256, tn=D_MODEL, also_bf16=True)
    dh1, dh1_b, grads["ffn_norm_w"] = _rmsnorm_bwd(du2, h1, nw["ffn_norm_w"], "ffn_norm_bwd", F32, residual=dh2,
                                                   also_bf16=True)
    dhm = _matmul(dh1_b, w_out_m, "nt", BF16, "d_hm")
    dha = _matmul(dh1_b, w_out_a, "nt", BF16, "d_ha")
    grads["w_out"] = jnp.concatenate([_matmul(hm, dh1_b, "tn", F32, "g_w_out_m", tn=D_MODEL),
                                      _unpad_heads(_matmul(ha, dh1_b, "tn", F32, "g_w_out_a", tn=D_MODEL), A_V)], axis=0)

    late, dha = start_late_reduce(grads, grads_b, dha)

    dqr, dkr, dvv = _flash_bwd(qb, kr, vv, _attn_delta(dha, ha, "attn_delta"), "attn_bwd")
    dm4, dgates, small = _mlstm_bwd(m4, q_t, gates, bias, mnorm, c_prev_all, nm_all, dhm, "mlstm_bwd")
    grads["mlstm_norm_w"] = small[:M_HEADS].reshape(1, M_HEADS, M_HEAD_DIM)
    grads["b_gates"] = small[M_HEADS:M_HEADS + 1, :2 * M_HEADS]
    dqpre, dgk = _rope_bwd(dqr, dkr, dgates, pos, "rope_bwd")
    dqa_n = _matmul(dqpre, w_qb, "nn", BF16, "d_qa_n")
    grads["w_q_b"] = _unpad_heads(_matmul(dqpre, qa_n, "tn", F32, "g_w_q_b"), A_NOPE + A_ROPE)
    dkv_n = _matmul([dkr, dvv], [w_k, w_v], "nn", BF16, "d_kv_n")
    grads["w_kv_b"] = _merge_w_kv_b_grad_t(_matmul(dkr, kv_n, "tn", F32, "g_w_k"),
                                           _matmul(dvv, kv_n, "tn", F32, "g_w_v"))
    dqa, grads["q_a_norm_w"] = _rmsnorm_bwd(dqa_n, qa, nw["q_a_norm_w"], "q_a_norm_bwd", BF16)
    dkva, grads["kv_a_norm_w"] = _rmsnorm_bwd(dkv_n, kva, nw["kv_a_norm_w"], "kv_a_norm_bwd", BF16)
    da2 = jnp.concatenate([dqa, dkva], axis=1)
    du1 = _matmul([dm4, da2, dgk], [w_in, w_a2, w_g], "nn", F32, "d_u1", b_rows=[n_m4, None, None])
    grads["w_in"] = _merge_w_in_grad_t(_matmul(dm4, u1, "tn", F32, "g_w_m4", tn=D_MODEL),
                                       _matmul(da2, u1, "tn", F32, "g_w_a2", tm=640, tn=D_MODEL),
                                       _matmul(dgk, u1, "tn", F32, "g_w_g", tn=D_MODEL))
    grad_x, grads["attn_norm_w"] = _rmsnorm_bwd(du1, x, nw["attn_norm_w"], "attn_norm_bwd", F32, residual=dh1)
    return loss, grad_x, grads, grads_b, late


MESH = pl.DeviceIdType.MESH
N_CHIPS = 4
EARLY = ("w_in", "w_q_b", "w_kv_b")
LATE = ("w_out", "w_gate", "w_up", "w_down")
BIG = EARLY + LATE
TRANSPOSED = ("w_in", "w_q_b", "w_kv_b", "w_gate", "w_up")
LANE_HALVED = ("w_in", "w_out", "w_gate", "w_up", "w_down")
SMALL = ("attn_norm_w", "b_gates", "mlstm_norm_w", "q_a_norm_w", "kv_a_norm_w", "ffn_norm_w", "final_norm_w")
HBM_SPEC = pl.BlockSpec(memory_space=pltpu.HBM)


def _stored(name, a):
    return a[0].T if name in TRANSPOSED else a[0]


def _unstored(name, a):
    return (a.T if name in TRANSPOSED else a)[None]


def _half_shape(name, shape):
    rs, cs = shape
    return (rs, cs // 2) if name in LANE_HALVED else (rs // 2, cs)


def _half(ref, name, h, *lead):
    rs, cs = ref.shape[-2:]
    if name in LANE_HALVED:
        return ref.at[(*lead, slice(None), pl.ds(h * (cs // 2), cs // 2))]
    return ref.at[(*lead, pl.ds(h * (rs // 2), rs // 2), slice(None))]


def _place():
    x, y, c = lax.axis_index("x"), lax.axis_index("y"), lax.axis_index("c")
    others = [(1 - x, y), (x, 1 - y), (1 - x, 1 - y)]
    return x, y, c, others


def _gather_weights(names, shards):
    n = len(shards)

    def body(*refs):
        _gather_body(names, refs[:n], refs[n:2 * n], *refs[2 * n:])

    gathered = pl.pallas_call(
        body, name="gather_weights", in_specs=[HBM_SPEC] * n, out_specs=[HBM_SPEC] * n,
        out_shape=[jax.ShapeDtypeStruct((N_CHIPS,) + s.shape, s.dtype) for s in shards],
        scratch_shapes=[pltpu.SemaphoreType.DMA((6 * n,)), pltpu.SemaphoreType.DMA((6 * n,))],
    )(*shards)
    return _with_own_slab(gathered, shards)


def _with_own_slab(gathered, shards):
    me = 2 * lax.axis_index("x") + lax.axis_index("y")
    return [lax.dynamic_update_slice(g, s[None], (me, 0, 0)) for g, s in zip(gathered, shards)]


def _gather_body(names, ins, outs, send_sems, recv_sems):
    x, y, c, others = _place()
    me = 2 * x + y
    sibling = (x, y, 1 - c)

    def copy(w, k, slab, core, to, src=None):
        dst = _half(outs[w], names[w], core, slab)
        return pltpu.make_async_remote_copy(
            src_ref=dst if src is None else src, dst_ref=dst, send_sem=send_sems.at[w * 6 + k],
            recv_sem=recv_sems.at[w * 6 + k], device_id=to, device_id_type=MESH)

    sends = []
    for w in range(len(names)):
        for j, chip in enumerate(others):
            cp = copy(w, j, me, c, (*chip, c), src=_half(ins[w], names[w], c))
            cp.start()
            sends.append(cp)
    for w in range(len(names)):
        for j, (ox, oy) in enumerate(others):
            slab = 2 * ox + oy
            copy(w, j, slab, c, (x, y, c)).wait_recv()
            fwd = copy(w, 3 + j, slab, c, sibling)
            fwd.start()
            sends.append(fwd)
    for w in range(len(names)):
        for j, (ox, oy) in enumerate(others):
            copy(w, 3 + j, 2 * ox + oy, 1 - c, (x, y, c)).wait_recv()
    for cp in sends:
        cp.wait_send()


GATHER_LATE_COLLECTIVE_ID = 1


def _gather_weights_async(names, shards):
    n = len(shards)
    src = [jax.new_ref(s, memory_space=pltpu.MemorySpace.HBM) for s in shards]
    dst = [jax.empty_ref(jax.ShapeDtypeStruct((N_CHIPS,) + s.shape, s.dtype), memory_space=pltpu.MemorySpace.HBM)
           for s in shards]

    @pl.kernel(mesh=plsc.ScalarSubcoreMesh(axis_name="sequencer", num_cores=1), name="gather_weights_async",
               scratch_types=(pltpu.SemaphoreType.DMA((6 * n,)), pltpu.SemaphoreType.DMA((6 * n,))),
               compiler_params=pltpu.CompilerParams(collective_id=GATHER_LATE_COLLECTIVE_ID))
    def launch(send_sems, recv_sems):
        x, y, c, others = _place()
        peers = [(ox, oy, c) for ox, oy in others] + [(x, y, 1 - c)]
        barrier = pltpu.get_barrier_semaphore()
        for peer in peers:
            pl.semaphore_signal(barrier, inc=1, device_id=peer, device_id_type=MESH)
        pl.semaphore_wait(barrier, len(peers))
        _gather_body(names, src, dst, send_sems, recv_sems)

    launch()
    return _with_own_slab([d[...] for d in dst], shards)


def _exchange(arrays, out_shapes, plan, copies_per_array, name):
    n = len(arrays)

    def body(*refs):
        ins, outs = refs[:n], refs[n:2 * n]
        send_sems, recv_sems = refs[2 * n:]
        rem = [pltpu.make_async_remote_copy(src_ref=s, dst_ref=d, send_sem=send_sems.at[k], recv_sem=recv_sems.at[k],
                                            device_id=to, device_id_type=MESH)
               for k, (s, d, to) in enumerate(plan(ins, outs, _place()))]
        for cp in rem:
            cp.start()
        for cp in rem:
            cp.wait_recv()
        for cp in rem:
            cp.wait_send()

    return pl.pallas_call(
        body, name=name, in_specs=[HBM_SPEC] * n, out_specs=[HBM_SPEC] * n, out_shape=out_shapes,
        scratch_shapes=[pltpu.SemaphoreType.DMA((copies_per_array * n,)),
                        pltpu.SemaphoreType.DMA((copies_per_array * n,))],
    )(*arrays)


def _plan_to_sibling(ins, outs, place):
    x, y, c, _ = place
    return [(ins[w], outs[w], (x, y, 1 - c)) for w in range(len(ins))]


def _plan_sibling_halves(names, ins, outs, place):
    x, y, c, _ = place
    return [(_half(ins[w], names[w], 1 - c, slice(None)), outs[w], (x, y, 1 - c)) for w in range(len(ins))]


def _plan_chip_partials(ins, outs, place):
    x, y, c, others = place
    return [(ins[w].at[2 * ox + oy], outs[w].at[j], (ox, oy, c))
            for w in range(len(ins)) for j, (ox, oy) in enumerate(others)]


def _chip_partial(g3, got, name, core):
    hs = tuple(got.shape[1:])
    if name in LANE_HALVED:
        g_in, g_spec = g3, pl.BlockSpec((1,) + hs, lambda s, c_ref: (s, 0, c_ref[0]))
    else:
        g_in, g_spec = g3.reshape((N_CHIPS, 2) + hs), pl.BlockSpec((1, 1) + hs, lambda s, c_ref: (s, c_ref[0], 0, 0))
    blk = pl.BlockSpec((1,) + hs, lambda s, c_ref: (s, 0, 0))

    def body(c_ref, g_ref, r_ref, o_ref):
        o_ref[0] = (g_ref[(0,) * (len(g_ref.shape) - 2)] + r_ref[0].astype(F32)).astype(o_ref.dtype)

    return pl.pallas_call(
        body, name="chip_partial_%s" % name,
        grid_spec=pltpu.PrefetchScalarGridSpec(num_scalar_prefetch=1, grid=(N_CHIPS,), in_specs=[g_spec, blk],
                                               out_specs=blk),
        out_shape=jax.ShapeDtypeStruct((N_CHIPS,) + hs, BF16), compiler_params=_cparams("parallel"),
    )(core, g_in, got)


def _shard_sum(part, got3, name, me):
    hs = tuple(part.shape[1:])

    def body(me_ref, p_ref, a_ref, b_ref, c_ref, o_ref):
        o_ref[...] = ((p_ref[0].astype(F32) + a_ref[0].astype(F32)) + b_ref[0].astype(F32)) + c_ref[0].astype(F32)

    slot = lambda j: pl.BlockSpec((1,) + hs, lambda i, me_ref: (j, 0, 0))
    return pl.pallas_call(
        body, name="shard_sum_%s" % name,
        grid_spec=pltpu.PrefetchScalarGridSpec(
            num_scalar_prefetch=1, grid=(1,),
            in_specs=[pl.BlockSpec((1,) + hs, lambda i, me_ref: (me_ref[0], 0, 0)), slot(0), slot(1), slot(2)],
            out_specs=pl.BlockSpec(hs, lambda i, me_ref: (0, 0))),
        out_shape=jax.ShapeDtypeStruct(hs, F32), compiler_params=_cparams("arbitrary"),
    )(me, part, got3, got3, got3)


def _adamw_halves(w, m, v, mine, other, name, core):
    hs = tuple(mine.shape)
    full = pl.BlockSpec(hs, (lambda h, c_ref: (0, h)) if name in LANE_HALVED else (lambda h, c_ref: (h, 0)))
    half = pl.BlockSpec(hs, lambda h, c_ref: (0, 0))

    def body(c_ref, w_ref, m_ref, v_ref, a_ref, b_ref, g_ref, d_ref, mo_ref, vo_ref):
        g = jnp.where(pl.program_id(0) == c_ref[0], a_ref[...], b_ref[...])
        delta, m_new, v_new = _adamw_math(w_ref[...], g, m_ref[...], v_ref[...])
        g_ref[...] = g
        d_ref[...] = delta
        mo_ref[...] = m_new
        vo_ref[...] = v_new

    out = jax.ShapeDtypeStruct(w.shape, F32)
    return pl.pallas_call(
        body, name="adamw_%s" % name,
        grid_spec=pltpu.PrefetchScalarGridSpec(num_scalar_prefetch=1, grid=(2,), in_specs=[full, full, full, half, half],
                                               out_specs=[full] * 4),
        out_shape=[out] * 4, compiler_params=_cparams("parallel"),
    )(core, w, m, v, mine, other)


def _adamw_math(w, g, m, v):
    m = ADAM_B1 * m + (1.0 - ADAM_B1) * g
    v = ADAM_B2 * v + (1.0 - ADAM_B2) * (g * g)
    m_hat = m / (1.0 - ADAM_B1 ** ADAM_STEP)
    v_hat = v / (1.0 - ADAM_B2 ** ADAM_STEP)
    delta = -ADAM_LR * (m_hat / (jnp.sqrt(v_hat) + ADAM_EPS) + ADAM_WD * w)
    return delta, m, v


def _core_index():
    return lax.axis_index("c").astype(jnp.int32).reshape(1)


def _chip_partials_of(names, grads, grads_b, tag):
    by_shard = lambda g: g.reshape(N_CHIPS, g.shape[0] // N_CHIPS, g.shape[1])
    g3 = [by_shard(grads[n]) for n in names]
    g3_b = [by_shard(grads_b[n]) if n in grads_b else g.astype(BF16) for n, g in zip(names, g3)]
    shapes = [jax.ShapeDtypeStruct((N_CHIPS,) + _half_shape(n, g.shape[1:]), BF16) for n, g in zip(names, g3)]
    got = _exchange(g3_b, shapes, functools.partial(_plan_sibling_halves, names), 1, "sibling_halves_" + tag)
    return [_chip_partial(g, r, n, _core_index()) for n, g, r in zip(names, g3, got)]


REDUCE_COLLECTIVE_ID = {"late": 2, "early": 3}


def _exchange_partials_async(part, tag):
    n = len(part)
    src = [jax.new_ref(p, memory_space=pltpu.MemorySpace.HBM) for p in part]
    dst = [jax.empty_ref(jax.ShapeDtypeStruct((3,) + p.shape[1:], p.dtype), memory_space=pltpu.MemorySpace.HBM)
           for p in part]

    @pl.kernel(mesh=plsc.ScalarSubcoreMesh(axis_name="sequencer", num_cores=1), name="chip_partials_async_" + tag,
               scratch_types=(pltpu.SemaphoreType.DMA((3 * n,)), pltpu.SemaphoreType.DMA((3 * n,))),
               compiler_params=pltpu.CompilerParams(collective_id=REDUCE_COLLECTIVE_ID[tag]))
    def launch(send_sems, recv_sems):
        place = _place()
        x, y, c, others = place
        barrier = pltpu.get_barrier_semaphore()
        for ox, oy in others:
            pl.semaphore_signal(barrier, inc=1, device_id=(ox, oy, c), device_id_type=MESH)
        pl.semaphore_wait(barrier, len(others))
        rem = [pltpu.make_async_remote_copy(src_ref=s, dst_ref=d, send_sem=send_sems.at[k], recv_sem=recv_sems.at[k],
                                            device_id=to, device_id_type=MESH)
               for k, (s, d, to) in enumerate(_plan_chip_partials(src, dst, place))]
        for cp in rem:
            cp.start()
        for cp in rem:
            cp.wait_recv()
        for cp in rem:
            cp.wait_send()

    launch()
    return [d[...] for d in dst]


def _finish_reduce(names, part, got3, tag):
    me = (2 * lax.axis_index("x") + lax.axis_index("y")).astype(jnp.int32).reshape(1)
    mine = [_shard_sum(p, r3, n, me) for n, p, r3 in zip(names, part, got3)]
    other = _exchange(mine, [jax.ShapeDtypeStruct(m.shape, F32) for m in mine], _plan_to_sibling, 1,
                      "sibling_result_" + tag)
    return {n: (a, b) for n, a, b in zip(names, mine, other)}


SMALL_ROWS = 8
SMALL_LAYOUT = {"attn_norm_w": (0, 0, 1024), "ffn_norm_w": (1, 0, 1024), "final_norm_w": (2, 0, 1024),
                "q_a_norm_w": (3, 0, 384), "kv_a_norm_w": (3, 384, 256), "mlstm_norm_w": (4, 0, 512),
                "b_gates": (4, 512, 8)}
LOSS_SLOT = (5, 0)


def _pack_small(vals, loss=None):
    tile = jnp.zeros((SMALL_ROWS, D_MODEL), F32)
    for n, (r, c0, width) in SMALL_LAYOUT.items():
        tile = tile.at[r, c0:c0 + width].set(vals[n].reshape(width).astype(F32))
    if loss is not None:
        tile = tile.at[LOSS_SLOT[0], LOSS_SLOT[1]].set(loss)
    return tile


def _unpack_small(tile, shapes):
    return {n: tile[r, c0:c0 + width].reshape(shapes[n]) for n, (r, c0, width) in SMALL_LAYOUT.items()}


def _small_allreduce_adamw(g_tile, w_tile, m_tile, v_tile):
    def body(g_ref, w_ref, m_ref, v_ref, gsum_ref, d_ref, mo_ref, vo_ref, slots, send_sems, recv_sems):
        x, y, c, _ = _place()
        me = 4 * x + 2 * y + c
        slots[me] = g_ref[...]
        copies = []
        for k in range(1, 8):
            dx, dy, dc = (k >> 2) & 1, (k >> 1) & 1, k & 1
            to = (x ^ dx, y ^ dy, c ^ dc)
            cp = pltpu.make_async_remote_copy(src_ref=g_ref, dst_ref=slots.at[me], send_sem=send_sems.at[k - 1],
                                              recv_sem=recv_sems.at[k - 1], device_id=to, device_id_type=MESH)
            cp.start()
            copies.append(cp)
        for k in range(1, 8):
            src = me ^ k
            pltpu.make_async_remote_copy(src_ref=g_ref, dst_ref=slots.at[src], send_sem=send_sems.at[k - 1],
                                         recv_sem=recv_sems.at[k - 1], device_id=(x, y, c),
                                         device_id_type=MESH).wait_recv()
        for cp in copies:
            cp.wait_send()
        total = slots[0]
        for d in range(1, 8):
            total = total + slots[d]
        gsum_ref[...] = total
        delta, m_new, v_new = _adamw_math(w_ref[...], total, m_ref[...], v_ref[...])
        d_ref[...] = delta
        mo_ref[...] = m_new
        vo_ref[...] = v_new

    vm = pl.BlockSpec(memory_space=pltpu.VMEM)
    tile = jax.ShapeDtypeStruct((SMALL_ROWS, D_MODEL), F32)
    return pl.pallas_call(
        body, name="small_allreduce_adamw", in_specs=[vm] * 4, out_specs=[vm] * 4, out_shape=[tile] * 4,
        scratch_shapes=[pltpu.VMEM((8, SMALL_ROWS, D_MODEL), F32), pltpu.SemaphoreType.DMA((7,)),
                        pltpu.SemaphoreType.DMA((7,))],
    )(g_tile, w_tile, m_tile, v_tile)


def kernel(x, positions, attn_norm_w, w_in, b_gates, mlstm_norm_w, q_a_norm_w, w_q_b, kv_a_norm_w, w_kv_b, w_out, ffn_norm_w, w_gate, w_up, w_down, final_norm_w, loss_target, m_attn_norm_w, m_w_in, m_b_gates, m_mlstm_norm_w, m_q_a_norm_w, m_w_q_b, m_kv_a_norm_w, m_w_kv_b, m_w_out, m_ffn_norm_w, m_w_gate, m_w_up, m_w_down, m_final_norm_w, v_attn_norm_w, v_w_in, v_b_gates, v_mlstm_norm_w, v_q_a_norm_w, v_w_q_b, v_kv_a_norm_w, v_w_kv_b, v_w_out, v_ffn_norm_w, v_w_gate, v_w_up, v_w_down, v_final_norm_w):
    names = ("attn_norm_w", "w_in", "b_gates", "mlstm_norm_w", "q_a_norm_w", "w_q_b", "kv_a_norm_w", "w_kv_b", "w_out",
             "ffn_norm_w", "w_gate", "w_up", "w_down", "final_norm_w")
    wts = dict(zip(names, (attn_norm_w, w_in, b_gates, mlstm_norm_w, q_a_norm_w, w_q_b, kv_a_norm_w, w_kv_b, w_out,
                           ffn_norm_w, w_gate, w_up, w_down, final_norm_w)))
    mom = dict(zip(names, (m_attn_norm_w, m_w_in, m_b_gates, m_mlstm_norm_w, m_q_a_norm_w, m_w_q_b, m_kv_a_norm_w,
                           m_w_kv_b, m_w_out, m_ffn_norm_w, m_w_gate, m_w_up, m_w_down, m_final_norm_w)))
    vel = dict(zip(names, (v_attn_norm_w, v_w_in, v_b_gates, v_mlstm_norm_w, v_q_a_norm_w, v_w_q_b, v_kv_a_norm_w,
                           v_w_kv_b, v_w_out, v_ffn_norm_w, v_w_gate, v_w_up, v_w_down, v_final_norm_w)))
    t = x.shape[1]

    shards = {n: _stored(n, wts[n]).astype(BF16) for n in BIG}
    early = _gather_weights(EARLY, [shards[n] for n in EARLY])
    early, late_in = lax.optimization_barrier((early, [shards[n] for n in LATE]))
    late = _gather_weights_async(LATE, late_in)
    full = {n: g.reshape(N_CHIPS * g.shape[1], g.shape[2]) for n, g in zip(EARLY + LATE, early + late)}

    nw = {n: wts[n] for n in SMALL}
    def start_late_reduce(grads, grads_b, marker):
        part = _chip_partials_of(LATE, grads, grads_b, "late")
        part, marker = lax.optimization_barrier((part, marker))
        return (part, _exchange_partials_async(part, "late")), marker

    loss, grad_x, grads, grads_b, (late_part, late_got3) = _local_step(x[0], positions.reshape(t, 1), loss_target[0],
                                                                       nw, full, start_late_reduce)

    early_part = _chip_partials_of(EARLY, grads, grads_b, "early")
    early_part, late_part = lax.optimization_barrier((list(early_part), list(late_part)))
    early_got3 = _exchange_partials_async(early_part, "early")
    core = _core_index()
    outs_g, outs_d, outs_m, outs_v = {}, {}, {}, {}

    def finish(group, part, got3, tag):
        halves = _finish_reduce(group, part, got3, tag)
        for n in group:
            res = _adamw_halves(_stored(n, wts[n]), _stored(n, mom[n]), _stored(n, vel[n]), *halves[n], n, core)
            outs_g[n], outs_d[n], outs_m[n], outs_v[n] = [_unstored(n, r) for r in res]

    finish(LATE, late_part, late_got3, "late")
    shapes = {n: wts[n].shape for n in SMALL}
    g_tile = _pack_small({n: grads[n] for n in SMALL}, loss=loss[0, 0])
    gsum, d_tile, m_tile, v_tile = _small_allreduce_adamw(g_tile, _pack_small(wts), _pack_small(mom), _pack_small(vel))
    finish(EARLY, early_part, early_got3, "early")
    outs_g.update(_unpack_small(gsum, shapes))
    outs_d.update(_unpack_small(d_tile, shapes))
    outs_m.update(_unpack_small(m_tile, shapes))
    outs_v.update(_unpack_small(v_tile, shapes))
    total_loss = gsum[LOSS_SLOT[0], LOSS_SLOT[1]]
    return (total_loss, grad_x[None], *[outs_g[n] for n in names], *[outs_d[n] for n in names],
            *[outs_m[n] for n in names], *[outs_v[n] for n in names])
```

```python
import functools
import math

import jax
import jax.numpy as jnp
from jax import lax
from jax.experimental import pallas as pl
from jax.experimental.pallas import tpu as pltpu
from jax.experimental.pallas import tpu_sc as plsc

F32 = jnp.float32
BF16 = jnp.bfloat16

D_MODEL = 1024
M_HEADS = 4
M_HEAD_DIM = 128
M_WIDTH = M_HEADS * M_HEAD_DIM
M_CHUNK = 64
A_HEADS = 8
A_NOPE = 64
A_ROPE = 32
A_V = 64
A_WIDTH = A_HEADS * A_V
A_QK_PAD = 128
Q_RANK = 384
KV_RANK = 256
ROPE_THETA = 10000.0
D_FF = 2816
D_IN = 2728
EPS = 1e-6
ATTN_SCALE = (A_NOPE + A_ROPE) ** -0.5
M_SCALE = M_HEAD_DIM ** -0.5

ADAM_LR = 0.001
ADAM_B1 = 0.9
ADAM_B2 = 0.999
ADAM_EPS = 1e-08
ADAM_WD = 0.01
ADAM_STEP = 10

VMEM_LIMIT_BYTES = 56 * 1024 * 1024
LANES = 128
NEG_INF = float("-inf")


def _cparams(*sem):
    return pltpu.CompilerParams(dimension_semantics=sem if sem else None, vmem_limit_bytes=VMEM_LIMIT_BYTES)


_DIMS = {"nn": (((1,), (0,)), ((), ())), "nt": (((1,), (1,)), ((), ())), "tn": (((0,), (0,)), ((), ()))}


def _matmul(a, b, mode, out_dtype, name, tm=None, tn=1024, residual=None, b_rows=None, lead_scale=None,
            also_bf16=False):
    a_list = list(a) if isinstance(a, (list, tuple)) else [a]
    b_list = list(b) if isinstance(b, (list, tuple)) else [b]
    rows_list = list(b_rows) if isinstance(b_rows, (list, tuple)) else [b_rows] * len(b_list)
    assert len(a_list) == len(b_list) == len(rows_list)
    dims = _DIMS[mode]
    specs, m, n = [], None, None
    for aa, bb, rr in zip(a_list, b_list, rows_list):
        b_shape = bb.shape if rr is None else (rr, bb.shape[1])
        if mode == "nn":
            (m1, k), (k2, n1) = aa.shape, b_shape
        elif mode == "nt":
            (m1, k), (n1, k2) = aa.shape, b_shape
        else:
            (k, m1), (k2, n1) = aa.shape, b_shape
        assert k == k2 and (m is None or (m, n) == (m1, n1)), (aa.shape, bb.shape, mode)
        m, n = m1, n1
        specs.append(k)
    if tm is None:
        tm = 512 if mode == "tn" else 1024
    tm, tn = min(tm, m), min(tn, n)
    assert m % tm == 0 and n % tn == 0, (m, n, tm, tn)
    in_specs = []
    for k in specs:
        in_specs.append(pl.BlockSpec((k, tm), lambda i, j: (0, i)) if mode == "tn"
                        else pl.BlockSpec((tm, k), lambda i, j: (i, 0)))
        in_specs.append(pl.BlockSpec((tn, k), lambda i, j: (j, 0)) if mode == "nt"
                        else pl.BlockSpec((k, tn), lambda i, j: (0, j)))
    o_spec = pl.BlockSpec((tm, tn), lambda i, j: (i, j))
    n_pairs = len(specs)

    def body(*refs):
        acc = None
        for p in range(n_pairs):
            part = lax.dot_general(refs[2 * p][...].astype(BF16), refs[2 * p + 1][...].astype(BF16), dims,
                                   preferred_element_type=F32)
            acc = part if acc is None else acc + part
        if lead_scale is not None:
            col = pl.program_id(1) * tn + lax.broadcasted_iota(jnp.int32, (1, tn), 1)
            acc = acc * jnp.where(col < lead_scale[0], lead_scale[1], 1.0)
        if residual is not None:
            acc = acc + refs[2 * n_pairs][...].astype(F32)
        outs = refs[2 * n_pairs + (residual is not None):]
        outs[0][...] = acc.astype(outs[0].dtype)
        if also_bf16:
            outs[1][...] = acc.astype(BF16)

    ins = [x for pair in zip(a_list, b_list) for x in pair] + ([residual] if residual is not None else [])
    in_specs = in_specs + ([o_spec] if residual is not None else [])
    out_shape = jax.ShapeDtypeStruct((m, n), out_dtype)
    if also_bf16:
        return pl.pallas_call(
            body, name=name, grid=(m // tm, n // tn), in_specs=in_specs, out_specs=[o_spec, o_spec],
            out_shape=[out_shape, jax.ShapeDtypeStruct((m, n), BF16)],
            compiler_params=_cparams("parallel", "parallel"),
        )(*ins)
    return pl.pallas_call(
        body, name=name, grid=(m // tm, n // tn), in_specs=in_specs, out_specs=o_spec, out_shape=out_shape,
        compiler_params=_cparams("parallel", "parallel"),
    )(*ins)


def _rmsnorm_fwd(x, w, name, tm=512):
    t, d = x.shape

    def body(x_ref, w_ref, o_ref):
        xf = x_ref[...].astype(F32)
        r = lax.rsqrt(jnp.mean(xf * xf, axis=-1, keepdims=True) + EPS)
        o_ref[...] = (xf * r * w_ref[...]).astype(o_ref.dtype)

    return pl.pallas_call(
        body, name=name, grid=(t // tm,),
        in_specs=[pl.BlockSpec((tm, d), lambda i: (i, 0)), pl.BlockSpec((1, d), lambda i: (0, 0))],
        out_specs=pl.BlockSpec((tm, d), lambda i: (i, 0)),
        out_shape=jax.ShapeDtypeStruct((t, d), BF16), compiler_params=_cparams("parallel"),
    )(x, w)


def _rmsnorm_bwd(dy, x, w, name, out_dtype, residual=None, also_bf16=False, tm=512):
    t, d = x.shape

    def body(dy_ref, x_ref, w_ref, *rest):
        dx_ref, dw_ref = rest[-3 if also_bf16 else -2], rest[-1]
        xf = x_ref[...].astype(F32)
        r = lax.rsqrt(jnp.mean(xf * xf, axis=-1, keepdims=True) + EPS)
        xh = xf * r
        dyf = dy_ref[...].astype(F32)
        dyh = dyf * w_ref[...]
        dx = r * (dyh - xh * jnp.mean(dyh * xh, axis=-1, keepdims=True))
        if residual is not None:
            dx = dx + rest[0][...].astype(F32)
        dx_ref[...] = dx.astype(dx_ref.dtype)
        if also_bf16:
            rest[-2][...] = dx.astype(BF16)
        part = jnp.sum(dyf * xh, axis=0, keepdims=True)

        @pl.when(pl.program_id(0) == 0)
        def _():
            dw_ref[...] = part

        @pl.when(pl.program_id(0) > 0)
        def _():
            dw_ref[...] += part

    row = pl.BlockSpec((tm, d), lambda i: (i, 0))
    vec = pl.BlockSpec((1, d), lambda i: (0, 0))
    ins = [dy, x, w] + ([residual] if residual is not None else [])
    extra = [jax.ShapeDtypeStruct((t, d), BF16)] if also_bf16 else []
    return pl.pallas_call(
        body, name=name, grid=(t // tm,), in_specs=[row, row, vec] + ([row] if residual is not None else []),
        out_specs=[row] + [row] * len(extra) + [vec],
        out_shape=[jax.ShapeDtypeStruct((t, d), out_dtype)] + extra + [jax.ShapeDtypeStruct((1, d), F32)],
        compiler_params=_cparams("arbitrary"),
    )(*ins)


def _final_loss(h, target, w, name, tm=512):
    t, d = h.shape

    def body(h_ref, t_ref, w_ref, dh_ref, dhb_ref, loss_ref, dw_ref):
        xf = h_ref[...]
        r = lax.rsqrt(jnp.mean(xf * xf, axis=-1, keepdims=True) + EPS)
        xh = xf * r
        err = xh * w_ref[...] - t_ref[...]
        part_loss = 0.5 * jnp.sum(jnp.sum(err * err, axis=-1, keepdims=True), axis=0, keepdims=True) * (1.0 / d)
        dy = err * (1.0 / d)
        dyh = dy * w_ref[...]
        dh = r * (dyh - xh * jnp.mean(dyh * xh, axis=-1, keepdims=True))
        dh_ref[...] = dh
        dhb_ref[...] = dh.astype(BF16)
        part_dw = jnp.sum(dy * xh, axis=0, keepdims=True)
        part_loss = jnp.broadcast_to(part_loss, (1, LANES))

        @pl.when(pl.program_id(0) == 0)
        def _():
            dw_ref[...] = part_dw
            loss_ref[...] = part_loss

        @pl.when(pl.program_id(0) > 0)
        def _():
            dw_ref[...] += part_dw
            loss_ref[...] += part_loss

    row = pl.BlockSpec((tm, d), lambda i: (i, 0))
    vec = pl.BlockSpec((1, d), lambda i: (0, 0))
    return pl.pallas_call(
        body, name=name, grid=(t // tm,), in_specs=[row, row, vec],
        out_specs=[row, row, pl.BlockSpec((1, LANES), lambda i: (0, 0)), vec],
        out_shape=[jax.ShapeDtypeStruct((t, d), F32), jax.ShapeDtypeStruct((t, d), BF16),
                   jax.ShapeDtypeStruct((1, LANES), F32), jax.ShapeDtypeStruct((1, d), F32)],
        compiler_params=_cparams("arbitrary"),
    )(h, target, w)


FFN_TN = 1408


def _ffn_gate_up(u, w_gate, w_up, name, tm=1024):
    t, d = u.shape
    n = w_gate.shape[0]
    tm = min(tm, t)

    def body(u_ref, wg_ref, wu_ref, g_ref, up_ref, act_ref):
        uu = u_ref[...]
        g = lax.dot_general(uu, wg_ref[...], _DIMS["nt"], preferred_element_type=F32)
        up = lax.dot_general(uu, wu_ref[...], _DIMS["nt"], preferred_element_type=F32)
        g_b, up_b = g.astype(BF16), up.astype(BF16)
        g_ref[...] = g_b
        up_ref[...] = up_b
        gf, uf = g_b.astype(F32), up_b.astype(F32)
        act_ref[...] = (gf * jax.nn.sigmoid(gf) * uf).astype(act_ref.dtype)

    w_spec = pl.BlockSpec((FFN_TN, d), lambda i, j: (j, 0))
    o_spec = pl.BlockSpec((tm, FFN_TN), lambda i, j: (i, j))
    out = jax.ShapeDtypeStruct((t, n), BF16)
    return pl.pallas_call(
        body, name=name, grid=(t // tm, n // FFN_TN),
        in_specs=[pl.BlockSpec((tm, d), lambda i, j: (i, 0)), w_spec, w_spec], out_specs=[o_spec] * 3,
        out_shape=[out] * 3, compiler_params=_cparams("parallel", "parallel"),
    )(u, w_gate, w_up)


def _ffn_down_bwd(dh, w_down, g, up, name, tm=1024):
    t, d = dh.shape
    n = w_down.shape[0]
    tm = min(tm, t)

    def body(dh_ref, w_ref, g_ref, up_ref, dg_ref, du_ref):
        df = lax.dot_general(dh_ref[...], w_ref[...], _DIMS["nt"], preferred_element_type=F32).astype(BF16).astype(F32)
        gf = g_ref[...].astype(F32)
        uf = up_ref[...].astype(F32)
        s = jax.nn.sigmoid(gf)
        dg_ref[...] = (df * uf * s * (1.0 + gf * (1.0 - s))).astype(dg_ref.dtype)
        du_ref[...] = (df * gf * s).astype(du_ref.dtype)

    o_spec = pl.BlockSpec((tm, FFN_TN), lambda i, j: (i, j))
    out = jax.ShapeDtypeStruct((t, n), BF16)
    return pl.pallas_call(
        body, name=name, grid=(t // tm, n // FFN_TN),
        in_specs=[pl.BlockSpec((tm, d), lambda i, j: (i, 0)), pl.BlockSpec((FFN_TN, d), lambda i, j: (j, 0)),
                  o_spec, o_spec],
        out_specs=[o_spec, o_spec], out_shape=[out, out], compiler_params=_cparams("parallel", "parallel"),
    )(dh, w_down, g, up)


M_BLOCK = 512
M_CHUNKS_PER_BLOCK = M_BLOCK // M_CHUNK
M_UNROLL = 2


def _log_sigmoid(z):
    return jnp.minimum(z, 0.0) - jnp.log(1.0 + jnp.exp(-jnp.abs(z)))


def _per_head(fn):
    return jnp.stack([fn(hd) for hd in range(M_HEADS)])


def _mlstm_chunk_fwd(q, k, v, gc, c_prev, n_prev, m_prev):
    L, H = M_CHUNK, M_HEADS
    tt = lax.broadcasted_iota(jnp.int32, (L, L), 0)
    ss = lax.broadcasted_iota(jnp.int32, (L, L), 1)
    eye = tt == ss
    causal = ss <= tt
    gct = jnp.transpose(gc)
    i_col = _per_head(lambda hd: gc[:, hd:hd + 1])
    f_col = _per_head(lambda hd: gc[:, H + hd:H + hd + 1])
    i_row = _per_head(lambda hd: gct[hd:hd + 1, :])
    lf_col = _log_sigmoid(f_col)
    lf_row = _log_sigmoid(_per_head(lambda hd: gct[H + hd:H + hd + 1, :]))
    b_col = jnp.sum(jnp.where(causal, lf_row, 0.0), axis=2, keepdims=True)
    b_row = jnp.sum(jnp.where(tt <= ss, lf_col, 0.0), axis=1, keepdims=True)
    g = jnp.sum(lf_col, axis=1, keepdims=True)
    a_row = g - b_row + i_row
    a_col = g - b_col + i_col
    m_loc = jnp.max(a_row, axis=2, keepdims=True)
    d_log = jnp.where(causal, b_col - b_row + i_row, NEG_INF)
    inter = b_col + m_prev
    m_t = jnp.maximum(jnp.max(d_log, axis=2, keepdims=True), inter)
    dmat = jnp.exp(d_log - m_t)
    amat = _per_head(lambda hd: lax.dot_general(q[hd], k[hd], _DIMS["nt"], preferred_element_type=F32))
    p = dmat * amat
    sig = jnp.exp(inter - m_t)
    c_prev_b = [c.astype(BF16) for c in c_prev]
    qc = _per_head(lambda hd: jnp.dot(q[hd], c_prev_b[hd], preferred_element_type=F32))
    p_b = p.astype(BF16)
    num = _per_head(lambda hd: jnp.dot(p_b[hd], v[hd], preferred_element_type=F32)) + sig * qc
    qf = _per_head(lambda hd: q[hd].astype(F32))
    qn = jnp.sum(qf * n_prev, axis=2, keepdims=True)
    den_raw = jnp.sum(p, axis=2, keepdims=True) + sig * qn
    floor = jnp.exp(-m_t)
    den = jnp.maximum(jnp.abs(den_raw), floor)
    h = num / den
    m_new = jnp.maximum(g + m_prev, m_loc)
    w_col = jnp.exp(a_col - m_new)
    w_row = jnp.exp(a_row - m_new)
    alpha = jnp.exp(g + m_prev - m_new)
    return dict(eye=eye, causal=causal, tt=tt, ss=ss, i_col=i_col, f_col=f_col, dmat=dmat, amat=amat, p=p, p_b=p_b,
                sig=sig, qc=qc, qf=qf, qn=qn, num=num, den_raw=den_raw, floor=floor, den=den, h=h, m_new=m_new,
                w_col=w_col, w_row=w_row, alpha=alpha, c_prev_b=c_prev_b, b_row=b_row, b_col=b_col, m_t=m_t)


def _chunk_major_t(xt):
    f, t = xt.shape
    return xt.reshape(f, t // M_CHUNK, M_CHUNK).transpose(1, 0, 2)


def _mlstm_fwd(m4, kt, gates, bias, norm_w, name):
    t = m4.shape[0]
    nblk = t // M_BLOCK
    nc = t // M_CHUNK
    L, dh = M_CHUNK, M_HEAD_DIM

    def body(m4_ref, kt_ref, g_ref, b_ref, w_ref, hm_ref, cp_ref, nm_ref, c_s, n_s, m_s):
        @pl.when(pl.program_id(0) == 0)
        def _():
            c_s[...] = jnp.zeros_like(c_s)
            n_s[...] = jnp.zeros_like(n_s)
            m_s[...] = jnp.zeros_like(m_s)

        row8 = lax.broadcasted_iota(jnp.int32, (8, dh), 0)

        def chunk(c, carry):
            rows = pl.ds(pl.multiple_of(c * L, L), L)
            gc = g_ref[rows, :] + b_ref[...]
            col = lambda part, hd: m4_ref[rows, part * M_WIDTH + hd * dh:part * M_WIDTH + (hd + 1) * dh]
            q = [col(0, hd) for hd in range(M_HEADS)]
            k = [col(1, hd) for hd in range(M_HEADS)]
            v = [col(2, hd) for hd in range(M_HEADS)]
            o = _per_head(lambda hd: col(3, hd).astype(F32))
            c_prev = [c_s[hd] for hd in range(M_HEADS)]
            n_prev = n_s[:, 0:1, :]
            m_prev = m_s[:, 0:1, 0:1]
            r = _mlstm_chunk_fwd(q, k, v, gc, c_prev, n_prev, m_prev)
            h = r["h"]
            hn = h * lax.rsqrt(jnp.mean(h * h, axis=-1, keepdims=True) + EPS) * w_ref[...]
            hm = (hn * jax.nn.sigmoid(o)).astype(hm_ref.dtype)
            nm = jnp.where(row8 == 0, n_prev, jnp.where(row8 == 1, m_prev, 0.0))
            kf = _per_head(lambda hd: k[hd].astype(F32))
            n_s[:, 0:1, :] = r["alpha"] * n_prev + jnp.sum(kf * r["w_col"], axis=1, keepdims=True)
            m_s[...] = jnp.broadcast_to(r["m_new"], (M_HEADS, 8, dh))
            for hd in range(M_HEADS):
                hm_ref[rows, hd * dh:(hd + 1) * dh] = hm[hd]
                cp_ref[hd, c] = r["c_prev_b"][hd]
                nm_ref[hd, c] = nm[hd]
                kw_t = (kt_ref[c, hd * dh:(hd + 1) * dh, :].astype(F32) * r["w_row"][hd]).astype(BF16)
                c_s[hd] = r["alpha"][hd] * c_prev[hd] + jnp.dot(kw_t, v[hd], preferred_element_type=F32)
            return carry

        lax.fori_loop(0, M_CHUNKS_PER_BLOCK, chunk, 0)

    return pl.pallas_call(
        body, name=name, grid=(nblk,),
        in_specs=[pl.BlockSpec((M_BLOCK, 4 * M_WIDTH), lambda i: (i, 0)),
                  pl.BlockSpec((M_CHUNKS_PER_BLOCK, M_WIDTH, L), lambda i: (i, 0, 0)),
                  pl.BlockSpec((M_BLOCK, LANES), lambda i: (i, 0)),
                  pl.BlockSpec((1, LANES), lambda i: (0, 0)),
                  pl.BlockSpec((M_HEADS, 1, dh), lambda i: (0, 0, 0))],
        out_specs=[pl.BlockSpec((M_BLOCK, M_WIDTH), lambda i: (i, 0)),
                   pl.BlockSpec((M_HEADS, M_CHUNKS_PER_BLOCK, dh, dh), lambda i: (0, i, 0, 0)),
                   pl.BlockSpec((M_HEADS, M_CHUNKS_PER_BLOCK, 8, dh), lambda i: (0, i, 0, 0))],
        out_shape=[jax.ShapeDtypeStruct((t, M_WIDTH), BF16),
                   jax.ShapeDtypeStruct((M_HEADS, nc, dh, dh), BF16),
                   jax.ShapeDtypeStruct((M_HEADS, nc, 8, dh), F32)],
        scratch_shapes=[pltpu.VMEM((M_HEADS, dh, dh), F32), pltpu.VMEM((M_HEADS, 8, dh), F32),
                        pltpu.VMEM((M_HEADS, 8, dh), F32)],
        compiler_params=_cparams("arbitrary"),
    )(m4, kt, gates, bias, norm_w)


def _mlstm_bwd(m4, qt, gates, bias, norm_w, c_prev_all, nm_all, dhm, name):
    t = m4.shape[0]
    nblk = t // M_BLOCK
    L, dh = M_CHUNK, M_HEAD_DIM

    def body(m4_ref, qt_ref, g_ref, b_ref, w_ref, cp_ref, nm_ref, dhm_ref, dm4_ref, dg_ref, small_ref, dc_s, dn_s):
        @pl.when(pl.program_id(0) == 0)
        def _():
            dc_s[...] = jnp.zeros_like(dc_s)
            dn_s[...] = jnp.zeros_like(dn_s)
            small_ref[...] = jnp.zeros_like(small_ref)

        lane = lax.broadcasted_iota(jnp.int32, (L, LANES), 1)
        row8 = lax.broadcasted_iota(jnp.int32, (8, LANES), 0)

        def chunk(ci, carry):
            c = M_CHUNKS_PER_BLOCK - 1 - ci
            rows = pl.ds(pl.multiple_of(c * L, L), L)
            gc = g_ref[rows, :] + b_ref[...]
            heads = range(M_HEADS)
            col = lambda part, hd: m4_ref[rows, part * M_WIDTH + hd * dh:part * M_WIDTH + (hd + 1) * dh]
            q = [col(0, hd) for hd in heads]
            k = [col(1, hd) for hd in heads]
            v = [col(2, hd) for hd in heads]
            o = _per_head(lambda hd: col(3, hd).astype(F32))
            q_t = [qt_ref[c, hd * dh:(hd + 1) * dh, :] for hd in heads]
            c_prev = [cp_ref[hd, c].astype(F32) for hd in heads]
            nm = _per_head(lambda hd: nm_ref[hd, c])
            n_prev = nm[:, 0:1, :]
            m_prev = nm[:, 1:2, 0:1]
            r = _mlstm_chunk_fwd(q, k, v, gc, c_prev, n_prev, m_prev)
            eye, tt, ss = r["eye"], r["tt"], r["ss"]
            h, den, sig, p = r["h"], r["den"], r["sig"], r["p"]
            w_col, alpha, c_prev_b = r["w_col"], r["alpha"], r["c_prev_b"]
            to_row = lambda colv: jnp.sum(jnp.where(eye, colv, 0.0), axis=1, keepdims=True)
            to_col = lambda rowv: jnp.sum(jnp.where(eye, rowv, 0.0), axis=2, keepdims=True)
            mm = lambda fn: _per_head(lambda hd: fn(hd))

            w_h = w_ref[...]
            rn = lax.rsqrt(jnp.mean(h * h, axis=-1, keepdims=True) + EPS)
            hh = h * rn
            og = jax.nn.sigmoid(o)
            dhm_c = _per_head(lambda hd: dhm_ref[rows, hd * dh:(hd + 1) * dh].astype(F32))
            dhn = dhm_c * og
            d_o = dhm_c * hh * w_h * og * (1.0 - og)
            d_norm = jnp.sum(dhn * hh, axis=1, keepdims=True)
            dhh = dhn * w_h
            dh_ = rn * (dhh - hh * jnp.mean(dhh * hh, axis=-1, keepdims=True))

            dnum = dh_ / den
            dden = -jnp.sum(dh_ * h, axis=-1, keepdims=True) / den
            dden_raw = jnp.where(jnp.abs(r["den_raw"]) >= r["floor"], dden * jnp.sign(r["den_raw"]), 0.0)
            dnum_b = dnum.astype(BF16)
            dp = mm(lambda hd: lax.dot_general(dnum_b[hd], v[hd], _DIMS["nt"], preferred_element_type=F32)) + dden_raw
            dc = [dc_s[hd] for hd in heads]
            dn = dn_s[:, 0:1, :]
            dc_b = [d.astype(BF16) for d in dc]
            g2_b = (sig * dnum).astype(BF16)
            sd = sig * dden_raw
            da_mat = (dp * r["dmat"]).astype(BF16)
            dqs = mm(lambda hd: lax.dot_general(g2_b[hd], c_prev_b[hd], _DIMS["nt"], preferred_element_type=F32)
                     + jnp.dot(da_mat[hd], k[hd], preferred_element_type=F32)) + sd * n_prev
            r_mat = mm(lambda hd: lax.dot_general(v[hd], dc_b[hd], _DIMS["nt"], preferred_element_type=F32)) + dn
            kf = _per_head(lambda hd: k[hd].astype(F32))
            dmat_t = jnp.exp(jnp.where(tt <= ss, r["b_row"] - r["b_col"] + r["i_col"], NEG_INF) - to_row(r["m_t"]))
            p_t = (dmat_t * mm(lambda hd: jnp.dot(k[hd], q_t[hd], preferred_element_type=F32))).astype(BF16)
            dp_t = mm(lambda hd: lax.dot_general(v[hd], dnum_b[hd], _DIMS["nt"], preferred_element_type=F32))
            da_t = ((dp_t + to_row(dden_raw)) * dmat_t).astype(BF16)
            d_k = mm(lambda hd: jnp.dot(da_t[hd], q[hd], preferred_element_type=F32)) + w_col * r_mat
            d_v = (mm(lambda hd: jnp.dot(p_t[hd], dnum_b[hd], preferred_element_type=F32))
                   + w_col * mm(lambda hd: jnp.dot(k[hd], dc_b[hd], preferred_element_type=F32)))
            da_col = jnp.sum(kf * r_mat, axis=-1, keepdims=True) * w_col
            dsig = jnp.sum(dnum * r["qc"], axis=-1, keepdims=True) + dden_raw * r["qn"]
            pp = dp * p
            r1 = jnp.sum(pp, axis=2, keepdims=True)
            c1_col = to_col(jnp.sum(pp, axis=1, keepdims=True))
            dcc = _per_head(lambda hd: dc[hd] * c_prev[hd])
            d_alpha = (jnp.sum(jnp.sum(dcc, axis=2, keepdims=True), axis=1, keepdims=True)
                       + jnp.sum(dn * n_prev, axis=2, keepdims=True))
            dgl = jnp.sum(da_col, axis=1, keepdims=True) + d_alpha * alpha
            db_col = r1 + dsig * sig - c1_col - da_col
            dli_col = c1_col + da_col
            dlf_col = jnp.sum(jnp.where(ss >= tt, to_row(db_col), 0.0), axis=2, keepdims=True) + dgl
            df_col = dlf_col * jax.nn.sigmoid(-r["f_col"])

            dn_s[:, 0:1, :] = alpha * dn + jnp.sum(sd * r["qf"], axis=1, keepdims=True)
            dq_out = (dqs * M_SCALE).astype(dm4_ref.dtype)
            dk_out, dv_out, do_out = d_k.astype(dm4_ref.dtype), d_v.astype(dm4_ref.dtype), d_o.astype(dm4_ref.dtype)
            dg_tile = jnp.zeros((L, LANES), F32)
            small = jnp.zeros((8, LANES), F32)
            for hd in heads:
                dc_s[hd] = alpha[hd] * dc[hd] + jnp.dot(q_t[hd], g2_b[hd], preferred_element_type=F32)
                dm4_ref[rows, hd * dh:(hd + 1) * dh] = dq_out[hd]
                dm4_ref[rows, M_WIDTH + hd * dh:M_WIDTH + (hd + 1) * dh] = dk_out[hd]
                dm4_ref[rows, 2 * M_WIDTH + hd * dh:2 * M_WIDTH + (hd + 1) * dh] = dv_out[hd]
                dm4_ref[rows, 3 * M_WIDTH + hd * dh:3 * M_WIDTH + (hd + 1) * dh] = do_out[hd]
                dg_tile = (dg_tile + jnp.where(lane == hd, dli_col[hd], 0.0)
                           + jnp.where(lane == M_HEADS + hd, df_col[hd], 0.0))
                small = small + jnp.where(row8 == hd, d_norm[hd], 0.0)
            dg_ref[rows, :] = dg_tile
            small = small + jnp.where(row8 == M_HEADS, jnp.sum(dg_tile, axis=0, keepdims=True), 0.0)
            small_ref[...] += small
            return carry

        lax.fori_loop(0, M_CHUNKS_PER_BLOCK, chunk, 0, unroll=M_UNROLL)

    rev = lambda i: nblk - 1 - i
    return pl.pallas_call(
        body, name=name, grid=(nblk,),
        in_specs=[pl.BlockSpec((M_BLOCK, 4 * M_WIDTH), lambda i: (rev(i), 0)),
                  pl.BlockSpec((M_CHUNKS_PER_BLOCK, M_WIDTH, L), lambda i: (rev(i), 0, 0)),
                  pl.BlockSpec((M_BLOCK, LANES), lambda i: (rev(i), 0)),
                  pl.BlockSpec((1, LANES), lambda i: (0, 0)),
                  pl.BlockSpec((M_HEADS, 1, dh), lambda i: (0, 0, 0)),
                  pl.BlockSpec((M_HEADS, M_CHUNKS_PER_BLOCK, dh, dh), lambda i: (0, rev(i), 0, 0)),
                  pl.BlockSpec((M_HEADS, M_CHUNKS_PER_BLOCK, 8, dh), lambda i: (0, rev(i), 0, 0)),
                  pl.BlockSpec((M_BLOCK, M_WIDTH), lambda i: (rev(i), 0))],
        out_specs=[pl.BlockSpec((M_BLOCK, 4 * M_WIDTH), lambda i: (rev(i), 0)),
                   pl.BlockSpec((M_BLOCK, LANES), lambda i: (rev(i), 0)),
                   pl.BlockSpec((8, LANES), lambda i: (0, 0))],
        out_shape=[jax.ShapeDtypeStruct((t, 4 * M_WIDTH), BF16), jax.ShapeDtypeStruct((t, LANES), F32),
                   jax.ShapeDtypeStruct((8, LANES), F32)],
        scratch_shapes=[pltpu.VMEM((M_HEADS, dh, dh), F32), pltpu.VMEM((M_HEADS, 8, dh), F32)],
        compiler_params=_cparams("arbitrary"),
    )(m4, qt, gates, bias, norm_w, c_prev_all, nm_all, dhm)


def _rope_tables(pos_col, width):
    lane = lax.broadcasted_iota(jnp.int32, (1, width), 1) % A_QK_PAD
    half = A_ROPE // 2
    first = (lane >= A_NOPE) & (lane < A_NOPE + half)
    second = (lane >= A_NOPE + half) & (lane < A_NOPE + A_ROPE)
    idx = jnp.where(first, lane - A_NOPE, lane - A_NOPE - half).astype(F32)
    inv_freq = jnp.exp(idx * (-math.log(ROPE_THETA) / half))
    ang = pos_col.astype(F32) * inv_freq
    cos, sin = jnp.cos(ang), jnp.sin(ang)
    rot = first | second
    return jnp.where(rot, cos, 1.0), jnp.where(first, -sin, 0.0), jnp.where(second, sin, 0.0)


def _rope_apply(vv, cosf, s1, s2):
    half = A_ROPE // 2
    w = vv.shape[-1]
    return vv * cosf + pltpu.roll(vv, w - half, 1) * s1 + pltpu.roll(vv, half, 1) * s2


def _rope_apply_t(dd, cosf, s1, s2):
    half = A_ROPE // 2
    w = dd.shape[-1]
    return dd * cosf + pltpu.roll(dd * s1, half, 1) + pltpu.roll(dd * s2, w - half, 1)


A_BIAS_LANE_K = A_NOPE + A_ROPE
A_BIAS_LANE_V = A_V


def _hi_lo(val):
    hi = val.astype(BF16)
    return hi, (val - hi.astype(F32)).astype(BF16)


def _qkv_up_rope(qa_n, kv_n, w_qb, w_k, w_v, gates, pos, name, tm=512):
    t = qa_n.shape[0]
    w = A_HEADS * LANES

    def body(qa_ref, kv_ref, wq_ref, wk_ref, wv_ref, g_ref, p_ref, qo_ref, ko_ref, vo_ref):
        cosf, s1, s2 = _rope_tables(p_ref[...], LANES)
        lane = lax.broadcasted_iota(jnp.int32, (1, LANES), 1)
        kr = jnp.where((lane >= A_NOPE) & (lane < A_NOPE + A_ROPE), g_ref[...], 0.0)
        kr = _rope_apply(kr, cosf, s1, s2)
        kr = jnp.where((lane == A_BIAS_LANE_K) | (lane == A_BIAS_LANE_K + 1), 1.0, kr)
        v_one = (lane == A_BIAS_LANE_V) | (lane == A_BIAS_LANE_V + 1)
        qpre = lax.dot_general(qa_ref[...], wq_ref[...], _DIMS["nt"], preferred_element_type=F32)
        kv = kv_ref[...]
        kpre = lax.dot_general(kv, wk_ref[...], _DIMS["nt"], preferred_element_type=F32)
        vpre = lax.dot_general(kv, wv_ref[...], _DIMS["nt"], preferred_element_type=F32)
        for hd in range(A_HEADS):
            sl = slice(hd * LANES, (hd + 1) * LANES)
            qo_ref[:, sl] = (_rope_apply(qpre[:, sl], cosf, s1, s2) * ATTN_SCALE).astype(qo_ref.dtype)
            ko_ref[:, sl] = (kpre[:, sl] + kr).astype(ko_ref.dtype)
            vo_ref[:, sl] = jnp.where(v_one, 1.0, vpre[:, sl]).astype(vo_ref.dtype)

    rows = lambda width: pl.BlockSpec((tm, width), lambda i: (i, 0))
    whole = lambda arr: pl.BlockSpec(arr.shape, lambda i: (0, 0))
    out = jax.ShapeDtypeStruct((t, w), BF16)
    return pl.pallas_call(
        body, name=name, grid=(t // tm,),
        in_specs=[rows(qa_n.shape[1]), rows(kv_n.shape[1]), whole(w_qb), whole(w_k), whole(w_v), rows(LANES), rows(1)],
        out_specs=[rows(w)] * 3, out_shape=[out] * 3, compiler_params=_cparams("parallel"),
    )(qa_n, kv_n, w_qb, w_k, w_v, gates, pos)


def _rope_bwd(dq, dk, dgates, pos, name, tm=512):
    t, w = dq.shape

    def body(dq_ref, dk_ref, dg_ref, p_ref, dqo_ref, dgo_ref):
        cosf, s1, s2 = _rope_tables(p_ref[...], LANES)
        acc = jnp.zeros((tm, LANES), F32)
        for hd in range(A_HEADS):
            sl = slice(hd * LANES, (hd + 1) * LANES)
            dqo_ref[:, sl] = (_rope_apply_t(dq_ref[:, sl], cosf, s1, s2) * ATTN_SCALE).astype(dqo_ref.dtype)
            acc = acc + dk_ref[:, sl].astype(F32)
        lane = lax.broadcasted_iota(jnp.int32, (1, LANES), 1)
        dkr = _rope_apply_t(acc, cosf, s1, s2)
        dkr = jnp.where((lane >= A_NOPE) & (lane < A_NOPE + A_ROPE), dkr, 0.0)
        dgo_ref[...] = (dg_ref[...] + dkr).astype(dgo_ref.dtype)

    row = pl.BlockSpec((tm, w), lambda i: (i, 0))
    nar = pl.BlockSpec((tm, LANES), lambda i: (i, 0))
    return pl.pallas_call(
        body, name=name, grid=(t // tm,),
        in_specs=[row, row, nar, pl.BlockSpec((tm, 1), lambda i: (i, 0))],
        out_specs=[row, nar],
        out_shape=[jax.ShapeDtypeStruct((t, w), BF16), jax.ShapeDtypeStruct((t, LANES), BF16)],
        compiler_params=_cparams("parallel"),
    )(dq, dk, dgates, pos)


A_TQ = 512
A_TK = 512


A_HEADS_PER_STEP = 2
A_HEADS_PER_STEP_FWD = 4


def _flash_fwd(q, k, v, name):
    t = q.shape[0]
    tq, tk = A_TQ, A_TK
    nh = A_HEADS_PER_STEP_FWD
    wblk = nh * LANES

    def body(q_ref, k_ref, v_ref, o_ref, qb_ref):
        i = pl.program_id(1)
        lane = lax.broadcasted_iota(jnp.int32, (tq, LANES), 1)
        qpos = i * tq + lax.broadcasted_iota(jnp.int32, (tq, tk), 0)
        kofs = lax.broadcasted_iota(jnp.int32, (tq, tk), 1)
        qs = [q_ref[:, hh * LANES:(hh + 1) * LANES] for hh in range(nh)]

        def step(kb, carry, masked):
            rows = pl.ds(pl.multiple_of(kb * tk, tk), tk)
            new = []
            for hh in range(nh):
                m, acc = carry[hh]
                kh = k_ref[rows, hh * LANES:(hh + 1) * LANES]
                s = lax.dot_general(qs[hh], kh, _DIMS["nt"], preferred_element_type=F32)
                if masked:
                    s = jnp.where(kb * tk + kofs <= qpos, s, NEG_INF)
                m_new = jnp.maximum(m, jnp.max(s, axis=1, keepdims=True))
                p = jnp.exp(s - m_new)
                acc = jnp.exp(m - m_new) * acc + jnp.dot(p.astype(BF16), v_ref[rows, hh * LANES:(hh + 1) * LANES],
                                                          preferred_element_type=F32)
                new.append((m_new, acc))
            return tuple(new)

        carry = tuple((jnp.full((tq, 1), NEG_INF, F32), jnp.zeros((tq, LANES), F32)) for _ in range(nh))
        n_full = (i * tq) // tk
        carry = lax.fori_loop(0, n_full, functools.partial(step, masked=False), carry)
        for d in range(tq // tk):
            carry = step(n_full + d, carry, True)
        for hh in range(nh):
            m, acc = carry[hh]
            l = acc[:, A_BIAS_LANE_V:A_BIAS_LANE_V + 1]
            o_ref[:, hh * LANES:(hh + 1) * LANES] = (acc / l).astype(o_ref.dtype)
            hi, lo = _hi_lo(-(m + jnp.log(l)))
            qb_ref[:, hh * LANES:(hh + 1) * LANES] = jnp.where(
                lane == A_BIAS_LANE_K, hi, jnp.where(lane == A_BIAS_LANE_K + 1, lo, qs[hh]))

    blk = pl.BlockSpec((tq, wblk), lambda j, i: (i, j))
    res = pl.BlockSpec((t, wblk), lambda j, i: (0, j))
    out = jax.ShapeDtypeStruct((t, A_HEADS * LANES), BF16)
    return pl.pallas_call(
        body, name=name, grid=(A_HEADS // nh, t // tq), in_specs=[blk, res, res], out_specs=[blk, blk],
        out_shape=[out, out], compiler_params=_cparams("parallel", "parallel"),
    )(q, k, v)


def _attn_delta(do, o, name, tm=512):
    t, w = do.shape

    def body(do_ref, o_ref, out_ref):
        lane = lax.broadcasted_iota(jnp.int32, (tm, LANES), 1)
        for hd in range(A_HEADS):
            sl = slice(hd * LANES, (hd + 1) * LANES)
            d = do_ref[:, sl]
            delta = jnp.sum(jnp.where(lane < A_V, d.astype(F32) * o_ref[:, sl].astype(F32), 0.0), axis=1, keepdims=True)
            hi, lo = _hi_lo(-delta)
            out_ref[:, sl] = jnp.where(lane == A_BIAS_LANE_V, hi, jnp.where(lane == A_BIAS_LANE_V + 1, lo, d))

    row = pl.BlockSpec((tm, w), lambda i: (i, 0))
    return pl.pallas_call(
        body, name=name, grid=(t // tm,), in_specs=[row, row], out_specs=row,
        out_shape=jax.ShapeDtypeStruct((t, w), BF16), compiler_params=_cparams("parallel"),
    )(do, o)


def _flash_bwd(qb, k, v, doe, name):
    t = qb.shape[0]
    tq, tk = A_TQ, A_TK
    assert tq == tk
    nh = A_HEADS_PER_STEP
    wblk = nh * LANES
    nq = t // tq

    def body(q_ref, k_ref, v_ref, do_ref, dq_ref, dk_ref, dv_ref):
        kb = pl.program_id(1)

        @pl.when(kb == 0)
        def _():
            dq_ref[...] = jnp.zeros_like(dq_ref)

        kpos = kb * tk + lax.broadcasted_iota(jnp.int32, (tk, tq), 0)
        qofs = lax.broadcasted_iota(jnp.int32, (tk, tq), 1)
        ks = [k_ref[:, hh * LANES:(hh + 1) * LANES] for hh in range(nh)]
        vs = [v_ref[:, hh * LANES:(hh + 1) * LANES] for hh in range(nh)]

        def step(qi, carry, masked):
            rows = pl.ds(pl.multiple_of(qi * tq, tq), tq)
            new = []
            for hh in range(nh):
                dk_acc, dv_acc = carry[hh]
                qh = q_ref[rows, hh * LANES:(hh + 1) * LANES]
                doh = do_ref[rows, hh * LANES:(hh + 1) * LANES]
                p_t = jnp.exp(lax.dot_general(ks[hh], qh, _DIMS["nt"], preferred_element_type=F32))
                if masked:
                    p_t = jnp.where(qi * tq + qofs >= kpos, p_t, 0.0)
                ds_t = (p_t * lax.dot_general(vs[hh], doh, _DIMS["nt"], preferred_element_type=F32)).astype(BF16)
                dv_acc = dv_acc + jnp.dot(p_t.astype(BF16), doh, preferred_element_type=F32)
                dk_acc = dk_acc + jnp.dot(ds_t, qh, preferred_element_type=F32)
                dq_ref[rows, hh * LANES:(hh + 1) * LANES] += lax.dot_general(ds_t, ks[hh], _DIMS["tn"],
                                                                             preferred_element_type=F32)
                new.append((dk_acc, dv_acc))
            return tuple(new)

        carry = tuple((jnp.zeros((tk, LANES), F32), jnp.zeros((tk, LANES), F32)) for _ in range(nh))
        carry = step(kb, carry, True)
        carry = lax.fori_loop(kb + 1, nq, functools.partial(step, masked=False), carry)
        for hh in range(nh):
            dk_ref[:, hh * LANES:(hh + 1) * LANES] = carry[hh][0].astype(dk_ref.dtype)
            dv_ref[:, hh * LANES:(hh + 1) * LANES] = carry[hh][1].astype(dv_ref.dtype)

    res = pl.BlockSpec((t, wblk), lambda j, kb: (0, j))
    blk = pl.BlockSpec((tk, wblk), lambda j, kb: (kb, j))
    wide = (t, A_HEADS * LANES)
    return pl.pallas_call(
        body, name=name, grid=(A_HEADS // nh, t // tk), in_specs=[res, blk, blk, res], out_specs=[res, blk, blk],
        out_shape=[jax.ShapeDtypeStruct(wide, F32), jax.ShapeDtypeStruct(wide, BF16), jax.ShapeDtypeStruct(wide, BF16)],
        compiler_params=_cparams("parallel", "arbitrary"),
    )(qb, k, v, doe)


_SPLITS = (M_WIDTH, M_WIDTH, M_WIDTH, M_WIDTH, M_HEADS, M_HEADS, Q_RANK, KV_RANK, A_ROPE)
_OFFS = tuple(sum(_SPLITS[:i]) for i in range(len(_SPLITS) + 1))
_GATE_BLOCK_KR = A_NOPE


def _split_w_in_t(wt):
    z = lambda n: jnp.zeros((n, wt.shape[1]), wt.dtype)
    w_g = jnp.concatenate([wt[_OFFS[4]:_OFFS[6]], z(_GATE_BLOCK_KR - 2 * M_HEADS), wt[_OFFS[8]:_OFFS[9]],
                           z(LANES - _GATE_BLOCK_KR - A_ROPE)], axis=0)
    return wt[_OFFS[6]:_OFFS[8]], w_g


def _merge_w_in_grad_t(g_m4, g_a2, g_g):
    return jnp.concatenate([g_m4, g_g[:2 * M_HEADS], g_a2, g_g[_GATE_BLOCK_KR:_GATE_BLOCK_KR + A_ROPE]], axis=0)


def _pad_heads(w, used):
    w3 = w.reshape(A_HEADS, used, w.shape[1])
    return jnp.pad(w3, ((0, 0), (0, LANES - used), (0, 0))).reshape(A_HEADS * LANES, w.shape[1])


def _unpad_heads(g, used):
    return g.reshape(A_HEADS, LANES, g.shape[1])[:, :used].reshape(A_HEADS * used, g.shape[1])


def _split_w_kv_b_t(wt):
    w3 = wt.reshape(A_HEADS, A_NOPE + A_V, wt.shape[1])
    pad = lambda part: jnp.pad(part, ((0, 0), (0, LANES - part.shape[1]), (0, 0))).reshape(A_HEADS * LANES, wt.shape[1])
    return pad(w3[:, :A_NOPE]), pad(w3[:, A_NOPE:])


def _merge_w_kv_b_grad_t(gk, gv):
    gk3 = gk.reshape(A_HEADS, LANES, gk.shape[1])[:, :A_NOPE]
    gv3 = gv.reshape(A_HEADS, LANES, gv.shape[1])[:, :A_V]
    return jnp.concatenate([gk3, gv3], axis=1).reshape(A_HEADS * (A_NOPE + A_V), gk.shape[1])


def _local_step(x, pos, target, nw, w, start_late_reduce):
    w_in = w["w_in"]
    w_a2, w_g = _split_w_in_t(w_in)
    w_qb = _pad_heads(w["w_q_b"], A_NOPE + A_ROPE)
    w_k, w_v = _split_w_kv_b_t(w["w_kv_b"])
    w_out_m, w_out_a = w["w_out"][:M_WIDTH], _pad_heads(w["w_out"][M_WIDTH:], A_V)
    w_gate, w_up, w_down = w["w_gate"], w["w_up"], w["w_down"]
    bias = jnp.pad(nw["b_gates"], ((0, 0), (0, LANES - 2 * M_HEADS)))
    mnorm = nw["mlstm_norm_w"].reshape(M_HEADS, 1, M_HEAD_DIM)
    n_m4 = 4 * M_WIDTH
    t_len = x.shape[0]

    u1 = _rmsnorm_fwd(x, nw["attn_norm_w"], "attn_norm")
    m4 = _matmul(u1, w_in, "nt", BF16, "proj_mlstm", b_rows=n_m4, lead_scale=(M_WIDTH, M_SCALE))
    a2 = _matmul(u1, w_a2, "nt", BF16, "proj_latents", tn=640)
    gates = _matmul(u1, w_g, "nt", F32, "proj_gates")
    q_t = _chunk_major_t(_matmul(w_in[:M_WIDTH], u1, "nt", BF16, "proj_q_t", lead_scale=(t_len, M_SCALE)))
    k_t = _chunk_major_t(_matmul(w_in[M_WIDTH:2 * M_WIDTH], u1, "nt", BF16, "proj_k_t"))
    hm, c_prev_all, nm_all = _mlstm_fwd(m4, k_t, gates, bias, mnorm, "mlstm_fwd")
    qa, kva = a2[:, :Q_RANK], a2[:, Q_RANK:]
    qa_n = _rmsnorm_fwd(qa, nw["q_a_norm_w"], "q_a_norm")
    kv_n = _rmsnorm_fwd(kva, nw["kv_a_norm_w"], "kv_a_norm")
    qr, kr, vv = _qkv_up_rope(qa_n, kv_n, w_qb, w_k, w_v, gates, pos, "qkv_up_rope")
    ha, qb = _flash_fwd(qr, kr, vv, "attn_fwd")
    h1 = _matmul([hm, ha], [w_out_m, w_out_a], "nn", F32, "out_proj", residual=x)
    u2 = _rmsnorm_fwd(h1, nw["ffn_norm_w"], "ffn_norm")
    gg, uu, act = _ffn_gate_up(u2, w_gate, w_up, "ffn_gate_up")
    h2 = _matmul(act, w_down, "nn", F32, "ffn_down", residual=h1)

    dh2, dh2_b, loss, g_final = _final_loss(h2, target, nw["final_norm_w"].reshape(1, D_MODEL), "final_loss")
    grads = {"final_norm_w": g_final.reshape(D_MODEL)}
    grads_b = {}
    dgg, duu = _ffn_down_bwd(dh2_b, w_down, gg, uu, "ffn_down_bwd")
    grads["w_down"], grads_b["w_down"] = _matmul(act, dh2_b, "tn", F32, "g_w_down", tm=256, tn=D_MODEL, also_bf16=True)
    du2 = _matmul([dgg, duu], [w_gate, w_up], "nn", F32, "d_u2", tm=1024, tn=512)
    grads["w_gate"], grads_b["w_gate"] = _matmul(dgg, u2, "tn", F32, "g_w_gate", tm=256, tn=D_MODEL, also_bf16=True)
    grads["w_up"], grads_b["w_up"] = _matmul(duu, u2, "tn", F32, "g_w_up", tm=256, tn=D_MODEL, also_bf16=True)
    dh1, dh1_b, grads["ffn_norm_w"] = _rmsnorm_bwd(du2, h1, nw["ffn_norm_w"], "ffn_norm_bwd", F32, residual=dh2,
                                                   also_bf16=True)
    dhm = _matmul(dh1_b, w_out_m, "nt", BF16, "d_hm")
    dha = _matmul(dh1_b, w_out_a, "nt", BF16, "d_ha")
    grads["w_out"] = jnp.concatenate([_matmul(hm, dh1_b, "tn", F32, "g_w_out_m", tn=D_MODEL),
                                      _unpad_heads(_matmul(ha, dh1_b, "tn", F32, "g_w_out_a", tn=D_MODEL), A_V)], axis=0)

    late, dha = start_late_reduce(grads, grads_b, dha)

    dqr, dkr, dvv = _flash_bwd(qb, kr, vv, _attn_delta(dha, ha, "attn_delta"), "attn_bwd")
    dm4, dgates, small = _mlstm_bwd(m4, q_t, gates, bias, mnorm, c_prev_all, nm_all, dhm, "mlstm_bwd")
    grads["mlstm_norm_w"] = small[:M_HEADS].reshape(1, M_HEADS, M_HEAD_DIM)
    grads["b_gates"] = small[M_HEADS:M_HEADS + 1, :2 * M_HEADS]
    dqpre, dgk = _rope_bwd(dqr, dkr, dgates, pos, "rope_bwd")
    dqa_n = _matmul(dqpre, w_qb, "nn", BF16, "d_qa_n")
    grads["w_q_b"] = _unpad_heads(_matmul(dqpre, qa_n, "tn", F32, "g_w_q_b"), A_NOPE + A_ROPE)
    dkv_n = _matmul([dkr, dvv], [w_k, w_v], "nn", BF16, "d_kv_n")
    grads["w_kv_b"] = _merge_w_kv_b_grad_t(_matmul(dkr, kv_n, "tn", F32, "g_w_k"),
                                           _matmul(dvv, kv_n, "tn", F32, "g_w_v"))
    dqa, grads["q_a_norm_w"] = _rmsnorm_bwd(dqa_n, qa, nw["q_a_norm_w"], "q_a_norm_bwd", BF16)
    dkva, grads["kv_a_norm_w"] = _rmsnorm_bwd(dkv_n, kva, nw["kv_a_norm_w"], "kv_a_norm_bwd", BF16)
    da2 = jnp.concatenate([dqa, dkva], axis=1)
    du1 = _matmul([dm4, da2, dgk], [w_in, w_a2, w_g], "nn", F32, "d_u1", b_rows=[n_m4, None, None])
    grads["w_in"] = _merge_w_in_grad_t(_matmul(dm4, u1, "tn", F32, "g_w_m4", tn=D_MODEL),
                                       _matmul(da2, u1, "tn", F32, "g_w_a2", tm=640, tn=D_MODEL),
                                       _matmul(dgk, u1, "tn", F32, "g_w_g", tn=D_MODEL))
    grad_x, grads["attn_norm_w"] = _rmsnorm_bwd(du1, x, nw["attn_norm_w"], "attn_norm_bwd", F32, residual=dh1)
    return loss, grad_x, grads, grads_b, late


MESH = pl.DeviceIdType.MESH
N_CHIPS = 4
EARLY = ("w_in", "w_q_b", "w_kv_b")
LATE = ("w_out", "w_gate", "w_up", "w_down")
BIG = EARLY + LATE
TRANSPOSED = ("w_in", "w_q_b", "w_kv_b", "w_gate", "w_up")
LANE_HALVED = ("w_in", "w_out", "w_gate", "w_up", "w_down")
SMALL = ("attn_norm_w", "b_gates", "mlstm_norm_w", "q_a_norm_w", "kv_a_norm_w", "ffn_norm_w", "final_norm_w")
HBM_SPEC = pl.BlockSpec(memory_space=pltpu.HBM)


def _stored(name, a):
    return a[0].T if name in TRANSPOSED else a[0]


def _unstored(name, a):
    return (a.T if name in TRANSPOSED else a)[None]


def _half_shape(name, shape):
    rs, cs = shape
    return (rs, cs // 2) if name in LANE_HALVED else (rs // 2, cs)


def _half(ref, name, h, *lead):
    rs, cs = ref.shape[-2:]
    if name in LANE_HALVED:
        return ref.at[(*lead, slice(None), pl.ds(h * (cs // 2), cs // 2))]
    return ref.at[(*lead, pl.ds(h * (rs // 2), rs // 2), slice(None))]


def _place():
    x, y, c = lax.axis_index("x"), lax.axis_index("y"), lax.axis_index("c")
    others = [(1 - x, y), (x, 1 - y), (1 - x, 1 - y)]
    return x, y, c, others


def _gather_weights(names, shards):
    n = len(shards)

    def body(*refs):
        _gather_body(names, refs[:n], refs[n:2 * n], *refs[2 * n:])

    gathered = pl.pallas_call(
        body, name="gather_weights", in_specs=[HBM_SPEC] * n, out_specs=[HBM_SPEC] * n,
        out_shape=[jax.ShapeDtypeStruct((N_CHIPS,) + s.shape, s.dtype) for s in shards],
        scratch_shapes=[pltpu.SemaphoreType.DMA((6 * n,)), pltpu.SemaphoreType.DMA((6 * n,))],
    )(*shards)
    return _with_own_slab(gathered, shards)


def _with_own_slab(gathered, shards):
    me = 2 * lax.axis_index("x") + lax.axis_index("y")
    return [lax.dynamic_update_slice(g, s[None], (me, 0, 0)) for g, s in zip(gathered, shards)]


def _gather_body(names, ins, outs, send_sems, recv_sems):
    x, y, c, others = _place()
    me = 2 * x + y
    sibling = (x, y, 1 - c)

    def copy(w, k, slab, core, to, src=None):
        dst = _half(outs[w], names[w], core, slab)
        return pltpu.make_async_remote_copy(
            src_ref=dst if src is None else src, dst_ref=dst, send_sem=send_sems.at[w * 6 + k],
            recv_sem=recv_sems.at[w * 6 + k], device_id=to, device_id_type=MESH)

    sends = []
    for w in range(len(names)):
        for j, chip in enumerate(others):
            cp = copy(w, j, me, c, (*chip, c), src=_half(ins[w], names[w], c))
            cp.start()
            sends.append(cp)
    for w in range(len(names)):
        for j, (ox, oy) in enumerate(others):
            slab = 2 * ox + oy
            copy(w, j, slab, c, (x, y, c)).wait_recv()
            fwd = copy(w, 3 + j, slab, c, sibling)
            fwd.start()
            sends.append(fwd)
    for w in range(len(names)):
        for j, (ox, oy) in enumerate(others):
            copy(w, 3 + j, 2 * ox + oy, 1 - c, (x, y, c)).wait_recv()
    for cp in sends:
        cp.wait_send()


GATHER_LATE_COLLECTIVE_ID = 1


def _gather_weights_async(names, shards):
    n = len(shards)
    src = [jax.new_ref(s, memory_space=pltpu.MemorySpace.HBM) for s in shards]
    dst = [jax.empty_ref(jax.ShapeDtypeStruct((N_CHIPS,) + s.shape, s.dtype), memory_space=pltpu.MemorySpace.HBM)
           for s in shards]

    @pl.kernel(mesh=plsc.ScalarSubcoreMesh(axis_name="sequencer", num_cores=1), name="gather_weights_async",
               scratch_types=(pltpu.SemaphoreType.DMA((6 * n,)), pltpu.SemaphoreType.DMA((6 * n,))),
               compiler_params=pltpu.CompilerParams(collective_id=GATHER_LATE_COLLECTIVE_ID))
    def launch(send_sems, recv_sems):
        x, y, c, others = _place()
        peers = [(ox, oy, c) for ox, oy in others] + [(x, y, 1 - c)]
        barrier = pltpu.get_barrier_semaphore()
        for peer in peers:
            pl.semaphore_signal(barrier, inc=1, device_id=peer, device_id_type=MESH)
        pl.semaphore_wait(barrier, len(peers))
        _gather_body(names, src, dst, send_sems, recv_sems)

    launch()
    return _with_own_slab([d[...] for d in dst], shards)


def _exchange(arrays, out_shapes, plan, copies_per_array, name):
    n = len(arrays)

    def body(*refs):
        ins, outs = refs[:n], refs[n:2 * n]
        send_sems, recv_sems = refs[2 * n:]
        rem = [pltpu.make_async_remote_copy(src_ref=s, dst_ref=d, send_sem=send_sems.at[k], recv_sem=recv_sems.at[k],
                                            device_id=to, device_id_type=MESH)
               for k, (s, d, to) in enumerate(plan(ins, outs, _place()))]
        for cp in rem:
            cp.start()
        for cp in rem:
            cp.wait_recv()
        for cp in rem:
            cp.wait_send()

    return pl.pallas_call(
        body, name=name, in_specs=[HBM_SPEC] * n, out_specs=[HBM_SPEC] * n, out_shape=out_shapes,
        scratch_shapes=[pltpu.SemaphoreType.DMA((copies_per_array * n,)),
                        pltpu.SemaphoreType.DMA((copies_per_array * n,))],
    )(*arrays)


def _plan_to_sibling(ins, outs, place):
    x, y, c, _ = place
    return [(ins[w], outs[w], (x, y, 1 - c)) for w in range(len(ins))]


def _plan_sibling_halves(names, ins, outs, place):
    x, y, c, _ = place
    return [(_half(ins[w], names[w], 1 - c, slice(None)), outs[w], (x, y, 1 - c)) for w in range(len(ins))]


def _plan_chip_partials(ins, outs, place):
    x, y, c, others = place
    return [(ins[w].at[2 * ox + oy], outs[w].at[j], (ox, oy, c))
            for w in range(len(ins)) for j, (ox, oy) in enumerate(others)]


def _chip_partial(g3, got, name, core):
    hs = tuple(got.shape[1:])
    if name in LANE_HALVED:
        g_in, g_spec = g3, pl.BlockSpec((1,) + hs, lambda s, c_ref: (s, 0, c_ref[0]))
    else:
        g_in, g_spec = g3.reshape((N_CHIPS, 2) + hs), pl.BlockSpec((1, 1) + hs, lambda s, c_ref: (s, c_ref[0], 0, 0))
    blk = pl.BlockSpec((1,) + hs, lambda s, c_ref: (s, 0, 0))

    def body(c_ref, g_ref, r_ref, o_ref):
        o_ref[0] = (g_ref[(0,) * (len(g_ref.shape) - 2)] + r_ref[0].astype(F32)).astype(o_ref.dtype)

    return pl.pallas_call(
        body, name="chip_partial_%s" % name,
        grid_spec=pltpu.PrefetchScalarGridSpec(num_scalar_prefetch=1, grid=(N_CHIPS,), in_specs=[g_spec, blk],
                                               out_specs=blk),
        out_shape=jax.ShapeDtypeStruct((N_CHIPS,) + hs, BF16), compiler_params=_cparams("parallel"),
    )(core, g_in, got)


def _shard_sum(part, got3, name, me):
    hs = tuple(part.shape[1:])

    def body(me_ref, p_ref, a_ref, b_ref, c_ref, o_ref):
        o_ref[...] = ((p_ref[0].astype(F32) + a_ref[0].astype(F32)) + b_ref[0].astype(F32)) + c_ref[0].astype(F32)

    slot = lambda j: pl.BlockSpec((1,) + hs, lambda i, me_ref: (j, 0, 0))
    return pl.pallas_call(
        body, name="shard_sum_%s" % name,
        grid_spec=pltpu.PrefetchScalarGridSpec(
            num_scalar_prefetch=1, grid=(1,),
            in_specs=[pl.BlockSpec((1,) + hs, lambda i, me_ref: (me_ref[0], 0, 0)), slot(0), slot(1), slot(2)],
            out_specs=pl.BlockSpec(hs, lambda i, me_ref: (0, 0))),
        out_shape=jax.ShapeDtypeStruct(hs, F32), compiler_params=_cparams("arbitrary"),
    )(me, part, got3, got3, got3)


N_PEERS = 7


def _shard_sum_direct(g3, got7, name, where):
    hs = tuple(got7.shape[1:])
    if name in LANE_HALVED:
        g_in, g_spec = g3, pl.BlockSpec((1,) + hs, lambda i, p: (p[0], 0, p[1]))
    else:
        g_in, g_spec = g3.reshape((N_CHIPS, 2) + hs), pl.BlockSpec((1, 1) + hs, lambda i, p: (p[0], p[1], 0, 0))

    def body(p_ref, g_ref, r_ref, o_ref):
        acc = g_ref[(0,) * (len(g_ref.shape) - 2)]
        for k in range(N_PEERS):
            acc = acc + r_ref[k].astype(F32)
        o_ref[...] = acc

    return pl.pallas_call(
        body, name="shard_sum_%s" % name,
        grid_spec=pltpu.PrefetchScalarGridSpec(
            num_scalar_prefetch=1, grid=(1,),
            in_specs=[g_spec, pl.BlockSpec((N_PEERS,) + hs, lambda i, p: (0, 0, 0))],
            out_specs=pl.BlockSpec(hs, lambda i, p: (0, 0))),
        out_shape=jax.ShapeDtypeStruct(hs, F32), compiler_params=_cparams("arbitrary"),
    )(where, g_in, got7)


def _adamw_halves(w, m, v, mine, other, name, core):
    hs = tuple(mine.shape)
    full = pl.BlockSpec(hs, (lambda h, c_ref: (0, h)) if name in LANE_HALVED else (lambda h, c_ref: (h, 0)))
    half = pl.BlockSpec(hs, lambda h, c_ref: (0, 0))

    def body(c_ref, w_ref, m_ref, v_ref, a_ref, b_ref, g_ref, d_ref, mo_ref, vo_ref):
        g = jnp.where(pl.program_id(0) == c_ref[0], a_ref[...], b_ref[...])
        delta, m_new, v_new = _adamw_math(w_ref[...], g, m_ref[...], v_ref[...])
        g_ref[...] = g
        d_ref[...] = delta
        mo_ref[...] = m_new
        vo_ref[...] = v_new

    out = jax.ShapeDtypeStruct(w.shape, F32)
    return pl.pallas_call(
        body, name="adamw_%s" % name,
        grid_spec=pltpu.PrefetchScalarGridSpec(num_scalar_prefetch=1, grid=(2,), in_specs=[full, full, full, half, half],
                                               out_specs=[full] * 4),
        out_shape=[out] * 4, compiler_params=_cparams("parallel"),
    )(core, w, m, v, mine, other)


def _adamw_math(w, g, m, v):
    m = ADAM_B1 * m + (1.0 - ADAM_B1) * g
    v = ADAM_B2 * v + (1.0 - ADAM_B2) * (g * g)
    m_hat = m / (1.0 - ADAM_B1 ** ADAM_STEP)
    v_hat = v / (1.0 - ADAM_B2 ** ADAM_STEP)
    delta = -ADAM_LR * (m_hat / (jnp.sqrt(v_hat) + ADAM_EPS) + ADAM_WD * w)
    return delta, m, v


def _core_index():
    return lax.axis_index("c").astype(jnp.int32).reshape(1)


def _chip_partials_of(names, grads, grads_b, tag):
    by_shard = lambda g: g.reshape(N_CHIPS, g.shape[0] // N_CHIPS, g.shape[1])
    g3 = [by_shard(grads[n]) for n in names]
    g3_b = [by_shard(grads_b[n]) if n in grads_b else g.astype(BF16) for n, g in zip(names, g3)]
    shapes = [jax.ShapeDtypeStruct((N_CHIPS,) + _half_shape(n, g.shape[1:]), BF16) for n, g in zip(names, g3)]
    got = _exchange(g3_b, shapes, functools.partial(_plan_sibling_halves, names), 1, "sibling_halves_" + tag)
    return [_chip_partial(g, r, n, _core_index()) for n, g, r in zip(names, g3, got)]


REDUCE_COLLECTIVE_ID = {"late": 2, "early": 3}


def _exchange_partials_async(part, tag):
    n = len(part)
    src = [jax.new_ref(p, memory_space=pltpu.MemorySpace.HBM) for p in part]
    dst = [jax.empty_ref(jax.ShapeDtypeStruct((3,) + p.shape[1:], p.dtype), memory_space=pltpu.MemorySpace.HBM)
           for p in part]

    @pl.kernel(mesh=plsc.ScalarSubcoreMesh(axis_name="sequencer", num_cores=1), name="chip_partials_async_" + tag,
               scratch_types=(pltpu.SemaphoreType.DMA((3 * n,)), pltpu.SemaphoreType.DMA((3 * n,))),
               compiler_params=pltpu.CompilerParams(collective_id=REDUCE_COLLECTIVE_ID[tag]))
    def launch(send_sems, recv_sems):
        place = _place()
        x, y, c, others = place
        barrier = pltpu.get_barrier_semaphore()
        for ox, oy in others:
            pl.semaphore_signal(barrier, inc=1, device_id=(ox, oy, c), device_id_type=MESH)
        pl.semaphore_wait(barrier, len(others))
        rem = [pltpu.make_async_remote_copy(src_ref=s, dst_ref=d, send_sem=send_sems.at[k], recv_sem=recv_sems.at[k],
                                            device_id=to, device_id_type=MESH)
               for k, (s, d, to) in enumerate(_plan_chip_partials(src, dst, place))]
        for cp in rem:
            cp.start()
        for cp in rem:
            cp.wait_recv()
        for cp in rem:
            cp.wait_send()

    launch()
    return [d[...] for d in dst]


DIRECT_REDUCE_COLLECTIVE_ID = 4


def _exchange_contributions_async(names, gb3):
    n = len(gb3)
    src = [jax.new_ref(g, memory_space=pltpu.MemorySpace.HBM) for g in gb3]
    dst = [jax.empty_ref(jax.ShapeDtypeStruct((N_PEERS,) + _half_shape(nm, g.shape[1:]), g.dtype),
                         memory_space=pltpu.MemorySpace.HBM) for nm, g in zip(names, gb3)]

    @pl.kernel(mesh=plsc.ScalarSubcoreMesh(axis_name="sequencer", num_cores=1), name="contributions_async",
               scratch_types=(pltpu.SemaphoreType.DMA((N_PEERS * n,)), pltpu.SemaphoreType.DMA((N_PEERS * n,))),
               compiler_params=pltpu.CompilerParams(collective_id=DIRECT_REDUCE_COLLECTIVE_ID))
    def launch(send_sems, recv_sems):
        x, y, c, _ = _place()
        peers = [(x ^ ((k >> 2) & 1), y ^ ((k >> 1) & 1), c ^ (k & 1)) for k in range(1, N_PEERS + 1)]
        barrier = pltpu.get_barrier_semaphore()
        for peer in peers:
            pl.semaphore_signal(barrier, inc=1, device_id=peer, device_id_type=MESH)
        pl.semaphore_wait(barrier, N_PEERS)
        rem = []
        for w in range(n):
            for k, (px, py, pc) in enumerate(peers):
                rem.append(pltpu.make_async_remote_copy(
                    src_ref=_half(src[w], names[w], pc, 2 * px + py), dst_ref=dst[w].at[k],
                    send_sem=send_sems.at[N_PEERS * w + k], recv_sem=recv_sems.at[N_PEERS * w + k],
                    device_id=(px, py, pc), device_id_type=MESH))
        for cp in rem:
            cp.start()
        for cp in rem:
            cp.wait_recv()
        for cp in rem:
            cp.wait_send()

    launch()
    return [d[...] for d in dst]


def _shard_index():
    return (2 * lax.axis_index("x") + lax.axis_index("y")).astype(jnp.int32).reshape(1)


def _with_sibling_half(names, mine, tag):
    other = _exchange(mine, [jax.ShapeDtypeStruct(m.shape, F32) for m in mine], _plan_to_sibling, 1,
                      "sibling_result_" + tag)
    return {n: (a, b) for n, a, b in zip(names, mine, other)}


SMALL_ROWS = 8
SMALL_LAYOUT = {"attn_norm_w": (0, 0, 1024), "ffn_norm_w": (1, 0, 1024), "final_norm_w": (2, 0, 1024),
                "q_a_norm_w": (3, 0, 384), "kv_a_norm_w": (3, 384, 256), "mlstm_norm_w": (4, 0, 512),
                "b_gates": (4, 512, 8)}
LOSS_SLOT = (5, 0)


def _pack_small(vals, loss=None):
    tile = jnp.zeros((SMALL_ROWS, D_MODEL), F32)
    for n, (r, c0, width) in SMALL_LAYOUT.items():
        tile = tile.at[r, c0:c0 + width].set(vals[n].reshape(width).astype(F32))
    if loss is not None:
        tile = tile.at[LOSS_SLOT[0], LOSS_SLOT[1]].set(loss)
    return tile


def _unpack_small(tile, shapes):
    return {n: tile[r, c0:c0 + width].reshape(shapes[n]) for n, (r, c0, width) in SMALL_LAYOUT.items()}


def _small_allreduce_adamw(g_tile, w_tile, m_tile, v_tile):
    def body(g_ref, w_ref, m_ref, v_ref, gsum_ref, d_ref, mo_ref, vo_ref, slots, send_sems, recv_sems):
        x, y, c, _ = _place()
        me = 4 * x + 2 * y + c
        slots[me] = g_ref[...]
        copies = []
        for k in range(1, 8):
            dx, dy, dc = (k >> 2) & 1, (k >> 1) & 1, k & 1
            to = (x ^ dx, y ^ dy, c ^ dc)
            cp = pltpu.make_async_remote_copy(src_ref=g_ref, dst_ref=slots.at[me], send_sem=send_sems.at[k - 1],
                                              recv_sem=recv_sems.at[k - 1], device_id=to, device_id_type=MESH)
            cp.start()
            copies.append(cp)
        for k in range(1, 8):
            src = me ^ k
            pltpu.make_async_remote_copy(src_ref=g_ref, dst_ref=slots.at[src], send_sem=send_sems.at[k - 1],
                                         recv_sem=recv_sems.at[k - 1], device_id=(x, y, c),
                                         device_id_type=MESH).wait_recv()
        for cp in copies:
            cp.wait_send()
        total = slots[0]
        for d in range(1, 8):
            total = total + slots[d]
        gsum_ref[...] = total
        delta, m_new, v_new = _adamw_math(w_ref[...], total, m_ref[...], v_ref[...])
        d_ref[...] = delta
        mo_ref[...] = m_new
        vo_ref[...] = v_new

    vm = pl.BlockSpec(memory_space=pltpu.VMEM)
    tile = jax.ShapeDtypeStruct((SMALL_ROWS, D_MODEL), F32)
    return pl.pallas_call(
        body, name="small_allreduce_adamw", in_specs=[vm] * 4, out_specs=[vm] * 4, out_shape=[tile] * 4,
        scratch_shapes=[pltpu.VMEM((8, SMALL_ROWS, D_MODEL), F32), pltpu.SemaphoreType.DMA((7,)),
                        pltpu.SemaphoreType.DMA((7,))],
    )(g_tile, w_tile, m_tile, v_tile)


def kernel(x, positions, attn_norm_w, w_in, b_gates, mlstm_norm_w, q_a_norm_w, w_q_b, kv_a_norm_w, w_kv_b, w_out, ffn_norm_w, w_gate, w_up, w_down, final_norm_w, loss_target, m_attn_norm_w, m_w_in, m_b_gates, m_mlstm_norm_w, m_q_a_norm_w, m_w_q_b, m_kv_a_norm_w, m_w_kv_b, m_w_out, m_ffn_norm_w, m_w_gate, m_w_up, m_w_down, m_final_norm_w, v_attn_norm_w, v_w_in, v_b_gates, v_mlstm_norm_w, v_q_a_norm_w, v_w_q_b, v_kv_a_norm_w, v_w_kv_b, v_w_out, v_ffn_norm_w, v_w_gate, v_w_up, v_w_down, v_final_norm_w):
    names = ("attn_norm_w", "w_in", "b_gates", "mlstm_norm_w", "q_a_norm_w", "w_q_b", "kv_a_norm_w", "w_kv_b", "w_out",
             "ffn_norm_w", "w_gate", "w_up", "w_down", "final_norm_w")
    wts = dict(zip(names, (attn_norm_w, w_in, b_gates, mlstm_norm_w, q_a_norm_w, w_q_b, kv_a_norm_w, w_kv_b, w_out,
                           ffn_norm_w, w_gate, w_up, w_down, final_norm_w)))
    mom = dict(zip(names, (m_attn_norm_w, m_w_in, m_b_gates, m_mlstm_norm_w, m_q_a_norm_w, m_w_q_b, m_kv_a_norm_w,
                           m_w_kv_b, m_w_out, m_ffn_norm_w, m_w_gate, m_w_up, m_w_down, m_final_norm_w)))
    vel = dict(zip(names, (v_attn_norm_w, v_w_in, v_b_gates, v_mlstm_norm_w, v_q_a_norm_w, v_w_q_b, v_kv_a_norm_w,
                           v_w_kv_b, v_w_out, v_ffn_norm_w, v_w_gate, v_w_up, v_w_down, v_final_norm_w)))
    t = x.shape[1]

    shards = {n: _stored(n, wts[n]).astype(BF16) for n in BIG}
    early = _gather_weights(EARLY, [shards[n] for n in EARLY])
    early, late_in = lax.optimization_barrier((early, [shards[n] for n in LATE]))
    late = _gather_weights_async(LATE, late_in)
    full = {n: g.reshape(N_CHIPS * g.shape[1], g.shape[2]) for n, g in zip(EARLY + LATE, early + late)}

    nw = {n: wts[n] for n in SMALL}
    by_shard = lambda g: g.reshape(N_CHIPS, g.shape[0] // N_CHIPS, g.shape[1])

    def start_late_reduce(grads, grads_b, marker):
        g3 = [by_shard(grads[n]) for n in LATE]
        gb3 = [by_shard(grads_b[n]) if n in grads_b else g.astype(BF16) for n, g in zip(LATE, g3)]
        gb3, marker = lax.optimization_barrier((gb3, marker))
        return (g3, _exchange_contributions_async(LATE, gb3)), marker

    loss, grad_x, grads, grads_b, (late_g3, late_got7) = _local_step(x[0], positions.reshape(t, 1), loss_target[0],
                                                                     nw, full, start_late_reduce)

    early_part = _chip_partials_of(EARLY, grads, grads_b, "early")
    early_part, late_got7 = lax.optimization_barrier((list(early_part), list(late_got7)))
    early_got3 = _exchange_partials_async(early_part, "early")
    core, shard = _core_index(), _shard_index()
    outs_g, outs_d, outs_m, outs_v = {}, {}, {}, {}

    def finish(group, mine, tag):
        halves = _with_sibling_half(group, mine, tag)
        for n in group:
            res = _adamw_halves(_stored(n, wts[n]), _stored(n, mom[n]), _stored(n, vel[n]), *halves[n], n, core)
            outs_g[n], outs_d[n], outs_m[n], outs_v[n] = [_unstored(n, r) for r in res]

    where = jnp.concatenate([shard, core])
    finish(LATE, [_shard_sum_direct(g, r7, n, where) for n, g, r7 in zip(LATE, late_g3, late_got7)], "late")
    shapes = {n: wts[n].shape for n in SMALL}
    g_tile = _pack_small({n: grads[n] for n in SMALL}, loss=loss[0, 0])
    gsum, d_tile, m_tile, v_tile = _small_allreduce_adamw(g_tile, _pack_small(wts), _pack_small(mom), _pack_small(vel))
    finish(EARLY, [_shard_sum(p, r3, n, shard) for n, p, r3 in zip(EARLY, early_part, early_got3)], "early")
    outs_g.update(_unpack_small(gsum, shapes))
    outs_d.update(_unpack_small(d_tile, shapes))
    outs_m.update(_unpack_small(m_tile, shapes))
    outs_v.update(_unpack_small(v_tile, shapes))
    total_loss = gsum[LOSS_SLOT[0], LOSS_SLOT[1]]
    return (total_loss, grad_x[None], *[outs_g[n] for n in names], *[outs_d[n] for n in names],
            *[outs_m[n] for n in names], *[outs_v[n] for n in names])
```

```python
import functools
import math

import jax
import jax.numpy as jnp
from jax import lax
from jax.experimental import pallas as pl
from jax.experimental.pallas import tpu as pltpu
from jax.experimental.pallas import tpu_sc as plsc

F32 = jnp.float32
BF16 = jnp.bfloat16

D_MODEL = 1024
M_HEADS = 4
M_HEAD_DIM = 128
M_WIDTH = M_HEADS * M_HEAD_DIM
M_CHUNK = 64
A_HEADS = 8
A_NOPE = 64
A_ROPE = 32
A_V = 64
A_WIDTH = A_HEADS * A_V
A_QK_PAD = 128
Q_RANK = 384
KV_RANK = 256
ROPE_THETA = 10000.0
D_FF = 2816
D_IN = 2728
EPS = 1e-6
ATTN_SCALE = (A_NOPE + A_ROPE) ** -0.5
M_SCALE = M_HEAD_DIM ** -0.5

ADAM_LR = 0.001
ADAM_B1 = 0.9
ADAM_B2 = 0.999
ADAM_EPS = 1e-08
ADAM_WD = 0.01
ADAM_STEP = 10

VMEM_LIMIT_BYTES = 56 * 1024 * 1024
LANES = 128
NEG_INF = float("-inf")


def _cparams(*sem):
    return pltpu.CompilerParams(dimension_semantics=sem if sem else None, vmem_limit_bytes=VMEM_LIMIT_BYTES)


_DIMS = {"nn": (((1,), (0,)), ((), ())), "nt": (((1,), (1,)), ((), ())), "tn": (((0,), (0,)), ((), ()))}


def _matmul(a, b, mode, out_dtype, name, tm=None, tn=1024, residual=None, b_rows=None, lead_scale=None,
            also_bf16=False):
    a_list = list(a) if isinstance(a, (list, tuple)) else [a]
    b_list = list(b) if isinstance(b, (list, tuple)) else [b]
    rows_list = list(b_rows) if isinstance(b_rows, (list, tuple)) else [b_rows] * len(b_list)
    assert len(a_list) == len(b_list) == len(rows_list)
    dims = _DIMS[mode]
    specs, m, n = [], None, None
    for aa, bb, rr in zip(a_list, b_list, rows_list):
        b_shape = bb.shape if rr is None else (rr, bb.shape[1])
        if mode == "nn":
            (m1, k), (k2, n1) = aa.shape, b_shape
        elif mode == "nt":
            (m1, k), (n1, k2) = aa.shape, b_shape
        else:
            (k, m1), (k2, n1) = aa.shape, b_shape
        assert k == k2 and (m is None or (m, n) == (m1, n1)), (aa.shape, bb.shape, mode)
        m, n = m1, n1
        specs.append(k)
    if tm is None:
        tm = 512 if mode == "tn" else 1024
    tm, tn = min(tm, m), min(tn, n)
    assert m % tm == 0 and n % tn == 0, (m, n, tm, tn)
    in_specs = []
    for k in specs:
        in_specs.append(pl.BlockSpec((k, tm), lambda i, j: (0, i)) if mode == "tn"
                        else pl.BlockSpec((tm, k), lambda i, j: (i, 0)))
        in_specs.append(pl.BlockSpec((tn, k), lambda i, j: (j, 0)) if mode == "nt"
                        else pl.BlockSpec((k, tn), lambda i, j: (0, j)))
    o_spec = pl.BlockSpec((tm, tn), lambda i, j: (i, j))
    n_pairs = len(specs)

    def body(*refs):
        acc = None
        for p in range(n_pairs):
            part = lax.dot_general(refs[2 * p][...].astype(BF16), refs[2 * p + 1][...].astype(BF16), dims,
                                   preferred_element_type=F32)
            acc = part if acc is None else acc + part
        if lead_scale is not None:
            col = pl.program_id(1) * tn + lax.broadcasted_iota(jnp.int32, (1, tn), 1)
            acc = acc * jnp.where(col < lead_scale[0], lead_scale[1], 1.0)
        if residual is not None:
            acc = acc + refs[2 * n_pairs][...].astype(F32)
        outs = refs[2 * n_pairs + (residual is not None):]
        outs[0][...] = acc.astype(outs[0].dtype)
        if also_bf16:
            outs[1][...] = acc.astype(BF16)

    ins = [x for pair in zip(a_list, b_list) for x in pair] + ([residual] if residual is not None else [])
    in_specs = in_specs + ([o_spec] if residual is not None else [])
    out_shape = jax.ShapeDtypeStruct((m, n), out_dtype)
    if also_bf16:
        return pl.pallas_call(
            body, name=name, grid=(m // tm, n // tn), in_specs=in_specs, out_specs=[o_spec, o_spec],
            out_shape=[out_shape, jax.ShapeDtypeStruct((m, n), BF16)],
            compiler_params=_cparams("parallel", "parallel"),
        )(*ins)
    return pl.pallas_call(
        body, name=name, grid=(m // tm, n // tn), in_specs=in_specs, out_specs=o_spec, out_shape=out_shape,
        compiler_params=_cparams("parallel", "parallel"),
    )(*ins)


def _rmsnorm_fwd(x, w, name, tm=512):
    t, d = x.shape

    def body(x_ref, w_ref, o_ref):
        xf = x_ref[...].astype(F32)
        r = lax.rsqrt(jnp.mean(xf * xf, axis=-1, keepdims=True) + EPS)
        o_ref[...] = (xf * r * w_ref[...]).astype(o_ref.dtype)

    return pl.pallas_call(
        body, name=name, grid=(t // tm,),
        in_specs=[pl.BlockSpec((tm, d), lambda i: (i, 0)), pl.BlockSpec((1, d), lambda i: (0, 0))],
        out_specs=pl.BlockSpec((tm, d), lambda i: (i, 0)),
        out_shape=jax.ShapeDtypeStruct((t, d), BF16), compiler_params=_cparams("parallel"),
    )(x, w)


def _rmsnorm_bwd(dy, x, w, name, out_dtype, residual=None, also_bf16=False, tm=512):
    t, d = x.shape

    def body(dy_ref, x_ref, w_ref, *rest):
        dx_ref, dw_ref = rest[-3 if also_bf16 else -2], rest[-1]
        xf = x_ref[...].astype(F32)
        r = lax.rsqrt(jnp.mean(xf * xf, axis=-1, keepdims=True) + EPS)
        xh = xf * r
        dyf = dy_ref[...].astype(F32)
        dyh = dyf * w_ref[...]
        dx = r * (dyh - xh * jnp.mean(dyh * xh, axis=-1, keepdims=True))
        if residual is not None:
            dx = dx + rest[0][...].astype(F32)
        dx_ref[...] = dx.astype(dx_ref.dtype)
        if also_bf16:
            rest[-2][...] = dx.astype(BF16)
        part = jnp.sum(dyf * xh, axis=0, keepdims=True)

        @pl.when(pl.program_id(0) == 0)
        def _():
            dw_ref[...] = part

        @pl.when(pl.program_id(0) > 0)
        def _():
            dw_ref[...] += part

    row = pl.BlockSpec((tm, d), lambda i: (i, 0))
    vec = pl.BlockSpec((1, d), lambda i: (0, 0))
    ins = [dy, x, w] + ([residual] if residual is not None else [])
    extra = [jax.ShapeDtypeStruct((t, d), BF16)] if also_bf16 else []
    return pl.pallas_call(
        body, name=name, grid=(t // tm,), in_specs=[row, row, vec] + ([row] if residual is not None else []),
        out_specs=[row] + [row] * len(extra) + [vec],
        out_shape=[jax.ShapeDtypeStruct((t, d), out_dtype)] + extra + [jax.ShapeDtypeStruct((1, d), F32)],
        compiler_params=_cparams("arbitrary"),
    )(*ins)


def _final_loss(h, target, w, name, tm=512):
    t, d = h.shape

    def body(h_ref, t_ref, w_ref, dh_ref, dhb_ref, loss_ref, dw_ref):
        xf = h_ref[...]
        r = lax.rsqrt(jnp.mean(xf * xf, axis=-1, keepdims=True) + EPS)
        xh = xf * r
        err = xh * w_ref[...] - t_ref[...]
        part_loss = 0.5 * jnp.sum(jnp.sum(err * err, axis=-1, keepdims=True), axis=0, keepdims=True) * (1.0 / d)
        dy = err * (1.0 / d)
        dyh = dy * w_ref[...]
        dh = r * (dyh - xh * jnp.mean(dyh * xh, axis=-1, keepdims=True))
        dh_ref[...] = dh
        dhb_ref[...] = dh.astype(BF16)
        part_dw = jnp.sum(dy * xh, axis=0, keepdims=True)
        part_loss = jnp.broadcast_to(part_loss, (1, LANES))

        @pl.when(pl.program_id(0) == 0)
        def _():
            dw_ref[...] = part_dw
            loss_ref[...] = part_loss

        @pl.when(pl.program_id(0) > 0)
        def _():
            dw_ref[...] += part_dw
            loss_ref[...] += part_loss

    row = pl.BlockSpec((tm, d), lambda i: (i, 0))
    vec = pl.BlockSpec((1, d), lambda i: (0, 0))
    return pl.pallas_call(
        body, name=name, grid=(t // tm,), in_specs=[row, row, vec],
        out_specs=[row, row, pl.BlockSpec((1, LANES), lambda i: (0, 0)), vec],
        out_shape=[jax.ShapeDtypeStruct((t, d), F32), jax.ShapeDtypeStruct((t, d), BF16),
                   jax.ShapeDtypeStruct((1, LANES), F32), jax.ShapeDtypeStruct((1, d), F32)],
        compiler_params=_cparams("arbitrary"),
    )(h, target, w)


FFN_TN = 1408


def _ffn_gate_up(u, w_gate, w_up, name, tm=1024):
    t, d = u.shape
    n = w_gate.shape[0]
    tm = min(tm, t)

    def body(u_ref, wg_ref, wu_ref, g_ref, up_ref, act_ref):
        uu = u_ref[...]
        g = lax.dot_general(uu, wg_ref[...], _DIMS["nt"], preferred_element_type=F32)
        up = lax.dot_general(uu, wu_ref[...], _DIMS["nt"], preferred_element_type=F32)
        g_b, up_b = g.astype(BF16), up.astype(BF16)
        g_ref[...] = g_b
        up_ref[...] = up_b
        gf, uf = g_b.astype(F32), up_b.astype(F32)
        act_ref[...] = (gf * jax.nn.sigmoid(gf) * uf).astype(act_ref.dtype)

    w_spec = pl.BlockSpec((FFN_TN, d), lambda i, j: (j, 0))
    o_spec = pl.BlockSpec((tm, FFN_TN), lambda i, j: (i, j))
    out = jax.ShapeDtypeStruct((t, n), BF16)
    return pl.pallas_call(
        body, name=name, grid=(t // tm, n // FFN_TN),
        in_specs=[pl.BlockSpec((tm, d), lambda i, j: (i, 0)), w_spec, w_spec], out_specs=[o_spec] * 3,
        out_shape=[out] * 3, compiler_params=_cparams("parallel", "parallel"),
    )(u, w_gate, w_up)


def _ffn_down_bwd(dh, w_down, g, up, name, tm=1024):
    t, d = dh.shape
    n = w_down.shape[0]
    tm = min(tm, t)

    def body(dh_ref, w_ref, g_ref, up_ref, dg_ref, du_ref):
        df = lax.dot_general(dh_ref[...], w_ref[...], _DIMS["nt"], preferred_element_type=F32).astype(BF16).astype(F32)
        gf = g_ref[...].astype(F32)
        uf = up_ref[...].astype(F32)
        s = jax.nn.sigmoid(gf)
        dg_ref[...] = (df * uf * s * (1.0 + gf * (1.0 - s))).astype(dg_ref.dtype)
        du_ref[...] = (df * gf * s).astype(du_ref.dtype)

    o_spec = pl.BlockSpec((tm, FFN_TN), lambda i, j: (i, j))
    out = jax.ShapeDtypeStruct((t, n), BF16)
    return pl.pallas_call(
        body, name=name, grid=(t // tm, n // FFN_TN),
        in_specs=[pl.BlockSpec((tm, d), lambda i, j: (i, 0)), pl.BlockSpec((FFN_TN, d), lambda i, j: (j, 0)),
                  o_spec, o_spec],
        out_specs=[o_spec, o_spec], out_shape=[out, out], compiler_params=_cparams("parallel", "parallel"),
    )(dh, w_down, g, up)


M_BLOCK = 512
M_CHUNKS_PER_BLOCK = M_BLOCK // M_CHUNK
M_UNROLL = 2


def _log_sigmoid(z):
    return jnp.minimum(z, 0.0) - jnp.log(1.0 + jnp.exp(-jnp.abs(z)))


def _per_head(fn):
    return jnp.stack([fn(hd) for hd in range(M_HEADS)])


def _mlstm_chunk_fwd(q, k, v, gc, c_prev, n_prev, m_prev):
    L, H = M_CHUNK, M_HEADS
    tt = lax.broadcasted_iota(jnp.int32, (L, L), 0)
    ss = lax.broadcasted_iota(jnp.int32, (L, L), 1)
    eye = tt == ss
    causal = ss <= tt
    gct = jnp.transpose(gc)
    i_col = _per_head(lambda hd: gc[:, hd:hd + 1])
    f_col = _per_head(lambda hd: gc[:, H + hd:H + hd + 1])
    i_row = _per_head(lambda hd: gct[hd:hd + 1, :])
    lf_col = _log_sigmoid(f_col)
    lf_row = _log_sigmoid(_per_head(lambda hd: gct[H + hd:H + hd + 1, :]))
    b_col = jnp.sum(jnp.where(causal, lf_row, 0.0), axis=2, keepdims=True)
    b_row = jnp.sum(jnp.where(tt <= ss, lf_col, 0.0), axis=1, keepdims=True)
    g = jnp.sum(lf_col, axis=1, keepdims=True)
    a_row = g - b_row + i_row
    a_col = g - b_col + i_col
    m_loc = jnp.max(a_row, axis=2, keepdims=True)
    d_log = jnp.where(causal, b_col - b_row + i_row, NEG_INF)
    inter = b_col + m_prev
    m_t = jnp.maximum(jnp.max(d_log, axis=2, keepdims=True), inter)
    dmat = jnp.exp(d_log - m_t)
    amat = _per_head(lambda hd: lax.dot_general(q[hd], k[hd], _DIMS["nt"], preferred_element_type=F32))
    p = dmat * amat
    sig = jnp.exp(inter - m_t)
    c_prev_b = [c.astype(BF16) for c in c_prev]
    qc = _per_head(lambda hd: jnp.dot(q[hd], c_prev_b[hd], preferred_element_type=F32))
    p_b = p.astype(BF16)
    num = _per_head(lambda hd: jnp.dot(p_b[hd], v[hd], preferred_element_type=F32)) + sig * qc
    qf = _per_head(lambda hd: q[hd].astype(F32))
    qn = jnp.sum(qf * n_prev, axis=2, keepdims=True)
    den_raw = jnp.sum(p, axis=2, keepdims=True) + sig * qn
    floor = jnp.exp(-m_t)
    den = jnp.maximum(jnp.abs(den_raw), floor)
    h = num / den
    m_new = jnp.maximum(g + m_prev, m_loc)
    w_col = jnp.exp(a_col - m_new)
    w_row = jnp.exp(a_row - m_new)
    alpha = jnp.exp(g + m_prev - m_new)
    return dict(eye=eye, causal=causal, tt=tt, ss=ss, i_col=i_col, f_col=f_col, dmat=dmat, amat=amat, p=p, p_b=p_b,
                sig=sig, qc=qc, qf=qf, qn=qn, num=num, den_raw=den_raw, floor=floor, den=den, h=h, m_new=m_new,
                w_col=w_col, w_row=w_row, alpha=alpha, c_prev_b=c_prev_b, b_row=b_row, b_col=b_col, m_t=m_t)


def _chunk_major_t(xt):
    f, t = xt.shape
    return xt.reshape(f, t // M_CHUNK, M_CHUNK).transpose(1, 0, 2)


def _mlstm_fwd(m4, kt, gates, bias, norm_w, name):
    t = m4.shape[0]
    nblk = t // M_BLOCK
    nc = t // M_CHUNK
    L, dh = M_CHUNK, M_HEAD_DIM

    def body(m4_ref, kt_ref, g_ref, b_ref, w_ref, hm_ref, cp_ref, nm_ref, c_s, n_s, m_s):
        @pl.when(pl.program_id(0) == 0)
        def _():
            c_s[...] = jnp.zeros_like(c_s)
            n_s[...] = jnp.zeros_like(n_s)
            m_s[...] = jnp.zeros_like(m_s)

        row8 = lax.broadcasted_iota(jnp.int32, (8, dh), 0)

        def chunk(c, carry):
            rows = pl.ds(pl.multiple_of(c * L, L), L)
            gc = g_ref[rows, :] + b_ref[...]
            col = lambda part, hd: m4_ref[rows, part * M_WIDTH + hd * dh:part * M_WIDTH + (hd + 1) * dh]
            q = [col(0, hd) for hd in range(M_HEADS)]
            k = [col(1, hd) for hd in range(M_HEADS)]
            v = [col(2, hd) for hd in range(M_HEADS)]
            o = _per_head(lambda hd: col(3, hd).astype(F32))
            c_prev = [c_s[hd] for hd in range(M_HEADS)]
            n_prev = n_s[:, 0:1, :]
            m_prev = m_s[:, 0:1, 0:1]
            r = _mlstm_chunk_fwd(q, k, v, gc, c_prev, n_prev, m_prev)
            h = r["h"]
            hn = h * lax.rsqrt(jnp.mean(h * h, axis=-1, keepdims=True) + EPS) * w_ref[...]
            hm = (hn * jax.nn.sigmoid(o)).astype(hm_ref.dtype)
            nm = jnp.where(row8 == 0, n_prev, jnp.where(row8 == 1, m_prev, 0.0))
            kf = _per_head(lambda hd: k[hd].astype(F32))
            n_s[:, 0:1, :] = r["alpha"] * n_prev + jnp.sum(kf * r["w_col"], axis=1, keepdims=True)
            m_s[...] = jnp.broadcast_to(r["m_new"], (M_HEADS, 8, dh))
            for hd in range(M_HEADS):
                hm_ref[rows, hd * dh:(hd + 1) * dh] = hm[hd]
                cp_ref[hd, c] = r["c_prev_b"][hd]
                nm_ref[hd, c] = nm[hd]
                kw_t = (kt_ref[c, hd * dh:(hd + 1) * dh, :].astype(F32) * r["w_row"][hd]).astype(BF16)
                c_s[hd] = r["alpha"][hd] * c_prev[hd] + jnp.dot(kw_t, v[hd], preferred_element_type=F32)
            return carry

        lax.fori_loop(0, M_CHUNKS_PER_BLOCK, chunk, 0)

    return pl.pallas_call(
        body, name=name, grid=(nblk,),
        in_specs=[pl.BlockSpec((M_BLOCK, 4 * M_WIDTH), lambda i: (i, 0)),
                  pl.BlockSpec((M_CHUNKS_PER_BLOCK, M_WIDTH, L), lambda i: (i, 0, 0)),
                  pl.BlockSpec((M_BLOCK, LANES), lambda i: (i, 0)),
                  pl.BlockSpec((1, LANES), lambda i: (0, 0)),
                  pl.BlockSpec((M_HEADS, 1, dh), lambda i: (0, 0, 0))],
        out_specs=[pl.BlockSpec((M_BLOCK, M_WIDTH), lambda i: (i, 0)),
                   pl.BlockSpec((M_HEADS, M_CHUNKS_PER_BLOCK, dh, dh), lambda i: (0, i, 0, 0)),
                   pl.BlockSpec((M_HEADS, M_CHUNKS_PER_BLOCK, 8, dh), lambda i: (0, i, 0, 0))],
        out_shape=[jax.ShapeDtypeStruct((t, M_WIDTH), BF16),
                   jax.ShapeDtypeStruct((M_HEADS, nc, dh, dh), BF16),
                   jax.ShapeDtypeStruct((M_HEADS, nc, 8, dh), F32)],
        scratch_shapes=[pltpu.VMEM((M_HEADS, dh, dh), F32), pltpu.VMEM((M_HEADS, 8, dh), F32),
                        pltpu.VMEM((M_HEADS, 8, dh), F32)],
        compiler_params=_cparams("arbitrary"),
    )(m4, kt, gates, bias, norm_w)


def _mlstm_bwd(m4, qt, gates, bias, norm_w, c_prev_all, nm_all, dhm, name):
    t = m4.shape[0]
    nblk = t // M_BLOCK
    L, dh = M_CHUNK, M_HEAD_DIM

    def body(m4_ref, qt_ref, g_ref, b_ref, w_ref, cp_ref, nm_ref, dhm_ref, dm4_ref, dg_ref, small_ref, dc_s, dn_s):
        @pl.when(pl.program_id(0) == 0)
        def _():
            dc_s[...] = jnp.zeros_like(dc_s)
            dn_s[...] = jnp.zeros_like(dn_s)
            small_ref[...] = jnp.zeros_like(small_ref)

        lane = lax.broadcasted_iota(jnp.int32, (L, LANES), 1)
        row8 = lax.broadcasted_iota(jnp.int32, (8, LANES), 0)

        def chunk(ci, carry):
            c = M_CHUNKS_PER_BLOCK - 1 - ci
            rows = pl.ds(pl.multiple_of(c * L, L), L)
            gc = g_ref[rows, :] + b_ref[...]
            heads = range(M_HEADS)
            col = lambda part, hd: m4_ref[rows, part * M_WIDTH + hd * dh:part * M_WIDTH + (hd + 1) * dh]
            q = [col(0, hd) for hd in heads]
            k = [col(1, hd) for hd in heads]
            v = [col(2, hd) for hd in heads]
            o = _per_head(lambda hd: col(3, hd).astype(F32))
            q_t = [qt_ref[c, hd * dh:(hd + 1) * dh, :] for hd in heads]
            c_prev = [cp_ref[hd, c].astype(F32) for hd in heads]
            nm = _per_head(lambda hd: nm_ref[hd, c])
            n_prev = nm[:, 0:1, :]
            m_prev = nm[:, 1:2, 0:1]
            r = _mlstm_chunk_fwd(q, k, v, gc, c_prev, n_prev, m_prev)
            eye, tt, ss = r["eye"], r["tt"], r["ss"]
            h, den, sig, p = r["h"], r["den"], r["sig"], r["p"]
            w_col, alpha, c_prev_b = r["w_col"], r["alpha"], r["c_prev_b"]
            to_row = lambda colv: jnp.sum(jnp.where(eye, colv, 0.0), axis=1, keepdims=True)
            to_col = lambda rowv: jnp.sum(jnp.where(eye, rowv, 0.0), axis=2, keepdims=True)
            mm = lambda fn: _per_head(lambda hd: fn(hd))

            w_h = w_ref[...]
            rn = lax.rsqrt(jnp.mean(h * h, axis=-1, keepdims=True) + EPS)
            hh = h * rn
            og = jax.nn.sigmoid(o)
            dhm_c = _per_head(lambda hd: dhm_ref[rows, hd * dh:(hd + 1) * dh].astype(F32))
            dhn = dhm_c * og
            d_o = dhm_c * hh * w_h * og * (1.0 - og)
            d_norm = jnp.sum(dhn * hh, axis=1, keepdims=True)
            dhh = dhn * w_h
            dh_ = rn * (dhh - hh * jnp.mean(dhh * hh, axis=-1, keepdims=True))

            dnum = dh_ / den
            dden = -jnp.sum(dh_ * h, axis=-1, keepdims=True) / den
            dden_raw = jnp.where(jnp.abs(r["den_raw"]) >= r["floor"], dden * jnp.sign(r["den_raw"]), 0.0)
            dnum_b = dnum.astype(BF16)
            dp = mm(lambda hd: lax.dot_general(dnum_b[hd], v[hd], _DIMS["nt"], preferred_element_type=F32)) + dden_raw
            dc = [dc_s[hd] for hd in heads]
            dn = dn_s[:, 0:1, :]
            dc_b = [d.astype(BF16) for d in dc]
            g2_b = (sig * dnum).astype(BF16)
            sd = sig * dden_raw
            da_mat = (dp * r["dmat"]).astype(BF16)
            dqs = mm(lambda hd: lax.dot_general(g2_b[hd], c_prev_b[hd], _DIMS["nt"], preferred_element_type=F32)
                     + jnp.dot(da_mat[hd], k[hd], preferred_element_type=F32)) + sd * n_prev
            r_mat = mm(lambda hd: lax.dot_general(v[hd], dc_b[hd], _DIMS["nt"], preferred_element_type=F32)) + dn
            kf = _per_head(lambda hd: k[hd].astype(F32))
            dmat_t = jnp.exp(jnp.where(tt <= ss, r["b_row"] - r["b_col"] + r["i_col"], NEG_INF) - to_row(r["m_t"]))
            p_t = (dmat_t * mm(lambda hd: jnp.dot(k[hd], q_t[hd], preferred_element_type=F32))).astype(BF16)
            dp_t = mm(lambda hd: lax.dot_general(v[hd], dnum_b[hd], _DIMS["nt"], preferred_element_type=F32))
            da_t = ((dp_t + to_row(dden_raw)) * dmat_t).astype(BF16)
            d_k = mm(lambda hd: jnp.dot(da_t[hd], q[hd], preferred_element_type=F32)) + w_col * r_mat
            d_v = (mm(lambda hd: jnp.dot(p_t[hd], dnum_b[hd], preferred_element_type=F32))
                   + w_col * mm(lambda hd: jnp.dot(k[hd], dc_b[hd], preferred_element_type=F32)))
            da_col = jnp.sum(kf * r_mat, axis=-1, keepdims=True) * w_col
            dsig = jnp.sum(dnum * r["qc"], axis=-1, keepdims=True) + dden_raw * r["qn"]
            pp = dp * p
            r1 = jnp.sum(pp, axis=2, keepdims=True)
            c1_col = to_col(jnp.sum(pp, axis=1, keepdims=True))
            dcc = _per_head(lambda hd: dc[hd] * c_prev[hd])
            d_alpha = (jnp.sum(jnp.sum(dcc, axis=2, keepdims=True), axis=1, keepdims=True)
                       + jnp.sum(dn * n_prev, axis=2, keepdims=True))
            dgl = jnp.sum(da_col, axis=1, keepdims=True) + d_alpha * alpha
            db_col = r1 + dsig * sig - c1_col - da_col
            dli_col = c1_col + da_col
            dlf_col = jnp.sum(jnp.where(ss >= tt, to_row(db_col), 0.0), axis=2, keepdims=True) + dgl
            df_col = dlf_col * jax.nn.sigmoid(-r["f_col"])

            dn_s[:, 0:1, :] = alpha * dn + jnp.sum(sd * r["qf"], axis=1, keepdims=True)
            dq_out = (dqs * M_SCALE).astype(dm4_ref.dtype)
            dk_out, dv_out, do_out = d_k.astype(dm4_ref.dtype), d_v.astype(dm4_ref.dtype), d_o.astype(dm4_ref.dtype)
            dg_tile = jnp.zeros((L, LANES), F32)
            small = jnp.zeros((8, LANES), F32)
            for hd in heads:
                dc_s[hd] = alpha[hd] * dc[hd] + jnp.dot(q_t[hd], g2_b[hd], preferred_element_type=F32)
                dm4_ref[rows, hd * dh:(hd + 1) * dh] = dq_out[hd]
                dm4_ref[rows, M_WIDTH + hd * dh:M_WIDTH + (hd + 1) * dh] = dk_out[hd]
                dm4_ref[rows, 2 * M_WIDTH + hd * dh:2 * M_WIDTH + (hd + 1) * dh] = dv_out[hd]
                dm4_ref[rows, 3 * M_WIDTH + hd * dh:3 * M_WIDTH + (hd + 1) * dh] = do_out[hd]
                dg_tile = (dg_tile + jnp.where(lane == hd, dli_col[hd], 0.0)
                           + jnp.where(lane == M_HEADS + hd, df_col[hd], 0.0))
                small = small + jnp.where(row8 == hd, d_norm[hd], 0.0)
            dg_ref[rows, :] = dg_tile
            small = small + jnp.where(row8 == M_HEADS, jnp.sum(dg_tile, axis=0, keepdims=True), 0.0)
            small_ref[...] += small
            return carry

        lax.fori_loop(0, M_CHUNKS_PER_BLOCK, chunk, 0, unroll=M_UNROLL)

    rev = lambda i: nblk - 1 - i
    return pl.pallas_call(
        body, name=name, grid=(nblk,),
        in_specs=[pl.BlockSpec((M_BLOCK, 4 * M_WIDTH), lambda i: (rev(i), 0)),
                  pl.BlockSpec((M_CHUNKS_PER_BLOCK, M_WIDTH, L), lambda i: (rev(i), 0, 0)),
                  pl.BlockSpec((M_BLOCK, LANES), lambda i: (rev(i), 0)),
                  pl.BlockSpec((1, LANES), lambda i: (0, 0)),
                  pl.BlockSpec((M_HEADS, 1, dh), lambda i: (0, 0, 0)),
                  pl.BlockSpec((M_HEADS, M_CHUNKS_PER_BLOCK, dh, dh), lambda i: (0, rev(i), 0, 0)),
                  pl.BlockSpec((M_HEADS, M_CHUNKS_PER_BLOCK, 8, dh), lambda i: (0, rev(i), 0, 0)),
                  pl.BlockSpec((M_BLOCK, M_WIDTH), lambda i: (rev(i), 0))],
        out_specs=[pl.BlockSpec((M_BLOCK, 4 * M_WIDTH), lambda i: (rev(i), 0)),
                   pl.BlockSpec((M_BLOCK, LANES), lambda i: (rev(i), 0)),
                   pl.BlockSpec((8, LANES), lambda i: (0, 0))],
        out_shape=[jax.ShapeDtypeStruct((t, 4 * M_WIDTH), BF16), jax.ShapeDtypeStruct((t, LANES), F32),
                   jax.ShapeDtypeStruct((8, LANES), F32)],
        scratch_shapes=[pltpu.VMEM((M_HEADS, dh, dh), F32), pltpu.VMEM((M_HEADS, 8, dh), F32)],
        compiler_params=_cparams("arbitrary"),
    )(m4, qt, gates, bias, norm_w, c_prev_all, nm_all, dhm)


def _rope_tables(pos_col, width):
    lane = lax.broadcasted_iota(jnp.int32, (1, width), 1) % A_QK_PAD
    half = A_ROPE // 2
    first = (lane >= A_NOPE) & (lane < A_NOPE + half)
    second = (lane >= A_NOPE + half) & (lane < A_NOPE + A_ROPE)
    idx = jnp.where(first, lane - A_NOPE, lane - A_NOPE - half).astype(F32)
    inv_freq = jnp.exp(idx * (-math.log(ROPE_THETA) / half))
    ang = pos_col.astype(F32) * inv_freq
    cos, sin = jnp.cos(ang), jnp.sin(ang)
    rot = first | second
    return jnp.where(rot, cos, 1.0), jnp.where(first, -sin, 0.0), jnp.where(second, sin, 0.0)


def _rope_apply(vv, cosf, s1, s2):
    half = A_ROPE // 2
    w = vv.shape[-1]
    return vv * cosf + pltpu.roll(vv, w - half, 1) * s1 + pltpu.roll(vv, half, 1) * s2


def _rope_apply_t(dd, cosf, s1, s2):
    half = A_ROPE // 2
    w = dd.shape[-1]
    return dd * cosf + pltpu.roll(dd * s1, half, 1) + pltpu.roll(dd * s2, w - half, 1)


A_BIAS_LANE_K = A_NOPE + A_ROPE
A_BIAS_LANE_V = A_V


def _hi_lo(val):
    hi = val.astype(BF16)
    return hi, (val - hi.astype(F32)).astype(BF16)


def _qkv_up_rope(qa_n, kv_n, w_qb, w_k, w_v, gates, pos, name, tm=512):
    t = qa_n.shape[0]
    w = A_HEADS * LANES

    def body(qa_ref, kv_ref, wq_ref, wk_ref, wv_ref, g_ref, p_ref, qo_ref, ko_ref, vo_ref):
        cosf, s1, s2 = _rope_tables(p_ref[...], LANES)
        lane = lax.broadcasted_iota(jnp.int32, (1, LANES), 1)
        kr = jnp.where((lane >= A_NOPE) & (lane < A_NOPE + A_ROPE), g_ref[...], 0.0)
        kr = _rope_apply(kr, cosf, s1, s2)
        kr = jnp.where((lane == A_BIAS_LANE_K) | (lane == A_BIAS_LANE_K + 1), 1.0, kr)
        v_one = (lane == A_BIAS_LANE_V) | (lane == A_BIAS_LANE_V + 1)
        qpre = lax.dot_general(qa_ref[...], wq_ref[...], _DIMS["nt"], preferred_element_type=F32)
        kv = kv_ref[...]
        kpre = lax.dot_general(kv, wk_ref[...], _DIMS["nt"], preferred_element_type=F32)
        vpre = lax.dot_general(kv, wv_ref[...], _DIMS["nt"], preferred_element_type=F32)
        for hd in range(A_HEADS):
            sl = slice(hd * LANES, (hd + 1) * LANES)
            qo_ref[:, sl] = (_rope_apply(qpre[:, sl], cosf, s1, s2) * ATTN_SCALE).astype(qo_ref.dtype)
            ko_ref[:, sl] = (kpre[:, sl] + kr).astype(ko_ref.dtype)
            vo_ref[:, sl] = jnp.where(v_one, 1.0, vpre[:, sl]).astype(vo_ref.dtype)

    rows = lambda width: pl.BlockSpec((tm, width), lambda i: (i, 0))
    whole = lambda arr: pl.BlockSpec(arr.shape, lambda i: (0, 0))
    out = jax.ShapeDtypeStruct((t, w), BF16)
    return pl.pallas_call(
        body, name=name, grid=(t // tm,),
        in_specs=[rows(qa_n.shape[1]), rows(kv_n.shape[1]), whole(w_qb), whole(w_k), whole(w_v), rows(LANES), rows(1)],
        out_specs=[rows(w)] * 3, out_shape=[out] * 3, compiler_params=_cparams("parallel"),
    )(qa_n, kv_n, w_qb, w_k, w_v, gates, pos)


def _rope_bwd(dq, dk, dgates, pos, name, tm=512):
    t, w = dq.shape

    def body(dq_ref, dk_ref, dg_ref, p_ref, dqo_ref, dgo_ref):
        cosf, s1, s2 = _rope_tables(p_ref[...], LANES)
        acc = jnp.zeros((tm, LANES), F32)
        for hd in range(A_HEADS):
            sl = slice(hd * LANES, (hd + 1) * LANES)
            dqo_ref[:, sl] = (_rope_apply_t(dq_ref[:, sl], cosf, s1, s2) * ATTN_SCALE).astype(dqo_ref.dtype)
            acc = acc + dk_ref[:, sl].astype(F32)
        lane = lax.broadcasted_iota(jnp.int32, (1, LANES), 1)
        dkr = _rope_apply_t(acc, cosf, s1, s2)
        dkr = jnp.where((lane >= A_NOPE) & (lane < A_NOPE + A_ROPE), dkr, 0.0)
        dgo_ref[...] = (dg_ref[...] + dkr).astype(dgo_ref.dtype)

    row = pl.BlockSpec((tm, w), lambda i: (i, 0))
    nar = pl.BlockSpec((tm, LANES), lambda i: (i, 0))
    return pl.pallas_call(
        body, name=name, grid=(t // tm,),
        in_specs=[row, row, nar, pl.BlockSpec((tm, 1), lambda i: (i, 0))],
        out_specs=[row, nar],
        out_shape=[jax.ShapeDtypeStruct((t, w), BF16), jax.ShapeDtypeStruct((t, LANES), BF16)],
        compiler_params=_cparams("parallel"),
    )(dq, dk, dgates, pos)


A_TQ = 512
A_TK = 512


A_HEADS_PER_STEP = 2
A_HEADS_PER_STEP_FWD = 4


def _flash_fwd(q, k, v, name):
    t = q.shape[0]
    tq, tk = A_TQ, A_TK
    nh = A_HEADS_PER_STEP_FWD
    wblk = nh * LANES

    def body(q_ref, k_ref, v_ref, o_ref, qb_ref):
        i = pl.program_id(1)
        lane = lax.broadcasted_iota(jnp.int32, (tq, LANES), 1)
        qpos = i * tq + lax.broadcasted_iota(jnp.int32, (tq, tk), 0)
        kofs = lax.broadcasted_iota(jnp.int32, (tq, tk), 1)
        qs = [q_ref[:, hh * LANES:(hh + 1) * LANES] for hh in range(nh)]

        def step(kb, carry, masked):
            rows = pl.ds(pl.multiple_of(kb * tk, tk), tk)
            new = []
            for hh in range(nh):
                m, acc = carry[hh]
                kh = k_ref[rows, hh * LANES:(hh + 1) * LANES]
                s = lax.dot_general(qs[hh], kh, _DIMS["nt"], preferred_element_type=F32)
                if masked:
                    s = jnp.where(kb * tk + kofs <= qpos, s, NEG_INF)
                m_new = jnp.maximum(m, jnp.max(s, axis=1, keepdims=True))
                p = jnp.exp(s - m_new)
                acc = jnp.exp(m - m_new) * acc + jnp.dot(p.astype(BF16), v_ref[rows, hh * LANES:(hh + 1) * LANES],
                                                          preferred_element_type=F32)
                new.append((m_new, acc))
            return tuple(new)

        carry = tuple((jnp.full((tq, 1), NEG_INF, F32), jnp.zeros((tq, LANES), F32)) for _ in range(nh))
        n_full = (i * tq) // tk
        carry = lax.fori_loop(0, n_full, functools.partial(step, masked=False), carry)
        for d in range(tq // tk):
            carry = step(n_full + d, carry, True)
        for hh in range(nh):
            m, acc = carry[hh]
            l = acc[:, A_BIAS_LANE_V:A_BIAS_LANE_V + 1]
            o_ref[:, hh * LANES:(hh + 1) * LANES] = (acc / l).astype(o_ref.dtype)
            hi, lo = _hi_lo(-(m + jnp.log(l)))
            qb_ref[:, hh * LANES:(hh + 1) * LANES] = jnp.where(
                lane == A_BIAS_LANE_K, hi, jnp.where(lane == A_BIAS_LANE_K + 1, lo, qs[hh]))

    blk = pl.BlockSpec((tq, wblk), lambda j, i: (i, j))
    res = pl.BlockSpec((t, wblk), lambda j, i: (0, j))
    out = jax.ShapeDtypeStruct((t, A_HEADS * LANES), BF16)
    return pl.pallas_call(
        body, name=name, grid=(A_HEADS // nh, t // tq), in_specs=[blk, res, res], out_specs=[blk, blk],
        out_shape=[out, out], compiler_params=_cparams("parallel", "parallel"),
    )(q, k, v)


def _attn_delta(do, o, name, tm=512):
    t, w = do.shape

    def body(do_ref, o_ref, out_ref):
        lane = lax.broadcasted_iota(jnp.int32, (tm, LANES), 1)
        for hd in range(A_HEADS):
            sl = slice(hd * LANES, (hd + 1) * LANES)
            d = do_ref[:, sl]
            delta = jnp.sum(jnp.where(lane < A_V, d.astype(F32) * o_ref[:, sl].astype(F32), 0.0), axis=1, keepdims=True)
            hi, lo = _hi_lo(-delta)
            out_ref[:, sl] = jnp.where(lane == A_BIAS_LANE_V, hi, jnp.where(lane == A_BIAS_LANE_V + 1, lo, d))

    row = pl.BlockSpec((tm, w), lambda i: (i, 0))
    return pl.pallas_call(
        body, name=name, grid=(t // tm,), in_specs=[row, row], out_specs=row,
        out_shape=jax.ShapeDtypeStruct((t, w), BF16), compiler_params=_cparams("parallel"),
    )(do, o)


def _flash_bwd(qb, k, v, doe, name):
    t = qb.shape[0]
    tq, tk = A_TQ, A_TK
    assert tq == tk
    nh = A_HEADS_PER_STEP
    wblk = nh * LANES
    nq = t // tq

    def body(q_ref, k_ref, v_ref, do_ref, dq_ref, dk_ref, dv_ref):
        kb = pl.program_id(1)

        @pl.when(kb == 0)
        def _():
            dq_ref[...] = jnp.zeros_like(dq_ref)

        kpos = kb * tk + lax.broadcasted_iota(jnp.int32, (tk, tq), 0)
        qofs = lax.broadcasted_iota(jnp.int32, (tk, tq), 1)
        ks = [k_ref[:, hh * LANES:(hh + 1) * LANES] for hh in range(nh)]
        vs = [v_ref[:, hh * LANES:(hh + 1) * LANES] for hh in range(nh)]

        def step(qi, carry, masked):
            rows = pl.ds(pl.multiple_of(qi * tq, tq), tq)
            new = []
            for hh in range(nh):
                dk_acc, dv_acc = carry[hh]
                qh = q_ref[rows, hh * LANES:(hh + 1) * LANES]
                doh = do_ref[rows, hh * LANES:(hh + 1) * LANES]
                p_t = jnp.exp(lax.dot_general(ks[hh], qh, _DIMS["nt"], preferred_element_type=F32))
                if masked:
                    p_t = jnp.where(qi * tq + qofs >= kpos, p_t, 0.0)
                ds_t = (p_t * lax.dot_general(vs[hh], doh, _DIMS["nt"], preferred_element_type=F32)).astype(BF16)
                dv_acc = dv_acc + jnp.dot(p_t.astype(BF16), doh, preferred_element_type=F32)
                dk_acc = dk_acc + jnp.dot(ds_t, qh, preferred_element_type=F32)
                dq_ref[rows, hh * LANES:(hh + 1) * LANES] += lax.dot_general(ds_t, ks[hh], _DIMS["tn"],
                                                                             preferred_element_type=F32)
                new.append((dk_acc, dv_acc))
            return tuple(new)

        carry = tuple((jnp.zeros((tk, LANES), F32), jnp.zeros((tk, LANES), F32)) for _ in range(nh))
        carry = step(kb, carry, True)
        carry = lax.fori_loop(kb + 1, nq, functools.partial(step, masked=False), carry)
        for hh in range(nh):
            dk_ref[:, hh * LANES:(hh + 1) * LANES] = carry[hh][0].astype(dk_ref.dtype)
            dv_ref[:, hh * LANES:(hh + 1) * LANES] = carry[hh][1].astype(dv_ref.dtype)

    res = pl.BlockSpec((t, wblk), lambda j, kb: (0, j))
    blk = pl.BlockSpec((tk, wblk), lambda j, kb: (kb, j))
    wide = (t, A_HEADS * LANES)
    return pl.pallas_call(
        body, name=name, grid=(A_HEADS // nh, t // tk), in_specs=[res, blk, blk, res], out_specs=[res, blk, blk],
        out_shape=[jax.ShapeDtypeStruct(wide, F32), jax.ShapeDtypeStruct(wide, BF16), jax.ShapeDtypeStruct(wide, BF16)],
        compiler_params=_cparams("parallel", "arbitrary"),
    )(qb, k, v, doe)


_SPLITS = (M_WIDTH, M_WIDTH, M_WIDTH, M_WIDTH, M_HEADS, M_HEADS, Q_RANK, KV_RANK, A_ROPE)
_OFFS = tuple(sum(_SPLITS[:i]) for i in range(len(_SPLITS) + 1))
_GATE_BLOCK_KR = A_NOPE


def _split_w_in_t(wt):
    z = lambda n: jnp.zeros((n, wt.shape[1]), wt.dtype)
    w_g = jnp.concatenate([wt[_OFFS[4]:_OFFS[6]], z(_GATE_BLOCK_KR - 2 * M_HEADS), wt[_OFFS[8]:_OFFS[9]],
                           z(LANES - _GATE_BLOCK_KR - A_ROPE)], axis=0)
    return wt[_OFFS[6]:_OFFS[8]], w_g


def _merge_w_in_grad_t(g_m4, g_a2, g_g):
    return jnp.concatenate([g_m4, g_g[:2 * M_HEADS], g_a2, g_g[_GATE_BLOCK_KR:_GATE_BLOCK_KR + A_ROPE]], axis=0)


def _pad_heads(w, used):
    w3 = w.reshape(A_HEADS, used, w.shape[1])
    return jnp.pad(w3, ((0, 0), (0, LANES - used), (0, 0))).reshape(A_HEADS * LANES, w.shape[1])


def _unpad_heads(g, used):
    return g.reshape(A_HEADS, LANES, g.shape[1])[:, :used].reshape(A_HEADS * used, g.shape[1])


def _split_w_kv_b_t(wt):
    w3 = wt.reshape(A_HEADS, A_NOPE + A_V, wt.shape[1])
    pad = lambda part: jnp.pad(part, ((0, 0), (0, LANES - part.shape[1]), (0, 0))).reshape(A_HEADS * LANES, wt.shape[1])
    return pad(w3[:, :A_NOPE]), pad(w3[:, A_NOPE:])


def _merge_w_kv_b_grad_t(gk, gv):
    gk3 = gk.reshape(A_HEADS, LANES, gk.shape[1])[:, :A_NOPE]
    gv3 = gv.reshape(A_HEADS, LANES, gv.shape[1])[:, :A_V]
    return jnp.concatenate([gk3, gv3], axis=1).reshape(A_HEADS * (A_NOPE + A_V), gk.shape[1])


def _local_step(x, pos, target, nw, w, start_late_reduce):
    w_in = w["w_in"]
    w_a2, w_g = _split_w_in_t(w_in)
    w_qb = _pad_heads(w["w_q_b"], A_NOPE + A_ROPE)
    w_k, w_v = _split_w_kv_b_t(w["w_kv_b"])
    w_out_m, w_out_a = w["w_out"][:M_WIDTH], _pad_heads(w["w_out"][M_WIDTH:], A_V)
    w_gate, w_up, w_down = w["w_gate"], w["w_up"], w["w_down"]
    bias = jnp.pad(nw["b_gates"], ((0, 0), (0, LANES - 2 * M_HEADS)))
    mnorm = nw["mlstm_norm_w"].reshape(M_HEADS, 1, M_HEAD_DIM)
    n_m4 = 4 * M_WIDTH
    t_len = x.shape[0]

    u1 = _rmsnorm_fwd(x, nw["attn_norm_w"], "attn_norm")
    m4 = _matmul(u1, w_in, "nt", BF16, "proj_mlstm", b_rows=n_m4, lead_scale=(M_WIDTH, M_SCALE))
    a2 = _matmul(u1, w_a2, "nt", BF16, "proj_latents", tn=640)
    gates = _matmul(u1, w_g, "nt", F32, "proj_gates")
    q_t = _chunk_major_t(_matmul(w_in[:M_WIDTH], u1, "nt", BF16, "proj_q_t", lead_scale=(t_len, M_SCALE)))
    k_t = _chunk_major_t(_matmul(w_in[M_WIDTH:2 * M_WIDTH], u1, "nt", BF16, "proj_k_t"))
    hm, c_prev_all, nm_all = _mlstm_fwd(m4, k_t, gates, bias, mnorm, "mlstm_fwd")
    qa, kva = a2[:, :Q_RANK], a2[:, Q_RANK:]
    qa_n = _rmsnorm_fwd(qa, nw["q_a_norm_w"], "q_a_norm")
    kv_n = _rmsnorm_fwd(kva, nw["kv_a_norm_w"], "kv_a_norm")
    qr, kr, vv = _qkv_up_rope(qa_n, kv_n, w_qb, w_k, w_v, gates, pos, "qkv_up_rope")
    ha, qb = _flash_fwd(qr, kr, vv, "attn_fwd")
    h1 = _matmul([hm, ha], [w_out_m, w_out_a], "nn", F32, "out_proj", residual=x)
    u2 = _rmsnorm_fwd(h1, nw["ffn_norm_w"], "ffn_norm")
    gg, uu, act = _ffn_gate_up(u2, w_gate, w_up, "ffn_gate_up")
    h2 = _matmul(act, w_down, "nn", F32, "ffn_down", residual=h1)

    dh2, dh2_b, loss, g_final = _final_loss(h2, target, nw["final_norm_w"].reshape(1, D_MODEL), "final_loss")
    grads = {"final_norm_w": g_final.reshape(D_MODEL)}
    grads_b = {}
    dgg, duu = _ffn_down_bwd(dh2_b, w_down, gg, uu, "ffn_down_bwd")
    grads["w_down"], grads_b["w_down"] = _matmul(act, dh2_b, "tn", F32, "g_w_down", tm=256, tn=D_MODEL, also_bf16=True)
    du2 = _matmul([dgg, duu], [w_gate, w_up], "nn", F32, "d_u2", tm=1024, tn=512)
    grads["w_gate"], grads_b["w_gate"] = _matmul(dgg, u2, "tn", F32, "g_w_gate", tm=256, tn=D_MODEL, also_bf16=True)
    grads["w_up"], grads_b["w_up"] = _matmul(duu, u2, "tn", F32, "g_w_up", tm=256, tn=D_MODEL, also_bf16=True)
    dh1, dh1_b, grads["ffn_norm_w"] = _rmsnorm_bwd(du2, h1, nw["ffn_norm_w"], "ffn_norm_bwd", F32, residual=dh2,
                                                   also_bf16=True)
    dhm = _matmul(dh1_b, w_out_m, "nt", BF16, "d_hm")
    dha = _matmul(dh1_b, w_out_a, "nt", BF16, "d_ha")
    grads["w_out"] = jnp.concatenate([_matmul(hm, dh1_b, "tn", F32, "g_w_out_m", tn=D_MODEL),
                                      _unpad_heads(_matmul(ha, dh1_b, "tn", F32, "g_w_out_a", tn=D_MODEL), A_V)], axis=0)

    late, dha = start_late_reduce(grads, grads_b, dha)

    dqr, dkr, dvv = _flash_bwd(qb, kr, vv, _attn_delta(dha, ha, "attn_delta"), "attn_bwd")
    dm4, dgates, small = _mlstm_bwd(m4, q_t, gates, bias, mnorm, c_prev_all, nm_all, dhm, "mlstm_bwd")
    grads["mlstm_norm_w"] = small[:M_HEADS].reshape(1, M_HEADS, M_HEAD_DIM)
    grads["b_gates"] = small[M_HEADS:M_HEADS + 1, :2 * M_HEADS]
    dqpre, dgk = _rope_bwd(dqr, dkr, dgates, pos, "rope_bwd")
    dqa_n = _matmul(dqpre, w_qb, "nn", BF16, "d_qa_n")
    grads["w_q_b"] = _unpad_heads(_matmul(dqpre, qa_n, "tn", F32, "g_w_q_b"), A_NOPE + A_ROPE)
    dkv_n = _matmul([dkr, dvv], [w_k, w_v], "nn", BF16, "d_kv_n")
    grads["w_kv_b"] = _merge_w_kv_b_grad_t(_matmul(dkr, kv_n, "tn", F32, "g_w_k"),
                                           _matmul(dvv, kv_n, "tn", F32, "g_w_v"))
    dqa, grads["q_a_norm_w"] = _rmsnorm_bwd(dqa_n, qa, nw["q_a_norm_w"], "q_a_norm_bwd", BF16)
    dkva, grads["kv_a_norm_w"] = _rmsnorm_bwd(dkv_n, kva, nw["kv_a_norm_w"], "kv_a_norm_bwd", BF16)
    da2 = jnp.concatenate([dqa, dkva], axis=1)
    du1 = _matmul([dm4, da2, dgk], [w_in, w_a2, w_g], "nn", F32, "d_u1", b_rows=[n_m4, None, None])
    grads["w_in"] = _merge_w_in_grad_t(_matmul(dm4, u1, "tn", F32, "g_w_m4", tn=D_MODEL),
                                       _matmul(da2, u1, "tn", F32, "g_w_a2", tm=640, tn=D_MODEL),
                                       _matmul(dgk, u1, "tn", F32, "g_w_g", tn=D_MODEL))
    grad_x, grads["attn_norm_w"] = _rmsnorm_bwd(du1, x, nw["attn_norm_w"], "attn_norm_bwd", F32, residual=dh1)
    return loss, grad_x, grads, grads_b, late


MESH = pl.DeviceIdType.MESH
N_CHIPS = 4
EARLY = ("w_in", "w_q_b", "w_kv_b")
LATE = ("w_out", "w_gate", "w_up", "w_down")
BIG = EARLY + LATE
TRANSPOSED = ("w_in", "w_q_b", "w_kv_b", "w_gate", "w_up")
LANE_HALVED = ("w_in", "w_out", "w_gate", "w_up", "w_down")
SMALL = ("attn_norm_w", "b_gates", "mlstm_norm_w", "q_a_norm_w", "kv_a_norm_w", "ffn_norm_w", "final_norm_w")
HBM_SPEC = pl.BlockSpec(memory_space=pltpu.HBM)


def _stored(name, a):
    return a[0].T if name in TRANSPOSED else a[0]


def _unstored(name, a):
    return (a.T if name in TRANSPOSED else a)[None]


def _half_shape(name, shape):
    rs, cs = shape
    return (rs, cs // 2) if name in LANE_HALVED else (rs // 2, cs)


def _half(ref, name, h, *lead):
    rs, cs = ref.shape[-2:]
    if name in LANE_HALVED:
        return ref.at[(*lead, slice(None), pl.ds(h * (cs // 2), cs // 2))]
    return ref.at[(*lead, pl.ds(h * (rs // 2), rs // 2), slice(None))]


def _place():
    x, y, c = lax.axis_index("x"), lax.axis_index("y"), lax.axis_index("c")
    others = [(1 - x, y), (x, 1 - y), (1 - x, 1 - y)]
    return x, y, c, others


def _gather_weights(names, shards):
    n = len(shards)

    def body(*refs):
        _gather_body(names, refs[:n], refs[n:2 * n], *refs[2 * n:])

    gathered = pl.pallas_call(
        body, name="gather_weights", in_specs=[HBM_SPEC] * n, out_specs=[HBM_SPEC] * n,
        out_shape=[jax.ShapeDtypeStruct((N_CHIPS,) + s.shape, s.dtype) for s in shards],
        scratch_shapes=[pltpu.SemaphoreType.DMA((6 * n,)), pltpu.SemaphoreType.DMA((6 * n,))],
    )(*shards)
    return _with_own_slab(gathered, shards)


def _with_own_slab(gathered, shards):
    me = 2 * lax.axis_index("x") + lax.axis_index("y")
    return [lax.dynamic_update_slice(g, s[None], (me, 0, 0)) for g, s in zip(gathered, shards)]


def _gather_body(names, ins, outs, send_sems, recv_sems):
    x, y, c, others = _place()
    me = 2 * x + y
    sibling = (x, y, 1 - c)

    def copy(w, k, slab, core, to, src=None):
        dst = _half(outs[w], names[w], core, slab)
        return pltpu.make_async_remote_copy(
            src_ref=dst if src is None else src, dst_ref=dst, send_sem=send_sems.at[w * 6 + k],
            recv_sem=recv_sems.at[w * 6 + k], device_id=to, device_id_type=MESH)

    sends = []
    for w in range(len(names)):
        for j, chip in enumerate(others):
            cp = copy(w, j, me, c, (*chip, c), src=_half(ins[w], names[w], c))
            cp.start()
            sends.append(cp)
    for w in range(len(names)):
        for j, (ox, oy) in enumerate(others):
            slab = 2 * ox + oy
            copy(w, j, slab, c, (x, y, c)).wait_recv()
            fwd = copy(w, 3 + j, slab, c, sibling)
            fwd.start()
            sends.append(fwd)
    for w in range(len(names)):
        for j, (ox, oy) in enumerate(others):
            copy(w, 3 + j, 2 * ox + oy, 1 - c, (x, y, c)).wait_recv()
    for cp in sends:
        cp.wait_send()


GATHER_LATE_COLLECTIVE_ID = 1


def _gather_weights_async(names, shards):
    n = len(shards)
    src = [jax.new_ref(s, memory_space=pltpu.MemorySpace.HBM) for s in shards]
    dst = [jax.empty_ref(jax.ShapeDtypeStruct((N_CHIPS,) + s.shape, s.dtype), memory_space=pltpu.MemorySpace.HBM)
           for s in shards]

    @pl.kernel(mesh=plsc.ScalarSubcoreMesh(axis_name="sequencer", num_cores=1), name="gather_weights_async",
               scratch_types=(pltpu.SemaphoreType.DMA((6 * n,)), pltpu.SemaphoreType.DMA((6 * n,))),
               compiler_params=pltpu.CompilerParams(collective_id=GATHER_LATE_COLLECTIVE_ID))
    def launch(send_sems, recv_sems):
        x, y, c, others = _place()
        peers = [(ox, oy, c) for ox, oy in others] + [(x, y, 1 - c)]
        barrier = pltpu.get_barrier_semaphore()
        for peer in peers:
            pl.semaphore_signal(barrier, inc=1, device_id=peer, device_id_type=MESH)
        pl.semaphore_wait(barrier, len(peers))
        _gather_body(names, src, dst, send_sems, recv_sems)

    launch()
    return _with_own_slab([d[...] for d in dst], shards)


def _exchange(arrays, out_shapes, plan, copies_per_array, name):
    n = len(arrays)

    def body(*refs):
        ins, outs = refs[:n], refs[n:2 * n]
        send_sems, recv_sems = refs[2 * n:]
        rem = [pltpu.make_async_remote_copy(src_ref=s, dst_ref=d, send_sem=send_sems.at[k], recv_sem=recv_sems.at[k],
                                            device_id=to, device_id_type=MESH)
               for k, (s, d, to) in enumerate(plan(ins, outs, _place()))]
        for cp in rem:
            cp.start()
        for cp in rem:
            cp.wait_recv()
        for cp in rem:
            cp.wait_send()

    return pl.pallas_call(
        body, name=name, in_specs=[HBM_SPEC] * n, out_specs=[HBM_SPEC] * n, out_shape=out_shapes,
        scratch_shapes=[pltpu.SemaphoreType.DMA((copies_per_array * n,)),
                        pltpu.SemaphoreType.DMA((copies_per_array * n,))],
    )(*arrays)


def _plan_to_sibling(ins, outs, place):
    x, y, c, _ = place
    return [(ins[w], outs[w], (x, y, 1 - c)) for w in range(len(ins))]


N_PEERS = 7


def _shard_sum_direct(g3, got7, name, where):
    hs = tuple(got7.shape[1:])
    if name in LANE_HALVED:
        g_in, g_spec = g3, pl.BlockSpec((1,) + hs, lambda i, p: (p[0], 0, p[1]))
    else:
        g_in, g_spec = g3.reshape((N_CHIPS, 2) + hs), pl.BlockSpec((1, 1) + hs, lambda i, p: (p[0], p[1], 0, 0))

    def body(p_ref, g_ref, r_ref, o_ref):
        acc = g_ref[(0,) * (len(g_ref.shape) - 2)]
        for k in range(N_PEERS):
            acc = acc + r_ref[k].astype(F32)
        o_ref[...] = acc

    return pl.pallas_call(
        body, name="shard_sum_%s" % name,
        grid_spec=pltpu.PrefetchScalarGridSpec(
            num_scalar_prefetch=1, grid=(1,),
            in_specs=[g_spec, pl.BlockSpec((N_PEERS,) + hs, lambda i, p: (0, 0, 0))],
            out_specs=pl.BlockSpec(hs, lambda i, p: (0, 0))),
        out_shape=jax.ShapeDtypeStruct(hs, F32), compiler_params=_cparams("arbitrary"),
    )(where, g_in, got7)


def _adamw_halves(w, m, v, mine, other, name, core):
    hs = tuple(mine.shape)
    full = pl.BlockSpec(hs, (lambda h, c_ref: (0, h)) if name in LANE_HALVED else (lambda h, c_ref: (h, 0)))
    half = pl.BlockSpec(hs, lambda h, c_ref: (0, 0))

    def body(c_ref, w_ref, m_ref, v_ref, a_ref, b_ref, g_ref, d_ref, mo_ref, vo_ref):
        g = jnp.where(pl.program_id(0) == c_ref[0], a_ref[...], b_ref[...])
        delta, m_new, v_new = _adamw_math(w_ref[...], g, m_ref[...], v_ref[...])
        g_ref[...] = g
        d_ref[...] = delta
        mo_ref[...] = m_new
        vo_ref[...] = v_new

    out = jax.ShapeDtypeStruct(w.shape, F32)
    return pl.pallas_call(
        body, name="adamw_%s" % name,
        grid_spec=pltpu.PrefetchScalarGridSpec(num_scalar_prefetch=1, grid=(2,), in_specs=[full, full, full, half, half],
                                               out_specs=[full] * 4),
        out_shape=[out] * 4, compiler_params=_cparams("parallel"),
    )(core, w, m, v, mine, other)


def _adamw_math(w, g, m, v):
    m = ADAM_B1 * m + (1.0 - ADAM_B1) * g
    v = ADAM_B2 * v + (1.0 - ADAM_B2) * (g * g)
    m_hat = m / (1.0 - ADAM_B1 ** ADAM_STEP)
    v_hat = v / (1.0 - ADAM_B2 ** ADAM_STEP)
    delta = -ADAM_LR * (m_hat / (jnp.sqrt(v_hat) + ADAM_EPS) + ADAM_WD * w)
    return delta, m, v


def _core_index():
    return lax.axis_index("c").astype(jnp.int32).reshape(1)


DIRECT_REDUCE_COLLECTIVE_ID = {"late": 2, "early": 3}


def _exchange_contributions_async(names, gb3, tag):
    n = len(gb3)
    src = [jax.new_ref(g, memory_space=pltpu.MemorySpace.HBM) for g in gb3]
    dst = [jax.empty_ref(jax.ShapeDtypeStruct((N_PEERS,) + _half_shape(nm, g.shape[1:]), g.dtype),
                         memory_space=pltpu.MemorySpace.HBM) for nm, g in zip(names, gb3)]

    @pl.kernel(mesh=plsc.ScalarSubcoreMesh(axis_name="sequencer", num_cores=1), name="contributions_async_" + tag,
               scratch_types=(pltpu.SemaphoreType.DMA((N_PEERS * n,)), pltpu.SemaphoreType.DMA((N_PEERS * n,))),
               compiler_params=pltpu.CompilerParams(collective_id=DIRECT_REDUCE_COLLECTIVE_ID[tag]))
    def launch(send_sems, recv_sems):
        x, y, c, _ = _place()
        peers = [(x ^ ((k >> 2) & 1), y ^ ((k >> 1) & 1), c ^ (k & 1)) for k in range(1, N_PEERS + 1)]
        barrier = pltpu.get_barrier_semaphore()
        for peer in peers:
            pl.semaphore_signal(barrier, inc=1, device_id=peer, device_id_type=MESH)
        pl.semaphore_wait(barrier, N_PEERS)
        rem = []
        for w in range(n):
            for k, (px, py, pc) in enumerate(peers):
                rem.append(pltpu.make_async_remote_copy(
                    src_ref=_half(src[w], names[w], pc, 2 * px + py), dst_ref=dst[w].at[k],
                    send_sem=send_sems.at[N_PEERS * w + k], recv_sem=recv_sems.at[N_PEERS * w + k],
                    device_id=(px, py, pc), device_id_type=MESH))
        for cp in rem:
            cp.start()
        for cp in rem:
            cp.wait_recv()
        for cp in rem:
            cp.wait_send()

    launch()
    return [d[...] for d in dst]


def _shard_index():
    return (2 * lax.axis_index("x") + lax.axis_index("y")).astype(jnp.int32).reshape(1)


def _with_sibling_half(names, mine, tag):
    other = _exchange(mine, [jax.ShapeDtypeStruct(m.shape, F32) for m in mine], _plan_to_sibling, 1,
                      "sibling_result_" + tag)
    return {n: (a, b) for n, a, b in zip(names, mine, other)}


SMALL_ROWS = 8
SMALL_LAYOUT = {"attn_norm_w": (0, 0, 1024), "ffn_norm_w": (1, 0, 1024), "final_norm_w": (2, 0, 1024),
                "q_a_norm_w": (3, 0, 384), "kv_a_norm_w": (3, 384, 256), "mlstm_norm_w": (4, 0, 512),
                "b_gates": (4, 512, 8)}
LOSS_SLOT = (5, 0)


def _pack_small(vals, loss=None):
    tile = jnp.zeros((SMALL_ROWS, D_MODEL), F32)
    for n, (r, c0, width) in SMALL_LAYOUT.items():
        tile = tile.at[r, c0:c0 + width].set(vals[n].reshape(width).astype(F32))
    if loss is not None:
        tile = tile.at[LOSS_SLOT[0], LOSS_SLOT[1]].set(loss)
    return tile


def _unpack_small(tile, shapes):
    return {n: tile[r, c0:c0 + width].reshape(shapes[n]) for n, (r, c0, width) in SMALL_LAYOUT.items()}


def _small_allreduce_adamw(g_tile, w_tile, m_tile, v_tile):
    def body(g_ref, w_ref, m_ref, v_ref, gsum_ref, d_ref, mo_ref, vo_ref, slots, send_sems, recv_sems):
        x, y, c, _ = _place()
        me = 4 * x + 2 * y + c
        slots[me] = g_ref[...]
        copies = []
        for k in range(1, 8):
            dx, dy, dc = (k >> 2) & 1, (k >> 1) & 1, k & 1
            to = (x ^ dx, y ^ dy, c ^ dc)
            cp = pltpu.make_async_remote_copy(src_ref=g_ref, dst_ref=slots.at[me], send_sem=send_sems.at[k - 1],
                                              recv_sem=recv_sems.at[k - 1], device_id=to, device_id_type=MESH)
            cp.start()
            copies.append(cp)
        for k in range(1, 8):
            src = me ^ k
            pltpu.make_async_remote_copy(src_ref=g_ref, dst_ref=slots.at[src], send_sem=send_sems.at[k - 1],
                                         recv_sem=recv_sems.at[k - 1], device_id=(x, y, c),
                                         device_id_type=MESH).wait_recv()
        for cp in copies:
            cp.wait_send()
        total = slots[0]
        for d in range(1, 8):
            total = total + slots[d]
        gsum_ref[...] = total
        delta, m_new, v_new = _adamw_math(w_ref[...], total, m_ref[...], v_ref[...])
        d_ref[...] = delta
        mo_ref[...] = m_new
        vo_ref[...] = v_new

    vm = pl.BlockSpec(memory_space=pltpu.VMEM)
    tile = jax.ShapeDtypeStruct((SMALL_ROWS, D_MODEL), F32)
    return pl.pallas_call(
        body, name="small_allreduce_adamw", in_specs=[vm] * 4, out_specs=[vm] * 4, out_shape=[tile] * 4,
        scratch_shapes=[pltpu.VMEM((8, SMALL_ROWS, D_MODEL), F32), pltpu.SemaphoreType.DMA((7,)),
                        pltpu.SemaphoreType.DMA((7,))],
    )(g_tile, w_tile, m_tile, v_tile)


def kernel(x, positions, attn_norm_w, w_in, b_gates, mlstm_norm_w, q_a_norm_w, w_q_b, kv_a_norm_w, w_kv_b, w_out, ffn_norm_w, w_gate, w_up, w_down, final_norm_w, loss_target, m_attn_norm_w, m_w_in, m_b_gates, m_mlstm_norm_w, m_q_a_norm_w, m_w_q_b, m_kv_a_norm_w, m_w_kv_b, m_w_out, m_ffn_norm_w, m_w_gate, m_w_up, m_w_down, m_final_norm_w, v_attn_norm_w, v_w_in, v_b_gates, v_mlstm_norm_w, v_q_a_norm_w, v_w_q_b, v_kv_a_norm_w, v_w_kv_b, v_w_out, v_ffn_norm_w, v_w_gate, v_w_up, v_w_down, v_final_norm_w):
    names = ("attn_norm_w", "w_in", "b_gates", "mlstm_norm_w", "q_a_norm_w", "w_q_b", "kv_a_norm_w", "w_kv_b", "w_out",
             "ffn_norm_w", "w_gate", "w_up", "w_down", "final_norm_w")
    wts = dict(zip(names, (attn_norm_w, w_in, b_gates, mlstm_norm_w, q_a_norm_w, w_q_b, kv_a_norm_w, w_kv_b, w_out,
                           ffn_norm_w, w_gate, w_up, w_down, final_norm_w)))
    mom = dict(zip(names, (m_attn_norm_w, m_w_in, m_b_gates, m_mlstm_norm_w, m_q_a_norm_w, m_w_q_b, m_kv_a_norm_w,
                           m_w_kv_b, m_w_out, m_ffn_norm_w, m_w_gate, m_w_up, m_w_down, m_final_norm_w)))
    vel = dict(zip(names, (v_attn_norm_w, v_w_in, v_b_gates, v_mlstm_norm_w, v_q_a_norm_w, v_w_q_b, v_kv_a_norm_w,
                           v_w_kv_b, v_w_out, v_ffn_norm_w, v_w_gate, v_w_up, v_w_down, v_final_norm_w)))
    t = x.shape[1]

    shards = {n: _stored(n, wts[n]).astype(BF16) for n in BIG}
    early = _gather_weights(EARLY, [shards[n] for n in EARLY])
    early, late_in = lax.optimization_barrier((early, [shards[n] for n in LATE]))
    late = _gather_weights_async(LATE, late_in)
    full = {n: g.reshape(N_CHIPS * g.shape[1], g.shape[2]) for n, g in zip(EARLY + LATE, early + late)}

    nw = {n: wts[n] for n in SMALL}
    by_shard = lambda g: g.reshape(N_CHIPS, g.shape[0] // N_CHIPS, g.shape[1])

    def start_reduce(group, tag, grads, grads_b, marker):
        g3 = [by_shard(grads[n]) for n in group]
        gb3 = [by_shard(grads_b[n]) if n in grads_b else g.astype(BF16) for n, g in zip(group, g3)]
        gb3, marker = lax.optimization_barrier((gb3, marker))
        return (g3, _exchange_contributions_async(group, gb3, tag)), marker

    loss, grad_x, grads, grads_b, (late_g3, late_got7) = _local_step(
        x[0], positions.reshape(t, 1), loss_target[0], nw, full, functools.partial(start_reduce, LATE, "late"))

    (early_g3, early_got7), late_got7 = start_reduce(EARLY, "early", grads, grads_b, list(late_got7))
    where = jnp.concatenate([_shard_index(), _core_index()])
    outs_g, outs_d, outs_m, outs_v = {}, {}, {}, {}

    def finish(group, g3, got7, tag):
        mine = [_shard_sum_direct(g, r7, n, where) for n, g, r7 in zip(group, g3, got7)]
        halves = _with_sibling_half(group, mine, tag)
        for n in group:
            res = _adamw_halves(_stored(n, wts[n]), _stored(n, mom[n]), _stored(n, vel[n]), *halves[n], n, where[1:])
            outs_g[n], outs_d[n], outs_m[n], outs_v[n] = [_unstored(n, r) for r in res]

    finish(LATE, late_g3, late_got7, "late")
    shapes = {n: wts[n].shape for n in SMALL}
    g_tile = _pack_small({n: grads[n] for n in SMALL}, loss=loss[0, 0])
    gsum, d_tile, m_tile, v_tile = _small_allreduce_adamw(g_tile, _pack_small(wts), _pack_small(mom), _pack_small(vel))
    finish(EARLY, early_g3, early_got7, "early")
    outs_g.update(_unpack_small(gsum, shapes))
    outs_d.update(_unpack_small(d_tile, shapes))
    outs_m.update(_unpack_small(m_tile, shapes))
    outs_v.update(_unpack_small(v_tile, shapes))
    total_loss = gsum[LOSS_SLOT[0], LOSS_SLOT[1]]
    return (total_loss, grad_x[None], *[outs_g[n] for n in names], *[outs_d[n] for n in names],
            *[outs_m[n] for n in names], *[outs_v[n] for n in names])
```

```python
import functools
import math

import jax
import jax.numpy as jnp
from jax import lax
from jax.experimental import pallas as pl
from jax.experimental.pallas import tpu as pltpu
from jax.experimental.pallas import tpu_sc as plsc

F32 = jnp.float32
BF16 = jnp.bfloat16

D_MODEL = 1024
M_HEADS = 4
M_HEAD_DIM = 128
M_WIDTH = M_HEADS * M_HEAD_DIM
M_CHUNK = 64
A_HEADS = 8
A_NOPE = 64
A_ROPE = 32
A_V = 64
A_WIDTH = A_HEADS * A_V
A_QK_PAD = 128
Q_RANK = 384
KV_RANK = 256
ROPE_THETA = 10000.0
D_FF = 2816
D_IN = 2728
EPS = 1e-6
ATTN_SCALE = (A_NOPE + A_ROPE) ** -0.5
M_SCALE = M_HEAD_DIM ** -0.5

ADAM_LR = 0.001
ADAM_B1 = 0.9
ADAM_B2 = 0.999
ADAM_EPS = 1e-08
ADAM_WD = 0.01
ADAM_STEP = 10

VMEM_LIMIT_BYTES = 56 * 1024 * 1024
LANES = 128
NEG_INF = float("-inf")


def _cparams(*sem):
    return pltpu.CompilerParams(dimension_semantics=sem if sem else None, vmem_limit_bytes=VMEM_LIMIT_BYTES)


_DIMS = {"nn": (((1,), (0,)), ((), ())), "nt": (((1,), (1,)), ((), ())), "tn": (((0,), (0,)), ((), ()))}


def _matmul(a, b, mode, out_dtype, name, tm=None, tn=1024, residual=None, b_rows=None, lead_scale=None,
            also_bf16=False):
    a_list = list(a) if isinstance(a, (list, tuple)) else [a]
    b_list = list(b) if isinstance(b, (list, tuple)) else [b]
    rows_list = list(b_rows) if isinstance(b_rows, (list, tuple)) else [b_rows] * len(b_list)
    assert len(a_list) == len(b_list) == len(rows_list)
    dims = _DIMS[mode]
    specs, m, n = [], None, None
    for aa, bb, rr in zip(a_list, b_list, rows_list):
        b_shape = bb.shape if rr is None else (rr, bb.shape[1])
        if mode == "nn":
            (m1, k), (k2, n1) = aa.shape, b_shape
        elif mode == "nt":
            (m1, k), (n1, k2) = aa.shape, b_shape
        else:
            (k, m1), (k2, n1) = aa.shape, b_shape
        assert k == k2 and (m is None or (m, n) == (m1, n1)), (aa.shape, bb.shape, mode)
        m, n = m1, n1
        specs.append(k)
    if tm is None:
        tm = 512 if mode == "tn" else 1024
    tm, tn = min(tm, m), min(tn, n)
    assert m % tm == 0 and n % tn == 0, (m, n, tm, tn)
    in_specs = []
    for k in specs:
        in_specs.append(pl.BlockSpec((k, tm), lambda i, j: (0, i)) if mode == "tn"
                        else pl.BlockSpec((tm, k), lambda i, j: (i, 0)))
        in_specs.append(pl.BlockSpec((tn, k), lambda i, j: (j, 0)) if mode == "nt"
                        else pl.BlockSpec((k, tn), lambda i, j: (0, j)))
    o_spec = pl.BlockSpec((tm, tn), lambda i, j: (i, j))
    n_pairs = len(specs)

    def body(*refs):
        acc = None
        for p in range(n_pairs):
            part = lax.dot_general(refs[2 * p][...].astype(BF16), refs[2 * p + 1][...].astype(BF16), dims,
                                   preferred_element_type=F32)
            acc = part if acc is None else acc + part
        if lead_scale is not None:
            col = pl.program_id(1) * tn + lax.broadcasted_iota(jnp.int32, (1, tn), 1)
            acc = acc * jnp.where(col < lead_scale[0], lead_scale[1], 1.0)
        if residual is not None:
            acc = acc + refs[2 * n_pairs][...].astype(F32)
        outs = refs[2 * n_pairs + (residual is not None):]
        outs[0][...] = acc.astype(outs[0].dtype)
        if also_bf16:
            outs[1][...] = acc.astype(BF16)

    ins = [x for pair in zip(a_list, b_list) for x in pair] + ([residual] if residual is not None else [])
    in_specs = in_specs + ([o_spec] if residual is not None else [])
    out_shape = jax.ShapeDtypeStruct((m, n), out_dtype)
    if also_bf16:
        return pl.pallas_call(
            body, name=name, grid=(m // tm, n // tn), in_specs=in_specs, out_specs=[o_spec, o_spec],
            out_shape=[out_shape, jax.ShapeDtypeStruct((m, n), BF16)],
            compiler_params=_cparams("parallel", "parallel"),
        )(*ins)
    return pl.pallas_call(
        body, name=name, grid=(m // tm, n // tn), in_specs=in_specs, out_specs=o_spec, out_shape=out_shape,
        compiler_params=_cparams("parallel", "parallel"),
    )(*ins)


def _rmsnorm_fwd(x, w, name, tm=512):
    t, d = x.shape

    def body(x_ref, w_ref, o_ref):
        xf = x_ref[...].astype(F32)
        r = lax.rsqrt(jnp.mean(xf * xf, axis=-1, keepdims=True) + EPS)
        o_ref[...] = (xf * r * w_ref[...]).astype(o_ref.dtype)

    return pl.pallas_call(
        body, name=name, grid=(t // tm,),
        in_specs=[pl.BlockSpec((tm, d), lambda i: (i, 0)), pl.BlockSpec((1, d), lambda i: (0, 0))],
        out_specs=pl.BlockSpec((tm, d), lambda i: (i, 0)),
        out_shape=jax.ShapeDtypeStruct((t, d), BF16), compiler_params=_cparams("parallel"),
    )(x, w)


def _rmsnorm_bwd(dy, x, w, name, out_dtype, residual=None, also_bf16=False, tm=512):
    t, d = x.shape

    def body(dy_ref, x_ref, w_ref, *rest):
        dx_ref, dw_ref = rest[-3 if also_bf16 else -2], rest[-1]
        xf = x_ref[...].astype(F32)
        r = lax.rsqrt(jnp.mean(xf * xf, axis=-1, keepdims=True) + EPS)
        xh = xf * r
        dyf = dy_ref[...].astype(F32)
        dyh = dyf * w_ref[...]
        dx = r * (dyh - xh * jnp.mean(dyh * xh, axis=-1, keepdims=True))
        if residual is not None:
            dx = dx + rest[0][...].astype(F32)
        dx_ref[...] = dx.astype(dx_ref.dtype)
        if also_bf16:
            rest[-2][...] = dx.astype(BF16)
        part = jnp.sum(dyf * xh, axis=0, keepdims=True)

        @pl.when(pl.program_id(0) == 0)
        def _():
            dw_ref[...] = part

        @pl.when(pl.program_id(0) > 0)
        def _():
            dw_ref[...] += part

    row = pl.BlockSpec((tm, d), lambda i: (i, 0))
    vec = pl.BlockSpec((1, d), lambda i: (0, 0))
    ins = [dy, x, w] + ([residual] if residual is not None else [])
    extra = [jax.ShapeDtypeStruct((t, d), BF16)] if also_bf16 else []
    return pl.pallas_call(
        body, name=name, grid=(t // tm,), in_specs=[row, row, vec] + ([row] if residual is not None else []),
        out_specs=[row] + [row] * len(extra) + [vec],
        out_shape=[jax.ShapeDtypeStruct((t, d), out_dtype)] + extra + [jax.ShapeDtypeStruct((1, d), F32)],
        compiler_params=_cparams("arbitrary"),
    )(*ins)


def _final_loss(h, target, w, name, tm=512):
    t, d = h.shape

    def body(h_ref, t_ref, w_ref, dh_ref, dhb_ref, loss_ref, dw_ref):
        xf = h_ref[...]
        r = lax.rsqrt(jnp.mean(xf * xf, axis=-1, keepdims=True) + EPS)
        xh = xf * r
        err = xh * w_ref[...] - t_ref[...]
        part_loss = 0.5 * jnp.sum(jnp.sum(err * err, axis=-1, keepdims=True), axis=0, keepdims=True) * (1.0 / d)
        dy = err * (1.0 / d)
        dyh = dy * w_ref[...]
        dh = r * (dyh - xh * jnp.mean(dyh * xh, axis=-1, keepdims=True))
        dh_ref[...] = dh
        dhb_ref[...] = dh.astype(BF16)
        part_dw = jnp.sum(dy * xh, axis=0, keepdims=True)
        part_loss = jnp.broadcast_to(part_loss, (1, LANES))

        @pl.when(pl.program_id(0) == 0)
        def _():
            dw_ref[...] = part_dw
            loss_ref[...] = part_loss

        @pl.when(pl.program_id(0) > 0)
        def _():
            dw_ref[...] += part_dw
            loss_ref[...] += part_loss

    row = pl.BlockSpec((tm, d), lambda i: (i, 0))
    vec = pl.BlockSpec((1, d), lambda i: (0, 0))
    return pl.pallas_call(
        body, name=name, grid=(t // tm,), in_specs=[row, row, vec],
        out_specs=[row, row, pl.BlockSpec((1, LANES), lambda i: (0, 0)), vec],
        out_shape=[jax.ShapeDtypeStruct((t, d), F32), jax.ShapeDtypeStruct((t, d), BF16),
                   jax.ShapeDtypeStruct((1, LANES), F32), jax.ShapeDtypeStruct((1, d), F32)],
        compiler_params=_cparams("arbitrary"),
    )(h, target, w)


FFN_TN = 1408


def _ffn_gate_up(u, w_gate, w_up, name, tm=1024):
    t, d = u.shape
    n = w_gate.shape[0]
    tm = min(tm, t)

    def body(u_ref, wg_ref, wu_ref, g_ref, up_ref, act_ref):
        uu = u_ref[...]
        g = lax.dot_general(uu, wg_ref[...], _DIMS["nt"], preferred_element_type=F32)
        up = lax.dot_general(uu, wu_ref[...], _DIMS["nt"], preferred_element_type=F32)
        g_b, up_b = g.astype(BF16), up.astype(BF16)
        g_ref[...] = g_b
        up_ref[...] = up_b
        gf, uf = g_b.astype(F32), up_b.astype(F32)
        act_ref[...] = (gf * jax.nn.sigmoid(gf) * uf).astype(act_ref.dtype)

    w_spec = pl.BlockSpec((FFN_TN, d), lambda i, j: (j, 0))
    o_spec = pl.BlockSpec((tm, FFN_TN), lambda i, j: (i, j))
    out = jax.ShapeDtypeStruct((t, n), BF16)
    return pl.pallas_call(
        body, name=name, grid=(t // tm, n // FFN_TN),
        in_specs=[pl.BlockSpec((tm, d), lambda i, j: (i, 0)), w_spec, w_spec], out_specs=[o_spec] * 3,
        out_shape=[out] * 3, compiler_params=_cparams("parallel", "parallel"),
    )(u, w_gate, w_up)


def _ffn_down_bwd(dh, w_down, g, up, name, tm=1024):
    t, d = dh.shape
    n = w_down.shape[0]
    tm = min(tm, t)

    def body(dh_ref, w_ref, g_ref, up_ref, dg_ref, du_ref):
        df = lax.dot_general(dh_ref[...], w_ref[...], _DIMS["nt"], preferred_element_type=F32).astype(BF16).astype(F32)
        gf = g_ref[...].astype(F32)
        uf = up_ref[...].astype(F32)
        s = jax.nn.sigmoid(gf)
        dg_ref[...] = (df * uf * s * (1.0 + gf * (1.0 - s))).astype(dg_ref.dtype)
        du_ref[...] = (df * gf * s).astype(du_ref.dtype)

    o_spec = pl.BlockSpec((tm, FFN_TN), lambda i, j: (i, j))
    out = jax.ShapeDtypeStruct((t, n), BF16)
    return pl.pallas_call(
        body, name=name, grid=(t // tm, n // FFN_TN),
        in_specs=[pl.BlockSpec((tm, d), lambda i, j: (i, 0)), pl.BlockSpec((FFN_TN, d), lambda i, j: (j, 0)),
                  o_spec, o_spec],
        out_specs=[o_spec, o_spec], out_shape=[out, out], compiler_params=_cparams("parallel", "parallel"),
    )(dh, w_down, g, up)


M_BLOCK = 512
M_CHUNKS_PER_BLOCK = M_BLOCK // M_CHUNK
M_UNROLL = 2


def _log_sigmoid(z):
    return jnp.minimum(z, 0.0) - jnp.log(1.0 + jnp.exp(-jnp.abs(z)))


def _per_head(fn):
    return jnp.stack([fn(hd) for hd in range(M_HEADS)])


def _mlstm_chunk_fwd(q, k, v, gc, c_prev, n_prev, m_prev):
    L, H = M_CHUNK, M_HEADS
    tt = lax.broadcasted_iota(jnp.int32, (L, L), 0)
    ss = lax.broadcasted_iota(jnp.int32, (L, L), 1)
    eye = tt == ss
    causal = ss <= tt
    gct = jnp.transpose(gc)
    i_col = _per_head(lambda hd: gc[:, hd:hd + 1])
    f_col = _per_head(lambda hd: gc[:, H + hd:H + hd + 1])
    i_row = _per_head(lambda hd: gct[hd:hd + 1, :])
    lf_col = _log_sigmoid(f_col)
    lf_row = _log_sigmoid(_per_head(lambda hd: gct[H + hd:H + hd + 1, :]))
    b_col = jnp.sum(jnp.where(causal, lf_row, 0.0), axis=2, keepdims=True)
    b_row = jnp.sum(jnp.where(tt <= ss, lf_col, 0.0), axis=1, keepdims=True)
    g = jnp.sum(lf_col, axis=1, keepdims=True)
    a_row = g - b_row + i_row
    a_col = g - b_col + i_col
    m_loc = jnp.max(a_row, axis=2, keepdims=True)
    d_log = jnp.where(causal, b_col - b_row + i_row, NEG_INF)
    inter = b_col + m_prev
    m_t = jnp.maximum(jnp.max(d_log, axis=2, keepdims=True), inter)
    dmat = jnp.exp(d_log - m_t)
    amat = _per_head(lambda hd: lax.dot_general(q[hd], k[hd], _DIMS["nt"], preferred_element_type=F32))
    p = dmat * amat
    sig = jnp.exp(inter - m_t)
    c_prev_b = [c.astype(BF16) for c in c_prev]
    qc = _per_head(lambda hd: jnp.dot(q[hd], c_prev_b[hd], preferred_element_type=F32))
    p_b = p.astype(BF16)
    num = _per_head(lambda hd: jnp.dot(p_b[hd], v[hd], preferred_element_type=F32)) + sig * qc
    qf = _per_head(lambda hd: q[hd].astype(F32))
    qn = jnp.sum(qf * n_prev, axis=2, keepdims=True)
    den_raw = jnp.sum(p, axis=2, keepdims=True) + sig * qn
    floor = jnp.exp(-m_t)
    den = jnp.maximum(jnp.abs(den_raw), floor)
    h = num / den
    m_new = jnp.maximum(g + m_prev, m_loc)
    w_col = jnp.exp(a_col - m_new)
    w_row = jnp.exp(a_row - m_new)
    alpha = jnp.exp(g + m_prev - m_new)
    return dict(eye=eye, causal=causal, tt=tt, ss=ss, i_col=i_col, f_col=f_col, dmat=dmat, amat=amat, p=p, p_b=p_b,
                sig=sig, qc=qc, qf=qf, qn=qn, num=num, den_raw=den_raw, floor=floor, den=den, h=h, m_new=m_new,
                w_col=w_col, w_row=w_row, alpha=alpha, c_prev_b=c_prev_b, b_row=b_row, b_col=b_col, m_t=m_t)


def _chunk_major_t(xt):
    f, t = xt.shape
    return xt.reshape(f, t // M_CHUNK, M_CHUNK).transpose(1, 0, 2)


def _mlstm_fwd(m4, kt, gates, bias, norm_w, name):
    t = m4.shape[0]
    nblk = t // M_BLOCK
    nc = t // M_CHUNK
    L, dh = M_CHUNK, M_HEAD_DIM

    def body(m4_ref, kt_ref, g_ref, b_ref, w_ref, hm_ref, cp_ref, nm_ref, c_s, n_s, m_s):
        @pl.when(pl.program_id(0) == 0)
        def _():
            c_s[...] = jnp.zeros_like(c_s)
            n_s[...] = jnp.zeros_like(n_s)
            m_s[...] = jnp.zeros_like(m_s)

        row8 = lax.broadcasted_iota(jnp.int32, (8, dh), 0)

        def chunk(c, carry):
            rows = pl.ds(pl.multiple_of(c * L, L), L)
            gc = g_ref[rows, :] + b_ref[...]
            col = lambda part, hd: m4_ref[rows, part * M_WIDTH + hd * dh:part * M_WIDTH + (hd + 1) * dh]
            q = [col(0, hd) for hd in range(M_HEADS)]
            k = [col(1, hd) for hd in range(M_HEADS)]
            v = [col(2, hd) for hd in range(M_HEADS)]
            o = _per_head(lambda hd: col(3, hd).astype(F32))
            c_prev = [c_s[hd] for hd in range(M_HEADS)]
            n_prev = n_s[:, 0:1, :]
            m_prev = m_s[:, 0:1, 0:1]
            r = _mlstm_chunk_fwd(q, k, v, gc, c_prev, n_prev, m_prev)
            h = r["h"]
            hn = h * lax.rsqrt(jnp.mean(h * h, axis=-1, keepdims=True) + EPS) * w_ref[...]
            hm = (hn * jax.nn.sigmoid(o)).astype(hm_ref.dtype)
            nm = jnp.where(row8 == 0, n_prev, jnp.where(row8 == 1, m_prev, 0.0))
            kf = _per_head(lambda hd: k[hd].astype(F32))
            n_s[:, 0:1, :] = r["alpha"] * n_prev + jnp.sum(kf * r["w_col"], axis=1, keepdims=True)
            m_s[...] = jnp.broadcast_to(r["m_new"], (M_HEADS, 8, dh))
            for hd in range(M_HEADS):
                hm_ref[rows, hd * dh:(hd + 1) * dh] = hm[hd]
                cp_ref[hd, c] = r["c_prev_b"][hd]
                nm_ref[hd, c] = nm[hd]
                kw_t = (kt_ref[c, hd * dh:(hd + 1) * dh, :].astype(F32) * r["w_row"][hd]).astype(BF16)
                c_s[hd] = r["alpha"][hd] * c_prev[hd] + jnp.dot(kw_t, v[hd], preferred_element_type=F32)
            return carry

        lax.fori_loop(0, M_CHUNKS_PER_BLOCK, chunk, 0)

    return pl.pallas_call(
        body, name=name, grid=(nblk,),
        in_specs=[pl.BlockSpec((M_BLOCK, 4 * M_WIDTH), lambda i: (i, 0)),
                  pl.BlockSpec((M_CHUNKS_PER_BLOCK, M_WIDTH, L), lambda i: (i, 0, 0)),
                  pl.BlockSpec((M_BLOCK, LANES), lambda i: (i, 0)),
                  pl.BlockSpec((1, LANES), lambda i: (0, 0)),
                  pl.BlockSpec((M_HEADS, 1, dh), lambda i: (0, 0, 0))],
        out_specs=[pl.BlockSpec((M_BLOCK, M_WIDTH), lambda i: (i, 0)),
                   pl.BlockSpec((M_HEADS, M_CHUNKS_PER_BLOCK, dh, dh), lambda i: (0, i, 0, 0)),
                   pl.BlockSpec((M_HEADS, M_CHUNKS_PER_BLOCK, 8, dh), lambda i: (0, i, 0, 0))],
        out_shape=[jax.ShapeDtypeStruct((t, M_WIDTH), BF16),
                   jax.ShapeDtypeStruct((M_HEADS, nc, dh, dh), BF16),
                   jax.ShapeDtypeStruct((M_HEADS, nc, 8, dh), F32)],
        scratch_shapes=[pltpu.VMEM((M_HEADS, dh, dh), F32), pltpu.VMEM((M_HEADS, 8, dh), F32),
                        pltpu.VMEM((M_HEADS, 8, dh), F32)],
        compiler_params=_cparams("arbitrary"),
    )(m4, kt, gates, bias, norm_w)


def _mlstm_bwd(m4, qt, gates, bias, norm_w, c_prev_all, nm_all, dhm, name):
    t = m4.shape[0]
    nblk = t // M_BLOCK
    L, dh = M_CHUNK, M_HEAD_DIM

    def body(m4_ref, qt_ref, g_ref, b_ref, w_ref, cp_ref, nm_ref, dhm_ref, dm4_ref, dg_ref, small_ref, dc_s, dn_s):
        @pl.when(pl.program_id(0) == 0)
        def _():
            dc_s[...] = jnp.zeros_like(dc_s)
            dn_s[...] = jnp.zeros_like(dn_s)
            small_ref[...] = jnp.zeros_like(small_ref)

        lane = lax.broadcasted_iota(jnp.int32, (L, LANES), 1)
        row8 = lax.broadcasted_iota(jnp.int32, (8, LANES), 0)

        def chunk(ci, carry):
            c = M_CHUNKS_PER_BLOCK - 1 - ci
            rows = pl.ds(pl.multiple_of(c * L, L), L)
            gc = g_ref[rows, :] + b_ref[...]
            heads = range(M_HEADS)
            col = lambda part, hd: m4_ref[rows, part * M_WIDTH + hd * dh:part * M_WIDTH + (hd + 1) * dh]
            q = [col(0, hd) for hd in heads]
            k = [col(1, hd) for hd in heads]
            v = [col(2, hd) for hd in heads]
            o = _per_head(lambda hd: col(3, hd).astype(F32))
            q_t = [qt_ref[c, hd * dh:(hd + 1) * dh, :] for hd in heads]
            c_prev = [cp_ref[hd, c].astype(F32) for hd in heads]
            nm = _per_head(lambda hd: nm_ref[hd, c])
            n_prev = nm[:, 0:1, :]
            m_prev = nm[:, 1:2, 0:1]
            r = _mlstm_chunk_fwd(q, k, v, gc, c_prev, n_prev, m_prev)
            eye, tt, ss = r["eye"], r["tt"], r["ss"]
            h, den, sig, p = r["h"], r["den"], r["sig"], r["p"]
            w_col, alpha, c_prev_b = r["w_col"], r["alpha"], r["c_prev_b"]
            to_row = lambda colv: jnp.sum(jnp.where(eye, colv, 0.0), axis=1, keepdims=True)
            to_col = lambda rowv: jnp.sum(jnp.where(eye, rowv, 0.0), axis=2, keepdims=True)
            mm = lambda fn: _per_head(lambda hd: fn(hd))

            w_h = w_ref[...]
            rn = lax.rsqrt(jnp.mean(h * h, axis=-1, keepdims=True) + EPS)
            hh = h * rn
            og = jax.nn.sigmoid(o)
            dhm_c = _per_head(lambda hd: dhm_ref[rows, hd * dh:(hd + 1) * dh].astype(F32))
            dhn = dhm_c * og
            d_o = dhm_c * hh * w_h * og * (1.0 - og)
            d_norm = jnp.sum(dhn * hh, axis=1, keepdims=True)
            dhh = dhn * w_h
            dh_ = rn * (dhh - hh * jnp.mean(dhh * hh, axis=-1, keepdims=True))

            dnum = dh_ / den
            dden = -jnp.sum(dh_ * h, axis=-1, keepdims=True) / den
            dden_raw = jnp.where(jnp.abs(r["den_raw"]) >= r["floor"], dden * jnp.sign(r["den_raw"]), 0.0)
            dnum_b = dnum.astype(BF16)
            dp = mm(lambda hd: lax.dot_general(dnum_b[hd], v[hd], _DIMS["nt"], preferred_element_type=F32)) + dden_raw
            dc = [dc_s[hd] for hd in heads]
            dn = dn_s[:, 0:1, :]
            dc_b = [d.astype(BF16) for d in dc]
            g2_b = (sig * dnum).astype(BF16)
            sd = sig * dden_raw
            da_mat = (dp * r["dmat"]).astype(BF16)
            dqs = mm(lambda hd: lax.dot_general(g2_b[hd], c_prev_b[hd], _DIMS["nt"], preferred_element_type=F32)
                     + jnp.dot(da_mat[hd], k[hd], preferred_element_type=F32)) + sd * n_prev
            r_mat = mm(lambda hd: lax.dot_general(v[hd], dc_b[hd], _DIMS["nt"], preferred_element_type=F32)) + dn
            kf = _per_head(lambda hd: k[hd].astype(F32))
            dmat_t = jnp.exp(jnp.where(tt <= ss, r["b_row"] - r["b_col"] + r["i_col"], NEG_INF) - to_row(r["m_t"]))
            p_t = (dmat_t * mm(lambda hd: jnp.dot(k[hd], q_t[hd], preferred_element_type=F32))).astype(BF16)
            dp_t = mm(lambda hd: lax.dot_general(v[hd], dnum_b[hd], _DIMS["nt"], preferred_element_type=F32))
            da_t = ((dp_t + to_row(dden_raw)) * dmat_t).astype(BF16)
            d_k = mm(lambda hd: jnp.dot(da_t[hd], q[hd], preferred_element_type=F32)) + w_col * r_mat
            d_v = (mm(lambda hd: jnp.dot(p_t[hd], dnum_b[hd], preferred_element_type=F32))
                   + w_col * mm(lambda hd: jnp.dot(k[hd], dc_b[hd], preferred_element_type=F32)))
            da_col = jnp.sum(kf * r_mat, axis=-1, keepdims=True) * w_col
            dsig = jnp.sum(dnum * r["qc"], axis=-1, keepdims=True) + dden_raw * r["qn"]
            pp = dp * p
            r1 = jnp.sum(pp, axis=2, keepdims=True)
            c1_col = to_col(jnp.sum(pp, axis=1, keepdims=True))
            dcc = _per_head(lambda hd: dc[hd] * c_prev[hd])
            d_alpha = (jnp.sum(jnp.sum(dcc, axis=2, keepdims=True), axis=1, keepdims=True)
                       + jnp.sum(dn * n_prev, axis=2, keepdims=True))
            dgl = jnp.sum(da_col, axis=1, keepdims=True) + d_alpha * alpha
            db_col = r1 + dsig * sig - c1_col - da_col
            dli_col = c1_col + da_col
            dlf_col = jnp.sum(jnp.where(ss >= tt, to_row(db_col), 0.0), axis=2, keepdims=True) + dgl
            df_col = dlf_col * jax.nn.sigmoid(-r["f_col"])

            dn_s[:, 0:1, :] = alpha * dn + jnp.sum(sd * r["qf"], axis=1, keepdims=True)
            dq_out = (dqs * M_SCALE).astype(dm4_ref.dtype)
            dk_out, dv_out, do_out = d_k.astype(dm4_ref.dtype), d_v.astype(dm4_ref.dtype), d_o.astype(dm4_ref.dtype)
            dg_tile = jnp.zeros((L, LANES), F32)
            small = jnp.zeros((8, LANES), F32)
            for hd in heads:
                dc_s[hd] = alpha[hd] * dc[hd] + jnp.dot(q_t[hd], g2_b[hd], preferred_element_type=F32)
                dm4_ref[rows, hd * dh:(hd + 1) * dh] = dq_out[hd]
                dm4_ref[rows, M_WIDTH + hd * dh:M_WIDTH + (hd + 1) * dh] = dk_out[hd]
                dm4_ref[rows, 2 * M_WIDTH + hd * dh:2 * M_WIDTH + (hd + 1) * dh] = dv_out[hd]
                dm4_ref[rows, 3 * M_WIDTH + hd * dh:3 * M_WIDTH + (hd + 1) * dh] = do_out[hd]
                dg_tile = (dg_tile + jnp.where(lane == hd, dli_col[hd], 0.0)
                           + jnp.where(lane == M_HEADS + hd, df_col[hd], 0.0))
                small = small + jnp.where(row8 == hd, d_norm[hd], 0.0)
            dg_ref[rows, :] = dg_tile
            small = small + jnp.where(row8 == M_HEADS, jnp.sum(dg_tile, axis=0, keepdims=True), 0.0)
            small_ref[...] += small
            return carry

        lax.fori_loop(0, M_CHUNKS_PER_BLOCK, chunk, 0, unroll=M_UNROLL)

    rev = lambda i: nblk - 1 - i
    return pl.pallas_call(
        body, name=name, grid=(nblk,),
        in_specs=[pl.BlockSpec((M_BLOCK, 4 * M_WIDTH), lambda i: (rev(i), 0)),
                  pl.BlockSpec((M_CHUNKS_PER_BLOCK, M_WIDTH, L), lambda i: (rev(i), 0, 0)),
                  pl.BlockSpec((M_BLOCK, LANES), lambda i: (rev(i), 0)),
                  pl.BlockSpec((1, LANES), lambda i: (0, 0)),
                  pl.BlockSpec((M_HEADS, 1, dh), lambda i: (0, 0, 0)),
                  pl.BlockSpec((M_HEADS, M_CHUNKS_PER_BLOCK, dh, dh), lambda i: (0, rev(i), 0, 0)),
                  pl.BlockSpec((M_HEADS, M_CHUNKS_PER_BLOCK, 8, dh), lambda i: (0, rev(i), 0, 0)),
                  pl.BlockSpec((M_BLOCK, M_WIDTH), lambda i: (rev(i), 0))],
        out_specs=[pl.BlockSpec((M_BLOCK, 4 * M_WIDTH), lambda i: (rev(i), 0)),
                   pl.BlockSpec((M_BLOCK, LANES), lambda i: (rev(i), 0)),
                   pl.BlockSpec((8, LANES), lambda i: (0, 0))],
        out_shape=[jax.ShapeDtypeStruct((t, 4 * M_WIDTH), BF16), jax.ShapeDtypeStruct((t, LANES), F32),
                   jax.ShapeDtypeStruct((8, LANES), F32)],
        scratch_shapes=[pltpu.VMEM((M_HEADS, dh, dh), F32), pltpu.VMEM((M_HEADS, 8, dh), F32)],
        compiler_params=_cparams("arbitrary"),
    )(m4, qt, gates, bias, norm_w, c_prev_all, nm_all, dhm)


def _rope_tables(pos_col, width):
    lane = lax.broadcasted_iota(jnp.int32, (1, width), 1) % A_QK_PAD
    half = A_ROPE // 2
    first = (lane >= A_NOPE) & (lane < A_NOPE + half)
    second = (lane >= A_NOPE + half) & (lane < A_NOPE + A_ROPE)
    idx = jnp.where(first, lane - A_NOPE, lane - A_NOPE - half).astype(F32)
    inv_freq = jnp.exp(idx * (-math.log(ROPE_THETA) / half))
    ang = pos_col.astype(F32) * inv_freq
    cos, sin = jnp.cos(ang), jnp.sin(ang)
    rot = first | second
    return jnp.where(rot, cos, 1.0), jnp.where(first, -sin, 0.0), jnp.where(second, sin, 0.0)


def _rope_apply(vv, cosf, s1, s2):
    half = A_ROPE // 2
    w = vv.shape[-1]
    return vv * cosf + pltpu.roll(vv, w - half, 1) * s1 + pltpu.roll(vv, half, 1) * s2


def _rope_apply_t(dd, cosf, s1, s2):
    half = A_ROPE // 2
    w = dd.shape[-1]
    return dd * cosf + pltpu.roll(dd * s1, half, 1) + pltpu.roll(dd * s2, w - half, 1)


A_BIAS_LANE_K = A_NOPE + A_ROPE
A_BIAS_LANE_V = A_V


def _hi_lo(val):
    hi = val.astype(BF16)
    return hi, (val - hi.astype(F32)).astype(BF16)


def _qkv_up_rope(qa_n, kv_n, w_qb, w_k, w_v, gates, pos, name, tm=512):
    t = qa_n.shape[0]
    w = A_HEADS * LANES

    def body(qa_ref, kv_ref, wq_ref, wk_ref, wv_ref, g_ref, p_ref, qo_ref, ko_ref, vo_ref):
        cosf, s1, s2 = _rope_tables(p_ref[...], LANES)
        lane = lax.broadcasted_iota(jnp.int32, (1, LANES), 1)
        kr = jnp.where((lane >= A_NOPE) & (lane < A_NOPE + A_ROPE), g_ref[...], 0.0)
        kr = _rope_apply(kr, cosf, s1, s2)
        kr = jnp.where((lane == A_BIAS_LANE_K) | (lane == A_BIAS_LANE_K + 1), 1.0, kr)
        v_one = (lane == A_BIAS_LANE_V) | (lane == A_BIAS_LANE_V + 1)
        qpre = lax.dot_general(qa_ref[...], wq_ref[...], _DIMS["nt"], preferred_element_type=F32)
        kv = kv_ref[...]
        kpre = lax.dot_general(kv, wk_ref[...], _DIMS["nt"], preferred_element_type=F32)
        vpre = lax.dot_general(kv, wv_ref[...], _DIMS["nt"], preferred_element_type=F32)
        for hd in range(A_HEADS):
            sl = slice(hd * LANES, (hd + 1) * LANES)
            qo_ref[:, sl] = (_rope_apply(qpre[:, sl], cosf, s1, s2) * ATTN_SCALE).astype(qo_ref.dtype)
            ko_ref[:, sl] = (kpre[:, sl] + kr).astype(ko_ref.dtype)
            vo_ref[:, sl] = jnp.where(v_one, 1.0, vpre[:, sl]).astype(vo_ref.dtype)

    rows = lambda width: pl.BlockSpec((tm, width), lambda i: (i, 0))
    whole = lambda arr: pl.BlockSpec(arr.shape, lambda i: (0, 0))
    out = jax.ShapeDtypeStruct((t, w), BF16)
    return pl.pallas_call(
        body, name=name, grid=(t // tm,),
        in_specs=[rows(qa_n.shape[1]), rows(kv_n.shape[1]), whole(w_qb), whole(w_k), whole(w_v), rows(LANES), rows(1)],
        out_specs=[rows(w)] * 3, out_shape=[out] * 3, compiler_params=_cparams("parallel"),
    )(qa_n, kv_n, w_qb, w_k, w_v, gates, pos)


def _rope_bwd(dq, dk, dgates, pos, name, tm=512):
    t, w = dq.shape

    def body(dq_ref, dk_ref, dg_ref, p_ref, dqo_ref, dgo_ref):
        cosf, s1, s2 = _rope_tables(p_ref[...], LANES)
        acc = jnp.zeros((tm, LANES), F32)
        for hd in range(A_HEADS):
            sl = slice(hd * LANES, (hd + 1) * LANES)
            dqo_ref[:, sl] = (_rope_apply_t(dq_ref[:, sl], cosf, s1, s2) * ATTN_SCALE).astype(dqo_ref.dtype)
            acc = acc + dk_ref[:, sl].astype(F32)
        lane = lax.broadcasted_iota(jnp.int32, (1, LANES), 1)
        dkr = _rope_apply_t(acc, cosf, s1, s2)
        dkr = jnp.where((lane >= A_NOPE) & (lane < A_NOPE + A_ROPE), dkr, 0.0)
        dgo_ref[...] = (dg_ref[...] + dkr).astype(dgo_ref.dtype)

    row = pl.BlockSpec((tm, w), lambda i: (i, 0))
    nar = pl.BlockSpec((tm, LANES), lambda i: (i, 0))
    return pl.pallas_call(
        body, name=name, grid=(t // tm,),
        in_specs=[row, row, nar, pl.BlockSpec((tm, 1), lambda i: (i, 0))],
        out_specs=[row, nar],
        out_shape=[jax.ShapeDtypeStruct((t, w), BF16), jax.ShapeDtypeStruct((t, LANES), BF16)],
        compiler_params=_cparams("parallel"),
    )(dq, dk, dgates, pos)


A_TQ = 512
A_TK = 512


A_HEADS_PER_STEP = 2
A_HEADS_PER_STEP_FWD = 4


def _flash_fwd(q, k, v, name):
    t = q.shape[0]
    tq, tk = A_TQ, A_TK
    nh = A_HEADS_PER_STEP_FWD
    wblk = nh * LANES

    def body(q_ref, k_ref, v_ref, o_ref, qb_ref):
        i = pl.program_id(1)
        lane = lax.broadcasted_iota(jnp.int32, (tq, LANES), 1)
        qpos = i * tq + lax.broadcasted_iota(jnp.int32, (tq, tk), 0)
        kofs = lax.broadcasted_iota(jnp.int32, (tq, tk), 1)
        qs = [q_ref[:, hh * LANES:(hh + 1) * LANES] for hh in range(nh)]

        def step(kb, carry, masked):
            rows = pl.ds(pl.multiple_of(kb * tk, tk), tk)
            new = []
            for hh in range(nh):
                m, acc = carry[hh]
                kh = k_ref[rows, hh * LANES:(hh + 1) * LANES]
                s = lax.dot_general(qs[hh], kh, _DIMS["nt"], preferred_element_type=F32)
                if masked:
                    s = jnp.where(kb * tk + kofs <= qpos, s, NEG_INF)
                m_new = jnp.maximum(m, jnp.max(s, axis=1, keepdims=True))
                p = jnp.exp(s - m_new)
                acc = jnp.exp(m - m_new) * acc + jnp.dot(p.astype(BF16), v_ref[rows, hh * LANES:(hh + 1) * LANES],
                                                          preferred_element_type=F32)
                new.append((m_new, acc))
            return tuple(new)

        carry = tuple((jnp.full((tq, 1), NEG_INF, F32), jnp.zeros((tq, LANES), F32)) for _ in range(nh))
        n_full = (i * tq) // tk
        carry = lax.fori_loop(0, n_full, functools.partial(step, masked=False), carry)
        for d in range(tq // tk):
            carry = step(n_full + d, carry, True)
        for hh in range(nh):
            m, acc = carry[hh]
            l = acc[:, A_BIAS_LANE_V:A_BIAS_LANE_V + 1]
            o_ref[:, hh * LANES:(hh + 1) * LANES] = (acc / l).astype(o_ref.dtype)
            hi, lo = _hi_lo(-(m + jnp.log(l)))
            qb_ref[:, hh * LANES:(hh + 1) * LANES] = jnp.where(
                lane == A_BIAS_LANE_K, hi, jnp.where(lane == A_BIAS_LANE_K + 1, lo, qs[hh]))

    blk = pl.BlockSpec((tq, wblk), lambda j, i: (i, j))
    res = pl.BlockSpec((t, wblk), lambda j, i: (0, j))
    out = jax.ShapeDtypeStruct((t, A_HEADS * LANES), BF16)
    return pl.pallas_call(
        body, name=name, grid=(A_HEADS // nh, t // tq), in_specs=[blk, res, res], out_specs=[blk, blk],
        out_shape=[out, out], compiler_params=_cparams("parallel", "parallel"),
    )(q, k, v)


def _attn_delta(do, o, name, tm=512):
    t, w = do.shape

    def body(do_ref, o_ref, out_ref):
        lane = lax.broadcasted_iota(jnp.int32, (tm, LANES), 1)
        for hd in range(A_HEADS):
            sl = slice(hd * LANES, (hd + 1) * LANES)
            d = do_ref[:, sl]
            delta = jnp.sum(jnp.where(lane < A_V, d.astype(F32) * o_ref[:, sl].astype(F32), 0.0), axis=1, keepdims=True)
            hi, lo = _hi_lo(-delta)
            out_ref[:, sl] = jnp.where(lane == A_BIAS_LANE_V, hi, jnp.where(lane == A_BIAS_LANE_V + 1, lo, d))

    row = pl.BlockSpec((tm, w), lambda i: (i, 0))
    return pl.pallas_call(
        body, name=name, grid=(t // tm,), in_specs=[row, row], out_specs=row,
        out_shape=jax.ShapeDtypeStruct((t, w), BF16), compiler_params=_cparams("parallel"),
    )(do, o)


def _flash_bwd(qb, k, v, doe, name):
    t = qb.shape[0]
    tq, tk = A_TQ, A_TK
    assert tq == tk
    nh = A_HEADS_PER_STEP
    wblk = nh * LANES
    nq = t // tq

    def body(q_ref, k_ref, v_ref, do_ref, dq_ref, dk_ref, dv_ref):
        kb = pl.program_id(1)

        @pl.when(kb == 0)
        def _():
            dq_ref[...] = jnp.zeros_like(dq_ref)

        kpos = kb * tk + lax.broadcasted_iota(jnp.int32, (tk, tq), 0)
        qofs = lax.broadcasted_iota(jnp.int32, (tk, tq), 1)
        ks = [k_ref[:, hh * LANES:(hh + 1) * LANES] for hh in range(nh)]
        vs = [v_ref[:, hh * LANES:(hh + 1) * LANES] for hh in range(nh)]

        def step(qi, carry, masked):
            rows = pl.ds(pl.multiple_of(qi * tq, tq), tq)
            new = []
            for hh in range(nh):
                dk_acc, dv_acc = carry[hh]
                qh = q_ref[rows, hh * LANES:(hh + 1) * LANES]
                doh = do_ref[rows, hh * LANES:(hh + 1) * LANES]
                p_t = jnp.exp(lax.dot_general(ks[hh], qh, _DIMS["nt"], preferred_element_type=F32))
                if masked:
                    p_t = jnp.where(qi * tq + qofs >= kpos, p_t, 0.0)
                ds_t = (p_t * lax.dot_general(vs[hh], doh, _DIMS["nt"], preferred_element_type=F32)).astype(BF16)
                dv_acc = dv_acc + jnp.dot(p_t.astype(BF16), doh, preferred_element_type=F32)
                dk_acc = dk_acc + jnp.dot(ds_t, qh, preferred_element_type=F32)
                dq_ref[rows, hh * LANES:(hh + 1) * LANES] += lax.dot_general(ds_t, ks[hh], _DIMS["tn"],
                                                                             preferred_element_type=F32)
                new.append((dk_acc, dv_acc))
            return tuple(new)

        carry = tuple((jnp.zeros((tk, LANES), F32), jnp.zeros((tk, LANES), F32)) for _ in range(nh))
        carry = step(kb, carry, True)
        carry = lax.fori_loop(kb + 1, nq, functools.partial(step, masked=False), carry)
        for hh in range(nh):
            dk_ref[:, hh * LANES:(hh + 1) * LANES] = carry[hh][0].astype(dk_ref.dtype)
            dv_ref[:, hh * LANES:(hh + 1) * LANES] = carry[hh][1].astype(dv_ref.dtype)

    res = pl.BlockSpec((t, wblk), lambda j, kb: (0, j))
    blk = pl.BlockSpec((tk, wblk), lambda j, kb: (kb, j))
    wide = (t, A_HEADS * LANES)
    return pl.pallas_call(
        body, name=name, grid=(A_HEADS // nh, t // tk), in_specs=[res, blk, blk, res], out_specs=[res, blk, blk],
        out_shape=[jax.ShapeDtypeStruct(wide, F32), jax.ShapeDtypeStruct(wide, BF16), jax.ShapeDtypeStruct(wide, BF16)],
        compiler_params=_cparams("parallel", "arbitrary"),
    )(qb, k, v, doe)


_SPLITS = (M_WIDTH, M_WIDTH, M_WIDTH, M_WIDTH, M_HEADS, M_HEADS, Q_RANK, KV_RANK, A_ROPE)
_OFFS = tuple(sum(_SPLITS[:i]) for i in range(len(_SPLITS) + 1))
_GATE_BLOCK_KR = A_NOPE


def _split_w_in_t(wt):
    z = lambda n: jnp.zeros((n, wt.shape[1]), wt.dtype)
    w_g = jnp.concatenate([wt[_OFFS[4]:_OFFS[6]], z(_GATE_BLOCK_KR - 2 * M_HEADS), wt[_OFFS[8]:_OFFS[9]],
                           z(LANES - _GATE_BLOCK_KR - A_ROPE)], axis=0)
    return wt[_OFFS[6]:_OFFS[8]], w_g


def _merge_w_in_grad_t(g_m4, g_a2, g_g):
    return jnp.concatenate([g_m4, g_g[:2 * M_HEADS], g_a2, g_g[_GATE_BLOCK_KR:_GATE_BLOCK_KR + A_ROPE]], axis=0)


def _pad_heads(w, used):
    w3 = w.reshape(A_HEADS, used, w.shape[1])
    return jnp.pad(w3, ((0, 0), (0, LANES - used), (0, 0))).reshape(A_HEADS * LANES, w.shape[1])


def _unpad_heads(g, used):
    return g.reshape(A_HEADS, LANES, g.shape[1])[:, :used].reshape(A_HEADS * used, g.shape[1])


def _split_w_kv_b_t(wt):
    w3 = wt.reshape(A_HEADS, A_NOPE + A_V, wt.shape[1])
    pad = lambda part: jnp.pad(part, ((0, 0), (0, LANES - part.shape[1]), (0, 0))).reshape(A_HEADS * LANES, wt.shape[1])
    return pad(w3[:, :A_NOPE]), pad(w3[:, A_NOPE:])


def _merge_w_kv_b_grad_t(gk, gv):
    gk3 = gk.reshape(A_HEADS, LANES, gk.shape[1])[:, :A_NOPE]
    gv3 = gv.reshape(A_HEADS, LANES, gv.shape[1])[:, :A_V]
    return jnp.concatenate([gk3, gv3], axis=1).reshape(A_HEADS * (A_NOPE + A_V), gk.shape[1])


def _local_step(x, pos, target, nw, w, start_late_reduce):
    w_in = w["w_in"]
    w_a2, w_g = _split_w_in_t(w_in)
    w_qb = _pad_heads(w["w_q_b"], A_NOPE + A_ROPE)
    w_k, w_v = _split_w_kv_b_t(w["w_kv_b"])
    w_out_m, w_out_a = w["w_out"][:M_WIDTH], _pad_heads(w["w_out"][M_WIDTH:], A_V)
    w_gate, w_up, w_down = w["w_gate"], w["w_up"], w["w_down"]
    bias = jnp.pad(nw["b_gates"], ((0, 0), (0, LANES - 2 * M_HEADS)))
    mnorm = nw["mlstm_norm_w"].reshape(M_HEADS, 1, M_HEAD_DIM)
    n_m4 = 4 * M_WIDTH
    t_len = x.shape[0]

    u1 = _rmsnorm_fwd(x, nw["attn_norm_w"], "attn_norm")
    m4 = _matmul(u1, w_in, "nt", BF16, "proj_mlstm", b_rows=n_m4, lead_scale=(M_WIDTH, M_SCALE))
    a2 = _matmul(u1, w_a2, "nt", BF16, "proj_latents", tn=640)
    gates = _matmul(u1, w_g, "nt", F32, "proj_gates")
    q_t = _chunk_major_t(_matmul(w_in[:M_WIDTH], u1, "nt", BF16, "proj_q_t", lead_scale=(t_len, M_SCALE)))
    k_t = _chunk_major_t(_matmul(w_in[M_WIDTH:2 * M_WIDTH], u1, "nt", BF16, "proj_k_t"))
    hm, c_prev_all, nm_all = _mlstm_fwd(m4, k_t, gates, bias, mnorm, "mlstm_fwd")
    qa, kva = a2[:, :Q_RANK], a2[:, Q_RANK:]
    qa_n = _rmsnorm_fwd(qa, nw["q_a_norm_w"], "q_a_norm")
    kv_n = _rmsnorm_fwd(kva, nw["kv_a_norm_w"], "kv_a_norm")
    qr, kr, vv = _qkv_up_rope(qa_n, kv_n, w_qb, w_k, w_v, gates, pos, "qkv_up_rope")
    ha, qb = _flash_fwd(qr, kr, vv, "attn_fwd")
    h1 = _matmul([hm, ha], [w_out_m, w_out_a], "nn", F32, "out_proj", residual=x)
    u2 = _rmsnorm_fwd(h1, nw["ffn_norm_w"], "ffn_norm")
    gg, uu, act = _ffn_gate_up(u2, w_gate, w_up, "ffn_gate_up")
    h2 = _matmul(act, w_down, "nn", F32, "ffn_down", residual=h1)

    dh2, dh2_b, loss, g_final = _final_loss(h2, target, nw["final_norm_w"].reshape(1, D_MODEL), "final_loss")
    grads = {"final_norm_w": g_final.reshape(D_MODEL)}
    grads_b = {}
    dgg, duu = _ffn_down_bwd(dh2_b, w_down, gg, uu, "ffn_down_bwd")
    grads["w_down"], grads_b["w_down"] = _matmul(act, dh2_b, "tn", F32, "g_w_down", tm=256, tn=D_MODEL, also_bf16=True)
    du2 = _matmul([dgg, duu], [w_gate, w_up], "nn", F32, "d_u2", tm=1024, tn=512)
    grads["w_gate"], grads_b["w_gate"] = _matmul(dgg, u2, "tn", F32, "g_w_gate", tm=256, tn=D_MODEL, also_bf16=True)
    grads["w_up"], grads_b["w_up"] = _matmul(duu, u2, "tn", F32, "g_w_up", tm=256, tn=D_MODEL, also_bf16=True)
    dh1, dh1_b, grads["ffn_norm_w"] = _rmsnorm_bwd(du2, h1, nw["ffn_norm_w"], "ffn_norm_bwd", F32, residual=dh2,
                                                   also_bf16=True)
    dhm = _matmul(dh1_b, w_out_m, "nt", BF16, "d_hm")
    dha = _matmul(dh1_b, w_out_a, "nt", BF16, "d_ha")
    grads["w_out"] = jnp.concatenate([_matmul(hm, dh1_b, "tn", F32, "g_w_out_m", tn=D_MODEL),
                                      _unpad_heads(_matmul(ha, dh1_b, "tn", F32, "g_w_out_a", tn=D_MODEL), A_V)], axis=0)

    late, dha = start_late_reduce(grads, grads_b, dha)

    dqr, dkr, dvv = _flash_bwd(qb, kr, vv, _attn_delta(dha, ha, "attn_delta"), "attn_bwd")
    dm4, dgates, small = _mlstm_bwd(m4, q_t, gates, bias, mnorm, c_prev_all, nm_all, dhm, "mlstm_bwd")
    grads["mlstm_norm_w"] = small[:M_HEADS].reshape(1, M_HEADS, M_HEAD_DIM)
    grads["b_gates"] = small[M_HEADS:M_HEADS + 1, :2 * M_HEADS]
    dqpre, dgk = _rope_bwd(dqr, dkr, dgates, pos, "rope_bwd")
    dqa_n = _matmul(dqpre, w_qb, "nn", BF16, "d_qa_n")
    grads["w_q_b"] = _unpad_heads(_matmul(dqpre, qa_n, "tn", F32, "g_w_q_b"), A_NOPE + A_ROPE)
    dkv_n = _matmul([dkr, dvv], [w_k, w_v], "nn", BF16, "d_kv_n")
    grads["w_kv_b"] = _merge_w_kv_b_grad_t(_matmul(dkr, kv_n, "tn", F32, "g_w_k"),
                                           _matmul(dvv, kv_n, "tn", F32, "g_w_v"))
    dqa, grads["q_a_norm_w"] = _rmsnorm_bwd(dqa_n, qa, nw["q_a_norm_w"], "q_a_norm_bwd", BF16)
    dkva, grads["kv_a_norm_w"] = _rmsnorm_bwd(dkv_n, kva, nw["kv_a_norm_w"], "kv_a_norm_bwd", BF16)
    da2 = jnp.concatenate([dqa, dkva], axis=1)
    du1 = _matmul([dm4, da2, dgk], [w_in, w_a2, w_g], "nn", F32, "d_u1", b_rows=[n_m4, None, None])
    grads["w_in"] = _merge_w_in_grad_t(_matmul(dm4, u1, "tn", F32, "g_w_m4", tn=D_MODEL),
                                       _matmul(da2, u1, "tn", F32, "g_w_a2", tm=640, tn=D_MODEL),
                                       _matmul(dgk, u1, "tn", F32, "g_w_g", tn=D_MODEL))
    grad_x, grads["attn_norm_w"] = _rmsnorm_bwd(du1, x, nw["attn_norm_w"], "attn_norm_bwd", F32, residual=dh1)
    return loss, grad_x, grads, grads_b, late


MESH = pl.DeviceIdType.MESH
N_CHIPS = 4
EARLY = ("w_in", "w_q_b", "w_kv_b")
LATE = ("w_out", "w_gate", "w_up", "w_down")
BIG = EARLY + LATE
TRANSPOSED = ("w_in", "w_q_b", "w_kv_b", "w_gate", "w_up")
LANE_HALVED = ("w_in", "w_out", "w_gate", "w_up", "w_down")
SMALL = ("attn_norm_w", "b_gates", "mlstm_norm_w", "q_a_norm_w", "kv_a_norm_w", "ffn_norm_w", "final_norm_w")
HBM_SPEC = pl.BlockSpec(memory_space=pltpu.HBM)


def _stored(name, a):
    return a[0].T if name in TRANSPOSED else a[0]


def _unstored(name, a):
    return (a.T if name in TRANSPOSED else a)[None]


def _half_shape(name, shape):
    rs, cs = shape
    return (rs, cs // 2) if name in LANE_HALVED else (rs // 2, cs)


def _half(ref, name, h, *lead):
    rs, cs = ref.shape[-2:]
    if name in LANE_HALVED:
        return ref.at[(*lead, slice(None), pl.ds(h * (cs // 2), cs // 2))]
    return ref.at[(*lead, pl.ds(h * (rs // 2), rs // 2), slice(None))]


def _place():
    x, y, c = lax.axis_index("x"), lax.axis_index("y"), lax.axis_index("c")
    others = [(1 - x, y), (x, 1 - y), (1 - x, 1 - y)]
    return x, y, c, others


def _gather_weights(names, shards):
    n = len(shards)

    def body(*refs):
        _gather_body(names, refs[:n], refs[n:2 * n], *refs[2 * n:])

    gathered = pl.pallas_call(
        body, name="gather_weights", in_specs=[HBM_SPEC] * n, out_specs=[HBM_SPEC] * n,
        out_shape=[jax.ShapeDtypeStruct((N_CHIPS,) + s.shape, s.dtype) for s in shards],
        scratch_shapes=[pltpu.SemaphoreType.DMA((6 * n,)), pltpu.SemaphoreType.DMA((6 * n,))],
    )(*shards)
    return _with_own_slab(gathered, shards)


def _with_own_slab(gathered, shards):
    me = 2 * lax.axis_index("x") + lax.axis_index("y")
    return [lax.dynamic_update_slice(g, s[None], (me, 0, 0)) for g, s in zip(gathered, shards)]


def _gather_body(names, ins, outs, send_sems, recv_sems):
    x, y, c, others = _place()
    me = 2 * x + y
    sibling = (x, y, 1 - c)

    def copy(w, k, slab, core, to, src=None):
        dst = _half(outs[w], names[w], core, slab)
        return pltpu.make_async_remote_copy(
            src_ref=dst if src is None else src, dst_ref=dst, send_sem=send_sems.at[w * 6 + k],
            recv_sem=recv_sems.at[w * 6 + k], device_id=to, device_id_type=MESH)

    sends = []
    for w in range(len(names)):
        for j, chip in enumerate(others):
            cp = copy(w, j, me, c, (*chip, c), src=_half(ins[w], names[w], c))
            cp.start()
            sends.append(cp)
    for w in range(len(names)):
        for j, (ox, oy) in enumerate(others):
            slab = 2 * ox + oy
            copy(w, j, slab, c, (x, y, c)).wait_recv()
            fwd = copy(w, 3 + j, slab, c, sibling)
            fwd.start()
            sends.append(fwd)
    for w in range(len(names)):
        for j, (ox, oy) in enumerate(others):
            copy(w, 3 + j, 2 * ox + oy, 1 - c, (x, y, c)).wait_recv()
    for cp in sends:
        cp.wait_send()


GATHER_LATE_COLLECTIVE_ID = 1


def _gather_weights_async(names, shards):
    n = len(shards)
    src = [jax.new_ref(s, memory_space=pltpu.MemorySpace.HBM) for s in shards]
    dst = [jax.empty_ref(jax.ShapeDtypeStruct((N_CHIPS,) + s.shape, s.dtype), memory_space=pltpu.MemorySpace.HBM)
           for s in shards]

    @pl.kernel(mesh=plsc.ScalarSubcoreMesh(axis_name="sequencer", num_cores=1), name="gather_weights_async",
               scratch_types=(pltpu.SemaphoreType.DMA((6 * n,)), pltpu.SemaphoreType.DMA((6 * n,))),
               compiler_params=pltpu.CompilerParams(collective_id=GATHER_LATE_COLLECTIVE_ID))
    def launch(send_sems, recv_sems):
        x, y, c, others = _place()
        peers = [(ox, oy, c) for ox, oy in others] + [(x, y, 1 - c)]
        barrier = pltpu.get_barrier_semaphore()
        for peer in peers:
            pl.semaphore_signal(barrier, inc=1, device_id=peer, device_id_type=MESH)
        pl.semaphore_wait(barrier, len(peers))
        _gather_body(names, src, dst, send_sems, recv_sems)

    launch()
    return _with_own_slab([d[...] for d in dst], shards)


def _exchange(arrays, out_shapes, plan, copies_per_array, name):
    n = len(arrays)

    def body(*refs):
        ins, outs = refs[:n], refs[n:2 * n]
        send_sems, recv_sems = refs[2 * n:]
        rem = [pltpu.make_async_remote_copy(src_ref=s, dst_ref=d, send_sem=send_sems.at[k], recv_sem=recv_sems.at[k],
                                            device_id=to, device_id_type=MESH)
               for k, (s, d, to) in enumerate(plan(ins, outs, _place()))]
        for cp in rem:
            cp.start()
        for cp in rem:
            cp.wait_recv()
        for cp in rem:
            cp.wait_send()

    return pl.pallas_call(
        body, name=name, in_specs=[HBM_SPEC] * n, out_specs=[HBM_SPEC] * n, out_shape=out_shapes,
        scratch_shapes=[pltpu.SemaphoreType.DMA((copies_per_array * n,)),
                        pltpu.SemaphoreType.DMA((copies_per_array * n,))],
    )(*arrays)


def _plan_to_sibling(ins, outs, place):
    x, y, c, _ = place
    return [(ins[w], outs[w], (x, y, 1 - c)) for w in range(len(ins))]


N_PEERS = 7


def _shard_sum_direct(g3, got7, name, where):
    hs = tuple(got7.shape[1:])
    if name in LANE_HALVED:
        g_in, g_spec = g3, pl.BlockSpec((1,) + hs, lambda i, p: (p[0], 0, p[1]))
    else:
        g_in, g_spec = g3.reshape((N_CHIPS, 2) + hs), pl.BlockSpec((1, 1) + hs, lambda i, p: (p[0], p[1], 0, 0))

    def body(p_ref, g_ref, r_ref, o_ref):
        acc = g_ref[(0,) * (len(g_ref.shape) - 2)]
        for k in range(N_PEERS):
            acc = acc + r_ref[k].astype(F32)
        o_ref[...] = acc

    return pl.pallas_call(
        body, name="shard_sum_%s" % name,
        grid_spec=pltpu.PrefetchScalarGridSpec(
            num_scalar_prefetch=1, grid=(1,),
            in_specs=[g_spec, pl.BlockSpec((N_PEERS,) + hs, lambda i, p: (0, 0, 0))],
            out_specs=pl.BlockSpec(hs, lambda i, p: (0, 0))),
        out_shape=jax.ShapeDtypeStruct(hs, F32), compiler_params=_cparams("arbitrary"),
    )(where, g_in, got7)


def _adamw_halves(w, m, v, mine, other, name, core):
    hs = tuple(mine.shape)
    full = pl.BlockSpec(hs, (lambda h, c_ref: (0, h)) if name in LANE_HALVED else (lambda h, c_ref: (h, 0)))
    half = pl.BlockSpec(hs, lambda h, c_ref: (0, 0))

    def body(c_ref, w_ref, m_ref, v_ref, a_ref, b_ref, g_ref, d_ref, mo_ref, vo_ref):
        g = jnp.where(pl.program_id(0) == c_ref[0], a_ref[...], b_ref[...])
        delta, m_new, v_new = _adamw_math(w_ref[...], g, m_ref[...], v_ref[...])
        g_ref[...] = g
        d_ref[...] = delta
        mo_ref[...] = m_new
        vo_ref[...] = v_new

    out = jax.ShapeDtypeStruct(w.shape, F32)
    return pl.pallas_call(
        body, name="adamw_%s" % name,
        grid_spec=pltpu.PrefetchScalarGridSpec(num_scalar_prefetch=1, grid=(2,), in_specs=[full, full, full, half, half],
                                               out_specs=[full] * 4),
        out_shape=[out] * 4, compiler_params=_cparams("parallel"),
    )(core, w, m, v, mine, other)


def _adamw_math(w, g, m, v):
    m = ADAM_B1 * m + (1.0 - ADAM_B1) * g
    v = ADAM_B2 * v + (1.0 - ADAM_B2) * (g * g)
    m_hat = m / (1.0 - ADAM_B1 ** ADAM_STEP)
    v_hat = v / (1.0 - ADAM_B2 ** ADAM_STEP)
    delta = -ADAM_LR * (m_hat / (jnp.sqrt(v_hat) + ADAM_EPS) + ADAM_WD * w)
    return delta, m, v


def _core_index():
    return lax.axis_index("c").astype(jnp.int32).reshape(1)


DIRECT_REDUCE_COLLECTIVE_ID = {"late": 2, "early": 3}


def _exchange_contributions_async(names, gb3, tag):
    n = len(gb3)
    src = [jax.new_ref(g, memory_space=pltpu.MemorySpace.HBM) for g in gb3]
    dst = [jax.empty_ref(jax.ShapeDtypeStruct((N_PEERS,) + _half_shape(nm, g.shape[1:]), g.dtype),
                         memory_space=pltpu.MemorySpace.HBM) for nm, g in zip(names, gb3)]

    @pl.kernel(mesh=plsc.ScalarSubcoreMesh(axis_name="sequencer", num_cores=1), name="contributions_async_" + tag,
               scratch_types=(pltpu.SemaphoreType.DMA((N_PEERS * n,)), pltpu.SemaphoreType.DMA((N_PEERS * n,))),
               compiler_params=pltpu.CompilerParams(collective_id=DIRECT_REDUCE_COLLECTIVE_ID[tag]))
    def launch(send_sems, recv_sems):
        x, y, c, _ = _place()
        peers = [(x ^ ((k >> 2) & 1), y ^ ((k >> 1) & 1), c ^ (k & 1)) for k in range(1, N_PEERS + 1)]
        barrier = pltpu.get_barrier_semaphore()
        for peer in peers:
            pl.semaphore_signal(barrier, inc=1, device_id=peer, device_id_type=MESH)
        pl.semaphore_wait(barrier, N_PEERS)
        rem = []
        for w in range(n):
            for k, (px, py, pc) in enumerate(peers):
                rem.append(pltpu.make_async_remote_copy(
                    src_ref=_half(src[w], names[w], pc, 2 * px + py), dst_ref=dst[w].at[k],
                    send_sem=send_sems.at[N_PEERS * w + k], recv_sem=recv_sems.at[N_PEERS * w + k],
                    device_id=(px, py, pc), device_id_type=MESH))
        for cp in rem:
            cp.start()
        for cp in rem:
            cp.wait_recv()
        for cp in rem:
            cp.wait_send()

    launch()
    return [d[...] for d in dst]


def _shard_index():
    return (2 * lax.axis_index("x") + lax.axis_index("y")).astype(jnp.int32).reshape(1)


def _with_sibling_half(names, mine, tag):
    other = _exchange(mine, [jax.ShapeDtypeStruct(m.shape, F32) for m in mine], _plan_to_sibling, 1,
                      "sibling_result_" + tag)
    return {n: (a, b) for n, a, b in zip(names, mine, other)}


SMALL_ROWS = 8
SMALL_LAYOUT = {"attn_norm_w": (0, 0, 1024), "ffn_norm_w": (1, 0, 1024), "final_norm_w": (2, 0, 1024),
                "q_a_norm_w": (3, 0, 384), "kv_a_norm_w": (3, 384, 256), "mlstm_norm_w": (4, 0, 512),
                "b_gates": (4, 512, 8)}
LOSS_SLOT = (5, 0)


def _small_allreduce_adamw(grads, loss, wts, mom, vel):
    shapes = {n: wts[n].shape for n in SMALL}
    flat = lambda d: [d[n].reshape(1, SMALL_LAYOUT[n][2]) for n in SMALL]
    ns = len(SMALL)

    def body(*refs):
        g_in, loss_ref = refs[:ns], refs[ns]
        w_in_, m_in, v_in = refs[ns + 1:2 * ns + 1], refs[2 * ns + 1:3 * ns + 1], refs[3 * ns + 1:4 * ns + 1]
        outs = refs[4 * ns + 1:8 * ns + 2]
        tile, slots, send_sems, recv_sems = refs[8 * ns + 2:]
        x, y, c, _ = _place()
        me = 4 * x + 2 * y + c
        tile[...] = jnp.zeros_like(tile)
        for n, g_ref in zip(SMALL, g_in):
            r, c0, width = SMALL_LAYOUT[n]
            tile[r:r + 1, c0:c0 + width] = g_ref[...]
        tile[LOSS_SLOT[0]:LOSS_SLOT[0] + 1, 0:LANES] = loss_ref[...]
        slots[me] = tile[...]
        copies = []
        for k in range(1, N_PEERS + 1):
            to = (x ^ ((k >> 2) & 1), y ^ ((k >> 1) & 1), c ^ (k & 1))
            cp = pltpu.make_async_remote_copy(src_ref=tile, dst_ref=slots.at[me], send_sem=send_sems.at[k - 1],
                                              recv_sem=recv_sems.at[k - 1], device_id=to, device_id_type=MESH)
            cp.start()
            copies.append(cp)
        for k in range(1, N_PEERS + 1):
            pltpu.make_async_remote_copy(src_ref=tile, dst_ref=slots.at[me ^ k], send_sem=send_sems.at[k - 1],
                                         recv_sem=recv_sems.at[k - 1], device_id=(x, y, c),
                                         device_id_type=MESH).wait_recv()
        for cp in copies:
            cp.wait_send()
        total = slots[0]
        for d in range(1, N_PEERS + 1):
            total = total + slots[d]
        for i, n in enumerate(SMALL):
            r, c0, width = SMALL_LAYOUT[n]
            g = total[r:r + 1, c0:c0 + width]
            delta, m_new, v_new = _adamw_math(w_in_[i][...], g, m_in[i][...], v_in[i][...])
            outs[i][...] = g
            outs[ns + i][...] = delta
            outs[2 * ns + i][...] = m_new
            outs[3 * ns + i][...] = v_new
        outs[4 * ns][...] = total[LOSS_SLOT[0]:LOSS_SLOT[0] + 1, 0:LANES]

    vm = pl.BlockSpec(memory_space=pltpu.VMEM)
    vec = [jax.ShapeDtypeStruct((1, SMALL_LAYOUT[n][2]), F32) for n in SMALL]
    res = pl.pallas_call(
        body, name="small_allreduce_adamw", in_specs=[vm] * (4 * ns + 1), out_specs=[vm] * (4 * ns + 1),
        out_shape=vec * 4 + [jax.ShapeDtypeStruct((1, LANES), F32)],
        scratch_shapes=[pltpu.VMEM((SMALL_ROWS, D_MODEL), F32), pltpu.VMEM((N_PEERS + 1, SMALL_ROWS, D_MODEL), F32),
                        pltpu.SemaphoreType.DMA((N_PEERS,)), pltpu.SemaphoreType.DMA((N_PEERS,))],
    )(*flat(grads), loss, *flat(wts), *flat(mom), *flat(vel))
    groups = [{n: r.reshape(shapes[n]) for n, r in zip(SMALL, res[i * ns:(i + 1) * ns])} for i in range(4)]
    return (*groups, res[4 * ns][0, 0])


def kernel(x, positions, attn_norm_w, w_in, b_gates, mlstm_norm_w, q_a_norm_w, w_q_b, kv_a_norm_w, w_kv_b, w_out, ffn_norm_w, w_gate, w_up, w_down, final_norm_w, loss_target, m_attn_norm_w, m_w_in, m_b_gates, m_mlstm_norm_w, m_q_a_norm_w, m_w_q_b, m_kv_a_norm_w, m_w_kv_b, m_w_out, m_ffn_norm_w, m_w_gate, m_w_up, m_w_down, m_final_norm_w, v_attn_norm_w, v_w_in, v_b_gates, v_mlstm_norm_w, v_q_a_norm_w, v_w_q_b, v_kv_a_norm_w, v_w_kv_b, v_w_out, v_ffn_norm_w, v_w_gate, v_w_up, v_w_down, v_final_norm_w):
    names = ("attn_norm_w", "w_in", "b_gates", "mlstm_norm_w", "q_a_norm_w", "w_q_b", "kv_a_norm_w", "w_kv_b", "w_out",
             "ffn_norm_w", "w_gate", "w_up", "w_down", "final_norm_w")
    wts = dict(zip(names, (attn_norm_w, w_in, b_gates, mlstm_norm_w, q_a_norm_w, w_q_b, kv_a_norm_w, w_kv_b, w_out,
                           ffn_norm_w, w_gate, w_up, w_down, final_norm_w)))
    mom = dict(zip(names, (m_attn_norm_w, m_w_in, m_b_gates, m_mlstm_norm_w, m_q_a_norm_w, m_w_q_b, m_kv_a_norm_w,
                           m_w_kv_b, m_w_out, m_ffn_norm_w, m_w_gate, m_w_up, m_w_down, m_final_norm_w)))
    vel = dict(zip(names, (v_attn_norm_w, v_w_in, v_b_gates, v_mlstm_norm_w, v_q_a_norm_w, v_w_q_b, v_kv_a_norm_w,
                           v_w_kv_b, v_w_out, v_ffn_norm_w, v_w_gate, v_w_up, v_w_down, v_final_norm_w)))
    t = x.shape[1]

    shards = {n: _stored(n, wts[n]).astype(BF16) for n in BIG}
    first, later = ("w_in",), tuple(n for n in BIG if n != "w_in")
    now = _gather_weights(first, [shards[n] for n in first])
    now, later_in = lax.optimization_barrier((now, [shards[n] for n in later]))
    behind = _gather_weights_async(later, later_in)
    full = {n: g.reshape(N_CHIPS * g.shape[1], g.shape[2]) for n, g in zip(first + later, list(now) + list(behind))}

    nw = {n: wts[n] for n in SMALL}
    by_shard = lambda g: g.reshape(N_CHIPS, g.shape[0] // N_CHIPS, g.shape[1])

    def start_reduce(group, tag, grads, grads_b, marker):
        g3 = [by_shard(grads[n]) for n in group]
        gb3 = [by_shard(grads_b[n]) if n in grads_b else g.astype(BF16) for n, g in zip(group, g3)]
        gb3, marker = lax.optimization_barrier((gb3, marker))
        return (g3, _exchange_contributions_async(group, gb3, tag)), marker

    loss, grad_x, grads, grads_b, (late_g3, late_got7) = _local_step(
        x[0], positions.reshape(t, 1), loss_target[0], nw, full, functools.partial(start_reduce, LATE, "late"))

    (early_g3, early_got7), late_got7 = start_reduce(EARLY, "early", grads, grads_b, list(late_got7))
    where = jnp.concatenate([_shard_index(), _core_index()])
    outs_g, outs_d, outs_m, outs_v = {}, {}, {}, {}

    def finish(group, g3, got7, tag):
        mine = [_shard_sum_direct(g, r7, n, where) for n, g, r7 in zip(group, g3, got7)]
        halves = _with_sibling_half(group, mine, tag)
        for n in group:
            res = _adamw_halves(_stored(n, wts[n]), _stored(n, mom[n]), _stored(n, vel[n]), *halves[n], n, where[1:])
            outs_g[n], outs_d[n], outs_m[n], outs_v[n] = [_unstored(n, r) for r in res]

    finish(LATE, late_g3, late_got7, "late")
    small_g, small_d, small_m, small_v, total_loss = _small_allreduce_adamw(grads, loss, wts, mom, vel)
    finish(EARLY, early_g3, early_got7, "early")
    outs_g.update(small_g)
    outs_d.update(small_d)
    outs_m.update(small_m)
    outs_v.update(small_v)
    return (total_loss, grad_x[None], *[outs_g[n] for n in names], *[outs_d[n] for n in names],
            *[outs_m[n] for n in names], *[outs_v[n] for n in names])
```

```python
import functools
import math

import jax
import jax.numpy as jnp
from jax import lax
from jax.experimental import pallas as pl
from jax.experimental.pallas import tpu as pltpu
from jax.experimental.pallas import tpu_sc as plsc

F32 = jnp.float32
BF16 = jnp.bfloat16

D_MODEL = 1024
M_HEADS = 4
M_HEAD_DIM = 128
M_WIDTH = M_HEADS * M_HEAD_DIM
M_CHUNK = 64
A_HEADS = 8
A_NOPE = 64
A_ROPE = 32
A_V = 64
A_WIDTH = A_HEADS * A_V
A_QK_PAD = 128
Q_RANK = 384
KV_RANK = 256
ROPE_THETA = 10000.0
D_FF = 2816
D_IN = 2728
EPS = 1e-6
ATTN_SCALE = (A_NOPE + A_ROPE) ** -0.5
M_SCALE = M_HEAD_DIM ** -0.5

ADAM_LR = 0.001
ADAM_B1 = 0.9
ADAM_B2 = 0.999
ADAM_EPS = 1e-08
ADAM_WD = 0.01
ADAM_STEP = 10

VMEM_LIMIT_BYTES = 56 * 1024 * 1024
LANES = 128
NEG_INF = float("-inf")


def _cparams(*sem):
    return pltpu.CompilerParams(dimension_semantics=sem if sem else None, vmem_limit_bytes=VMEM_LIMIT_BYTES)


_DIMS = {"nn": (((1,), (0,)), ((), ())), "nt": (((1,), (1,)), ((), ())), "tn": (((0,), (0,)), ((), ()))}


def _matmul(a, b, mode, out_dtype, name, tm=None, tn=1024, residual=None, a_rows=None, b_rows=None, lead_scale=None,
            lead_row_scale=None, also_bf16=False):
    a_list = list(a) if isinstance(a, (list, tuple)) else [a]
    b_list = list(b) if isinstance(b, (list, tuple)) else [b]
    rows_list = list(b_rows) if isinstance(b_rows, (list, tuple)) else [b_rows] * len(b_list)
    assert len(a_list) == len(b_list) == len(rows_list)
    dims = _DIMS[mode]
    specs, m, n = [], None, None
    for aa, bb, rr in zip(a_list, b_list, rows_list):
        b_shape = bb.shape if rr is None else (rr, bb.shape[1])
        a_shape = aa.shape if a_rows is None else (a_rows, aa.shape[1])
        if mode == "nn":
            (m1, k), (k2, n1) = a_shape, b_shape
        elif mode == "nt":
            (m1, k), (n1, k2) = a_shape, b_shape
        else:
            (k, m1), (k2, n1) = aa.shape, b_shape
        assert k == k2 and (m is None or (m, n) == (m1, n1)), (aa.shape, bb.shape, mode)
        m, n = m1, n1
        specs.append(k)
    if tm is None:
        tm = 512 if mode == "tn" else 1024
    tm, tn = min(tm, m), min(tn, n)
    assert m % tm == 0 and n % tn == 0, (m, n, tm, tn)
    in_specs = []
    for k in specs:
        in_specs.append(pl.BlockSpec((k, tm), lambda i, j: (0, i)) if mode == "tn"
                        else pl.BlockSpec((tm, k), lambda i, j: (i, 0)))
        in_specs.append(pl.BlockSpec((tn, k), lambda i, j: (j, 0)) if mode == "nt"
                        else pl.BlockSpec((k, tn), lambda i, j: (0, j)))
    o_spec = pl.BlockSpec((tm, tn), lambda i, j: (i, j))
    n_pairs = len(specs)

    def body(*refs):
        acc = None
        for p in range(n_pairs):
            part = lax.dot_general(refs[2 * p][...].astype(BF16), refs[2 * p + 1][...].astype(BF16), dims,
                                   preferred_element_type=F32)
            acc = part if acc is None else acc + part
        if lead_scale is not None:
            col = pl.program_id(1) * tn + lax.broadcasted_iota(jnp.int32, (1, tn), 1)
            acc = acc * jnp.where(col < lead_scale[0], lead_scale[1], 1.0)
        if lead_row_scale is not None:
            row = pl.program_id(0) * tm + lax.broadcasted_iota(jnp.int32, (tm, 1), 0)
            acc = acc * jnp.where(row < lead_row_scale[0], lead_row_scale[1], 1.0)
        if residual is not None:
            acc = acc + refs[2 * n_pairs][...].astype(F32)
        outs = refs[2 * n_pairs + (residual is not None):]
        outs[0][...] = acc.astype(outs[0].dtype)
        if also_bf16:
            outs[1][...] = acc.astype(BF16)

    ins = [x for pair in zip(a_list, b_list) for x in pair] + ([residual] if residual is not None else [])
    in_specs = in_specs + ([o_spec] if residual is not None else [])
    out_shape = jax.ShapeDtypeStruct((m, n), out_dtype)
    if also_bf16:
        return pl.pallas_call(
            body, name=name, grid=(m // tm, n // tn), in_specs=in_specs, out_specs=[o_spec, o_spec],
            out_shape=[out_shape, jax.ShapeDtypeStruct((m, n), BF16)],
            compiler_params=_cparams("parallel", "parallel"),
        )(*ins)
    return pl.pallas_call(
        body, name=name, grid=(m // tm, n // tn), in_specs=in_specs, out_specs=o_spec, out_shape=out_shape,
        compiler_params=_cparams("parallel", "parallel"),
    )(*ins)


def _rms(xf, w):
    return xf * lax.rsqrt(jnp.mean(xf * xf, axis=-1, keepdims=True) + EPS) * w


def _rms_bwd(dyf, xf, w):
    r = lax.rsqrt(jnp.mean(xf * xf, axis=-1, keepdims=True) + EPS)
    xh = xf * r
    dyh = dyf * w
    return r * (dyh - xh * jnp.mean(dyh * xh, axis=-1, keepdims=True)), dyf * xh


def _accumulate(ref, part):
    @pl.when(pl.program_id(0) == 0)
    def _():
        ref[...] = part

    @pl.when(pl.program_id(0) > 0)
    def _():
        ref[...] += part


def _rows(tm, width):
    return pl.BlockSpec((tm, width), lambda i: (i, 0))


def _whole(arr):
    return pl.BlockSpec(arr.shape, lambda i: (0,) * arr.ndim)


def _proj_latents_norm(u, w_a2, w_qn, w_kvn, name, tm=512):
    t = u.shape[0]
    tm = min(tm, t)

    def body(u_ref, w_ref, wq_ref, wkv_ref, a2_ref, qn_ref, kvn_ref):
        a2 = lax.dot_general(u_ref[...], w_ref[...], _DIMS["nt"], preferred_element_type=F32).astype(BF16)
        a2_ref[...] = a2
        qn_ref[...] = _rms(a2[:, :Q_RANK].astype(F32), wq_ref[...]).astype(BF16)
        kvn_ref[...] = _rms(a2[:, Q_RANK:].astype(F32), wkv_ref[...]).astype(BF16)

    return pl.pallas_call(
        body, name=name, grid=(t // tm,), in_specs=[_rows(tm, u.shape[1]), _whole(w_a2), _whole(w_qn), _whole(w_kvn)],
        out_specs=[_rows(tm, Q_RANK + KV_RANK), _rows(tm, Q_RANK), _rows(tm, KV_RANK)],
        out_shape=[jax.ShapeDtypeStruct((t, Q_RANK + KV_RANK), BF16), jax.ShapeDtypeStruct((t, Q_RANK), BF16),
                   jax.ShapeDtypeStruct((t, KV_RANK), BF16)],
        compiler_params=_cparams("parallel"),
    )(u, w_a2, w_qn, w_kvn)


def _out_proj_norm(hm, ha, w_m, w_a, x, w_norm, name, tm=512):
    t = x.shape[0]
    tm = min(tm, t)

    def body(hm_ref, ha_ref, wm_ref, wa_ref, x_ref, wn_ref, h_ref, u_ref):
        h = (jnp.dot(hm_ref[...], wm_ref[...], preferred_element_type=F32)
             + jnp.dot(ha_ref[...], wa_ref[...], preferred_element_type=F32) + x_ref[...])
        h_ref[...] = h
        u_ref[...] = _rms(h, wn_ref[...]).astype(BF16)

    d = x.shape[1]
    return pl.pallas_call(
        body, name=name, grid=(t // tm,),
        in_specs=[_rows(tm, hm.shape[1]), _rows(tm, ha.shape[1]), _whole(w_m), _whole(w_a), _rows(tm, d), _whole(w_norm)],
        out_specs=[_rows(tm, d), _rows(tm, d)],
        out_shape=[jax.ShapeDtypeStruct((t, d), F32), jax.ShapeDtypeStruct((t, d), BF16)],
        compiler_params=_cparams("parallel"),
    )(hm, ha, w_m, w_a, x, w_norm)


def _d_u1_norm_bwd(das, ws, w_rows, x, w_norm, residual, name, tm=512):
    t, d = x.shape
    tm = min(tm, t)
    n = len(das)

    def body(*refs):
        x_ref, wn_ref, r_ref, dx_ref, dw_ref = refs[2 * n:]
        du = None
        for p in range(n):
            wp = refs[n + p][...] if w_rows[p] is None else refs[n + p][0:w_rows[p], :]
            part = jnp.dot(refs[p][...], wp, preferred_element_type=F32)
            du = part if du is None else du + part
        dx, dw_terms = _rms_bwd(du, x_ref[...], wn_ref[...])
        dx_ref[...] = dx + r_ref[...]
        _accumulate(dw_ref, jnp.sum(dw_terms, axis=0, keepdims=True))

    return pl.pallas_call(
        body, name=name, grid=(t // tm,),
        in_specs=[_rows(tm, a.shape[1]) for a in das] + [_whole(w) for w in ws] + [_rows(tm, d), _whole(w_norm), _rows(tm, d)],
        out_specs=[_rows(tm, d), pl.BlockSpec((1, d), lambda i: (0, 0))],
        out_shape=[jax.ShapeDtypeStruct((t, d), F32), jax.ShapeDtypeStruct((1, d), F32)],
        compiler_params=_cparams("arbitrary"),
    )(*das, *ws, x, w_norm, residual)


def _rmsnorm_fwd(x, w, name, tm=512):
    t, d = x.shape

    def body(x_ref, w_ref, o_ref):
        xf = x_ref[...].astype(F32)
        r = lax.rsqrt(jnp.mean(xf * xf, axis=-1, keepdims=True) + EPS)
        o_ref[...] = (xf * r * w_ref[...]).astype(o_ref.dtype)

    return pl.pallas_call(
        body, name=name, grid=(t // tm,),
        in_specs=[pl.BlockSpec((tm, d), lambda i: (i, 0)), pl.BlockSpec((1, d), lambda i: (0, 0))],
        out_specs=pl.BlockSpec((tm, d), lambda i: (i, 0)),
        out_shape=jax.ShapeDtypeStruct((t, d), BF16), compiler_params=_cparams("parallel"),
    )(x, w)


def _rmsnorm_bwd(dy, x, w, name, out_dtype, residual=None, also_bf16=False, tm=512):
    t, d = x.shape

    def body(dy_ref, x_ref, w_ref, *rest):
        dx_ref, dw_ref = rest[-3 if also_bf16 else -2], rest[-1]
        xf = x_ref[...].astype(F32)
        r = lax.rsqrt(jnp.mean(xf * xf, axis=-1, keepdims=True) + EPS)
        xh = xf * r
        dyf = dy_ref[...].astype(F32)
        dyh = dyf * w_ref[...]
        dx = r * (dyh - xh * jnp.mean(dyh * xh, axis=-1, keepdims=True))
        if residual is not None:
            dx = dx + rest[0][...].astype(F32)
        dx_ref[...] = dx.astype(dx_ref.dtype)
        if also_bf16:
            rest[-2][...] = dx.astype(BF16)
        part = jnp.sum(dyf * xh, axis=0, keepdims=True)

        @pl.when(pl.program_id(0) == 0)
        def _():
            dw_ref[...] = part

        @pl.when(pl.program_id(0) > 0)
        def _():
            dw_ref[...] += part

    row = pl.BlockSpec((tm, d), lambda i: (i, 0))
    vec = pl.BlockSpec((1, d), lambda i: (0, 0))
    ins = [dy, x, w] + ([residual] if residual is not None else [])
    extra = [jax.ShapeDtypeStruct((t, d), BF16)] if also_bf16 else []
    return pl.pallas_call(
        body, name=name, grid=(t // tm,), in_specs=[row, row, vec] + ([row] if residual is not None else []),
        out_specs=[row] + [row] * len(extra) + [vec],
        out_shape=[jax.ShapeDtypeStruct((t, d), out_dtype)] + extra + [jax.ShapeDtypeStruct((1, d), F32)],
        compiler_params=_cparams("arbitrary"),
    )(*ins)


def _final_loss(h, target, w, name, tm=512):
    t, d = h.shape

    def body(h_ref, t_ref, w_ref, dh_ref, dhb_ref, loss_ref, dw_ref):
        xf = h_ref[...]
        r = lax.rsqrt(jnp.mean(xf * xf, axis=-1, keepdims=True) + EPS)
        xh = xf * r
        err = xh * w_ref[...] - t_ref[...]
        part_loss = 0.5 * jnp.sum(jnp.sum(err * err, axis=-1, keepdims=True), axis=0, keepdims=True) * (1.0 / d)
        dy = err * (1.0 / d)
        dyh = dy * w_ref[...]
        dh = r * (dyh - xh * jnp.mean(dyh * xh, axis=-1, keepdims=True))
        dh_ref[...] = dh
        dhb_ref[...] = dh.astype(BF16)
        part_dw = jnp.sum(dy * xh, axis=0, keepdims=True)
        part_loss = jnp.broadcast_to(part_loss, (1, LANES))

        @pl.when(pl.program_id(0) == 0)
        def _():
            dw_ref[...] = part_dw
            loss_ref[...] = part_loss

        @pl.when(pl.program_id(0) > 0)
        def _():
            dw_ref[...] += part_dw
            loss_ref[...] += part_loss

    row = pl.BlockSpec((tm, d), lambda i: (i, 0))
    vec = pl.BlockSpec((1, d), lambda i: (0, 0))
    return pl.pallas_call(
        body, name=name, grid=(t // tm,), in_specs=[row, row, vec],
        out_specs=[row, row, pl.BlockSpec((1, LANES), lambda i: (0, 0)), vec],
        out_shape=[jax.ShapeDtypeStruct((t, d), F32), jax.ShapeDtypeStruct((t, d), BF16),
                   jax.ShapeDtypeStruct((1, LANES), F32), jax.ShapeDtypeStruct((1, d), F32)],
        compiler_params=_cparams("arbitrary"),
    )(h, target, w)


FFN_TN = 1408


def _ffn_gate_up(u, w_gate, w_up, name, tm=1024):
    t, d = u.shape
    n = w_gate.shape[0]
    tm = min(tm, t)

    def body(u_ref, wg_ref, wu_ref, g_ref, up_ref, act_ref):
        uu = u_ref[...]
        g = lax.dot_general(uu, wg_ref[...], _DIMS["nt"], preferred_element_type=F32)
        up = lax.dot_general(uu, wu_ref[...], _DIMS["nt"], preferred_element_type=F32)
        g_b, up_b = g.astype(BF16), up.astype(BF16)
        g_ref[...] = g_b
        up_ref[...] = up_b
        gf, uf = g_b.astype(F32), up_b.astype(F32)
        act_ref[...] = (gf * jax.nn.sigmoid(gf) * uf).astype(act_ref.dtype)

    w_spec = pl.BlockSpec((FFN_TN, d), lambda i, j: (j, 0))
    o_spec = pl.BlockSpec((tm, FFN_TN), lambda i, j: (i, j))
    out = jax.ShapeDtypeStruct((t, n), BF16)
    return pl.pallas_call(
        body, name=name, grid=(t // tm, n // FFN_TN),
        in_specs=[pl.BlockSpec((tm, d), lambda i, j: (i, 0)), w_spec, w_spec], out_specs=[o_spec] * 3,
        out_shape=[out] * 3, compiler_params=_cparams("parallel", "parallel"),
    )(u, w_gate, w_up)


def _ffn_down_bwd(dh, w_down, g, up, name, tm=1024):
    t, d = dh.shape
    n = w_down.shape[0]
    tm = min(tm, t)

    def body(dh_ref, w_ref, g_ref, up_ref, dg_ref, du_ref):
        df = lax.dot_general(dh_ref[...], w_ref[...], _DIMS["nt"], preferred_element_type=F32).astype(BF16).astype(F32)
        gf = g_ref[...].astype(F32)
        uf = up_ref[...].astype(F32)
        s = jax.nn.sigmoid(gf)
        dg_ref[...] = (df * uf * s * (1.0 + gf * (1.0 - s))).astype(dg_ref.dtype)
        du_ref[...] = (df * gf * s).astype(du_ref.dtype)

    o_spec = pl.BlockSpec((tm, FFN_TN), lambda i, j: (i, j))
    out = jax.ShapeDtypeStruct((t, n), BF16)
    return pl.pallas_call(
        body, name=name, grid=(t // tm, n // FFN_TN),
        in_specs=[pl.BlockSpec((tm, d), lambda i, j: (i, 0)), pl.BlockSpec((FFN_TN, d), lambda i, j: (j, 0)),
                  o_spec, o_spec],
        out_specs=[o_spec, o_spec], out_shape=[out, out], compiler_params=_cparams("parallel", "parallel"),
    )(dh, w_down, g, up)


M_BLOCK = 512
M_CHUNKS_PER_BLOCK = M_BLOCK // M_CHUNK
M_UNROLL = 2


def _log_sigmoid(z):
    return jnp.minimum(z, 0.0) - jnp.log(1.0 + jnp.exp(-jnp.abs(z)))


def _per_head(fn):
    return jnp.stack([fn(hd) for hd in range(M_HEADS)])


def _mlstm_chunk_fwd(q, k, v, gc, c_prev, n_prev, m_prev):
    L, H = M_CHUNK, M_HEADS
    tt = lax.broadcasted_iota(jnp.int32, (L, L), 0)
    ss = lax.broadcasted_iota(jnp.int32, (L, L), 1)
    eye = tt == ss
    causal = ss <= tt
    gct = jnp.transpose(gc)
    i_col = _per_head(lambda hd: gc[:, hd:hd + 1])
    f_col = _per_head(lambda hd: gc[:, H + hd:H + hd + 1])
    i_row = _per_head(lambda hd: gct[hd:hd + 1, :])
    lf_col = _log_sigmoid(f_col)
    lf_row = _log_sigmoid(_per_head(lambda hd: gct[H + hd:H + hd + 1, :]))
    b_col = jnp.sum(jnp.where(causal, lf_row, 0.0), axis=2, keepdims=True)
    b_row = jnp.sum(jnp.where(tt <= ss, lf_col, 0.0), axis=1, keepdims=True)
    g = jnp.sum(lf_col, axis=1, keepdims=True)
    a_row = g - b_row + i_row
    a_col = g - b_col + i_col
    m_loc = jnp.max(a_row, axis=2, keepdims=True)
    d_log = jnp.where(causal, b_col - b_row + i_row, NEG_INF)
    inter = b_col + m_prev
    m_t = jnp.maximum(jnp.max(d_log, axis=2, keepdims=True), inter)
    dmat = jnp.exp(d_log - m_t)
    amat = _per_head(lambda hd: lax.dot_general(q[hd], k[hd], _DIMS["nt"], preferred_element_type=F32))
    p = dmat * amat
    sig = jnp.exp(inter - m_t)
    c_prev_b = [c.astype(BF16) for c in c_prev]
    qc = _per_head(lambda hd: jnp.dot(q[hd], c_prev_b[hd], preferred_element_type=F32))
    p_b = p.astype(BF16)
    num = _per_head(lambda hd: jnp.dot(p_b[hd], v[hd], preferred_element_type=F32)) + sig * qc
    qf = _per_head(lambda hd: q[hd].astype(F32))
    qn = jnp.sum(qf * n_prev, axis=2, keepdims=True)
    den_raw = jnp.sum(p, axis=2, keepdims=True) + sig * qn
    floor = jnp.exp(-m_t)
    den = jnp.maximum(jnp.abs(den_raw), floor)
    h = num / den
    m_new = jnp.maximum(g + m_prev, m_loc)
    w_col = jnp.exp(a_col - m_new)
    w_row = jnp.exp(a_row - m_new)
    alpha = jnp.exp(g + m_prev - m_new)
    return dict(eye=eye, causal=causal, tt=tt, ss=ss, i_col=i_col, f_col=f_col, dmat=dmat, amat=amat, p=p, p_b=p_b,
                sig=sig, qc=qc, qf=qf, qn=qn, num=num, den_raw=den_raw, floor=floor, den=den, h=h, m_new=m_new,
                w_col=w_col, w_row=w_row, alpha=alpha, c_prev_b=c_prev_b, b_row=b_row, b_col=b_col, m_t=m_t)


def _chunk_major_t(xt):
    f, t = xt.shape
    return xt.reshape(f, t // M_CHUNK, M_CHUNK).transpose(1, 0, 2)


def _mlstm_fwd(m4, kt, gates, bias, norm_w, name):
    t = m4.shape[0]
    nblk = t // M_BLOCK
    nc = t // M_CHUNK
    L, dh = M_CHUNK, M_HEAD_DIM

    def body(m4_ref, kt_ref, g_ref, b_ref, w_ref, hm_ref, cp_ref, nm_ref, c_s, n_s, m_s):
        @pl.when(pl.program_id(0) == 0)
        def _():
            c_s[...] = jnp.zeros_like(c_s)
            n_s[...] = jnp.zeros_like(n_s)
            m_s[...] = jnp.zeros_like(m_s)

        row8 = lax.broadcasted_iota(jnp.int32, (8, dh), 0)

        def chunk(c, carry):
            rows = pl.ds(pl.multiple_of(c * L, L), L)
            gc = g_ref[rows, :] + b_ref[...]
            col = lambda part, hd: m4_ref[rows, part * M_WIDTH + hd * dh:part * M_WIDTH + (hd + 1) * dh]
            q = [col(0, hd) for hd in range(M_HEADS)]
            k = [col(1, hd) for hd in range(M_HEADS)]
            v = [col(2, hd) for hd in range(M_HEADS)]
            o = _per_head(lambda hd: col(3, hd).astype(F32))
            c_prev = [c_s[hd] for hd in range(M_HEADS)]
            n_prev = n_s[:, 0:1, :]
            m_prev = m_s[:, 0:1, 0:1]
            r = _mlstm_chunk_fwd(q, k, v, gc, c_prev, n_prev, m_prev)
            h = r["h"]
            hn = h * lax.rsqrt(jnp.mean(h * h, axis=-1, keepdims=True) + EPS) * w_ref[...]
            hm = (hn * jax.nn.sigmoid(o)).astype(hm_ref.dtype)
            nm = jnp.where(row8 == 0, n_prev, jnp.where(row8 == 1, m_prev, 0.0))
            kf = _per_head(lambda hd: k[hd].astype(F32))
            n_s[:, 0:1, :] = r["alpha"] * n_prev + jnp.sum(kf * r["w_col"], axis=1, keepdims=True)
            m_s[...] = jnp.broadcast_to(r["m_new"], (M_HEADS, 8, dh))
            for hd in range(M_HEADS):
                hm_ref[rows, hd * dh:(hd + 1) * dh] = hm[hd]
                cp_ref[hd, c] = r["c_prev_b"][hd]
                nm_ref[hd, c] = nm[hd]
                kw_t = (kt_ref[c, hd * dh:(hd + 1) * dh, :].astype(F32) * r["w_row"][hd]).astype(BF16)
                c_s[hd] = r["alpha"][hd] * c_prev[hd] + jnp.dot(kw_t, v[hd], preferred_element_type=F32)
            return carry

        lax.fori_loop(0, M_CHUNKS_PER_BLOCK, chunk, 0)

    return pl.pallas_call(
        body, name=name, grid=(nblk,),
        in_specs=[pl.BlockSpec((M_BLOCK, 4 * M_WIDTH), lambda i: (i, 0)),
                  pl.BlockSpec((M_CHUNKS_PER_BLOCK, M_WIDTH, L), lambda i: (i, 1, 0)),
                  pl.BlockSpec((M_BLOCK, LANES), lambda i: (i, 0)),
                  pl.BlockSpec((1, LANES), lambda i: (0, 0)),
                  pl.BlockSpec((M_HEADS, 1, dh), lambda i: (0, 0, 0))],
        out_specs=[pl.BlockSpec((M_BLOCK, M_WIDTH), lambda i: (i, 0)),
                   pl.BlockSpec((M_HEADS, M_CHUNKS_PER_BLOCK, dh, dh), lambda i: (0, i, 0, 0)),
                   pl.BlockSpec((M_HEADS, M_CHUNKS_PER_BLOCK, 8, dh), lambda i: (0, i, 0, 0))],
        out_shape=[jax.ShapeDtypeStruct((t, M_WIDTH), BF16),
                   jax.ShapeDtypeStruct((M_HEADS, nc, dh, dh), BF16),
                   jax.ShapeDtypeStruct((M_HEADS, nc, 8, dh), F32)],
        scratch_shapes=[pltpu.VMEM((M_HEADS, dh, dh), F32), pltpu.VMEM((M_HEADS, 8, dh), F32),
                        pltpu.VMEM((M_HEADS, 8, dh), F32)],
        compiler_params=_cparams("arbitrary"),
    )(m4, kt, gates, bias, norm_w)


def _mlstm_bwd(m4, qt, gates, bias, norm_w, c_prev_all, nm_all, dhm, name):
    t = m4.shape[0]
    nblk = t // M_BLOCK
    L, dh = M_CHUNK, M_HEAD_DIM

    def body(m4_ref, qt_ref, g_ref, b_ref, w_ref, cp_ref, nm_ref, dhm_ref, dm4_ref, dg_ref, small_ref, dc_s, dn_s):
        @pl.when(pl.program_id(0) == 0)
        def _():
            dc_s[...] = jnp.zeros_like(dc_s)
            dn_s[...] = jnp.zeros_like(dn_s)
            small_ref[...] = jnp.zeros_like(small_ref)

        lane = lax.broadcasted_iota(jnp.int32, (L, LANES), 1)
        row8 = lax.broadcasted_iota(jnp.int32, (8, LANES), 0)

        def chunk(ci, carry):
            c = M_CHUNKS_PER_BLOCK - 1 - ci
            rows = pl.ds(pl.multiple_of(c * L, L), L)
            gc = g_ref[rows, :] + b_ref[...]
            heads = range(M_HEADS)
            col = lambda part, hd: m4_ref[rows, part * M_WIDTH + hd * dh:part * M_WIDTH + (hd + 1) * dh]
            q = [col(0, hd) for hd in heads]
            k = [col(1, hd) for hd in heads]
            v = [col(2, hd) for hd in heads]
            o = _per_head(lambda hd: col(3, hd).astype(F32))
            q_t = [qt_ref[c, hd * dh:(hd + 1) * dh, :] for hd in heads]
            c_prev = [cp_ref[hd, c].astype(F32) for hd in heads]
            nm = _per_head(lambda hd: nm_ref[hd, c])
            n_prev = nm[:, 0:1, :]
            m_prev = nm[:, 1:2, 0:1]
            r = _mlstm_chunk_fwd(q, k, v, gc, c_prev, n_prev, m_prev)
            eye, tt, ss = r["eye"], r["tt"], r["ss"]
            h, den, sig, p = r["h"], r["den"], r["sig"], r["p"]
            w_col, alpha, c_prev_b = r["w_col"], r["alpha"], r["c_prev_b"]
            to_row = lambda colv: jnp.sum(jnp.where(eye, colv, 0.0), axis=1, keepdims=True)
            to_col = lambda rowv: jnp.sum(jnp.where(eye, rowv, 0.0), axis=2, keepdims=True)
            mm = lambda fn: _per_head(lambda hd: fn(hd))

            w_h = w_ref[...]
            rn = lax.rsqrt(jnp.mean(h * h, axis=-1, keepdims=True) + EPS)
            hh = h * rn
            og = jax.nn.sigmoid(o)
            dhm_c = _per_head(lambda hd: dhm_ref[rows, hd * dh:(hd + 1) * dh].astype(F32))
            dhn = dhm_c * og
            d_o = dhm_c * hh * w_h * og * (1.0 - og)
            d_norm = jnp.sum(dhn * hh, axis=1, keepdims=True)
            dhh = dhn * w_h
            dh_ = rn * (dhh - hh * jnp.mean(dhh * hh, axis=-1, keepdims=True))

            dnum = dh_ / den
            dden = -jnp.sum(dh_ * h, axis=-1, keepdims=True) / den
            dden_raw = jnp.where(jnp.abs(r["den_raw"]) >= r["floor"], dden * jnp.sign(r["den_raw"]), 0.0)
            dnum_b = dnum.astype(BF16)
            dp = mm(lambda hd: lax.dot_general(dnum_b[hd], v[hd], _DIMS["nt"], preferred_element_type=F32)) + dden_raw
            dc = [dc_s[hd] for hd in heads]
            dn = dn_s[:, 0:1, :]
            dc_b = [d.astype(BF16) for d in dc]
            g2_b = (sig * dnum).astype(BF16)
            sd = sig * dden_raw
            da_mat = (dp * r["dmat"]).astype(BF16)
            dqs = mm(lambda hd: lax.dot_general(g2_b[hd], c_prev_b[hd], _DIMS["nt"], preferred_element_type=F32)
                     + jnp.dot(da_mat[hd], k[hd], preferred_element_type=F32)) + sd * n_prev
            r_mat = mm(lambda hd: lax.dot_general(v[hd], dc_b[hd], _DIMS["nt"], preferred_element_type=F32)) + dn
            kf = _per_head(lambda hd: k[hd].astype(F32))
            dmat_t = jnp.exp(jnp.where(tt <= ss, r["b_row"] - r["b_col"] + r["i_col"], NEG_INF) - to_row(r["m_t"]))
            p_t = (dmat_t * mm(lambda hd: jnp.dot(k[hd], q_t[hd], preferred_element_type=F32))).astype(BF16)
            dp_t = mm(lambda hd: lax.dot_general(v[hd], dnum_b[hd], _DIMS["nt"], preferred_element_type=F32))
            da_t = ((dp_t + to_row(dden_raw)) * dmat_t).astype(BF16)
            d_k = mm(lambda hd: jnp.dot(da_t[hd], q[hd], preferred_element_type=F32)) + w_col * r_mat
            d_v = (mm(lambda hd: jnp.dot(p_t[hd], dnum_b[hd], preferred_element_type=F32))
                   + w_col * mm(lambda hd: jnp.dot(k[hd], dc_b[hd], preferred_element_type=F32)))
            da_col = jnp.sum(kf * r_mat, axis=-1, keepdims=True) * w_col
            dsig = jnp.sum(dnum * r["qc"], axis=-1, keepdims=True) + dden_raw * r["qn"]
            pp = dp * p
            r1 = jnp.sum(pp, axis=2, keepdims=True)
            c1_col = to_col(jnp.sum(pp, axis=1, keepdims=True))
            dcc = _per_head(lambda hd: dc[hd] * c_prev[hd])
            d_alpha = (jnp.sum(jnp.sum(dcc, axis=2, keepdims=True), axis=1, keepdims=True)
                       + jnp.sum(dn * n_prev, axis=2, keepdims=True))
            dgl = jnp.sum(da_col, axis=1, keepdims=True) + d_alpha * alpha
            db_col = r1 + dsig * sig - c1_col - da_col
            dli_col = c1_col + da_col
            dlf_col = jnp.sum(jnp.where(ss >= tt, to_row(db_col), 0.0), axis=2, keepdims=True) + dgl
            df_col = dlf_col * jax.nn.sigmoid(-r["f_col"])

            dn_s[:, 0:1, :] = alpha * dn + jnp.sum(sd * r["qf"], axis=1, keepdims=True)
            dq_out = (dqs * M_SCALE).astype(dm4_ref.dtype)
            dk_out, dv_out, do_out = d_k.astype(dm4_ref.dtype), d_v.astype(dm4_ref.dtype), d_o.astype(dm4_ref.dtype)
            dg_tile = jnp.zeros((L, LANES), F32)
            small = jnp.zeros((8, LANES), F32)
            for hd in heads:
                dc_s[hd] = alpha[hd] * dc[hd] + jnp.dot(q_t[hd], g2_b[hd], preferred_element_type=F32)
                dm4_ref[rows, hd * dh:(hd + 1) * dh] = dq_out[hd]
                dm4_ref[rows, M_WIDTH + hd * dh:M_WIDTH + (hd + 1) * dh] = dk_out[hd]
                dm4_ref[rows, 2 * M_WIDTH + hd * dh:2 * M_WIDTH + (hd + 1) * dh] = dv_out[hd]
                dm4_ref[rows, 3 * M_WIDTH + hd * dh:3 * M_WIDTH + (hd + 1) * dh] = do_out[hd]
                dg_tile = (dg_tile + jnp.where(lane == hd, dli_col[hd], 0.0)
                           + jnp.where(lane == M_HEADS + hd, df_col[hd], 0.0))
                small = small + jnp.where(row8 == hd, d_norm[hd], 0.0)
            dg_ref[rows, :] = dg_tile
            small = small + jnp.where(row8 == M_HEADS, jnp.sum(dg_tile, axis=0, keepdims=True), 0.0)
            small_ref[...] += small
            return carry

        lax.fori_loop(0, M_CHUNKS_PER_BLOCK, chunk, 0, unroll=M_UNROLL)

    rev = lambda i: nblk - 1 - i
    return pl.pallas_call(
        body, name=name, grid=(nblk,),
        in_specs=[pl.BlockSpec((M_BLOCK, 4 * M_WIDTH), lambda i: (rev(i), 0)),
                  pl.BlockSpec((M_CHUNKS_PER_BLOCK, M_WIDTH, L), lambda i: (rev(i), 0, 0)),
                  pl.BlockSpec((M_BLOCK, LANES), lambda i: (rev(i), 0)),
                  pl.BlockSpec((1, LANES), lambda i: (0, 0)),
                  pl.BlockSpec((M_HEADS, 1, dh), lambda i: (0, 0, 0)),
                  pl.BlockSpec((M_HEADS, M_CHUNKS_PER_BLOCK, dh, dh), lambda i: (0, rev(i), 0, 0)),
                  pl.BlockSpec((M_HEADS, M_CHUNKS_PER_BLOCK, 8, dh), lambda i: (0, rev(i), 0, 0)),
                  pl.BlockSpec((M_BLOCK, M_WIDTH), lambda i: (rev(i), 0))],
        out_specs=[pl.BlockSpec((M_BLOCK, 4 * M_WIDTH), lambda i: (rev(i), 0)),
                   pl.BlockSpec((M_BLOCK, LANES), lambda i: (rev(i), 0)),
                   pl.BlockSpec((8, LANES), lambda i: (0, 0))],
        out_shape=[jax.ShapeDtypeStruct((t, 4 * M_WIDTH), BF16), jax.ShapeDtypeStruct((t, LANES), F32),
                   jax.ShapeDtypeStruct((8, LANES), F32)],
        scratch_shapes=[pltpu.VMEM((M_HEADS, dh, dh), F32), pltpu.VMEM((M_HEADS, 8, dh), F32)],
        compiler_params=_cparams("arbitrary"),
    )(m4, qt, gates, bias, norm_w, c_prev_all, nm_all, dhm)


def _rope_tables(pos_col, width):
    lane = lax.broadcasted_iota(jnp.int32, (1, width), 1) % A_QK_PAD
    half = A_ROPE // 2
    first = (lane >= A_NOPE) & (lane < A_NOPE + half)
    second = (lane >= A_NOPE + half) & (lane < A_NOPE + A_ROPE)
    idx = jnp.where(first, lane - A_NOPE, lane - A_NOPE - half).astype(F32)
    inv_freq = jnp.exp(idx * (-math.log(ROPE_THETA) / half))
    ang = pos_col.astype(F32) * inv_freq
    cos, sin = jnp.cos(ang), jnp.sin(ang)
    rot = first | second
    return jnp.where(rot, cos, 1.0), jnp.where(first, -sin, 0.0), jnp.where(second, sin, 0.0)


def _rope_apply(vv, cosf, s1, s2):
    half = A_ROPE // 2
    w = vv.shape[-1]
    return vv * cosf + pltpu.roll(vv, w - half, 1) * s1 + pltpu.roll(vv, half, 1) * s2


def _rope_apply_t(dd, cosf, s1, s2):
    half = A_ROPE // 2
    w = dd.shape[-1]
    return dd * cosf + pltpu.roll(dd * s1, half, 1) + pltpu.roll(dd * s2, w - half, 1)


A_BIAS_LANE_K = A_NOPE + A_ROPE
A_BIAS_LANE_V = A_V


def _hi_lo(val):
    hi = val.astype(BF16)
    return hi, (val - hi.astype(F32)).astype(BF16)


def _qkv_up_rope(qa_n, kv_n, w_qb, w_k, w_v, gates, pos, name, tm=512):
    t = qa_n.shape[0]
    w = A_HEADS * LANES

    def body(qa_ref, kv_ref, wq_ref, wk_ref, wv_ref, g_ref, p_ref, qo_ref, ko_ref, vo_ref):
        cosf, s1, s2 = _rope_tables(p_ref[...], LANES)
        lane = lax.broadcasted_iota(jnp.int32, (1, LANES), 1)
        kr = jnp.where((lane >= A_NOPE) & (lane < A_NOPE + A_ROPE), g_ref[...], 0.0)
        kr = _rope_apply(kr, cosf, s1, s2)
        kr = jnp.where((lane == A_BIAS_LANE_K) | (lane == A_BIAS_LANE_K + 1), 1.0, kr)
        v_one = (lane == A_BIAS_LANE_V) | (lane == A_BIAS_LANE_V + 1)
        qpre = lax.dot_general(qa_ref[...], wq_ref[...], _DIMS["nt"], preferred_element_type=F32)
        kv = kv_ref[...]
        kpre = lax.dot_general(kv, wk_ref[...], _DIMS["nt"], preferred_element_type=F32)
        vpre = lax.dot_general(kv, wv_ref[...], _DIMS["nt"], preferred_element_type=F32)
        for hd in range(A_HEADS):
            sl = slice(hd * LANES, (hd + 1) * LANES)
            qo_ref[:, sl] = (_rope_apply(qpre[:, sl], cosf, s1, s2) * ATTN_SCALE).astype(qo_ref.dtype)
            ko_ref[:, sl] = (kpre[:, sl] + kr).astype(ko_ref.dtype)
            vo_ref[:, sl] = jnp.where(v_one, 1.0, vpre[:, sl]).astype(vo_ref.dtype)

    rows = lambda width: pl.BlockSpec((tm, width), lambda i: (i, 0))
    whole = lambda arr: pl.BlockSpec(arr.shape, lambda i: (0, 0))
    out = jax.ShapeDtypeStruct((t, w), BF16)
    return pl.pallas_call(
        body, name=name, grid=(t // tm,),
        in_specs=[rows(qa_n.shape[1]), rows(kv_n.shape[1]), whole(w_qb), whole(w_k), whole(w_v), rows(LANES), rows(1)],
        out_specs=[rows(w)] * 3, out_shape=[out] * 3, compiler_params=_cparams("parallel"),
    )(qa_n, kv_n, w_qb, w_k, w_v, gates, pos)


def _rope_bwd(dq, dk, dgates, pos, name, tm=512):
    t, w = dq.shape

    def body(dq_ref, dk_ref, dg_ref, p_ref, dqo_ref, dgo_ref):
        cosf, s1, s2 = _rope_tables(p_ref[...], LANES)
        acc = jnp.zeros((tm, LANES), F32)
        for hd in range(A_HEADS):
            sl = slice(hd * LANES, (hd + 1) * LANES)
            dqo_ref[:, sl] = (_rope_apply_t(dq_ref[:, sl], cosf, s1, s2) * ATTN_SCALE).astype(dqo_ref.dtype)
            acc = acc + dk_ref[:, sl].astype(F32)
        lane = lax.broadcasted_iota(jnp.int32, (1, LANES), 1)
        dkr = _rope_apply_t(acc, cosf, s1, s2)
        dkr = jnp.where((lane >= A_NOPE) & (lane < A_NOPE + A_ROPE), dkr, 0.0)
        dgo_ref[...] = (dg_ref[...] + dkr).astype(dgo_ref.dtype)

    row = pl.BlockSpec((tm, w), lambda i: (i, 0))
    nar = pl.BlockSpec((tm, LANES), lambda i: (i, 0))
    return pl.pallas_call(
        body, name=name, grid=(t // tm,),
        in_specs=[row, row, nar, pl.BlockSpec((tm, 1), lambda i: (i, 0))],
        out_specs=[row, nar],
        out_shape=[jax.ShapeDtypeStruct((t, w), BF16), jax.ShapeDtypeStruct((t, LANES), BF16)],
        compiler_params=_cparams("parallel"),
    )(dq, dk, dgates, pos)


A_TQ = 512
A_TK = 512


A_HEADS_PER_STEP = 2
A_HEADS_PER_STEP_FWD = 4


def _flash_fwd(q, k, v, name):
    t = q.shape[0]
    tq, tk = A_TQ, A_TK
    nh = A_HEADS_PER_STEP_FWD
    wblk = nh * LANES

    def body(q_ref, k_ref, v_ref, o_ref, qb_ref):
        i = pl.program_id(1)
        lane = lax.broadcasted_iota(jnp.int32, (tq, LANES), 1)
        qpos = i * tq + lax.broadcasted_iota(jnp.int32, (tq, tk), 0)
        kofs = lax.broadcasted_iota(jnp.int32, (tq, tk), 1)
        qs = [q_ref[:, hh * LANES:(hh + 1) * LANES] for hh in range(nh)]

        def step(kb, carry, masked):
            rows = pl.ds(pl.multiple_of(kb * tk, tk), tk)
            new = []
            for hh in range(nh):
                m, acc = carry[hh]
                kh = k_ref[rows, hh * LANES:(hh + 1) * LANES]
                s = lax.dot_general(qs[hh], kh, _DIMS["nt"], preferred_element_type=F32)
                if masked:
                    s = jnp.where(kb * tk + kofs <= qpos, s, NEG_INF)
                m_new = jnp.maximum(m, jnp.max(s, axis=1, keepdims=True))
                p = jnp.exp(s - m_new)
                acc = jnp.exp(m - m_new) * acc + jnp.dot(p.astype(BF16), v_ref[rows, hh * LANES:(hh + 1) * LANES],
                                                          preferred_element_type=F32)
                new.append((m_new, acc))
            return tuple(new)

        carry = tuple((jnp.full((tq, 1), NEG_INF, F32), jnp.zeros((tq, LANES), F32)) for _ in range(nh))
        n_full = (i * tq) // tk
        carry = lax.fori_loop(0, n_full, functools.partial(step, masked=False), carry)
        for d in range(tq // tk):
            carry = step(n_full + d, carry, True)
        for hh in range(nh):
            m, acc = carry[hh]
            l = acc[:, A_BIAS_LANE_V:A_BIAS_LANE_V + 1]
            o_ref[:, hh * LANES:(hh + 1) * LANES] = (acc / l).astype(o_ref.dtype)
            hi, lo = _hi_lo(-(m + jnp.log(l)))
            qb_ref[:, hh * LANES:(hh + 1) * LANES] = jnp.where(
                lane == A_BIAS_LANE_K, hi, jnp.where(lane == A_BIAS_LANE_K + 1, lo, qs[hh]))

    blk = pl.BlockSpec((tq, wblk), lambda j, i: (i, j))
    res = pl.BlockSpec((t, wblk), lambda j, i: (0, j))
    out = jax.ShapeDtypeStruct((t, A_HEADS * LANES), BF16)
    return pl.pallas_call(
        body, name=name, grid=(A_HEADS // nh, t // tq), in_specs=[blk, res, res], out_specs=[blk, blk],
        out_shape=[out, out], compiler_params=_cparams("parallel", "parallel"),
    )(q, k, v)


def _d_attn_out(dh, w_out_a, o, name, tm=512):
    t, w = o.shape
    tm = min(tm, t)

    def body(dh_ref, w_ref, o_ref, out_ref):
        do = lax.dot_general(dh_ref[...], w_ref[...], _DIMS["nt"], preferred_element_type=F32).astype(BF16)
        lane = lax.broadcasted_iota(jnp.int32, (tm, LANES), 1)
        for hd in range(A_HEADS):
            sl = slice(hd * LANES, (hd + 1) * LANES)
            d = do[:, sl]
            delta = jnp.sum(jnp.where(lane < A_V, d.astype(F32) * o_ref[:, sl].astype(F32), 0.0), axis=1, keepdims=True)
            hi, lo = _hi_lo(-delta)
            out_ref[:, sl] = jnp.where(lane == A_BIAS_LANE_V, hi, jnp.where(lane == A_BIAS_LANE_V + 1, lo, d))

    return pl.pallas_call(
        body, name=name, grid=(t // tm,), in_specs=[_rows(tm, dh.shape[1]), _whole(w_out_a), _rows(tm, w)],
        out_specs=_rows(tm, w), out_shape=jax.ShapeDtypeStruct((t, w), BF16), compiler_params=_cparams("parallel"),
    )(dh, w_out_a, o)


def _flash_bwd(qb, k, v, doe, name):
    t = qb.shape[0]
    tq, tk = A_TQ, A_TK
    assert tq == tk
    nh = A_HEADS_PER_STEP
    wblk = nh * LANES
    nq = t // tq

    def body(q_ref, k_ref, v_ref, do_ref, dq_ref, dk_ref, dv_ref):
        kb = pl.program_id(1)

        @pl.when(kb == 0)
        def _():
            dq_ref[...] = jnp.zeros_like(dq_ref)

        kpos = kb * tk + lax.broadcasted_iota(jnp.int32, (tk, tq), 0)
        qofs = lax.broadcasted_iota(jnp.int32, (tk, tq), 1)
        ks = [k_ref[:, hh * LANES:(hh + 1) * LANES] for hh in range(nh)]
        vs = [v_ref[:, hh * LANES:(hh + 1) * LANES] for hh in range(nh)]

        def step(qi, carry, masked):
            rows = pl.ds(pl.multiple_of(qi * tq, tq), tq)
            new = []
            for hh in range(nh):
                dk_acc, dv_acc = carry[hh]
                qh = q_ref[rows, hh * LANES:(hh + 1) * LANES]
                doh = do_ref[rows, hh * LANES:(hh + 1) * LANES]
                p_t = jnp.exp(lax.dot_general(ks[hh], qh, _DIMS["nt"], preferred_element_type=F32))
                if masked:
                    p_t = jnp.where(qi * tq + qofs >= kpos, p_t, 0.0)
                ds_t = (p_t * lax.dot_general(vs[hh], doh, _DIMS["nt"], preferred_element_type=F32)).astype(BF16)
                dv_acc = dv_acc + jnp.dot(p_t.astype(BF16), doh, preferred_element_type=F32)
                dk_acc = dk_acc + jnp.dot(ds_t, qh, preferred_element_type=F32)
                dq_ref[rows, hh * LANES:(hh + 1) * LANES] += lax.dot_general(ds_t, ks[hh], _DIMS["tn"],
                                                                             preferred_element_type=F32)
                new.append((dk_acc, dv_acc))
            return tuple(new)

        carry = tuple((jnp.zeros((tk, LANES), F32), jnp.zeros((tk, LANES), F32)) for _ in range(nh))
        carry = step(kb, carry, True)
        carry = lax.fori_loop(kb + 1, nq, functools.partial(step, masked=False), carry)
        for hh in range(nh):
            dk_ref[:, hh * LANES:(hh + 1) * LANES] = carry[hh][0].astype(dk_ref.dtype)
            dv_ref[:, hh * LANES:(hh + 1) * LANES] = carry[hh][1].astype(dv_ref.dtype)

    res = pl.BlockSpec((t, wblk), lambda j, kb: (0, j))
    blk = pl.BlockSpec((tk, wblk), lambda j, kb: (kb, j))
    wide = (t, A_HEADS * LANES)
    return pl.pallas_call(
        body, name=name, grid=(A_HEADS // nh, t // tk), in_specs=[res, blk, blk, res], out_specs=[res, blk, blk],
        out_shape=[jax.ShapeDtypeStruct(wide, F32), jax.ShapeDtypeStruct(wide, BF16), jax.ShapeDtypeStruct(wide, BF16)],
        compiler_params=_cparams("parallel", "arbitrary"),
    )(qb, k, v, doe)


_SPLITS = (M_WIDTH, M_WIDTH, M_WIDTH, M_WIDTH, M_HEADS, M_HEADS, Q_RANK, KV_RANK, A_ROPE)
_OFFS = tuple(sum(_SPLITS[:i]) for i in range(len(_SPLITS) + 1))
_GATE_BLOCK_KR = A_NOPE


def _split_w_in_t(wt):
    z = lambda n: jnp.zeros((n, wt.shape[1]), wt.dtype)
    w_g = jnp.concatenate([wt[_OFFS[4]:_OFFS[6]], z(_GATE_BLOCK_KR - 2 * M_HEADS), wt[_OFFS[8]:_OFFS[9]],
                           z(LANES - _GATE_BLOCK_KR - A_ROPE)], axis=0)
    return wt[_OFFS[6]:_OFFS[8]], w_g


def _merge_w_in_grad_t(g_m4, g_a2, g_g):
    return jnp.concatenate([g_m4, g_g[:2 * M_HEADS], g_a2, g_g[_GATE_BLOCK_KR:_GATE_BLOCK_KR + A_ROPE]], axis=0)


def _pad_heads(w, used):
    w3 = w.reshape(A_HEADS, used, w.shape[1])
    return jnp.pad(w3, ((0, 0), (0, LANES - used), (0, 0))).reshape(A_HEADS * LANES, w.shape[1])


def _unpad_heads(g, used):
    return g.reshape(A_HEADS, LANES, g.shape[1])[:, :used].reshape(A_HEADS * used, g.shape[1])


def _split_w_kv_b_t(wt):
    w3 = wt.reshape(A_HEADS, A_NOPE + A_V, wt.shape[1])
    pad = lambda part: jnp.pad(part, ((0, 0), (0, LANES - part.shape[1]), (0, 0))).reshape(A_HEADS * LANES, wt.shape[1])
    return pad(w3[:, :A_NOPE]), pad(w3[:, A_NOPE:])


def _merge_w_kv_b_grad_t(gk, gv):
    gk3 = gk.reshape(A_HEADS, LANES, gk.shape[1])[:, :A_NOPE]
    gv3 = gv.reshape(A_HEADS, LANES, gv.shape[1])[:, :A_V]
    return jnp.concatenate([gk3, gv3], axis=1).reshape(A_HEADS * (A_NOPE + A_V), gk.shape[1])


def _local_step(x, pos, target, nw, w, start_late_reduce):
    w_in = w["w_in"]
    w_a2, w_g = _split_w_in_t(w_in)
    w_qb = _pad_heads(w["w_q_b"], A_NOPE + A_ROPE)
    w_k, w_v = _split_w_kv_b_t(w["w_kv_b"])
    w_out_m, w_out_a = w["w_out"][:M_WIDTH], _pad_heads(w["w_out"][M_WIDTH:], A_V)
    w_gate, w_up, w_down = w["w_gate"], w["w_up"], w["w_down"]
    bias = jnp.pad(nw["b_gates"], ((0, 0), (0, LANES - 2 * M_HEADS)))
    mnorm = nw["mlstm_norm_w"].reshape(M_HEADS, 1, M_HEAD_DIM)
    n_m4 = 4 * M_WIDTH

    u1 = _rmsnorm_fwd(x, nw["attn_norm_w"], "attn_norm")
    m4 = _matmul(u1, w_in, "nt", BF16, "proj_mlstm", b_rows=n_m4, lead_scale=(M_WIDTH, M_SCALE))
    a2, qa_n, kv_n = _proj_latents_norm(u1, w_a2, nw["q_a_norm_w"], nw["kv_a_norm_w"], "proj_latents")
    gates = _matmul(u1, w_g, "nt", F32, "proj_gates")
    qk_t = _chunk_major_t(_matmul(w_in, u1, "nt", BF16, "proj_qk_t", a_rows=2 * M_WIDTH,
                                  lead_row_scale=(M_WIDTH, M_SCALE)))
    hm, c_prev_all, nm_all = _mlstm_fwd(m4, qk_t, gates, bias, mnorm, "mlstm_fwd")
    qa, kva = a2[:, :Q_RANK], a2[:, Q_RANK:]
    qr, kr, vv = _qkv_up_rope(qa_n, kv_n, w_qb, w_k, w_v, gates, pos, "qkv_up_rope")
    ha, qb = _flash_fwd(qr, kr, vv, "attn_fwd")
    h1, u2 = _out_proj_norm(hm, ha, w_out_m, w_out_a, x, nw["ffn_norm_w"], "out_proj")
    gg, uu, act = _ffn_gate_up(u2, w_gate, w_up, "ffn_gate_up")
    h2 = _matmul(act, w_down, "nn", F32, "ffn_down", residual=h1)

    dh2, dh2_b, loss, g_final = _final_loss(h2, target, nw["final_norm_w"].reshape(1, D_MODEL), "final_loss")
    grads = {"final_norm_w": g_final.reshape(D_MODEL)}
    grads_b = {}
    dgg, duu = _ffn_down_bwd(dh2_b, w_down, gg, uu, "ffn_down_bwd")
    grads["w_down"], grads_b["w_down"] = _matmul(act, dh2_b, "tn", F32, "g_w_down", tm=256, tn=D_MODEL, also_bf16=True)
    du2 = _matmul([dgg, duu], [w_gate, w_up], "nn", F32, "d_u2", tm=1024, tn=512)
    grads["w_gate"], grads_b["w_gate"] = _matmul(dgg, u2, "tn", F32, "g_w_gate", tm=256, tn=D_MODEL, also_bf16=True)
    grads["w_up"], grads_b["w_up"] = _matmul(duu, u2, "tn", F32, "g_w_up", tm=256, tn=D_MODEL, also_bf16=True)
    dh1, dh1_b, grads["ffn_norm_w"] = _rmsnorm_bwd(du2, h1, nw["ffn_norm_w"], "ffn_norm_bwd", F32, residual=dh2,
                                                   also_bf16=True)
    dhm = _matmul(dh1_b, w_out_m, "nt", BF16, "d_hm")
    doe = _d_attn_out(dh1_b, w_out_a, ha, "d_ha")
    grads["w_out"] = jnp.concatenate([_matmul(hm, dh1_b, "tn", F32, "g_w_out_m", tn=D_MODEL),
                                      _unpad_heads(_matmul(ha, dh1_b, "tn", F32, "g_w_out_a", tn=D_MODEL), A_V)], axis=0)

    late, doe = start_late_reduce(grads, grads_b, doe)

    dqr, dkr, dvv = _flash_bwd(qb, kr, vv, doe, "attn_bwd")
    dm4, dgates, small = _mlstm_bwd(m4, qk_t, gates, bias, mnorm, c_prev_all, nm_all, dhm, "mlstm_bwd")
    grads["mlstm_norm_w"] = small[:M_HEADS].reshape(1, M_HEADS, M_HEAD_DIM)
    grads["b_gates"] = small[M_HEADS:M_HEADS + 1, :2 * M_HEADS]
    dqpre, dgk = _rope_bwd(dqr, dkr, dgates, pos, "rope_bwd")
    dqa_n = _matmul(dqpre, w_qb, "nn", BF16, "d_qa_n")
    grads["w_q_b"] = _unpad_heads(_matmul(dqpre, qa_n, "tn", F32, "g_w_q_b"), A_NOPE + A_ROPE)
    dkv_n = _matmul([dkr, dvv], [w_k, w_v], "nn", BF16, "d_kv_n")
    grads["w_kv_b"] = _merge_w_kv_b_grad_t(_matmul(dkr, kv_n, "tn", F32, "g_w_k"),
                                           _matmul(dvv, kv_n, "tn", F32, "g_w_v"))
    dqa, grads["q_a_norm_w"] = _rmsnorm_bwd(dqa_n, qa, nw["q_a_norm_w"], "q_a_norm_bwd", BF16)
    dkva, grads["kv_a_norm_w"] = _rmsnorm_bwd(dkv_n, kva, nw["kv_a_norm_w"], "kv_a_norm_bwd", BF16)
    da2 = jnp.concatenate([dqa, dkva], axis=1)
    grads["w_in"] = _merge_w_in_grad_t(_matmul(dm4, u1, "tn", F32, "g_w_m4", tn=D_MODEL),
                                       _matmul(da2, u1, "tn", F32, "g_w_a2", tm=640, tn=D_MODEL),
                                       _matmul(dgk, u1, "tn", F32, "g_w_g", tn=D_MODEL))
    grad_x, grads["attn_norm_w"] = _d_u1_norm_bwd([dm4, da2, dgk], [w_in, w_a2, w_g], [n_m4, None, None], x,
                                                  nw["attn_norm_w"], dh1, "d_u1")
    return loss, grad_x, grads, grads_b, late


MESH = pl.DeviceIdType.MESH
N_CHIPS = 4
EARLY = ("w_in", "w_q_b", "w_kv_b")
LATE = ("w_out", "w_gate", "w_up", "w_down")
BIG = EARLY + LATE
TRANSPOSED = ("w_in", "w_q_b", "w_kv_b", "w_gate", "w_up")
LANE_HALVED = ("w_in", "w_out", "w_gate", "w_up", "w_down")
SMALL = ("attn_norm_w", "b_gates", "mlstm_norm_w", "q_a_norm_w", "kv_a_norm_w", "ffn_norm_w", "final_norm_w")
HBM_SPEC = pl.BlockSpec(memory_space=pltpu.HBM)


def _stored(name, a):
    return a[0].T if name in TRANSPOSED else a[0]


def _unstored(name, a):
    return (a.T if name in TRANSPOSED else a)[None]


def _half_shape(name, shape):
    rs, cs = shape
    return (rs, cs // 2) if name in LANE_HALVED else (rs // 2, cs)


def _half(ref, name, h, *lead):
    rs, cs = ref.shape[-2:]
    if name in LANE_HALVED:
        return ref.at[(*lead, slice(None), pl.ds(h * (cs // 2), cs // 2))]
    return ref.at[(*lead, pl.ds(h * (rs // 2), rs // 2), slice(None))]


def _place():
    x, y, c = lax.axis_index("x"), lax.axis_index("y"), lax.axis_index("c")
    others = [(1 - x, y), (x, 1 - y), (1 - x, 1 - y)]
    return x, y, c, others


def _gather_weights(names, shards):
    n = len(shards)

    def body(*refs):
        _gather_body(names, refs[:n], refs[n:2 * n], *refs[2 * n:])

    gathered = pl.pallas_call(
        body, name="gather_weights", in_specs=[HBM_SPEC] * n, out_specs=[HBM_SPEC] * n,
        out_shape=[jax.ShapeDtypeStruct((N_CHIPS,) + s.shape, s.dtype) for s in shards],
        scratch_shapes=[pltpu.SemaphoreType.DMA((6 * n,)), pltpu.SemaphoreType.DMA((6 * n,))],
    )(*shards)
    return _with_own_slab(gathered, shards)


def _with_own_slab(gathered, shards):
    me = 2 * lax.axis_index("x") + lax.axis_index("y")
    return [lax.dynamic_update_slice(g, s[None], (me, 0, 0)) for g, s in zip(gathered, shards)]


def _gather_body(names, ins, outs, send_sems, recv_sems):
    x, y, c, others = _place()
    me = 2 * x + y
    sibling = (x, y, 1 - c)

    def copy(w, k, slab, core, to, src=None):
        dst = _half(outs[w], names[w], core, slab)
        return pltpu.make_async_remote_copy(
            src_ref=dst if src is None else src, dst_ref=dst, send_sem=send_sems.at[w * 6 + k],
            recv_sem=recv_sems.at[w * 6 + k], device_id=to, device_id_type=MESH)

    sends = []
    for w in range(len(names)):
        for j, chip in enumerate(others):
            cp = copy(w, j, me, c, (*chip, c), src=_half(ins[w], names[w], c))
            cp.start()
            sends.append(cp)
    for w in range(len(names)):
        for j, (ox, oy) in enumerate(others):
            slab = 2 * ox + oy
            copy(w, j, slab, c, (x, y, c)).wait_recv()
            fwd = copy(w, 3 + j, slab, c, sibling)
            fwd.start()
            sends.append(fwd)
    for w in range(len(names)):
        for j, (ox, oy) in enumerate(others):
            copy(w, 3 + j, 2 * ox + oy, 1 - c, (x, y, c)).wait_recv()
    for cp in sends:
        cp.wait_send()


GATHER_LATE_COLLECTIVE_ID = 1


def _gather_weights_async(names, shards):
    n = len(shards)
    src = [jax.new_ref(s, memory_space=pltpu.MemorySpace.HBM) for s in shards]
    dst = [jax.empty_ref(jax.ShapeDtypeStruct((N_CHIPS,) + s.shape, s.dtype), memory_space=pltpu.MemorySpace.HBM)
           for s in shards]

    @pl.kernel(mesh=plsc.ScalarSubcoreMesh(axis_name="sequencer", num_cores=1), name="gather_weights_async",
               scratch_types=(pltpu.SemaphoreType.DMA((6 * n,)), pltpu.SemaphoreType.DMA((6 * n,))),
               compiler_params=pltpu.CompilerParams(collective_id=GATHER_LATE_COLLECTIVE_ID))
    def launch(send_sems, recv_sems):
        x, y, c, others = _place()
        peers = [(ox, oy, c) for ox, oy in others] + [(x, y, 1 - c)]
        barrier = pltpu.get_barrier_semaphore()
        for peer in peers:
            pl.semaphore_signal(barrier, inc=1, device_id=peer, device_id_type=MESH)
        pl.semaphore_wait(barrier, len(peers))
        _gather_body(names, src, dst, send_sems, recv_sems)

    launch()
    return _with_own_slab([d[...] for d in dst], shards)


def _exchange(arrays, out_shapes, plan, copies_per_array, name):
    n = len(arrays)

    def body(*refs):
        ins, outs = refs[:n], refs[n:2 * n]
        send_sems, recv_sems = refs[2 * n:]
        rem = [pltpu.make_async_remote_copy(src_ref=s, dst_ref=d, send_sem=send_sems.at[k], recv_sem=recv_sems.at[k],
                                            device_id=to, device_id_type=MESH)
               for k, (s, d, to) in enumerate(plan(ins, outs, _place()))]
        for cp in rem:
            cp.start()
        for cp in rem:
            cp.wait_recv()
        for cp in rem:
            cp.wait_send()

    return pl.pallas_call(
        body, name=name, in_specs=[HBM_SPEC] * n, out_specs=[HBM_SPEC] * n, out_shape=out_shapes,
        scratch_shapes=[pltpu.SemaphoreType.DMA((copies_per_array * n,)),
                        pltpu.SemaphoreType.DMA((copies_per_array * n,))],
    )(*arrays)


def _plan_to_sibling(ins, outs, place):
    x, y, c, _ = place
    return [(ins[w], outs[w], (x, y, 1 - c)) for w in range(len(ins))]


N_PEERS = 7


def _shard_sum_direct(g3, got7, name, where):
    hs = tuple(got7.shape[1:])
    if name in LANE_HALVED:
        g_in, g_spec = g3, pl.BlockSpec((1,) + hs, lambda i, p: (p[0], 0, p[1]))
    else:
        g_in, g_spec = g3.reshape((N_CHIPS, 2) + hs), pl.BlockSpec((1, 1) + hs, lambda i, p: (p[0], p[1], 0, 0))

    def body(p_ref, g_ref, r_ref, o_ref):
        acc = g_ref[(0,) * (len(g_ref.shape) - 2)]
        for k in range(N_PEERS):
            acc = acc + r_ref[k].astype(F32)
        o_ref[...] = acc

    return pl.pallas_call(
        body, name="shard_sum_%s" % name,
        grid_spec=pltpu.PrefetchScalarGridSpec(
            num_scalar_prefetch=1, grid=(1,),
            in_specs=[g_spec, pl.BlockSpec((N_PEERS,) + hs, lambda i, p: (0, 0, 0))],
            out_specs=pl.BlockSpec(hs, lambda i, p: (0, 0))),
        out_shape=jax.ShapeDtypeStruct(hs, F32), compiler_params=_cparams("arbitrary"),
    )(where, g_in, got7)


def _adamw_halves(w, m, v, mine, other, name, core):
    hs = tuple(mine.shape)
    full = pl.BlockSpec(hs, (lambda h, c_ref: (0, h)) if name in LANE_HALVED else (lambda h, c_ref: (h, 0)))
    half = pl.BlockSpec(hs, lambda h, c_ref: (0, 0))

    def body(c_ref, w_ref, m_ref, v_ref, a_ref, b_ref, g_ref, d_ref, mo_ref, vo_ref):
        g = jnp.where(pl.program_id(0) == c_ref[0], a_ref[...], b_ref[...])
        delta, m_new, v_new = _adamw_math(w_ref[...], g, m_ref[...], v_ref[...])
        g_ref[...] = g
        d_ref[...] = delta
        mo_ref[...] = m_new
        vo_ref[...] = v_new

    out = jax.ShapeDtypeStruct(w.shape, F32)
    return pl.pallas_call(
        body, name="adamw_%s" % name,
        grid_spec=pltpu.PrefetchScalarGridSpec(num_scalar_prefetch=1, grid=(2,), in_specs=[full, full, full, half, half],
                                               out_specs=[full] * 4),
        out_shape=[out] * 4, compiler_params=_cparams("parallel"),
    )(core, w, m, v, mine, other)


def _adamw_math(w, g, m, v):
    m = ADAM_B1 * m + (1.0 - ADAM_B1) * g
    v = ADAM_B2 * v + (1.0 - ADAM_B2) * (g * g)
    m_hat = m / (1.0 - ADAM_B1 ** ADAM_STEP)
    v_hat = v / (1.0 - ADAM_B2 ** ADAM_STEP)
    delta = -ADAM_LR * (m_hat / (jnp.sqrt(v_hat) + ADAM_EPS) + ADAM_WD * w)
    return delta, m, v


def _core_index():
    return lax.axis_index("c").astype(jnp.int32).reshape(1)


DIRECT_REDUCE_COLLECTIVE_ID = {"late": 2, "early": 3}


def _exchange_contributions_async(names, gb3, tag):
    n = len(gb3)
    src = [jax.new_ref(g, memory_space=pltpu.MemorySpace.HBM) for g in gb3]
    dst = [jax.empty_ref(jax.ShapeDtypeStruct((N_PEERS,) + _half_shape(nm, g.shape[1:]), g.dtype),
                         memory_space=pltpu.MemorySpace.HBM) for nm, g in zip(names, gb3)]

    @pl.kernel(mesh=plsc.ScalarSubcoreMesh(axis_name="sequencer", num_cores=1), name="contributions_async_" + tag,
               scratch_types=(pltpu.SemaphoreType.DMA((N_PEERS * n,)), pltpu.SemaphoreType.DMA((N_PEERS * n,))),
               compiler_params=pltpu.CompilerParams(collective_id=DIRECT_REDUCE_COLLECTIVE_ID[tag]))
    def launch(send_sems, recv_sems):
        x, y, c, _ = _place()
        peers = [(x ^ ((k >> 2) & 1), y ^ ((k >> 1) & 1), c ^ (k & 1)) for k in range(1, N_PEERS + 1)]
        barrier = pltpu.get_barrier_semaphore()
        for peer in peers:
            pl.semaphore_signal(barrier, inc=1, device_id=peer, device_id_type=MESH)
        pl.semaphore_wait(barrier, N_PEERS)
        rem = []
        for w in range(n):
            for k, (px, py, pc) in enumerate(peers):
                rem.append(pltpu.make_async_remote_copy(
                    src_ref=_half(src[w], names[w], pc, 2 * px + py), dst_ref=dst[w].at[k],
                    send_sem=send_sems.at[N_PEERS * w + k], recv_sem=recv_sems.at[N_PEERS * w + k],
                    device_id=(px, py, pc), device_id_type=MESH))
        for cp in rem:
            cp.start()
        for cp in rem:
            cp.wait_recv()
        for cp in rem:
            cp.wait_send()

    launch()
    return [d[...] for d in dst]


def _shard_index():
    return (2 * lax.axis_index("x") + lax.axis_index("y")).astype(jnp.int32).reshape(1)


def _with_sibling_half(names, mine, tag):
    other = _exchange(mine, [jax.ShapeDtypeStruct(m.shape, F32) for m in mine], _plan_to_sibling, 1,
                      "sibling_result_" + tag)
    return {n: (a, b) for n, a, b in zip(names, mine, other)}


SMALL_ROWS = 8
SMALL_LAYOUT = {"attn_norm_w": (0, 0, 1024), "ffn_norm_w": (1, 0, 1024), "final_norm_w": (2, 0, 1024),
                "q_a_norm_w": (3, 0, 384), "kv_a_norm_w": (3, 384, 256), "mlstm_norm_w": (4, 0, 512),
                "b_gates": (4, 512, 8)}
LOSS_SLOT = (5, 0)


def _small_allreduce_adamw(grads, loss, wts, mom, vel):
    shapes = {n: wts[n].shape for n in SMALL}
    flat = lambda d: [d[n].reshape(1, SMALL_LAYOUT[n][2]) for n in SMALL]
    ns = len(SMALL)

    def body(*refs):
        g_in, loss_ref = refs[:ns], refs[ns]
        w_in_, m_in, v_in = refs[ns + 1:2 * ns + 1], refs[2 * ns + 1:3 * ns + 1], refs[3 * ns + 1:4 * ns + 1]
        outs = refs[4 * ns + 1:8 * ns + 2]
        tile, slots, send_sems, recv_sems = refs[8 * ns + 2:]
        x, y, c, _ = _place()
        me = 4 * x + 2 * y + c
        tile[...] = jnp.zeros_like(tile)
        for n, g_ref in zip(SMALL, g_in):
            r, c0, width = SMALL_LAYOUT[n]
            tile[r:r + 1, c0:c0 + width] = g_ref[...]
        tile[LOSS_SLOT[0]:LOSS_SLOT[0] + 1, 0:LANES] = loss_ref[...]
        slots[me] = tile[...]
        copies = []
        for k in range(1, N_PEERS + 1):
            to = (x ^ ((k >> 2) & 1), y ^ ((k >> 1) & 1), c ^ (k & 1))
            cp = pltpu.make_async_remote_copy(src_ref=tile, dst_ref=slots.at[me], send_sem=send_sems.at[k - 1],
                                              recv_sem=recv_sems.at[k - 1], device_id=to, device_id_type=MESH)
            cp.start()
            copies.append(cp)
        for k in range(1, N_PEERS + 1):
            pltpu.make_async_remote_copy(src_ref=tile, dst_ref=slots.at[me ^ k], send_sem=send_sems.at[k - 1],
                                         recv_sem=recv_sems.at[k - 1], device_id=(x, y, c),
                                         device_id_type=MESH).wait_recv()
        for cp in copies:
            cp.wait_send()
        total = slots[0]
        for d in range(1, N_PEERS + 1):
            total = total + slots[d]
        for i, n in enumerate(SMALL):
            r, c0, width = SMALL_LAYOUT[n]
            g = total[r:r + 1, c0:c0 + width]
            delta, m_new, v_new = _adamw_math(w_in_[i][...], g, m_in[i][...], v_in[i][...])
            outs[i][...] = g
            outs[ns + i][...] = delta
            outs[2 * ns + i][...] = m_new
            outs[3 * ns + i][...] = v_new
        outs[4 * ns][...] = total[LOSS_SLOT[0]:LOSS_SLOT[0] + 1, 0:LANES]

    vm = pl.BlockSpec(memory_space=pltpu.VMEM)
    vec = [jax.ShapeDtypeStruct((1, SMALL_LAYOUT[n][2]), F32) for n in SMALL]
    res = pl.pallas_call(
        body, name="small_allreduce_adamw", in_specs=[vm] * (4 * ns + 1), out_specs=[vm] * (4 * ns + 1),
        out_shape=vec * 4 + [jax.ShapeDtypeStruct((1, LANES), F32)],
        scratch_shapes=[pltpu.VMEM((SMALL_ROWS, D_MODEL), F32), pltpu.VMEM((N_PEERS + 1, SMALL_ROWS, D_MODEL), F32),
                        pltpu.SemaphoreType.DMA((N_PEERS,)), pltpu.SemaphoreType.DMA((N_PEERS,))],
    )(*flat(grads), loss, *flat(wts), *flat(mom), *flat(vel))
    groups = [{n: r.reshape(shapes[n]) for n, r in zip(SMALL, res[i * ns:(i + 1) * ns])} for i in range(4)]
    return (*groups, res[4 * ns][0, 0])


def kernel(x, positions, attn_norm_w, w_in, b_gates, mlstm_norm_w, q_a_norm_w, w_q_b, kv_a_norm_w, w_kv_b, w_out, ffn_norm_w, w_gate, w_up, w_down, final_norm_w, loss_target, m_attn_norm_w, m_w_in, m_b_gates, m_mlstm_norm_w, m_q_a_norm_w, m_w_q_b, m_kv_a_norm_w, m_w_kv_b, m_w_out, m_ffn_norm_w, m_w_gate, m_w_up, m_w_down, m_final_norm_w, v_attn_norm_w, v_w_in, v_b_gates, v_mlstm_norm_w, v_q_a_norm_w, v_w_q_b, v_kv_a_norm_w, v_w_kv_b, v_w_out, v_ffn_norm_w, v_w_gate, v_w_up, v_w_down, v_final_norm_w):
    names = ("attn_norm_w", "w_in", "b_gates", "mlstm_norm_w", "q_a_norm_w", "w_q_b", "kv_a_norm_w", "w_kv_b", "w_out",
             "ffn_norm_w", "w_gate", "w_up", "w_down", "final_norm_w")
    wts = dict(zip(names, (attn_norm_w, w_in, b_gates, mlstm_norm_w, q_a_norm_w, w_q_b, kv_a_norm_w, w_kv_b, w_out,
                           ffn_norm_w, w_gate, w_up, w_down, final_norm_w)))
    mom = dict(zip(names, (m_attn_norm_w, m_w_in, m_b_gates, m_mlstm_norm_w, m_q_a_norm_w, m_w_q_b, m_kv_a_norm_w,
                           m_w_kv_b, m_w_out, m_ffn_norm_w, m_w_gate, m_w_up, m_w_down, m_final_norm_w)))
    vel = dict(zip(names, (v_attn_norm_w, v_w_in, v_b_gates, v_mlstm_norm_w, v_q_a_norm_w, v_w_q_b, v_kv_a_norm_w,
                           v_w_kv_b, v_w_out, v_ffn_norm_w, v_w_gate, v_w_up, v_w_down, v_final_norm_w)))
    t = x.shape[1]

    shards = {n: _stored(n, wts[n]).astype(BF16) for n in BIG}
    first, later = ("w_in",), tuple(n for n in BIG if n != "w_in")
    now = _gather_weights(first, [shards[n] for n in first])
    now, later_in = lax.optimization_barrier((now, [shards[n] for n in later]))
    behind = _gather_weights_async(later, later_in)
    full = {n: g.reshape(N_CHIPS * g.shape[1], g.shape[2]) for n, g in zip(first + later, list(now) + list(behind))}

    nw = {n: wts[n] for n in SMALL}
    by_shard = lambda g: g.reshape(N_CHIPS, g.shape[0] // N_CHIPS, g.shape[1])

    def start_reduce(group, tag, grads, grads_b, marker):
        g3 = [by_shard(grads[n]) for n in group]
        gb3 = [by_shard(grads_b[n]) if n in grads_b else g.astype(BF16) for n, g in zip(group, g3)]
        gb3, marker = lax.optimization_barrier((gb3, marker))
        return (g3, _exchange_contributions_async(group, gb3, tag)), marker

    loss, grad_x, grads, grads_b, (late_g3, late_got7) = _local_step(
        x[0], positions.reshape(t, 1), loss_target[0], nw, full, functools.partial(start_reduce, LATE, "late"))

    (early_g3, early_got7), late_got7 = start_reduce(EARLY, "early", grads, grads_b, list(late_got7))
    where = jnp.concatenate([_shard_index(), _core_index()])
    outs_g, outs_d, outs_m, outs_v = {}, {}, {}, {}

    def finish(group, g3, got7, tag):
        mine = [_shard_sum_direct(g, r7, n, where) for n, g, r7 in zip(group, g3, got7)]
        halves = _with_sibling_half(group, mine, tag)
        for n in group:
            res = _adamw_halves(_stored(n, wts[n]), _stored(n, mom[n]), _stored(n, vel[n]), *halves[n], n, where[1:])
            outs_g[n], outs_d[n], outs_m[n], outs_v[n] = [_unstored(n, r) for r in res]

    finish(LATE, late_g3, late_got7, "late")
    small_g, small_d, small_m, small_v, total_loss = _small_allreduce_adamw(grads, loss, wts, mom, vel)
    finish(EARLY, early_g3, early_got7, "early")
    outs_g.update(small_g)
    outs_d.update(small_d)
    outs_m.update(small_m)
    outs_v.update(small_v)
    return (total_loss, grad_x[None], *[outs_g[n] for n in names], *[outs_d[n] for n in names],
            *[outs_m[n] for n in names], *[outs_v[n] for n in names])
```

```python
import functools
import math

import jax
import jax.numpy as jnp
from jax import lax
from jax.experimental import pallas as pl
from jax.experimental.pallas import tpu as pltpu
from jax.experimental.pallas import tpu_sc as plsc

F32 = jnp.float32
BF16 = jnp.bfloat16

D_MODEL = 1024
M_HEADS = 4
M_HEAD_DIM = 128
M_WIDTH = M_HEADS * M_HEAD_DIM
M_CHUNK = 64
A_HEADS = 8
A_NOPE = 64
A_ROPE = 32
A_V = 64
A_WIDTH = A_HEADS * A_V
A_QK_PAD = 128
Q_RANK = 384
KV_RANK = 256
ROPE_THETA = 10000.0
D_FF = 2816
D_IN = 2728
EPS = 1e-6
ATTN_SCALE = (A_NOPE + A_ROPE) ** -0.5
M_SCALE = M_HEAD_DIM ** -0.5

ADAM_LR = 0.001
ADAM_B1 = 0.9
ADAM_B2 = 0.999
ADAM_EPS = 1e-08
ADAM_WD = 0.01
ADAM_STEP = 10

VMEM_LIMIT_BYTES = 56 * 1024 * 1024
LANES = 128
NEG_INF = float("-inf")


def _cparams(*sem):
    return pltpu.CompilerParams(dimension_semantics=sem if sem else None, vmem_limit_bytes=VMEM_LIMIT_BYTES)


_DIMS = {"nn": (((1,), (0,)), ((), ())), "nt": (((1,), (1,)), ((), ())), "tn": (((0,), (0,)), ((), ()))}


def _matmul(a, b, mode, out_dtype, name, tm=None, tn=1024, residual=None, a_rows=None, b_rows=None, lead_scale=None,
            lead_row_scale=None, also_bf16=False):
    a_list = list(a) if isinstance(a, (list, tuple)) else [a]
    b_list = list(b) if isinstance(b, (list, tuple)) else [b]
    rows_list = list(b_rows) if isinstance(b_rows, (list, tuple)) else [b_rows] * len(b_list)
    assert len(a_list) == len(b_list) == len(rows_list)
    dims = _DIMS[mode]
    specs, m, n = [], None, None
    for aa, bb, rr in zip(a_list, b_list, rows_list):
        b_shape = bb.shape if rr is None else (rr, bb.shape[1])
        a_shape = aa.shape if a_rows is None else (a_rows, aa.shape[1])
        if mode == "nn":
            (m1, k), (k2, n1) = a_shape, b_shape
        elif mode == "nt":
            (m1, k), (n1, k2) = a_shape, b_shape
        else:
            (k, m1), (k2, n1) = aa.shape, b_shape
        assert k == k2 and (m is None or (m, n) == (m1, n1)), (aa.shape, bb.shape, mode)
        m, n = m1, n1
        specs.append(k)
    if tm is None:
        tm = 512 if mode == "tn" else 1024
    tm, tn = min(tm, m), min(tn, n)
    assert m % tm == 0 and n % tn == 0, (m, n, tm, tn)
    in_specs = []
    for k in specs:
        in_specs.append(pl.BlockSpec((k, tm), lambda i, j: (0, i)) if mode == "tn"
                        else pl.BlockSpec((tm, k), lambda i, j: (i, 0)))
        in_specs.append(pl.BlockSpec((tn, k), lambda i, j: (j, 0)) if mode == "nt"
                        else pl.BlockSpec((k, tn), lambda i, j: (0, j)))
    o_spec = pl.BlockSpec((tm, tn), lambda i, j: (i, j))
    n_pairs = len(specs)

    def body(*refs):
        acc = None
        for p in range(n_pairs):
            part = lax.dot_general(refs[2 * p][...].astype(BF16), refs[2 * p + 1][...].astype(BF16), dims,
                                   preferred_element_type=F32)
            acc = part if acc is None else acc + part
        if lead_scale is not None:
            col = pl.program_id(1) * tn + lax.broadcasted_iota(jnp.int32, (1, tn), 1)
            acc = acc * jnp.where(col < lead_scale[0], lead_scale[1], 1.0)
        if lead_row_scale is not None:
            row = pl.program_id(0) * tm + lax.broadcasted_iota(jnp.int32, (tm, 1), 0)
            acc = acc * jnp.where(row < lead_row_scale[0], lead_row_scale[1], 1.0)
        if residual is not None:
            acc = acc + refs[2 * n_pairs][...].astype(F32)
        outs = refs[2 * n_pairs + (residual is not None):]
        outs[0][...] = acc.astype(outs[0].dtype)
        if also_bf16:
            outs[1][...] = acc.astype(BF16)

    ins = [x for pair in zip(a_list, b_list) for x in pair] + ([residual] if residual is not None else [])
    in_specs = in_specs + ([o_spec] if residual is not None else [])
    out_shape = jax.ShapeDtypeStruct((m, n), out_dtype)
    if also_bf16:
        return pl.pallas_call(
            body, name=name, grid=(m // tm, n // tn), in_specs=in_specs, out_specs=[o_spec, o_spec],
            out_shape=[out_shape, jax.ShapeDtypeStruct((m, n), BF16)],
            compiler_params=_cparams("parallel", "parallel"),
        )(*ins)
    return pl.pallas_call(
        body, name=name, grid=(m // tm, n // tn), in_specs=in_specs, out_specs=o_spec, out_shape=out_shape,
        compiler_params=_cparams("parallel", "parallel"),
    )(*ins)


def _rms(xf, w):
    return xf * lax.rsqrt(jnp.mean(xf * xf, axis=-1, keepdims=True) + EPS) * w


def _rms_bwd(dyf, xf, w):
    r = lax.rsqrt(jnp.mean(xf * xf, axis=-1, keepdims=True) + EPS)
    xh = xf * r
    dyh = dyf * w
    return r * (dyh - xh * jnp.mean(dyh * xh, axis=-1, keepdims=True)), dyf * xh


def _accumulate(ref, part):
    @pl.when(pl.program_id(0) == 0)
    def _():
        ref[...] = part

    @pl.when(pl.program_id(0) > 0)
    def _():
        ref[...] += part


def _rows(tm, width):
    return pl.BlockSpec((tm, width), lambda i: (i, 0))


def _whole(arr):
    return pl.BlockSpec(arr.shape, lambda i: (0,) * arr.ndim)


def _proj_latents_norm(u, w_a2, w_qn, w_kvn, name, tm=512):
    t = u.shape[0]
    tm = min(tm, t)

    def body(u_ref, w_ref, wq_ref, wkv_ref, a2_ref, qn_ref, kvn_ref):
        a2 = lax.dot_general(u_ref[...], w_ref[...], _DIMS["nt"], preferred_element_type=F32).astype(BF16)
        a2_ref[...] = a2
        qn_ref[...] = _rms(a2[:, :Q_RANK].astype(F32), wq_ref[...]).astype(BF16)
        kvn_ref[...] = _rms(a2[:, Q_RANK:].astype(F32), wkv_ref[...]).astype(BF16)

    return pl.pallas_call(
        body, name=name, grid=(t // tm,), in_specs=[_rows(tm, u.shape[1]), _whole(w_a2), _whole(w_qn), _whole(w_kvn)],
        out_specs=[_rows(tm, Q_RANK + KV_RANK), _rows(tm, Q_RANK), _rows(tm, KV_RANK)],
        out_shape=[jax.ShapeDtypeStruct((t, Q_RANK + KV_RANK), BF16), jax.ShapeDtypeStruct((t, Q_RANK), BF16),
                   jax.ShapeDtypeStruct((t, KV_RANK), BF16)],
        compiler_params=_cparams("parallel"),
    )(u, w_a2, w_qn, w_kvn)


def _out_proj_norm(hm, ha, w_m, w_a, x, w_norm, name, tm=512):
    t = x.shape[0]
    tm = min(tm, t)

    def body(hm_ref, ha_ref, wm_ref, wa_ref, x_ref, wn_ref, h_ref, u_ref):
        h = (jnp.dot(hm_ref[...], wm_ref[...], preferred_element_type=F32)
             + jnp.dot(ha_ref[...], wa_ref[...], preferred_element_type=F32) + x_ref[...])
        h_ref[...] = h
        u_ref[...] = _rms(h, wn_ref[...]).astype(BF16)

    d = x.shape[1]
    return pl.pallas_call(
        body, name=name, grid=(t // tm,),
        in_specs=[_rows(tm, hm.shape[1]), _rows(tm, ha.shape[1]), _whole(w_m), _whole(w_a), _rows(tm, d), _whole(w_norm)],
        out_specs=[_rows(tm, d), _rows(tm, d)],
        out_shape=[jax.ShapeDtypeStruct((t, d), F32), jax.ShapeDtypeStruct((t, d), BF16)],
        compiler_params=_cparams("parallel"),
    )(hm, ha, w_m, w_a, x, w_norm)


def _rmsnorm_fwd(x, w, name, tm=512):
    t, d = x.shape

    def body(x_ref, w_ref, o_ref):
        xf = x_ref[...].astype(F32)
        r = lax.rsqrt(jnp.mean(xf * xf, axis=-1, keepdims=True) + EPS)
        o_ref[...] = (xf * r * w_ref[...]).astype(o_ref.dtype)

    return pl.pallas_call(
        body, name=name, grid=(t // tm,),
        in_specs=[pl.BlockSpec((tm, d), lambda i: (i, 0)), pl.BlockSpec((1, d), lambda i: (0, 0))],
        out_specs=pl.BlockSpec((tm, d), lambda i: (i, 0)),
        out_shape=jax.ShapeDtypeStruct((t, d), BF16), compiler_params=_cparams("parallel"),
    )(x, w)


def _ffn_down_loss(act, w_down, h1, target, w, name, tm=512):
    t, d = h1.shape
    tm = min(tm, t)

    def body(a_ref, wd_ref, h_ref, t_ref, w_ref, dh_ref, dhb_ref, loss_ref, dw_ref):
        xf = jnp.dot(a_ref[...], wd_ref[...], preferred_element_type=F32) + h_ref[...]
        r = lax.rsqrt(jnp.mean(xf * xf, axis=-1, keepdims=True) + EPS)
        xh = xf * r
        err = xh * w_ref[...] - t_ref[...]
        part_loss = 0.5 * jnp.sum(jnp.sum(err * err, axis=-1, keepdims=True), axis=0, keepdims=True) * (1.0 / d)
        dy = err * (1.0 / d)
        dyh = dy * w_ref[...]
        dh = r * (dyh - xh * jnp.mean(dyh * xh, axis=-1, keepdims=True))
        dh_ref[...] = dh
        dhb_ref[...] = dh.astype(BF16)
        _accumulate(dw_ref, jnp.sum(dy * xh, axis=0, keepdims=True))
        _accumulate(loss_ref, jnp.broadcast_to(part_loss, (1, LANES)))

    vec = pl.BlockSpec((1, d), lambda i: (0, 0))
    return pl.pallas_call(
        body, name=name, grid=(t // tm,),
        in_specs=[_rows(tm, act.shape[1]), _whole(w_down), _rows(tm, d), _rows(tm, d), vec],
        out_specs=[_rows(tm, d), _rows(tm, d), pl.BlockSpec((1, LANES), lambda i: (0, 0)), vec],
        out_shape=[jax.ShapeDtypeStruct((t, d), F32), jax.ShapeDtypeStruct((t, d), BF16),
                   jax.ShapeDtypeStruct((1, LANES), F32), jax.ShapeDtypeStruct((1, d), F32)],
        compiler_params=_cparams("arbitrary"),
    )(act, w_down, h1, target, w)


def _matmul_norm_bwd(das, ws, x, w_norm, name, out_dtype, residual=None, also_bf16=False, w_rows=None, tm=512):
    t, d = x.shape
    tm = min(tm, t)
    n = len(das)

    def body(*refs):
        x_ref, wn_ref = refs[2 * n], refs[2 * n + 1]
        outs = refs[2 * n + 2 + (residual is not None):]
        dy = None
        for p in range(n):
            wp = refs[n + p][...] if w_rows is None or w_rows[p] is None else refs[n + p][0:w_rows[p], :]
            part = jnp.dot(refs[p][...], wp, preferred_element_type=F32)
            dy = part if dy is None else dy + part
        dx, dw_terms = _rms_bwd(dy, x_ref[...].astype(F32), wn_ref[...])
        if residual is not None:
            dx = dx + refs[2 * n + 2][...]
        outs[0][...] = dx.astype(outs[0].dtype)
        if also_bf16:
            outs[1][...] = dx.astype(BF16)
        _accumulate(outs[-1], jnp.sum(dw_terms, axis=0, keepdims=True))

    extra = [jax.ShapeDtypeStruct((t, d), BF16)] if also_bf16 else []
    return pl.pallas_call(
        body, name=name, grid=(t // tm,),
        in_specs=([_rows(tm, a.shape[1]) for a in das] + [_whole(w) for w in ws] + [_rows(tm, d), _whole(w_norm)]
                  + ([_rows(tm, d)] if residual is not None else [])),
        out_specs=[_rows(tm, d)] * (1 + len(extra)) + [pl.BlockSpec((1, d), lambda i: (0, 0))],
        out_shape=[jax.ShapeDtypeStruct((t, d), out_dtype)] + extra + [jax.ShapeDtypeStruct((1, d), F32)],
        compiler_params=_cparams("arbitrary"),
    )(*das, *ws, x, w_norm, *([residual] if residual is not None else []))


FFN_TN = 1408


def _ffn_gate_up(u, w_gate, w_up, name, tm=1024):
    t, d = u.shape
    n = w_gate.shape[0]
    tm = min(tm, t)

    def body(u_ref, wg_ref, wu_ref, g_ref, up_ref, act_ref):
        uu = u_ref[...]
        g = lax.dot_general(uu, wg_ref[...], _DIMS["nt"], preferred_element_type=F32)
        up = lax.dot_general(uu, wu_ref[...], _DIMS["nt"], preferred_element_type=F32)
        g_b, up_b = g.astype(BF16), up.astype(BF16)
        g_ref[...] = g_b
        up_ref[...] = up_b
        gf, uf = g_b.astype(F32), up_b.astype(F32)
        act_ref[...] = (gf * jax.nn.sigmoid(gf) * uf).astype(act_ref.dtype)

    w_spec = pl.BlockSpec((FFN_TN, d), lambda i, j: (j, 0))
    o_spec = pl.BlockSpec((tm, FFN_TN), lambda i, j: (i, j))
    out = jax.ShapeDtypeStruct((t, n), BF16)
    return pl.pallas_call(
        body, name=name, grid=(t // tm, n // FFN_TN),
        in_specs=[pl.BlockSpec((tm, d), lambda i, j: (i, 0)), w_spec, w_spec], out_specs=[o_spec] * 3,
        out_shape=[out] * 3, compiler_params=_cparams("parallel", "parallel"),
    )(u, w_gate, w_up)


def _ffn_down_bwd(dh, w_down, g, up, name, tm=1024):
    t, d = dh.shape
    n = w_down.shape[0]
    tm = min(tm, t)

    def body(dh_ref, w_ref, g_ref, up_ref, dg_ref, du_ref):
        df = lax.dot_general(dh_ref[...], w_ref[...], _DIMS["nt"], preferred_element_type=F32).astype(BF16).astype(F32)
        gf = g_ref[...].astype(F32)
        uf = up_ref[...].astype(F32)
        s = jax.nn.sigmoid(gf)
        dg_ref[...] = (df * uf * s * (1.0 + gf * (1.0 - s))).astype(dg_ref.dtype)
        du_ref[...] = (df * gf * s).astype(du_ref.dtype)

    o_spec = pl.BlockSpec((tm, FFN_TN), lambda i, j: (i, j))
    out = jax.ShapeDtypeStruct((t, n), BF16)
    return pl.pallas_call(
        body, name=name, grid=(t // tm, n // FFN_TN),
        in_specs=[pl.BlockSpec((tm, d), lambda i, j: (i, 0)), pl.BlockSpec((FFN_TN, d), lambda i, j: (j, 0)),
                  o_spec, o_spec],
        out_specs=[o_spec, o_spec], out_shape=[out, out], compiler_params=_cparams("parallel", "parallel"),
    )(dh, w_down, g, up)


M_BLOCK = 512
M_CHUNKS_PER_BLOCK = M_BLOCK // M_CHUNK
M_UNROLL = 2


def _log_sigmoid(z):
    return jnp.minimum(z, 0.0) - jnp.log(1.0 + jnp.exp(-jnp.abs(z)))


def _per_head(fn):
    return jnp.stack([fn(hd) for hd in range(M_HEADS)])


def _mlstm_chunk_fwd(q, k, v, gc, c_prev, n_prev, m_prev):
    L, H = M_CHUNK, M_HEADS
    tt = lax.broadcasted_iota(jnp.int32, (L, L), 0)
    ss = lax.broadcasted_iota(jnp.int32, (L, L), 1)
    eye = tt == ss
    causal = ss <= tt
    gct = jnp.transpose(gc)
    i_col = _per_head(lambda hd: gc[:, hd:hd + 1])
    f_col = _per_head(lambda hd: gc[:, H + hd:H + hd + 1])
    i_row = _per_head(lambda hd: gct[hd:hd + 1, :])
    lf_col = _log_sigmoid(f_col)
    lf_row = _log_sigmoid(_per_head(lambda hd: gct[H + hd:H + hd + 1, :]))
    b_col = jnp.sum(jnp.where(causal, lf_row, 0.0), axis=2, keepdims=True)
    b_row = jnp.sum(jnp.where(tt <= ss, lf_col, 0.0), axis=1, keepdims=True)
    g = jnp.sum(lf_col, axis=1, keepdims=True)
    a_row = g - b_row + i_row
    a_col = g - b_col + i_col
    m_loc = jnp.max(a_row, axis=2, keepdims=True)
    d_log = jnp.where(causal, b_col - b_row + i_row, NEG_INF)
    inter = b_col + m_prev
    m_t = jnp.maximum(jnp.max(d_log, axis=2, keepdims=True), inter)
    dmat = jnp.exp(d_log - m_t)
    amat = _per_head(lambda hd: lax.dot_general(q[hd], k[hd], _DIMS["nt"], preferred_element_type=F32))
    p = dmat * amat
    sig = jnp.exp(inter - m_t)
    c_prev_b = [c.astype(BF16) for c in c_prev]
    qc = _per_head(lambda hd: jnp.dot(q[hd], c_prev_b[hd], preferred_element_type=F32))
    p_b = p.astype(BF16)
    num = _per_head(lambda hd: jnp.dot(p_b[hd], v[hd], preferred_element_type=F32)) + sig * qc
    qf = _per_head(lambda hd: q[hd].astype(F32))
    qn = jnp.sum(qf * n_prev, axis=2, keepdims=True)
    den_raw = jnp.sum(p, axis=2, keepdims=True) + sig * qn
    floor = jnp.exp(-m_t)
    den = jnp.maximum(jnp.abs(den_raw), floor)
    h = num / den
    m_new = jnp.maximum(g + m_prev, m_loc)
    w_col = jnp.exp(a_col - m_new)
    w_row = jnp.exp(a_row - m_new)
    alpha = jnp.exp(g + m_prev - m_new)
    return dict(eye=eye, causal=causal, tt=tt, ss=ss, i_col=i_col, f_col=f_col, dmat=dmat, amat=amat, p=p, p_b=p_b,
                sig=sig, qc=qc, qf=qf, qn=qn, num=num, den_raw=den_raw, floor=floor, den=den, h=h, m_new=m_new,
                w_col=w_col, w_row=w_row, alpha=alpha, c_prev_b=c_prev_b, b_row=b_row, b_col=b_col, m_t=m_t)


def _chunk_major_t(xt):
    f, t = xt.shape
    return xt.reshape(f, t // M_CHUNK, M_CHUNK).transpose(1, 0, 2)


def _mlstm_fwd(m4, kt, gates, bias, norm_w, name):
    t = m4.shape[0]
    nblk = t // M_BLOCK
    nc = t // M_CHUNK
    L, dh = M_CHUNK, M_HEAD_DIM

    def body(m4_ref, kt_ref, g_ref, b_ref, w_ref, hm_ref, cp_ref, nm_ref, c_s, n_s, m_s):
        @pl.when(pl.program_id(0) == 0)
        def _():
            c_s[...] = jnp.zeros_like(c_s)
            n_s[...] = jnp.zeros_like(n_s)
            m_s[...] = jnp.zeros_like(m_s)

        row8 = lax.broadcasted_iota(jnp.int32, (8, dh), 0)

        def chunk(c, carry):
            rows = pl.ds(pl.multiple_of(c * L, L), L)
            gc = g_ref[rows, :] + b_ref[...]
            col = lambda part, hd: m4_ref[rows, part * M_WIDTH + hd * dh:part * M_WIDTH + (hd + 1) * dh]
            q = [col(0, hd) for hd in range(M_HEADS)]
            k = [col(1, hd) for hd in range(M_HEADS)]
            v = [col(2, hd) for hd in range(M_HEADS)]
            o = _per_head(lambda hd: col(3, hd).astype(F32))
            c_prev = [c_s[hd] for hd in range(M_HEADS)]
            n_prev = n_s[:, 0:1, :]
            m_prev = m_s[:, 0:1, 0:1]
            r = _mlstm_chunk_fwd(q, k, v, gc, c_prev, n_prev, m_prev)
            h = r["h"]
            hn = h * lax.rsqrt(jnp.mean(h * h, axis=-1, keepdims=True) + EPS) * w_ref[...]
            hm = (hn * jax.nn.sigmoid(o)).astype(hm_ref.dtype)
            nm = jnp.where(row8 == 0, n_prev, jnp.where(row8 == 1, m_prev, 0.0))
            kf = _per_head(lambda hd: k[hd].astype(F32))
            n_s[:, 0:1, :] = r["alpha"] * n_prev + jnp.sum(kf * r["w_col"], axis=1, keepdims=True)
            m_s[...] = jnp.broadcast_to(r["m_new"], (M_HEADS, 8, dh))
            for hd in range(M_HEADS):
                hm_ref[rows, hd * dh:(hd + 1) * dh] = hm[hd]
                cp_ref[hd, c] = r["c_prev_b"][hd]
                nm_ref[hd, c] = nm[hd]
                kw_t = (kt_ref[c, hd * dh:(hd + 1) * dh, :].astype(F32) * r["w_row"][hd]).astype(BF16)
                c_s[hd] = r["alpha"][hd] * c_prev[hd] + jnp.dot(kw_t, v[hd], preferred_element_type=F32)
            return carry

        lax.fori_loop(0, M_CHUNKS_PER_BLOCK, chunk, 0)

    return pl.pallas_call(
        body, name=name, grid=(nblk,),
        in_specs=[pl.BlockSpec((M_BLOCK, 4 * M_WIDTH), lambda i: (i, 0)),
                  pl.BlockSpec((M_CHUNKS_PER_BLOCK, M_WIDTH, L), lambda i: (i, 1, 0)),
                  pl.BlockSpec((M_BLOCK, LANES), lambda i: (i, 0)),
                  pl.BlockSpec((1, LANES), lambda i: (0, 0)),
                  pl.BlockSpec((M_HEADS, 1, dh), lambda i: (0, 0, 0))],
        out_specs=[pl.BlockSpec((M_BLOCK, M_WIDTH), lambda i: (i, 0)),
                   pl.BlockSpec((M_HEADS, M_CHUNKS_PER_BLOCK, dh, dh), lambda i: (0, i, 0, 0)),
                   pl.BlockSpec((M_HEADS, M_CHUNKS_PER_BLOCK, 8, dh), lambda i: (0, i, 0, 0))],
        out_shape=[jax.ShapeDtypeStruct((t, M_WIDTH), BF16),
                   jax.ShapeDtypeStruct((M_HEADS, nc, dh, dh), BF16),
                   jax.ShapeDtypeStruct((M_HEADS, nc, 8, dh), F32)],
        scratch_shapes=[pltpu.VMEM((M_HEADS, dh, dh), F32), pltpu.VMEM((M_HEADS, 8, dh), F32),
                        pltpu.VMEM((M_HEADS, 8, dh), F32)],
        compiler_params=_cparams("arbitrary"),
    )(m4, kt, gates, bias, norm_w)


def _mlstm_bwd(m4, qt, gates, bias, norm_w, c_prev_all, nm_all, dhm, name):
    t = m4.shape[0]
    nblk = t // M_BLOCK
    L, dh = M_CHUNK, M_HEAD_DIM

    def body(m4_ref, qt_ref, g_ref, b_ref, w_ref, cp_ref, nm_ref, dhm_ref, dm4_ref, dg_ref, small_ref, dc_s, dn_s):
        @pl.when(pl.program_id(0) == 0)
        def _():
            dc_s[...] = jnp.zeros_like(dc_s)
            dn_s[...] = jnp.zeros_like(dn_s)
            small_ref[...] = jnp.zeros_like(small_ref)

        lane = lax.broadcasted_iota(jnp.int32, (L, LANES), 1)
        row8 = lax.broadcasted_iota(jnp.int32, (8, LANES), 0)

        def chunk(ci, carry):
            c = M_CHUNKS_PER_BLOCK - 1 - ci
            rows = pl.ds(pl.multiple_of(c * L, L), L)
            gc = g_ref[rows, :] + b_ref[...]
            heads = range(M_HEADS)
            col = lambda part, hd: m4_ref[rows, part * M_WIDTH + hd * dh:part * M_WIDTH + (hd + 1) * dh]
            q = [col(0, hd) for hd in heads]
            k = [col(1, hd) for hd in heads]
            v = [col(2, hd) for hd in heads]
            o = _per_head(lambda hd: col(3, hd).astype(F32))
            q_t = [qt_ref[c, hd * dh:(hd + 1) * dh, :] for hd in heads]
            c_prev = [cp_ref[hd, c].astype(F32) for hd in heads]
            nm = _per_head(lambda hd: nm_ref[hd, c])
            n_prev = nm[:, 0:1, :]
            m_prev = nm[:, 1:2, 0:1]
            r = _mlstm_chunk_fwd(q, k, v, gc, c_prev, n_prev, m_prev)
            eye, tt, ss = r["eye"], r["tt"], r["ss"]
            h, den, sig, p = r["h"], r["den"], r["sig"], r["p"]
            w_col, alpha, c_prev_b = r["w_col"], r["alpha"], r["c_prev_b"]
            to_row = lambda colv: jnp.sum(jnp.where(eye, colv, 0.0), axis=1, keepdims=True)
            to_col = lambda rowv: jnp.sum(jnp.where(eye, rowv, 0.0), axis=2, keepdims=True)
            mm = lambda fn: _per_head(lambda hd: fn(hd))

            w_h = w_ref[...]
            rn = lax.rsqrt(jnp.mean(h * h, axis=-1, keepdims=True) + EPS)
            hh = h * rn
            og = jax.nn.sigmoid(o)
            dhm_c = _per_head(lambda hd: dhm_ref[rows, hd * dh:(hd + 1) * dh].astype(F32))
            dhn = dhm_c * og
            d_o = dhm_c * hh * w_h * og * (1.0 - og)
            d_norm = jnp.sum(dhn * hh, axis=1, keepdims=True)
            dhh = dhn * w_h
            dh_ = rn * (dhh - hh * jnp.mean(dhh * hh, axis=-1, keepdims=True))

            dnum = dh_ / den
            dden = -jnp.sum(dh_ * h, axis=-1, keepdims=True) / den
            dden_raw = jnp.where(jnp.abs(r["den_raw"]) >= r["floor"], dden * jnp.sign(r["den_raw"]), 0.0)
            dnum_b = dnum.astype(BF16)
            dp = mm(lambda hd: lax.dot_general(dnum_b[hd], v[hd], _DIMS["nt"], preferred_element_type=F32)) + dden_raw
            dc = [dc_s[hd] for hd in heads]
            dn = dn_s[:, 0:1, :]
            dc_b = [d.astype(BF16) for d in dc]
            g2_b = (sig * dnum).astype(BF16)
            sd = sig * dden_raw
            da_mat = (dp * r["dmat"]).astype(BF16)
            dqs = mm(lambda hd: lax.dot_general(g2_b[hd], c_prev_b[hd], _DIMS["nt"], preferred_element_type=F32)
                     + jnp.dot(da_mat[hd], k[hd], preferred_element_type=F32)) + sd * n_prev
            r_mat = mm(lambda hd: lax.dot_general(v[hd], dc_b[hd], _DIMS["nt"], preferred_element_type=F32)) + dn
            kf = _per_head(lambda hd: k[hd].astype(F32))
            dmat_t = jnp.exp(jnp.where(tt <= ss, r["b_row"] - r["b_col"] + r["i_col"], NEG_INF) - to_row(r["m_t"]))
            p_t = (dmat_t * mm(lambda hd: jnp.dot(k[hd], q_t[hd], preferred_element_type=F32))).astype(BF16)
            dp_t = mm(lambda hd: lax.dot_general(v[hd], dnum_b[hd], _DIMS["nt"], preferred_element_type=F32))
            da_t = ((dp_t + to_row(dden_raw)) * dmat_t).astype(BF16)
            d_k = mm(lambda hd: jnp.dot(da_t[hd], q[hd], preferred_element_type=F32)) + w_col * r_mat
            d_v = (mm(lambda hd: jnp.dot(p_t[hd], dnum_b[hd], preferred_element_type=F32))
                   + w_col * mm(lambda hd: jnp.dot(k[hd], dc_b[hd], preferred_element_type=F32)))
            da_col = jnp.sum(kf * r_mat, axis=-1, keepdims=True) * w_col
            dsig = jnp.sum(dnum * r["qc"], axis=-1, keepdims=True) + dden_raw * r["qn"]
            pp = dp * p
            r1 = jnp.sum(pp, axis=2, keepdims=True)
            c1_col = to_col(jnp.sum(pp, axis=1, keepdims=True))
            dcc = _per_head(lambda hd: dc[hd] * c_prev[hd])
            d_alpha = (jnp.sum(jnp.sum(dcc, axis=2, keepdims=True), axis=1, keepdims=True)
                       + jnp.sum(dn * n_prev, axis=2, keepdims=True))
            dgl = jnp.sum(da_col, axis=1, keepdims=True) + d_alpha * alpha
            db_col = r1 + dsig * sig - c1_col - da_col
            dli_col = c1_col + da_col
            dlf_col = jnp.sum(jnp.where(ss >= tt, to_row(db_col), 0.0), axis=2, keepdims=True) + dgl
            df_col = dlf_col * jax.nn.sigmoid(-r["f_col"])

            dn_s[:, 0:1, :] = alpha * dn + jnp.sum(sd * r["qf"], axis=1, keepdims=True)
            dq_out = (dqs * M_SCALE).astype(dm4_ref.dtype)
            dk_out, dv_out, do_out = d_k.astype(dm4_ref.dtype), d_v.astype(dm4_ref.dtype), d_o.astype(dm4_ref.dtype)
            dg_tile = jnp.zeros((L, LANES), F32)
            small = jnp.zeros((8, LANES), F32)
            for hd in heads:
                dc_s[hd] = alpha[hd] * dc[hd] + jnp.dot(q_t[hd], g2_b[hd], preferred_element_type=F32)
                dm4_ref[rows, hd * dh:(hd + 1) * dh] = dq_out[hd]
                dm4_ref[rows, M_WIDTH + hd * dh:M_WIDTH + (hd + 1) * dh] = dk_out[hd]
                dm4_ref[rows, 2 * M_WIDTH + hd * dh:2 * M_WIDTH + (hd + 1) * dh] = dv_out[hd]
                dm4_ref[rows, 3 * M_WIDTH + hd * dh:3 * M_WIDTH + (hd + 1) * dh] = do_out[hd]
                dg_tile = (dg_tile + jnp.where(lane == hd, dli_col[hd], 0.0)
                           + jnp.where(lane == M_HEADS + hd, df_col[hd], 0.0))
                small = small + jnp.where(row8 == hd, d_norm[hd], 0.0)
            dg_ref[rows, :] = dg_tile
            small = small + jnp.where(row8 == M_HEADS, jnp.sum(dg_tile, axis=0, keepdims=True), 0.0)
            small_ref[...] += small
            return carry

        lax.fori_loop(0, M_CHUNKS_PER_BLOCK, chunk, 0, unroll=M_UNROLL)

    rev = lambda i: nblk - 1 - i
    return pl.pallas_call(
        body, name=name, grid=(nblk,),
        in_specs=[pl.BlockSpec((M_BLOCK, 4 * M_WIDTH), lambda i: (rev(i), 0)),
                  pl.BlockSpec((M_CHUNKS_PER_BLOCK, M_WIDTH, L), lambda i: (rev(i), 0, 0)),
                  pl.BlockSpec((M_BLOCK, LANES), lambda i: (rev(i), 0)),
                  pl.BlockSpec((1, LANES), lambda i: (0, 0)),
                  pl.BlockSpec((M_HEADS, 1, dh), lambda i: (0, 0, 0)),
                  pl.BlockSpec((M_HEADS, M_CHUNKS_PER_BLOCK, dh, dh), lambda i: (0, rev(i), 0, 0)),
                  pl.BlockSpec((M_HEADS, M_CHUNKS_PER_BLOCK, 8, dh), lambda i: (0, rev(i), 0, 0)),
                  pl.BlockSpec((M_BLOCK, M_WIDTH), lambda i: (rev(i), 0))],
        out_specs=[pl.BlockSpec((M_BLOCK, 4 * M_WIDTH), lambda i: (rev(i), 0)),
                   pl.BlockSpec((M_BLOCK, LANES), lambda i: (rev(i), 0)),
                   pl.BlockSpec((8, LANES), lambda i: (0, 0))],
        out_shape=[jax.ShapeDtypeStruct((t, 4 * M_WIDTH), BF16), jax.ShapeDtypeStruct((t, LANES), F32),
                   jax.ShapeDtypeStruct((8, LANES), F32)],
        scratch_shapes=[pltpu.VMEM((M_HEADS, dh, dh), F32), pltpu.VMEM((M_HEADS, 8, dh), F32)],
        compiler_params=_cparams("arbitrary"),
    )(m4, qt, gates, bias, norm_w, c_prev_all, nm_all, dhm)


def _rope_tables(pos_col, width):
    lane = lax.broadcasted_iota(jnp.int32, (1, width), 1) % A_QK_PAD
    half = A_ROPE // 2
    first = (lane >= A_NOPE) & (lane < A_NOPE + half)
    second = (lane >= A_NOPE + half) & (lane < A_NOPE + A_ROPE)
    idx = jnp.where(first, lane - A_NOPE, lane - A_NOPE - half).astype(F32)
    inv_freq = jnp.exp(idx * (-math.log(ROPE_THETA) / half))
    ang = pos_col.astype(F32) * inv_freq
    cos, sin = jnp.cos(ang), jnp.sin(ang)
    rot = first | second
    return jnp.where(rot, cos, 1.0), jnp.where(first, -sin, 0.0), jnp.where(second, sin, 0.0)


def _rope_apply(vv, cosf, s1, s2):
    half = A_ROPE // 2
    w = vv.shape[-1]
    return vv * cosf + pltpu.roll(vv, w - half, 1) * s1 + pltpu.roll(vv, half, 1) * s2


def _rope_apply_t(dd, cosf, s1, s2):
    half = A_ROPE // 2
    w = dd.shape[-1]
    return dd * cosf + pltpu.roll(dd * s1, half, 1) + pltpu.roll(dd * s2, w - half, 1)


A_BIAS_LANE_K = A_NOPE + A_ROPE
A_BIAS_LANE_V = A_V


def _hi_lo(val):
    hi = val.astype(BF16)
    return hi, (val - hi.astype(F32)).astype(BF16)


def _qkv_up_rope(qa_n, kv_n, w_qb, w_k, w_v, gates, pos, name, tm=512):
    t = qa_n.shape[0]
    w = A_HEADS * LANES

    def body(qa_ref, kv_ref, wq_ref, wk_ref, wv_ref, g_ref, p_ref, qo_ref, ko_ref, vo_ref):
        cosf, s1, s2 = _rope_tables(p_ref[...], LANES)
        lane = lax.broadcasted_iota(jnp.int32, (1, LANES), 1)
        kr = jnp.where((lane >= A_NOPE) & (lane < A_NOPE + A_ROPE), g_ref[...], 0.0)
        kr = _rope_apply(kr, cosf, s1, s2)
        kr = jnp.where((lane == A_BIAS_LANE_K) | (lane == A_BIAS_LANE_K + 1), 1.0, kr)
        v_one = (lane == A_BIAS_LANE_V) | (lane == A_BIAS_LANE_V + 1)
        qpre = lax.dot_general(qa_ref[...], wq_ref[...], _DIMS["nt"], preferred_element_type=F32)
        kv = kv_ref[...]
        kpre = lax.dot_general(kv, wk_ref[...], _DIMS["nt"], preferred_element_type=F32)
        vpre = lax.dot_general(kv, wv_ref[...], _DIMS["nt"], preferred_element_type=F32)
        for hd in range(A_HEADS):
            sl = slice(hd * LANES, (hd + 1) * LANES)
            qo_ref[:, sl] = (_rope_apply(qpre[:, sl], cosf, s1, s2) * ATTN_SCALE).astype(qo_ref.dtype)
            ko_ref[:, sl] = (kpre[:, sl] + kr).astype(ko_ref.dtype)
            vo_ref[:, sl] = jnp.where(v_one, 1.0, vpre[:, sl]).astype(vo_ref.dtype)

    rows = lambda width: pl.BlockSpec((tm, width), lambda i: (i, 0))
    whole = lambda arr: pl.BlockSpec(arr.shape, lambda i: (0, 0))
    out = jax.ShapeDtypeStruct((t, w), BF16)
    return pl.pallas_call(
        body, name=name, grid=(t // tm,),
        in_specs=[rows(qa_n.shape[1]), rows(kv_n.shape[1]), whole(w_qb), whole(w_k), whole(w_v), rows(LANES), rows(1)],
        out_specs=[rows(w)] * 3, out_shape=[out] * 3, compiler_params=_cparams("parallel"),
    )(qa_n, kv_n, w_qb, w_k, w_v, gates, pos)


def _rope_bwd(dq, dk, dgates, pos, name, tm=512):
    t, w = dq.shape

    def body(dq_ref, dk_ref, dg_ref, p_ref, dqo_ref, dgo_ref):
        cosf, s1, s2 = _rope_tables(p_ref[...], LANES)
        acc = jnp.zeros((tm, LANES), F32)
        for hd in range(A_HEADS):
            sl = slice(hd * LANES, (hd + 1) * LANES)
            dqo_ref[:, sl] = (_rope_apply_t(dq_ref[:, sl], cosf, s1, s2) * ATTN_SCALE).astype(dqo_ref.dtype)
            acc = acc + dk_ref[:, sl].astype(F32)
        lane = lax.broadcasted_iota(jnp.int32, (1, LANES), 1)
        dkr = _rope_apply_t(acc, cosf, s1, s2)
        dkr = jnp.where((lane >= A_NOPE) & (lane < A_NOPE + A_ROPE), dkr, 0.0)
        dgo_ref[...] = (dg_ref[...] + dkr).astype(dgo_ref.dtype)

    row = pl.BlockSpec((tm, w), lambda i: (i, 0))
    nar = pl.BlockSpec((tm, LANES), lambda i: (i, 0))
    return pl.pallas_call(
        body, name=name, grid=(t // tm,),
        in_specs=[row, row, nar, pl.BlockSpec((tm, 1), lambda i: (i, 0))],
        out_specs=[row, nar],
        out_shape=[jax.ShapeDtypeStruct((t, w), BF16), jax.ShapeDtypeStruct((t, LANES), BF16)],
        compiler_params=_cparams("parallel"),
    )(dq, dk, dgates, pos)


A_TQ = 512
A_TK = 512


A_HEADS_PER_STEP = 2
A_HEADS_PER_STEP_FWD = 4


def _flash_fwd(q, k, v, name):
    t = q.shape[0]
    tq, tk = A_TQ, A_TK
    nh = A_HEADS_PER_STEP_FWD
    wblk = nh * LANES

    def body(q_ref, k_ref, v_ref, o_ref, qb_ref):
        i = pl.program_id(1)
        lane = lax.broadcasted_iota(jnp.int32, (tq, LANES), 1)
        qpos = i * tq + lax.broadcasted_iota(jnp.int32, (tq, tk), 0)
        kofs = lax.broadcasted_iota(jnp.int32, (tq, tk), 1)
        qs = [q_ref[:, hh * LANES:(hh + 1) * LANES] for hh in range(nh)]

        def step(kb, carry, masked):
            rows = pl.ds(pl.multiple_of(kb * tk, tk), tk)
            new = []
            for hh in range(nh):
                m, acc = carry[hh]
                kh = k_ref[rows, hh * LANES:(hh + 1) * LANES]
                s = lax.dot_general(qs[hh], kh, _DIMS["nt"], preferred_element_type=F32)
                if masked:
                    s = jnp.where(kb * tk + kofs <= qpos, s, NEG_INF)
                m_new = jnp.maximum(m, jnp.max(s, axis=1, keepdims=True))
                p = jnp.exp(s - m_new)
                acc = jnp.exp(m - m_new) * acc + jnp.dot(p.astype(BF16), v_ref[rows, hh * LANES:(hh + 1) * LANES],
                                                          preferred_element_type=F32)
                new.append((m_new, acc))
            return tuple(new)

        carry = tuple((jnp.full((tq, 1), NEG_INF, F32), jnp.zeros((tq, LANES), F32)) for _ in range(nh))
        n_full = (i * tq) // tk
        carry = lax.fori_loop(0, n_full, functools.partial(step, masked=False), carry)
        for d in range(tq // tk):
            carry = step(n_full + d, carry, True)
        for hh in range(nh):
            m, acc = carry[hh]
            l = acc[:, A_BIAS_LANE_V:A_BIAS_LANE_V + 1]
            o_ref[:, hh * LANES:(hh + 1) * LANES] = (acc / l).astype(o_ref.dtype)
            hi, lo = _hi_lo(-(m + jnp.log(l)))
            qb_ref[:, hh * LANES:(hh + 1) * LANES] = jnp.where(
                lane == A_BIAS_LANE_K, hi, jnp.where(lane == A_BIAS_LANE_K + 1, lo, qs[hh]))

    blk = pl.BlockSpec((tq, wblk), lambda j, i: (i, j))
    res = pl.BlockSpec((t, wblk), lambda j, i: (0, j))
    out = jax.ShapeDtypeStruct((t, A_HEADS * LANES), BF16)
    return pl.pallas_call(
        body, name=name, grid=(A_HEADS // nh, t // tq), in_specs=[blk, res, res], out_specs=[blk, blk],
        out_shape=[out, out], compiler_params=_cparams("parallel", "parallel"),
    )(q, k, v)


def _d_attn_out(dh, w_out_a, o, name, tm=512):
    t, w = o.shape
    tm = min(tm, t)

    def body(dh_ref, w_ref, o_ref, out_ref):
        do = lax.dot_general(dh_ref[...], w_ref[...], _DIMS["nt"], preferred_element_type=F32).astype(BF16)
        lane = lax.broadcasted_iota(jnp.int32, (tm, LANES), 1)
        for hd in range(A_HEADS):
            sl = slice(hd * LANES, (hd + 1) * LANES)
            d = do[:, sl]
            delta = jnp.sum(jnp.where(lane < A_V, d.astype(F32) * o_ref[:, sl].astype(F32), 0.0), axis=1, keepdims=True)
            hi, lo = _hi_lo(-delta)
            out_ref[:, sl] = jnp.where(lane == A_BIAS_LANE_V, hi, jnp.where(lane == A_BIAS_LANE_V + 1, lo, d))

    return pl.pallas_call(
        body, name=name, grid=(t // tm,), in_specs=[_rows(tm, dh.shape[1]), _whole(w_out_a), _rows(tm, w)],
        out_specs=_rows(tm, w), out_shape=jax.ShapeDtypeStruct((t, w), BF16), compiler_params=_cparams("parallel"),
    )(dh, w_out_a, o)


def _flash_bwd(qb, k, v, doe, name):
    t = qb.shape[0]
    tq, tk = A_TQ, A_TK
    assert tq == tk
    nh = A_HEADS_PER_STEP
    wblk = nh * LANES
    nq = t // tq

    def body(q_ref, k_ref, v_ref, do_ref, dq_ref, dk_ref, dv_ref):
        kb = pl.program_id(1)

        @pl.when(kb == 0)
        def _():
            dq_ref[...] = jnp.zeros_like(dq_ref)

        kpos = kb * tk + lax.broadcasted_iota(jnp.int32, (tk, tq), 0)
        qofs = lax.broadcasted_iota(jnp.int32, (tk, tq), 1)
        ks = [k_ref[:, hh * LANES:(hh + 1) * LANES] for hh in range(nh)]
        vs = [v_ref[:, hh * LANES:(hh + 1) * LANES] for hh in range(nh)]

        def step(qi, carry, masked):
            rows = pl.ds(pl.multiple_of(qi * tq, tq), tq)
            new = []
            for hh in range(nh):
                dk_acc, dv_acc = carry[hh]
                qh = q_ref[rows, hh * LANES:(hh + 1) * LANES]
                doh = do_ref[rows, hh * LANES:(hh + 1) * LANES]
                p_t = jnp.exp(lax.dot_general(ks[hh], qh, _DIMS["nt"], preferred_element_type=F32))
                if masked:
                    p_t = jnp.where(qi * tq + qofs >= kpos, p_t, 0.0)
                ds_t = (p_t * lax.dot_general(vs[hh], doh, _DIMS["nt"], preferred_element_type=F32)).astype(BF16)
                dv_acc = dv_acc + jnp.dot(p_t.astype(BF16), doh, preferred_element_type=F32)
                dk_acc = dk_acc + jnp.dot(ds_t, qh, preferred_element_type=F32)
                dq_ref[rows, hh * LANES:(hh + 1) * LANES] += lax.dot_general(ds_t, ks[hh], _DIMS["tn"],
                                                                             preferred_element_type=F32)
                new.append((dk_acc, dv_acc))
            return tuple(new)

        carry = tuple((jnp.zeros((tk, LANES), F32), jnp.zeros((tk, LANES), F32)) for _ in range(nh))
        carry = step(kb, carry, True)
        carry = lax.fori_loop(kb + 1, nq, functools.partial(step, masked=False), carry)
        for hh in range(nh):
            dk_ref[:, hh * LANES:(hh + 1) * LANES] = carry[hh][0].astype(dk_ref.dtype)
            dv_ref[:, hh * LANES:(hh + 1) * LANES] = carry[hh][1].astype(dv_ref.dtype)

    res = pl.BlockSpec((t, wblk), lambda j, kb: (0, j))
    blk = pl.BlockSpec((tk, wblk), lambda j, kb: (kb, j))
    wide = (t, A_HEADS * LANES)
    return pl.pallas_call(
        body, name=name, grid=(A_HEADS // nh, t // tk), in_specs=[res, blk, blk, res], out_specs=[res, blk, blk],
        out_shape=[jax.ShapeDtypeStruct(wide, F32), jax.ShapeDtypeStruct(wide, BF16), jax.ShapeDtypeStruct(wide, BF16)],
        compiler_params=_cparams("parallel", "arbitrary"),
    )(qb, k, v, doe)


_SPLITS = (M_WIDTH, M_WIDTH, M_WIDTH, M_WIDTH, M_HEADS, M_HEADS, Q_RANK, KV_RANK, A_ROPE)
_OFFS = tuple(sum(_SPLITS[:i]) for i in range(len(_SPLITS) + 1))
_GATE_BLOCK_KR = A_NOPE


def _split_w_in_t(wt):
    z = lambda n: jnp.zeros((n, wt.shape[1]), wt.dtype)
    w_g = jnp.concatenate([wt[_OFFS[4]:_OFFS[6]], z(_GATE_BLOCK_KR - 2 * M_HEADS), wt[_OFFS[8]:_OFFS[9]],
                           z(LANES - _GATE_BLOCK_KR - A_ROPE)], axis=0)
    return wt[_OFFS[6]:_OFFS[8]], w_g


def _merge_w_in_grad_t(g_m4, g_a2, g_g):
    return jnp.concatenate([g_m4, g_g[:2 * M_HEADS], g_a2, g_g[_GATE_BLOCK_KR:_GATE_BLOCK_KR + A_ROPE]], axis=0)


def _pad_heads(w, used):
    w3 = w.reshape(A_HEADS, used, w.shape[1])
    return jnp.pad(w3, ((0, 0), (0, LANES - used), (0, 0))).reshape(A_HEADS * LANES, w.shape[1])


def _unpad_heads(g, used):
    return g.reshape(A_HEADS, LANES, g.shape[1])[:, :used].reshape(A_HEADS * used, g.shape[1])


def _split_w_kv_b_t(wt):
    w3 = wt.reshape(A_HEADS, A_NOPE + A_V, wt.shape[1])
    pad = lambda part: jnp.pad(part, ((0, 0), (0, LANES - part.shape[1]), (0, 0))).reshape(A_HEADS * LANES, wt.shape[1])
    return pad(w3[:, :A_NOPE]), pad(w3[:, A_NOPE:])


def _merge_w_kv_b_grad_t(gk, gv):
    gk3 = gk.reshape(A_HEADS, LANES, gk.shape[1])[:, :A_NOPE]
    gv3 = gv.reshape(A_HEADS, LANES, gv.shape[1])[:, :A_V]
    return jnp.concatenate([gk3, gv3], axis=1).reshape(A_HEADS * (A_NOPE + A_V), gk.shape[1])


def _local_step(x, pos, target, nw, w, start_late_reduce):
    w_in = w["w_in"]
    w_a2, w_g = _split_w_in_t(w_in)
    w_qb = _pad_heads(w["w_q_b"], A_NOPE + A_ROPE)
    w_k, w_v = _split_w_kv_b_t(w["w_kv_b"])
    w_out_m, w_out_a = w["w_out"][:M_WIDTH], _pad_heads(w["w_out"][M_WIDTH:], A_V)
    w_gate, w_up, w_down = w["w_gate"], w["w_up"], w["w_down"]
    bias = jnp.pad(nw["b_gates"], ((0, 0), (0, LANES - 2 * M_HEADS)))
    mnorm = nw["mlstm_norm_w"].reshape(M_HEADS, 1, M_HEAD_DIM)
    n_m4 = 4 * M_WIDTH

    u1 = _rmsnorm_fwd(x, nw["attn_norm_w"], "attn_norm")
    m4 = _matmul(u1, w_in, "nt", BF16, "proj_mlstm", b_rows=n_m4, lead_scale=(M_WIDTH, M_SCALE))
    a2, qa_n, kv_n = _proj_latents_norm(u1, w_a2, nw["q_a_norm_w"], nw["kv_a_norm_w"], "proj_latents")
    gates = _matmul(u1, w_g, "nt", F32, "proj_gates")
    qk_t = _chunk_major_t(_matmul(w_in, u1, "nt", BF16, "proj_qk_t", a_rows=2 * M_WIDTH,
                                  lead_row_scale=(M_WIDTH, M_SCALE)))
    hm, c_prev_all, nm_all = _mlstm_fwd(m4, qk_t, gates, bias, mnorm, "mlstm_fwd")
    qa, kva = a2[:, :Q_RANK], a2[:, Q_RANK:]
    qr, kr, vv = _qkv_up_rope(qa_n, kv_n, w_qb, w_k, w_v, gates, pos, "qkv_up_rope")
    ha, qb = _flash_fwd(qr, kr, vv, "attn_fwd")
    h1, u2 = _out_proj_norm(hm, ha, w_out_m, w_out_a, x, nw["ffn_norm_w"], "out_proj")
    gg, uu, act = _ffn_gate_up(u2, w_gate, w_up, "ffn_gate_up")

    dh2, dh2_b, loss, g_final = _ffn_down_loss(act, w_down, h1, target, nw["final_norm_w"].reshape(1, D_MODEL),
                                               "ffn_down_loss")
    grads = {"final_norm_w": g_final.reshape(D_MODEL)}
    grads_b = {}
    dgg, duu = _ffn_down_bwd(dh2_b, w_down, gg, uu, "ffn_down_bwd")
    grads["w_down"], grads_b["w_down"] = _matmul(act, dh2_b, "tn", F32, "g_w_down", tm=256, tn=D_MODEL, also_bf16=True)
    dh1, dh1_b, grads["ffn_norm_w"] = _matmul_norm_bwd([dgg, duu], [w_gate, w_up], h1, nw["ffn_norm_w"], "d_u2", F32,
                                                       residual=dh2, also_bf16=True, tm=256)
    grads["w_gate"], grads_b["w_gate"] = _matmul(dgg, u2, "tn", F32, "g_w_gate", tm=256, tn=D_MODEL, also_bf16=True)
    grads["w_up"], grads_b["w_up"] = _matmul(duu, u2, "tn", F32, "g_w_up", tm=256, tn=D_MODEL, also_bf16=True)
    dhm = _matmul(dh1_b, w_out_m, "nt", BF16, "d_hm")
    doe = _d_attn_out(dh1_b, w_out_a, ha, "d_ha")
    grads["w_out"] = jnp.concatenate([_matmul(hm, dh1_b, "tn", F32, "g_w_out_m", tn=D_MODEL),
                                      _unpad_heads(_matmul(ha, dh1_b, "tn", F32, "g_w_out_a", tn=D_MODEL), A_V)], axis=0)

    late, doe = start_late_reduce(grads, grads_b, doe)

    dqr, dkr, dvv = _flash_bwd(qb, kr, vv, doe, "attn_bwd")
    dm4, dgates, small = _mlstm_bwd(m4, qk_t, gates, bias, mnorm, c_prev_all, nm_all, dhm, "mlstm_bwd")
    grads["mlstm_norm_w"] = small[:M_HEADS].reshape(1, M_HEADS, M_HEAD_DIM)
    grads["b_gates"] = small[M_HEADS:M_HEADS + 1, :2 * M_HEADS]
    dqpre, dgk = _rope_bwd(dqr, dkr, dgates, pos, "rope_bwd")
    grads["w_q_b"] = _unpad_heads(_matmul(dqpre, qa_n, "tn", F32, "g_w_q_b"), A_NOPE + A_ROPE)
    grads["w_kv_b"] = _merge_w_kv_b_grad_t(_matmul(dkr, kv_n, "tn", F32, "g_w_k"),
                                           _matmul(dvv, kv_n, "tn", F32, "g_w_v"))
    dqa, grads["q_a_norm_w"] = _matmul_norm_bwd([dqpre], [w_qb], qa, nw["q_a_norm_w"], "d_qa", BF16)
    dkva, grads["kv_a_norm_w"] = _matmul_norm_bwd([dkr, dvv], [w_k, w_v], kva, nw["kv_a_norm_w"], "d_kva", BF16)
    da2 = jnp.concatenate([dqa, dkva], axis=1)
    grads["w_in"] = _merge_w_in_grad_t(_matmul(dm4, u1, "tn", F32, "g_w_m4", tn=D_MODEL),
                                       _matmul(da2, u1, "tn", F32, "g_w_a2", tm=640, tn=D_MODEL),
                                       _matmul(dgk, u1, "tn", F32, "g_w_g", tn=D_MODEL))
    grad_x, grads["attn_norm_w"] = _matmul_norm_bwd([dm4, da2, dgk], [w_in, w_a2, w_g], x, nw["attn_norm_w"], "d_u1",
                                                    F32, residual=dh1, w_rows=[n_m4, None, None])
    return loss, grad_x, grads, grads_b, late


MESH = pl.DeviceIdType.MESH
N_CHIPS = 4
EARLY = ("w_in", "w_q_b", "w_kv_b")
LATE = ("w_out", "w_gate", "w_up", "w_down")
BIG = EARLY + LATE
TRANSPOSED = ("w_in", "w_q_b", "w_kv_b", "w_gate", "w_up")
LANE_HALVED = ("w_in", "w_out", "w_gate", "w_up", "w_down")
SMALL = ("attn_norm_w", "b_gates", "mlstm_norm_w", "q_a_norm_w", "kv_a_norm_w", "ffn_norm_w", "final_norm_w")
HBM_SPEC = pl.BlockSpec(memory_space=pltpu.HBM)


def _stored(name, a):
    return a[0].T if name in TRANSPOSED else a[0]


def _unstored(name, a):
    return (a.T if name in TRANSPOSED else a)[None]


def _half_shape(name, shape):
    rs, cs = shape
    return (rs, cs // 2) if name in LANE_HALVED else (rs // 2, cs)


def _half(ref, name, h, *lead):
    rs, cs = ref.shape[-2:]
    if name in LANE_HALVED:
        return ref.at[(*lead, slice(None), pl.ds(h * (cs // 2), cs // 2))]
    return ref.at[(*lead, pl.ds(h * (rs // 2), rs // 2), slice(None))]


def _place():
    x, y, c = lax.axis_index("x"), lax.axis_index("y"), lax.axis_index("c")
    others = [(1 - x, y), (x, 1 - y), (1 - x, 1 - y)]
    return x, y, c, others


def _gather_weights(names, shards):
    n = len(shards)

    def body(*refs):
        _gather_body(names, refs[:n], refs[n:2 * n], *refs[2 * n:])

    gathered = pl.pallas_call(
        body, name="gather_weights", in_specs=[HBM_SPEC] * n, out_specs=[HBM_SPEC] * n,
        out_shape=[jax.ShapeDtypeStruct((N_CHIPS,) + s.shape, s.dtype) for s in shards],
        scratch_shapes=[pltpu.SemaphoreType.DMA((6 * n,)), pltpu.SemaphoreType.DMA((6 * n,))],
    )(*shards)
    return _with_own_slab(gathered, shards)


def _with_own_slab(gathered, shards):
    me = 2 * lax.axis_index("x") + lax.axis_index("y")
    return [lax.dynamic_update_slice(g, s[None], (me, 0, 0)) for g, s in zip(gathered, shards)]


def _gather_body(names, ins, outs, send_sems, recv_sems):
    x, y, c, others = _place()
    me = 2 * x + y
    sibling = (x, y, 1 - c)

    def copy(w, k, slab, core, to, src=None):
        dst = _half(outs[w], names[w], core, slab)
        return pltpu.make_async_remote_copy(
            src_ref=dst if src is None else src, dst_ref=dst, send_sem=send_sems.at[w * 6 + k],
            recv_sem=recv_sems.at[w * 6 + k], device_id=to, device_id_type=MESH)

    sends = []
    for w in range(len(names)):
        for j, chip in enumerate(others):
            cp = copy(w, j, me, c, (*chip, c), src=_half(ins[w], names[w], c))
            cp.start()
            sends.append(cp)
    for w in range(len(names)):
        for j, (ox, oy) in enumerate(others):
            slab = 2 * ox + oy
            copy(w, j, slab, c, (x, y, c)).wait_recv()
            fwd = copy(w, 3 + j, slab, c, sibling)
            fwd.start()
            sends.append(fwd)
    for w in range(len(names)):
        for j, (ox, oy) in enumerate(others):
            copy(w, 3 + j, 2 * ox + oy, 1 - c, (x, y, c)).wait_recv()
    for cp in sends:
        cp.wait_send()


GATHER_LATE_COLLECTIVE_ID = 1


def _gather_weights_async(names, shards):
    n = len(shards)
    src = [jax.new_ref(s, memory_space=pltpu.MemorySpace.HBM) for s in shards]
    dst = [jax.empty_ref(jax.ShapeDtypeStruct((N_CHIPS,) + s.shape, s.dtype), memory_space=pltpu.MemorySpace.HBM)
           for s in shards]

    @pl.kernel(mesh=plsc.ScalarSubcoreMesh(axis_name="sequencer", num_cores=1), name="gather_weights_async",
               scratch_types=(pltpu.SemaphoreType.DMA((6 * n,)), pltpu.SemaphoreType.DMA((6 * n,))),
               compiler_params=pltpu.CompilerParams(collective_id=GATHER_LATE_COLLECTIVE_ID))
    def launch(send_sems, recv_sems):
        x, y, c, others = _place()
        peers = [(ox, oy, c) for ox, oy in others] + [(x, y, 1 - c)]
        barrier = pltpu.get_barrier_semaphore()
        for peer in peers:
            pl.semaphore_signal(barrier, inc=1, device_id=peer, device_id_type=MESH)
        pl.semaphore_wait(barrier, len(peers))
        _gather_body(names, src, dst, send_sems, recv_sems)

    launch()
    return _with_own_slab([d[...] for d in dst], shards)


def _exchange(arrays, out_shapes, plan, copies_per_array, name):
    n = len(arrays)

    def body(*refs):
        ins, outs = refs[:n], refs[n:2 * n]
        send_sems, recv_sems = refs[2 * n:]
        rem = [pltpu.make_async_remote_copy(src_ref=s, dst_ref=d, send_sem=send_sems.at[k], recv_sem=recv_sems.at[k],
                                            device_id=to, device_id_type=MESH)
               for k, (s, d, to) in enumerate(plan(ins, outs, _place()))]
        for cp in rem:
            cp.start()
        for cp in rem:
            cp.wait_recv()
        for cp in rem:
            cp.wait_send()

    return pl.pallas_call(
        body, name=name, in_specs=[HBM_SPEC] * n, out_specs=[HBM_SPEC] * n, out_shape=out_shapes,
        scratch_shapes=[pltpu.SemaphoreType.DMA((copies_per_array * n,)),
                        pltpu.SemaphoreType.DMA((copies_per_array * n,))],
    )(*arrays)


def _plan_to_sibling(ins, outs, place):
    x, y, c, _ = place
    return [(ins[w], outs[w], (x, y, 1 - c)) for w in range(len(ins))]


N_PEERS = 7


def _shard_sum_direct(g3, got7, name, where):
    hs = tuple(got7.shape[1:])
    if name in LANE_HALVED:
        g_in, g_spec = g3, pl.BlockSpec((1,) + hs, lambda i, p: (p[0], 0, p[1]))
    else:
        g_in, g_spec = g3.reshape((N_CHIPS, 2) + hs), pl.BlockSpec((1, 1) + hs, lambda i, p: (p[0], p[1], 0, 0))

    def body(p_ref, g_ref, r_ref, o_ref):
        acc = g_ref[(0,) * (len(g_ref.shape) - 2)]
        for k in range(N_PEERS):
            acc = acc + r_ref[k].astype(F32)
        o_ref[...] = acc

    return pl.pallas_call(
        body, name="shard_sum_%s" % name,
        grid_spec=pltpu.PrefetchScalarGridSpec(
            num_scalar_prefetch=1, grid=(1,),
            in_specs=[g_spec, pl.BlockSpec((N_PEERS,) + hs, lambda i, p: (0, 0, 0))],
            out_specs=pl.BlockSpec(hs, lambda i, p: (0, 0))),
        out_shape=jax.ShapeDtypeStruct(hs, F32), compiler_params=_cparams("arbitrary"),
    )(where, g_in, got7)


def _adamw_halves(w, m, v, mine, other, name, core):
    hs = tuple(mine.shape)
    full = pl.BlockSpec(hs, (lambda h, c_ref: (0, h)) if name in LANE_HALVED else (lambda h, c_ref: (h, 0)))
    half = pl.BlockSpec(hs, lambda h, c_ref: (0, 0))

    def body(c_ref, w_ref, m_ref, v_ref, a_ref, b_ref, g_ref, d_ref, mo_ref, vo_ref):
        g = jnp.where(pl.program_id(0) == c_ref[0], a_ref[...], b_ref[...])
        delta, m_new, v_new = _adamw_math(w_ref[...], g, m_ref[...], v_ref[...])
        g_ref[...] = g
        d_ref[...] = delta
        mo_ref[...] = m_new
        vo_ref[...] = v_new

    out = jax.ShapeDtypeStruct(w.shape, F32)
    return pl.pallas_call(
        body, name="adamw_%s" % name,
        grid_spec=pltpu.PrefetchScalarGridSpec(num_scalar_prefetch=1, grid=(2,), in_specs=[full, full, full, half, half],
                                               out_specs=[full] * 4),
        out_shape=[out] * 4, compiler_params=_cparams("parallel"),
    )(core, w, m, v, mine, other)


def _adamw_math(w, g, m, v):
    m = ADAM_B1 * m + (1.0 - ADAM_B1) * g
    v = ADAM_B2 * v + (1.0 - ADAM_B2) * (g * g)
    m_hat = m / (1.0 - ADAM_B1 ** ADAM_STEP)
    v_hat = v / (1.0 - ADAM_B2 ** ADAM_STEP)
    delta = -ADAM_LR * (m_hat / (jnp.sqrt(v_hat) + ADAM_EPS) + ADAM_WD * w)
    return delta, m, v


def _core_index():
    return lax.axis_index("c").astype(jnp.int32).reshape(1)


DIRECT_REDUCE_COLLECTIVE_ID = {"late": 2, "early": 3}


def _exchange_contributions_async(names, gb3, tag):
    n = len(gb3)
    src = [jax.new_ref(g, memory_space=pltpu.MemorySpace.HBM) for g in gb3]
    dst = [jax.empty_ref(jax.ShapeDtypeStruct((N_PEERS,) + _half_shape(nm, g.shape[1:]), g.dtype),
                         memory_space=pltpu.MemorySpace.HBM) for nm, g in zip(names, gb3)]

    @pl.kernel(mesh=plsc.ScalarSubcoreMesh(axis_name="sequencer", num_cores=1), name="contributions_async_" + tag,
               scratch_types=(pltpu.SemaphoreType.DMA((N_PEERS * n,)), pltpu.SemaphoreType.DMA((N_PEERS * n,))),
               compiler_params=pltpu.CompilerParams(collective_id=DIRECT_REDUCE_COLLECTIVE_ID[tag]))
    def launch(send_sems, recv_sems):
        x, y, c, _ = _place()
        peers = [(x ^ ((k >> 2) & 1), y ^ ((k >> 1) & 1), c ^ (k & 1)) for k in range(1, N_PEERS + 1)]
        barrier = pltpu.get_barrier_semaphore()
        for peer in peers:
            pl.semaphore_signal(barrier, inc=1, device_id=peer, device_id_type=MESH)
        pl.semaphore_wait(barrier, N_PEERS)
        rem = []
        for w in range(n):
            for k, (px, py, pc) in enumerate(peers):
                rem.append(pltpu.make_async_remote_copy(
                    src_ref=_half(src[w], names[w], pc, 2 * px + py), dst_ref=dst[w].at[k],
                    send_sem=send_sems.at[N_PEERS * w + k], recv_sem=recv_sems.at[N_PEERS * w + k],
                    device_id=(px, py, pc), device_id_type=MESH))
        for cp in rem:
            cp.start()
        for cp in rem:
            cp.wait_recv()
        for cp in rem:
            cp.wait_send()

    launch()
    return [d[...] for d in dst]


def _shard_index():
    return (2 * lax.axis_index("x") + lax.axis_index("y")).astype(jnp.int32).reshape(1)


def _with_sibling_half(names, mine, tag):
    other = _exchange(mine, [jax.ShapeDtypeStruct(m.shape, F32) for m in mine], _plan_to_sibling, 1,
                      "sibling_result_" + tag)
    return {n: (a, b) for n, a, b in zip(names, mine, other)}


SMALL_ROWS = 8
SMALL_LAYOUT = {"attn_norm_w": (0, 0, 1024), "ffn_norm_w": (1, 0, 1024), "final_norm_w": (2, 0, 1024),
                "q_a_norm_w": (3, 0, 384), "kv_a_norm_w": (3, 384, 256), "mlstm_norm_w": (4, 0, 512),
                "b_gates": (4, 512, 8)}
LOSS_SLOT = (5, 0)


def _small_allreduce_adamw(grads, loss, wts, mom, vel):
    shapes = {n: wts[n].shape for n in SMALL}
    flat = lambda d: [d[n].reshape(1, SMALL_LAYOUT[n][2]) for n in SMALL]
    ns = len(SMALL)

    def body(*refs):
        g_in, loss_ref = refs[:ns], refs[ns]
        w_in_, m_in, v_in = refs[ns + 1:2 * ns + 1], refs[2 * ns + 1:3 * ns + 1], refs[3 * ns + 1:4 * ns + 1]
        outs = refs[4 * ns + 1:8 * ns + 2]
        tile, slots, send_sems, recv_sems = refs[8 * ns + 2:]
        x, y, c, _ = _place()
        me = 4 * x + 2 * y + c
        tile[...] = jnp.zeros_like(tile)
        for n, g_ref in zip(SMALL, g_in):
            r, c0, width = SMALL_LAYOUT[n]
            tile[r:r + 1, c0:c0 + width] = g_ref[...]
        tile[LOSS_SLOT[0]:LOSS_SLOT[0] + 1, 0:LANES] = loss_ref[...]
        slots[me] = tile[...]
        copies = []
        for k in range(1, N_PEERS + 1):
            to = (x ^ ((k >> 2) & 1), y ^ ((k >> 1) & 1), c ^ (k & 1))
            cp = pltpu.make_async_remote_copy(src_ref=tile, dst_ref=slots.at[me], send_sem=send_sems.at[k - 1],
                                              recv_sem=recv_sems.at[k - 1], device_id=to, device_id_type=MESH)
            cp.start()
            copies.append(cp)
        for k in range(1, N_PEERS + 1):
            pltpu.make_async_remote_copy(src_ref=tile, dst_ref=slots.at[me ^ k], send_sem=send_sems.at[k - 1],
                                         recv_sem=recv_sems.at[k - 1], device_id=(x, y, c),
                                         device_id_type=MESH).wait_recv()
        for cp in copies:
            cp.wait_send()
        total = slots[0]
        for d in range(1, N_PEERS + 1):
            total = total + slots[d]
        for i, n in enumerate(SMALL):
            r, c0, width = SMALL_LAYOUT[n]
            g = total[r:r + 1, c0:c0 + width]
            delta, m_new, v_new = _adamw_math(w_in_[i][...], g, m_in[i][...], v_in[i][...])
            outs[i][...] = g
            outs[ns + i][...] = delta
            outs[2 * ns + i][...] = m_new
            outs[3 * ns + i][...] = v_new
        outs[4 * ns][...] = total[LOSS_SLOT[0]:LOSS_SLOT[0] + 1, 0:LANES]

    vm = pl.BlockSpec(memory_space=pltpu.VMEM)
    vec = [jax.ShapeDtypeStruct((1, SMALL_LAYOUT[n][2]), F32) for n in SMALL]
    res = pl.pallas_call(
        body, name="small_allreduce_adamw", in_specs=[vm] * (4 * ns + 1), out_specs=[vm] * (4 * ns + 1),
        out_shape=vec * 4 + [jax.ShapeDtypeStruct((1, LANES), F32)],
        scratch_shapes=[pltpu.VMEM((SMALL_ROWS, D_MODEL), F32), pltpu.VMEM((N_PEERS + 1, SMALL_ROWS, D_MODEL), F32),
                        pltpu.SemaphoreType.DMA((N_PEERS,)), pltpu.SemaphoreType.DMA((N_PEERS,))],
    )(*flat(grads), loss, *flat(wts), *flat(mom), *flat(vel))
    groups = [{n: r.reshape(shapes[n]) for n, r in zip(SMALL, res[i * ns:(i + 1) * ns])} for i in range(4)]
    return (*groups, res[4 * ns][0, 0])


def kernel(x, positions, attn_norm_w, w_in, b_gates, mlstm_norm_w, q_a_norm_w, w_q_b, kv_a_norm_w, w_kv_b, w_out, ffn_norm_w, w_gate, w_up, w_down, final_norm_w, loss_target, m_attn_norm_w, m_w_in, m_b_gates, m_mlstm_norm_w, m_q_a_norm_w, m_w_q_b, m_kv_a_norm_w, m_w_kv_b, m_w_out, m_ffn_norm_w, m_w_gate, m_w_up, m_w_down, m_final_norm_w, v_attn_norm_w, v_w_in, v_b_gates, v_mlstm_norm_w, v_q_a_norm_w, v_w_q_b, v_kv_a_norm_w, v_w_kv_b, v_w_out, v_ffn_norm_w, v_w_gate, v_w_up, v_w_down, v_final_norm_w):
    names = ("attn_norm_w", "w_in", "b_gates", "mlstm_norm_w", "q_a_norm_w", "w_q_b", "kv_a_norm_w", "w_kv_b", "w_out",
             "ffn_norm_w", "w_gate", "w_up", "w_down", "final_norm_w")
    wts = dict(zip(names, (attn_norm_w, w_in, b_gates, mlstm_norm_w, q_a_norm_w, w_q_b, kv_a_norm_w, w_kv_b, w_out,
                           ffn_norm_w, w_gate, w_up, w_down, final_norm_w)))
    mom = dict(zip(names, (m_attn_norm_w, m_w_in, m_b_gates, m_mlstm_norm_w, m_q_a_norm_w, m_w_q_b, m_kv_a_norm_w,
                           m_w_kv_b, m_w_out, m_ffn_norm_w, m_w_gate, m_w_up, m_w_down, m_final_norm_w)))
    vel = dict(zip(names, (v_attn_norm_w, v_w_in, v_b_gates, v_mlstm_norm_w, v_q_a_norm_w, v_w_q_b, v_kv_a_norm_w,
                           v_w_kv_b, v_w_out, v_ffn_norm_w, v_w_gate, v_w_up, v_w_down, v_final_norm_w)))
    t = x.shape[1]

    shards = {n: _stored(n, wts[n]).astype(BF16) for n in BIG}
    first, later = ("w_in",), tuple(n for n in BIG if n != "w_in")
    now = _gather_weights(first, [shards[n] for n in first])
    now, later_in = lax.optimization_barrier((now, [shards[n] for n in later]))
    behind = _gather_weights_async(later, later_in)
    full = {n: g.reshape(N_CHIPS * g.shape[1], g.shape[2]) for n, g in zip(first + later, list(now) + list(behind))}

    nw = {n: wts[n] for n in SMALL}
    by_shard = lambda g: g.reshape(N_CHIPS, g.shape[0] // N_CHIPS, g.shape[1])

    def start_reduce(group, tag, grads, grads_b, marker):
        g3 = [by_shard(grads[n]) for n in group]
        gb3 = [by_shard(grads_b[n]) if n in grads_b else g.astype(BF16) for n, g in zip(group, g3)]
        gb3, marker = lax.optimization_barrier((gb3, marker))
        return (g3, _exchange_contributions_async(group, gb3, tag)), marker

    loss, grad_x, grads, grads_b, (late_g3, late_got7) = _local_step(
        x[0], positions.reshape(t, 1), loss_target[0], nw, full, functools.partial(start_reduce, LATE, "late"))

    (early_g3, early_got7), late_got7 = start_reduce(EARLY, "early", grads, grads_b, list(late_got7))
    where = jnp.concatenate([_shard_index(), _core_index()])
    outs_g, outs_d, outs_m, outs_v = {}, {}, {}, {}

    def finish(group, g3, got7, tag):
        mine = [_shard_sum_direct(g, r7, n, where) for n, g, r7 in zip(group, g3, got7)]
        halves = _with_sibling_half(group, mine, tag)
        for n in group:
            res = _adamw_halves(_stored(n, wts[n]), _stored(n, mom[n]), _stored(n, vel[n]), *halves[n], n, where[1:])
            outs_g[n], outs_d[n], outs_m[n], outs_v[n] = [_unstored(n, r) for r in res]

    finish(LATE, late_g3, late_got7, "late")
    small_g, small_d, small_m, small_v, total_loss = _small_allreduce_adamw(grads, loss, wts, mom, vel)
    finish(EARLY, early_g3, early_got7, "early")
    outs_g.update(small_g)
    outs_d.update(small_d)
    outs_m.update(small_m)
    outs_v.update(small_v)
    return (total_loss, grad_x[None], *[outs_g[n] for n in names], *[outs_d[n] for n in names],
            *[outs_m[n] for n in names], *[outs_v[n] for n in names])
```

```python
import functools
import math

import jax
import jax.numpy as jnp
from jax import lax
from jax.experimental import pallas as pl
from jax.experimental.pallas import tpu as pltpu
from jax.experimental.pallas import tpu_sc as plsc

F32 = jnp.float32
BF16 = jnp.bfloat16

D_MODEL = 1024
M_HEADS = 4
M_HEAD_DIM = 128
M_WIDTH = M_HEADS * M_HEAD_DIM
M_CHUNK = 64
A_HEADS = 8
A_NOPE = 64
A_ROPE = 32
A_V = 64
A_WIDTH = A_HEADS * A_V
A_QK_PAD = 128
Q_RANK = 384
KV_RANK = 256
ROPE_THETA = 10000.0
D_FF = 2816
D_IN = 2728
EPS = 1e-6
ATTN_SCALE = (A_NOPE + A_ROPE) ** -0.5
M_SCALE = M_HEAD_DIM ** -0.5

ADAM_LR = 0.001
ADAM_B1 = 0.9
ADAM_B2 = 0.999
ADAM_EPS = 1e-08
ADAM_WD = 0.01
ADAM_STEP = 10

VMEM_LIMIT_BYTES = 56 * 1024 * 1024
LANES = 128
NEG_INF = float("-inf")


def _cparams(*sem):
    return pltpu.CompilerParams(dimension_semantics=sem if sem else None, vmem_limit_bytes=VMEM_LIMIT_BYTES)


_DIMS = {"nn": (((1,), (0,)), ((), ())), "nt": (((1,), (1,)), ((), ())), "tn": (((0,), (0,)), ((), ()))}


def _matmul(a, b, mode, out_dtype, name, tm=None, tn=1024, residual=None, a_rows=None, b_rows=None, lead_scale=None,
            lead_row_scale=None, also_bf16=False, col_chunk=None):
    a_list = list(a) if isinstance(a, (list, tuple)) else [a]
    b_list = list(b) if isinstance(b, (list, tuple)) else [b]
    rows_list = list(b_rows) if isinstance(b_rows, (list, tuple)) else [b_rows] * len(b_list)
    assert len(a_list) == len(b_list) == len(rows_list)
    dims = _DIMS[mode]
    specs, m, n = [], None, None
    for aa, bb, rr in zip(a_list, b_list, rows_list):
        b_shape = bb.shape if rr is None else (rr, bb.shape[1])
        a_shape = aa.shape if a_rows is None else (a_rows, aa.shape[1])
        if mode == "nn":
            (m1, k), (k2, n1) = a_shape, b_shape
        elif mode == "nt":
            (m1, k), (n1, k2) = a_shape, b_shape
        else:
            (k, m1), (k2, n1) = aa.shape, b_shape
        assert k == k2 and (m is None or (m, n) == (m1, n1)), (aa.shape, bb.shape, mode)
        m, n = m1, n1
        specs.append(k)
    if tm is None:
        tm = 512 if mode == "tn" else 1024
    tm, tn = min(tm, m), min(tn, n)
    assert m % tm == 0 and n % tn == 0, (m, n, tm, tn)
    in_specs = []
    for k in specs:
        in_specs.append(pl.BlockSpec((k, tm), lambda i, j: (0, i)) if mode == "tn"
                        else pl.BlockSpec((tm, k), lambda i, j: (i, 0)))
        in_specs.append(pl.BlockSpec((tn, k), lambda i, j: (j, 0)) if mode == "nt"
                        else pl.BlockSpec((k, tn), lambda i, j: (0, j)))
    o_spec = pl.BlockSpec((tm, tn), lambda i, j: (i, j))
    n_pairs = len(specs)

    def body(*refs):
        acc = None
        for p in range(n_pairs):
            part = lax.dot_general(refs[2 * p][...].astype(BF16), refs[2 * p + 1][...].astype(BF16), dims,
                                   preferred_element_type=F32)
            acc = part if acc is None else acc + part
        if lead_scale is not None:
            col = pl.program_id(1) * tn + lax.broadcasted_iota(jnp.int32, (1, tn), 1)
            acc = acc * jnp.where(col < lead_scale[0], lead_scale[1], 1.0)
        if lead_row_scale is not None:
            row = pl.program_id(0) * tm + lax.broadcasted_iota(jnp.int32, (tm, 1), 0)
            acc = acc * jnp.where(row < lead_row_scale[0], lead_row_scale[1], 1.0)
        if residual is not None:
            acc = acc + refs[2 * n_pairs][...].astype(F32)
        outs = refs[2 * n_pairs + (residual is not None):]
        if col_chunk is not None:
            res = acc.astype(outs[0].dtype)
            for c in range(tn // col_chunk):
                outs[0][c] = res[:, c * col_chunk:(c + 1) * col_chunk]
            return
        outs[0][...] = acc.astype(outs[0].dtype)
        if also_bf16:
            outs[1][...] = acc.astype(BF16)

    ins = [x for pair in zip(a_list, b_list) for x in pair] + ([residual] if residual is not None else [])
    in_specs = in_specs + ([o_spec] if residual is not None else [])
    if col_chunk is not None:
        assert residual is None and not also_bf16 and tn % col_chunk == 0
        return pl.pallas_call(
            body, name=name, grid=(m // tm, n // tn), in_specs=in_specs,
            out_specs=pl.BlockSpec((tn // col_chunk, tm, col_chunk), lambda i, j: (j, i, 0)),
            out_shape=jax.ShapeDtypeStruct((n // col_chunk, m, col_chunk), out_dtype),
            compiler_params=_cparams("parallel", "parallel"),
        )(*ins)
    out_shape = jax.ShapeDtypeStruct((m, n), out_dtype)
    if also_bf16:
        return pl.pallas_call(
            body, name=name, grid=(m // tm, n // tn), in_specs=in_specs, out_specs=[o_spec, o_spec],
            out_shape=[out_shape, jax.ShapeDtypeStruct((m, n), BF16)],
            compiler_params=_cparams("parallel", "parallel"),
        )(*ins)
    return pl.pallas_call(
        body, name=name, grid=(m // tm, n // tn), in_specs=in_specs, out_specs=o_spec, out_shape=out_shape,
        compiler_params=_cparams("parallel", "parallel"),
    )(*ins)


def _rms(xf, w):
    return xf * lax.rsqrt(jnp.mean(xf * xf, axis=-1, keepdims=True) + EPS) * w


def _rms_bwd(dyf, xf, w):
    r = lax.rsqrt(jnp.mean(xf * xf, axis=-1, keepdims=True) + EPS)
    xh = xf * r
    dyh = dyf * w
    return r * (dyh - xh * jnp.mean(dyh * xh, axis=-1, keepdims=True)), dyf * xh


def _accumulate(ref, part):
    @pl.when(pl.program_id(0) == 0)
    def _():
        ref[...] = part

    @pl.when(pl.program_id(0) > 0)
    def _():
        ref[...] += part


def _rows(tm, width):
    return pl.BlockSpec((tm, width), lambda i: (i, 0))


def _whole(arr):
    return pl.BlockSpec(arr.shape, lambda i: (0,) * arr.ndim)


def _proj_latents_norm(u, w_a2, w_qn, w_kvn, name, tm=512):
    t = u.shape[0]
    tm = min(tm, t)

    def body(u_ref, w_ref, wq_ref, wkv_ref, a2_ref, qn_ref, kvn_ref):
        a2 = lax.dot_general(u_ref[...], w_ref[...], _DIMS["nt"], preferred_element_type=F32).astype(BF16)
        a2_ref[...] = a2
        qn_ref[...] = _rms(a2[:, :Q_RANK].astype(F32), wq_ref[...]).astype(BF16)
        kvn_ref[...] = _rms(a2[:, Q_RANK:].astype(F32), wkv_ref[...]).astype(BF16)

    return pl.pallas_call(
        body, name=name, grid=(t // tm,), in_specs=[_rows(tm, u.shape[1]), _whole(w_a2), _whole(w_qn), _whole(w_kvn)],
        out_specs=[_rows(tm, Q_RANK + KV_RANK), _rows(tm, Q_RANK), _rows(tm, KV_RANK)],
        out_shape=[jax.ShapeDtypeStruct((t, Q_RANK + KV_RANK), BF16), jax.ShapeDtypeStruct((t, Q_RANK), BF16),
                   jax.ShapeDtypeStruct((t, KV_RANK), BF16)],
        compiler_params=_cparams("parallel"),
    )(u, w_a2, w_qn, w_kvn)


def _out_proj_norm(hm, ha, w_m, w_a, x, w_norm, name, tm=512):
    t = x.shape[0]
    tm = min(tm, t)

    def body(hm_ref, ha_ref, wm_ref, wa_ref, x_ref, wn_ref, h_ref, u_ref):
        h = (jnp.dot(hm_ref[...], wm_ref[...], preferred_element_type=F32)
             + jnp.dot(ha_ref[...], wa_ref[...], preferred_element_type=F32) + x_ref[...])
        h_ref[...] = h
        u_ref[...] = _rms(h, wn_ref[...]).astype(BF16)

    d = x.shape[1]
    return pl.pallas_call(
        body, name=name, grid=(t // tm,),
        in_specs=[_rows(tm, hm.shape[1]), _rows(tm, ha.shape[1]), _whole(w_m), _whole(w_a), _rows(tm, d), _whole(w_norm)],
        out_specs=[_rows(tm, d), _rows(tm, d)],
        out_shape=[jax.ShapeDtypeStruct((t, d), F32), jax.ShapeDtypeStruct((t, d), BF16)],
        compiler_params=_cparams("parallel"),
    )(hm, ha, w_m, w_a, x, w_norm)


def _rmsnorm_fwd(x, w, name, tm=512):
    t, d = x.shape

    def body(x_ref, w_ref, o_ref):
        xf = x_ref[...].astype(F32)
        r = lax.rsqrt(jnp.mean(xf * xf, axis=-1, keepdims=True) + EPS)
        o_ref[...] = (xf * r * w_ref[...]).astype(o_ref.dtype)

    return pl.pallas_call(
        body, name=name, grid=(t // tm,),
        in_specs=[pl.BlockSpec((tm, d), lambda i: (i, 0)), pl.BlockSpec((1, d), lambda i: (0, 0))],
        out_specs=pl.BlockSpec((tm, d), lambda i: (i, 0)),
        out_shape=jax.ShapeDtypeStruct((t, d), BF16), compiler_params=_cparams("parallel"),
    )(x, w)


def _ffn_down_loss(act, w_down, h1, target, w, name, tm=512):
    t, d = h1.shape
    tm = min(tm, t)

    def body(a_ref, wd_ref, h_ref, t_ref, w_ref, dh_ref, dhb_ref, loss_ref, dw_ref):
        xf = jnp.dot(a_ref[...], wd_ref[...], preferred_element_type=F32) + h_ref[...]
        r = lax.rsqrt(jnp.mean(xf * xf, axis=-1, keepdims=True) + EPS)
        xh = xf * r
        err = xh * w_ref[...] - t_ref[...]
        part_loss = 0.5 * jnp.sum(jnp.sum(err * err, axis=-1, keepdims=True), axis=0, keepdims=True) * (1.0 / d)
        dy = err * (1.0 / d)
        dyh = dy * w_ref[...]
        dh = r * (dyh - xh * jnp.mean(dyh * xh, axis=-1, keepdims=True))
        dh_ref[...] = dh
        dhb_ref[...] = dh.astype(BF16)
        _accumulate(dw_ref, jnp.sum(dy * xh, axis=0, keepdims=True))
        _accumulate(loss_ref, jnp.broadcast_to(part_loss, (1, LANES)))

    vec = pl.BlockSpec((1, d), lambda i: (0, 0))
    return pl.pallas_call(
        body, name=name, grid=(t // tm,),
        in_specs=[_rows(tm, act.shape[1]), _whole(w_down), _rows(tm, d), _rows(tm, d), vec],
        out_specs=[_rows(tm, d), _rows(tm, d), pl.BlockSpec((1, LANES), lambda i: (0, 0)), vec],
        out_shape=[jax.ShapeDtypeStruct((t, d), F32), jax.ShapeDtypeStruct((t, d), BF16),
                   jax.ShapeDtypeStruct((1, LANES), F32), jax.ShapeDtypeStruct((1, d), F32)],
        compiler_params=_cparams("arbitrary"),
    )(act, w_down, h1, target, w)


def _matmul_norm_bwd(das, ws, x, w_norm, name, out_dtype, residual=None, also_bf16=False, w_rows=None, tm=512):
    t, d = x.shape
    tm = min(tm, t)
    n = len(das)

    def body(*refs):
        x_ref, wn_ref = refs[2 * n], refs[2 * n + 1]
        outs = refs[2 * n + 2 + (residual is not None):]
        dy = None
        for p in range(n):
            wp = refs[n + p][...] if w_rows is None or w_rows[p] is None else refs[n + p][0:w_rows[p], :]
            part = jnp.dot(refs[p][...], wp, preferred_element_type=F32)
            dy = part if dy is None else dy + part
        dx, dw_terms = _rms_bwd(dy, x_ref[...].astype(F32), wn_ref[...])
        if residual is not None:
            dx = dx + refs[2 * n + 2][...]
        outs[0][...] = dx.astype(outs[0].dtype)
        if also_bf16:
            outs[1][...] = dx.astype(BF16)
        _accumulate(outs[-1], jnp.sum(dw_terms, axis=0, keepdims=True))

    extra = [jax.ShapeDtypeStruct((t, d), BF16)] if also_bf16 else []
    return pl.pallas_call(
        body, name=name, grid=(t // tm,),
        in_specs=([_rows(tm, a.shape[1]) for a in das] + [_whole(w) for w in ws] + [_rows(tm, d), _whole(w_norm)]
                  + ([_rows(tm, d)] if residual is not None else [])),
        out_specs=[_rows(tm, d)] * (1 + len(extra)) + [pl.BlockSpec((1, d), lambda i: (0, 0))],
        out_shape=[jax.ShapeDtypeStruct((t, d), out_dtype)] + extra + [jax.ShapeDtypeStruct((1, d), F32)],
        compiler_params=_cparams("arbitrary"),
    )(*das, *ws, x, w_norm, *([residual] if residual is not None else []))


FFN_TN = 1408


def _ffn_gate_up(u, w_gate, w_up, name, tm=1024):
    t, d = u.shape
    n = w_gate.shape[0]
    tm = min(tm, t)

    def body(u_ref, wg_ref, wu_ref, g_ref, up_ref, act_ref):
        uu = u_ref[...]
        g = lax.dot_general(uu, wg_ref[...], _DIMS["nt"], preferred_element_type=F32)
        up = lax.dot_general(uu, wu_ref[...], _DIMS["nt"], preferred_element_type=F32)
        g_b, up_b = g.astype(BF16), up.astype(BF16)
        g_ref[...] = g_b
        up_ref[...] = up_b
        gf, uf = g_b.astype(F32), up_b.astype(F32)
        act_ref[...] = (gf * jax.nn.sigmoid(gf) * uf).astype(act_ref.dtype)

    w_spec = pl.BlockSpec((FFN_TN, d), lambda i, j: (j, 0))
    o_spec = pl.BlockSpec((tm, FFN_TN), lambda i, j: (i, j))
    out = jax.ShapeDtypeStruct((t, n), BF16)
    return pl.pallas_call(
        body, name=name, grid=(t // tm, n // FFN_TN),
        in_specs=[pl.BlockSpec((tm, d), lambda i, j: (i, 0)), w_spec, w_spec], out_specs=[o_spec] * 3,
        out_shape=[out] * 3, compiler_params=_cparams("parallel", "parallel"),
    )(u, w_gate, w_up)


def _ffn_down_bwd(dh, w_down, g, up, name, tm=1024):
    t, d = dh.shape
    n = w_down.shape[0]
    tm = min(tm, t)

    def body(dh_ref, w_ref, g_ref, up_ref, dg_ref, du_ref):
        df = lax.dot_general(dh_ref[...], w_ref[...], _DIMS["nt"], preferred_element_type=F32).astype(BF16).astype(F32)
        gf = g_ref[...].astype(F32)
        uf = up_ref[...].astype(F32)
        s = jax.nn.sigmoid(gf)
        dg_ref[...] = (df * uf * s * (1.0 + gf * (1.0 - s))).astype(dg_ref.dtype)
        du_ref[...] = (df * gf * s).astype(du_ref.dtype)

    o_spec = pl.BlockSpec((tm, FFN_TN), lambda i, j: (i, j))
    out = jax.ShapeDtypeStruct((t, n), BF16)
    return pl.pallas_call(
        body, name=name, grid=(t // tm, n // FFN_TN),
        in_specs=[pl.BlockSpec((tm, d), lambda i, j: (i, 0)), pl.BlockSpec((FFN_TN, d), lambda i, j: (j, 0)),
                  o_spec, o_spec],
        out_specs=[o_spec, o_spec], out_shape=[out, out], compiler_params=_cparams("parallel", "parallel"),
    )(dh, w_down, g, up)


M_BLOCK = 512
M_CHUNKS_PER_BLOCK = M_BLOCK // M_CHUNK
M_UNROLL = 2


def _log_sigmoid(z):
    return jnp.minimum(z, 0.0) - jnp.log(1.0 + jnp.exp(-jnp.abs(z)))


def _per_head(fn):
    return jnp.stack([fn(hd) for hd in range(M_HEADS)])


def _mlstm_chunk_fwd(q, k, v, gc, c_prev, n_prev, m_prev):
    L, H = M_CHUNK, M_HEADS
    tt = lax.broadcasted_iota(jnp.int32, (L, L), 0)
    ss = lax.broadcasted_iota(jnp.int32, (L, L), 1)
    eye = tt == ss
    causal = ss <= tt
    gct = jnp.transpose(gc)
    i_col = _per_head(lambda hd: gc[:, hd:hd + 1])
    f_col = _per_head(lambda hd: gc[:, H + hd:H + hd + 1])
    i_row = _per_head(lambda hd: gct[hd:hd + 1, :])
    lf_col = _log_sigmoid(f_col)
    lf_row = _log_sigmoid(_per_head(lambda hd: gct[H + hd:H + hd + 1, :]))
    b_col = jnp.sum(jnp.where(causal, lf_row, 0.0), axis=2, keepdims=True)
    b_row = jnp.sum(jnp.where(tt <= ss, lf_col, 0.0), axis=1, keepdims=True)
    g = jnp.sum(lf_col, axis=1, keepdims=True)
    a_row = g - b_row + i_row
    a_col = g - b_col + i_col
    m_loc = jnp.max(a_row, axis=2, keepdims=True)
    d_log = jnp.where(causal, b_col - b_row + i_row, NEG_INF)
    inter = b_col + m_prev
    m_t = jnp.maximum(jnp.max(d_log, axis=2, keepdims=True), inter)
    dmat = jnp.exp(d_log - m_t)
    amat = _per_head(lambda hd: lax.dot_general(q[hd], k[hd], _DIMS["nt"], preferred_element_type=F32))
    p = dmat * amat
    sig = jnp.exp(inter - m_t)
    c_prev_b = [c.astype(BF16) for c in c_prev]
    qc = _per_head(lambda hd: jnp.dot(q[hd], c_prev_b[hd], preferred_element_type=F32))
    p_b = p.astype(BF16)
    num = _per_head(lambda hd: jnp.dot(p_b[hd], v[hd], preferred_element_type=F32)) + sig * qc
    qf = _per_head(lambda hd: q[hd].astype(F32))
    qn = jnp.sum(qf * n_prev, axis=2, keepdims=True)
    den_raw = jnp.sum(p, axis=2, keepdims=True) + sig * qn
    floor = jnp.exp(-m_t)
    den = jnp.maximum(jnp.abs(den_raw), floor)
    h = num / den
    m_new = jnp.maximum(g + m_prev, m_loc)
    w_col = jnp.exp(a_col - m_new)
    w_row = jnp.exp(a_row - m_new)
    alpha = jnp.exp(g + m_prev - m_new)
    return dict(eye=eye, causal=causal, tt=tt, ss=ss, i_col=i_col, f_col=f_col, dmat=dmat, amat=amat, p=p, p_b=p_b,
                sig=sig, qc=qc, qf=qf, qn=qn, num=num, den_raw=den_raw, floor=floor, den=den, h=h, m_new=m_new,
                w_col=w_col, w_row=w_row, alpha=alpha, c_prev_b=c_prev_b, b_row=b_row, b_col=b_col, m_t=m_t)


def _mlstm_fwd(m4, kt, gates, bias, norm_w, name):
    t = m4.shape[0]
    nblk = t // M_BLOCK
    nc = t // M_CHUNK
    L, dh = M_CHUNK, M_HEAD_DIM

    def body(m4_ref, kt_ref, g_ref, b_ref, w_ref, hm_ref, cp_ref, nm_ref, c_s, n_s, m_s):
        @pl.when(pl.program_id(0) == 0)
        def _():
            c_s[...] = jnp.zeros_like(c_s)
            n_s[...] = jnp.zeros_like(n_s)
            m_s[...] = jnp.zeros_like(m_s)

        row8 = lax.broadcasted_iota(jnp.int32, (8, dh), 0)

        def chunk(c, carry):
            rows = pl.ds(pl.multiple_of(c * L, L), L)
            gc = g_ref[rows, :] + b_ref[...]
            col = lambda part, hd: m4_ref[rows, part * M_WIDTH + hd * dh:part * M_WIDTH + (hd + 1) * dh]
            q = [col(0, hd) for hd in range(M_HEADS)]
            k = [col(1, hd) for hd in range(M_HEADS)]
            v = [col(2, hd) for hd in range(M_HEADS)]
            o = _per_head(lambda hd: col(3, hd).astype(F32))
            c_prev = [c_s[hd] for hd in range(M_HEADS)]
            n_prev = n_s[:, 0:1, :]
            m_prev = m_s[:, 0:1, 0:1]
            r = _mlstm_chunk_fwd(q, k, v, gc, c_prev, n_prev, m_prev)
            h = r["h"]
            hn = h * lax.rsqrt(jnp.mean(h * h, axis=-1, keepdims=True) + EPS) * w_ref[...]
            hm = (hn * jax.nn.sigmoid(o)).astype(hm_ref.dtype)
            nm = jnp.where(row8 == 0, n_prev, jnp.where(row8 == 1, m_prev, 0.0))
            kf = _per_head(lambda hd: k[hd].astype(F32))
            n_s[:, 0:1, :] = r["alpha"] * n_prev + jnp.sum(kf * r["w_col"], axis=1, keepdims=True)
            m_s[...] = jnp.broadcast_to(r["m_new"], (M_HEADS, 8, dh))
            for hd in range(M_HEADS):
                hm_ref[rows, hd * dh:(hd + 1) * dh] = hm[hd]
                cp_ref[hd, c] = r["c_prev_b"][hd]
                nm_ref[hd, c] = nm[hd]
                kw_t = (kt_ref[c, hd * dh:(hd + 1) * dh, :].astype(F32) * r["w_row"][hd]).astype(BF16)
                c_s[hd] = r["alpha"][hd] * c_prev[hd] + jnp.dot(kw_t, v[hd], preferred_element_type=F32)
            return carry

        lax.fori_loop(0, M_CHUNKS_PER_BLOCK, chunk, 0, unroll=M_UNROLL)

    return pl.pallas_call(
        body, name=name, grid=(nblk,),
        in_specs=[pl.BlockSpec((M_BLOCK, 4 * M_WIDTH), lambda i: (i, 0)),
                  pl.BlockSpec((M_CHUNKS_PER_BLOCK, M_WIDTH, L), lambda i: (i, 1, 0)),
                  pl.BlockSpec((M_BLOCK, LANES), lambda i: (i, 0)),
                  pl.BlockSpec((1, LANES), lambda i: (0, 0)),
                  pl.BlockSpec((M_HEADS, 1, dh), lambda i: (0, 0, 0))],
        out_specs=[pl.BlockSpec((M_BLOCK, M_WIDTH), lambda i: (i, 0)),
                   pl.BlockSpec((M_HEADS, M_CHUNKS_PER_BLOCK, dh, dh), lambda i: (0, i, 0, 0)),
                   pl.BlockSpec((M_HEADS, M_CHUNKS_PER_BLOCK, 8, dh), lambda i: (0, i, 0, 0))],
        out_shape=[jax.ShapeDtypeStruct((t, M_WIDTH), BF16),
                   jax.ShapeDtypeStruct((M_HEADS, nc, dh, dh), BF16),
                   jax.ShapeDtypeStruct((M_HEADS, nc, 8, dh), F32)],
        scratch_shapes=[pltpu.VMEM((M_HEADS, dh, dh), F32), pltpu.VMEM((M_HEADS, 8, dh), F32),
                        pltpu.VMEM((M_HEADS, 8, dh), F32)],
        compiler_params=_cparams("arbitrary"),
    )(m4, kt, gates, bias, norm_w)


def _mlstm_bwd(m4, qt, gates, bias, norm_w, c_prev_all, nm_all, dhm, name):
    t = m4.shape[0]
    nblk = t // M_BLOCK
    L, dh = M_CHUNK, M_HEAD_DIM

    def body(m4_ref, qt_ref, g_ref, b_ref, w_ref, cp_ref, nm_ref, dhm_ref, dm4_ref, dg_ref, small_ref, dc_s, dn_s):
        @pl.when(pl.program_id(0) == 0)
        def _():
            dc_s[...] = jnp.zeros_like(dc_s)
            dn_s[...] = jnp.zeros_like(dn_s)
            small_ref[...] = jnp.zeros_like(small_ref)

        lane = lax.broadcasted_iota(jnp.int32, (L, LANES), 1)
        row8 = lax.broadcasted_iota(jnp.int32, (8, LANES), 0)

        def chunk(ci, carry):
            c = M_CHUNKS_PER_BLOCK - 1 - ci
            rows = pl.ds(pl.multiple_of(c * L, L), L)
            gc = g_ref[rows, :] + b_ref[...]
            heads = range(M_HEADS)
            col = lambda part, hd: m4_ref[rows, part * M_WIDTH + hd * dh:part * M_WIDTH + (hd + 1) * dh]
            q = [col(0, hd) for hd in heads]
            k = [col(1, hd) for hd in heads]
            v = [col(2, hd) for hd in heads]
            o = _per_head(lambda hd: col(3, hd).astype(F32))
            q_t = [qt_ref[c, hd * dh:(hd + 1) * dh, :] for hd in heads]
            c_prev = [cp_ref[hd, c].astype(F32) for hd in heads]
            nm = _per_head(lambda hd: nm_ref[hd, c])
            n_prev = nm[:, 0:1, :]
            m_prev = nm[:, 1:2, 0:1]
            r = _mlstm_chunk_fwd(q, k, v, gc, c_prev, n_prev, m_prev)
            eye, tt, ss = r["eye"], r["tt"], r["ss"]
            h, den, sig, p = r["h"], r["den"], r["sig"], r["p"]
            w_col, alpha, c_prev_b = r["w_col"], r["alpha"], r["c_prev_b"]
            to_row = lambda colv: jnp.sum(jnp.where(eye, colv, 0.0), axis=1, keepdims=True)
            to_col = lambda rowv: jnp.sum(jnp.where(eye, rowv, 0.0), axis=2, keepdims=True)
            mm = lambda fn: _per_head(lambda hd: fn(hd))

            w_h = w_ref[...]
            rn = lax.rsqrt(jnp.mean(h * h, axis=-1, keepdims=True) + EPS)
            hh = h * rn
            og = jax.nn.sigmoid(o)
            dhm_c = _per_head(lambda hd: dhm_ref[rows, hd * dh:(hd + 1) * dh].astype(F32))
            dhn = dhm_c * og
            d_o = dhm_c * hh * w_h * og * (1.0 - og)
            d_norm = jnp.sum(dhn * hh, axis=1, keepdims=True)
            dhh = dhn * w_h
            dh_ = rn * (dhh - hh * jnp.mean(dhh * hh, axis=-1, keepdims=True))

            dnum = dh_ / den
            dden = -jnp.sum(dh_ * h, axis=-1, keepdims=True) / den
            dden_raw = jnp.where(jnp.abs(r["den_raw"]) >= r["floor"], dden * jnp.sign(r["den_raw"]), 0.0)
            dnum_b = dnum.astype(BF16)
            dp = mm(lambda hd: lax.dot_general(dnum_b[hd], v[hd], _DIMS["nt"], preferred_element_type=F32)) + dden_raw
            dc = [dc_s[hd] for hd in heads]
            dn = dn_s[:, 0:1, :]
            dc_b = [d.astype(BF16) for d in dc]
            g2_b = (sig * dnum).astype(BF16)
            sd = sig * dden_raw
            da_mat = (dp * r["dmat"]).astype(BF16)
            dqs = mm(lambda hd: lax.dot_general(g2_b[hd], c_prev_b[hd], _DIMS["nt"], preferred_element_type=F32)
                     + jnp.dot(da_mat[hd], k[hd], preferred_element_type=F32)) + sd * n_prev
            r_mat = mm(lambda hd: lax.dot_general(v[hd], dc_b[hd], _DIMS["nt"], preferred_element_type=F32)) + dn
            kf = _per_head(lambda hd: k[hd].astype(F32))
            dmat_t = jnp.exp(jnp.where(tt <= ss, r["b_row"] - r["b_col"] + r["i_col"], NEG_INF) - to_row(r["m_t"]))
            p_t = (dmat_t * mm(lambda hd: jnp.dot(k[hd], q_t[hd], preferred_element_type=F32))).astype(BF16)
            dp_t = mm(lambda hd: lax.dot_general(v[hd], dnum_b[hd], _DIMS["nt"], preferred_element_type=F32))
            da_t = ((dp_t + to_row(dden_raw)) * dmat_t).astype(BF16)
            d_k = mm(lambda hd: jnp.dot(da_t[hd], q[hd], preferred_element_type=F32)) + w_col * r_mat
            d_v = (mm(lambda hd: jnp.dot(p_t[hd], dnum_b[hd], preferred_element_type=F32))
                   + w_col * mm(lambda hd: jnp.dot(k[hd], dc_b[hd], preferred_element_type=F32)))
            da_col = jnp.sum(kf * r_mat, axis=-1, keepdims=True) * w_col
            dsig = jnp.sum(dnum * r["qc"], axis=-1, keepdims=True) + dden_raw * r["qn"]
            pp = dp * p
            r1 = jnp.sum(pp, axis=2, keepdims=True)
            c1_col = to_col(jnp.sum(pp, axis=1, keepdims=True))
            dcc = _per_head(lambda hd: dc[hd] * c_prev[hd])
            d_alpha = (jnp.sum(jnp.sum(dcc, axis=2, keepdims=True), axis=1, keepdims=True)
                       + jnp.sum(dn * n_prev, axis=2, keepdims=True))
            dgl = jnp.sum(da_col, axis=1, keepdims=True) + d_alpha * alpha
            db_col = r1 + dsig * sig - c1_col - da_col
            dli_col = c1_col + da_col
            dlf_col = jnp.sum(jnp.where(ss >= tt, to_row(db_col), 0.0), axis=2, keepdims=True) + dgl
            df_col = dlf_col * jax.nn.sigmoid(-r["f_col"])

            dn_s[:, 0:1, :] = alpha * dn + jnp.sum(sd * r["qf"], axis=1, keepdims=True)
            dq_out = (dqs * M_SCALE).astype(dm4_ref.dtype)
            dk_out, dv_out, do_out = d_k.astype(dm4_ref.dtype), d_v.astype(dm4_ref.dtype), d_o.astype(dm4_ref.dtype)
            dg_tile = jnp.zeros((L, LANES), F32)
            small = jnp.zeros((8, LANES), F32)
            for hd in heads:
                dc_s[hd] = alpha[hd] * dc[hd] + jnp.dot(q_t[hd], g2_b[hd], preferred_element_type=F32)
                dm4_ref[rows, hd * dh:(hd + 1) * dh] = dq_out[hd]
                dm4_ref[rows, M_WIDTH + hd * dh:M_WIDTH + (hd + 1) * dh] = dk_out[hd]
                dm4_ref[rows, 2 * M_WIDTH + hd * dh:2 * M_WIDTH + (hd + 1) * dh] = dv_out[hd]
                dm4_ref[rows, 3 * M_WIDTH + hd * dh:3 * M_WIDTH + (hd + 1) * dh] = do_out[hd]
                dg_tile = (dg_tile + jnp.where(lane == hd, dli_col[hd], 0.0)
                           + jnp.where(lane == M_HEADS + hd, df_col[hd], 0.0))
                small = small + jnp.where(row8 == hd, d_norm[hd], 0.0)
            dg_ref[rows, :] = dg_tile
            small = small + jnp.where(row8 == M_HEADS, jnp.sum(dg_tile, axis=0, keepdims=True), 0.0)
            small_ref[...] += small
            return carry

        lax.fori_loop(0, M_CHUNKS_PER_BLOCK, chunk, 0, unroll=M_UNROLL)

    rev = lambda i: nblk - 1 - i
    return pl.pallas_call(
        body, name=name, grid=(nblk,),
        in_specs=[pl.BlockSpec((M_BLOCK, 4 * M_WIDTH), lambda i: (rev(i), 0)),
                  pl.BlockSpec((M_CHUNKS_PER_BLOCK, M_WIDTH, L), lambda i: (rev(i), 0, 0)),
                  pl.BlockSpec((M_BLOCK, LANES), lambda i: (rev(i), 0)),
                  pl.BlockSpec((1, LANES), lambda i: (0, 0)),
                  pl.BlockSpec((M_HEADS, 1, dh), lambda i: (0, 0, 0)),
                  pl.BlockSpec((M_HEADS, M_CHUNKS_PER_BLOCK, dh, dh), lambda i: (0, rev(i), 0, 0)),
                  pl.BlockSpec((M_HEADS, M_CHUNKS_PER_BLOCK, 8, dh), lambda i: (0, rev(i), 0, 0)),
                  pl.BlockSpec((M_BLOCK, M_WIDTH), lambda i: (rev(i), 0))],
        out_specs=[pl.BlockSpec((M_BLOCK, 4 * M_WIDTH), lambda i: (rev(i), 0)),
                   pl.BlockSpec((M_BLOCK, LANES), lambda i: (rev(i), 0)),
                   pl.BlockSpec((8, LANES), lambda i: (0, 0))],
        out_shape=[jax.ShapeDtypeStruct((t, 4 * M_WIDTH), BF16), jax.ShapeDtypeStruct((t, LANES), F32),
                   jax.ShapeDtypeStruct((8, LANES), F32)],
        scratch_shapes=[pltpu.VMEM((M_HEADS, dh, dh), F32), pltpu.VMEM((M_HEADS, 8, dh), F32)],
        compiler_params=_cparams("arbitrary"),
    )(m4, qt, gates, bias, norm_w, c_prev_all, nm_all, dhm)


def _rope_tables(pos_col, width):
    lane = lax.broadcasted_iota(jnp.int32, (1, width), 1) % A_QK_PAD
    half = A_ROPE // 2
    first = (lane >= A_NOPE) & (lane < A_NOPE + half)
    second = (lane >= A_NOPE + half) & (lane < A_NOPE + A_ROPE)
    idx = jnp.where(first, lane - A_NOPE, lane - A_NOPE - half).astype(F32)
    inv_freq = jnp.exp(idx * (-math.log(ROPE_THETA) / half))
    ang = pos_col.astype(F32) * inv_freq
    cos, sin = jnp.cos(ang), jnp.sin(ang)
    rot = first | second
    return jnp.where(rot, cos, 1.0), jnp.where(first, -sin, 0.0), jnp.where(second, sin, 0.0)


def _rope_apply(vv, cosf, s1, s2):
    half = A_ROPE // 2
    w = vv.shape[-1]
    return vv * cosf + pltpu.roll(vv, w - half, 1) * s1 + pltpu.roll(vv, half, 1) * s2


def _rope_apply_t(dd, cosf, s1, s2):
    half = A_ROPE // 2
    w = dd.shape[-1]
    return dd * cosf + pltpu.roll(dd * s1, half, 1) + pltpu.roll(dd * s2, w - half, 1)


A_BIAS_LANE_K = A_NOPE + A_ROPE
A_BIAS_LANE_V = A_V


def _hi_lo(val):
    hi = val.astype(BF16)
    return hi, (val - hi.astype(F32)).astype(BF16)


def _qkv_up_rope(qa_n, kv_n, w_qb, w_k, w_v, gates, pos, name, tm=512):
    t = qa_n.shape[0]
    w = A_HEADS * LANES

    def body(qa_ref, kv_ref, wq_ref, wk_ref, wv_ref, g_ref, p_ref, qo_ref, ko_ref, vo_ref):
        cosf, s1, s2 = _rope_tables(p_ref[...], LANES)
        lane = lax.broadcasted_iota(jnp.int32, (1, LANES), 1)
        kr = jnp.where((lane >= A_NOPE) & (lane < A_NOPE + A_ROPE), g_ref[...], 0.0)
        kr = _rope_apply(kr, cosf, s1, s2)
        kr = jnp.where((lane == A_BIAS_LANE_K) | (lane == A_BIAS_LANE_K + 1), 1.0, kr)
        v_one = (lane == A_BIAS_LANE_V) | (lane == A_BIAS_LANE_V + 1)
        qpre = lax.dot_general(qa_ref[...], wq_ref[...], _DIMS["nt"], preferred_element_type=F32)
        kv = kv_ref[...]
        kpre = lax.dot_general(kv, wk_ref[...], _DIMS["nt"], preferred_element_type=F32)
        vpre = lax.dot_general(kv, wv_ref[...], _DIMS["nt"], preferred_element_type=F32)
        for hd in range(A_HEADS):
            sl = slice(hd * LANES, (hd + 1) * LANES)
            qo_ref[:, sl] = (_rope_apply(qpre[:, sl], cosf, s1, s2) * ATTN_SCALE).astype(qo_ref.dtype)
            ko_ref[:, sl] = (kpre[:, sl] + kr).astype(ko_ref.dtype)
            vo_ref[:, sl] = jnp.where(v_one, 1.0, vpre[:, sl]).astype(vo_ref.dtype)

    rows = lambda width: pl.BlockSpec((tm, width), lambda i: (i, 0))
    whole = lambda arr: pl.BlockSpec(arr.shape, lambda i: (0, 0))
    out = jax.ShapeDtypeStruct((t, w), BF16)
    return pl.pallas_call(
        body, name=name, grid=(t // tm,),
        in_specs=[rows(qa_n.shape[1]), rows(kv_n.shape[1]), whole(w_qb), whole(w_k), whole(w_v), rows(LANES), rows(1)],
        out_specs=[rows(w)] * 3, out_shape=[out] * 3, compiler_params=_cparams("parallel"),
    )(qa_n, kv_n, w_qb, w_k, w_v, gates, pos)


def _rope_bwd(dq, dk, dgates, pos, name, tm=512):
    t, w = dq.shape

    def body(dq_ref, dk_ref, dg_ref, p_ref, dqo_ref, dgo_ref):
        cosf, s1, s2 = _rope_tables(p_ref[...], LANES)
        acc = jnp.zeros((tm, LANES), F32)
        for hd in range(A_HEADS):
            sl = slice(hd * LANES, (hd + 1) * LANES)
            dqo_ref[:, sl] = (_rope_apply_t(dq_ref[:, sl], cosf, s1, s2) * ATTN_SCALE).astype(dqo_ref.dtype)
            acc = acc + dk_ref[:, sl].astype(F32)
        lane = lax.broadcasted_iota(jnp.int32, (1, LANES), 1)
        dkr = _rope_apply_t(acc, cosf, s1, s2)
        dkr = jnp.where((lane >= A_NOPE) & (lane < A_NOPE + A_ROPE), dkr, 0.0)
        dgo_ref[...] = (dg_ref[...] + dkr).astype(dgo_ref.dtype)

    row = pl.BlockSpec((tm, w), lambda i: (i, 0))
    nar = pl.BlockSpec((tm, LANES), lambda i: (i, 0))
    return pl.pallas_call(
        body, name=name, grid=(t // tm,),
        in_specs=[row, row, nar, pl.BlockSpec((tm, 1), lambda i: (i, 0))],
        out_specs=[row, nar],
        out_shape=[jax.ShapeDtypeStruct((t, w), BF16), jax.ShapeDtypeStruct((t, LANES), BF16)],
        compiler_params=_cparams("parallel"),
    )(dq, dk, dgates, pos)


A_TQ = 512
A_TK = 512


A_HEADS_PER_STEP = 2
A_HEADS_PER_STEP_FWD = 4


def _flash_fwd(q, k, v, name):
    t = q.shape[0]
    tq, tk = A_TQ, A_TK
    nh = A_HEADS_PER_STEP_FWD
    wblk = nh * LANES

    def body(q_ref, k_ref, v_ref, o_ref, qb_ref):
        i = pl.program_id(1)
        lane = lax.broadcasted_iota(jnp.int32, (tq, LANES), 1)
        qpos = i * tq + lax.broadcasted_iota(jnp.int32, (tq, tk), 0)
        kofs = lax.broadcasted_iota(jnp.int32, (tq, tk), 1)
        qs = [q_ref[:, hh * LANES:(hh + 1) * LANES] for hh in range(nh)]

        def step(kb, carry, masked):
            rows = pl.ds(pl.multiple_of(kb * tk, tk), tk)
            new = []
            for hh in range(nh):
                m, acc = carry[hh]
                kh = k_ref[rows, hh * LANES:(hh + 1) * LANES]
                s = lax.dot_general(qs[hh], kh, _DIMS["nt"], preferred_element_type=F32)
                if masked:
                    s = jnp.where(kb * tk + kofs <= qpos, s, NEG_INF)
                m_new = jnp.maximum(m, jnp.max(s, axis=1, keepdims=True))
                p = jnp.exp(s - m_new)
                acc = jnp.exp(m - m_new) * acc + jnp.dot(p.astype(BF16), v_ref[rows, hh * LANES:(hh + 1) * LANES],
                                                          preferred_element_type=F32)
                new.append((m_new, acc))
            return tuple(new)

        carry = tuple((jnp.full((tq, 1), NEG_INF, F32), jnp.zeros((tq, LANES), F32)) for _ in range(nh))
        n_full = (i * tq) // tk
        carry = lax.fori_loop(0, n_full, functools.partial(step, masked=False), carry)
        for d in range(tq // tk):
            carry = step(n_full + d, carry, True)
        for hh in range(nh):
            m, acc = carry[hh]
            l = acc[:, A_BIAS_LANE_V:A_BIAS_LANE_V + 1]
            o_ref[:, hh * LANES:(hh + 1) * LANES] = (acc / l).astype(o_ref.dtype)
            hi, lo = _hi_lo(-(m + jnp.log(l)))
            qb_ref[:, hh * LANES:(hh + 1) * LANES] = jnp.where(
                lane == A_BIAS_LANE_K, hi, jnp.where(lane == A_BIAS_LANE_K + 1, lo, qs[hh]))

    blk = pl.BlockSpec((tq, wblk), lambda j, i: (i, j))
    res = pl.BlockSpec((t, wblk), lambda j, i: (0, j))
    out = jax.ShapeDtypeStruct((t, A_HEADS * LANES), BF16)
    return pl.pallas_call(
        body, name=name, grid=(A_HEADS // nh, t // tq), in_specs=[blk, res, res], out_specs=[blk, blk],
        out_shape=[out, out], compiler_params=_cparams("parallel", "parallel"),
    )(q, k, v)


def _d_attn_out(dh, w_out_a, o, name, tm=512):
    t, w = o.shape
    tm = min(tm, t)

    def body(dh_ref, w_ref, o_ref, out_ref):
        do = lax.dot_general(dh_ref[...], w_ref[...], _DIMS["nt"], preferred_element_type=F32).astype(BF16)
        lane = lax.broadcasted_iota(jnp.int32, (tm, LANES), 1)
        for hd in range(A_HEADS):
            sl = slice(hd * LANES, (hd + 1) * LANES)
            d = do[:, sl]
            delta = jnp.sum(jnp.where(lane < A_V, d.astype(F32) * o_ref[:, sl].astype(F32), 0.0), axis=1, keepdims=True)
            hi, lo = _hi_lo(-delta)
            out_ref[:, sl] = jnp.where(lane == A_BIAS_LANE_V, hi, jnp.where(lane == A_BIAS_LANE_V + 1, lo, d))

    return pl.pallas_call(
        body, name=name, grid=(t // tm,), in_specs=[_rows(tm, dh.shape[1]), _whole(w_out_a), _rows(tm, w)],
        out_specs=_rows(tm, w), out_shape=jax.ShapeDtypeStruct((t, w), BF16), compiler_params=_cparams("parallel"),
    )(dh, w_out_a, o)


def _flash_bwd(qb, k, v, doe, name):
    t = qb.shape[0]
    tq, tk = A_TQ, A_TK
    assert tq == tk
    nh = A_HEADS_PER_STEP
    wblk = nh * LANES
    nq = t // tq

    def body(q_ref, k_ref, v_ref, do_ref, dq_ref, dk_ref, dv_ref):
        kb = pl.program_id(1)

        @pl.when(kb == 0)
        def _():
            dq_ref[...] = jnp.zeros_like(dq_ref)

        kpos = kb * tk + lax.broadcasted_iota(jnp.int32, (tk, tq), 0)
        qofs = lax.broadcasted_iota(jnp.int32, (tk, tq), 1)
        ks = [k_ref[:, hh * LANES:(hh + 1) * LANES] for hh in range(nh)]
        vs = [v_ref[:, hh * LANES:(hh + 1) * LANES] for hh in range(nh)]

        def step(qi, carry, masked):
            rows = pl.ds(pl.multiple_of(qi * tq, tq), tq)
            new = []
            for hh in range(nh):
                dk_acc, dv_acc = carry[hh]
                qh = q_ref[rows, hh * LANES:(hh + 1) * LANES]
                doh = do_ref[rows, hh * LANES:(hh + 1) * LANES]
                p_t = jnp.exp(lax.dot_general(ks[hh], qh, _DIMS["nt"], preferred_element_type=F32))
                if masked:
                    p_t = jnp.where(qi * tq + qofs >= kpos, p_t, 0.0)
                ds_t = (p_t * lax.dot_general(vs[hh], doh, _DIMS["nt"], preferred_element_type=F32)).astype(BF16)
                dv_acc = dv_acc + jnp.dot(p_t.astype(BF16), doh, preferred_element_type=F32)
                dk_acc = dk_acc + jnp.dot(ds_t, qh, preferred_element_type=F32)
                dq_ref[rows, hh * LANES:(hh + 1) * LANES] += lax.dot_general(ds_t, ks[hh], _DIMS["tn"],
                                                                             preferred_element_type=F32)
                new.append((dk_acc, dv_acc))
            return tuple(new)

        carry = tuple((jnp.zeros((tk, LANES), F32), jnp.zeros((tk, LANES), F32)) for _ in range(nh))
        carry = step(kb, carry, True)
        carry = lax.fori_loop(kb + 1, nq, functools.partial(step, masked=False), carry)
        for hh in range(nh):
            dk_ref[:, hh * LANES:(hh + 1) * LANES] = carry[hh][0].astype(dk_ref.dtype)
            dv_ref[:, hh * LANES:(hh + 1) * LANES] = carry[hh][1].astype(dv_ref.dtype)

    res = pl.BlockSpec((t, wblk), lambda j, kb: (0, j))
    blk = pl.BlockSpec((tk, wblk), lambda j, kb: (kb, j))
    wide = (t, A_HEADS * LANES)
    return pl.pallas_call(
        body, name=name, grid=(A_HEADS // nh, t // tk), in_specs=[res, blk, blk, res], out_specs=[res, blk, blk],
        out_shape=[jax.ShapeDtypeStruct(wide, F32), jax.ShapeDtypeStruct(wide, BF16), jax.ShapeDtypeStruct(wide, BF16)],
        compiler_params=_cparams("parallel", "arbitrary"),
    )(qb, k, v, doe)


_SPLITS = (M_WIDTH, M_WIDTH, M_WIDTH, M_WIDTH, M_HEADS, M_HEADS, Q_RANK, KV_RANK, A_ROPE)
_OFFS = tuple(sum(_SPLITS[:i]) for i in range(len(_SPLITS) + 1))
_GATE_BLOCK_KR = A_NOPE


def _split_w_in_t(wt):
    z = lambda n: jnp.zeros((n, wt.shape[1]), wt.dtype)
    w_g = jnp.concatenate([wt[_OFFS[4]:_OFFS[6]], z(_GATE_BLOCK_KR - 2 * M_HEADS), wt[_OFFS[8]:_OFFS[9]],
                           z(LANES - _GATE_BLOCK_KR - A_ROPE)], axis=0)
    return wt[_OFFS[6]:_OFFS[8]], w_g


def _merge_w_in_grad_t(g_m4, g_a2, g_g):
    return jnp.concatenate([g_m4, g_g[:2 * M_HEADS], g_a2, g_g[_GATE_BLOCK_KR:_GATE_BLOCK_KR + A_ROPE]], axis=0)


def _pad_heads(w, used):
    w3 = w.reshape(A_HEADS, used, w.shape[1])
    return jnp.pad(w3, ((0, 0), (0, LANES - used), (0, 0))).reshape(A_HEADS * LANES, w.shape[1])


def _unpad_heads(g, used):
    return g.reshape(A_HEADS, LANES, g.shape[1])[:, :used].reshape(A_HEADS * used, g.shape[1])


def _split_w_kv_b_t(wt):
    w3 = wt.reshape(A_HEADS, A_NOPE + A_V, wt.shape[1])
    pad = lambda part: jnp.pad(part, ((0, 0), (0, LANES - part.shape[1]), (0, 0))).reshape(A_HEADS * LANES, wt.shape[1])
    return pad(w3[:, :A_NOPE]), pad(w3[:, A_NOPE:])


def _merge_w_kv_b_grad_t(gk, gv):
    gk3 = gk.reshape(A_HEADS, LANES, gk.shape[1])[:, :A_NOPE]
    gv3 = gv.reshape(A_HEADS, LANES, gv.shape[1])[:, :A_V]
    return jnp.concatenate([gk3, gv3], axis=1).reshape(A_HEADS * (A_NOPE + A_V), gk.shape[1])


def _local_step(x, pos, target, nw, w, start_late_reduce):
    w_in = w["w_in"]
    w_a2, w_g = _split_w_in_t(w_in)
    w_qb = _pad_heads(w["w_q_b"], A_NOPE + A_ROPE)
    w_k, w_v = _split_w_kv_b_t(w["w_kv_b"])
    w_out_m, w_out_a = w["w_out"][:M_WIDTH], _pad_heads(w["w_out"][M_WIDTH:], A_V)
    w_gate, w_up, w_down = w["w_gate"], w["w_up"], w["w_down"]
    bias = jnp.pad(nw["b_gates"], ((0, 0), (0, LANES - 2 * M_HEADS)))
    mnorm = nw["mlstm_norm_w"].reshape(M_HEADS, 1, M_HEAD_DIM)
    n_m4 = 4 * M_WIDTH

    u1 = _rmsnorm_fwd(x, nw["attn_norm_w"], "attn_norm")
    m4 = _matmul(u1, w_in, "nt", BF16, "proj_mlstm", b_rows=n_m4, lead_scale=(M_WIDTH, M_SCALE))
    a2, qa_n, kv_n = _proj_latents_norm(u1, w_a2, nw["q_a_norm_w"], nw["kv_a_norm_w"], "proj_latents")
    gates = _matmul(u1, w_g, "nt", F32, "proj_gates")
    qk_t = _matmul(w_in, u1, "nt", BF16, "proj_qk_t", a_rows=2 * M_WIDTH, lead_row_scale=(M_WIDTH, M_SCALE),
                   col_chunk=M_CHUNK)
    hm, c_prev_all, nm_all = _mlstm_fwd(m4, qk_t, gates, bias, mnorm, "mlstm_fwd")
    qa, kva = a2[:, :Q_RANK], a2[:, Q_RANK:]
    qr, kr, vv = _qkv_up_rope(qa_n, kv_n, w_qb, w_k, w_v, gates, pos, "qkv_up_rope")
    ha, qb = _flash_fwd(qr, kr, vv, "attn_fwd")
    h1, u2 = _out_proj_norm(hm, ha, w_out_m, w_out_a, x, nw["ffn_norm_w"], "out_proj")
    gg, uu, act = _ffn_gate_up(u2, w_gate, w_up, "ffn_gate_up")

    dh2, dh2_b, loss, g_final = _ffn_down_loss(act, w_down, h1, target, nw["final_norm_w"].reshape(1, D_MODEL),
                                               "ffn_down_loss")
    grads = {"final_norm_w": g_final.reshape(D_MODEL)}
    grads_b = {}
    dgg, duu = _ffn_down_bwd(dh2_b, w_down, gg, uu, "ffn_down_bwd")
    grads["w_down"], grads_b["w_down"] = _matmul(act, dh2_b, "tn", F32, "g_w_down", tm=256, tn=D_MODEL, also_bf16=True)
    dh1, dh1_b, grads["ffn_norm_w"] = _matmul_norm_bwd([dgg, duu], [w_gate, w_up], h1, nw["ffn_norm_w"], "d_u2", F32,
                                                       residual=dh2, also_bf16=True, tm=256)
    grads["w_gate"], grads_b["w_gate"] = _matmul(dgg, u2, "tn", F32, "g_w_gate", tm=256, tn=D_MODEL, also_bf16=True)
    grads["w_up"], grads_b["w_up"] = _matmul(duu, u2, "tn", F32, "g_w_up", tm=256, tn=D_MODEL, also_bf16=True)
    dhm = _matmul(dh1_b, w_out_m, "nt", BF16, "d_hm")
    doe = _d_attn_out(dh1_b, w_out_a, ha, "d_ha")
    grads["w_out"] = jnp.concatenate([_matmul(hm, dh1_b, "tn", F32, "g_w_out_m", tn=D_MODEL),
                                      _unpad_heads(_matmul(ha, dh1_b, "tn", F32, "g_w_out_a", tn=D_MODEL), A_V)], axis=0)

    late, doe = start_late_reduce(grads, grads_b, doe)

    dqr, dkr, dvv = _flash_bwd(qb, kr, vv, doe, "attn_bwd")
    dm4, dgates, small = _mlstm_bwd(m4, qk_t, gates, bias, mnorm, c_prev_all, nm_all, dhm, "mlstm_bwd")
    grads["mlstm_norm_w"] = small[:M_HEADS].reshape(1, M_HEADS, M_HEAD_DIM)
    grads["b_gates"] = small[M_HEADS:M_HEADS + 1, :2 * M_HEADS]
    dqpre, dgk = _rope_bwd(dqr, dkr, dgates, pos, "rope_bwd")
    grads["w_q_b"] = _unpad_heads(_matmul(dqpre, qa_n, "tn", F32, "g_w_q_b"), A_NOPE + A_ROPE)
    grads["w_kv_b"] = _merge_w_kv_b_grad_t(_matmul(dkr, kv_n, "tn", F32, "g_w_k"),
                                           _matmul(dvv, kv_n, "tn", F32, "g_w_v"))
    dqa, grads["q_a_norm_w"] = _matmul_norm_bwd([dqpre], [w_qb], qa, nw["q_a_norm_w"], "d_qa", BF16)
    dkva, grads["kv_a_norm_w"] = _matmul_norm_bwd([dkr, dvv], [w_k, w_v], kva, nw["kv_a_norm_w"], "d_kva", BF16)
    da2 = jnp.concatenate([dqa, dkva], axis=1)
    grads["w_in"] = _merge_w_in_grad_t(_matmul(dm4, u1, "tn", F32, "g_w_m4", tn=D_MODEL),
                                       _matmul(da2, u1, "tn", F32, "g_w_a2", tm=640, tn=D_MODEL),
                                       _matmul(dgk, u1, "tn", F32, "g_w_g", tn=D_MODEL))
    grad_x, grads["attn_norm_w"] = _matmul_norm_bwd([dm4, da2, dgk], [w_in, w_a2, w_g], x, nw["attn_norm_w"], "d_u1",
                                                    F32, residual=dh1, w_rows=[n_m4, None, None])
    return loss, grad_x, grads, grads_b, late


MESH = pl.DeviceIdType.MESH
N_CHIPS = 4
EARLY = ("w_in", "w_q_b", "w_kv_b")
LATE = ("w_out", "w_gate", "w_up", "w_down")
BIG = EARLY + LATE
TRANSPOSED = ("w_in", "w_q_b", "w_kv_b", "w_gate", "w_up")
LANE_HALVED = ("w_in", "w_out", "w_gate", "w_up", "w_down")
SMALL = ("attn_norm_w", "b_gates", "mlstm_norm_w", "q_a_norm_w", "kv_a_norm_w", "ffn_norm_w", "final_norm_w")
HBM_SPEC = pl.BlockSpec(memory_space=pltpu.HBM)


def _stored(name, a):
    return a[0].T if name in TRANSPOSED else a[0]


def _unstored(name, a):
    return (a.T if name in TRANSPOSED else a)[None]


def _half_shape(name, shape):
    rs, cs = shape
    return (rs, cs // 2) if name in LANE_HALVED else (rs // 2, cs)


def _half(ref, name, h, *lead):
    rs, cs = ref.shape[-2:]
    if name in LANE_HALVED:
        return ref.at[(*lead, slice(None), pl.ds(h * (cs // 2), cs // 2))]
    return ref.at[(*lead, pl.ds(h * (rs // 2), rs // 2), slice(None))]


def _place():
    x, y, c = lax.axis_index("x"), lax.axis_index("y"), lax.axis_index("c")
    others = [(1 - x, y), (x, 1 - y), (1 - x, 1 - y)]
    return x, y, c, others


def _gather_weights(names, shards):
    n = len(shards)

    def body(*refs):
        _gather_body(names, refs[:n], refs[n:2 * n], *refs[2 * n:])

    gathered = pl.pallas_call(
        body, name="gather_weights", in_specs=[HBM_SPEC] * n, out_specs=[HBM_SPEC] * n,
        out_shape=[jax.ShapeDtypeStruct((N_CHIPS,) + s.shape, s.dtype) for s in shards],
        scratch_shapes=[pltpu.SemaphoreType.DMA((6 * n,)), pltpu.SemaphoreType.DMA((6 * n,))],
    )(*shards)
    return _with_own_slab(gathered, shards)


def _with_own_slab(gathered, shards):
    me = 2 * lax.axis_index("x") + lax.axis_index("y")
    return [lax.dynamic_update_slice(g, s[None], (me, 0, 0)) for g, s in zip(gathered, shards)]


def _gather_body(names, ins, outs, send_sems, recv_sems):
    x, y, c, others = _place()
    me = 2 * x + y
    sibling = (x, y, 1 - c)

    def copy(w, k, slab, core, to, src=None):
        dst = _half(outs[w], names[w], core, slab)
        return pltpu.make_async_remote_copy(
            src_ref=dst if src is None else src, dst_ref=dst, send_sem=send_sems.at[w * 6 + k],
            recv_sem=recv_sems.at[w * 6 + k], device_id=to, device_id_type=MESH)

    sends = []
    for w in range(len(names)):
        for j, chip in enumerate(others):
            cp = copy(w, j, me, c, (*chip, c), src=_half(ins[w], names[w], c))
            cp.start()
            sends.append(cp)
    for w in range(len(names)):
        for j, (ox, oy) in enumerate(others):
            slab = 2 * ox + oy
            copy(w, j, slab, c, (x, y, c)).wait_recv()
            fwd = copy(w, 3 + j, slab, c, sibling)
            fwd.start()
            sends.append(fwd)
    for w in range(len(names)):
        for j, (ox, oy) in enumerate(others):
            copy(w, 3 + j, 2 * ox + oy, 1 - c, (x, y, c)).wait_recv()
    for cp in sends:
        cp.wait_send()


GATHER_LATE_COLLECTIVE_ID = 1


def _gather_weights_async(names, shards):
    n = len(shards)
    src = [jax.new_ref(s, memory_space=pltpu.MemorySpace.HBM) for s in shards]
    dst = [jax.empty_ref(jax.ShapeDtypeStruct((N_CHIPS,) + s.shape, s.dtype), memory_space=pltpu.MemorySpace.HBM)
           for s in shards]

    @pl.kernel(mesh=plsc.ScalarSubcoreMesh(axis_name="sequencer", num_cores=1), name="gather_weights_async",
               scratch_types=(pltpu.SemaphoreType.DMA((6 * n,)), pltpu.SemaphoreType.DMA((6 * n,))),
               compiler_params=pltpu.CompilerParams(collective_id=GATHER_LATE_COLLECTIVE_ID))
    def launch(send_sems, recv_sems):
        x, y, c, others = _place()
        peers = [(ox, oy, c) for ox, oy in others] + [(x, y, 1 - c)]
        barrier = pltpu.get_barrier_semaphore()
        for peer in peers:
            pl.semaphore_signal(barrier, inc=1, device_id=peer, device_id_type=MESH)
        pl.semaphore_wait(barrier, len(peers))
        _gather_body(names, src, dst, send_sems, recv_sems)

    launch()
    return _with_own_slab([d[...] for d in dst], shards)


def _exchange(arrays, out_shapes, plan, copies_per_array, name):
    n = len(arrays)

    def body(*refs):
        ins, outs = refs[:n], refs[n:2 * n]
        send_sems, recv_sems = refs[2 * n:]
        rem = [pltpu.make_async_remote_copy(src_ref=s, dst_ref=d, send_sem=send_sems.at[k], recv_sem=recv_sems.at[k],
                                            device_id=to, device_id_type=MESH)
               for k, (s, d, to) in enumerate(plan(ins, outs, _place()))]
        for cp in rem:
            cp.start()
        for cp in rem:
            cp.wait_recv()
        for cp in rem:
            cp.wait_send()

    return pl.pallas_call(
        body, name=name, in_specs=[HBM_SPEC] * n, out_specs=[HBM_SPEC] * n, out_shape=out_shapes,
        scratch_shapes=[pltpu.SemaphoreType.DMA((copies_per_array * n,)),
                        pltpu.SemaphoreType.DMA((copies_per_array * n,))],
    )(*arrays)


def _plan_to_sibling(ins, outs, place):
    x, y, c, _ = place
    return [(ins[w], outs[w], (x, y, 1 - c)) for w in range(len(ins))]


N_PEERS = 7


def _shard_sum_direct(g3, got7, name, where):
    hs = tuple(got7.shape[1:])
    if name in LANE_HALVED:
        g_in, g_spec = g3, pl.BlockSpec((1,) + hs, lambda i, p: (p[0], 0, p[1]))
    else:
        g_in, g_spec = g3.reshape((N_CHIPS, 2) + hs), pl.BlockSpec((1, 1) + hs, lambda i, p: (p[0], p[1], 0, 0))

    def body(p_ref, g_ref, r_ref, o_ref):
        acc = g_ref[(0,) * (len(g_ref.shape) - 2)]
        for k in range(N_PEERS):
            acc = acc + r_ref[k].astype(F32)
        o_ref[...] = acc

    return pl.pallas_call(
        body, name="shard_sum_%s" % name,
        grid_spec=pltpu.PrefetchScalarGridSpec(
            num_scalar_prefetch=1, grid=(1,),
            in_specs=[g_spec, pl.BlockSpec((N_PEERS,) + hs, lambda i, p: (0, 0, 0))],
            out_specs=pl.BlockSpec(hs, lambda i, p: (0, 0))),
        out_shape=jax.ShapeDtypeStruct(hs, F32), compiler_params=_cparams("arbitrary"),
    )(where, g_in, got7)


def _adamw_halves(w, m, v, mine, other, name, core):
    hs = tuple(mine.shape)
    full = pl.BlockSpec(hs, (lambda h, c_ref: (0, h)) if name in LANE_HALVED else (lambda h, c_ref: (h, 0)))
    half = pl.BlockSpec(hs, lambda h, c_ref: (0, 0))

    def body(c_ref, w_ref, m_ref, v_ref, a_ref, b_ref, g_ref, d_ref, mo_ref, vo_ref):
        g = jnp.where(pl.program_id(0) == c_ref[0], a_ref[...], b_ref[...])
        delta, m_new, v_new = _adamw_math(w_ref[...], g, m_ref[...], v_ref[...])
        g_ref[...] = g
        d_ref[...] = delta
        mo_ref[...] = m_new
        vo_ref[...] = v_new

    out = jax.ShapeDtypeStruct(w.shape, F32)
    return pl.pallas_call(
        body, name="adamw_%s" % name,
        grid_spec=pltpu.PrefetchScalarGridSpec(num_scalar_prefetch=1, grid=(2,), in_specs=[full, full, full, half, half],
                                               out_specs=[full] * 4),
        out_shape=[out] * 4, compiler_params=_cparams("parallel"),
    )(core, w, m, v, mine, other)


def _adamw_math(w, g, m, v):
    m = ADAM_B1 * m + (1.0 - ADAM_B1) * g
    v = ADAM_B2 * v + (1.0 - ADAM_B2) * (g * g)
    m_hat = m / (1.0 - ADAM_B1 ** ADAM_STEP)
    v_hat = v / (1.0 - ADAM_B2 ** ADAM_STEP)
    delta = -ADAM_LR * (m_hat / (jnp.sqrt(v_hat) + ADAM_EPS) + ADAM_WD * w)
    return delta, m, v


def _core_index():
    return lax.axis_index("c").astype(jnp.int32).reshape(1)


DIRECT_REDUCE_COLLECTIVE_ID = {"late": 2, "early": 3}


def _exchange_contributions_async(names, gb3, tag):
    n = len(gb3)
    src = [jax.new_ref(g, memory_space=pltpu.MemorySpace.HBM) for g in gb3]
    dst = [jax.empty_ref(jax.ShapeDtypeStruct((N_PEERS,) + _half_shape(nm, g.shape[1:]), g.dtype),
                         memory_space=pltpu.MemorySpace.HBM) for nm, g in zip(names, gb3)]

    @pl.kernel(mesh=plsc.ScalarSubcoreMesh(axis_name="sequencer", num_cores=1), name="contributions_async_" + tag,
               scratch_types=(pltpu.SemaphoreType.DMA((N_PEERS * n,)), pltpu.SemaphoreType.DMA((N_PEERS * n,))),
               compiler_params=pltpu.CompilerParams(collective_id=DIRECT_REDUCE_COLLECTIVE_ID[tag]))
    def launch(send_sems, recv_sems):
        x, y, c, _ = _place()
        peers = [(x ^ ((k >> 2) & 1), y ^ ((k >> 1) & 1), c ^ (k & 1)) for k in range(1, N_PEERS + 1)]
        barrier = pltpu.get_barrier_semaphore()
        for peer in peers:
            pl.semaphore_signal(barrier, inc=1, device_id=peer, device_id_type=MESH)
        pl.semaphore_wait(barrier, N_PEERS)
        rem = []
        for w in range(n):
            for k, (px, py, pc) in enumerate(peers):
                rem.append(pltpu.make_async_remote_copy(
                    src_ref=_half(src[w], names[w], pc, 2 * px + py), dst_ref=dst[w].at[k],
                    send_sem=send_sems.at[N_PEERS * w + k], recv_sem=recv_sems.at[N_PEERS * w + k],
                    device_id=(px, py, pc), device_id_type=MESH))
        for cp in rem:
            cp.start()
        for cp in rem:
            cp.wait_recv()
        for cp in rem:
            cp.wait_send()

    launch()
    return [d[...] for d in dst]


def _shard_index():
    return (2 * lax.axis_index("x") + lax.axis_index("y")).astype(jnp.int32).reshape(1)


def _with_sibling_half(names, mine, tag):
    other = _exchange(mine, [jax.ShapeDtypeStruct(m.shape, F32) for m in mine], _plan_to_sibling, 1,
                      "sibling_result_" + tag)
    return {n: (a, b) for n, a, b in zip(names, mine, other)}


SMALL_ROWS = 8
SMALL_LAYOUT = {"attn_norm_w": (0, 0, 1024), "ffn_norm_w": (1, 0, 1024), "final_norm_w": (2, 0, 1024),
                "q_a_norm_w": (3, 0, 384), "kv_a_norm_w": (3, 384, 256), "mlstm_norm_w": (4, 0, 512),
                "b_gates": (4, 512, 8)}
LOSS_SLOT = (5, 0)


def _small_allreduce_adamw(grads, loss, wts, mom, vel):
    shapes = {n: wts[n].shape for n in SMALL}
    flat = lambda d: [d[n].reshape(1, SMALL_LAYOUT[n][2]) for n in SMALL]
    ns = len(SMALL)

    def body(*refs):
        g_in, loss_ref = refs[:ns], refs[ns]
        w_in_, m_in, v_in = refs[ns + 1:2 * ns + 1], refs[2 * ns + 1:3 * ns + 1], refs[3 * ns + 1:4 * ns + 1]
        outs = refs[4 * ns + 1:8 * ns + 2]
        tile, slots, send_sems, recv_sems = refs[8 * ns + 2:]
        x, y, c, _ = _place()
        me = 4 * x + 2 * y + c
        tile[...] = jnp.zeros_like(tile)
        for n, g_ref in zip(SMALL, g_in):
            r, c0, width = SMALL_LAYOUT[n]
            tile[r:r + 1, c0:c0 + width] = g_ref[...]
        tile[LOSS_SLOT[0]:LOSS_SLOT[0] + 1, 0:LANES] = loss_ref[...]
        slots[me] = tile[...]
        copies = []
        for k in range(1, N_PEERS + 1):
            to = (x ^ ((k >> 2) & 1), y ^ ((k >> 1) & 1), c ^ (k & 1))
            cp = pltpu.make_async_remote_copy(src_ref=tile, dst_ref=slots.at[me], send_sem=send_sems.at[k - 1],
                                              recv_sem=recv_sems.at[k - 1], device_id=to, device_id_type=MESH)
            cp.start()
            copies.append(cp)
        for k in range(1, N_PEERS + 1):
            pltpu.make_async_remote_copy(src_ref=tile, dst_ref=slots.at[me ^ k], send_sem=send_sems.at[k - 1],
                                         recv_sem=recv_sems.at[k - 1], device_id=(x, y, c),
                                         device_id_type=MESH).wait_recv()
        for cp in copies:
            cp.wait_send()
        total = slots[0]
        for d in range(1, N_PEERS + 1):
            total = total + slots[d]
        for i, n in enumerate(SMALL):
            r, c0, width = SMALL_LAYOUT[n]
            g = total[r:r + 1, c0:c0 + width]
            delta, m_new, v_new = _adamw_math(w_in_[i][...], g, m_in[i][...], v_in[i][...])
            outs[i][...] = g
            outs[ns + i][...] = delta
            outs[2 * ns + i][...] = m_new
            outs[3 * ns + i][...] = v_new
        outs[4 * ns][...] = total[LOSS_SLOT[0]:LOSS_SLOT[0] + 1, 0:LANES]

    vm = pl.BlockSpec(memory_space=pltpu.VMEM)
    vec = [jax.ShapeDtypeStruct((1, SMALL_LAYOUT[n][2]), F32) for n in SMALL]
    res = pl.pallas_call(
        body, name="small_allreduce_adamw", in_specs=[vm] * (4 * ns + 1), out_specs=[vm] * (4 * ns + 1),
        out_shape=vec * 4 + [jax.ShapeDtypeStruct((1, LANES), F32)],
        scratch_shapes=[pltpu.VMEM((SMALL_ROWS, D_MODEL), F32), pltpu.VMEM((N_PEERS + 1, SMALL_ROWS, D_MODEL), F32),
                        pltpu.SemaphoreType.DMA((N_PEERS,)), pltpu.SemaphoreType.DMA((N_PEERS,))],
    )(*flat(grads), loss, *flat(wts), *flat(mom), *flat(vel))
    groups = [{n: r.reshape(shapes[n]) for n, r in zip(SMALL, res[i * ns:(i + 1) * ns])} for i in range(4)]
    return (*groups, res[4 * ns][0, 0])


def kernel(x, positions, attn_norm_w, w_in, b_gates, mlstm_norm_w, q_a_norm_w, w_q_b, kv_a_norm_w, w_kv_b, w_out, ffn_norm_w, w_gate, w_up, w_down, final_norm_w, loss_target, m_attn_norm_w, m_w_in, m_b_gates, m_mlstm_norm_w, m_q_a_norm_w, m_w_q_b, m_kv_a_norm_w, m_w_kv_b, m_w_out, m_ffn_norm_w, m_w_gate, m_w_up, m_w_down, m_final_norm_w, v_attn_norm_w, v_w_in, v_b_gates, v_mlstm_norm_w, v_q_a_norm_w, v_w_q_b, v_kv_a_norm_w, v_w_kv_b, v_w_out, v_ffn_norm_w, v_w_gate, v_w_up, v_w_down, v_final_norm_w):
    names = ("attn_norm_w", "w_in", "b_gates", "mlstm_norm_w", "q_a_norm_w", "w_q_b", "kv_a_norm_w", "w_kv_b", "w_out",
             "ffn_norm_w", "w_gate", "w_up", "w_down", "final_norm_w")
    wts = dict(zip(names, (attn_norm_w, w_in, b_gates, mlstm_norm_w, q_a_norm_w, w_q_b, kv_a_norm_w, w_kv_b, w_out,
                           ffn_norm_w, w_gate, w_up, w_down, final_norm_w)))
    mom = dict(zip(names, (m_attn_norm_w, m_w_in, m_b_gates, m_mlstm_norm_w, m_q_a_norm_w, m_w_q_b, m_kv_a_norm_w,
                           m_w_kv_b, m_w_out, m_ffn_norm_w, m_w_gate, m_w_up, m_w_down, m_final_norm_w)))
    vel = dict(zip(names, (v_attn_norm_w, v_w_in, v_b_gates, v_mlstm_norm_w, v_q_a_norm_w, v_w_q_b, v_kv_a_norm_w,
                           v_w_kv_b, v_w_out, v_ffn_norm_w, v_w_gate, v_w_up, v_w_down, v_final_norm_w)))
    t = x.shape[1]

    shards = {n: _stored(n, wts[n]).astype(BF16) for n in BIG}
    first, later = ("w_in",), tuple(n for n in BIG if n != "w_in")
    now = _gather_weights(first, [shards[n] for n in first])
    now, later_in = lax.optimization_barrier((now, [shards[n] for n in later]))
    behind = _gather_weights_async(later, later_in)
    full = {n: g.reshape(N_CHIPS * g.shape[1], g.shape[2]) for n, g in zip(first + later, list(now) + list(behind))}

    nw = {n: wts[n] for n in SMALL}
    by_shard = lambda g: g.reshape(N_CHIPS, g.shape[0] // N_CHIPS, g.shape[1])

    def start_reduce(group, tag, grads, grads_b, marker):
        g3 = [by_shard(grads[n]) for n in group]
        gb3 = [by_shard(grads_b[n]) if n in grads_b else g.astype(BF16) for n, g in zip(group, g3)]
        gb3, marker = lax.optimization_barrier((gb3, marker))
        return (g3, _exchange_contributions_async(group, gb3, tag)), marker

    loss, grad_x, grads, grads_b, (late_g3, late_got7) = _local_step(
        x[0], positions.reshape(t, 1), loss_target[0], nw, full, functools.partial(start_reduce, LATE, "late"))

    (early_g3, early_got7), late_got7 = start_reduce(EARLY, "early", grads, grads_b, list(late_got7))
    where = jnp.concatenate([_shard_index(), _core_index()])
    outs_g, outs_d, outs_m, outs_v = {}, {}, {}, {}

    def finish(group, g3, got7, tag):
        mine = [_shard_sum_direct(g, r7, n, where) for n, g, r7 in zip(group, g3, got7)]
        halves = _with_sibling_half(group, mine, tag)
        for n in group:
            res = _adamw_halves(_stored(n, wts[n]), _stored(n, mom[n]), _stored(n, vel[n]), *halves[n], n, where[1:])
            outs_g[n], outs_d[n], outs_m[n], outs_v[n] = [_unstored(n, r) for r in res]

    finish(LATE, late_g3, late_got7, "late")
    small_g, small_d, small_m, small_v, total_loss = _small_allreduce_adamw(grads, loss, wts, mom, vel)
    finish(EARLY, early_g3, early_got7, "early")
    outs_g.update(small_g)
    outs_d.update(small_d)
    outs_m.update(small_m)
    outs_v.update(small_v)
    return (total_loss, grad_x[None], *[outs_g[n] for n in names], *[outs_d[n] for n in names],
            *[outs_m[n] for n in names], *[outs_v[n] for n in names])
```

```python
import functools
import math

import jax
import jax.numpy as jnp
from jax import lax
from jax.experimental import pallas as pl
from jax.experimental.pallas import tpu as pltpu
from jax.experimental.pallas import tpu_sc as plsc

F32 = jnp.float32
BF16 = jnp.bfloat16

D_MODEL = 1024
M_HEADS = 4
M_HEAD_DIM = 128
M_WIDTH = M_HEADS * M_HEAD_DIM
M_CHUNK = 64
A_HEADS = 8
A_NOPE = 64
A_ROPE = 32
A_V = 64
A_WIDTH = A_HEADS * A_V
A_QK_PAD = 128
Q_RANK = 384
KV_RANK = 256
ROPE_THETA = 10000.0
D_FF = 2816
D_IN = 2728
EPS = 1e-6
ATTN_SCALE = (A_NOPE + A_ROPE) ** -0.5
M_SCALE = M_HEAD_DIM ** -0.5

ADAM_LR = 0.001
ADAM_B1 = 0.9
ADAM_B2 = 0.999
ADAM_EPS = 1e-08
ADAM_WD = 0.01
ADAM_STEP = 10

VMEM_LIMIT_BYTES = 56 * 1024 * 1024
LANES = 128
NEG_INF = float("-inf")


def _cparams(*sem):
    return pltpu.CompilerParams(dimension_semantics=sem if sem else None, vmem_limit_bytes=VMEM_LIMIT_BYTES)


_DIMS = {"nn": (((1,), (0,)), ((), ())), "nt": (((1,), (1,)), ((), ())), "tn": (((0,), (0,)), ((), ()))}


def _matmul(a, b, mode, out_dtype, name, tm=None, tn=1024, residual=None, a_rows=None, b_rows=None, lead_scale=None,
            lead_row_scale=None, also_bf16=False, col_chunk=None):
    a_list = list(a) if isinstance(a, (list, tuple)) else [a]
    b_list = list(b) if isinstance(b, (list, tuple)) else [b]
    rows_list = list(b_rows) if isinstance(b_rows, (list, tuple)) else [b_rows] * len(b_list)
    assert len(a_list) == len(b_list) == len(rows_list)
    dims = _DIMS[mode]
    specs, m, n = [], None, None
    for aa, bb, rr in zip(a_list, b_list, rows_list):
        b_shape = bb.shape if rr is None else (rr, bb.shape[1])
        a_shape = aa.shape if a_rows is None else (a_rows, aa.shape[1])
        if mode == "nn":
            (m1, k), (k2, n1) = a_shape, b_shape
        elif mode == "nt":
            (m1, k), (n1, k2) = a_shape, b_shape
        else:
            (k, m1), (k2, n1) = aa.shape, b_shape
        assert k == k2 and (m is None or (m, n) == (m1, n1)), (aa.shape, bb.shape, mode)
        m, n = m1, n1
        specs.append(k)
    if tm is None:
        tm = 512 if mode == "tn" else 1024
    tm, tn = min(tm, m), min(tn, n)
    assert m % tm == 0 and n % tn == 0, (m, n, tm, tn)
    in_specs = []
    for k in specs:
        in_specs.append(pl.BlockSpec((k, tm), lambda i, j: (0, i)) if mode == "tn"
                        else pl.BlockSpec((tm, k), lambda i, j: (i, 0)))
        in_specs.append(pl.BlockSpec((tn, k), lambda i, j: (j, 0)) if mode == "nt"
                        else pl.BlockSpec((k, tn), lambda i, j: (0, j)))
    o_spec = pl.BlockSpec((tm, tn), lambda i, j: (i, j))
    n_pairs = len(specs)

    def body(*refs):
        acc = None
        for p in range(n_pairs):
            part = lax.dot_general(refs[2 * p][...].astype(BF16), refs[2 * p + 1][...].astype(BF16), dims,
                                   preferred_element_type=F32)
            acc = part if acc is None else acc + part
        if lead_scale is not None:
            col = pl.program_id(1) * tn + lax.broadcasted_iota(jnp.int32, (1, tn), 1)
            acc = acc * jnp.where(col < lead_scale[0], lead_scale[1], 1.0)
        if lead_row_scale is not None:
            row = pl.program_id(0) * tm + lax.broadcasted_iota(jnp.int32, (tm, 1), 0)
            acc = acc * jnp.where(row < lead_row_scale[0], lead_row_scale[1], 1.0)
        if residual is not None:
            acc = acc + refs[2 * n_pairs][...].astype(F32)
        outs = refs[2 * n_pairs + (residual is not None):]
        if col_chunk is not None:
            res = acc.astype(outs[0].dtype)
            for c in range(tn // col_chunk):
                outs[0][c] = res[:, c * col_chunk:(c + 1) * col_chunk]
            return
        outs[0][...] = acc.astype(outs[0].dtype)
        if also_bf16:
            outs[1][...] = acc.astype(BF16)

    ins = [x for pair in zip(a_list, b_list) for x in pair] + ([residual] if residual is not None else [])
    in_specs = in_specs + ([o_spec] if residual is not None else [])
    if col_chunk is not None:
        assert residual is None and not also_bf16 and tn % col_chunk == 0
        return pl.pallas_call(
            body, name=name, grid=(m // tm, n // tn), in_specs=in_specs,
            out_specs=pl.BlockSpec((tn // col_chunk, tm, col_chunk), lambda i, j: (j, i, 0)),
            out_shape=jax.ShapeDtypeStruct((n // col_chunk, m, col_chunk), out_dtype),
            compiler_params=_cparams("parallel", "parallel"),
        )(*ins)
    out_shape = jax.ShapeDtypeStruct((m, n), out_dtype)
    if also_bf16:
        return pl.pallas_call(
            body, name=name, grid=(m // tm, n // tn), in_specs=in_specs, out_specs=[o_spec, o_spec],
            out_shape=[out_shape, jax.ShapeDtypeStruct((m, n), BF16)],
            compiler_params=_cparams("parallel", "parallel"),
        )(*ins)
    return pl.pallas_call(
        body, name=name, grid=(m // tm, n // tn), in_specs=in_specs, out_specs=o_spec, out_shape=out_shape,
        compiler_params=_cparams("parallel", "parallel"),
    )(*ins)


def _rms(xf, w):
    return xf * lax.rsqrt(jnp.mean(xf * xf, axis=-1, keepdims=True) + EPS) * w


def _rms_bwd(dyf, xf, w):
    r = lax.rsqrt(jnp.mean(xf * xf, axis=-1, keepdims=True) + EPS)
    xh = xf * r
    dyh = dyf * w
    return r * (dyh - xh * jnp.mean(dyh * xh, axis=-1, keepdims=True)), dyf * xh


def _accumulate(ref, part):
    @pl.when(pl.program_id(0) == 0)
    def _():
        ref[...] = part

    @pl.when(pl.program_id(0) > 0)
    def _():
        ref[...] += part


def _rows(tm, width):
    return pl.BlockSpec((tm, width), lambda i: (i, 0))


def _whole(arr):
    return pl.BlockSpec(arr.shape, lambda i: (0,) * arr.ndim)


def _proj_latents_norm(u, w_a2, w_qn, w_kvn, name, tm=512):
    t = u.shape[0]
    tm = min(tm, t)

    def body(u_ref, w_ref, wq_ref, wkv_ref, a2_ref, qn_ref, kvn_ref):
        a2 = lax.dot_general(u_ref[...], w_ref[...], _DIMS["nt"], preferred_element_type=F32).astype(BF16)
        a2_ref[...] = a2
        qn_ref[...] = _rms(a2[:, :Q_RANK].astype(F32), wq_ref[...]).astype(BF16)
        kvn_ref[...] = _rms(a2[:, Q_RANK:].astype(F32), wkv_ref[...]).astype(BF16)

    return pl.pallas_call(
        body, name=name, grid=(t // tm,), in_specs=[_rows(tm, u.shape[1]), _whole(w_a2), _whole(w_qn), _whole(w_kvn)],
        out_specs=[_rows(tm, Q_RANK + KV_RANK), _rows(tm, Q_RANK), _rows(tm, KV_RANK)],
        out_shape=[jax.ShapeDtypeStruct((t, Q_RANK + KV_RANK), BF16), jax.ShapeDtypeStruct((t, Q_RANK), BF16),
                   jax.ShapeDtypeStruct((t, KV_RANK), BF16)],
        compiler_params=_cparams("parallel"),
    )(u, w_a2, w_qn, w_kvn)


def _out_proj_norm(hm, ha, w_m, w_a, x, w_norm, name, tm=512):
    t = x.shape[0]
    tm = min(tm, t)

    def body(hm_ref, ha_ref, wm_ref, wa_ref, x_ref, wn_ref, h_ref, u_ref):
        h = (jnp.dot(hm_ref[...], wm_ref[...], preferred_element_type=F32)
             + jnp.dot(ha_ref[...], wa_ref[...], preferred_element_type=F32) + x_ref[...])
        h_ref[...] = h
        u_ref[...] = _rms(h, wn_ref[...]).astype(BF16)

    d = x.shape[1]
    return pl.pallas_call(
        body, name=name, grid=(t // tm,),
        in_specs=[_rows(tm, hm.shape[1]), _rows(tm, ha.shape[1]), _whole(w_m), _whole(w_a), _rows(tm, d), _whole(w_norm)],
        out_specs=[_rows(tm, d), _rows(tm, d)],
        out_shape=[jax.ShapeDtypeStruct((t, d), F32), jax.ShapeDtypeStruct((t, d), BF16)],
        compiler_params=_cparams("parallel"),
    )(hm, ha, w_m, w_a, x, w_norm)


def _rmsnorm_fwd(x, w, name, tm=512):
    t, d = x.shape

    def body(x_ref, w_ref, o_ref):
        xf = x_ref[...].astype(F32)
        r = lax.rsqrt(jnp.mean(xf * xf, axis=-1, keepdims=True) + EPS)
        o_ref[...] = (xf * r * w_ref[...]).astype(o_ref.dtype)

    return pl.pallas_call(
        body, name=name, grid=(t // tm,),
        in_specs=[pl.BlockSpec((tm, d), lambda i: (i, 0)), pl.BlockSpec((1, d), lambda i: (0, 0))],
        out_specs=pl.BlockSpec((tm, d), lambda i: (i, 0)),
        out_shape=jax.ShapeDtypeStruct((t, d), BF16), compiler_params=_cparams("parallel"),
    )(x, w)


def _ffn_down_loss(act, w_down, h1, target, w, name, tm=512):
    t, d = h1.shape
    tm = min(tm, t)

    def body(a_ref, wd_ref, h_ref, t_ref, w_ref, dh_ref, dhb_ref, loss_ref, dw_ref):
        xf = jnp.dot(a_ref[...], wd_ref[...], preferred_element_type=F32) + h_ref[...]
        r = lax.rsqrt(jnp.mean(xf * xf, axis=-1, keepdims=True) + EPS)
        xh = xf * r
        err = xh * w_ref[...] - t_ref[...]
        part_loss = 0.5 * jnp.sum(jnp.sum(err * err, axis=-1, keepdims=True), axis=0, keepdims=True) * (1.0 / d)
        dy = err * (1.0 / d)
        dyh = dy * w_ref[...]
        dh = r * (dyh - xh * jnp.mean(dyh * xh, axis=-1, keepdims=True))
        dh_ref[...] = dh
        dhb_ref[...] = dh.astype(BF16)
        _accumulate(dw_ref, jnp.sum(dy * xh, axis=0, keepdims=True))
        _accumulate(loss_ref, jnp.broadcast_to(part_loss, (1, LANES)))

    vec = pl.BlockSpec((1, d), lambda i: (0, 0))
    return pl.pallas_call(
        body, name=name, grid=(t // tm,),
        in_specs=[_rows(tm, act.shape[1]), _whole(w_down), _rows(tm, d), _rows(tm, d), vec],
        out_specs=[_rows(tm, d), _rows(tm, d), pl.BlockSpec((1, LANES), lambda i: (0, 0)), vec],
        out_shape=[jax.ShapeDtypeStruct((t, d), F32), jax.ShapeDtypeStruct((t, d), BF16),
                   jax.ShapeDtypeStruct((1, LANES), F32), jax.ShapeDtypeStruct((1, d), F32)],
        compiler_params=_cparams("arbitrary"),
    )(act, w_down, h1, target, w)


def _matmul_norm_bwd(das, ws, x, w_norm, name, out_dtype, residual=None, also_bf16=False, w_rows=None, tm=512):
    t, d = x.shape
    tm = min(tm, t)
    n = len(das)

    def body(*refs):
        x_ref, wn_ref = refs[2 * n], refs[2 * n + 1]
        outs = refs[2 * n + 2 + (residual is not None):]
        dy = None
        for p in range(n):
            wp = refs[n + p][...] if w_rows is None or w_rows[p] is None else refs[n + p][0:w_rows[p], :]
            part = jnp.dot(refs[p][...], wp, preferred_element_type=F32)
            dy = part if dy is None else dy + part
        dx, dw_terms = _rms_bwd(dy, x_ref[...].astype(F32), wn_ref[...])
        if residual is not None:
            dx = dx + refs[2 * n + 2][...]
        outs[0][...] = dx.astype(outs[0].dtype)
        if also_bf16:
            outs[1][...] = dx.astype(BF16)
        _accumulate(outs[-1], jnp.sum(dw_terms, axis=0, keepdims=True))

    extra = [jax.ShapeDtypeStruct((t, d), BF16)] if also_bf16 else []
    return pl.pallas_call(
        body, name=name, grid=(t // tm,),
        in_specs=([_rows(tm, a.shape[1]) for a in das] + [_whole(w) for w in ws] + [_rows(tm, d), _whole(w_norm)]
                  + ([_rows(tm, d)] if residual is not None else [])),
        out_specs=[_rows(tm, d)] * (1 + len(extra)) + [pl.BlockSpec((1, d), lambda i: (0, 0))],
        out_shape=[jax.ShapeDtypeStruct((t, d), out_dtype)] + extra + [jax.ShapeDtypeStruct((1, d), F32)],
        compiler_params=_cparams("arbitrary"),
    )(*das, *ws, x, w_norm, *([residual] if residual is not None else []))


FFN_TN = 1408


def _ffn_gate_up(u, w_gate, w_up, name, tm=1024):
    t, d = u.shape
    n = w_gate.shape[0]
    tm = min(tm, t)

    def body(u_ref, wg_ref, wu_ref, g_ref, up_ref, act_ref):
        uu = u_ref[...]
        g = lax.dot_general(uu, wg_ref[...], _DIMS["nt"], preferred_element_type=F32)
        up = lax.dot_general(uu, wu_ref[...], _DIMS["nt"], preferred_element_type=F32)
        g_b, up_b = g.astype(BF16), up.astype(BF16)
        g_ref[...] = g_b
        up_ref[...] = up_b
        gf, uf = g_b.astype(F32), up_b.astype(F32)
        act_ref[...] = (gf * jax.nn.sigmoid(gf) * uf).astype(act_ref.dtype)

    w_spec = pl.BlockSpec((FFN_TN, d), lambda i, j: (j, 0))
    o_spec = pl.BlockSpec((tm, FFN_TN), lambda i, j: (i, j))
    out = jax.ShapeDtypeStruct((t, n), BF16)
    return pl.pallas_call(
        body, name=name, grid=(t // tm, n // FFN_TN),
        in_specs=[pl.BlockSpec((tm, d), lambda i, j: (i, 0)), w_spec, w_spec], out_specs=[o_spec] * 3,
        out_shape=[out] * 3, compiler_params=_cparams("parallel", "parallel"),
    )(u, w_gate, w_up)


def _ffn_down_bwd(dh, w_down, g, up, name, tm=1024):
    t, d = dh.shape
    n = w_down.shape[0]
    tm = min(tm, t)

    def body(dh_ref, w_ref, g_ref, up_ref, dg_ref, du_ref):
        df = lax.dot_general(dh_ref[...], w_ref[...], _DIMS["nt"], preferred_element_type=F32).astype(BF16).astype(F32)
        gf = g_ref[...].astype(F32)
        uf = up_ref[...].astype(F32)
        s = jax.nn.sigmoid(gf)
        dg_ref[...] = (df * uf * s * (1.0 + gf * (1.0 - s))).astype(dg_ref.dtype)
        du_ref[...] = (df * gf * s).astype(du_ref.dtype)

    o_spec = pl.BlockSpec((tm, FFN_TN), lambda i, j: (i, j))
    out = jax.ShapeDtypeStruct((t, n), BF16)
    return pl.pallas_call(
        body, name=name, grid=(t // tm, n // FFN_TN),
        in_specs=[pl.BlockSpec((tm, d), lambda i, j: (i, 0)), pl.BlockSpec((FFN_TN, d), lambda i, j: (j, 0)),
                  o_spec, o_spec],
        out_specs=[o_spec, o_spec], out_shape=[out, out], compiler_params=_cparams("parallel", "parallel"),
    )(dh, w_down, g, up)


M_BLOCK = 512
M_CHUNKS_PER_BLOCK = M_BLOCK // M_CHUNK
M_UNROLL = 2


def _log_sigmoid(z):
    return jnp.minimum(z, 0.0) - jnp.log(1.0 + jnp.exp(-jnp.abs(z)))


def _per_head(fn):
    return jnp.stack([fn(hd) for hd in range(M_HEADS)])


def _mlstm_chunk_fwd(q, k, v, gc, c_prev, n_prev, m_prev):
    L, H = M_CHUNK, M_HEADS
    tt = lax.broadcasted_iota(jnp.int32, (L, L), 0)
    ss = lax.broadcasted_iota(jnp.int32, (L, L), 1)
    eye = tt == ss
    causal = ss <= tt
    gct = jnp.transpose(gc)
    i_col = _per_head(lambda hd: gc[:, hd:hd + 1])
    f_col = _per_head(lambda hd: gc[:, H + hd:H + hd + 1])
    i_row = _per_head(lambda hd: gct[hd:hd + 1, :])
    lf_col = _log_sigmoid(f_col)
    lf_row = _log_sigmoid(_per_head(lambda hd: gct[H + hd:H + hd + 1, :]))
    b_col = jnp.sum(jnp.where(causal, lf_row, 0.0), axis=2, keepdims=True)
    b_row = jnp.sum(jnp.where(tt <= ss, lf_col, 0.0), axis=1, keepdims=True)
    g = jnp.sum(lf_col, axis=1, keepdims=True)
    a_row = g - b_row + i_row
    a_col = g - b_col + i_col
    m_loc = jnp.max(a_row, axis=2, keepdims=True)
    d_log = jnp.where(causal, b_col - b_row + i_row, NEG_INF)
    inter = b_col + m_prev
    m_t = jnp.maximum(jnp.max(d_log, axis=2, keepdims=True), inter)
    dmat = jnp.exp(d_log - m_t)
    amat = _per_head(lambda hd: lax.dot_general(q[hd], k[hd], _DIMS["nt"], preferred_element_type=F32))
    p = dmat * amat
    sig = jnp.exp(inter - m_t)
    c_prev_b = [c.astype(BF16) for c in c_prev]
    qc = _per_head(lambda hd: jnp.dot(q[hd], c_prev_b[hd], preferred_element_type=F32))
    p_b = p.astype(BF16)
    num = _per_head(lambda hd: jnp.dot(p_b[hd], v[hd], preferred_element_type=F32)) + sig * qc
    qf = _per_head(lambda hd: q[hd].astype(F32))
    qn = jnp.sum(qf * n_prev, axis=2, keepdims=True)
    den_raw = jnp.sum(p, axis=2, keepdims=True) + sig * qn
    floor = jnp.exp(-m_t)
    den = jnp.maximum(jnp.abs(den_raw), floor)
    h = num / den
    m_new = jnp.maximum(g + m_prev, m_loc)
    w_col = jnp.exp(a_col - m_new)
    w_row = jnp.exp(a_row - m_new)
    alpha = jnp.exp(g + m_prev - m_new)
    return dict(eye=eye, causal=causal, tt=tt, ss=ss, i_col=i_col, f_col=f_col, dmat=dmat, amat=amat, p=p, p_b=p_b,
                sig=sig, qc=qc, qf=qf, qn=qn, num=num, den_raw=den_raw, floor=floor, den=den, h=h, m_new=m_new,
                w_col=w_col, w_row=w_row, alpha=alpha, c_prev_b=c_prev_b, b_row=b_row, b_col=b_col, m_t=m_t)


def _mlstm_fwd(m4, kt, gates, bias, norm_w, name):
    t = m4.shape[0]
    nblk = t // M_BLOCK
    nc = t // M_CHUNK
    L, dh = M_CHUNK, M_HEAD_DIM

    def body(m4_ref, kt_ref, g_ref, b_ref, w_ref, hm_ref, cp_ref, nm_ref, c_s, n_s, m_s):
        @pl.when(pl.program_id(0) == 0)
        def _():
            c_s[...] = jnp.zeros_like(c_s)
            n_s[...] = jnp.zeros_like(n_s)
            m_s[...] = jnp.zeros_like(m_s)

        row8 = lax.broadcasted_iota(jnp.int32, (8, dh), 0)

        def chunk(c, carry):
            rows = pl.ds(pl.multiple_of(c * L, L), L)
            gc = g_ref[rows, :] + b_ref[...]
            col = lambda part, hd: m4_ref[rows, part * M_WIDTH + hd * dh:part * M_WIDTH + (hd + 1) * dh]
            q = [col(0, hd) for hd in range(M_HEADS)]
            k = [col(1, hd) for hd in range(M_HEADS)]
            v = [col(2, hd) for hd in range(M_HEADS)]
            o = _per_head(lambda hd: col(3, hd).astype(F32))
            c_prev = [c_s[hd] for hd in range(M_HEADS)]
            n_prev = n_s[:, 0:1, :]
            m_prev = m_s[:, 0:1, 0:1]
            r = _mlstm_chunk_fwd(q, k, v, gc, c_prev, n_prev, m_prev)
            h = r["h"]
            hn = h * lax.rsqrt(jnp.mean(h * h, axis=-1, keepdims=True) + EPS) * w_ref[...]
            hm = (hn * jax.nn.sigmoid(o)).astype(hm_ref.dtype)
            nm = jnp.where(row8 == 0, n_prev, jnp.where(row8 == 1, m_prev, 0.0))
            kf = _per_head(lambda hd: k[hd].astype(F32))
            n_s[:, 0:1, :] = r["alpha"] * n_prev + jnp.sum(kf * r["w_col"], axis=1, keepdims=True)
            m_s[...] = jnp.broadcast_to(r["m_new"], (M_HEADS, 8, dh))
            for hd in range(M_HEADS):
                hm_ref[rows, hd * dh:(hd + 1) * dh] = hm[hd]
                cp_ref[hd, c] = r["c_prev_b"][hd]
                nm_ref[hd, c] = nm[hd]
                kw_t = (kt_ref[c, hd * dh:(hd + 1) * dh, :].astype(F32) * r["w_row"][hd]).astype(BF16)
                c_s[hd] = r["alpha"][hd] * c_prev[hd] + jnp.dot(kw_t, v[hd], preferred_element_type=F32)
            return carry

        lax.fori_loop(0, M_CHUNKS_PER_BLOCK, chunk, 0, unroll=M_UNROLL)

    return pl.pallas_call(
        body, name=name, grid=(nblk,),
        in_specs=[pl.BlockSpec((M_BLOCK, 4 * M_WIDTH), lambda i: (i, 0)),
                  pl.BlockSpec((M_CHUNKS_PER_BLOCK, M_WIDTH, L), lambda i: (i, 1, 0)),
                  pl.BlockSpec((M_BLOCK, LANES), lambda i: (i, 0)),
                  pl.BlockSpec((1, LANES), lambda i: (0, 0)),
                  pl.BlockSpec((M_HEADS, 1, dh), lambda i: (0, 0, 0))],
        out_specs=[pl.BlockSpec((M_BLOCK, M_WIDTH), lambda i: (i, 0)),
                   pl.BlockSpec((M_HEADS, M_CHUNKS_PER_BLOCK, dh, dh), lambda i: (0, i, 0, 0)),
                   pl.BlockSpec((M_HEADS, M_CHUNKS_PER_BLOCK, 8, dh), lambda i: (0, i, 0, 0))],
        out_shape=[jax.ShapeDtypeStruct((t, M_WIDTH), BF16),
                   jax.ShapeDtypeStruct((M_HEADS, nc, dh, dh), BF16),
                   jax.ShapeDtypeStruct((M_HEADS, nc, 8, dh), F32)],
        scratch_shapes=[pltpu.VMEM((M_HEADS, dh, dh), F32), pltpu.VMEM((M_HEADS, 8, dh), F32),
                        pltpu.VMEM((M_HEADS, 8, dh), F32)],
        compiler_params=_cparams("arbitrary"),
    )(m4, kt, gates, bias, norm_w)


def _mlstm_bwd(m4, qt, gates, bias, norm_w, c_prev_all, nm_all, dhm, name):
    t = m4.shape[0]
    nblk = t // M_BLOCK
    L, dh = M_CHUNK, M_HEAD_DIM

    def body(m4_ref, qt_ref, g_ref, b_ref, w_ref, cp_ref, nm_ref, dhm_ref, dm4_ref, dg_ref, small_ref, dc_s, dn_s):
        @pl.when(pl.program_id(0) == 0)
        def _():
            dc_s[...] = jnp.zeros_like(dc_s)
            dn_s[...] = jnp.zeros_like(dn_s)
            small_ref[...] = jnp.zeros_like(small_ref)

        lane = lax.broadcasted_iota(jnp.int32, (L, LANES), 1)
        row8 = lax.broadcasted_iota(jnp.int32, (8, LANES), 0)

        def chunk(ci, carry):
            c = M_CHUNKS_PER_BLOCK - 1 - ci
            rows = pl.ds(pl.multiple_of(c * L, L), L)
            gc = g_ref[rows, :] + b_ref[...]
            heads = range(M_HEADS)
            col = lambda part, hd: m4_ref[rows, part * M_WIDTH + hd * dh:part * M_WIDTH + (hd + 1) * dh]
            q = [col(0, hd) for hd in heads]
            k = [col(1, hd) for hd in heads]
            v = [col(2, hd) for hd in heads]
            o = _per_head(lambda hd: col(3, hd).astype(F32))
            q_t = [qt_ref[c, hd * dh:(hd + 1) * dh, :] for hd in heads]
            c_prev = [cp_ref[hd, c].astype(F32) for hd in heads]
            nm = _per_head(lambda hd: nm_ref[hd, c])
            n_prev = nm[:, 0:1, :]
            m_prev = nm[:, 1:2, 0:1]
            r = _mlstm_chunk_fwd(q, k, v, gc, c_prev, n_prev, m_prev)
            eye, tt, ss = r["eye"], r["tt"], r["ss"]
            h, den, sig, p = r["h"], r["den"], r["sig"], r["p"]
            w_col, alpha, c_prev_b = r["w_col"], r["alpha"], r["c_prev_b"]
            to_row = lambda colv: jnp.sum(jnp.where(eye, colv, 0.0), axis=1, keepdims=True)
            to_col = lambda rowv: jnp.sum(jnp.where(eye, rowv, 0.0), axis=2, keepdims=True)
            mm = lambda fn: _per_head(lambda hd: fn(hd))

            w_h = w_ref[...]
            rn = lax.rsqrt(jnp.mean(h * h, axis=-1, keepdims=True) + EPS)
            hh = h * rn
            og = jax.nn.sigmoid(o)
            dhm_c = _per_head(lambda hd: dhm_ref[rows, hd * dh:(hd + 1) * dh].astype(F32))
            dhn = dhm_c * og
            d_o = dhm_c * hh * w_h * og * (1.0 - og)
            d_norm = jnp.sum(dhn * hh, axis=1, keepdims=True)
            dhh = dhn * w_h
            dh_ = rn * (dhh - hh * jnp.mean(dhh * hh, axis=-1, keepdims=True))

            dnum = dh_ / den
            dden = -jnp.sum(dh_ * h, axis=-1, keepdims=True) / den
            dden_raw = jnp.where(jnp.abs(r["den_raw"]) >= r["floor"], dden * jnp.sign(r["den_raw"]), 0.0)
            dnum_b = dnum.astype(BF16)
            dp = mm(lambda hd: lax.dot_general(dnum_b[hd], v[hd], _DIMS["nt"], preferred_element_type=F32)) + dden_raw
            dc = [dc_s[hd] for hd in heads]
            dn = dn_s[:, 0:1, :]
            dc_b = [d.astype(BF16) for d in dc]
            g2_b = (sig * dnum).astype(BF16)
            sd = sig * dden_raw
            da_mat = (dp * r["dmat"]).astype(BF16)
            dqs = mm(lambda hd: lax.dot_general(g2_b[hd], c_prev_b[hd], _DIMS["nt"], preferred_element_type=F32)
                     + jnp.dot(da_mat[hd], k[hd], preferred_element_type=F32)) + sd * n_prev
            r_mat = mm(lambda hd: lax.dot_general(v[hd], dc_b[hd], _DIMS["nt"], preferred_element_type=F32)) + dn
            kf = _per_head(lambda hd: k[hd].astype(F32))
            dmat_t = jnp.exp(jnp.where(tt <= ss, r["b_row"] - r["b_col"] + r["i_col"], NEG_INF) - to_row(r["m_t"]))
            p_t = (dmat_t * mm(lambda hd: jnp.dot(k[hd], q_t[hd], preferred_element_type=F32))).astype(BF16)
            dp_t = mm(lambda hd: lax.dot_general(v[hd], dnum_b[hd], _DIMS["nt"], preferred_element_type=F32))
            da_t = ((dp_t + to_row(dden_raw)) * dmat_t).astype(BF16)
            d_k = mm(lambda hd: jnp.dot(da_t[hd], q[hd], preferred_element_type=F32)) + w_col * r_mat
            d_v = (mm(lambda hd: jnp.dot(p_t[hd], dnum_b[hd], preferred_element_type=F32))
                   + w_col * mm(lambda hd: jnp.dot(k[hd], dc_b[hd], preferred_element_type=F32)))
            da_col = jnp.sum(kf * r_mat, axis=-1, keepdims=True) * w_col
            dsig = jnp.sum(dnum * r["qc"], axis=-1, keepdims=True) + dden_raw * r["qn"]
            pp = dp * p
            r1 = jnp.sum(pp, axis=2, keepdims=True)
            c1_col = to_col(jnp.sum(pp, axis=1, keepdims=True))
            dcc = _per_head(lambda hd: dc[hd] * c_prev[hd])
            d_alpha = (jnp.sum(jnp.sum(dcc, axis=2, keepdims=True), axis=1, keepdims=True)
                       + jnp.sum(dn * n_prev, axis=2, keepdims=True))
            dgl = jnp.sum(da_col, axis=1, keepdims=True) + d_alpha * alpha
            db_col = r1 + dsig * sig - c1_col - da_col
            dli_col = c1_col + da_col
            dlf_col = jnp.sum(jnp.where(ss >= tt, to_row(db_col), 0.0), axis=2, keepdims=True) + dgl
            df_col = dlf_col * jax.nn.sigmoid(-r["f_col"])

            dn_s[:, 0:1, :] = alpha * dn + jnp.sum(sd * r["qf"], axis=1, keepdims=True)
            dq_out = (dqs * M_SCALE).astype(dm4_ref.dtype)
            dk_out, dv_out, do_out = d_k.astype(dm4_ref.dtype), d_v.astype(dm4_ref.dtype), d_o.astype(dm4_ref.dtype)
            dg_tile = jnp.zeros((L, LANES), F32)
            small = jnp.zeros((8, LANES), F32)
            for hd in heads:
                dc_s[hd] = alpha[hd] * dc[hd] + jnp.dot(q_t[hd], g2_b[hd], preferred_element_type=F32)
                dm4_ref[rows, hd * dh:(hd + 1) * dh] = dq_out[hd]
                dm4_ref[rows, M_WIDTH + hd * dh:M_WIDTH + (hd + 1) * dh] = dk_out[hd]
                dm4_ref[rows, 2 * M_WIDTH + hd * dh:2 * M_WIDTH + (hd + 1) * dh] = dv_out[hd]
                dm4_ref[rows, 3 * M_WIDTH + hd * dh:3 * M_WIDTH + (hd + 1) * dh] = do_out[hd]
                dg_tile = (dg_tile + jnp.where(lane == hd, dli_col[hd], 0.0)
                           + jnp.where(lane == M_HEADS + hd, df_col[hd], 0.0))
                small = small + jnp.where(row8 == hd, d_norm[hd], 0.0)
            dg_ref[rows, :] = dg_tile
            small = small + jnp.where(row8 == M_HEADS, jnp.sum(dg_tile, axis=0, keepdims=True), 0.0)
            small_ref[...] += small
            return carry

        lax.fori_loop(0, M_CHUNKS_PER_BLOCK, chunk, 0, unroll=M_UNROLL)

    rev = lambda i: nblk - 1 - i
    return pl.pallas_call(
        body, name=name, grid=(nblk,),
        in_specs=[pl.BlockSpec((M_BLOCK, 4 * M_WIDTH), lambda i: (rev(i), 0)),
                  pl.BlockSpec((M_CHUNKS_PER_BLOCK, M_WIDTH, L), lambda i: (rev(i), 0, 0)),
                  pl.BlockSpec((M_BLOCK, LANES), lambda i: (rev(i), 0)),
                  pl.BlockSpec((1, LANES), lambda i: (0, 0)),
                  pl.BlockSpec((M_HEADS, 1, dh), lambda i: (0, 0, 0)),
                  pl.BlockSpec((M_HEADS, M_CHUNKS_PER_BLOCK, dh, dh), lambda i: (0, rev(i), 0, 0)),
                  pl.BlockSpec((M_HEADS, M_CHUNKS_PER_BLOCK, 8, dh), lambda i: (0, rev(i), 0, 0)),
                  pl.BlockSpec((M_BLOCK, M_WIDTH), lambda i: (rev(i), 0))],
        out_specs=[pl.BlockSpec((M_BLOCK, 4 * M_WIDTH), lambda i: (rev(i), 0)),
                   pl.BlockSpec((M_BLOCK, LANES), lambda i: (rev(i), 0)),
                   pl.BlockSpec((8, LANES), lambda i: (0, 0))],
        out_shape=[jax.ShapeDtypeStruct((t, 4 * M_WIDTH), BF16), jax.ShapeDtypeStruct((t, LANES), F32),
                   jax.ShapeDtypeStruct((8, LANES), F32)],
        scratch_shapes=[pltpu.VMEM((M_HEADS, dh, dh), F32), pltpu.VMEM((M_HEADS, 8, dh), F32)],
        compiler_params=_cparams("arbitrary"),
    )(m4, qt, gates, bias, norm_w, c_prev_all, nm_all, dhm)


def _rope_tables(pos_col, width):
    lane = lax.broadcasted_iota(jnp.int32, (1, width), 1) % A_QK_PAD
    half = A_ROPE // 2
    first = (lane >= A_NOPE) & (lane < A_NOPE + half)
    second = (lane >= A_NOPE + half) & (lane < A_NOPE + A_ROPE)
    idx = jnp.where(first, lane - A_NOPE, lane - A_NOPE - half).astype(F32)
    inv_freq = jnp.exp(idx * (-math.log(ROPE_THETA) / half))
    ang = pos_col.astype(F32) * inv_freq
    cos, sin = jnp.cos(ang), jnp.sin(ang)
    rot = first | second
    return jnp.where(rot, cos, 1.0), jnp.where(first, -sin, 0.0), jnp.where(second, sin, 0.0)


def _rope_apply(vv, cosf, s1, s2):
    half = A_ROPE // 2
    w = vv.shape[-1]
    return vv * cosf + pltpu.roll(vv, w - half, 1) * s1 + pltpu.roll(vv, half, 1) * s2


def _rope_apply_t(dd, cosf, s1, s2):
    half = A_ROPE // 2
    w = dd.shape[-1]
    return dd * cosf + pltpu.roll(dd * s1, half, 1) + pltpu.roll(dd * s2, w - half, 1)


A_BIAS_LANE_K = A_NOPE + A_ROPE
A_BIAS_LANE_V = A_V


def _hi_lo(val):
    hi = val.astype(BF16)
    return hi, (val - hi.astype(F32)).astype(BF16)


def _qkv_up_rope(qa_n, kv_n, w_qb, w_k, w_v, gates, pos, name, tm=512):
    t = qa_n.shape[0]
    w = A_HEADS * LANES

    def body(qa_ref, kv_ref, wq_ref, wk_ref, wv_ref, g_ref, p_ref, qo_ref, ko_ref, vo_ref):
        cosf, s1, s2 = _rope_tables(p_ref[...], LANES)
        lane = lax.broadcasted_iota(jnp.int32, (1, LANES), 1)
        kr = jnp.where((lane >= A_NOPE) & (lane < A_NOPE + A_ROPE), g_ref[...], 0.0)
        kr = _rope_apply(kr, cosf, s1, s2)
        kr = jnp.where((lane == A_BIAS_LANE_K) | (lane == A_BIAS_LANE_K + 1), 1.0, kr)
        v_one = (lane == A_BIAS_LANE_V) | (lane == A_BIAS_LANE_V + 1)
        qpre = lax.dot_general(qa_ref[...], wq_ref[...], _DIMS["nt"], preferred_element_type=F32)
        kv = kv_ref[...]
        kpre = lax.dot_general(kv, wk_ref[...], _DIMS["nt"], preferred_element_type=F32)
        vpre = lax.dot_general(kv, wv_ref[...], _DIMS["nt"], preferred_element_type=F32)
        for hd in range(A_HEADS):
            sl = slice(hd * LANES, (hd + 1) * LANES)
            qo_ref[:, sl] = (_rope_apply(qpre[:, sl], cosf, s1, s2) * ATTN_SCALE).astype(qo_ref.dtype)
            ko_ref[:, sl] = (kpre[:, sl] + kr).astype(ko_ref.dtype)
            vo_ref[:, sl] = jnp.where(v_one, 1.0, vpre[:, sl]).astype(vo_ref.dtype)

    rows = lambda width: pl.BlockSpec((tm, width), lambda i: (i, 0))
    whole = lambda arr: pl.BlockSpec(arr.shape, lambda i: (0, 0))
    out = jax.ShapeDtypeStruct((t, w), BF16)
    return pl.pallas_call(
        body, name=name, grid=(t // tm,),
        in_specs=[rows(qa_n.shape[1]), rows(kv_n.shape[1]), whole(w_qb), whole(w_k), whole(w_v), rows(LANES), rows(1)],
        out_specs=[rows(w)] * 3, out_shape=[out] * 3, compiler_params=_cparams("parallel"),
    )(qa_n, kv_n, w_qb, w_k, w_v, gates, pos)


def _rope_bwd(dq, dk, dgates, pos, name, tm=512):
    t, w = dq.shape

    def body(dq_ref, dk_ref, dg_ref, p_ref, dqo_ref, dgo_ref):
        cosf, s1, s2 = _rope_tables(p_ref[...], LANES)
        acc = jnp.zeros((tm, LANES), F32)
        for hd in range(A_HEADS):
            sl = slice(hd * LANES, (hd + 1) * LANES)
            dqo_ref[:, sl] = (_rope_apply_t(dq_ref[:, sl], cosf, s1, s2) * ATTN_SCALE).astype(dqo_ref.dtype)
            acc = acc + dk_ref[:, sl].astype(F32)
        lane = lax.broadcasted_iota(jnp.int32, (1, LANES), 1)
        dkr = _rope_apply_t(acc, cosf, s1, s2)
        dkr = jnp.where((lane >= A_NOPE) & (lane < A_NOPE + A_ROPE), dkr, 0.0)
        dgo_ref[...] = (dg_ref[...] + dkr).astype(dgo_ref.dtype)

    row = pl.BlockSpec((tm, w), lambda i: (i, 0))
    nar = pl.BlockSpec((tm, LANES), lambda i: (i, 0))
    return pl.pallas_call(
        body, name=name, grid=(t // tm,),
        in_specs=[row, row, nar, pl.BlockSpec((tm, 1), lambda i: (i, 0))],
        out_specs=[row, nar],
        out_shape=[jax.ShapeDtypeStruct((t, w), BF16), jax.ShapeDtypeStruct((t, LANES), BF16)],
        compiler_params=_cparams("parallel"),
    )(dq, dk, dgates, pos)


A_TQ = 512
A_TK = 512
A_TQ_FWD = 1024


A_HEADS_PER_STEP = 2
A_HEADS_PER_STEP_FWD = 4


def _flash_fwd(q, k, v, name):
    t = q.shape[0]
    tq, tk = min(A_TQ_FWD, t), A_TK
    nh = A_HEADS_PER_STEP_FWD
    wblk = nh * LANES

    def body(q_ref, k_ref, v_ref, o_ref, qb_ref):
        i = pl.program_id(1)
        lane = lax.broadcasted_iota(jnp.int32, (tq, LANES), 1)
        qpos = i * tq + lax.broadcasted_iota(jnp.int32, (tq, tk), 0)
        kofs = lax.broadcasted_iota(jnp.int32, (tq, tk), 1)
        qs = [q_ref[:, hh * LANES:(hh + 1) * LANES] for hh in range(nh)]

        def step(kb, carry, masked):
            rows = pl.ds(pl.multiple_of(kb * tk, tk), tk)
            new = []
            for hh in range(nh):
                m, acc = carry[hh]
                kh = k_ref[rows, hh * LANES:(hh + 1) * LANES]
                s = lax.dot_general(qs[hh], kh, _DIMS["nt"], preferred_element_type=F32)
                if masked:
                    s = jnp.where(kb * tk + kofs <= qpos, s, NEG_INF)
                m_new = jnp.maximum(m, jnp.max(s, axis=1, keepdims=True))
                p = jnp.exp(s - m_new)
                acc = jnp.exp(m - m_new) * acc + jnp.dot(p.astype(BF16), v_ref[rows, hh * LANES:(hh + 1) * LANES],
                                                          preferred_element_type=F32)
                new.append((m_new, acc))
            return tuple(new)

        carry = tuple((jnp.full((tq, 1), NEG_INF, F32), jnp.zeros((tq, LANES), F32)) for _ in range(nh))
        n_full = (i * tq) // tk
        carry = lax.fori_loop(0, n_full, functools.partial(step, masked=False), carry)
        for d in range(tq // tk):
            carry = step(n_full + d, carry, True)
        for hh in range(nh):
            m, acc = carry[hh]
            l = acc[:, A_BIAS_LANE_V:A_BIAS_LANE_V + 1]
            o_ref[:, hh * LANES:(hh + 1) * LANES] = (acc / l).astype(o_ref.dtype)
            hi, lo = _hi_lo(-(m + jnp.log(l)))
            qb_ref[:, hh * LANES:(hh + 1) * LANES] = jnp.where(
                lane == A_BIAS_LANE_K, hi, jnp.where(lane == A_BIAS_LANE_K + 1, lo, qs[hh]))

    blk = pl.BlockSpec((tq, wblk), lambda j, i: (i, j))
    res = pl.BlockSpec((t, wblk), lambda j, i: (0, j))
    out = jax.ShapeDtypeStruct((t, A_HEADS * LANES), BF16)
    return pl.pallas_call(
        body, name=name, grid=(A_HEADS // nh, t // tq), in_specs=[blk, res, res], out_specs=[blk, blk],
        out_shape=[out, out], compiler_params=_cparams("parallel", "parallel"),
    )(q, k, v)


def _d_out_proj(dh, w_out_m, w_out_a, o, name, tm=512):
    t, w = o.shape
    tm = min(tm, t)

    def body(dh_ref, wm_ref, w_ref, o_ref, dhm_ref, out_ref):
        dh_blk = dh_ref[...]
        dhm_ref[...] = lax.dot_general(dh_blk, wm_ref[...], _DIMS["nt"], preferred_element_type=F32).astype(BF16)
        do = lax.dot_general(dh_blk, w_ref[...], _DIMS["nt"], preferred_element_type=F32).astype(BF16)
        lane = lax.broadcasted_iota(jnp.int32, (tm, LANES), 1)
        for hd in range(A_HEADS):
            sl = slice(hd * LANES, (hd + 1) * LANES)
            d = do[:, sl]
            delta = jnp.sum(jnp.where(lane < A_V, d.astype(F32) * o_ref[:, sl].astype(F32), 0.0), axis=1, keepdims=True)
            hi, lo = _hi_lo(-delta)
            out_ref[:, sl] = jnp.where(lane == A_BIAS_LANE_V, hi, jnp.where(lane == A_BIAS_LANE_V + 1, lo, d))

    return pl.pallas_call(
        body, name=name, grid=(t // tm,),
        in_specs=[_rows(tm, dh.shape[1]), _whole(w_out_m), _whole(w_out_a), _rows(tm, w)],
        out_specs=[_rows(tm, w_out_m.shape[0]), _rows(tm, w)],
        out_shape=[jax.ShapeDtypeStruct((t, w_out_m.shape[0]), BF16), jax.ShapeDtypeStruct((t, w), BF16)],
        compiler_params=_cparams("parallel"),
    )(dh, w_out_m, w_out_a, o)


def _flash_bwd(qb, k, v, doe, name):
    t = qb.shape[0]
    tq, tk = A_TQ, A_TK
    assert tq == tk
    nh = A_HEADS_PER_STEP
    wblk = nh * LANES
    nq = t // tq

    def body(q_ref, k_ref, v_ref, do_ref, dq_ref, dk_ref, dv_ref):
        kb = pl.program_id(1)

        @pl.when(kb == 0)
        def _():
            dq_ref[...] = jnp.zeros_like(dq_ref)

        kpos = kb * tk + lax.broadcasted_iota(jnp.int32, (tk, tq), 0)
        qofs = lax.broadcasted_iota(jnp.int32, (tk, tq), 1)
        ks = [k_ref[:, hh * LANES:(hh + 1) * LANES] for hh in range(nh)]
        vs = [v_ref[:, hh * LANES:(hh + 1) * LANES] for hh in range(nh)]

        def step(qi, carry, masked):
            rows = pl.ds(pl.multiple_of(qi * tq, tq), tq)
            new = []
            for hh in range(nh):
                dk_acc, dv_acc = carry[hh]
                qh = q_ref[rows, hh * LANES:(hh + 1) * LANES]
                doh = do_ref[rows, hh * LANES:(hh + 1) * LANES]
                p_t = jnp.exp(lax.dot_general(ks[hh], qh, _DIMS["nt"], preferred_element_type=F32))
                if masked:
                    p_t = jnp.where(qi * tq + qofs >= kpos, p_t, 0.0)
                ds_t = (p_t * lax.dot_general(vs[hh], doh, _DIMS["nt"], preferred_element_type=F32)).astype(BF16)
                dv_acc = dv_acc + jnp.dot(p_t.astype(BF16), doh, preferred_element_type=F32)
                dk_acc = dk_acc + jnp.dot(ds_t, qh, preferred_element_type=F32)
                dq_ref[rows, hh * LANES:(hh + 1) * LANES] += lax.dot_general(ds_t, ks[hh], _DIMS["tn"],
                                                                             preferred_element_type=F32)
                new.append((dk_acc, dv_acc))
            return tuple(new)

        carry = tuple((jnp.zeros((tk, LANES), F32), jnp.zeros((tk, LANES), F32)) for _ in range(nh))
        carry = step(kb, carry, True)
        carry = lax.fori_loop(kb + 1, nq, functools.partial(step, masked=False), carry)
        for hh in range(nh):
            dk_ref[:, hh * LANES:(hh + 1) * LANES] = carry[hh][0].astype(dk_ref.dtype)
            dv_ref[:, hh * LANES:(hh + 1) * LANES] = carry[hh][1].astype(dv_ref.dtype)

    res = pl.BlockSpec((t, wblk), lambda j, kb: (0, j))
    blk = pl.BlockSpec((tk, wblk), lambda j, kb: (kb, j))
    wide = (t, A_HEADS * LANES)
    return pl.pallas_call(
        body, name=name, grid=(A_HEADS // nh, t // tk), in_specs=[res, blk, blk, res], out_specs=[res, blk, blk],
        out_shape=[jax.ShapeDtypeStruct(wide, F32), jax.ShapeDtypeStruct(wide, BF16), jax.ShapeDtypeStruct(wide, BF16)],
        compiler_params=_cparams("parallel", "arbitrary"),
    )(qb, k, v, doe)


_SPLITS = (M_WIDTH, M_WIDTH, M_WIDTH, M_WIDTH, M_HEADS, M_HEADS, Q_RANK, KV_RANK, A_ROPE)
_OFFS = tuple(sum(_SPLITS[:i]) for i in range(len(_SPLITS) + 1))
_GATE_BLOCK_KR = A_NOPE


def _split_w_in_t(wt):
    z = lambda n: jnp.zeros((n, wt.shape[1]), wt.dtype)
    w_g = jnp.concatenate([wt[_OFFS[4]:_OFFS[6]], z(_GATE_BLOCK_KR - 2 * M_HEADS), wt[_OFFS[8]:_OFFS[9]],
                           z(LANES - _GATE_BLOCK_KR - A_ROPE)], axis=0)
    return wt[_OFFS[6]:_OFFS[8]], w_g


def _merge_w_in_grad_t(g_m4, g_a2, g_g):
    return jnp.concatenate([g_m4, g_g[:2 * M_HEADS], g_a2, g_g[_GATE_BLOCK_KR:_GATE_BLOCK_KR + A_ROPE]], axis=0)


def _pad_heads(w, used):
    w3 = w.reshape(A_HEADS, used, w.shape[1])
    return jnp.pad(w3, ((0, 0), (0, LANES - used), (0, 0))).reshape(A_HEADS * LANES, w.shape[1])


def _unpad_heads(g, used):
    return g.reshape(A_HEADS, LANES, g.shape[1])[:, :used].reshape(A_HEADS * used, g.shape[1])


def _split_w_kv_b_t(wt):
    w3 = wt.reshape(A_HEADS, A_NOPE + A_V, wt.shape[1])
    pad = lambda part: jnp.pad(part, ((0, 0), (0, LANES - part.shape[1]), (0, 0))).reshape(A_HEADS * LANES, wt.shape[1])
    return pad(w3[:, :A_NOPE]), pad(w3[:, A_NOPE:])


def _merge_w_kv_b_grad_t(gk, gv):
    gk3 = gk.reshape(A_HEADS, LANES, gk.shape[1])[:, :A_NOPE]
    gv3 = gv.reshape(A_HEADS, LANES, gv.shape[1])[:, :A_V]
    return jnp.concatenate([gk3, gv3], axis=1).reshape(A_HEADS * (A_NOPE + A_V), gk.shape[1])


def _local_step(x, pos, target, nw, w, start_late_reduce):
    w_in = w["w_in"]
    w_a2, w_g = _split_w_in_t(w_in)
    w_qb = _pad_heads(w["w_q_b"], A_NOPE + A_ROPE)
    w_k, w_v = _split_w_kv_b_t(w["w_kv_b"])
    w_out_m, w_out_a = w["w_out"][:M_WIDTH], _pad_heads(w["w_out"][M_WIDTH:], A_V)
    w_gate, w_up, w_down = w["w_gate"], w["w_up"], w["w_down"]
    bias = jnp.pad(nw["b_gates"], ((0, 0), (0, LANES - 2 * M_HEADS)))
    mnorm = nw["mlstm_norm_w"].reshape(M_HEADS, 1, M_HEAD_DIM)
    n_m4 = 4 * M_WIDTH

    u1 = _rmsnorm_fwd(x, nw["attn_norm_w"], "attn_norm")
    m4 = _matmul(u1, w_in, "nt", BF16, "proj_mlstm", b_rows=n_m4, lead_scale=(M_WIDTH, M_SCALE))
    a2, qa_n, kv_n = _proj_latents_norm(u1, w_a2, nw["q_a_norm_w"], nw["kv_a_norm_w"], "proj_latents")
    gates = _matmul(u1, w_g, "nt", F32, "proj_gates")
    qk_t = _matmul(w_in, u1, "nt", BF16, "proj_qk_t", a_rows=2 * M_WIDTH, lead_row_scale=(M_WIDTH, M_SCALE),
                   col_chunk=M_CHUNK)
    hm, c_prev_all, nm_all = _mlstm_fwd(m4, qk_t, gates, bias, mnorm, "mlstm_fwd")
    qa, kva = a2[:, :Q_RANK], a2[:, Q_RANK:]
    qr, kr, vv = _qkv_up_rope(qa_n, kv_n, w_qb, w_k, w_v, gates, pos, "qkv_up_rope")
    ha, qb = _flash_fwd(qr, kr, vv, "attn_fwd")
    h1, u2 = _out_proj_norm(hm, ha, w_out_m, w_out_a, x, nw["ffn_norm_w"], "out_proj")
    gg, uu, act = _ffn_gate_up(u2, w_gate, w_up, "ffn_gate_up")

    dh2, dh2_b, loss, g_final = _ffn_down_loss(act, w_down, h1, target, nw["final_norm_w"].reshape(1, D_MODEL),
                                               "ffn_down_loss")
    grads = {"final_norm_w": g_final.reshape(D_MODEL)}
    grads_b = {}
    dgg, duu = _ffn_down_bwd(dh2_b, w_down, gg, uu, "ffn_down_bwd")
    grads["w_down"], grads_b["w_down"] = _matmul(act, dh2_b, "tn", F32, "g_w_down", tm=256, tn=D_MODEL, also_bf16=True)
    dh1, dh1_b, grads["ffn_norm_w"] = _matmul_norm_bwd([dgg, duu], [w_gate, w_up], h1, nw["ffn_norm_w"], "d_u2", F32,
                                                       residual=dh2, also_bf16=True, tm=256)
    grads["w_gate"], grads_b["w_gate"] = _matmul(dgg, u2, "tn", F32, "g_w_gate", tm=256, tn=D_MODEL, also_bf16=True)
    grads["w_up"], grads_b["w_up"] = _matmul(duu, u2, "tn", F32, "g_w_up", tm=256, tn=D_MODEL, also_bf16=True)
    dhm, doe = _d_out_proj(dh1_b, w_out_m, w_out_a, ha, "d_out_proj")
    grads["w_out"] = jnp.concatenate([_matmul(hm, dh1_b, "tn", F32, "g_w_out_m", tn=D_MODEL),
                                      _unpad_heads(_matmul(ha, dh1_b, "tn", F32, "g_w_out_a", tn=D_MODEL), A_V)], axis=0)

    late, doe = start_late_reduce(grads, grads_b, doe)

    dqr, dkr, dvv = _flash_bwd(qb, kr, vv, doe, "attn_bwd")
    dm4, dgates, small = _mlstm_bwd(m4, qk_t, gates, bias, mnorm, c_prev_all, nm_all, dhm, "mlstm_bwd")
    grads["mlstm_norm_w"] = small[:M_HEADS].reshape(1, M_HEADS, M_HEAD_DIM)
    grads["b_gates"] = small[M_HEADS:M_HEADS + 1, :2 * M_HEADS]
    dqpre, dgk = _rope_bwd(dqr, dkr, dgates, pos, "rope_bwd")
    grads["w_q_b"] = _unpad_heads(_matmul(dqpre, qa_n, "tn", F32, "g_w_q_b"), A_NOPE + A_ROPE)
    grads["w_kv_b"] = _merge_w_kv_b_grad_t(_matmul(dkr, kv_n, "tn", F32, "g_w_k"),
                                           _matmul(dvv, kv_n, "tn", F32, "g_w_v"))
    dqa, grads["q_a_norm_w"] = _matmul_norm_bwd([dqpre], [w_qb], qa, nw["q_a_norm_w"], "d_qa", BF16)
    dkva, grads["kv_a_norm_w"] = _matmul_norm_bwd([dkr, dvv], [w_k, w_v], kva, nw["kv_a_norm_w"], "d_kva", BF16)
    da2 = jnp.concatenate([dqa, dkva], axis=1)
    grads["w_in"] = _merge_w_in_grad_t(_matmul(dm4, u1, "tn", F32, "g_w_m4", tn=D_MODEL),
                                       _matmul(da2, u1, "tn", F32, "g_w_a2", tm=640, tn=D_MODEL),
                                       _matmul(dgk, u1, "tn", F32, "g_w_g", tn=D_MODEL))
    grad_x, grads["attn_norm_w"] = _matmul_norm_bwd([dm4, da2, dgk], [w_in, w_a2, w_g], x, nw["attn_norm_w"], "d_u1",
                                                    F32, residual=dh1, w_rows=[n_m4, None, None])
    return loss, grad_x, grads, grads_b, late


MESH = pl.DeviceIdType.MESH
N_CHIPS = 4
EARLY = ("w_in", "w_q_b", "w_kv_b")
LATE = ("w_out", "w_gate", "w_up", "w_down")
BIG = EARLY + LATE
TRANSPOSED = ("w_in", "w_q_b", "w_kv_b", "w_gate", "w_up")
LANE_HALVED = ("w_in", "w_out", "w_gate", "w_up", "w_down")
SMALL = ("attn_norm_w", "b_gates", "mlstm_norm_w", "q_a_norm_w", "kv_a_norm_w", "ffn_norm_w", "final_norm_w")
HBM_SPEC = pl.BlockSpec(memory_space=pltpu.HBM)


def _stored(name, a):
    return a[0].T if name in TRANSPOSED else a[0]


def _unstored(name, a):
    return (a.T if name in TRANSPOSED else a)[None]


def _half_shape(name, shape):
    rs, cs = shape
    return (rs, cs // 2) if name in LANE_HALVED else (rs // 2, cs)


def _half(ref, name, h, *lead):
    rs, cs = ref.shape[-2:]
    if name in LANE_HALVED:
        return ref.at[(*lead, slice(None), pl.ds(h * (cs // 2), cs // 2))]
    return ref.at[(*lead, pl.ds(h * (rs // 2), rs // 2), slice(None))]


def _place():
    x, y, c = lax.axis_index("x"), lax.axis_index("y"), lax.axis_index("c")
    others = [(1 - x, y), (x, 1 - y), (1 - x, 1 - y)]
    return x, y, c, others


def _gather_weights(names, shards):
    n = len(shards)

    def body(*refs):
        _gather_body(names, refs[:n], refs[n:2 * n], *refs[2 * n:])

    gathered = pl.pallas_call(
        body, name="gather_weights", in_specs=[HBM_SPEC] * n, out_specs=[HBM_SPEC] * n,
        out_shape=[jax.ShapeDtypeStruct((N_CHIPS,) + s.shape, s.dtype) for s in shards],
        scratch_shapes=[pltpu.SemaphoreType.DMA((6 * n,)), pltpu.SemaphoreType.DMA((6 * n,))],
    )(*shards)
    return _with_own_slab(gathered, shards)


def _with_own_slab(gathered, shards):
    me = 2 * lax.axis_index("x") + lax.axis_index("y")
    return [lax.dynamic_update_slice(g, s[None], (me, 0, 0)) for g, s in zip(gathered, shards)]


def _gather_body(names, ins, outs, send_sems, recv_sems):
    x, y, c, others = _place()
    me = 2 * x + y
    sibling = (x, y, 1 - c)

    def copy(w, k, slab, core, to, src=None):
        dst = _half(outs[w], names[w], core, slab)
        return pltpu.make_async_remote_copy(
            src_ref=dst if src is None else src, dst_ref=dst, send_sem=send_sems.at[w * 6 + k],
            recv_sem=recv_sems.at[w * 6 + k], device_id=to, device_id_type=MESH)

    sends = []
    for w in range(len(names)):
        for j, chip in enumerate(others):
            cp = copy(w, j, me, c, (*chip, c), src=_half(ins[w], names[w], c))
            cp.start()
            sends.append(cp)
    for w in range(len(names)):
        for j, (ox, oy) in enumerate(others):
            slab = 2 * ox + oy
            copy(w, j, slab, c, (x, y, c)).wait_recv()
            fwd = copy(w, 3 + j, slab, c, sibling)
            fwd.start()
            sends.append(fwd)
    for w in range(len(names)):
        for j, (ox, oy) in enumerate(others):
            copy(w, 3 + j, 2 * ox + oy, 1 - c, (x, y, c)).wait_recv()
    for cp in sends:
        cp.wait_send()


GATHER_LATE_COLLECTIVE_ID = 1


def _gather_weights_async(names, shards):
    n = len(shards)
    src = [jax.new_ref(s, memory_space=pltpu.MemorySpace.HBM) for s in shards]
    dst = [jax.empty_ref(jax.ShapeDtypeStruct((N_CHIPS,) + s.shape, s.dtype), memory_space=pltpu.MemorySpace.HBM)
           for s in shards]

    @pl.kernel(mesh=plsc.ScalarSubcoreMesh(axis_name="sequencer", num_cores=1), name="gather_weights_async",
               scratch_types=(pltpu.SemaphoreType.DMA((6 * n,)), pltpu.SemaphoreType.DMA((6 * n,))),
               compiler_params=pltpu.CompilerParams(collective_id=GATHER_LATE_COLLECTIVE_ID))
    def launch(send_sems, recv_sems):
        x, y, c, others = _place()
        peers = [(ox, oy, c) for ox, oy in others] + [(x, y, 1 - c)]
        barrier = pltpu.get_barrier_semaphore()
        for peer in peers:
            pl.semaphore_signal(barrier, inc=1, device_id=peer, device_id_type=MESH)
        pl.semaphore_wait(barrier, len(peers))
        _gather_body(names, src, dst, send_sems, recv_sems)

    launch()
    return _with_own_slab([d[...] for d in dst], shards)


def _exchange(arrays, out_shapes, plan, copies_per_array, name):
    n = len(arrays)

    def body(*refs):
        ins, outs = refs[:n], refs[n:2 * n]
        send_sems, recv_sems = refs[2 * n:]
        rem = [pltpu.make_async_remote_copy(src_ref=s, dst_ref=d, send_sem=send_sems.at[k], recv_sem=recv_sems.at[k],
                                            device_id=to, device_id_type=MESH)
               for k, (s, d, to) in enumerate(plan(ins, outs, _place()))]
        for cp in rem:
            cp.start()
        for cp in rem:
            cp.wait_recv()
        for cp in rem:
            cp.wait_send()

    return pl.pallas_call(
        body, name=name, in_specs=[HBM_SPEC] * n, out_specs=[HBM_SPEC] * n, out_shape=out_shapes,
        scratch_shapes=[pltpu.SemaphoreType.DMA((copies_per_array * n,)),
                        pltpu.SemaphoreType.DMA((copies_per_array * n,))],
    )(*arrays)


def _plan_to_sibling(ins, outs, place):
    x, y, c, _ = place
    return [(ins[w], outs[w], (x, y, 1 - c)) for w in range(len(ins))]


N_PEERS = 7


def _shard_sum_direct(g3, got7, name, where):
    hs = tuple(got7.shape[1:])
    if name in LANE_HALVED:
        g_in, g_spec = g3, pl.BlockSpec((1,) + hs, lambda i, p: (p[0], 0, p[1]))
    else:
        g_in, g_spec = g3.reshape((N_CHIPS, 2) + hs), pl.BlockSpec((1, 1) + hs, lambda i, p: (p[0], p[1], 0, 0))

    def body(p_ref, g_ref, r_ref, o_ref):
        acc = g_ref[(0,) * (len(g_ref.shape) - 2)]
        for k in range(N_PEERS):
            acc = acc + r_ref[k].astype(F32)
        o_ref[...] = acc

    return pl.pallas_call(
        body, name="shard_sum_%s" % name,
        grid_spec=pltpu.PrefetchScalarGridSpec(
            num_scalar_prefetch=1, grid=(1,),
            in_specs=[g_spec, pl.BlockSpec((N_PEERS,) + hs, lambda i, p: (0, 0, 0))],
            out_specs=pl.BlockSpec(hs, lambda i, p: (0, 0))),
        out_shape=jax.ShapeDtypeStruct(hs, F32), compiler_params=_cparams("arbitrary"),
    )(where, g_in, got7)


def _adamw_halves(w, m, v, mine, other, name, core):
    hs = tuple(mine.shape)
    full = pl.BlockSpec(hs, (lambda h, c_ref: (0, h)) if name in LANE_HALVED else (lambda h, c_ref: (h, 0)))
    half = pl.BlockSpec(hs, lambda h, c_ref: (0, 0))

    def body(c_ref, w_ref, m_ref, v_ref, a_ref, b_ref, g_ref, d_ref, mo_ref, vo_ref):
        g = jnp.where(pl.program_id(0) == c_ref[0], a_ref[...], b_ref[...])
        delta, m_new, v_new = _adamw_math(w_ref[...], g, m_ref[...], v_ref[...])
        g_ref[...] = g
        d_ref[...] = delta
        mo_ref[...] = m_new
        vo_ref[...] = v_new

    out = jax.ShapeDtypeStruct(w.shape, F32)
    return pl.pallas_call(
        body, name="adamw_%s" % name,
        grid_spec=pltpu.PrefetchScalarGridSpec(num_scalar_prefetch=1, grid=(2,), in_specs=[full, full, full, half, half],
                                               out_specs=[full] * 4),
        out_shape=[out] * 4, compiler_params=_cparams("parallel"),
    )(core, w, m, v, mine, other)


def _adamw_math(w, g, m, v):
    m = ADAM_B1 * m + (1.0 - ADAM_B1) * g
    v = ADAM_B2 * v + (1.0 - ADAM_B2) * (g * g)
    m_hat = m / (1.0 - ADAM_B1 ** ADAM_STEP)
    v_hat = v / (1.0 - ADAM_B2 ** ADAM_STEP)
    delta = -ADAM_LR * (m_hat / (jnp.sqrt(v_hat) + ADAM_EPS) + ADAM_WD * w)
    return delta, m, v


def _core_index():
    return lax.axis_index("c").astype(jnp.int32).reshape(1)


DIRECT_REDUCE_COLLECTIVE_ID = {"late": 2, "early": 3}


def _exchange_contributions_async(names, gb3, tag):
    n = len(gb3)
    src = [jax.new_ref(g, memory_space=pltpu.MemorySpace.HBM) for g in gb3]
    dst = [jax.empty_ref(jax.ShapeDtypeStruct((N_PEERS,) + _half_shape(nm, g.shape[1:]), g.dtype),
                         memory_space=pltpu.MemorySpace.HBM) for nm, g in zip(names, gb3)]

    @pl.kernel(mesh=plsc.ScalarSubcoreMesh(axis_name="sequencer", num_cores=1), name="contributions_async_" + tag,
               scratch_types=(pltpu.SemaphoreType.DMA((N_PEERS * n,)), pltpu.SemaphoreType.DMA((N_PEERS * n,))),
               compiler_params=pltpu.CompilerParams(collective_id=DIRECT_REDUCE_COLLECTIVE_ID[tag]))
    def launch(send_sems, recv_sems):
        x, y, c, _ = _place()
        peers = [(x ^ ((k >> 2) & 1), y ^ ((k >> 1) & 1), c ^ (k & 1)) for k in range(1, N_PEERS + 1)]
        barrier = pltpu.get_barrier_semaphore()
        for peer in peers:
            pl.semaphore_signal(barrier, inc=1, device_id=peer, device_id_type=MESH)
        pl.semaphore_wait(barrier, N_PEERS)
        rem = []
        for w in range(n):
            for k, (px, py, pc) in enumerate(peers):
                rem.append(pltpu.make_async_remote_copy(
                    src_ref=_half(src[w], names[w], pc, 2 * px + py), dst_ref=dst[w].at[k],
                    send_sem=send_sems.at[N_PEERS * w + k], recv_sem=recv_sems.at[N_PEERS * w + k],
                    device_id=(px, py, pc), device_id_type=MESH))
        for cp in rem:
            cp.start()
        for cp in rem:
            cp.wait_recv()
        for cp in rem:
            cp.wait_send()

    launch()
    return [d[...] for d in dst]


def _shard_index():
    return (2 * lax.axis_index("x") + lax.axis_index("y")).astype(jnp.int32).reshape(1)


def _with_sibling_half(names, mine, tag):
    other = _exchange(mine, [jax.ShapeDtypeStruct(m.shape, F32) for m in mine], _plan_to_sibling, 1,
                      "sibling_result_" + tag)
    return {n: (a, b) for n, a, b in zip(names, mine, other)}


SMALL_ROWS = 8
SMALL_LAYOUT = {"attn_norm_w": (0, 0, 1024), "ffn_norm_w": (1, 0, 1024), "final_norm_w": (2, 0, 1024),
                "q_a_norm_w": (3, 0, 384), "kv_a_norm_w": (3, 384, 256), "mlstm_norm_w": (4, 0, 512),
                "b_gates": (4, 512, 8)}
LOSS_SLOT = (5, 0)


def _small_allreduce_adamw(grads, loss, wts, mom, vel):
    shapes = {n: wts[n].shape for n in SMALL}
    flat = lambda d: [d[n].reshape(1, SMALL_LAYOUT[n][2]) for n in SMALL]
    ns = len(SMALL)

    def body(*refs):
        g_in, loss_ref = refs[:ns], refs[ns]
        w_in_, m_in, v_in = refs[ns + 1:2 * ns + 1], refs[2 * ns + 1:3 * ns + 1], refs[3 * ns + 1:4 * ns + 1]
        outs = refs[4 * ns + 1:8 * ns + 2]
        tile, slots, send_sems, recv_sems = refs[8 * ns + 2:]
        x, y, c, _ = _place()
        me = 4 * x + 2 * y + c
        tile[...] = jnp.zeros_like(tile)
        for n, g_ref in zip(SMALL, g_in):
            r, c0, width = SMALL_LAYOUT[n]
            tile[r:r + 1, c0:c0 + width] = g_ref[...]
        tile[LOSS_SLOT[0]:LOSS_SLOT[0] + 1, 0:LANES] = loss_ref[...]
        slots[me] = tile[...]
        copies = []
        for k in range(1, N_PEERS + 1):
            to = (x ^ ((k >> 2) & 1), y ^ ((k >> 1) & 1), c ^ (k & 1))
            cp = pltpu.make_async_remote_copy(src_ref=tile, dst_ref=slots.at[me], send_sem=send_sems.at[k - 1],
                                              recv_sem=recv_sems.at[k - 1], device_id=to, device_id_type=MESH)
            cp.start()
            copies.append(cp)
        for k in range(1, N_PEERS + 1):
            pltpu.make_async_remote_copy(src_ref=tile, dst_ref=slots.at[me ^ k], send_sem=send_sems.at[k - 1],
                                         recv_sem=recv_sems.at[k - 1], device_id=(x, y, c),
                                         device_id_type=MESH).wait_recv()
        for cp in copies:
            cp.wait_send()
        total = slots[0]
        for d in range(1, N_PEERS + 1):
            total = total + slots[d]
        for i, n in enumerate(SMALL):
            r, c0, width = SMALL_LAYOUT[n]
            g = total[r:r + 1, c0:c0 + width]
            delta, m_new, v_new = _adamw_math(w_in_[i][...], g, m_in[i][...], v_in[i][...])
            outs[i][...] = g
            outs[ns + i][...] = delta
            outs[2 * ns + i][...] = m_new
            outs[3 * ns + i][...] = v_new
        outs[4 * ns][...] = total[LOSS_SLOT[0]:LOSS_SLOT[0] + 1, 0:LANES]

    vm = pl.BlockSpec(memory_space=pltpu.VMEM)
    vec = [jax.ShapeDtypeStruct((1, SMALL_LAYOUT[n][2]), F32) for n in SMALL]
    res = pl.pallas_call(
        body, name="small_allreduce_adamw", in_specs=[vm] * (4 * ns + 1), out_specs=[vm] * (4 * ns + 1),
        out_shape=vec * 4 + [jax.ShapeDtypeStruct((1, LANES), F32)],
        scratch_shapes=[pltpu.VMEM((SMALL_ROWS, D_MODEL), F32), pltpu.VMEM((N_PEERS + 1, SMALL_ROWS, D_MODEL), F32),
                        pltpu.SemaphoreType.DMA((N_PEERS,)), pltpu.SemaphoreType.DMA((N_PEERS,))],
    )(*flat(grads), loss, *flat(wts), *flat(mom), *flat(vel))
    groups = [{n: r.reshape(shapes[n]) for n, r in zip(SMALL, res[i * ns:(i + 1) * ns])} for i in range(4)]
    return (*groups, res[4 * ns][0, 0])


def kernel(x, positions, attn_norm_w, w_in, b_gates, mlstm_norm_w, q_a_norm_w, w_q_b, kv_a_norm_w, w_kv_b, w_out, ffn_norm_w, w_gate, w_up, w_down, final_norm_w, loss_target, m_attn_norm_w, m_w_in, m_b_gates, m_mlstm_norm_w, m_q_a_norm_w, m_w_q_b, m_kv_a_norm_w, m_w_kv_b, m_w_out, m_ffn_norm_w, m_w_gate, m_w_up, m_w_down, m_final_norm_w, v_attn_norm_w, v_w_in, v_b_gates, v_mlstm_norm_w, v_q_a_norm_w, v_w_q_b, v_kv_a_norm_w, v_w_kv_b, v_w_out, v_ffn_norm_w, v_w_gate, v_w_up, v_w_down, v_final_norm_w):
    names = ("attn_norm_w", "w_in", "b_gates", "mlstm_norm_w", "q_a_norm_w", "w_q_b", "kv_a_norm_w", "w_kv_b", "w_out",
             "ffn_norm_w", "w_gate", "w_up", "w_down", "final_norm_w")
    wts = dict(zip(names, (attn_norm_w, w_in, b_gates, mlstm_norm_w, q_a_norm_w, w_q_b, kv_a_norm_w, w_kv_b, w_out,
                           ffn_norm_w, w_gate, w_up, w_down, final_norm_w)))
    mom = dict(zip(names, (m_attn_norm_w, m_w_in, m_b_gates, m_mlstm_norm_w, m_q_a_norm_w, m_w_q_b, m_kv_a_norm_w,
                           m_w_kv_b, m_w_out, m_ffn_norm_w, m_w_gate, m_w_up, m_w_down, m_final_norm_w)))
    vel = dict(zip(names, (v_attn_norm_w, v_w_in, v_b_gates, v_mlstm_norm_w, v_q_a_norm_w, v_w_q_b, v_kv_a_norm_w,
                           v_w_kv_b, v_w_out, v_ffn_norm_w, v_w_gate, v_w_up, v_w_down, v_final_norm_w)))
    t = x.shape[1]

    shards = {n: _stored(n, wts[n]).astype(BF16) for n in BIG}
    first, later = ("w_in",), tuple(n for n in BIG if n != "w_in")
    now = _gather_weights(first, [shards[n] for n in first])
    now, later_in = lax.optimization_barrier((now, [shards[n] for n in later]))
    behind = _gather_weights_async(later, later_in)
    full = {n: g.reshape(N_CHIPS * g.shape[1], g.shape[2]) for n, g in zip(first + later, list(now) + list(behind))}

    nw = {n: wts[n] for n in SMALL}
    by_shard = lambda g: g.reshape(N_CHIPS, g.shape[0] // N_CHIPS, g.shape[1])

    def start_reduce(group, tag, grads, grads_b, marker):
        g3 = [by_shard(grads[n]) for n in group]
        gb3 = [by_shard(grads_b[n]) if n in grads_b else g.astype(BF16) for n, g in zip(group, g3)]
        gb3, marker = lax.optimization_barrier((gb3, marker))
        return (g3, _exchange_contributions_async(group, gb3, tag)), marker

    loss, grad_x, grads, grads_b, (late_g3, late_got7) = _local_step(
        x[0], positions.reshape(t, 1), loss_target[0], nw, full, functools.partial(start_reduce, LATE, "late"))

    (early_g3, early_got7), late_got7 = start_reduce(EARLY, "early", grads, grads_b, list(late_got7))
    where = jnp.concatenate([_shard_index(), _core_index()])
    outs_g, outs_d, outs_m, outs_v = {}, {}, {}, {}

    def finish(group, g3, got7, tag):
        mine = [_shard_sum_direct(g, r7, n, where) for n, g, r7 in zip(group, g3, got7)]
        halves = _with_sibling_half(group, mine, tag)
        for n in group:
            res = _adamw_halves(_stored(n, wts[n]), _stored(n, mom[n]), _stored(n, vel[n]), *halves[n], n, where[1:])
            outs_g[n], outs_d[n], outs_m[n], outs_v[n] = [_unstored(n, r) for r in res]

    finish(LATE, late_g3, late_got7, "late")
    small_g, small_d, small_m, small_v, total_loss = _small_allreduce_adamw(grads, loss, wts, mom, vel)
    finish(EARLY, early_g3, early_got7, "early")
    outs_g.update(small_g)
    outs_d.update(small_d)
    outs_m.update(small_m)
    outs_v.update(small_v)
    return (total_loss, grad_x[None], *[outs_g[n] for n in names], *[outs_d[n] for n in names],
            *[outs_m[n] for n in names], *[outs_v[n] for n in names])
```

```python
import functools
import math

import jax
import jax.numpy as jnp
from jax import lax
from jax.experimental import pallas as pl
from jax.experimental.pallas import tpu as pltpu
from jax.experimental.pallas import tpu_sc as plsc

F32 = jnp.float32
BF16 = jnp.bfloat16

D_MODEL = 1024
M_HEADS = 4
M_HEAD_DIM = 128
M_WIDTH = M_HEADS * M_HEAD_DIM
M_CHUNK = 64
A_HEADS = 8
A_NOPE = 64
A_ROPE = 32
A_V = 64
A_WIDTH = A_HEADS * A_V
A_QK_PAD = 128
Q_RANK = 384
KV_RANK = 256
ROPE_THETA = 10000.0
D_FF = 2816
D_IN = 2728
EPS = 1e-6
ATTN_SCALE = (A_NOPE + A_ROPE) ** -0.5
M_SCALE = M_HEAD_DIM ** -0.5

ADAM_LR = 0.001
ADAM_B1 = 0.9
ADAM_B2 = 0.999
ADAM_EPS = 1e-08
ADAM_WD = 0.01
ADAM_STEP = 10

VMEM_LIMIT_BYTES = 56 * 1024 * 1024
LANES = 128
NEG_INF = float("-inf")


def _cparams(*sem):
    return pltpu.CompilerParams(dimension_semantics=sem if sem else None, vmem_limit_bytes=VMEM_LIMIT_BYTES)


_DIMS = {"nn": (((1,), (0,)), ((), ())), "nt": (((1,), (1,)), ((), ())), "tn": (((0,), (0,)), ((), ()))}


def _matmul(a, b, mode, out_dtype, name, tm=None, tn=1024, residual=None, a_rows=None, b_rows=None, lead_scale=None,
            lead_row_scale=None, also_bf16=False, col_chunk=None):
    a_list = list(a) if isinstance(a, (list, tuple)) else [a]
    b_list = list(b) if isinstance(b, (list, tuple)) else [b]
    rows_list = list(b_rows) if isinstance(b_rows, (list, tuple)) else [b_rows] * len(b_list)
    assert len(a_list) == len(b_list) == len(rows_list)
    dims = _DIMS[mode]
    specs, m, n = [], None, None
    for aa, bb, rr in zip(a_list, b_list, rows_list):
        b_shape = bb.shape if rr is None else (rr, bb.shape[1])
        a_shape = aa.shape if a_rows is None else (a_rows, aa.shape[1])
        if mode == "nn":
            (m1, k), (k2, n1) = a_shape, b_shape
        elif mode == "nt":
            (m1, k), (n1, k2) = a_shape, b_shape
        else:
            (k, m1), (k2, n1) = aa.shape, b_shape
        assert k == k2 and (m is None or (m, n) == (m1, n1)), (aa.shape, bb.shape, mode)
        m, n = m1, n1
        specs.append(k)
    if tm is None:
        tm = 512 if mode == "tn" else 1024
    tm, tn = min(tm, m), min(tn, n)
    assert m % tm == 0 and n % tn == 0, (m, n, tm, tn)
    in_specs = []
    for k in specs:
        in_specs.append(pl.BlockSpec((k, tm), lambda i, j: (0, i)) if mode == "tn"
                        else pl.BlockSpec((tm, k), lambda i, j: (i, 0)))
        in_specs.append(pl.BlockSpec((tn, k), lambda i, j: (j, 0)) if mode == "nt"
                        else pl.BlockSpec((k, tn), lambda i, j: (0, j)))
    o_spec = pl.BlockSpec((tm, tn), lambda i, j: (i, j))
    n_pairs = len(specs)

    def body(*refs):
        acc = None
        for p in range(n_pairs):
            part = lax.dot_general(refs[2 * p][...].astype(BF16), refs[2 * p + 1][...].astype(BF16), dims,
                                   preferred_element_type=F32)
            acc = part if acc is None else acc + part
        if lead_scale is not None:
            col = pl.program_id(1) * tn + lax.broadcasted_iota(jnp.int32, (1, tn), 1)
            acc = acc * jnp.where(col < lead_scale[0], lead_scale[1], 1.0)
        if lead_row_scale is not None:
            row = pl.program_id(0) * tm + lax.broadcasted_iota(jnp.int32, (tm, 1), 0)
            acc = acc * jnp.where(row < lead_row_scale[0], lead_row_scale[1], 1.0)
        if residual is not None:
            acc = acc + refs[2 * n_pairs][...].astype(F32)
        outs = refs[2 * n_pairs + (residual is not None):]
        if col_chunk is not None:
            res = acc.astype(outs[0].dtype)
            for c in range(tn // col_chunk):
                outs[0][c] = res[:, c * col_chunk:(c + 1) * col_chunk]
            return
        outs[0][...] = acc.astype(outs[0].dtype)
        if also_bf16:
            outs[1][...] = acc.astype(BF16)

    ins = [x for pair in zip(a_list, b_list) for x in pair] + ([residual] if residual is not None else [])
    in_specs = in_specs + ([o_spec] if residual is not None else [])
    if col_chunk is not None:
        assert residual is None and not also_bf16 and tn % col_chunk == 0
        return pl.pallas_call(
            body, name=name, grid=(m // tm, n // tn), in_specs=in_specs,
            out_specs=pl.BlockSpec((tn // col_chunk, tm, col_chunk), lambda i, j: (j, i, 0)),
            out_shape=jax.ShapeDtypeStruct((n // col_chunk, m, col_chunk), out_dtype),
            compiler_params=_cparams("parallel", "parallel"),
        )(*ins)
    out_shape = jax.ShapeDtypeStruct((m, n), out_dtype)
    if also_bf16:
        return pl.pallas_call(
            body, name=name, grid=(m // tm, n // tn), in_specs=in_specs, out_specs=[o_spec, o_spec],
            out_shape=[out_shape, jax.ShapeDtypeStruct((m, n), BF16)],
            compiler_params=_cparams("parallel", "parallel"),
        )(*ins)
    return pl.pallas_call(
        body, name=name, grid=(m // tm, n // tn), in_specs=in_specs, out_specs=o_spec, out_shape=out_shape,
        compiler_params=_cparams("parallel", "parallel"),
    )(*ins)


def _rms(xf, w):
    return xf * lax.rsqrt(jnp.mean(xf * xf, axis=-1, keepdims=True) + EPS) * w


def _rms_bwd(dyf, xf, w):
    r = lax.rsqrt(jnp.mean(xf * xf, axis=-1, keepdims=True) + EPS)
    xh = xf * r
    dyh = dyf * w
    return r * (dyh - xh * jnp.mean(dyh * xh, axis=-1, keepdims=True)), dyf * xh


def _accumulate(ref, part):
    @pl.when(pl.program_id(0) == 0)
    def _():
        ref[...] = part

    @pl.when(pl.program_id(0) > 0)
    def _():
        ref[...] += part


def _rows(tm, width):
    return pl.BlockSpec((tm, width), lambda i: (i, 0))


def _whole(arr):
    return pl.BlockSpec(arr.shape, lambda i: (0,) * arr.ndim)


def _norm_proj_latents(x, w_norm, w_a2, w_g, w_qn, w_kvn, name, tm=512):
    t, d = x.shape
    tm = min(tm, t)
    n_a2 = Q_RANK + KV_RANK

    def body(x_ref, wn_ref, wa_ref, wg_ref, wq_ref, wkv_ref, u_ref, a2_ref, qn_ref, kvn_ref, g_ref):
        u = _rms(x_ref[...], wn_ref[...]).astype(BF16)
        u_ref[...] = u
        a2 = lax.dot_general(u, wa_ref[...], _DIMS["nt"], preferred_element_type=F32).astype(BF16)
        a2_ref[...] = a2
        qn_ref[...] = _rms(a2[:, :Q_RANK].astype(F32), wq_ref[...]).astype(BF16)
        kvn_ref[...] = _rms(a2[:, Q_RANK:].astype(F32), wkv_ref[...]).astype(BF16)
        g_ref[...] = lax.dot_general(u, wg_ref[...], _DIMS["nt"], preferred_element_type=F32)

    return pl.pallas_call(
        body, name=name, grid=(t // tm,),
        in_specs=[_rows(tm, d), _whole(w_norm), _whole(w_a2), _whole(w_g), _whole(w_qn), _whole(w_kvn)],
        out_specs=[_rows(tm, d), _rows(tm, n_a2), _rows(tm, Q_RANK), _rows(tm, KV_RANK), _rows(tm, LANES)],
        out_shape=[jax.ShapeDtypeStruct((t, d), BF16), jax.ShapeDtypeStruct((t, n_a2), BF16),
                   jax.ShapeDtypeStruct((t, Q_RANK), BF16), jax.ShapeDtypeStruct((t, KV_RANK), BF16),
                   jax.ShapeDtypeStruct((t, LANES), F32)],
        compiler_params=_cparams("parallel"),
    )(x, w_norm, w_a2, w_g, w_qn, w_kvn)


def _out_proj_norm(hm, ha, w_m, w_a, x, w_norm, name, tm=512):
    t = x.shape[0]
    tm = min(tm, t)

    def body(hm_ref, ha_ref, wm_ref, wa_ref, x_ref, wn_ref, h_ref, u_ref):
        h = (jnp.dot(hm_ref[...], wm_ref[...], preferred_element_type=F32)
             + jnp.dot(ha_ref[...], wa_ref[...], preferred_element_type=F32) + x_ref[...])
        h_ref[...] = h
        u_ref[...] = _rms(h, wn_ref[...]).astype(BF16)

    d = x.shape[1]
    return pl.pallas_call(
        body, name=name, grid=(t // tm,),
        in_specs=[_rows(tm, hm.shape[1]), _rows(tm, ha.shape[1]), _whole(w_m), _whole(w_a), _rows(tm, d), _whole(w_norm)],
        out_specs=[_rows(tm, d), _rows(tm, d)],
        out_shape=[jax.ShapeDtypeStruct((t, d), F32), jax.ShapeDtypeStruct((t, d), BF16)],
        compiler_params=_cparams("parallel"),
    )(hm, ha, w_m, w_a, x, w_norm)


def _ffn_down_loss(act, w_down, h1, target, w, name, tm=512):
    t, d = h1.shape
    tm = min(tm, t)

    def body(a_ref, wd_ref, h_ref, t_ref, w_ref, dh_ref, dhb_ref, loss_ref, dw_ref):
        xf = jnp.dot(a_ref[...], wd_ref[...], preferred_element_type=F32) + h_ref[...]
        r = lax.rsqrt(jnp.mean(xf * xf, axis=-1, keepdims=True) + EPS)
        xh = xf * r
        err = xh * w_ref[...] - t_ref[...]
        part_loss = 0.5 * jnp.sum(jnp.sum(err * err, axis=-1, keepdims=True), axis=0, keepdims=True) * (1.0 / d)
        dy = err * (1.0 / d)
        dyh = dy * w_ref[...]
        dh = r * (dyh - xh * jnp.mean(dyh * xh, axis=-1, keepdims=True))
        dh_ref[...] = dh
        dhb_ref[...] = dh.astype(BF16)
        _accumulate(dw_ref, jnp.sum(dy * xh, axis=0, keepdims=True))
        _accumulate(loss_ref, jnp.broadcast_to(part_loss, (1, LANES)))

    vec = pl.BlockSpec((1, d), lambda i: (0, 0))
    return pl.pallas_call(
        body, name=name, grid=(t // tm,),
        in_specs=[_rows(tm, act.shape[1]), _whole(w_down), _rows(tm, d), _rows(tm, d), vec],
        out_specs=[_rows(tm, d), _rows(tm, d), pl.BlockSpec((1, LANES), lambda i: (0, 0)), vec],
        out_shape=[jax.ShapeDtypeStruct((t, d), F32), jax.ShapeDtypeStruct((t, d), BF16),
                   jax.ShapeDtypeStruct((1, LANES), F32), jax.ShapeDtypeStruct((1, d), F32)],
        compiler_params=_cparams("arbitrary"),
    )(act, w_down, h1, target, w)


def _matmul_norm_bwd(das, ws, x, w_norm, name, out_dtype, residual=None, also_bf16=False, w_rows=None, tm=512):
    t, d = x.shape
    tm = min(tm, t)
    n = len(das)

    def body(*refs):
        x_ref, wn_ref = refs[2 * n], refs[2 * n + 1]
        outs = refs[2 * n + 2 + (residual is not None):]
        dy = None
        for p in range(n):
            wp = refs[n + p][...] if w_rows is None or w_rows[p] is None else refs[n + p][0:w_rows[p], :]
            part = jnp.dot(refs[p][...], wp, preferred_element_type=F32)
            dy = part if dy is None else dy + part
        dx, dw_terms = _rms_bwd(dy, x_ref[...].astype(F32), wn_ref[...])
        if residual is not None:
            dx = dx + refs[2 * n + 2][...]
        outs[0][...] = dx.astype(outs[0].dtype)
        if also_bf16:
            outs[1][...] = dx.astype(BF16)
        _accumulate(outs[-1], jnp.sum(dw_terms, axis=0, keepdims=True))

    extra = [jax.ShapeDtypeStruct((t, d), BF16)] if also_bf16 else []
    return pl.pallas_call(
        body, name=name, grid=(t // tm,),
        in_specs=([_rows(tm, a.shape[1]) for a in das] + [_whole(w) for w in ws] + [_rows(tm, d), _whole(w_norm)]
                  + ([_rows(tm, d)] if residual is not None else [])),
        out_specs=[_rows(tm, d)] * (1 + len(extra)) + [pl.BlockSpec((1, d), lambda i: (0, 0))],
        out_shape=[jax.ShapeDtypeStruct((t, d), out_dtype)] + extra + [jax.ShapeDtypeStruct((1, d), F32)],
        compiler_params=_cparams("arbitrary"),
    )(*das, *ws, x, w_norm, *([residual] if residual is not None else []))


FFN_TN = 1408


def _ffn_gate_up(u, w_gate, w_up, name, tm=1024):
    t, d = u.shape
    n = w_gate.shape[0]
    tm = min(tm, t)

    def body(u_ref, wg_ref, wu_ref, g_ref, up_ref, act_ref):
        uu = u_ref[...]
        g = lax.dot_general(uu, wg_ref[...], _DIMS["nt"], preferred_element_type=F32)
        up = lax.dot_general(uu, wu_ref[...], _DIMS["nt"], preferred_element_type=F32)
        g_b, up_b = g.astype(BF16), up.astype(BF16)
        g_ref[...] = g_b
        up_ref[...] = up_b
        gf, uf = g_b.astype(F32), up_b.astype(F32)
        act_ref[...] = (gf * jax.nn.sigmoid(gf) * uf).astype(act_ref.dtype)

    w_spec = pl.BlockSpec((FFN_TN, d), lambda i, j: (j, 0))
    o_spec = pl.BlockSpec((tm, FFN_TN), lambda i, j: (i, j))
    out = jax.ShapeDtypeStruct((t, n), BF16)
    return pl.pallas_call(
        body, name=name, grid=(t // tm, n // FFN_TN),
        in_specs=[pl.BlockSpec((tm, d), lambda i, j: (i, 0)), w_spec, w_spec], out_specs=[o_spec] * 3,
        out_shape=[out] * 3, compiler_params=_cparams("parallel", "parallel"),
    )(u, w_gate, w_up)


def _ffn_down_bwd(dh, w_down, g, up, name, tm=1024):
    t, d = dh.shape
    n = w_down.shape[0]
    tm = min(tm, t)

    def body(dh_ref, w_ref, g_ref, up_ref, dg_ref, du_ref):
        df = lax.dot_general(dh_ref[...], w_ref[...], _DIMS["nt"], preferred_element_type=F32).astype(BF16).astype(F32)
        gf = g_ref[...].astype(F32)
        uf = up_ref[...].astype(F32)
        s = jax.nn.sigmoid(gf)
        dg_ref[...] = (df * uf * s * (1.0 + gf * (1.0 - s))).astype(dg_ref.dtype)
        du_ref[...] = (df * gf * s).astype(du_ref.dtype)

    o_spec = pl.BlockSpec((tm, FFN_TN), lambda i, j: (i, j))
    out = jax.ShapeDtypeStruct((t, n), BF16)
    return pl.pallas_call(
        body, name=name, grid=(t // tm, n // FFN_TN),
        in_specs=[pl.BlockSpec((tm, d), lambda i, j: (i, 0)), pl.BlockSpec((FFN_TN, d), lambda i, j: (j, 0)),
                  o_spec, o_spec],
        out_specs=[o_spec, o_spec], out_shape=[out, out], compiler_params=_cparams("parallel", "parallel"),
    )(dh, w_down, g, up)


M_BLOCK = 512
M_CHUNKS_PER_BLOCK = M_BLOCK // M_CHUNK
M_UNROLL = 2


def _log_sigmoid(z):
    return jnp.minimum(z, 0.0) - jnp.log(1.0 + jnp.exp(-jnp.abs(z)))


def _per_head(fn):
    return jnp.stack([fn(hd) for hd in range(M_HEADS)])


def _mlstm_chunk_fwd(q, k, v, gc, c_prev, n_prev, m_prev):
    L, H = M_CHUNK, M_HEADS
    tt = lax.broadcasted_iota(jnp.int32, (L, L), 0)
    ss = lax.broadcasted_iota(jnp.int32, (L, L), 1)
    eye = tt == ss
    causal = ss <= tt
    gct = jnp.transpose(gc)
    i_col = _per_head(lambda hd: gc[:, hd:hd + 1])
    f_col = _per_head(lambda hd: gc[:, H + hd:H + hd + 1])
    i_row = _per_head(lambda hd: gct[hd:hd + 1, :])
    lf_col = _log_sigmoid(f_col)
    lf_row = _log_sigmoid(_per_head(lambda hd: gct[H + hd:H + hd + 1, :]))
    b_col = jnp.sum(jnp.where(causal, lf_row, 0.0), axis=2, keepdims=True)
    b_row = jnp.sum(jnp.where(tt <= ss, lf_col, 0.0), axis=1, keepdims=True)
    g = jnp.sum(lf_col, axis=1, keepdims=True)
    a_row = g - b_row + i_row
    a_col = g - b_col + i_col
    m_loc = jnp.max(a_row, axis=2, keepdims=True)
    d_log = jnp.where(causal, b_col - b_row + i_row, NEG_INF)
    inter = b_col + m_prev
    m_t = jnp.maximum(jnp.max(d_log, axis=2, keepdims=True), inter)
    dmat = jnp.exp(d_log - m_t)
    amat = _per_head(lambda hd: lax.dot_general(q[hd], k[hd], _DIMS["nt"], preferred_element_type=F32))
    p = dmat * amat
    sig = jnp.exp(inter - m_t)
    c_prev_b = [c.astype(BF16) for c in c_prev]
    qc = _per_head(lambda hd: jnp.dot(q[hd], c_prev_b[hd], preferred_element_type=F32))
    p_b = p.astype(BF16)
    num = _per_head(lambda hd: jnp.dot(p_b[hd], v[hd], preferred_element_type=F32)) + sig * qc
    qf = _per_head(lambda hd: q[hd].astype(F32))
    qn = jnp.sum(qf * n_prev, axis=2, keepdims=True)
    den_raw = jnp.sum(p, axis=2, keepdims=True) + sig * qn
    floor = jnp.exp(-m_t)
    den = jnp.maximum(jnp.abs(den_raw), floor)
    h = num / den
    m_new = jnp.maximum(g + m_prev, m_loc)
    w_col = jnp.exp(a_col - m_new)
    w_row = jnp.exp(a_row - m_new)
    alpha = jnp.exp(g + m_prev - m_new)
    return dict(eye=eye, causal=causal, tt=tt, ss=ss, i_col=i_col, f_col=f_col, dmat=dmat, amat=amat, p=p, p_b=p_b,
                sig=sig, qc=qc, qf=qf, qn=qn, num=num, den_raw=den_raw, floor=floor, den=den, h=h, m_new=m_new,
                w_col=w_col, w_row=w_row, alpha=alpha, c_prev_b=c_prev_b, b_row=b_row, b_col=b_col, m_t=m_t)


def _mlstm_fwd(m4, kt, gates, bias, norm_w, name):
    t = m4.shape[0]
    nblk = t // M_BLOCK
    nc = t // M_CHUNK
    L, dh = M_CHUNK, M_HEAD_DIM

    def body(m4_ref, kt_ref, g_ref, b_ref, w_ref, hm_ref, cp_ref, nm_ref, c_s, n_s, m_s):
        @pl.when(pl.program_id(0) == 0)
        def _():
            c_s[...] = jnp.zeros_like(c_s)
            n_s[...] = jnp.zeros_like(n_s)
            m_s[...] = jnp.zeros_like(m_s)

        row8 = lax.broadcasted_iota(jnp.int32, (8, dh), 0)

        def chunk(c, carry):
            rows = pl.ds(pl.multiple_of(c * L, L), L)
            gc = g_ref[rows, :] + b_ref[...]
            col = lambda part, hd: m4_ref[rows, part * M_WIDTH + hd * dh:part * M_WIDTH + (hd + 1) * dh]
            q = [col(0, hd) for hd in range(M_HEADS)]
            k = [col(1, hd) for hd in range(M_HEADS)]
            v = [col(2, hd) for hd in range(M_HEADS)]
            o = _per_head(lambda hd: col(3, hd).astype(F32))
            c_prev = [c_s[hd] for hd in range(M_HEADS)]
            n_prev = n_s[:, 0:1, :]
            m_prev = m_s[:, 0:1, 0:1]
            r = _mlstm_chunk_fwd(q, k, v, gc, c_prev, n_prev, m_prev)
            h = r["h"]
            hn = h * lax.rsqrt(jnp.mean(h * h, axis=-1, keepdims=True) + EPS) * w_ref[...]
            hm = (hn * jax.nn.sigmoid(o)).astype(hm_ref.dtype)
            nm = jnp.where(row8 == 0, n_prev, jnp.where(row8 == 1, m_prev, 0.0))
            kf = _per_head(lambda hd: k[hd].astype(F32))
            n_s[:, 0:1, :] = r["alpha"] * n_prev + jnp.sum(kf * r["w_col"], axis=1, keepdims=True)
            m_s[...] = jnp.broadcast_to(r["m_new"], (M_HEADS, 8, dh))
            for hd in range(M_HEADS):
                hm_ref[rows, hd * dh:(hd + 1) * dh] = hm[hd]
                cp_ref[hd, c] = r["c_prev_b"][hd]
                nm_ref[hd, c] = nm[hd]
                kw_t = (kt_ref[c, hd * dh:(hd + 1) * dh, :].astype(F32) * r["w_row"][hd]).astype(BF16)
                c_s[hd] = r["alpha"][hd] * c_prev[hd] + jnp.dot(kw_t, v[hd], preferred_element_type=F32)
            return carry

        lax.fori_loop(0, M_CHUNKS_PER_BLOCK, chunk, 0, unroll=M_UNROLL)

    return pl.pallas_call(
        body, name=name, grid=(nblk,),
        in_specs=[pl.BlockSpec((M_BLOCK, 4 * M_WIDTH), lambda i: (i, 0)),
                  pl.BlockSpec((M_CHUNKS_PER_BLOCK, M_WIDTH, L), lambda i: (i, 1, 0)),
                  pl.BlockSpec((M_BLOCK, LANES), lambda i: (i, 0)),
                  pl.BlockSpec((1, LANES), lambda i: (0, 0)),
                  pl.BlockSpec((M_HEADS, 1, dh), lambda i: (0, 0, 0))],
        out_specs=[pl.BlockSpec((M_BLOCK, M_WIDTH), lambda i: (i, 0)),
                   pl.BlockSpec((M_HEADS, M_CHUNKS_PER_BLOCK, dh, dh), lambda i: (0, i, 0, 0)),
                   pl.BlockSpec((M_HEADS, M_CHUNKS_PER_BLOCK, 8, dh), lambda i: (0, i, 0, 0))],
        out_shape=[jax.ShapeDtypeStruct((t, M_WIDTH), BF16),
                   jax.ShapeDtypeStruct((M_HEADS, nc, dh, dh), BF16),
                   jax.ShapeDtypeStruct((M_HEADS, nc, 8, dh), F32)],
        scratch_shapes=[pltpu.VMEM((M_HEADS, dh, dh), F32), pltpu.VMEM((M_HEADS, 8, dh), F32),
                        pltpu.VMEM((M_HEADS, 8, dh), F32)],
        compiler_params=_cparams("arbitrary"),
    )(m4, kt, gates, bias, norm_w)


def _mlstm_bwd(m4, qt, gates, bias, norm_w, c_prev_all, nm_all, dhm, name):
    t = m4.shape[0]
    nblk = t // M_BLOCK
    L, dh = M_CHUNK, M_HEAD_DIM

    def body(m4_ref, qt_ref, g_ref, b_ref, w_ref, cp_ref, nm_ref, dhm_ref, dm4_ref, dg_ref, small_ref, dc_s, dn_s):
        @pl.when(pl.program_id(0) == 0)
        def _():
            dc_s[...] = jnp.zeros_like(dc_s)
            dn_s[...] = jnp.zeros_like(dn_s)
            small_ref[...] = jnp.zeros_like(small_ref)

        lane = lax.broadcasted_iota(jnp.int32, (L, LANES), 1)
        row8 = lax.broadcasted_iota(jnp.int32, (8, LANES), 0)

        def chunk(ci, carry):
            c = M_CHUNKS_PER_BLOCK - 1 - ci
            rows = pl.ds(pl.multiple_of(c * L, L), L)
            gc = g_ref[rows, :] + b_ref[...]
            heads = range(M_HEADS)
            col = lambda part, hd: m4_ref[rows, part * M_WIDTH + hd * dh:part * M_WIDTH + (hd + 1) * dh]
            q = [col(0, hd) for hd in heads]
            k = [col(1, hd) for hd in heads]
            v = [col(2, hd) for hd in heads]
            o = _per_head(lambda hd: col(3, hd).astype(F32))
            q_t = [qt_ref[c, hd * dh:(hd + 1) * dh, :] for hd in heads]
            c_prev = [cp_ref[hd, c].astype(F32) for hd in heads]
            nm = _per_head(lambda hd: nm_ref[hd, c])
            n_prev = nm[:, 0:1, :]
            m_prev = nm[:, 1:2, 0:1]
            r = _mlstm_chunk_fwd(q, k, v, gc, c_prev, n_prev, m_prev)
            eye, tt, ss = r["eye"], r["tt"], r["ss"]
            h, den, sig, p = r["h"], r["den"], r["sig"], r["p"]
            w_col, alpha, c_prev_b = r["w_col"], r["alpha"], r["c_prev_b"]
            to_row = lambda colv: jnp.sum(jnp.where(eye, colv, 0.0), axis=1, keepdims=True)
            to_col = lambda rowv: jnp.sum(jnp.where(eye, rowv, 0.0), axis=2, keepdims=True)
            mm = lambda fn: _per_head(lambda hd: fn(hd))

            w_h = w_ref[...]
            rn = lax.rsqrt(jnp.mean(h * h, axis=-1, keepdims=True) + EPS)
            hh = h * rn
            og = jax.nn.sigmoid(o)
            dhm_c = _per_head(lambda hd: dhm_ref[rows, hd * dh:(hd + 1) * dh].astype(F32))
            dhn = dhm_c * og
            d_o = dhm_c * hh * w_h * og * (1.0 - og)
            d_norm = jnp.sum(dhn * hh, axis=1, keepdims=True)
            dhh = dhn * w_h
            dh_ = rn * (dhh - hh * jnp.mean(dhh * hh, axis=-1, keepdims=True))

            dnum = dh_ / den
            dden = -jnp.sum(dh_ * h, axis=-1, keepdims=True) / den
            dden_raw = jnp.where(jnp.abs(r["den_raw"]) >= r["floor"], dden * jnp.sign(r["den_raw"]), 0.0)
            dnum_b = dnum.astype(BF16)
            dp = mm(lambda hd: lax.dot_general(dnum_b[hd], v[hd], _DIMS["nt"], preferred_element_type=F32)) + dden_raw
            dc = [dc_s[hd] for hd in heads]
            dn = dn_s[:, 0:1, :]
            dc_b = [d.astype(BF16) for d in dc]
            g2_b = (sig * dnum).astype(BF16)
            sd = sig * dden_raw
            da_mat = (dp * r["dmat"]).astype(BF16)
            dqs = mm(lambda hd: lax.dot_general(g2_b[hd], c_prev_b[hd], _DIMS["nt"], preferred_element_type=F32)
                     + jnp.dot(da_mat[hd], k[hd], preferred_element_type=F32)) + sd * n_prev
            r_mat = mm(lambda hd: lax.dot_general(v[hd], dc_b[hd], _DIMS["nt"], preferred_element_type=F32)) + dn
            kf = _per_head(lambda hd: k[hd].astype(F32))
            dmat_t = jnp.exp(jnp.where(tt <= ss, r["b_row"] - r["b_col"] + r["i_col"], NEG_INF) - to_row(r["m_t"]))
            p_t = (dmat_t * mm(lambda hd: jnp.dot(k[hd], q_t[hd], preferred_element_type=F32))).astype(BF16)
            dp_t = mm(lambda hd: lax.dot_general(v[hd], dnum_b[hd], _DIMS["nt"], preferred_element_type=F32))
            da_t = ((dp_t + to_row(dden_raw)) * dmat_t).astype(BF16)
            d_k = mm(lambda hd: jnp.dot(da_t[hd], q[hd], preferred_element_type=F32)) + w_col * r_mat
            d_v = (mm(lambda hd: jnp.dot(p_t[hd], dnum_b[hd], preferred_element_type=F32))
                   + w_col * mm(lambda hd: jnp.dot(k[hd], dc_b[hd], preferred_element_type=F32)))
            da_col = jnp.sum(kf * r_mat, axis=-1, keepdims=True) * w_col
            dsig = jnp.sum(dnum * r["qc"], axis=-1, keepdims=True) + dden_raw * r["qn"]
            pp = dp * p
            r1 = jnp.sum(pp, axis=2, keepdims=True)
            c1_col = to_col(jnp.sum(pp, axis=1, keepdims=True))
            dcc = _per_head(lambda hd: dc[hd] * c_prev[hd])
            d_alpha = (jnp.sum(jnp.sum(dcc, axis=2, keepdims=True), axis=1, keepdims=True)
                       + jnp.sum(dn * n_prev, axis=2, keepdims=True))
            dgl = jnp.sum(da_col, axis=1, keepdims=True) + d_alpha * alpha
            db_col = r1 + dsig * sig - c1_col - da_col
            dli_col = c1_col + da_col
            dlf_col = jnp.sum(jnp.where(ss >= tt, to_row(db_col), 0.0), axis=2, keepdims=True) + dgl
            df_col = dlf_col * jax.nn.sigmoid(-r["f_col"])

            dn_s[:, 0:1, :] = alpha * dn + jnp.sum(sd * r["qf"], axis=1, keepdims=True)
            dq_out = (dqs * M_SCALE).astype(dm4_ref.dtype)
            dk_out, dv_out, do_out = d_k.astype(dm4_ref.dtype), d_v.astype(dm4_ref.dtype), d_o.astype(dm4_ref.dtype)
            dg_tile = jnp.zeros((L, LANES), F32)
            small = jnp.zeros((8, LANES), F32)
            for hd in heads:
                dc_s[hd] = alpha[hd] * dc[hd] + jnp.dot(q_t[hd], g2_b[hd], preferred_element_type=F32)
                dm4_ref[rows, hd * dh:(hd + 1) * dh] = dq_out[hd]
                dm4_ref[rows, M_WIDTH + hd * dh:M_WIDTH + (hd + 1) * dh] = dk_out[hd]
                dm4_ref[rows, 2 * M_WIDTH + hd * dh:2 * M_WIDTH + (hd + 1) * dh] = dv_out[hd]
                dm4_ref[rows, 3 * M_WIDTH + hd * dh:3 * M_WIDTH + (hd + 1) * dh] = do_out[hd]
                dg_tile = (dg_tile + jnp.where(lane == hd, dli_col[hd], 0.0)
                           + jnp.where(lane == M_HEADS + hd, df_col[hd], 0.0))
                small = small + jnp.where(row8 == hd, d_norm[hd], 0.0)
            dg_ref[rows, :] = dg_tile
            small = small + jnp.where(row8 == M_HEADS, jnp.sum(dg_tile, axis=0, keepdims=True), 0.0)
            small_ref[...] += small
            return carry

        lax.fori_loop(0, M_CHUNKS_PER_BLOCK, chunk, 0, unroll=M_UNROLL)

    rev = lambda i: nblk - 1 - i
    return pl.pallas_call(
        body, name=name, grid=(nblk,),
        in_specs=[pl.BlockSpec((M_BLOCK, 4 * M_WIDTH), lambda i: (rev(i), 0)),
                  pl.BlockSpec((M_CHUNKS_PER_BLOCK, M_WIDTH, L), lambda i: (rev(i), 0, 0)),
                  pl.BlockSpec((M_BLOCK, LANES), lambda i: (rev(i), 0)),
                  pl.BlockSpec((1, LANES), lambda i: (0, 0)),
                  pl.BlockSpec((M_HEADS, 1, dh), lambda i: (0, 0, 0)),
                  pl.BlockSpec((M_HEADS, M_CHUNKS_PER_BLOCK, dh, dh), lambda i: (0, rev(i), 0, 0)),
                  pl.BlockSpec((M_HEADS, M_CHUNKS_PER_BLOCK, 8, dh), lambda i: (0, rev(i), 0, 0)),
                  pl.BlockSpec((M_BLOCK, M_WIDTH), lambda i: (rev(i), 0))],
        out_specs=[pl.BlockSpec((M_BLOCK, 4 * M_WIDTH), lambda i: (rev(i), 0)),
                   pl.BlockSpec((M_BLOCK, LANES), lambda i: (rev(i), 0)),
                   pl.BlockSpec((8, LANES), lambda i: (0, 0))],
        out_shape=[jax.ShapeDtypeStruct((t, 4 * M_WIDTH), BF16), jax.ShapeDtypeStruct((t, LANES), F32),
                   jax.ShapeDtypeStruct((8, LANES), F32)],
        scratch_shapes=[pltpu.VMEM((M_HEADS, dh, dh), F32), pltpu.VMEM((M_HEADS, 8, dh), F32)],
        compiler_params=_cparams("arbitrary"),
    )(m4, qt, gates, bias, norm_w, c_prev_all, nm_all, dhm)


def _rope_tables(pos_col, width):
    lane = lax.broadcasted_iota(jnp.int32, (1, width), 1) % A_QK_PAD
    half = A_ROPE // 2
    first = (lane >= A_NOPE) & (lane < A_NOPE + half)
    second = (lane >= A_NOPE + half) & (lane < A_NOPE + A_ROPE)
    idx = jnp.where(first, lane - A_NOPE, lane - A_NOPE - half).astype(F32)
    inv_freq = jnp.exp(idx * (-math.log(ROPE_THETA) / half))
    ang = pos_col.astype(F32) * inv_freq
    cos, sin = jnp.cos(ang), jnp.sin(ang)
    rot = first | second
    return jnp.where(rot, cos, 1.0), jnp.where(first, -sin, 0.0), jnp.where(second, sin, 0.0)


def _rope_apply(vv, cosf, s1, s2):
    half = A_ROPE // 2
    w = vv.shape[-1]
    return vv * cosf + pltpu.roll(vv, w - half, 1) * s1 + pltpu.roll(vv, half, 1) * s2


def _rope_apply_t(dd, cosf, s1, s2):
    half = A_ROPE // 2
    w = dd.shape[-1]
    return dd * cosf + pltpu.roll(dd * s1, half, 1) + pltpu.roll(dd * s2, w - half, 1)


A_BIAS_LANE_K = A_NOPE + A_ROPE
A_BIAS_LANE_V = A_V


def _hi_lo(val):
    hi = val.astype(BF16)
    return hi, (val - hi.astype(F32)).astype(BF16)


def _qkv_up_rope(qa_n, kv_n, w_qb, w_k, w_v, gates, pos, name, tm=512):
    t = qa_n.shape[0]
    w = A_HEADS * LANES

    def body(qa_ref, kv_ref, wq_ref, wk_ref, wv_ref, g_ref, p_ref, qo_ref, ko_ref, vo_ref):
        cosf, s1, s2 = _rope_tables(p_ref[...], LANES)
        lane = lax.broadcasted_iota(jnp.int32, (1, LANES), 1)
        kr = jnp.where((lane >= A_NOPE) & (lane < A_NOPE + A_ROPE), g_ref[...], 0.0)
        kr = _rope_apply(kr, cosf, s1, s2)
        kr = jnp.where((lane == A_BIAS_LANE_K) | (lane == A_BIAS_LANE_K + 1), 1.0, kr)
        v_one = (lane == A_BIAS_LANE_V) | (lane == A_BIAS_LANE_V + 1)
        qpre = lax.dot_general(qa_ref[...], wq_ref[...], _DIMS["nt"], preferred_element_type=F32)
        kv = kv_ref[...]
        kpre = lax.dot_general(kv, wk_ref[...], _DIMS["nt"], preferred_element_type=F32)
        vpre = lax.dot_general(kv, wv_ref[...], _DIMS["nt"], preferred_element_type=F32)
        for hd in range(A_HEADS):
            sl = slice(hd * LANES, (hd + 1) * LANES)
            qo_ref[:, sl] = (_rope_apply(qpre[:, sl], cosf, s1, s2) * ATTN_SCALE).astype(qo_ref.dtype)
            ko_ref[:, sl] = (kpre[:, sl] + kr).astype(ko_ref.dtype)
            vo_ref[:, sl] = jnp.where(v_one, 1.0, vpre[:, sl]).astype(vo_ref.dtype)

    rows = lambda width: pl.BlockSpec((tm, width), lambda i: (i, 0))
    whole = lambda arr: pl.BlockSpec(arr.shape, lambda i: (0, 0))
    out = jax.ShapeDtypeStruct((t, w), BF16)
    return pl.pallas_call(
        body, name=name, grid=(t // tm,),
        in_specs=[rows(qa_n.shape[1]), rows(kv_n.shape[1]), whole(w_qb), whole(w_k), whole(w_v), rows(LANES), rows(1)],
        out_specs=[rows(w)] * 3, out_shape=[out] * 3, compiler_params=_cparams("parallel"),
    )(qa_n, kv_n, w_qb, w_k, w_v, gates, pos)


def _rope_bwd(dq, dk, dgates, pos, name, tm=512):
    t, w = dq.shape

    def body(dq_ref, dk_ref, dg_ref, p_ref, dqo_ref, dgo_ref):
        cosf, s1, s2 = _rope_tables(p_ref[...], LANES)
        acc = jnp.zeros((tm, LANES), F32)
        for hd in range(A_HEADS):
            sl = slice(hd * LANES, (hd + 1) * LANES)
            dqo_ref[:, sl] = (_rope_apply_t(dq_ref[:, sl], cosf, s1, s2) * ATTN_SCALE).astype(dqo_ref.dtype)
            acc = acc + dk_ref[:, sl].astype(F32)
        lane = lax.broadcasted_iota(jnp.int32, (1, LANES), 1)
        dkr = _rope_apply_t(acc, cosf, s1, s2)
        dkr = jnp.where((lane >= A_NOPE) & (lane < A_NOPE + A_ROPE), dkr, 0.0)
        dgo_ref[...] = (dg_ref[...] + dkr).astype(dgo_ref.dtype)

    row = pl.BlockSpec((tm, w), lambda i: (i, 0))
    nar = pl.BlockSpec((tm, LANES), lambda i: (i, 0))
    return pl.pallas_call(
        body, name=name, grid=(t // tm,),
        in_specs=[row, row, nar, pl.BlockSpec((tm, 1), lambda i: (i, 0))],
        out_specs=[row, nar],
        out_shape=[jax.ShapeDtypeStruct((t, w), BF16), jax.ShapeDtypeStruct((t, LANES), BF16)],
        compiler_params=_cparams("parallel"),
    )(dq, dk, dgates, pos)


A_TQ = 512
A_TK = 512
A_TQ_FWD = 1024


A_HEADS_PER_STEP = 2
A_HEADS_PER_STEP_FWD = 4


def _flash_fwd(q, k, v, name):
    t = q.shape[0]
    tq, tk = min(A_TQ_FWD, t), A_TK
    nh = A_HEADS_PER_STEP_FWD
    wblk = nh * LANES

    def body(q_ref, k_ref, v_ref, o_ref, qb_ref):
        i = pl.program_id(1)
        lane = lax.broadcasted_iota(jnp.int32, (tq, LANES), 1)
        qpos = i * tq + lax.broadcasted_iota(jnp.int32, (tq, tk), 0)
        kofs = lax.broadcasted_iota(jnp.int32, (tq, tk), 1)
        qs = [q_ref[:, hh * LANES:(hh + 1) * LANES] for hh in range(nh)]

        def step(kb, carry, masked):
            rows = pl.ds(pl.multiple_of(kb * tk, tk), tk)
            new = []
            for hh in range(nh):
                m, acc = carry[hh]
                kh = k_ref[rows, hh * LANES:(hh + 1) * LANES]
                s = lax.dot_general(qs[hh], kh, _DIMS["nt"], preferred_element_type=F32)
                if masked:
                    s = jnp.where(kb * tk + kofs <= qpos, s, NEG_INF)
                m_new = jnp.maximum(m, jnp.max(s, axis=1, keepdims=True))
                p = jnp.exp(s - m_new)
                acc = jnp.exp(m - m_new) * acc + jnp.dot(p.astype(BF16), v_ref[rows, hh * LANES:(hh + 1) * LANES],
                                                          preferred_element_type=F32)
                new.append((m_new, acc))
            return tuple(new)

        carry = tuple((jnp.full((tq, 1), NEG_INF, F32), jnp.zeros((tq, LANES), F32)) for _ in range(nh))
        n_full = (i * tq) // tk
        carry = lax.fori_loop(0, n_full, functools.partial(step, masked=False), carry)
        for d in range(tq // tk):
            carry = step(n_full + d, carry, True)
        for hh in range(nh):
            m, acc = carry[hh]
            l = acc[:, A_BIAS_LANE_V:A_BIAS_LANE_V + 1]
            o_ref[:, hh * LANES:(hh + 1) * LANES] = (acc / l).astype(o_ref.dtype)
            hi, lo = _hi_lo(-(m + jnp.log(l)))
            qb_ref[:, hh * LANES:(hh + 1) * LANES] = jnp.where(
                lane == A_BIAS_LANE_K, hi, jnp.where(lane == A_BIAS_LANE_K + 1, lo, qs[hh]))

    blk = pl.BlockSpec((tq, wblk), lambda j, i: (i, j))
    res = pl.BlockSpec((t, wblk), lambda j, i: (0, j))
    out = jax.ShapeDtypeStruct((t, A_HEADS * LANES), BF16)
    return pl.pallas_call(
        body, name=name, grid=(A_HEADS // nh, t // tq), in_specs=[blk, res, res], out_specs=[blk, blk],
        out_shape=[out, out], compiler_params=_cparams("parallel", "parallel"),
    )(q, k, v)


def _d_out_proj(dh, w_out_m, w_out_a, o, name, tm=512):
    t, w = o.shape
    tm = min(tm, t)

    def body(dh_ref, wm_ref, w_ref, o_ref, dhm_ref, out_ref):
        dh_blk = dh_ref[...]
        dhm_ref[...] = lax.dot_general(dh_blk, wm_ref[...], _DIMS["nt"], preferred_element_type=F32).astype(BF16)
        do = lax.dot_general(dh_blk, w_ref[...], _DIMS["nt"], preferred_element_type=F32).astype(BF16)
        lane = lax.broadcasted_iota(jnp.int32, (tm, LANES), 1)
        for hd in range(A_HEADS):
            sl = slice(hd * LANES, (hd + 1) * LANES)
            d = do[:, sl]
            delta = jnp.sum(jnp.where(lane < A_V, d.astype(F32) * o_ref[:, sl].astype(F32), 0.0), axis=1, keepdims=True)
            hi, lo = _hi_lo(-delta)
            out_ref[:, sl] = jnp.where(lane == A_BIAS_LANE_V, hi, jnp.where(lane == A_BIAS_LANE_V + 1, lo, d))

    return pl.pallas_call(
        body, name=name, grid=(t // tm,),
        in_specs=[_rows(tm, dh.shape[1]), _whole(w_out_m), _whole(w_out_a), _rows(tm, w)],
        out_specs=[_rows(tm, w_out_m.shape[0]), _rows(tm, w)],
        out_shape=[jax.ShapeDtypeStruct((t, w_out_m.shape[0]), BF16), jax.ShapeDtypeStruct((t, w), BF16)],
        compiler_params=_cparams("parallel"),
    )(dh, w_out_m, w_out_a, o)


def _flash_bwd(qb, k, v, doe, name):
    t = qb.shape[0]
    tq, tk = A_TQ, A_TK
    assert tq == tk
    nh = A_HEADS_PER_STEP
    wblk = nh * LANES
    nq = t // tq

    def body(q_ref, k_ref, v_ref, do_ref, dq_ref, dk_ref, dv_ref):
        kb = pl.program_id(1)

        @pl.when(kb == 0)
        def _():
            dq_ref[...] = jnp.zeros_like(dq_ref)

        kpos = kb * tk + lax.broadcasted_iota(jnp.int32, (tk, tq), 0)
        qofs = lax.broadcasted_iota(jnp.int32, (tk, tq), 1)
        ks = [k_ref[:, hh * LANES:(hh + 1) * LANES] for hh in range(nh)]
        vs = [v_ref[:, hh * LANES:(hh + 1) * LANES] for hh in range(nh)]

        def step(qi, carry, masked):
            rows = pl.ds(pl.multiple_of(qi * tq, tq), tq)
            new = []
            for hh in range(nh):
                dk_acc, dv_acc = carry[hh]
                qh = q_ref[rows, hh * LANES:(hh + 1) * LANES]
                doh = do_ref[rows, hh * LANES:(hh + 1) * LANES]
                p_t = jnp.exp(lax.dot_general(ks[hh], qh, _DIMS["nt"], preferred_element_type=F32))
                if masked:
                    p_t = jnp.where(qi * tq + qofs >= kpos, p_t, 0.0)
                ds_t = (p_t * lax.dot_general(vs[hh], doh, _DIMS["nt"], preferred_element_type=F32)).astype(BF16)
                dv_acc = dv_acc + jnp.dot(p_t.astype(BF16), doh, preferred_element_type=F32)
                dk_acc = dk_acc + jnp.dot(ds_t, qh, preferred_element_type=F32)
                dq_ref[rows, hh * LANES:(hh + 1) * LANES] += lax.dot_general(ds_t, ks[hh], _DIMS["tn"],
                                                                             preferred_element_type=F32)
                new.append((dk_acc, dv_acc))
            return tuple(new)

        carry = tuple((jnp.zeros((tk, LANES), F32), jnp.zeros((tk, LANES), F32)) for _ in range(nh))
        carry = step(kb, carry, True)
        carry = lax.fori_loop(kb + 1, nq, functools.partial(step, masked=False), carry)
        for hh in range(nh):
            dk_ref[:, hh * LANES:(hh + 1) * LANES] = carry[hh][0].astype(dk_ref.dtype)
            dv_ref[:, hh * LANES:(hh + 1) * LANES] = carry[hh][1].astype(dv_ref.dtype)

    res = pl.BlockSpec((t, wblk), lambda j, kb: (0, j))
    blk = pl.BlockSpec((tk, wblk), lambda j, kb: (kb, j))
    wide = (t, A_HEADS * LANES)
    return pl.pallas_call(
        body, name=name, grid=(A_HEADS // nh, t // tk), in_specs=[res, blk, blk, res], out_specs=[res, blk, blk],
        out_shape=[jax.ShapeDtypeStruct(wide, F32), jax.ShapeDtypeStruct(wide, BF16), jax.ShapeDtypeStruct(wide, BF16)],
        compiler_params=_cparams("parallel", "arbitrary"),
    )(qb, k, v, doe)


_SPLITS = (M_WIDTH, M_WIDTH, M_WIDTH, M_WIDTH, M_HEADS, M_HEADS, Q_RANK, KV_RANK, A_ROPE)
_OFFS = tuple(sum(_SPLITS[:i]) for i in range(len(_SPLITS) + 1))
_GATE_BLOCK_KR = A_NOPE


def _split_w_in_t(wt):
    z = lambda n: jnp.zeros((n, wt.shape[1]), wt.dtype)
    w_g = jnp.concatenate([wt[_OFFS[4]:_OFFS[6]], z(_GATE_BLOCK_KR - 2 * M_HEADS), wt[_OFFS[8]:_OFFS[9]],
                           z(LANES - _GATE_BLOCK_KR - A_ROPE)], axis=0)
    return wt[_OFFS[6]:_OFFS[8]], w_g


def _merge_w_in_grad_t(g_m4, g_a2, g_g):
    return jnp.concatenate([g_m4, g_g[:2 * M_HEADS], g_a2, g_g[_GATE_BLOCK_KR:_GATE_BLOCK_KR + A_ROPE]], axis=0)


def _pad_heads(w, used):
    w3 = w.reshape(A_HEADS, used, w.shape[1])
    return jnp.pad(w3, ((0, 0), (0, LANES - used), (0, 0))).reshape(A_HEADS * LANES, w.shape[1])


def _unpad_heads(g, used):
    return g.reshape(A_HEADS, LANES, g.shape[1])[:, :used].reshape(A_HEADS * used, g.shape[1])


def _split_w_kv_b_t(wt):
    w3 = wt.reshape(A_HEADS, A_NOPE + A_V, wt.shape[1])
    pad = lambda part: jnp.pad(part, ((0, 0), (0, LANES - part.shape[1]), (0, 0))).reshape(A_HEADS * LANES, wt.shape[1])
    return pad(w3[:, :A_NOPE]), pad(w3[:, A_NOPE:])


def _merge_w_kv_b_grad_t(gk, gv):
    gk3 = gk.reshape(A_HEADS, LANES, gk.shape[1])[:, :A_NOPE]
    gv3 = gv.reshape(A_HEADS, LANES, gv.shape[1])[:, :A_V]
    return jnp.concatenate([gk3, gv3], axis=1).reshape(A_HEADS * (A_NOPE + A_V), gk.shape[1])


def _local_step(x, pos, target, nw, w, start_late_reduce):
    w_in = w["w_in"]
    w_a2, w_g = _split_w_in_t(w_in)
    w_qb = _pad_heads(w["w_q_b"], A_NOPE + A_ROPE)
    w_k, w_v = _split_w_kv_b_t(w["w_kv_b"])
    w_out_m, w_out_a = w["w_out"][:M_WIDTH], _pad_heads(w["w_out"][M_WIDTH:], A_V)
    w_gate, w_up, w_down = w["w_gate"], w["w_up"], w["w_down"]
    bias = jnp.pad(nw["b_gates"], ((0, 0), (0, LANES - 2 * M_HEADS)))
    mnorm = nw["mlstm_norm_w"].reshape(M_HEADS, 1, M_HEAD_DIM)
    n_m4 = 4 * M_WIDTH

    u1, a2, qa_n, kv_n, gates = _norm_proj_latents(x, nw["attn_norm_w"], w_a2, w_g, nw["q_a_norm_w"],
                                                   nw["kv_a_norm_w"], "norm_proj_latents")
    m4 = _matmul(u1, w_in, "nt", BF16, "proj_mlstm", b_rows=n_m4, lead_scale=(M_WIDTH, M_SCALE))
    qk_t = _matmul(w_in, u1, "nt", BF16, "proj_qk_t", a_rows=2 * M_WIDTH, lead_row_scale=(M_WIDTH, M_SCALE),
                   col_chunk=M_CHUNK)
    hm, c_prev_all, nm_all = _mlstm_fwd(m4, qk_t, gates, bias, mnorm, "mlstm_fwd")
    qa, kva = a2[:, :Q_RANK], a2[:, Q_RANK:]
    qr, kr, vv = _qkv_up_rope(qa_n, kv_n, w_qb, w_k, w_v, gates, pos, "qkv_up_rope")
    ha, qb = _flash_fwd(qr, kr, vv, "attn_fwd")
    h1, u2 = _out_proj_norm(hm, ha, w_out_m, w_out_a, x, nw["ffn_norm_w"], "out_proj")
    gg, uu, act = _ffn_gate_up(u2, w_gate, w_up, "ffn_gate_up")

    dh2, dh2_b, loss, g_final = _ffn_down_loss(act, w_down, h1, target, nw["final_norm_w"].reshape(1, D_MODEL),
                                               "ffn_down_loss")
    grads = {"final_norm_w": g_final.reshape(D_MODEL)}
    grads_b = {}
    dgg, duu = _ffn_down_bwd(dh2_b, w_down, gg, uu, "ffn_down_bwd")
    grads["w_down"], grads_b["w_down"] = _matmul(act, dh2_b, "tn", F32, "g_w_down", tm=256, tn=D_MODEL, also_bf16=True)
    dh1, dh1_b, grads["ffn_norm_w"] = _matmul_norm_bwd([dgg, duu], [w_gate, w_up], h1, nw["ffn_norm_w"], "d_u2", F32,
                                                       residual=dh2, also_bf16=True, tm=256)
    grads["w_gate"], grads_b["w_gate"] = _matmul(dgg, u2, "tn", F32, "g_w_gate", tm=256, tn=D_MODEL, also_bf16=True)
    grads["w_up"], grads_b["w_up"] = _matmul(duu, u2, "tn", F32, "g_w_up", tm=256, tn=D_MODEL, also_bf16=True)
    dhm, doe = _d_out_proj(dh1_b, w_out_m, w_out_a, ha, "d_out_proj")
    grads["w_out"] = jnp.concatenate([_matmul(hm, dh1_b, "tn", F32, "g_w_out_m", tn=D_MODEL),
                                      _unpad_heads(_matmul(ha, dh1_b, "tn", F32, "g_w_out_a", tn=D_MODEL), A_V)], axis=0)

    late, doe = start_late_reduce(grads, grads_b, doe)

    dqr, dkr, dvv = _flash_bwd(qb, kr, vv, doe, "attn_bwd")
    dm4, dgates, small = _mlstm_bwd(m4, qk_t, gates, bias, mnorm, c_prev_all, nm_all, dhm, "mlstm_bwd")
    grads["mlstm_norm_w"] = small[:M_HEADS].reshape(1, M_HEADS, M_HEAD_DIM)
    grads["b_gates"] = small[M_HEADS:M_HEADS + 1, :2 * M_HEADS]
    dqpre, dgk = _rope_bwd(dqr, dkr, dgates, pos, "rope_bwd")
    grads["w_q_b"] = _unpad_heads(_matmul(dqpre, qa_n, "tn", F32, "g_w_q_b"), A_NOPE + A_ROPE)
    grads["w_kv_b"] = _merge_w_kv_b_grad_t(_matmul(dkr, kv_n, "tn", F32, "g_w_k"),
                                           _matmul(dvv, kv_n, "tn", F32, "g_w_v"))
    dqa, grads["q_a_norm_w"] = _matmul_norm_bwd([dqpre], [w_qb], qa, nw["q_a_norm_w"], "d_qa", BF16)
    dkva, grads["kv_a_norm_w"] = _matmul_norm_bwd([dkr, dvv], [w_k, w_v], kva, nw["kv_a_norm_w"], "d_kva", BF16)
    da2 = jnp.concatenate([dqa, dkva], axis=1)
    grads["w_in"] = _merge_w_in_grad_t(_matmul(dm4, u1, "tn", F32, "g_w_m4", tn=D_MODEL),
                                       _matmul(da2, u1, "tn", F32, "g_w_a2", tm=640, tn=D_MODEL),
                                       _matmul(dgk, u1, "tn", F32, "g_w_g", tn=D_MODEL))
    grad_x, grads["attn_norm_w"] = _matmul_norm_bwd([dm4, da2, dgk], [w_in, w_a2, w_g], x, nw["attn_norm_w"], "d_u1",
                                                    F32, residual=dh1, w_rows=[n_m4, None, None])
    return loss, grad_x, grads, grads_b, late


MESH = pl.DeviceIdType.MESH
N_CHIPS = 4
EARLY = ("w_in", "w_q_b", "w_kv_b")
LATE = ("w_out", "w_gate", "w_up", "w_down")
BIG = EARLY + LATE
TRANSPOSED = ("w_in", "w_q_b", "w_kv_b", "w_gate", "w_up")
LANE_HALVED = ("w_in", "w_out", "w_gate", "w_up", "w_down")
SMALL = ("attn_norm_w", "b_gates", "mlstm_norm_w", "q_a_norm_w", "kv_a_norm_w", "ffn_norm_w", "final_norm_w")
HBM_SPEC = pl.BlockSpec(memory_space=pltpu.HBM)


def _stored(name, a):
    return a[0].T if name in TRANSPOSED else a[0]


def _unstored(name, a):
    return (a.T if name in TRANSPOSED else a)[None]


def _half_shape(name, shape):
    rs, cs = shape
    return (rs, cs // 2) if name in LANE_HALVED else (rs // 2, cs)


def _half(ref, name, h, *lead):
    rs, cs = ref.shape[-2:]
    if name in LANE_HALVED:
        return ref.at[(*lead, slice(None), pl.ds(h * (cs // 2), cs // 2))]
    return ref.at[(*lead, pl.ds(h * (rs // 2), rs // 2), slice(None))]


def _place():
    x, y, c = lax.axis_index("x"), lax.axis_index("y"), lax.axis_index("c")
    others = [(1 - x, y), (x, 1 - y), (1 - x, 1 - y)]
    return x, y, c, others


def _gather_weights(names, shards):
    n = len(shards)

    def body(*refs):
        _gather_body(names, refs[:n], refs[n:2 * n], *refs[2 * n:])

    gathered = pl.pallas_call(
        body, name="gather_weights", in_specs=[HBM_SPEC] * n, out_specs=[HBM_SPEC] * n,
        out_shape=[jax.ShapeDtypeStruct((N_CHIPS,) + s.shape, s.dtype) for s in shards],
        scratch_shapes=[pltpu.SemaphoreType.DMA((6 * n,)), pltpu.SemaphoreType.DMA((6 * n,))],
    )(*shards)
    return _with_own_slab(gathered, shards)


def _with_own_slab(gathered, shards):
    me = 2 * lax.axis_index("x") + lax.axis_index("y")
    return [lax.dynamic_update_slice(g, s[None], (me, 0, 0)) for g, s in zip(gathered, shards)]


def _gather_body(names, ins, outs, send_sems, recv_sems):
    x, y, c, others = _place()
    me = 2 * x + y
    sibling = (x, y, 1 - c)

    def copy(w, k, slab, core, to, src=None):
        dst = _half(outs[w], names[w], core, slab)
        return pltpu.make_async_remote_copy(
            src_ref=dst if src is None else src, dst_ref=dst, send_sem=send_sems.at[w * 6 + k],
            recv_sem=recv_sems.at[w * 6 + k], device_id=to, device_id_type=MESH)

    sends = []
    for w in range(len(names)):
        for j, chip in enumerate(others):
            cp = copy(w, j, me, c, (*chip, c), src=_half(ins[w], names[w], c))
            cp.start()
            sends.append(cp)
    for w in range(len(names)):
        for j, (ox, oy) in enumerate(others):
            slab = 2 * ox + oy
            copy(w, j, slab, c, (x, y, c)).wait_recv()
            fwd = copy(w, 3 + j, slab, c, sibling)
            fwd.start()
            sends.append(fwd)
    for w in range(len(names)):
        for j, (ox, oy) in enumerate(others):
            copy(w, 3 + j, 2 * ox + oy, 1 - c, (x, y, c)).wait_recv()
    for cp in sends:
        cp.wait_send()


GATHER_LATE_COLLECTIVE_ID = 1


def _gather_weights_async(names, shards):
    n = len(shards)
    src = [jax.new_ref(s, memory_space=pltpu.MemorySpace.HBM) for s in shards]
    dst = [jax.empty_ref(jax.ShapeDtypeStruct((N_CHIPS,) + s.shape, s.dtype), memory_space=pltpu.MemorySpace.HBM)
           for s in shards]

    @pl.kernel(mesh=plsc.ScalarSubcoreMesh(axis_name="sequencer", num_cores=1), name="gather_weights_async",
               scratch_types=(pltpu.SemaphoreType.DMA((6 * n,)), pltpu.SemaphoreType.DMA((6 * n,))),
               compiler_params=pltpu.CompilerParams(collective_id=GATHER_LATE_COLLECTIVE_ID))
    def launch(send_sems, recv_sems):
        x, y, c, others = _place()
        peers = [(ox, oy, c) for ox, oy in others] + [(x, y, 1 - c)]
        barrier = pltpu.get_barrier_semaphore()
        for peer in peers:
            pl.semaphore_signal(barrier, inc=1, device_id=peer, device_id_type=MESH)
        pl.semaphore_wait(barrier, len(peers))
        _gather_body(names, src, dst, send_sems, recv_sems)

    launch()
    return _with_own_slab([d[...] for d in dst], shards)


def _exchange(arrays, out_shapes, plan, copies_per_array, name):
    n = len(arrays)

    def body(*refs):
        ins, outs = refs[:n], refs[n:2 * n]
        send_sems, recv_sems = refs[2 * n:]
        rem = [pltpu.make_async_remote_copy(src_ref=s, dst_ref=d, send_sem=send_sems.at[k], recv_sem=recv_sems.at[k],
                                            device_id=to, device_id_type=MESH)
               for k, (s, d, to) in enumerate(plan(ins, outs, _place()))]
        for cp in rem:
            cp.start()
        for cp in rem:
            cp.wait_recv()
        for cp in rem:
            cp.wait_send()

    return pl.pallas_call(
        body, name=name, in_specs=[HBM_SPEC] * n, out_specs=[HBM_SPEC] * n, out_shape=out_shapes,
        scratch_shapes=[pltpu.SemaphoreType.DMA((copies_per_array * n,)),
                        pltpu.SemaphoreType.DMA((copies_per_array * n,))],
    )(*arrays)


def _plan_to_sibling(ins, outs, place):
    x, y, c, _ = place
    return [(ins[w], outs[w], (x, y, 1 - c)) for w in range(len(ins))]


N_PEERS = 7


def _shard_sum_direct(g3, got7, name, where):
    hs = tuple(got7.shape[1:])
    if name in LANE_HALVED:
        g_in, g_spec = g3, pl.BlockSpec((1,) + hs, lambda i, p: (p[0], 0, p[1]))
    else:
        g_in, g_spec = g3.reshape((N_CHIPS, 2) + hs), pl.BlockSpec((1, 1) + hs, lambda i, p: (p[0], p[1], 0, 0))

    def body(p_ref, g_ref, r_ref, o_ref):
        acc = g_ref[(0,) * (len(g_ref.shape) - 2)]
        for k in range(N_PEERS):
            acc = acc + r_ref[k].astype(F32)
        o_ref[...] = acc

    return pl.pallas_call(
        body, name="shard_sum_%s" % name,
        grid_spec=pltpu.PrefetchScalarGridSpec(
            num_scalar_prefetch=1, grid=(1,),
            in_specs=[g_spec, pl.BlockSpec((N_PEERS,) + hs, lambda i, p: (0, 0, 0))],
            out_specs=pl.BlockSpec(hs, lambda i, p: (0, 0))),
        out_shape=jax.ShapeDtypeStruct(hs, F32), compiler_params=_cparams("arbitrary"),
    )(where, g_in, got7)


def _adamw_halves(w, m, v, mine, other, name, core):
    hs = tuple(mine.shape)
    full = pl.BlockSpec(hs, (lambda h, c_ref: (0, h)) if name in LANE_HALVED else (lambda h, c_ref: (h, 0)))
    half = pl.BlockSpec(hs, lambda h, c_ref: (0, 0))

    def body(c_ref, w_ref, m_ref, v_ref, a_ref, b_ref, g_ref, d_ref, mo_ref, vo_ref):
        g = jnp.where(pl.program_id(0) == c_ref[0], a_ref[...], b_ref[...])
        delta, m_new, v_new = _adamw_math(w_ref[...], g, m_ref[...], v_ref[...])
        g_ref[...] = g
        d_ref[...] = delta
        mo_ref[...] = m_new
        vo_ref[...] = v_new

    out = jax.ShapeDtypeStruct(w.shape, F32)
    return pl.pallas_call(
        body, name="adamw_%s" % name,
        grid_spec=pltpu.PrefetchScalarGridSpec(num_scalar_prefetch=1, grid=(2,), in_specs=[full, full, full, half, half],
                                               out_specs=[full] * 4),
        out_shape=[out] * 4, compiler_params=_cparams("parallel"),
    )(core, w, m, v, mine, other)


def _adamw_math(w, g, m, v):
    m = ADAM_B1 * m + (1.0 - ADAM_B1) * g
    v = ADAM_B2 * v + (1.0 - ADAM_B2) * (g * g)
    m_hat = m / (1.0 - ADAM_B1 ** ADAM_STEP)
    v_hat = v / (1.0 - ADAM_B2 ** ADAM_STEP)
    delta = -ADAM_LR * (m_hat / (jnp.sqrt(v_hat) + ADAM_EPS) + ADAM_WD * w)
    return delta, m, v


def _core_index():
    return lax.axis_index("c").astype(jnp.int32).reshape(1)


DIRECT_REDUCE_COLLECTIVE_ID = {"late": 2, "early": 3}


def _exchange_contributions_async(names, gb3, tag):
    n = len(gb3)
    src = [jax.new_ref(g, memory_space=pltpu.MemorySpace.HBM) for g in gb3]
    dst = [jax.empty_ref(jax.ShapeDtypeStruct((N_PEERS,) + _half_shape(nm, g.shape[1:]), g.dtype),
                         memory_space=pltpu.MemorySpace.HBM) for nm, g in zip(names, gb3)]

    @pl.kernel(mesh=plsc.ScalarSubcoreMesh(axis_name="sequencer", num_cores=1), name="contributions_async_" + tag,
               scratch_types=(pltpu.SemaphoreType.DMA((N_PEERS * n,)), pltpu.SemaphoreType.DMA((N_PEERS * n,))),
               compiler_params=pltpu.CompilerParams(collective_id=DIRECT_REDUCE_COLLECTIVE_ID[tag]))
    def launch(send_sems, recv_sems):
        x, y, c, _ = _place()
        peers = [(x ^ ((k >> 2) & 1), y ^ ((k >> 1) & 1), c ^ (k & 1)) for k in range(1, N_PEERS + 1)]
        barrier = pltpu.get_barrier_semaphore()
        for peer in peers:
            pl.semaphore_signal(barrier, inc=1, device_id=peer, device_id_type=MESH)
        pl.semaphore_wait(barrier, N_PEERS)
        rem = []
        for w in range(n):
            for k, (px, py, pc) in enumerate(peers):
                rem.append(pltpu.make_async_remote_copy(
                    src_ref=_half(src[w], names[w], pc, 2 * px + py), dst_ref=dst[w].at[k],
                    send_sem=send_sems.at[N_PEERS * w + k], recv_sem=recv_sems.at[N_PEERS * w + k],
                    device_id=(px, py, pc), device_id_type=MESH))
        for cp in rem:
            cp.start()
        for cp in rem:
            cp.wait_recv()
        for cp in rem:
            cp.wait_send()

    launch()
    return [d[...] for d in dst]


def _shard_index():
    return (2 * lax.axis_index("x") + lax.axis_index("y")).astype(jnp.int32).reshape(1)


def _with_sibling_half(names, mine, tag):
    other = _exchange(mine, [jax.ShapeDtypeStruct(m.shape, F32) for m in mine], _plan_to_sibling, 1,
                      "sibling_result_" + tag)
    return {n: (a, b) for n, a, b in zip(names, mine, other)}


SMALL_ROWS = 8
SMALL_LAYOUT = {"attn_norm_w": (0, 0, 1024), "ffn_norm_w": (1, 0, 1024), "final_norm_w": (2, 0, 1024),
                "q_a_norm_w": (3, 0, 384), "kv_a_norm_w": (3, 384, 256), "mlstm_norm_w": (4, 0, 512),
                "b_gates": (4, 512, 8)}
LOSS_SLOT = (5, 0)


def _small_allreduce_adamw(grads, loss, wts, mom, vel):
    shapes = {n: wts[n].shape for n in SMALL}
    flat = lambda d: [d[n].reshape(1, SMALL_LAYOUT[n][2]) for n in SMALL]
    ns = len(SMALL)

    def body(*refs):
        g_in, loss_ref = refs[:ns], refs[ns]
        w_in_, m_in, v_in = refs[ns + 1:2 * ns + 1], refs[2 * ns + 1:3 * ns + 1], refs[3 * ns + 1:4 * ns + 1]
        outs = refs[4 * ns + 1:8 * ns + 2]
        tile, slots, send_sems, recv_sems = refs[8 * ns + 2:]
        x, y, c, _ = _place()
        me = 4 * x + 2 * y + c
        tile[...] = jnp.zeros_like(tile)
        for n, g_ref in zip(SMALL, g_in):
            r, c0, width = SMALL_LAYOUT[n]
            tile[r:r + 1, c0:c0 + width] = g_ref[...]
        tile[LOSS_SLOT[0]:LOSS_SLOT[0] + 1, 0:LANES] = loss_ref[...]
        slots[me] = tile[...]
        copies = []
        for k in range(1, N_PEERS + 1):
            to = (x ^ ((k >> 2) & 1), y ^ ((k >> 1) & 1), c ^ (k & 1))
            cp = pltpu.make_async_remote_copy(src_ref=tile, dst_ref=slots.at[me], send_sem=send_sems.at[k - 1],
                                              recv_sem=recv_sems.at[k - 1], device_id=to, device_id_type=MESH)
            cp.start()
            copies.append(cp)
        for k in range(1, N_PEERS + 1):
            pltpu.make_async_remote_copy(src_ref=tile, dst_ref=slots.at[me ^ k], send_sem=send_sems.at[k - 1],
                                         recv_sem=recv_sems.at[k - 1], device_id=(x, y, c),
                                         device_id_type=MESH).wait_recv()
        for cp in copies:
            cp.wait_send()
        total = slots[0]
        for d in range(1, N_PEERS + 1):
            total = total + slots[d]
        for i, n in enumerate(SMALL):
            r, c0, width = SMALL_LAYOUT[n]
            g = total[r:r + 1, c0:c0 + width]
            delta, m_new, v_new = _adamw_math(w_in_[i][...], g, m_in[i][...], v_in[i][...])
            outs[i][...] = g
            outs[ns + i][...] = delta
            outs[2 * ns + i][...] = m_new
            outs[3 * ns + i][...] = v_new
        outs[4 * ns][...] = total[LOSS_SLOT[0]:LOSS_SLOT[0] + 1, 0:LANES]

    vm = pl.BlockSpec(memory_space=pltpu.VMEM)
    vec = [jax.ShapeDtypeStruct((1, SMALL_LAYOUT[n][2]), F32) for n in SMALL]
    res = pl.pallas_call(
        body, name="small_allreduce_adamw", in_specs=[vm] * (4 * ns + 1), out_specs=[vm] * (4 * ns + 1),
        out_shape=vec * 4 + [jax.ShapeDtypeStruct((1, LANES), F32)],
        scratch_shapes=[pltpu.VMEM((SMALL_ROWS, D_MODEL), F32), pltpu.VMEM((N_PEERS + 1, SMALL_ROWS, D_MODEL), F32),
                        pltpu.SemaphoreType.DMA((N_PEERS,)), pltpu.SemaphoreType.DMA((N_PEERS,))],
    )(*flat(grads), loss, *flat(wts), *flat(mom), *flat(vel))
    groups = [{n: r.reshape(shapes[n]) for n, r in zip(SMALL, res[i * ns:(i + 1) * ns])} for i in range(4)]
    return (*groups, res[4 * ns][0, 0])


def kernel(x, positions, attn_norm_w, w_in, b_gates, mlstm_norm_w, q_a_norm_w, w_q_b, kv_a_norm_w, w_kv_b, w_out, ffn_norm_w, w_gate, w_up, w_down, final_norm_w, loss_target, m_attn_norm_w, m_w_in, m_b_gates, m_mlstm_norm_w, m_q_a_norm_w, m_w_q_b, m_kv_a_norm_w, m_w_kv_b, m_w_out, m_ffn_norm_w, m_w_gate, m_w_up, m_w_down, m_final_norm_w, v_attn_norm_w, v_w_in, v_b_gates, v_mlstm_norm_w, v_q_a_norm_w, v_w_q_b, v_kv_a_norm_w, v_w_kv_b, v_w_out, v_ffn_norm_w, v_w_gate, v_w_up, v_w_down, v_final_norm_w):
    names = ("attn_norm_w", "w_in", "b_gates", "mlstm_norm_w", "q_a_norm_w", "w_q_b", "kv_a_norm_w", "w_kv_b", "w_out",
             "ffn_norm_w", "w_gate", "w_up", "w_down", "final_norm_w")
    wts = dict(zip(names, (attn_norm_w, w_in, b_gates, mlstm_norm_w, q_a_norm_w, w_q_b, kv_a_norm_w, w_kv_b, w_out,
                           ffn_norm_w, w_gate, w_up, w_down, final_norm_w)))
    mom = dict(zip(names, (m_attn_norm_w, m_w_in, m_b_gates, m_mlstm_norm_w, m_q_a_norm_w, m_w_q_b, m_kv_a_norm_w,
                           m_w_kv_b, m_w_out, m_ffn_norm_w, m_w_gate, m_w_up, m_w_down, m_final_norm_w)))
    vel = dict(zip(names, (v_attn_norm_w, v_w_in, v_b_gates, v_mlstm_norm_w, v_q_a_norm_w, v_w_q_b, v_kv_a_norm_w,
                           v_w_kv_b, v_w_out, v_ffn_norm_w, v_w_gate, v_w_up, v_w_down, v_final_norm_w)))
    t = x.shape[1]

    shards = {n: _stored(n, wts[n]).astype(BF16) for n in BIG}
    first, later = ("w_in",), tuple(n for n in BIG if n != "w_in")
    now = _gather_weights(first, [shards[n] for n in first])
    now, later_in = lax.optimization_barrier((now, [shards[n] for n in later]))
    behind = _gather_weights_async(later, later_in)
    full = {n: g.reshape(N_CHIPS * g.shape[1], g.shape[2]) for n, g in zip(first + later, list(now) + list(behind))}

    nw = {n: wts[n] for n in SMALL}
    by_shard = lambda g: g.reshape(N_CHIPS, g.shape[0] // N_CHIPS, g.shape[1])

    def start_reduce(group, tag, grads, grads_b, marker):
        g3 = [by_shard(grads[n]) for n in group]
        gb3 = [by_shard(grads_b[n]) if n in grads_b else g.astype(BF16) for n, g in zip(group, g3)]
        gb3, marker = lax.optimization_barrier((gb3, marker))
        return (g3, _exchange_contributions_async(group, gb3, tag)), marker

    loss, grad_x, grads, grads_b, (late_g3, late_got7) = _local_step(
        x[0], positions.reshape(t, 1), loss_target[0], nw, full, functools.partial(start_reduce, LATE, "late"))

    (early_g3, early_got7), late_got7 = start_reduce(EARLY, "early", grads, grads_b, list(late_got7))
    where = jnp.concatenate([_shard_index(), _core_index()])
    outs_g, outs_d, outs_m, outs_v = {}, {}, {}, {}

    def finish(group, g3, got7, tag):
        mine = [_shard_sum_direct(g, r7, n, where) for n, g, r7 in zip(group, g3, got7)]
        halves = _with_sibling_half(group, mine, tag)
        for n in group:
            res = _adamw_halves(_stored(n, wts[n]), _stored(n, mom[n]), _stored(n, vel[n]), *halves[n], n, where[1:])
            outs_g[n], outs_d[n], outs_m[n], outs_v[n] = [_unstored(n, r) for r in res]

    finish(LATE, late_g3, late_got7, "late")
    small_g, small_d, small_m, small_v, total_loss = _small_allreduce_adamw(grads, loss, wts, mom, vel)
    finish(EARLY, early_g3, early_got7, "early")
    outs_g.update(small_g)
    outs_d.update(small_d)
    outs_m.update(small_m)
    outs_v.update(small_v)
    return (total_loss, grad_x[None], *[outs_g[n] for n in names], *[outs_d[n] for n in names],
            *[outs_m[n] for n in names], *[outs_v[n] for n in names])
```

```python
import functools
import math

import jax
import jax.numpy as jnp
from jax import lax
from jax.experimental import pallas as pl
from jax.experimental.pallas import tpu as pltpu
from jax.experimental.pallas import tpu_sc as plsc

F32 = jnp.float32
BF16 = jnp.bfloat16

D_MODEL = 1024
M_HEADS = 4
M_HEAD_DIM = 128
M_WIDTH = M_HEADS * M_HEAD_DIM
M_CHUNK = 64
A_HEADS = 8
A_NOPE = 64
A_ROPE = 32
A_V = 64
A_WIDTH = A_HEADS * A_V
A_QK_PAD = 128
Q_RANK = 384
KV_RANK = 256
ROPE_THETA = 10000.0
EPS = 1e-6
ATTN_SCALE = (A_NOPE + A_ROPE) ** -0.5
M_SCALE = M_HEAD_DIM ** -0.5

ADAM_LR = 0.001
ADAM_B1 = 0.9
ADAM_B2 = 0.999
ADAM_EPS = 1e-08
ADAM_WD = 0.01
ADAM_STEP = 10

VMEM_LIMIT_BYTES = 56 * 1024 * 1024
LANES = 128
NEG_INF = float("-inf")


def _cparams(*sem):
    return pltpu.CompilerParams(dimension_semantics=sem if sem else None, vmem_limit_bytes=VMEM_LIMIT_BYTES)


_DIMS = {"nn": (((1,), (0,)), ((), ())), "nt": (((1,), (1,)), ((), ())), "tn": (((0,), (0,)), ((), ()))}


def _matmul(a, b, mode, out_dtype, name, tm=None, tn=1024, residual=None, a_rows=None, b_rows=None, lead_scale=None,
            lead_row_scale=None, also_bf16=False, col_chunk=None):
    a_list = list(a) if isinstance(a, (list, tuple)) else [a]
    b_list = list(b) if isinstance(b, (list, tuple)) else [b]
    rows_list = list(b_rows) if isinstance(b_rows, (list, tuple)) else [b_rows] * len(b_list)
    assert len(a_list) == len(b_list) == len(rows_list)
    dims = _DIMS[mode]
    specs, m, n = [], None, None
    for aa, bb, rr in zip(a_list, b_list, rows_list):
        b_shape = bb.shape if rr is None else (rr, bb.shape[1])
        a_shape = aa.shape if a_rows is None else (a_rows, aa.shape[1])
        if mode == "nn":
            (m1, k), (k2, n1) = a_shape, b_shape
        elif mode == "nt":
            (m1, k), (n1, k2) = a_shape, b_shape
        else:
            (k, m1), (k2, n1) = aa.shape, b_shape
        assert k == k2 and (m is None or (m, n) == (m1, n1)), (aa.shape, bb.shape, mode)
        m, n = m1, n1
        specs.append(k)
    if tm is None:
        tm = 512 if mode == "tn" else 1024
    tm, tn = min(tm, m), min(tn, n)
    assert m % tm == 0 and n % tn == 0, (m, n, tm, tn)
    in_specs = []
    for k in specs:
        in_specs.append(pl.BlockSpec((k, tm), lambda i, j: (0, i)) if mode == "tn"
                        else pl.BlockSpec((tm, k), lambda i, j: (i, 0)))
        in_specs.append(pl.BlockSpec((tn, k), lambda i, j: (j, 0)) if mode == "nt"
                        else pl.BlockSpec((k, tn), lambda i, j: (0, j)))
    o_spec = pl.BlockSpec((tm, tn), lambda i, j: (i, j))
    n_pairs = len(specs)

    def body(*refs):
        acc = None
        for p in range(n_pairs):
            part = lax.dot_general(refs[2 * p][...].astype(BF16), refs[2 * p + 1][...].astype(BF16), dims,
                                   preferred_element_type=F32)
            acc = part if acc is None else acc + part
        if lead_scale is not None:
            col = pl.program_id(1) * tn + lax.broadcasted_iota(jnp.int32, (1, tn), 1)
            acc = acc * jnp.where(col < lead_scale[0], lead_scale[1], 1.0)
        if lead_row_scale is not None:
            row = pl.program_id(0) * tm + lax.broadcasted_iota(jnp.int32, (tm, 1), 0)
            acc = acc * jnp.where(row < lead_row_scale[0], lead_row_scale[1], 1.0)
        if residual is not None:
            acc = acc + refs[2 * n_pairs][...].astype(F32)
        outs = refs[2 * n_pairs + (residual is not None):]
        if col_chunk is not None:
            res = acc.astype(outs[0].dtype)
            for c in range(tn // col_chunk):
                outs[0][c] = res[:, c * col_chunk:(c + 1) * col_chunk]
            return
        outs[0][...] = acc.astype(outs[0].dtype)
        if also_bf16:
            outs[1][...] = acc.astype(BF16)

    ins = [x for pair in zip(a_list, b_list) for x in pair] + ([residual] if residual is not None else [])
    in_specs = in_specs + ([o_spec] if residual is not None else [])
    if col_chunk is not None:
        assert residual is None and not also_bf16 and tn % col_chunk == 0
        return pl.pallas_call(
            body, name=name, grid=(m // tm, n // tn), in_specs=in_specs,
            out_specs=pl.BlockSpec((tn // col_chunk, tm, col_chunk), lambda i, j: (j, i, 0)),
            out_shape=jax.ShapeDtypeStruct((n // col_chunk, m, col_chunk), out_dtype),
            compiler_params=_cparams("parallel", "parallel"),
        )(*ins)
    out_shape = jax.ShapeDtypeStruct((m, n), out_dtype)
    if also_bf16:
        return pl.pallas_call(
            body, name=name, grid=(m // tm, n // tn), in_specs=in_specs, out_specs=[o_spec, o_spec],
            out_shape=[out_shape, jax.ShapeDtypeStruct((m, n), BF16)],
            compiler_params=_cparams("parallel", "parallel"),
        )(*ins)
    return pl.pallas_call(
        body, name=name, grid=(m // tm, n // tn), in_specs=in_specs, out_specs=o_spec, out_shape=out_shape,
        compiler_params=_cparams("parallel", "parallel"),
    )(*ins)


def _rms(xf, w):
    return xf * lax.rsqrt(jnp.mean(xf * xf, axis=-1, keepdims=True) + EPS) * w


def _rms_bwd(dyf, xf, w):
    r = lax.rsqrt(jnp.mean(xf * xf, axis=-1, keepdims=True) + EPS)
    xh = xf * r
    dyh = dyf * w
    return r * (dyh - xh * jnp.mean(dyh * xh, axis=-1, keepdims=True)), dyf * xh


def _accumulate(ref, part):
    @pl.when(pl.program_id(0) == 0)
    def _():
        ref[...] = part

    @pl.when(pl.program_id(0) > 0)
    def _():
        ref[...] += part


def _rows(tm, width):
    return pl.BlockSpec((tm, width), lambda i: (i, 0))


def _whole(arr):
    return pl.BlockSpec(arr.shape, lambda i: (0,) * arr.ndim)


def _norm_proj_latents(x, w_norm, w_a2, w_g, w_qn, w_kvn, name, tm=512):
    t, d = x.shape
    tm = min(tm, t)
    n_a2 = Q_RANK + KV_RANK

    def body(x_ref, wn_ref, wa_ref, wg_ref, wq_ref, wkv_ref, u_ref, a2_ref, qn_ref, kvn_ref, g_ref):
        u = _rms(x_ref[...], wn_ref[...]).astype(BF16)
        u_ref[...] = u
        a2 = lax.dot_general(u, wa_ref[...], _DIMS["nt"], preferred_element_type=F32).astype(BF16)
        a2_ref[...] = a2
        qn_ref[...] = _rms(a2[:, :Q_RANK].astype(F32), wq_ref[...]).astype(BF16)
        kvn_ref[...] = _rms(a2[:, Q_RANK:].astype(F32), wkv_ref[...]).astype(BF16)
        g_ref[...] = lax.dot_general(u, wg_ref[...], _DIMS["nt"], preferred_element_type=F32)

    return pl.pallas_call(
        body, name=name, grid=(t // tm,),
        in_specs=[_rows(tm, d), _whole(w_norm), _whole(w_a2), _whole(w_g), _whole(w_qn), _whole(w_kvn)],
        out_specs=[_rows(tm, d), _rows(tm, n_a2), _rows(tm, Q_RANK), _rows(tm, KV_RANK), _rows(tm, LANES)],
        out_shape=[jax.ShapeDtypeStruct((t, d), BF16), jax.ShapeDtypeStruct((t, n_a2), BF16),
                   jax.ShapeDtypeStruct((t, Q_RANK), BF16), jax.ShapeDtypeStruct((t, KV_RANK), BF16),
                   jax.ShapeDtypeStruct((t, LANES), F32)],
        compiler_params=_cparams("parallel"),
    )(x, w_norm, w_a2, w_g, w_qn, w_kvn)


def _out_proj_norm(hm, ha, w_m, w_a, x, w_norm, name, tm=512):
    t = x.shape[0]
    tm = min(tm, t)

    def body(hm_ref, ha_ref, wm_ref, wa_ref, x_ref, wn_ref, h_ref, u_ref):
        h = (jnp.dot(hm_ref[...], wm_ref[...], preferred_element_type=F32)
             + jnp.dot(ha_ref[...], wa_ref[...], preferred_element_type=F32) + x_ref[...])
        h_ref[...] = h
        u_ref[...] = _rms(h, wn_ref[...]).astype(BF16)

    d = x.shape[1]
    return pl.pallas_call(
        body, name=name, grid=(t // tm,),
        in_specs=[_rows(tm, hm.shape[1]), _rows(tm, ha.shape[1]), _whole(w_m), _whole(w_a), _rows(tm, d), _whole(w_norm)],
        out_specs=[_rows(tm, d), _rows(tm, d)],
        out_shape=[jax.ShapeDtypeStruct((t, d), F32), jax.ShapeDtypeStruct((t, d), BF16)],
        compiler_params=_cparams("parallel"),
    )(hm, ha, w_m, w_a, x, w_norm)


def _ffn_down_loss(act, w_down, h1, target, w, name, tm=512):
    t, d = h1.shape
    tm = min(tm, t)

    def body(a_ref, wd_ref, h_ref, t_ref, w_ref, dh_ref, dhb_ref, loss_ref, dw_ref):
        xf = jnp.dot(a_ref[...], wd_ref[...], preferred_element_type=F32) + h_ref[...]
        r = lax.rsqrt(jnp.mean(xf * xf, axis=-1, keepdims=True) + EPS)
        xh = xf * r
        err = xh * w_ref[...] - t_ref[...]
        part_loss = 0.5 * jnp.sum(jnp.sum(err * err, axis=-1, keepdims=True), axis=0, keepdims=True) * (1.0 / d)
        dy = err * (1.0 / d)
        dyh = dy * w_ref[...]
        dh = r * (dyh - xh * jnp.mean(dyh * xh, axis=-1, keepdims=True))
        dh_ref[...] = dh
        dhb_ref[...] = dh.astype(BF16)
        _accumulate(dw_ref, jnp.sum(dy * xh, axis=0, keepdims=True))
        _accumulate(loss_ref, jnp.broadcast_to(part_loss, (1, LANES)))

    vec = pl.BlockSpec((1, d), lambda i: (0, 0))
    return pl.pallas_call(
        body, name=name, grid=(t // tm,),
        in_specs=[_rows(tm, act.shape[1]), _whole(w_down), _rows(tm, d), _rows(tm, d), vec],
        out_specs=[_rows(tm, d), _rows(tm, d), pl.BlockSpec((1, LANES), lambda i: (0, 0)), vec],
        out_shape=[jax.ShapeDtypeStruct((t, d), F32), jax.ShapeDtypeStruct((t, d), BF16),
                   jax.ShapeDtypeStruct((1, LANES), F32), jax.ShapeDtypeStruct((1, d), F32)],
        compiler_params=_cparams("arbitrary"),
    )(act, w_down, h1, target, w)


def _matmul_norm_bwd(das, ws, x, w_norm, name, out_dtype, residual=None, also_bf16=False, w_rows=None, tm=512):
    t, d = x.shape
    tm = min(tm, t)
    n = len(das)

    def body(*refs):
        x_ref, wn_ref = refs[2 * n], refs[2 * n + 1]
        outs = refs[2 * n + 2 + (residual is not None):]
        dy = None
        for p in range(n):
            wp = refs[n + p][...] if w_rows is None or w_rows[p] is None else refs[n + p][0:w_rows[p], :]
            part = jnp.dot(refs[p][...], wp, preferred_element_type=F32)
            dy = part if dy is None else dy + part
        dx, dw_terms = _rms_bwd(dy, x_ref[...].astype(F32), wn_ref[...])
        if residual is not None:
            dx = dx + refs[2 * n + 2][...]
        outs[0][...] = dx.astype(outs[0].dtype)
        if also_bf16:
            outs[1][...] = dx.astype(BF16)
        _accumulate(outs[-1], jnp.sum(dw_terms, axis=0, keepdims=True))

    extra = [jax.ShapeDtypeStruct((t, d), BF16)] if also_bf16 else []
    return pl.pallas_call(
        body, name=name, grid=(t // tm,),
        in_specs=([_rows(tm, a.shape[1]) for a in das] + [_whole(w) for w in ws] + [_rows(tm, d), _whole(w_norm)]
                  + ([_rows(tm, d)] if residual is not None else [])),
        out_specs=[_rows(tm, d)] * (1 + len(extra)) + [pl.BlockSpec((1, d), lambda i: (0, 0))],
        out_shape=[jax.ShapeDtypeStruct((t, d), out_dtype)] + extra + [jax.ShapeDtypeStruct((1, d), F32)],
        compiler_params=_cparams("arbitrary"),
    )(*das, *ws, x, w_norm, *([residual] if residual is not None else []))


FFN_TN = 1408


def _ffn_gate_up(u, w_gate, w_up, name, tm=1024):
    t, d = u.shape
    n = w_gate.shape[0]
    tm = min(tm, t)

    def body(u_ref, wg_ref, wu_ref, g_ref, up_ref, act_ref):
        uu = u_ref[...]
        g = lax.dot_general(uu, wg_ref[...], _DIMS["nt"], preferred_element_type=F32)
        up = lax.dot_general(uu, wu_ref[...], _DIMS["nt"], preferred_element_type=F32)
        g_b, up_b = g.astype(BF16), up.astype(BF16)
        g_ref[...] = g_b
        up_ref[...] = up_b
        gf, uf = g_b.astype(F32), up_b.astype(F32)
        act_ref[...] = (gf * jax.nn.sigmoid(gf) * uf).astype(act_ref.dtype)

    w_spec = pl.BlockSpec((FFN_TN, d), lambda i, j: (j, 0))
    o_spec = pl.BlockSpec((tm, FFN_TN), lambda i, j: (i, j))
    out = jax.ShapeDtypeStruct((t, n), BF16)
    return pl.pallas_call(
        body, name=name, grid=(t // tm, n // FFN_TN),
        in_specs=[pl.BlockSpec((tm, d), lambda i, j: (i, 0)), w_spec, w_spec], out_specs=[o_spec] * 3,
        out_shape=[out] * 3, compiler_params=_cparams("parallel", "parallel"),
    )(u, w_gate, w_up)


def _ffn_down_bwd(dh, w_down, g, up, name, tm=1024):
    t, d = dh.shape
    n = w_down.shape[0]
    tm = min(tm, t)

    def body(dh_ref, w_ref, g_ref, up_ref, dg_ref, du_ref):
        df = lax.dot_general(dh_ref[...], w_ref[...], _DIMS["nt"], preferred_element_type=F32).astype(BF16).astype(F32)
        gf = g_ref[...].astype(F32)
        uf = up_ref[...].astype(F32)
        s = jax.nn.sigmoid(gf)
        dg_ref[...] = (df * uf * s * (1.0 + gf * (1.0 - s))).astype(dg_ref.dtype)
        du_ref[...] = (df * gf * s).astype(du_ref.dtype)

    o_spec = pl.BlockSpec((tm, FFN_TN), lambda i, j: (i, j))
    out = jax.ShapeDtypeStruct((t, n), BF16)
    return pl.pallas_call(
        body, name=name, grid=(t // tm, n // FFN_TN),
        in_specs=[pl.BlockSpec((tm, d), lambda i, j: (i, 0)), pl.BlockSpec((FFN_TN, d), lambda i, j: (j, 0)),
                  o_spec, o_spec],
        out_specs=[o_spec, o_spec], out_shape=[out, out], compiler_params=_cparams("parallel", "parallel"),
    )(dh, w_down, g, up)


M_BLOCK = 512
M_CHUNKS_PER_BLOCK = M_BLOCK // M_CHUNK
M_UNROLL = 2


def _log_sigmoid(z):
    return jnp.minimum(z, 0.0) - jnp.log(1.0 + jnp.exp(-jnp.abs(z)))


def _per_head(fn):
    return jnp.stack([fn(hd) for hd in range(M_HEADS)])


def _mlstm_chunk_fwd(q, k, v, gc, c_prev, n_prev, m_prev):
    L, H = M_CHUNK, M_HEADS
    tt = lax.broadcasted_iota(jnp.int32, (L, L), 0)
    ss = lax.broadcasted_iota(jnp.int32, (L, L), 1)
    eye = tt == ss
    causal = ss <= tt
    gct = jnp.transpose(gc)
    i_col = _per_head(lambda hd: gc[:, hd:hd + 1])
    f_col = _per_head(lambda hd: gc[:, H + hd:H + hd + 1])
    i_row = _per_head(lambda hd: gct[hd:hd + 1, :])
    lf_col = _log_sigmoid(f_col)
    lf_row = _log_sigmoid(_per_head(lambda hd: gct[H + hd:H + hd + 1, :]))
    b_col = jnp.sum(jnp.where(causal, lf_row, 0.0), axis=2, keepdims=True)
    b_row = jnp.sum(jnp.where(tt <= ss, lf_col, 0.0), axis=1, keepdims=True)
    g = jnp.sum(lf_col, axis=1, keepdims=True)
    a_row = g - b_row + i_row
    a_col = g - b_col + i_col
    m_loc = jnp.max(a_row, axis=2, keepdims=True)
    d_log = jnp.where(causal, b_col - b_row + i_row, NEG_INF)
    inter = b_col + m_prev
    m_t = jnp.maximum(jnp.max(d_log, axis=2, keepdims=True), inter)
    dmat = jnp.exp(d_log - m_t)
    amat = _per_head(lambda hd: lax.dot_general(q[hd], k[hd], _DIMS["nt"], preferred_element_type=F32))
    p = dmat * amat
    sig = jnp.exp(inter - m_t)
    c_prev_b = [c.astype(BF16) for c in c_prev]
    qc = _per_head(lambda hd: jnp.dot(q[hd], c_prev_b[hd], preferred_element_type=F32))
    p_b = p.astype(BF16)
    num = _per_head(lambda hd: jnp.dot(p_b[hd], v[hd], preferred_element_type=F32)) + sig * qc
    qf = _per_head(lambda hd: q[hd].astype(F32))
    qn = jnp.sum(qf * n_prev, axis=2, keepdims=True)
    den_raw = jnp.sum(p, axis=2, keepdims=True) + sig * qn
    floor = jnp.exp(-m_t)
    den = jnp.maximum(jnp.abs(den_raw), floor)
    h = num / den
    m_new = jnp.maximum(g + m_prev, m_loc)
    w_col = jnp.exp(a_col - m_new)
    w_row = jnp.exp(a_row - m_new)
    alpha = jnp.exp(g + m_prev - m_new)
    return dict(eye=eye, causal=causal, tt=tt, ss=ss, i_col=i_col, f_col=f_col, dmat=dmat, amat=amat, p=p, p_b=p_b,
                sig=sig, qc=qc, qf=qf, qn=qn, num=num, den_raw=den_raw, floor=floor, den=den, h=h, m_new=m_new,
                w_col=w_col, w_row=w_row, alpha=alpha, c_prev_b=c_prev_b, b_row=b_row, b_col=b_col, m_t=m_t)


def _mlstm_fwd(m4, kt, gates, bias, norm_w, name):
    t = m4.shape[0]
    nblk = t // M_BLOCK
    nc = t // M_CHUNK
    L, dh = M_CHUNK, M_HEAD_DIM

    def body(m4_ref, kt_ref, g_ref, b_ref, w_ref, hm_ref, cp_ref, nm_ref, c_s, n_s, m_s):
        @pl.when(pl.program_id(0) == 0)
        def _():
            c_s[...] = jnp.zeros_like(c_s)
            n_s[...] = jnp.zeros_like(n_s)
            m_s[...] = jnp.zeros_like(m_s)

        row8 = lax.broadcasted_iota(jnp.int32, (8, dh), 0)

        def chunk(c, carry):
            rows = pl.ds(pl.multiple_of(c * L, L), L)
            gc = g_ref[rows, :] + b_ref[...]
            col = lambda part, hd: m4_ref[rows, part * M_WIDTH + hd * dh:part * M_WIDTH + (hd + 1) * dh]
            q = [col(0, hd) for hd in range(M_HEADS)]
            k = [col(1, hd) for hd in range(M_HEADS)]
            v = [col(2, hd) for hd in range(M_HEADS)]
            o = _per_head(lambda hd: col(3, hd).astype(F32))
            c_prev = [c_s[hd] for hd in range(M_HEADS)]
            n_prev = n_s[:, 0:1, :]
            m_prev = m_s[:, 0:1, 0:1]
            r = _mlstm_chunk_fwd(q, k, v, gc, c_prev, n_prev, m_prev)
            h = r["h"]
            hn = h * lax.rsqrt(jnp.mean(h * h, axis=-1, keepdims=True) + EPS) * w_ref[...]
            hm = (hn * jax.nn.sigmoid(o)).astype(hm_ref.dtype)
            nm = jnp.where(row8 == 0, n_prev, jnp.where(row8 == 1, m_prev, 0.0))
            kf = _per_head(lambda hd: k[hd].astype(F32))
            n_s[:, 0:1, :] = r["alpha"] * n_prev + jnp.sum(kf * r["w_col"], axis=1, keepdims=True)
            m_s[...] = jnp.broadcast_to(r["m_new"], (M_HEADS, 8, dh))
            for hd in range(M_HEADS):
                hm_ref[rows, hd * dh:(hd + 1) * dh] = hm[hd]
                cp_ref[hd, c] = r["c_prev_b"][hd]
                nm_ref[hd, c] = nm[hd]
                kw_t = (kt_ref[c, hd * dh:(hd + 1) * dh, :].astype(F32) * r["w_row"][hd]).astype(BF16)
                c_s[hd] = r["alpha"][hd] * c_prev[hd] + jnp.dot(kw_t, v[hd], preferred_element_type=F32)
            return carry

        lax.fori_loop(0, M_CHUNKS_PER_BLOCK, chunk, 0, unroll=M_UNROLL)

    return pl.pallas_call(
        body, name=name, grid=(nblk,),
        in_specs=[pl.BlockSpec((M_BLOCK, 4 * M_WIDTH), lambda i: (i, 0)),
                  pl.BlockSpec((M_CHUNKS_PER_BLOCK, M_WIDTH, L), lambda i: (i, 1, 0)),
                  pl.BlockSpec((M_BLOCK, LANES), lambda i: (i, 0)),
                  pl.BlockSpec((1, LANES), lambda i: (0, 0)),
                  pl.BlockSpec((M_HEADS, 1, dh), lambda i: (0, 0, 0))],
        out_specs=[pl.BlockSpec((M_BLOCK, M_WIDTH), lambda i: (i, 0)),
                   pl.BlockSpec((M_HEADS, M_CHUNKS_PER_BLOCK, dh, dh), lambda i: (0, i, 0, 0)),
                   pl.BlockSpec((M_HEADS, M_CHUNKS_PER_BLOCK, 8, dh), lambda i: (0, i, 0, 0))],
        out_shape=[jax.ShapeDtypeStruct((t, M_WIDTH), BF16),
                   jax.ShapeDtypeStruct((M_HEADS, nc, dh, dh), BF16),
                   jax.ShapeDtypeStruct((M_HEADS, nc, 8, dh), F32)],
        scratch_shapes=[pltpu.VMEM((M_HEADS, dh, dh), F32), pltpu.VMEM((M_HEADS, 8, dh), F32),
                        pltpu.VMEM((M_HEADS, 8, dh), F32)],
        compiler_params=_cparams("arbitrary"),
    )(m4, kt, gates, bias, norm_w)


def _mlstm_bwd(m4, qt, gates, bias, norm_w, c_prev_all, nm_all, dhm, name):
    t = m4.shape[0]
    nblk = t // M_BLOCK
    L, dh = M_CHUNK, M_HEAD_DIM

    def body(m4_ref, qt_ref, g_ref, b_ref, w_ref, cp_ref, nm_ref, dhm_ref, dm4_ref, dg_ref, small_ref, dc_s, dn_s):
        @pl.when(pl.program_id(0) == 0)
        def _():
            dc_s[...] = jnp.zeros_like(dc_s)
            dn_s[...] = jnp.zeros_like(dn_s)
            small_ref[...] = jnp.zeros_like(small_ref)

        lane = lax.broadcasted_iota(jnp.int32, (L, LANES), 1)
        row8 = lax.broadcasted_iota(jnp.int32, (8, LANES), 0)

        def chunk(ci, carry):
            c = M_CHUNKS_PER_BLOCK - 1 - ci
            rows = pl.ds(pl.multiple_of(c * L, L), L)
            gc = g_ref[rows, :] + b_ref[...]
            heads = range(M_HEADS)
            col = lambda part, hd: m4_ref[rows, part * M_WIDTH + hd * dh:part * M_WIDTH + (hd + 1) * dh]
            q = [col(0, hd) for hd in heads]
            k = [col(1, hd) for hd in heads]
            v = [col(2, hd) for hd in heads]
            o = _per_head(lambda hd: col(3, hd).astype(F32))
            q_t = [qt_ref[c, hd * dh:(hd + 1) * dh, :] for hd in heads]
            c_prev = [cp_ref[hd, c].astype(F32) for hd in heads]
            nm = _per_head(lambda hd: nm_ref[hd, c])
            n_prev = nm[:, 0:1, :]
            m_prev = nm[:, 1:2, 0:1]
            r = _mlstm_chunk_fwd(q, k, v, gc, c_prev, n_prev, m_prev)
            eye, tt, ss = r["eye"], r["tt"], r["ss"]
            h, den, sig, p = r["h"], r["den"], r["sig"], r["p"]
            w_col, alpha, c_prev_b = r["w_col"], r["alpha"], r["c_prev_b"]
            to_row = lambda colv: jnp.sum(jnp.where(eye, colv, 0.0), axis=1, keepdims=True)
            to_col = lambda rowv: jnp.sum(jnp.where(eye, rowv, 0.0), axis=2, keepdims=True)
            mm = lambda fn: _per_head(lambda hd: fn(hd))

            w_h = w_ref[...]
            rn = lax.rsqrt(jnp.mean(h * h, axis=-1, keepdims=True) + EPS)
            hh = h * rn
            og = jax.nn.sigmoid(o)
            dhm_c = _per_head(lambda hd: dhm_ref[rows, hd * dh:(hd + 1) * dh].astype(F32))
            dhn = dhm_c * og
            d_o = dhm_c * hh * w_h * og * (1.0 - og)
            d_norm = jnp.sum(dhn * hh, axis=1, keepdims=True)
            dhh = dhn * w_h
            dh_ = rn * (dhh - hh * jnp.mean(dhh * hh, axis=-1, keepdims=True))

            dnum = dh_ / den
            dden = -jnp.sum(dh_ * h, axis=-1, keepdims=True) / den
            dden_raw = jnp.where(jnp.abs(r["den_raw"]) >= r["floor"], dden * jnp.sign(r["den_raw"]), 0.0)
            dnum_b = dnum.astype(BF16)
            dp = mm(lambda hd: lax.dot_general(dnum_b[hd], v[hd], _DIMS["nt"], preferred_element_type=F32)) + dden_raw
            dc = [dc_s[hd] for hd in heads]
            dn = dn_s[:, 0:1, :]
            dc_b = [d.astype(BF16) for d in dc]
            g2_b = (sig * dnum).astype(BF16)
            sd = sig * dden_raw
            da_mat = (dp * r["dmat"]).astype(BF16)
            dqs = mm(lambda hd: lax.dot_general(g2_b[hd], c_prev_b[hd], _DIMS["nt"], preferred_element_type=F32)
                     + jnp.dot(da_mat[hd], k[hd], preferred_element_type=F32)) + sd * n_prev
            r_mat = mm(lambda hd: lax.dot_general(v[hd], dc_b[hd], _DIMS["nt"], preferred_element_type=F32)) + dn
            kf = _per_head(lambda hd: k[hd].astype(F32))
            dmat_t = jnp.exp(jnp.where(tt <= ss, r["b_row"] - r["b_col"] + r["i_col"], NEG_INF) - to_row(r["m_t"]))
            p_t = (dmat_t * mm(lambda hd: jnp.dot(k[hd], q_t[hd], preferred_element_type=F32))).astype(BF16)
            dp_t = mm(lambda hd: lax.dot_general(v[hd], dnum_b[hd], _DIMS["nt"], preferred_element_type=F32))
            da_t = ((dp_t + to_row(dden_raw)) * dmat_t).astype(BF16)
            d_k = mm(lambda hd: jnp.dot(da_t[hd], q[hd], preferred_element_type=F32)) + w_col * r_mat
            d_v = (mm(lambda hd: jnp.dot(p_t[hd], dnum_b[hd], preferred_element_type=F32))
                   + w_col * mm(lambda hd: jnp.dot(k[hd], dc_b[hd], preferred_element_type=F32)))
            da_col = jnp.sum(kf * r_mat, axis=-1, keepdims=True) * w_col
            dsig = jnp.sum(dnum * r["qc"], axis=-1, keepdims=True) + dden_raw * r["qn"]
            pp = dp * p
            r1 = jnp.sum(pp, axis=2, keepdims=True)
            c1_col = to_col(jnp.sum(pp, axis=1, keepdims=True))
            dcc = _per_head(lambda hd: dc[hd] * c_prev[hd])
            d_alpha = (jnp.sum(jnp.sum(dcc, axis=2, keepdims=True), axis=1, keepdims=True)
                       + jnp.sum(dn * n_prev, axis=2, keepdims=True))
            dgl = jnp.sum(da_col, axis=1, keepdims=True) + d_alpha * alpha
            db_col = r1 + dsig * sig - c1_col - da_col
            dli_col = c1_col + da_col
            dlf_col = jnp.sum(jnp.where(ss >= tt, to_row(db_col), 0.0), axis=2, keepdims=True) + dgl
            df_col = dlf_col * jax.nn.sigmoid(-r["f_col"])

            dn_s[:, 0:1, :] = alpha * dn + jnp.sum(sd * r["qf"], axis=1, keepdims=True)
            dq_out = (dqs * M_SCALE).astype(dm4_ref.dtype)
            dk_out, dv_out, do_out = d_k.astype(dm4_ref.dtype), d_v.astype(dm4_ref.dtype), d_o.astype(dm4_ref.dtype)
            dg_tile = jnp.zeros((L, LANES), F32)
            small = jnp.zeros((8, LANES), F32)
            for hd in heads:
                dc_s[hd] = alpha[hd] * dc[hd] + jnp.dot(q_t[hd], g2_b[hd], preferred_element_type=F32)
                dm4_ref[rows, hd * dh:(hd + 1) * dh] = dq_out[hd]
                dm4_ref[rows, M_WIDTH + hd * dh:M_WIDTH + (hd + 1) * dh] = dk_out[hd]
                dm4_ref[rows, 2 * M_WIDTH + hd * dh:2 * M_WIDTH + (hd + 1) * dh] = dv_out[hd]
                dm4_ref[rows, 3 * M_WIDTH + hd * dh:3 * M_WIDTH + (hd + 1) * dh] = do_out[hd]
                dg_tile = (dg_tile + jnp.where(lane == hd, dli_col[hd], 0.0)
                           + jnp.where(lane == M_HEADS + hd, df_col[hd], 0.0))
                small = small + jnp.where(row8 == hd, d_norm[hd], 0.0)
            dg_ref[rows, :] = dg_tile
            small = small + jnp.where(row8 == M_HEADS, jnp.sum(dg_tile, axis=0, keepdims=True), 0.0)
            small_ref[...] += small
            return carry

        lax.fori_loop(0, M_CHUNKS_PER_BLOCK, chunk, 0, unroll=M_UNROLL)

    rev = lambda i: nblk - 1 - i
    return pl.pallas_call(
        body, name=name, grid=(nblk,),
        in_specs=[pl.BlockSpec((M_BLOCK, 4 * M_WIDTH), lambda i: (rev(i), 0)),
                  pl.BlockSpec((M_CHUNKS_PER_BLOCK, M_WIDTH, L), lambda i: (rev(i), 0, 0)),
                  pl.BlockSpec((M_BLOCK, LANES), lambda i: (rev(i), 0)),
                  pl.BlockSpec((1, LANES), lambda i: (0, 0)),
                  pl.BlockSpec((M_HEADS, 1, dh), lambda i: (0, 0, 0)),
                  pl.BlockSpec((M_HEADS, M_CHUNKS_PER_BLOCK, dh, dh), lambda i: (0, rev(i), 0, 0)),
                  pl.BlockSpec((M_HEADS, M_CHUNKS_PER_BLOCK, 8, dh), lambda i: (0, rev(i), 0, 0)),
                  pl.BlockSpec((M_BLOCK, M_WIDTH), lambda i: (rev(i), 0))],
        out_specs=[pl.BlockSpec((M_BLOCK, 4 * M_WIDTH), lambda i: (rev(i), 0)),
                   pl.BlockSpec((M_BLOCK, LANES), lambda i: (rev(i), 0)),
                   pl.BlockSpec((8, LANES), lambda i: (0, 0))],
        out_shape=[jax.ShapeDtypeStruct((t, 4 * M_WIDTH), BF16), jax.ShapeDtypeStruct((t, LANES), F32),
                   jax.ShapeDtypeStruct((8, LANES), F32)],
        scratch_shapes=[pltpu.VMEM((M_HEADS, dh, dh), F32), pltpu.VMEM((M_HEADS, 8, dh), F32)],
        compiler_params=_cparams("arbitrary"),
    )(m4, qt, gates, bias, norm_w, c_prev_all, nm_all, dhm)


def _rope_tables(pos_col, width):
    lane = lax.broadcasted_iota(jnp.int32, (1, width), 1) % A_QK_PAD
    half = A_ROPE // 2
    first = (lane >= A_NOPE) & (lane < A_NOPE + half)
    second = (lane >= A_NOPE + half) & (lane < A_NOPE + A_ROPE)
    idx = jnp.where(first, lane - A_NOPE, lane - A_NOPE - half).astype(F32)
    inv_freq = jnp.exp(idx * (-math.log(ROPE_THETA) / half))
    ang = pos_col.astype(F32) * inv_freq
    cos, sin = jnp.cos(ang), jnp.sin(ang)
    rot = first | second
    return jnp.where(rot, cos, 1.0), jnp.where(first, -sin, 0.0), jnp.where(second, sin, 0.0)


def _rope_apply(vv, cosf, s1, s2):
    half = A_ROPE // 2
    w = vv.shape[-1]
    return vv * cosf + pltpu.roll(vv, w - half, 1) * s1 + pltpu.roll(vv, half, 1) * s2


def _rope_apply_t(dd, cosf, s1, s2):
    half = A_ROPE // 2
    w = dd.shape[-1]
    return dd * cosf + pltpu.roll(dd * s1, half, 1) + pltpu.roll(dd * s2, w - half, 1)


A_BIAS_LANE_K = A_NOPE + A_ROPE
A_BIAS_LANE_V = A_V


def _hi_lo(val):
    hi = val.astype(BF16)
    return hi, (val - hi.astype(F32)).astype(BF16)


def _qkv_up_rope(qa_n, kv_n, w_qb, w_k, w_v, gates, pos, name, tm=512):
    t = qa_n.shape[0]
    w = A_HEADS * LANES

    def body(qa_ref, kv_ref, wq_ref, wk_ref, wv_ref, g_ref, p_ref, qo_ref, ko_ref, vo_ref):
        cosf, s1, s2 = _rope_tables(p_ref[...], LANES)
        lane = lax.broadcasted_iota(jnp.int32, (1, LANES), 1)
        kr = jnp.where((lane >= A_NOPE) & (lane < A_NOPE + A_ROPE), g_ref[...], 0.0)
        kr = _rope_apply(kr, cosf, s1, s2)
        kr = jnp.where((lane == A_BIAS_LANE_K) | (lane == A_BIAS_LANE_K + 1), 1.0, kr)
        v_one = (lane == A_BIAS_LANE_V) | (lane == A_BIAS_LANE_V + 1)
        qpre = lax.dot_general(qa_ref[...], wq_ref[...], _DIMS["nt"], preferred_element_type=F32)
        kv = kv_ref[...]
        kpre = lax.dot_general(kv, wk_ref[...], _DIMS["nt"], preferred_element_type=F32)
        vpre = lax.dot_general(kv, wv_ref[...], _DIMS["nt"], preferred_element_type=F32)
        for hd in range(A_HEADS):
            sl = slice(hd * LANES, (hd + 1) * LANES)
            qo_ref[:, sl] = (_rope_apply(qpre[:, sl], cosf, s1, s2) * ATTN_SCALE).astype(qo_ref.dtype)
            ko_ref[:, sl] = (kpre[:, sl] + kr).astype(ko_ref.dtype)
            vo_ref[:, sl] = jnp.where(v_one, 1.0, vpre[:, sl]).astype(vo_ref.dtype)

    rows = lambda width: pl.BlockSpec((tm, width), lambda i: (i, 0))
    whole = lambda arr: pl.BlockSpec(arr.shape, lambda i: (0, 0))
    out = jax.ShapeDtypeStruct((t, w), BF16)
    return pl.pallas_call(
        body, name=name, grid=(t // tm,),
        in_specs=[rows(qa_n.shape[1]), rows(kv_n.shape[1]), whole(w_qb), whole(w_k), whole(w_v), rows(LANES), rows(1)],
        out_specs=[rows(w)] * 3, out_shape=[out] * 3, compiler_params=_cparams("parallel"),
    )(qa_n, kv_n, w_qb, w_k, w_v, gates, pos)


def _rope_bwd(dq, dk, dgates, pos, name, tm=512):
    t, w = dq.shape

    def body(dq_ref, dk_ref, dg_ref, p_ref, dqo_ref, dgo_ref):
        cosf, s1, s2 = _rope_tables(p_ref[...], LANES)
        acc = jnp.zeros((tm, LANES), F32)
        for hd in range(A_HEADS):
            sl = slice(hd * LANES, (hd + 1) * LANES)
            dqo_ref[:, sl] = (_rope_apply_t(dq_ref[:, sl], cosf, s1, s2) * ATTN_SCALE).astype(dqo_ref.dtype)
            acc = acc + dk_ref[:, sl].astype(F32)
        lane = lax.broadcasted_iota(jnp.int32, (1, LANES), 1)
        dkr = _rope_apply_t(acc, cosf, s1, s2)
        dkr = jnp.where((lane >= A_NOPE) & (lane < A_NOPE + A_ROPE), dkr, 0.0)
        dgo_ref[...] = (dg_ref[...] + dkr).astype(dgo_ref.dtype)

    row = pl.BlockSpec((tm, w), lambda i: (i, 0))
    nar = pl.BlockSpec((tm, LANES), lambda i: (i, 0))
    return pl.pallas_call(
        body, name=name, grid=(t // tm,),
        in_specs=[row, row, nar, pl.BlockSpec((tm, 1), lambda i: (i, 0))],
        out_specs=[row, nar],
        out_shape=[jax.ShapeDtypeStruct((t, w), BF16), jax.ShapeDtypeStruct((t, LANES), BF16)],
        compiler_params=_cparams("parallel"),
    )(dq, dk, dgates, pos)


A_TQ = 512
A_TK = 512
A_TQ_FWD = 1024


A_HEADS_PER_STEP = 4
A_HEADS_PER_STEP_FWD = 4


def _flash_fwd(q, k, v, name):
    t = q.shape[0]
    tq, tk = min(A_TQ_FWD, t), A_TK
    nh = A_HEADS_PER_STEP_FWD
    wblk = nh * LANES

    def body(q_ref, k_ref, v_ref, o_ref, qb_ref):
        i = pl.program_id(1)
        lane = lax.broadcasted_iota(jnp.int32, (tq, LANES), 1)
        qpos = i * tq + lax.broadcasted_iota(jnp.int32, (tq, tk), 0)
        kofs = lax.broadcasted_iota(jnp.int32, (tq, tk), 1)
        qs = [q_ref[:, hh * LANES:(hh + 1) * LANES] for hh in range(nh)]

        def step(kb, carry, masked):
            rows = pl.ds(pl.multiple_of(kb * tk, tk), tk)
            new = []
            for hh in range(nh):
                m, acc = carry[hh]
                kh = k_ref[rows, hh * LANES:(hh + 1) * LANES]
                s = lax.dot_general(qs[hh], kh, _DIMS["nt"], preferred_element_type=F32)
                if masked:
                    s = jnp.where(kb * tk + kofs <= qpos, s, NEG_INF)
                m_new = jnp.maximum(m, jnp.max(s, axis=1, keepdims=True))
                p = jnp.exp(s - m_new)
                acc = jnp.exp(m - m_new) * acc + jnp.dot(p.astype(BF16), v_ref[rows, hh * LANES:(hh + 1) * LANES],
                                                          preferred_element_type=F32)
                new.append((m_new, acc))
            return tuple(new)

        carry = tuple((jnp.full((tq, 1), NEG_INF, F32), jnp.zeros((tq, LANES), F32)) for _ in range(nh))
        n_full = (i * tq) // tk
        carry = lax.fori_loop(0, n_full, functools.partial(step, masked=False), carry)
        for d in range(tq // tk):
            carry = step(n_full + d, carry, True)
        for hh in range(nh):
            m, acc = carry[hh]
            l = acc[:, A_BIAS_LANE_V:A_BIAS_LANE_V + 1]
            o_ref[:, hh * LANES:(hh + 1) * LANES] = (acc / l).astype(o_ref.dtype)
            hi, lo = _hi_lo(-(m + jnp.log(l)))
            qb_ref[:, hh * LANES:(hh + 1) * LANES] = jnp.where(
                lane == A_BIAS_LANE_K, hi, jnp.where(lane == A_BIAS_LANE_K + 1, lo, qs[hh]))

    blk = pl.BlockSpec((tq, wblk), lambda j, i: (i, j))
    res = pl.BlockSpec((t, wblk), lambda j, i: (0, j))
    out = jax.ShapeDtypeStruct((t, A_HEADS * LANES), BF16)
    return pl.pallas_call(
        body, name=name, grid=(A_HEADS // nh, t // tq), in_specs=[blk, res, res], out_specs=[blk, blk],
        out_shape=[out, out], compiler_params=_cparams("parallel", "parallel"),
    )(q, k, v)


def _d_out_proj(dh, w_out_m, w_out_a, o, name, tm=512):
    t, w = o.shape
    tm = min(tm, t)

    def body(dh_ref, wm_ref, w_ref, o_ref, dhm_ref, out_ref):
        dh_blk = dh_ref[...]
        dhm_ref[...] = lax.dot_general(dh_blk, wm_ref[...], _DIMS["nt"], preferred_element_type=F32).astype(BF16)
        do = lax.dot_general(dh_blk, w_ref[...], _DIMS["nt"], preferred_element_type=F32).astype(BF16)
        lane = lax.broadcasted_iota(jnp.int32, (tm, LANES), 1)
        for hd in range(A_HEADS):
            sl = slice(hd * LANES, (hd + 1) * LANES)
            d = do[:, sl]
            delta = jnp.sum(jnp.where(lane < A_V, d.astype(F32) * o_ref[:, sl].astype(F32), 0.0), axis=1, keepdims=True)
            hi, lo = _hi_lo(-delta)
            out_ref[:, sl] = jnp.where(lane == A_BIAS_LANE_V, hi, jnp.where(lane == A_BIAS_LANE_V + 1, lo, d))

    return pl.pallas_call(
        body, name=name, grid=(t // tm,),
        in_specs=[_rows(tm, dh.shape[1]), _whole(w_out_m), _whole(w_out_a), _rows(tm, w)],
        out_specs=[_rows(tm, w_out_m.shape[0]), _rows(tm, w)],
        out_shape=[jax.ShapeDtypeStruct((t, w_out_m.shape[0]), BF16), jax.ShapeDtypeStruct((t, w), BF16)],
        compiler_params=_cparams("parallel"),
    )(dh, w_out_m, w_out_a, o)


def _flash_bwd(qb, k, v, doe, name):
    t = qb.shape[0]
    tq, tk = A_TQ, A_TK
    assert tq == tk
    nh = A_HEADS_PER_STEP
    wblk = nh * LANES
    nq = t // tq

    def body(q_ref, k_ref, v_ref, do_ref, dq_ref, dk_ref, dv_ref):
        kb = pl.program_id(1)

        @pl.when(kb == 0)
        def _():
            dq_ref[...] = jnp.zeros_like(dq_ref)

        kpos = kb * tk + lax.broadcasted_iota(jnp.int32, (tk, tq), 0)
        qofs = lax.broadcasted_iota(jnp.int32, (tk, tq), 1)
        ks = [k_ref[:, hh * LANES:(hh + 1) * LANES] for hh in range(nh)]
        vs = [v_ref[:, hh * LANES:(hh + 1) * LANES] for hh in range(nh)]

        def step(qi, carry, masked):
            rows = pl.ds(pl.multiple_of(qi * tq, tq), tq)
            new = []
            for hh in range(nh):
                dk_acc, dv_acc = carry[hh]
                qh = q_ref[rows, hh * LANES:(hh + 1) * LANES]
                doh = do_ref[rows, hh * LANES:(hh + 1) * LANES]
                p_t = jnp.exp(lax.dot_general(ks[hh], qh, _DIMS["nt"], preferred_element_type=F32))
                if masked:
                    p_t = jnp.where(qi * tq + qofs >= kpos, p_t, 0.0)
                ds_t = (p_t * lax.dot_general(vs[hh], doh, _DIMS["nt"], preferred_element_type=F32)).astype(BF16)
                dv_acc = dv_acc + jnp.dot(p_t.astype(BF16), doh, preferred_element_type=F32)
                dk_acc = dk_acc + jnp.dot(ds_t, qh, preferred_element_type=F32)
                dq_ref[rows, hh * LANES:(hh + 1) * LANES] += lax.dot_general(ds_t, ks[hh], _DIMS["tn"],
                                                                             preferred_element_type=F32)
                new.append((dk_acc, dv_acc))
            return tuple(new)

        carry = tuple((jnp.zeros((tk, LANES), F32), jnp.zeros((tk, LANES), F32)) for _ in range(nh))
        carry = step(kb, carry, True)
        carry = lax.fori_loop(kb + 1, nq, functools.partial(step, masked=False), carry)
        for hh in range(nh):
            dk_ref[:, hh * LANES:(hh + 1) * LANES] = carry[hh][0].astype(dk_ref.dtype)
            dv_ref[:, hh * LANES:(hh + 1) * LANES] = carry[hh][1].astype(dv_ref.dtype)

    res = pl.BlockSpec((t, wblk), lambda j, kb: (0, j))
    blk = pl.BlockSpec((tk, wblk), lambda j, kb: (kb, j))
    wide = (t, A_HEADS * LANES)
    return pl.pallas_call(
        body, name=name, grid=(A_HEADS // nh, t // tk), in_specs=[res, blk, blk, res], out_specs=[res, blk, blk],
        out_shape=[jax.ShapeDtypeStruct(wide, F32), jax.ShapeDtypeStruct(wide, BF16), jax.ShapeDtypeStruct(wide, BF16)],
        compiler_params=_cparams("parallel", "arbitrary"),
    )(qb, k, v, doe)


_SPLITS = (M_WIDTH, M_WIDTH, M_WIDTH, M_WIDTH, M_HEADS, M_HEADS, Q_RANK, KV_RANK, A_ROPE)
_OFFS = tuple(sum(_SPLITS[:i]) for i in range(len(_SPLITS) + 1))
_GATE_BLOCK_KR = A_NOPE


def _split_w_in_t(wt):
    z = lambda n: jnp.zeros((n, wt.shape[1]), wt.dtype)
    w_g = jnp.concatenate([wt[_OFFS[4]:_OFFS[6]], z(_GATE_BLOCK_KR - 2 * M_HEADS), wt[_OFFS[8]:_OFFS[9]],
                           z(LANES - _GATE_BLOCK_KR - A_ROPE)], axis=0)
    return wt[_OFFS[6]:_OFFS[8]], w_g


def _merge_w_in_grad_t(g_m4, g_a2, g_g):
    return jnp.concatenate([g_m4, g_g[:2 * M_HEADS], g_a2, g_g[_GATE_BLOCK_KR:_GATE_BLOCK_KR + A_ROPE]], axis=0)


def _pad_heads(w, used):
    w3 = w.reshape(A_HEADS, used, w.shape[1])
    return jnp.pad(w3, ((0, 0), (0, LANES - used), (0, 0))).reshape(A_HEADS * LANES, w.shape[1])


def _unpad_heads(g, used):
    return g.reshape(A_HEADS, LANES, g.shape[1])[:, :used].reshape(A_HEADS * used, g.shape[1])


def _split_w_kv_b_t(wt):
    w3 = wt.reshape(A_HEADS, A_NOPE + A_V, wt.shape[1])
    pad = lambda part: jnp.pad(part, ((0, 0), (0, LANES - part.shape[1]), (0, 0))).reshape(A_HEADS * LANES, wt.shape[1])
    return pad(w3[:, :A_NOPE]), pad(w3[:, A_NOPE:])


def _merge_w_kv_b_grad_t(gk, gv):
    gk3 = gk.reshape(A_HEADS, LANES, gk.shape[1])[:, :A_NOPE]
    gv3 = gv.reshape(A_HEADS, LANES, gv.shape[1])[:, :A_V]
    return jnp.concatenate([gk3, gv3], axis=1).reshape(A_HEADS * (A_NOPE + A_V), gk.shape[1])


def _local_step(x, pos, target, nw, w, start_late_reduce):
    w_in = w["w_in"]
    w_a2, w_g = _split_w_in_t(w_in)
    w_qb = _pad_heads(w["w_q_b"], A_NOPE + A_ROPE)
    w_k, w_v = _split_w_kv_b_t(w["w_kv_b"])
    w_out_m, w_out_a = w["w_out"][:M_WIDTH], _pad_heads(w["w_out"][M_WIDTH:], A_V)
    w_gate, w_up, w_down = w["w_gate"], w["w_up"], w["w_down"]
    bias = jnp.pad(nw["b_gates"], ((0, 0), (0, LANES - 2 * M_HEADS)))
    mnorm = nw["mlstm_norm_w"].reshape(M_HEADS, 1, M_HEAD_DIM)
    n_m4 = 4 * M_WIDTH

    u1, a2, qa_n, kv_n, gates = _norm_proj_latents(x, nw["attn_norm_w"], w_a2, w_g, nw["q_a_norm_w"],
                                                   nw["kv_a_norm_w"], "norm_proj_latents")
    m4 = _matmul(u1, w_in, "nt", BF16, "proj_mlstm", b_rows=n_m4, lead_scale=(M_WIDTH, M_SCALE))
    qk_t = _matmul(w_in, u1, "nt", BF16, "proj_qk_t", a_rows=2 * M_WIDTH, lead_row_scale=(M_WIDTH, M_SCALE),
                   col_chunk=M_CHUNK)
    hm, c_prev_all, nm_all = _mlstm_fwd(m4, qk_t, gates, bias, mnorm, "mlstm_fwd")
    qa, kva = a2[:, :Q_RANK], a2[:, Q_RANK:]
    qr, kr, vv = _qkv_up_rope(qa_n, kv_n, w_qb, w_k, w_v, gates, pos, "qkv_up_rope")
    ha, qb = _flash_fwd(qr, kr, vv, "attn_fwd")
    h1, u2 = _out_proj_norm(hm, ha, w_out_m, w_out_a, x, nw["ffn_norm_w"], "out_proj")
    gg, uu, act = _ffn_gate_up(u2, w_gate, w_up, "ffn_gate_up")

    dh2, dh2_b, loss, g_final = _ffn_down_loss(act, w_down, h1, target, nw["final_norm_w"].reshape(1, D_MODEL),
                                               "ffn_down_loss")
    grads = {"final_norm_w": g_final.reshape(D_MODEL)}
    grads_b = {}
    dgg, duu = _ffn_down_bwd(dh2_b, w_down, gg, uu, "ffn_down_bwd")
    grads["w_down"], grads_b["w_down"] = _matmul(act, dh2_b, "tn", F32, "g_w_down", tm=256, tn=D_MODEL, also_bf16=True)
    dh1, dh1_b, grads["ffn_norm_w"] = _matmul_norm_bwd([dgg, duu], [w_gate, w_up], h1, nw["ffn_norm_w"], "d_u2", F32,
                                                       residual=dh2, also_bf16=True, tm=256)
    grads["w_gate"], grads_b["w_gate"] = _matmul(dgg, u2, "tn", F32, "g_w_gate", tm=256, tn=D_MODEL, also_bf16=True)
    grads["w_up"], grads_b["w_up"] = _matmul(duu, u2, "tn", F32, "g_w_up", tm=256, tn=D_MODEL, also_bf16=True)
    dhm, doe = _d_out_proj(dh1_b, w_out_m, w_out_a, ha, "d_out_proj")
    grads["w_out"] = jnp.concatenate([_matmul(hm, dh1_b, "tn", F32, "g_w_out_m", tn=D_MODEL),
                                      _unpad_heads(_matmul(ha, dh1_b, "tn", F32, "g_w_out_a", tn=D_MODEL), A_V)], axis=0)

    late, doe = start_late_reduce(grads, grads_b, doe)

    dqr, dkr, dvv = _flash_bwd(qb, kr, vv, doe, "attn_bwd")
    dm4, dgates, small = _mlstm_bwd(m4, qk_t, gates, bias, mnorm, c_prev_all, nm_all, dhm, "mlstm_bwd")
    grads["mlstm_norm_w"] = small[:M_HEADS].reshape(1, M_HEADS, M_HEAD_DIM)
    grads["b_gates"] = small[M_HEADS:M_HEADS + 1, :2 * M_HEADS]
    dqpre, dgk = _rope_bwd(dqr, dkr, dgates, pos, "rope_bwd")
    grads["w_q_b"] = _unpad_heads(_matmul(dqpre, qa_n, "tn", F32, "g_w_q_b"), A_NOPE + A_ROPE)
    grads["w_kv_b"] = _merge_w_kv_b_grad_t(_matmul(dkr, kv_n, "tn", F32, "g_w_k"),
                                           _matmul(dvv, kv_n, "tn", F32, "g_w_v"))
    dqa, grads["q_a_norm_w"] = _matmul_norm_bwd([dqpre], [w_qb], qa, nw["q_a_norm_w"], "d_qa", BF16)
    dkva, grads["kv_a_norm_w"] = _matmul_norm_bwd([dkr, dvv], [w_k, w_v], kva, nw["kv_a_norm_w"], "d_kva", BF16)
    da2 = jnp.concatenate([dqa, dkva], axis=1)
    grads["w_in"] = _merge_w_in_grad_t(_matmul(dm4, u1, "tn", F32, "g_w_m4", tn=D_MODEL),
                                       _matmul(da2, u1, "tn", F32, "g_w_a2", tm=640, tn=D_MODEL),
                                       _matmul(dgk, u1, "tn", F32, "g_w_g", tn=D_MODEL))
    grad_x, grads["attn_norm_w"] = _matmul_norm_bwd([dm4, da2, dgk], [w_in, w_a2, w_g], x, nw["attn_norm_w"], "d_u1",
                                                    F32, residual=dh1, w_rows=[n_m4, None, None])
    return loss, grad_x, grads, grads_b, late


MESH = pl.DeviceIdType.MESH
N_CHIPS = 4
EARLY = ("w_in", "w_q_b", "w_kv_b")
LATE = ("w_out", "w_gate", "w_up", "w_down")
BIG = EARLY + LATE
TRANSPOSED = ("w_in", "w_q_b", "w_kv_b", "w_gate", "w_up")
LANE_HALVED = ("w_in", "w_out", "w_gate", "w_up", "w_down")
SMALL = ("attn_norm_w", "b_gates", "mlstm_norm_w", "q_a_norm_w", "kv_a_norm_w", "ffn_norm_w", "final_norm_w")
HBM_SPEC = pl.BlockSpec(memory_space=pltpu.HBM)


def _stored(name, a):
    return a[0].T if name in TRANSPOSED else a[0]


def _unstored(name, a):
    return (a.T if name in TRANSPOSED else a)[None]


def _half_shape(name, shape):
    rs, cs = shape
    return (rs, cs // 2) if name in LANE_HALVED else (rs // 2, cs)


def _half(ref, name, h, *lead):
    rs, cs = ref.shape[-2:]
    if name in LANE_HALVED:
        return ref.at[(*lead, slice(None), pl.ds(h * (cs // 2), cs // 2))]
    return ref.at[(*lead, pl.ds(h * (rs // 2), rs // 2), slice(None))]


def _place():
    x, y, c = lax.axis_index("x"), lax.axis_index("y"), lax.axis_index("c")
    others = [(1 - x, y), (x, 1 - y), (1 - x, 1 - y)]
    return x, y, c, others


def _gather_weights(names, shards):
    n = len(shards)

    def body(*refs):
        _gather_body(names, refs[:n], refs[n:2 * n], *refs[2 * n:])

    gathered = pl.pallas_call(
        body, name="gather_weights", in_specs=[HBM_SPEC] * n, out_specs=[HBM_SPEC] * n,
        out_shape=[jax.ShapeDtypeStruct((N_CHIPS,) + s.shape, s.dtype) for s in shards],
        scratch_shapes=[pltpu.SemaphoreType.DMA((6 * n,)), pltpu.SemaphoreType.DMA((6 * n,))],
    )(*shards)
    return _with_own_slab(gathered, shards)


def _with_own_slab(gathered, shards):
    me = 2 * lax.axis_index("x") + lax.axis_index("y")
    return [lax.dynamic_update_slice(g, s[None], (me, 0, 0)) for g, s in zip(gathered, shards)]


def _gather_body(names, ins, outs, send_sems, recv_sems):
    x, y, c, others = _place()
    me = 2 * x + y
    sibling = (x, y, 1 - c)

    def copy(w, k, slab, core, to, src=None):
        dst = _half(outs[w], names[w], core, slab)
        return pltpu.make_async_remote_copy(
            src_ref=dst if src is None else src, dst_ref=dst, send_sem=send_sems.at[w * 6 + k],
            recv_sem=recv_sems.at[w * 6 + k], device_id=to, device_id_type=MESH)

    sends = []
    for w in range(len(names)):
        for j, chip in enumerate(others):
            cp = copy(w, j, me, c, (*chip, c), src=_half(ins[w], names[w], c))
            cp.start()
            sends.append(cp)
    for w in range(len(names)):
        for j, (ox, oy) in enumerate(others):
            slab = 2 * ox + oy
            copy(w, j, slab, c, (x, y, c)).wait_recv()
            fwd = copy(w, 3 + j, slab, c, sibling)
            fwd.start()
            sends.append(fwd)
    for w in range(len(names)):
        for j, (ox, oy) in enumerate(others):
            copy(w, 3 + j, 2 * ox + oy, 1 - c, (x, y, c)).wait_recv()
    for cp in sends:
        cp.wait_send()


GATHER_LATE_COLLECTIVE_ID = 1


def _gather_weights_async(names, shards):
    n = len(shards)
    src = [jax.new_ref(s, memory_space=pltpu.MemorySpace.HBM) for s in shards]
    dst = [jax.empty_ref(jax.ShapeDtypeStruct((N_CHIPS,) + s.shape, s.dtype), memory_space=pltpu.MemorySpace.HBM)
           for s in shards]

    @pl.kernel(mesh=plsc.ScalarSubcoreMesh(axis_name="sequencer", num_cores=1), name="gather_weights_async",
               scratch_types=(pltpu.SemaphoreType.DMA((6 * n,)), pltpu.SemaphoreType.DMA((6 * n,))),
               compiler_params=pltpu.CompilerParams(collective_id=GATHER_LATE_COLLECTIVE_ID))
    def launch(send_sems, recv_sems):
        x, y, c, others = _place()
        peers = [(ox, oy, c) for ox, oy in others] + [(x, y, 1 - c)]
        barrier = pltpu.get_barrier_semaphore()
        for peer in peers:
            pl.semaphore_signal(barrier, inc=1, device_id=peer, device_id_type=MESH)
        pl.semaphore_wait(barrier, len(peers))
        _gather_body(names, src, dst, send_sems, recv_sems)

    launch()
    return _with_own_slab([d[...] for d in dst], shards)


def _exchange(arrays, out_shapes, plan, copies_per_array, name):
    n = len(arrays)

    def body(*refs):
        ins, outs = refs[:n], refs[n:2 * n]
        send_sems, recv_sems = refs[2 * n:]
        rem = [pltpu.make_async_remote_copy(src_ref=s, dst_ref=d, send_sem=send_sems.at[k], recv_sem=recv_sems.at[k],
                                            device_id=to, device_id_type=MESH)
               for k, (s, d, to) in enumerate(plan(ins, outs, _place()))]
        for cp in rem:
            cp.start()
        for cp in rem:
            cp.wait_recv()
        for cp in rem:
            cp.wait_send()

    return pl.pallas_call(
        body, name=name, in_specs=[HBM_SPEC] * n, out_specs=[HBM_SPEC] * n, out_shape=out_shapes,
        scratch_shapes=[pltpu.SemaphoreType.DMA((copies_per_array * n,)),
                        pltpu.SemaphoreType.DMA((copies_per_array * n,))],
    )(*arrays)


def _plan_to_sibling(ins, outs, place):
    x, y, c, _ = place
    return [(ins[w], outs[w], (x, y, 1 - c)) for w in range(len(ins))]


N_PEERS = 7


def _shard_sum_direct(g3, got7, name, where):
    hs = tuple(got7.shape[1:])
    if name in LANE_HALVED:
        g_in, g_spec = g3, pl.BlockSpec((1,) + hs, lambda i, p: (p[0], 0, p[1]))
    else:
        g_in, g_spec = g3.reshape((N_CHIPS, 2) + hs), pl.BlockSpec((1, 1) + hs, lambda i, p: (p[0], p[1], 0, 0))

    def body(p_ref, g_ref, r_ref, o_ref):
        acc = g_ref[(0,) * (len(g_ref.shape) - 2)]
        for k in range(N_PEERS):
            acc = acc + r_ref[k].astype(F32)
        o_ref[...] = acc

    return pl.pallas_call(
        body, name="shard_sum_%s" % name,
        grid_spec=pltpu.PrefetchScalarGridSpec(
            num_scalar_prefetch=1, grid=(1,),
            in_specs=[g_spec, pl.BlockSpec((N_PEERS,) + hs, lambda i, p: (0, 0, 0))],
            out_specs=pl.BlockSpec(hs, lambda i, p: (0, 0))),
        out_shape=jax.ShapeDtypeStruct(hs, F32), compiler_params=_cparams("arbitrary"),
    )(where, g_in, got7)


def _adamw_halves(w, m, v, mine, other, name, core):
    hs = tuple(mine.shape)
    full = pl.BlockSpec(hs, (lambda h, c_ref: (0, h)) if name in LANE_HALVED else (lambda h, c_ref: (h, 0)))
    half = pl.BlockSpec(hs, lambda h, c_ref: (0, 0))

    def body(c_ref, w_ref, m_ref, v_ref, a_ref, b_ref, g_ref, d_ref, mo_ref, vo_ref):
        g = jnp.where(pl.program_id(0) == c_ref[0], a_ref[...], b_ref[...])
        delta, m_new, v_new = _adamw_math(w_ref[...], g, m_ref[...], v_ref[...])
        g_ref[...] = g
        d_ref[...] = delta
        mo_ref[...] = m_new
        vo_ref[...] = v_new

    out = jax.ShapeDtypeStruct(w.shape, F32)
    return pl.pallas_call(
        body, name="adamw_%s" % name,
        grid_spec=pltpu.PrefetchScalarGridSpec(num_scalar_prefetch=1, grid=(2,), in_specs=[full, full, full, half, half],
                                               out_specs=[full] * 4),
        out_shape=[out] * 4, compiler_params=_cparams("parallel"),
    )(core, w, m, v, mine, other)


def _adamw_math(w, g, m, v):
    m = ADAM_B1 * m + (1.0 - ADAM_B1) * g
    v = ADAM_B2 * v + (1.0 - ADAM_B2) * (g * g)
    m_hat = m / (1.0 - ADAM_B1 ** ADAM_STEP)
    v_hat = v / (1.0 - ADAM_B2 ** ADAM_STEP)
    delta = -ADAM_LR * (m_hat / (jnp.sqrt(v_hat) + ADAM_EPS) + ADAM_WD * w)
    return delta, m, v


def _core_index():
    return lax.axis_index("c").astype(jnp.int32).reshape(1)


DIRECT_REDUCE_COLLECTIVE_ID = {"late": 2, "early": 3}


def _exchange_contributions_async(names, gb3, tag):
    n = len(gb3)
    src = [jax.new_ref(g, memory_space=pltpu.MemorySpace.HBM) for g in gb3]
    dst = [jax.empty_ref(jax.ShapeDtypeStruct((N_PEERS,) + _half_shape(nm, g.shape[1:]), g.dtype),
                         memory_space=pltpu.MemorySpace.HBM) for nm, g in zip(names, gb3)]

    @pl.kernel(mesh=plsc.ScalarSubcoreMesh(axis_name="sequencer", num_cores=1), name="contributions_async_" + tag,
               scratch_types=(pltpu.SemaphoreType.DMA((N_PEERS * n,)), pltpu.SemaphoreType.DMA((N_PEERS * n,))),
               compiler_params=pltpu.CompilerParams(collective_id=DIRECT_REDUCE_COLLECTIVE_ID[tag]))
    def launch(send_sems, recv_sems):
        x, y, c, _ = _place()
        peers = [(x ^ ((k >> 2) & 1), y ^ ((k >> 1) & 1), c ^ (k & 1)) for k in range(1, N_PEERS + 1)]
        barrier = pltpu.get_barrier_semaphore()
        for peer in peers:
            pl.semaphore_signal(barrier, inc=1, device_id=peer, device_id_type=MESH)
        pl.semaphore_wait(barrier, N_PEERS)
        rem = []
        for w in range(n):
            for k, (px, py, pc) in enumerate(peers):
                rem.append(pltpu.make_async_remote_copy(
                    src_ref=_half(src[w], names[w], pc, 2 * px + py), dst_ref=dst[w].at[k],
                    send_sem=send_sems.at[N_PEERS * w + k], recv_sem=recv_sems.at[N_PEERS * w + k],
                    device_id=(px, py, pc), device_id_type=MESH))
        for cp in rem:
            cp.start()
        for cp in rem:
            cp.wait_recv()
        for cp in rem:
            cp.wait_send()

    launch()
    return [d[...] for d in dst]


def _shard_index():
    return (2 * lax.axis_index("x") + lax.axis_index("y")).astype(jnp.int32).reshape(1)


def _with_sibling_half(names, mine, tag):
    other = _exchange(mine, [jax.ShapeDtypeStruct(m.shape, F32) for m in mine], _plan_to_sibling, 1,
                      "sibling_result_" + tag)
    return {n: (a, b) for n, a, b in zip(names, mine, other)}


SMALL_ROWS = 8
SMALL_LAYOUT = {"attn_norm_w": (0, 0, 1024), "ffn_norm_w": (1, 0, 1024), "final_norm_w": (2, 0, 1024),
                "q_a_norm_w": (3, 0, 384), "kv_a_norm_w": (3, 384, 256), "mlstm_norm_w": (4, 0, 512),
                "b_gates": (4, 512, 8)}
LOSS_SLOT = (5, 0)


def _small_allreduce_adamw(grads, loss, wts, mom, vel):
    shapes = {n: wts[n].shape for n in SMALL}
    flat = lambda d: [d[n].reshape(1, SMALL_LAYOUT[n][2]) for n in SMALL]
    ns = len(SMALL)

    def body(*refs):
        g_in, loss_ref = refs[:ns], refs[ns]
        w_in_, m_in, v_in = refs[ns + 1:2 * ns + 1], refs[2 * ns + 1:3 * ns + 1], refs[3 * ns + 1:4 * ns + 1]
        outs = refs[4 * ns + 1:8 * ns + 2]
        tile, slots, send_sems, recv_sems = refs[8 * ns + 2:]
        x, y, c, _ = _place()
        me = 4 * x + 2 * y + c
        tile[...] = jnp.zeros_like(tile)
        for n, g_ref in zip(SMALL, g_in):
            r, c0, width = SMALL_LAYOUT[n]
            tile[r:r + 1, c0:c0 + width] = g_ref[...]
        tile[LOSS_SLOT[0]:LOSS_SLOT[0] + 1, 0:LANES] = loss_ref[...]
        slots[me] = tile[...]
        copies = []
        for k in range(1, N_PEERS + 1):
            to = (x ^ ((k >> 2) & 1), y ^ ((k >> 1) & 1), c ^ (k & 1))
            cp = pltpu.make_async_remote_copy(src_ref=tile, dst_ref=slots.at[me], send_sem=send_sems.at[k - 1],
                                              recv_sem=recv_sems.at[k - 1], device_id=to, device_id_type=MESH)
            cp.start()
            copies.append(cp)
        for k in range(1, N_PEERS + 1):
            pltpu.make_async_remote_copy(src_ref=tile, dst_ref=slots.at[me ^ k], send_sem=send_sems.at[k - 1],
                                         recv_sem=recv_sems.at[k - 1], device_id=(x, y, c),
                                         device_id_type=MESH).wait_recv()
        for cp in copies:
            cp.wait_send()
        total = slots[0]
        for d in range(1, N_PEERS + 1):
            total = total + slots[d]
        for i, n in enumerate(SMALL):
            r, c0, width = SMALL_LAYOUT[n]
            g = total[r:r + 1, c0:c0 + width]
            delta, m_new, v_new = _adamw_math(w_in_[i][...], g, m_in[i][...], v_in[i][...])
            outs[i][...] = g
            outs[ns + i][...] = delta
            outs[2 * ns + i][...] = m_new
            outs[3 * ns + i][...] = v_new
        outs[4 * ns][...] = total[LOSS_SLOT[0]:LOSS_SLOT[0] + 1, 0:LANES]

    vm = pl.BlockSpec(memory_space=pltpu.VMEM)
    vec = [jax.ShapeDtypeStruct((1, SMALL_LAYOUT[n][2]), F32) for n in SMALL]
    res = pl.pallas_call(
        body, name="small_allreduce_adamw", in_specs=[vm] * (4 * ns + 1), out_specs=[vm] * (4 * ns + 1),
        out_shape=vec * 4 + [jax.ShapeDtypeStruct((1, LANES), F32)],
        scratch_shapes=[pltpu.VMEM((SMALL_ROWS, D_MODEL), F32), pltpu.VMEM((N_PEERS + 1, SMALL_ROWS, D_MODEL), F32),
                        pltpu.SemaphoreType.DMA((N_PEERS,)), pltpu.SemaphoreType.DMA((N_PEERS,))],
    )(*flat(grads), loss, *flat(wts), *flat(mom), *flat(vel))
    groups = [{n: r.reshape(shapes[n]) for n, r in zip(SMALL, res[i * ns:(i + 1) * ns])} for i in range(4)]
    return (*groups, res[4 * ns][0, 0])


def kernel(x, positions, attn_norm_w, w_in, b_gates, mlstm_norm_w, q_a_norm_w, w_q_b, kv_a_norm_w, w_kv_b, w_out, ffn_norm_w, w_gate, w_up, w_down, final_norm_w, loss_target, m_attn_norm_w, m_w_in, m_b_gates, m_mlstm_norm_w, m_q_a_norm_w, m_w_q_b, m_kv_a_norm_w, m_w_kv_b, m_w_out, m_ffn_norm_w, m_w_gate, m_w_up, m_w_down, m_final_norm_w, v_attn_norm_w, v_w_in, v_b_gates, v_mlstm_norm_w, v_q_a_norm_w, v_w_q_b, v_kv_a_norm_w, v_w_kv_b, v_w_out, v_ffn_norm_w, v_w_gate, v_w_up, v_w_down, v_final_norm_w):
    names = ("attn_norm_w", "w_in", "b_gates", "mlstm_norm_w", "q_a_norm_w", "w_q_b", "kv_a_norm_w", "w_kv_b", "w_out",
             "ffn_norm_w", "w_gate", "w_up", "w_down", "final_norm_w")
    wts = dict(zip(names, (attn_norm_w, w_in, b_gates, mlstm_norm_w, q_a_norm_w, w_q_b, kv_a_norm_w, w_kv_b, w_out,
                           ffn_norm_w, w_gate, w_up, w_down, final_norm_w)))
    mom = dict(zip(names, (m_attn_norm_w, m_w_in, m_b_gates, m_mlstm_norm_w, m_q_a_norm_w, m_w_q_b, m_kv_a_norm_w,
                           m_w_kv_b, m_w_out, m_ffn_norm_w, m_w_gate, m_w_up, m_w_down, m_final_norm_w)))
    vel = dict(zip(names, (v_attn_norm_w, v_w_in, v_b_gates, v_mlstm_norm_w, v_q_a_norm_w, v_w_q_b, v_kv_a_norm_w,
                           v_w_kv_b, v_w_out, v_ffn_norm_w, v_w_gate, v_w_up, v_w_down, v_final_norm_w)))
    t = x.shape[1]

    shards = {n: _stored(n, wts[n]).astype(BF16) for n in BIG}
    first, later = ("w_in",), tuple(n for n in BIG if n != "w_in")
    now = _gather_weights(first, [shards[n] for n in first])
    now, later_in = lax.optimization_barrier((now, [shards[n] for n in later]))
    behind = _gather_weights_async(later, later_in)
    full = {n: g.reshape(N_CHIPS * g.shape[1], g.shape[2]) for n, g in zip(first + later, list(now) + list(behind))}

    nw = {n: wts[n] for n in SMALL}
    by_shard = lambda g: g.reshape(N_CHIPS, g.shape[0] // N_CHIPS, g.shape[1])

    def start_reduce(group, tag, grads, grads_b, marker):
        g3 = [by_shard(grads[n]) for n in group]
        gb3 = [by_shard(grads_b[n]) if n in grads_b else g.astype(BF16) for n, g in zip(group, g3)]
        gb3, marker = lax.optimization_barrier((gb3, marker))
        return (g3, _exchange_contributions_async(group, gb3, tag)), marker

    loss, grad_x, grads, grads_b, (late_g3, late_got7) = _local_step(
        x[0], positions.reshape(t, 1), loss_target[0], nw, full, functools.partial(start_reduce, LATE, "late"))

    (early_g3, early_got7), late_got7 = start_reduce(EARLY, "early", grads, grads_b, list(late_got7))
    where = jnp.concatenate([_shard_index(), _core_index()])
    outs_g, outs_d, outs_m, outs_v = {}, {}, {}, {}

    def finish(group, g3, got7, tag):
        mine = [_shard_sum_direct(g, r7, n, where) for n, g, r7 in zip(group, g3, got7)]
        halves = _with_sibling_half(group, mine, tag)
        for n in group:
            res = _adamw_halves(_stored(n, wts[n]), _stored(n, mom[n]), _stored(n, vel[n]), *halves[n], n, where[1:])
            outs_g[n], outs_d[n], outs_m[n], outs_v[n] = [_unstored(n, r) for r in res]

    finish(LATE, late_g3, late_got7, "late")
    small_g, small_d, small_m, small_v, total_loss = _small_allreduce_adamw(grads, loss, wts, mom, vel)
    finish(EARLY, early_g3, early_got7, "early")
    outs_g.update(small_g)
    outs_d.update(small_d)
    outs_m.update(small_m)
    outs_v.update(small_v)
    return (total_loss, grad_x[None], *[outs_g[n] for n in names], *[outs_d[n] for n in names],
            *[outs_m[n] for n in names], *[outs_v[n] for n in names])
```

```python
import functools
import math

import jax
import jax.numpy as jnp
from jax import lax
from jax.experimental import pallas as pl
from jax.experimental.pallas import tpu as pltpu
from jax.experimental.pallas import tpu_sc as plsc

F32 = jnp.float32
BF16 = jnp.bfloat16

D_MODEL = 1024
M_HEADS = 4
M_HEAD_DIM = 128
M_WIDTH = M_HEADS * M_HEAD_DIM
M_CHUNK = 64
A_HEADS = 8
A_NOPE = 64
A_ROPE = 32
A_V = 64
A_QK_PAD = 128
Q_RANK = 384
KV_RANK = 256
ROPE_THETA = 10000.0
EPS = 1e-6
ATTN_SCALE = (A_NOPE + A_ROPE) ** -0.5
M_SCALE = M_HEAD_DIM ** -0.5

ADAM_LR = 0.001
ADAM_B1 = 0.9
ADAM_B2 = 0.999
ADAM_EPS = 1e-08
ADAM_WD = 0.01
ADAM_STEP = 10

VMEM_LIMIT_BYTES = 56 * 1024 * 1024
LANES = 128
NEG_INF = float("-inf")


def _cparams(*sem):
    return pltpu.CompilerParams(dimension_semantics=sem if sem else None, vmem_limit_bytes=VMEM_LIMIT_BYTES)


_DIMS = {"nn": (((1,), (0,)), ((), ())), "nt": (((1,), (1,)), ((), ())), "tn": (((0,), (0,)), ((), ()))}


def _matmul(a, b, mode, out_dtype, name, tm=None, tn=1024, residual=None, a_rows=None, b_rows=None, lead_scale=None,
            lead_row_scale=None, also_bf16=False, col_chunk=None):
    a_list = list(a) if isinstance(a, (list, tuple)) else [a]
    b_list = list(b) if isinstance(b, (list, tuple)) else [b]
    rows_list = list(b_rows) if isinstance(b_rows, (list, tuple)) else [b_rows] * len(b_list)
    assert len(a_list) == len(b_list) == len(rows_list)
    dims = _DIMS[mode]
    specs, m, n = [], None, None
    for aa, bb, rr in zip(a_list, b_list, rows_list):
        b_shape = bb.shape if rr is None else (rr, bb.shape[1])
        a_shape = aa.shape if a_rows is None else (a_rows, aa.shape[1])
        if mode == "nn":
            (m1, k), (k2, n1) = a_shape, b_shape
        elif mode == "nt":
            (m1, k), (n1, k2) = a_shape, b_shape
        else:
            (k, m1), (k2, n1) = aa.shape, b_shape
        assert k == k2 and (m is None or (m, n) == (m1, n1)), (aa.shape, bb.shape, mode)
        m, n = m1, n1
        specs.append(k)
    if tm is None:
        tm = 512 if mode == "tn" else 1024
    tm, tn = min(tm, m), min(tn, n)
    assert m % tm == 0 and n % tn == 0, (m, n, tm, tn)
    in_specs = []
    for k in specs:
        in_specs.append(pl.BlockSpec((k, tm), lambda i, j: (0, i)) if mode == "tn"
                        else pl.BlockSpec((tm, k), lambda i, j: (i, 0)))
        in_specs.append(pl.BlockSpec((tn, k), lambda i, j: (j, 0)) if mode == "nt"
                        else pl.BlockSpec((k, tn), lambda i, j: (0, j)))
    o_spec = pl.BlockSpec((tm, tn), lambda i, j: (i, j))
    n_pairs = len(specs)

    def body(*refs):
        acc = None
        for p in range(n_pairs):
            part = lax.dot_general(refs[2 * p][...].astype(BF16), refs[2 * p + 1][...].astype(BF16), dims,
                                   preferred_element_type=F32)
            acc = part if acc is None else acc + part
        if lead_scale is not None:
            col = pl.program_id(1) * tn + lax.broadcasted_iota(jnp.int32, (1, tn), 1)
            acc = acc * jnp.where(col < lead_scale[0], lead_scale[1], 1.0)
        if lead_row_scale is not None:
            row = pl.program_id(0) * tm + lax.broadcasted_iota(jnp.int32, (tm, 1), 0)
            acc = acc * jnp.where(row < lead_row_scale[0], lead_row_scale[1], 1.0)
        if residual is not None:
            acc = acc + refs[2 * n_pairs][...].astype(F32)
        outs = refs[2 * n_pairs + (residual is not None):]
        if col_chunk is not None:
            res = acc.astype(outs[0].dtype)
            for c in range(tn // col_chunk):
                outs[0][c] = res[:, c * col_chunk:(c + 1) * col_chunk]
            return
        outs[0][...] = acc.astype(outs[0].dtype)
        if also_bf16:
            outs[1][...] = acc.astype(BF16)

    ins = [x for pair in zip(a_list, b_list) for x in pair] + ([residual] if residual is not None else [])
    in_specs = in_specs + ([o_spec] if residual is not None else [])
    if col_chunk is not None:
        assert residual is None and not also_bf16 and tn % col_chunk == 0
        return pl.pallas_call(
            body, name=name, grid=(m // tm, n // tn), in_specs=in_specs,
            out_specs=pl.BlockSpec((tn // col_chunk, tm, col_chunk), lambda i, j: (j, i, 0)),
            out_shape=jax.ShapeDtypeStruct((n // col_chunk, m, col_chunk), out_dtype),
            compiler_params=_cparams("parallel", "parallel"),
        )(*ins)
    out_shape = jax.ShapeDtypeStruct((m, n), out_dtype)
    if also_bf16:
        return pl.pallas_call(
            body, name=name, grid=(m // tm, n // tn), in_specs=in_specs, out_specs=[o_spec, o_spec],
            out_shape=[out_shape, jax.ShapeDtypeStruct((m, n), BF16)],
            compiler_params=_cparams("parallel", "parallel"),
        )(*ins)
    return pl.pallas_call(
        body, name=name, grid=(m // tm, n // tn), in_specs=in_specs, out_specs=o_spec, out_shape=out_shape,
        compiler_params=_cparams("parallel", "parallel"),
    )(*ins)


def _rms(xf, w):
    return xf * lax.rsqrt(jnp.mean(xf * xf, axis=-1, keepdims=True) + EPS) * w


def _rms_bwd(dyf, xf, w):
    r = lax.rsqrt(jnp.mean(xf * xf, axis=-1, keepdims=True) + EPS)
    xh = xf * r
    dyh = dyf * w
    return r * (dyh - xh * jnp.mean(dyh * xh, axis=-1, keepdims=True)), dyf * xh


def _accumulate(ref, part):
    @pl.when(pl.program_id(0) == 0)
    def _():
        ref[...] = part

    @pl.when(pl.program_id(0) > 0)
    def _():
        ref[...] += part


def _rows(tm, width):
    return pl.BlockSpec((tm, width), lambda i: (i, 0))


def _whole(arr):
    return pl.BlockSpec(arr.shape, lambda i: (0,) * arr.ndim)


def _norm_proj_latents(x, w_norm, w_a2, w_g, w_qn, w_kvn, name, tm=512):
    t, d = x.shape
    tm = min(tm, t)
    n_a2 = Q_RANK + KV_RANK

    def body(x_ref, wn_ref, wa_ref, wg_ref, wq_ref, wkv_ref, u_ref, a2_ref, qn_ref, kvn_ref, g_ref):
        u = _rms(x_ref[...], wn_ref[...]).astype(BF16)
        u_ref[...] = u
        a2 = lax.dot_general(u, wa_ref[...], _DIMS["nt"], preferred_element_type=F32).astype(BF16)
        a2_ref[...] = a2
        qn_ref[...] = _rms(a2[:, :Q_RANK].astype(F32), wq_ref[...]).astype(BF16)
        kvn_ref[...] = _rms(a2[:, Q_RANK:].astype(F32), wkv_ref[...]).astype(BF16)
        g_ref[...] = lax.dot_general(u, wg_ref[...], _DIMS["nt"], preferred_element_type=F32)

    return pl.pallas_call(
        body, name=name, grid=(t // tm,),
        in_specs=[_rows(tm, d), _whole(w_norm), _whole(w_a2), _whole(w_g), _whole(w_qn), _whole(w_kvn)],
        out_specs=[_rows(tm, d), _rows(tm, n_a2), _rows(tm, Q_RANK), _rows(tm, KV_RANK), _rows(tm, LANES)],
        out_shape=[jax.ShapeDtypeStruct((t, d), BF16), jax.ShapeDtypeStruct((t, n_a2), BF16),
                   jax.ShapeDtypeStruct((t, Q_RANK), BF16), jax.ShapeDtypeStruct((t, KV_RANK), BF16),
                   jax.ShapeDtypeStruct((t, LANES), F32)],
        compiler_params=_cparams("parallel"),
    )(x, w_norm, w_a2, w_g, w_qn, w_kvn)


def _out_proj_norm(hm, ha, w_m, w_a, x, w_norm, name, tm=512):
    t = x.shape[0]
    tm = min(tm, t)

    def body(hm_ref, ha_ref, wm_ref, wa_ref, x_ref, wn_ref, h_ref, u_ref):
        h = (jnp.dot(hm_ref[...], wm_ref[...], preferred_element_type=F32)
             + jnp.dot(ha_ref[...], wa_ref[...], preferred_element_type=F32) + x_ref[...])
        h_ref[...] = h
        u_ref[...] = _rms(h, wn_ref[...]).astype(BF16)

    d = x.shape[1]
    return pl.pallas_call(
        body, name=name, grid=(t // tm,),
        in_specs=[_rows(tm, hm.shape[1]), _rows(tm, ha.shape[1]), _whole(w_m), _whole(w_a), _rows(tm, d), _whole(w_norm)],
        out_specs=[_rows(tm, d), _rows(tm, d)],
        out_shape=[jax.ShapeDtypeStruct((t, d), F32), jax.ShapeDtypeStruct((t, d), BF16)],
        compiler_params=_cparams("parallel"),
    )(hm, ha, w_m, w_a, x, w_norm)


def _ffn_down_loss(act, w_down, h1, target, w, name, tm=512):
    t, d = h1.shape
    tm = min(tm, t)

    def body(a_ref, wd_ref, h_ref, t_ref, w_ref, dh_ref, dhb_ref, loss_ref, dw_ref):
        xf = jnp.dot(a_ref[...], wd_ref[...], preferred_element_type=F32) + h_ref[...]
        r = lax.rsqrt(jnp.mean(xf * xf, axis=-1, keepdims=True) + EPS)
        xh = xf * r
        err = xh * w_ref[...] - t_ref[...]
        part_loss = 0.5 * jnp.sum(jnp.sum(err * err, axis=-1, keepdims=True), axis=0, keepdims=True) * (1.0 / d)
        dy = err * (1.0 / d)
        dyh = dy * w_ref[...]
        dh = r * (dyh - xh * jnp.mean(dyh * xh, axis=-1, keepdims=True))
        dh_ref[...] = dh
        dhb_ref[...] = dh.astype(BF16)
        _accumulate(dw_ref, jnp.sum(dy * xh, axis=0, keepdims=True))
        _accumulate(loss_ref, jnp.broadcast_to(part_loss, (1, LANES)))

    vec = pl.BlockSpec((1, d), lambda i: (0, 0))
    return pl.pallas_call(
        body, name=name, grid=(t // tm,),
        in_specs=[_rows(tm, act.shape[1]), _whole(w_down), _rows(tm, d), _rows(tm, d), vec],
        out_specs=[_rows(tm, d), _rows(tm, d), pl.BlockSpec((1, LANES), lambda i: (0, 0)), vec],
        out_shape=[jax.ShapeDtypeStruct((t, d), F32), jax.ShapeDtypeStruct((t, d), BF16),
                   jax.ShapeDtypeStruct((1, LANES), F32), jax.ShapeDtypeStruct((1, d), F32)],
        compiler_params=_cparams("arbitrary"),
    )(act, w_down, h1, target, w)


def _matmul_norm_bwd(das, ws, x, w_norm, name, out_dtype, residual=None, also_bf16=False, w_rows=None, tm=512):
    t, d = x.shape
    tm = min(tm, t)
    n = len(das)

    def body(*refs):
        x_ref, wn_ref = refs[2 * n], refs[2 * n + 1]
        outs = refs[2 * n + 2 + (residual is not None):]
        dy = None
        for p in range(n):
            wp = refs[n + p][...] if w_rows is None or w_rows[p] is None else refs[n + p][0:w_rows[p], :]
            part = jnp.dot(refs[p][...], wp, preferred_element_type=F32)
            dy = part if dy is None else dy + part
        dx, dw_terms = _rms_bwd(dy, x_ref[...].astype(F32), wn_ref[...])
        if residual is not None:
            dx = dx + refs[2 * n + 2][...]
        outs[0][...] = dx.astype(outs[0].dtype)
        if also_bf16:
            outs[1][...] = dx.astype(BF16)
        _accumulate(outs[-1], jnp.sum(dw_terms, axis=0, keepdims=True))

    extra = [jax.ShapeDtypeStruct((t, d), BF16)] if also_bf16 else []
    return pl.pallas_call(
        body, name=name, grid=(t // tm,),
        in_specs=([_rows(tm, a.shape[1]) for a in das] + [_whole(w) for w in ws] + [_rows(tm, d), _whole(w_norm)]
                  + ([_rows(tm, d)] if residual is not None else [])),
        out_specs=[_rows(tm, d)] * (1 + len(extra)) + [pl.BlockSpec((1, d), lambda i: (0, 0))],
        out_shape=[jax.ShapeDtypeStruct((t, d), out_dtype)] + extra + [jax.ShapeDtypeStruct((1, d), F32)],
        compiler_params=_cparams("arbitrary"),
    )(*das, *ws, x, w_norm, *([residual] if residual is not None else []))


FFN_TN = 1408


def _ffn_gate_up(u, w_gate, w_up, name, tm=1024):
    t, d = u.shape
    n = w_gate.shape[0]
    tm = min(tm, t)

    def body(u_ref, wg_ref, wu_ref, g_ref, up_ref, act_ref):
        uu = u_ref[...]
        g = lax.dot_general(uu, wg_ref[...], _DIMS["nt"], preferred_element_type=F32)
        up = lax.dot_general(uu, wu_ref[...], _DIMS["nt"], preferred_element_type=F32)
        g_b, up_b = g.astype(BF16), up.astype(BF16)
        g_ref[...] = g_b
        up_ref[...] = up_b
        gf, uf = g_b.astype(F32), up_b.astype(F32)
        act_ref[...] = (gf * jax.nn.sigmoid(gf) * uf).astype(act_ref.dtype)

    w_spec = pl.BlockSpec((FFN_TN, d), lambda i, j: (j, 0))
    o_spec = pl.BlockSpec((tm, FFN_TN), lambda i, j: (i, j))
    out = jax.ShapeDtypeStruct((t, n), BF16)
    return pl.pallas_call(
        body, name=name, grid=(t // tm, n // FFN_TN),
        in_specs=[pl.BlockSpec((tm, d), lambda i, j: (i, 0)), w_spec, w_spec], out_specs=[o_spec] * 3,
        out_shape=[out] * 3, compiler_params=_cparams("parallel", "parallel"),
    )(u, w_gate, w_up)


def _ffn_down_bwd(dh, w_down, g, up, name, tm=1024):
    t, d = dh.shape
    n = w_down.shape[0]
    tm = min(tm, t)

    def body(dh_ref, w_ref, g_ref, up_ref, dg_ref, du_ref):
        df = lax.dot_general(dh_ref[...], w_ref[...], _DIMS["nt"], preferred_element_type=F32).astype(BF16).astype(F32)
        gf = g_ref[...].astype(F32)
        uf = up_ref[...].astype(F32)
        s = jax.nn.sigmoid(gf)
        dg_ref[...] = (df * uf * s * (1.0 + gf * (1.0 - s))).astype(dg_ref.dtype)
        du_ref[...] = (df * gf * s).astype(du_ref.dtype)

    o_spec = pl.BlockSpec((tm, FFN_TN), lambda i, j: (i, j))
    out = jax.ShapeDtypeStruct((t, n), BF16)
    return pl.pallas_call(
        body, name=name, grid=(t // tm, n // FFN_TN),
        in_specs=[pl.BlockSpec((tm, d), lambda i, j: (i, 0)), pl.BlockSpec((FFN_TN, d), lambda i, j: (j, 0)),
                  o_spec, o_spec],
        out_specs=[o_spec, o_spec], out_shape=[out, out], compiler_params=_cparams("parallel", "parallel"),
    )(dh, w_down, g, up)


M_BLOCK = 512
M_CHUNKS_PER_BLOCK = M_BLOCK // M_CHUNK
M_UNROLL = 2


def _log_sigmoid(z):
    return jnp.minimum(z, 0.0) - jnp.log(1.0 + jnp.exp(-jnp.abs(z)))


def _per_head(fn):
    return jnp.stack([fn(hd) for hd in range(M_HEADS)])


def _mlstm_chunk_fwd(q, k, v, gc, c_prev, n_prev, m_prev):
    L, H = M_CHUNK, M_HEADS
    tt = lax.broadcasted_iota(jnp.int32, (L, L), 0)
    ss = lax.broadcasted_iota(jnp.int32, (L, L), 1)
    eye = tt == ss
    causal = ss <= tt
    gct = jnp.transpose(gc)
    i_col = _per_head(lambda hd: gc[:, hd:hd + 1])
    f_col = _per_head(lambda hd: gc[:, H + hd:H + hd + 1])
    i_row = _per_head(lambda hd: gct[hd:hd + 1, :])
    lf_col = _log_sigmoid(f_col)
    lf_row = _log_sigmoid(_per_head(lambda hd: gct[H + hd:H + hd + 1, :]))
    b_col = jnp.sum(jnp.where(causal, lf_row, 0.0), axis=2, keepdims=True)
    b_row = jnp.sum(jnp.where(tt <= ss, lf_col, 0.0), axis=1, keepdims=True)
    g = jnp.sum(lf_col, axis=1, keepdims=True)
    a_row = g - b_row + i_row
    a_col = g - b_col + i_col
    m_loc = jnp.max(a_row, axis=2, keepdims=True)
    d_log = jnp.where(causal, b_col - b_row + i_row, NEG_INF)
    inter = b_col + m_prev
    m_t = jnp.maximum(jnp.max(d_log, axis=2, keepdims=True), inter)
    dmat = jnp.exp(d_log - m_t)
    amat = _per_head(lambda hd: lax.dot_general(q[hd], k[hd], _DIMS["nt"], preferred_element_type=F32))
    p = dmat * amat
    sig = jnp.exp(inter - m_t)
    c_prev_b = [c.astype(BF16) for c in c_prev]
    qc = _per_head(lambda hd: jnp.dot(q[hd], c_prev_b[hd], preferred_element_type=F32))
    p_b = p.astype(BF16)
    num = _per_head(lambda hd: jnp.dot(p_b[hd], v[hd], preferred_element_type=F32)) + sig * qc
    qf = _per_head(lambda hd: q[hd].astype(F32))
    qn = jnp.sum(qf * n_prev, axis=2, keepdims=True)
    den_raw = jnp.sum(p, axis=2, keepdims=True) + sig * qn
    floor = jnp.exp(-m_t)
    den = jnp.maximum(jnp.abs(den_raw), floor)
    h = num / den
    m_new = jnp.maximum(g + m_prev, m_loc)
    w_col = jnp.exp(a_col - m_new)
    w_row = jnp.exp(a_row - m_new)
    alpha = jnp.exp(g + m_prev - m_new)
    return dict(eye=eye, causal=causal, tt=tt, ss=ss, i_col=i_col, f_col=f_col, dmat=dmat, amat=amat, p=p, p_b=p_b,
                sig=sig, qc=qc, qf=qf, qn=qn, num=num, den_raw=den_raw, floor=floor, den=den, h=h, m_new=m_new,
                w_col=w_col, w_row=w_row, alpha=alpha, c_prev_b=c_prev_b, b_row=b_row, b_col=b_col, m_t=m_t)


def _mlstm_fwd(m4, kt, gates, bias, norm_w, name):
    t = m4.shape[0]
    nblk = t // M_BLOCK
    nc = t // M_CHUNK
    L, dh = M_CHUNK, M_HEAD_DIM

    def body(m4_ref, kt_ref, g_ref, b_ref, w_ref, hm_ref, cp_ref, nm_ref, c_s, n_s, m_s):
        @pl.when(pl.program_id(0) == 0)
        def _():
            c_s[...] = jnp.zeros_like(c_s)
            n_s[...] = jnp.zeros_like(n_s)
            m_s[...] = jnp.zeros_like(m_s)

        row8 = lax.broadcasted_iota(jnp.int32, (8, dh), 0)

        def chunk(c, carry):
            rows = pl.ds(pl.multiple_of(c * L, L), L)
            gc = g_ref[rows, :] + b_ref[...]
            col = lambda part, hd: m4_ref[rows, part * M_WIDTH + hd * dh:part * M_WIDTH + (hd + 1) * dh]
            q = [col(0, hd) for hd in range(M_HEADS)]
            k = [col(1, hd) for hd in range(M_HEADS)]
            v = [col(2, hd) for hd in range(M_HEADS)]
            o = _per_head(lambda hd: col(3, hd).astype(F32))
            c_prev = [c_s[hd] for hd in range(M_HEADS)]
            n_prev = n_s[:, 0:1, :]
            m_prev = m_s[:, 0:1, 0:1]
            r = _mlstm_chunk_fwd(q, k, v, gc, c_prev, n_prev, m_prev)
            h = r["h"]
            hn = h * lax.rsqrt(jnp.mean(h * h, axis=-1, keepdims=True) + EPS) * w_ref[...]
            hm = (hn * jax.nn.sigmoid(o)).astype(hm_ref.dtype)
            nm = jnp.where(row8 == 0, n_prev, jnp.where(row8 == 1, m_prev, 0.0))
            kf = _per_head(lambda hd: k[hd].astype(F32))
            n_s[:, 0:1, :] = r["alpha"] * n_prev + jnp.sum(kf * r["w_col"], axis=1, keepdims=True)
            m_s[...] = jnp.broadcast_to(r["m_new"], (M_HEADS, 8, dh))
            for hd in range(M_HEADS):
                hm_ref[rows, hd * dh:(hd + 1) * dh] = hm[hd]
                cp_ref[hd, c] = r["c_prev_b"][hd]
                nm_ref[hd, c] = nm[hd]
                kw_t = (kt_ref[c, hd * dh:(hd + 1) * dh, :].astype(F32) * r["w_row"][hd]).astype(BF16)
                c_s[hd] = r["alpha"][hd] * c_prev[hd] + jnp.dot(kw_t, v[hd], preferred_element_type=F32)
            return carry

        lax.fori_loop(0, M_CHUNKS_PER_BLOCK, chunk, 0, unroll=M_UNROLL)

    return pl.pallas_call(
        body, name=name, grid=(nblk,),
        in_specs=[pl.BlockSpec((M_BLOCK, 4 * M_WIDTH), lambda i: (i, 0)),
                  pl.BlockSpec((M_CHUNKS_PER_BLOCK, M_WIDTH, L), lambda i: (i, 1, 0)),
                  pl.BlockSpec((M_BLOCK, LANES), lambda i: (i, 0)),
                  pl.BlockSpec((1, LANES), lambda i: (0, 0)),
                  pl.BlockSpec((M_HEADS, 1, dh), lambda i: (0, 0, 0))],
        out_specs=[pl.BlockSpec((M_BLOCK, M_WIDTH), lambda i: (i, 0)),
                   pl.BlockSpec((M_HEADS, M_CHUNKS_PER_BLOCK, dh, dh), lambda i: (0, i, 0, 0)),
                   pl.BlockSpec((M_HEADS, M_CHUNKS_PER_BLOCK, 8, dh), lambda i: (0, i, 0, 0))],
        out_shape=[jax.ShapeDtypeStruct((t, M_WIDTH), BF16),
                   jax.ShapeDtypeStruct((M_HEADS, nc, dh, dh), BF16),
                   jax.ShapeDtypeStruct((M_HEADS, nc, 8, dh), F32)],
        scratch_shapes=[pltpu.VMEM((M_HEADS, dh, dh), F32), pltpu.VMEM((M_HEADS, 8, dh), F32),
                        pltpu.VMEM((M_HEADS, 8, dh), F32)],
        compiler_params=_cparams("arbitrary"),
    )(m4, kt, gates, bias, norm_w)


def _mlstm_bwd(m4, qt, gates, bias, norm_w, c_prev_all, nm_all, dhm, name):
    t = m4.shape[0]
    nblk = t // M_BLOCK
    L, dh = M_CHUNK, M_HEAD_DIM

    def body(m4_ref, qt_ref, g_ref, b_ref, w_ref, cp_ref, nm_ref, dhm_ref, dm4_ref, dg_ref, small_ref, dc_s, dn_s):
        @pl.when(pl.program_id(0) == 0)
        def _():
            dc_s[...] = jnp.zeros_like(dc_s)
            dn_s[...] = jnp.zeros_like(dn_s)
            small_ref[...] = jnp.zeros_like(small_ref)

        lane = lax.broadcasted_iota(jnp.int32, (L, LANES), 1)
        row8 = lax.broadcasted_iota(jnp.int32, (8, LANES), 0)

        def chunk(ci, carry):
            c = M_CHUNKS_PER_BLOCK - 1 - ci
            rows = pl.ds(pl.multiple_of(c * L, L), L)
            gc = g_ref[rows, :] + b_ref[...]
            heads = range(M_HEADS)
            col = lambda part, hd: m4_ref[rows, part * M_WIDTH + hd * dh:part * M_WIDTH + (hd + 1) * dh]
            q = [col(0, hd) for hd in heads]
            k = [col(1, hd) for hd in heads]
            v = [col(2, hd) for hd in heads]
            o = _per_head(lambda hd: col(3, hd).astype(F32))
            q_t = [qt_ref[c, hd * dh:(hd + 1) * dh, :] for hd in heads]
            c_prev = [cp_ref[hd, c].astype(F32) for hd in heads]
            nm = _per_head(lambda hd: nm_ref[hd, c])
            n_prev = nm[:, 0:1, :]
            m_prev = nm[:, 1:2, 0:1]
            r = _mlstm_chunk_fwd(q, k, v, gc, c_prev, n_prev, m_prev)
            eye, tt, ss = r["eye"], r["tt"], r["ss"]
            h, den, sig, p = r["h"], r["den"], r["sig"], r["p"]
            w_col, alpha, c_prev_b = r["w_col"], r["alpha"], r["c_prev_b"]
            to_row = lambda colv: jnp.sum(jnp.where(eye, colv, 0.0), axis=1, keepdims=True)
            to_col = lambda rowv: jnp.sum(jnp.where(eye, rowv, 0.0), axis=2, keepdims=True)
            mm = _per_head

            w_h = w_ref[...]
            rn = lax.rsqrt(jnp.mean(h * h, axis=-1, keepdims=True) + EPS)
            hh = h * rn
            og = jax.nn.sigmoid(o)
            dhm_c = _per_head(lambda hd: dhm_ref[rows, hd * dh:(hd + 1) * dh].astype(F32))
            dhn = dhm_c * og
            d_o = dhm_c * hh * w_h * og * (1.0 - og)
            d_norm = jnp.sum(dhn * hh, axis=1, keepdims=True)
            dhh = dhn * w_h
            dh_ = rn * (dhh - hh * jnp.mean(dhh * hh, axis=-1, keepdims=True))

            dnum = dh_ / den
            dden = -jnp.sum(dh_ * h, axis=-1, keepdims=True) / den
            dden_raw = jnp.where(jnp.abs(r["den_raw"]) >= r["floor"], dden * jnp.sign(r["den_raw"]), 0.0)
            dnum_b = dnum.astype(BF16)
            dp = mm(lambda hd: lax.dot_general(dnum_b[hd], v[hd], _DIMS["nt"], preferred_element_type=F32)) + dden_raw
            dc = [dc_s[hd] for hd in heads]
            dn = dn_s[:, 0:1, :]
            dc_b = [d.astype(BF16) for d in dc]
            g2_b = (sig * dnum).astype(BF16)
            sd = sig * dden_raw
            da_mat = (dp * r["dmat"]).astype(BF16)
            dqs = mm(lambda hd: lax.dot_general(g2_b[hd], c_prev_b[hd], _DIMS["nt"], preferred_element_type=F32)
                     + jnp.dot(da_mat[hd], k[hd], preferred_element_type=F32)) + sd * n_prev
            r_mat = mm(lambda hd: lax.dot_general(v[hd], dc_b[hd], _DIMS["nt"], preferred_element_type=F32)) + dn
            kf = _per_head(lambda hd: k[hd].astype(F32))
            dmat_t = jnp.exp(jnp.where(tt <= ss, r["b_row"] - r["b_col"] + r["i_col"], NEG_INF) - to_row(r["m_t"]))
            p_t = (dmat_t * mm(lambda hd: jnp.dot(k[hd], q_t[hd], preferred_element_type=F32))).astype(BF16)
            dp_t = mm(lambda hd: lax.dot_general(v[hd], dnum_b[hd], _DIMS["nt"], preferred_element_type=F32))
            da_t = ((dp_t + to_row(dden_raw)) * dmat_t).astype(BF16)
            d_k = mm(lambda hd: jnp.dot(da_t[hd], q[hd], preferred_element_type=F32)) + w_col * r_mat
            d_v = (mm(lambda hd: jnp.dot(p_t[hd], dnum_b[hd], preferred_element_type=F32))
                   + w_col * mm(lambda hd: jnp.dot(k[hd], dc_b[hd], preferred_element_type=F32)))
            da_col = jnp.sum(kf * r_mat, axis=-1, keepdims=True) * w_col
            dsig = jnp.sum(dnum * r["qc"], axis=-1, keepdims=True) + dden_raw * r["qn"]
            pp = dp * p
            r1 = jnp.sum(pp, axis=2, keepdims=True)
            c1_col = to_col(jnp.sum(pp, axis=1, keepdims=True))
            dcc = _per_head(lambda hd: dc[hd] * c_prev[hd])
            d_alpha = (jnp.sum(jnp.sum(dcc, axis=2, keepdims=True), axis=1, keepdims=True)
                       + jnp.sum(dn * n_prev, axis=2, keepdims=True))
            dgl = jnp.sum(da_col, axis=1, keepdims=True) + d_alpha * alpha
            db_col = r1 + dsig * sig - c1_col - da_col
            dli_col = c1_col + da_col
            dlf_col = jnp.sum(jnp.where(ss >= tt, to_row(db_col), 0.0), axis=2, keepdims=True) + dgl
            df_col = dlf_col * jax.nn.sigmoid(-r["f_col"])

            dn_s[:, 0:1, :] = alpha * dn + jnp.sum(sd * r["qf"], axis=1, keepdims=True)
            dq_out = (dqs * M_SCALE).astype(dm4_ref.dtype)
            dk_out, dv_out, do_out = d_k.astype(dm4_ref.dtype), d_v.astype(dm4_ref.dtype), d_o.astype(dm4_ref.dtype)
            dg_tile = jnp.zeros((L, LANES), F32)
            small = jnp.zeros((8, LANES), F32)
            for hd in heads:
                dc_s[hd] = alpha[hd] * dc[hd] + jnp.dot(q_t[hd], g2_b[hd], preferred_element_type=F32)
                dm4_ref[rows, hd * dh:(hd + 1) * dh] = dq_out[hd]
                dm4_ref[rows, M_WIDTH + hd * dh:M_WIDTH + (hd + 1) * dh] = dk_out[hd]
                dm4_ref[rows, 2 * M_WIDTH + hd * dh:2 * M_WIDTH + (hd + 1) * dh] = dv_out[hd]
                dm4_ref[rows, 3 * M_WIDTH + hd * dh:3 * M_WIDTH + (hd + 1) * dh] = do_out[hd]
                dg_tile = (dg_tile + jnp.where(lane == hd, dli_col[hd], 0.0)
                           + jnp.where(lane == M_HEADS + hd, df_col[hd], 0.0))
                small = small + jnp.where(row8 == hd, d_norm[hd], 0.0)
            dg_ref[rows, :] = dg_tile
            small = small + jnp.where(row8 == M_HEADS, jnp.sum(dg_tile, axis=0, keepdims=True), 0.0)
            small_ref[...] += small
            return carry

        lax.fori_loop(0, M_CHUNKS_PER_BLOCK, chunk, 0, unroll=M_UNROLL)

    rev = lambda i: nblk - 1 - i
    return pl.pallas_call(
        body, name=name, grid=(nblk,),
        in_specs=[pl.BlockSpec((M_BLOCK, 4 * M_WIDTH), lambda i: (rev(i), 0)),
                  pl.BlockSpec((M_CHUNKS_PER_BLOCK, M_WIDTH, L), lambda i: (rev(i), 0, 0)),
                  pl.BlockSpec((M_BLOCK, LANES), lambda i: (rev(i), 0)),
                  pl.BlockSpec((1, LANES), lambda i: (0, 0)),
                  pl.BlockSpec((M_HEADS, 1, dh), lambda i: (0, 0, 0)),
                  pl.BlockSpec((M_HEADS, M_CHUNKS_PER_BLOCK, dh, dh), lambda i: (0, rev(i), 0, 0)),
                  pl.BlockSpec((M_HEADS, M_CHUNKS_PER_BLOCK, 8, dh), lambda i: (0, rev(i), 0, 0)),
                  pl.BlockSpec((M_BLOCK, M_WIDTH), lambda i: (rev(i), 0))],
        out_specs=[pl.BlockSpec((M_BLOCK, 4 * M_WIDTH), lambda i: (rev(i), 0)),
                   pl.BlockSpec((M_BLOCK, LANES), lambda i: (rev(i), 0)),
                   pl.BlockSpec((8, LANES), lambda i: (0, 0))],
        out_shape=[jax.ShapeDtypeStruct((t, 4 * M_WIDTH), BF16), jax.ShapeDtypeStruct((t, LANES), F32),
                   jax.ShapeDtypeStruct((8, LANES), F32)],
        scratch_shapes=[pltpu.VMEM((M_HEADS, dh, dh), F32), pltpu.VMEM((M_HEADS, 8, dh), F32)],
        compiler_params=_cparams("arbitrary"),
    )(m4, qt, gates, bias, norm_w, c_prev_all, nm_all, dhm)


def _rope_tables(pos_col, width):
    lane = lax.broadcasted_iota(jnp.int32, (1, width), 1) % A_QK_PAD
    half = A_ROPE // 2
    first = (lane >= A_NOPE) & (lane < A_NOPE + half)
    second = (lane >= A_NOPE + half) & (lane < A_NOPE + A_ROPE)
    idx = jnp.where(first, lane - A_NOPE, lane - A_NOPE - half).astype(F32)
    inv_freq = jnp.exp(idx * (-math.log(ROPE_THETA) / half))
    ang = pos_col.astype(F32) * inv_freq
    cos, sin = jnp.cos(ang), jnp.sin(ang)
    rot = first | second
    return jnp.where(rot, cos, 1.0), jnp.where(first, -sin, 0.0), jnp.where(second, sin, 0.0)


def _rope_apply(vv, cosf, s1, s2):
    half = A_ROPE // 2
    w = vv.shape[-1]
    return vv * cosf + pltpu.roll(vv, w - half, 1) * s1 + pltpu.roll(vv, half, 1) * s2


def _rope_apply_t(dd, cosf, s1, s2):
    half = A_ROPE // 2
    w = dd.shape[-1]
    return dd * cosf + pltpu.roll(dd * s1, half, 1) + pltpu.roll(dd * s2, w - half, 1)


A_BIAS_LANE_K = A_NOPE + A_ROPE
A_BIAS_LANE_V = A_V


def _hi_lo(val):
    hi = val.astype(BF16)
    return hi, (val - hi.astype(F32)).astype(BF16)


def _qkv_up_rope(qa_n, kv_n, w_qb, w_k, w_v, gates, pos, name, tm=512):
    t = qa_n.shape[0]
    w = A_HEADS * LANES

    def body(qa_ref, kv_ref, wq_ref, wk_ref, wv_ref, g_ref, p_ref, qo_ref, ko_ref, vo_ref):
        cosf, s1, s2 = _rope_tables(p_ref[...], LANES)
        lane = lax.broadcasted_iota(jnp.int32, (1, LANES), 1)
        kr = jnp.where((lane >= A_NOPE) & (lane < A_NOPE + A_ROPE), g_ref[...], 0.0)
        kr = _rope_apply(kr, cosf, s1, s2)
        kr = jnp.where((lane == A_BIAS_LANE_K) | (lane == A_BIAS_LANE_K + 1), 1.0, kr)
        v_one = (lane == A_BIAS_LANE_V) | (lane == A_BIAS_LANE_V + 1)
        qpre = lax.dot_general(qa_ref[...], wq_ref[...], _DIMS["nt"], preferred_element_type=F32)
        kv = kv_ref[...]
        kpre = lax.dot_general(kv, wk_ref[...], _DIMS["nt"], preferred_element_type=F32)
        vpre = lax.dot_general(kv, wv_ref[...], _DIMS["nt"], preferred_element_type=F32)
        for hd in range(A_HEADS):
            sl = slice(hd * LANES, (hd + 1) * LANES)
            qo_ref[:, sl] = (_rope_apply(qpre[:, sl], cosf, s1, s2) * ATTN_SCALE).astype(qo_ref.dtype)
            ko_ref[:, sl] = (kpre[:, sl] + kr).astype(ko_ref.dtype)
            vo_ref[:, sl] = jnp.where(v_one, 1.0, vpre[:, sl]).astype(vo_ref.dtype)

    rows = lambda width: pl.BlockSpec((tm, width), lambda i: (i, 0))
    whole = lambda arr: pl.BlockSpec(arr.shape, lambda i: (0, 0))
    out = jax.ShapeDtypeStruct((t, w), BF16)
    return pl.pallas_call(
        body, name=name, grid=(t // tm,),
        in_specs=[rows(qa_n.shape[1]), rows(kv_n.shape[1]), whole(w_qb), whole(w_k), whole(w_v), rows(LANES), rows(1)],
        out_specs=[rows(w)] * 3, out_shape=[out] * 3, compiler_params=_cparams("parallel"),
    )(qa_n, kv_n, w_qb, w_k, w_v, gates, pos)


def _rope_bwd(dq, dk, dgates, pos, name, tm=512):
    t, w = dq.shape

    def body(dq_ref, dk_ref, dg_ref, p_ref, dqo_ref, dgo_ref):
        cosf, s1, s2 = _rope_tables(p_ref[...], LANES)
        acc = jnp.zeros((tm, LANES), F32)
        for hd in range(A_HEADS):
            sl = slice(hd * LANES, (hd + 1) * LANES)
            dqo_ref[:, sl] = (_rope_apply_t(dq_ref[:, sl], cosf, s1, s2) * ATTN_SCALE).astype(dqo_ref.dtype)
            acc = acc + dk_ref[:, sl].astype(F32)
        lane = lax.broadcasted_iota(jnp.int32, (1, LANES), 1)
        dkr = _rope_apply_t(acc, cosf, s1, s2)
        dkr = jnp.where((lane >= A_NOPE) & (lane < A_NOPE + A_ROPE), dkr, 0.0)
        dgo_ref[...] = (dg_ref[...] + dkr).astype(dgo_ref.dtype)

    row = pl.BlockSpec((tm, w), lambda i: (i, 0))
    nar = pl.BlockSpec((tm, LANES), lambda i: (i, 0))
    return pl.pallas_call(
        body, name=name, grid=(t // tm,),
        in_specs=[row, row, nar, pl.BlockSpec((tm, 1), lambda i: (i, 0))],
        out_specs=[row, nar],
        out_shape=[jax.ShapeDtypeStruct((t, w), BF16), jax.ShapeDtypeStruct((t, LANES), BF16)],
        compiler_params=_cparams("parallel"),
    )(dq, dk, dgates, pos)


A_TQ = 512
A_TK = 512
A_TQ_FWD = 1024


A_HEADS_PER_STEP = 4
A_HEADS_PER_STEP_FWD = 4


def _flash_fwd(q, k, v, name):
    t = q.shape[0]
    tq, tk = min(A_TQ_FWD, t), A_TK
    nh = A_HEADS_PER_STEP_FWD
    wblk = nh * LANES

    def body(q_ref, k_ref, v_ref, o_ref, qb_ref):
        i = pl.program_id(1)
        lane = lax.broadcasted_iota(jnp.int32, (tk, LANES), 1)
        causal = lax.broadcasted_iota(jnp.int32, (tk, tk), 1) <= lax.broadcasted_iota(jnp.int32, (tk, tk), 0)
        head = lambda ref, rows, hh: ref[rows, hh * LANES:(hh + 1) * LANES]
        qs = [head(q_ref, slice(None), hh) for hh in range(nh)]

        def tile(qh, kh, vh, m, acc, mask):
            s = lax.dot_general(qh, kh, _DIMS["nt"], preferred_element_type=F32)
            if mask is not None:
                s = jnp.where(mask, s, NEG_INF)
            m_new = jnp.maximum(m, jnp.max(s, axis=1, keepdims=True))
            p = jnp.exp(s - m_new)
            return m_new, jnp.exp(m - m_new) * acc + jnp.dot(p.astype(BF16), vh, preferred_element_type=F32)

        def step(kb, carry):
            rows = pl.ds(pl.multiple_of(kb * tk, tk), tk)
            return tuple(tile(qs[hh], head(k_ref, rows, hh), head(v_ref, rows, hh), *carry[hh], None)
                         for hh in range(nh))

        carry = tuple((jnp.full((tq, 1), NEG_INF, F32), jnp.zeros((tq, LANES), F32)) for _ in range(nh))
        n_full = (i * tq) // tk
        carry = lax.fori_loop(0, n_full, step, carry)
        blocks = [slice(r * tk, (r + 1) * tk) for r in range(tq // tk)]
        parts = [[(m[b], acc[b]) for b in blocks] for m, acc in carry]
        for d in range(len(blocks)):
            rows = pl.ds(pl.multiple_of((n_full + d) * tk, tk), tk)
            for hh in range(nh):
                kh, vh = head(k_ref, rows, hh), head(v_ref, rows, hh)
                for r in range(d, len(blocks)):
                    parts[hh][r] = tile(qs[hh][blocks[r]], kh, vh, *parts[hh][r], causal if r == d else None)
        for hh in range(nh):
            for b, (m, acc) in zip(blocks, parts[hh]):
                l = acc[:, A_BIAS_LANE_V:A_BIAS_LANE_V + 1]
                o_ref[b, hh * LANES:(hh + 1) * LANES] = (acc / l).astype(o_ref.dtype)
                hi, lo = _hi_lo(-(m + jnp.log(l)))
                qb_ref[b, hh * LANES:(hh + 1) * LANES] = jnp.where(
                    lane == A_BIAS_LANE_K, hi, jnp.where(lane == A_BIAS_LANE_K + 1, lo, qs[hh][b]))

    blk = pl.BlockSpec((tq, wblk), lambda j, i: (i, j))
    res = pl.BlockSpec((t, wblk), lambda j, i: (0, j))
    out = jax.ShapeDtypeStruct((t, A_HEADS * LANES), BF16)
    return pl.pallas_call(
        body, name=name, grid=(A_HEADS // nh, t // tq), in_specs=[blk, res, res], out_specs=[blk, blk],
        out_shape=[out, out], compiler_params=_cparams("parallel", "parallel"),
    )(q, k, v)


def _d_out_proj(dh, w_out_m, w_out_a, o, name, tm=512):
    t, w = o.shape
    tm = min(tm, t)

    def body(dh_ref, wm_ref, w_ref, o_ref, dhm_ref, out_ref):
        dh_blk = dh_ref[...]
        dhm_ref[...] = lax.dot_general(dh_blk, wm_ref[...], _DIMS["nt"], preferred_element_type=F32).astype(BF16)
        do = lax.dot_general(dh_blk, w_ref[...], _DIMS["nt"], preferred_element_type=F32).astype(BF16)
        lane = lax.broadcasted_iota(jnp.int32, (tm, LANES), 1)
        for hd in range(A_HEADS):
            sl = slice(hd * LANES, (hd + 1) * LANES)
            d = do[:, sl]
            delta = jnp.sum(jnp.where(lane < A_V, d.astype(F32) * o_ref[:, sl].astype(F32), 0.0), axis=1, keepdims=True)
            hi, lo = _hi_lo(-delta)
            out_ref[:, sl] = jnp.where(lane == A_BIAS_LANE_V, hi, jnp.where(lane == A_BIAS_LANE_V + 1, lo, d))

    return pl.pallas_call(
        body, name=name, grid=(t // tm,),
        in_specs=[_rows(tm, dh.shape[1]), _whole(w_out_m), _whole(w_out_a), _rows(tm, w)],
        out_specs=[_rows(tm, w_out_m.shape[0]), _rows(tm, w)],
        out_shape=[jax.ShapeDtypeStruct((t, w_out_m.shape[0]), BF16), jax.ShapeDtypeStruct((t, w), BF16)],
        compiler_params=_cparams("parallel"),
    )(dh, w_out_m, w_out_a, o)


def _flash_bwd(qb, k, v, doe, name):
    t = qb.shape[0]
    tq, tk = A_TQ, A_TK
    assert tq == tk
    nh = A_HEADS_PER_STEP
    wblk = nh * LANES
    nq = t // tq

    def body(q_ref, k_ref, v_ref, do_ref, dq_ref, dk_ref, dv_ref):
        kb = pl.program_id(1)

        @pl.when(kb == 0)
        def _():
            dq_ref[...] = jnp.zeros_like(dq_ref)

        kpos = kb * tk + lax.broadcasted_iota(jnp.int32, (tk, tq), 0)
        qofs = lax.broadcasted_iota(jnp.int32, (tk, tq), 1)
        ks = [k_ref[:, hh * LANES:(hh + 1) * LANES] for hh in range(nh)]
        vs = [v_ref[:, hh * LANES:(hh + 1) * LANES] for hh in range(nh)]

        def step(qi, carry, masked):
            rows = pl.ds(pl.multiple_of(qi * tq, tq), tq)
            new = []
            for hh in range(nh):
                dk_acc, dv_acc = carry[hh]
                qh = q_ref[rows, hh * LANES:(hh + 1) * LANES]
                doh = do_ref[rows, hh * LANES:(hh + 1) * LANES]
                p_t = jnp.exp(lax.dot_general(ks[hh], qh, _DIMS["nt"], preferred_element_type=F32))
                if masked:
                    p_t = jnp.where(qi * tq + qofs >= kpos, p_t, 0.0)
                ds_t = (p_t * lax.dot_general(vs[hh], doh, _DIMS["nt"], preferred_element_type=F32)).astype(BF16)
                dv_acc = dv_acc + jnp.dot(p_t.astype(BF16), doh, preferred_element_type=F32)
                dk_acc = dk_acc + jnp.dot(ds_t, qh, preferred_element_type=F32)
                dq_ref[rows, hh * LANES:(hh + 1) * LANES] += lax.dot_general(ds_t, ks[hh], _DIMS["tn"],
                                                                             preferred_element_type=F32)
                new.append((dk_acc, dv_acc))
            return tuple(new)

        carry = tuple((jnp.zeros((tk, LANES), F32), jnp.zeros((tk, LANES), F32)) for _ in range(nh))
        carry = step(kb, carry, True)
        carry = lax.fori_loop(kb + 1, nq, functools.partial(step, masked=False), carry)
        for hh in range(nh):
            dk_ref[:, hh * LANES:(hh + 1) * LANES] = carry[hh][0].astype(dk_ref.dtype)
            dv_ref[:, hh * LANES:(hh + 1) * LANES] = carry[hh][1].astype(dv_ref.dtype)

    res = pl.BlockSpec((t, wblk), lambda j, kb: (0, j))
    blk = pl.BlockSpec((tk, wblk), lambda j, kb: (kb, j))
    wide = (t, A_HEADS * LANES)
    return pl.pallas_call(
        body, name=name, grid=(A_HEADS // nh, t // tk), in_specs=[res, blk, blk, res], out_specs=[res, blk, blk],
        out_shape=[jax.ShapeDtypeStruct(wide, F32), jax.ShapeDtypeStruct(wide, BF16), jax.ShapeDtypeStruct(wide, BF16)],
        compiler_params=_cparams("parallel", "arbitrary"),
    )(qb, k, v, doe)


_SPLITS = (M_WIDTH, M_WIDTH, M_WIDTH, M_WIDTH, M_HEADS, M_HEADS, Q_RANK, KV_RANK, A_ROPE)
_OFFS = tuple(sum(_SPLITS[:i]) for i in range(len(_SPLITS) + 1))
_GATE_BLOCK_KR = A_NOPE


def _split_w_in_t(wt):
    z = lambda n: jnp.zeros((n, wt.shape[1]), wt.dtype)
    w_g = jnp.concatenate([wt[_OFFS[4]:_OFFS[6]], z(_GATE_BLOCK_KR - 2 * M_HEADS), wt[_OFFS[8]:_OFFS[9]],
                           z(LANES - _GATE_BLOCK_KR - A_ROPE)], axis=0)
    return wt[_OFFS[6]:_OFFS[8]], w_g


def _merge_w_in_grad_t(g_m4, g_a2, g_g):
    return jnp.concatenate([g_m4, g_g[:2 * M_HEADS], g_a2, g_g[_GATE_BLOCK_KR:_GATE_BLOCK_KR + A_ROPE]], axis=0)


def _pad_heads(w, used):
    w3 = w.reshape(A_HEADS, used, w.shape[1])
    return jnp.pad(w3, ((0, 0), (0, LANES - used), (0, 0))).reshape(A_HEADS * LANES, w.shape[1])


def _unpad_heads(g, used):
    return g.reshape(A_HEADS, LANES, g.shape[1])[:, :used].reshape(A_HEADS * used, g.shape[1])


def _split_w_kv_b_t(wt):
    w3 = wt.reshape(A_HEADS, A_NOPE + A_V, wt.shape[1])
    pad = lambda part: jnp.pad(part, ((0, 0), (0, LANES - part.shape[1]), (0, 0))).reshape(A_HEADS * LANES, wt.shape[1])
    return pad(w3[:, :A_NOPE]), pad(w3[:, A_NOPE:])


def _merge_w_kv_b_grad_t(gk, gv):
    gk3 = gk.reshape(A_HEADS, LANES, gk.shape[1])[:, :A_NOPE]
    gv3 = gv.reshape(A_HEADS, LANES, gv.shape[1])[:, :A_V]
    return jnp.concatenate([gk3, gv3], axis=1).reshape(A_HEADS * (A_NOPE + A_V), gk.shape[1])


def _local_step(x, pos, target, nw, w, start_late_reduce):
    w_in = w["w_in"]
    w_a2, w_g = _split_w_in_t(w_in)
    w_qb = _pad_heads(w["w_q_b"], A_NOPE + A_ROPE)
    w_k, w_v = _split_w_kv_b_t(w["w_kv_b"])
    w_out_m, w_out_a = w["w_out"][:M_WIDTH], _pad_heads(w["w_out"][M_WIDTH:], A_V)
    w_gate, w_up, w_down = w["w_gate"], w["w_up"], w["w_down"]
    bias = jnp.pad(nw["b_gates"], ((0, 0), (0, LANES - 2 * M_HEADS)))
    mnorm = nw["mlstm_norm_w"].reshape(M_HEADS, 1, M_HEAD_DIM)
    n_m4 = 4 * M_WIDTH

    u1, a2, qa_n, kv_n, gates = _norm_proj_latents(x, nw["attn_norm_w"], w_a2, w_g, nw["q_a_norm_w"],
                                                   nw["kv_a_norm_w"], "norm_proj_latents")
    m4 = _matmul(u1, w_in, "nt", BF16, "proj_mlstm", b_rows=n_m4, lead_scale=(M_WIDTH, M_SCALE))
    qk_t = _matmul(w_in, u1, "nt", BF16, "proj_qk_t", a_rows=2 * M_WIDTH, lead_row_scale=(M_WIDTH, M_SCALE),
                   col_chunk=M_CHUNK)
    hm, c_prev_all, nm_all = _mlstm_fwd(m4, qk_t, gates, bias, mnorm, "mlstm_fwd")
    qa, kva = a2[:, :Q_RANK], a2[:, Q_RANK:]
    qr, kr, vv = _qkv_up_rope(qa_n, kv_n, w_qb, w_k, w_v, gates, pos, "qkv_up_rope")
    ha, qb = _flash_fwd(qr, kr, vv, "attn_fwd")
    h1, u2 = _out_proj_norm(hm, ha, w_out_m, w_out_a, x, nw["ffn_norm_w"], "out_proj")
    gg, uu, act = _ffn_gate_up(u2, w_gate, w_up, "ffn_gate_up")

    dh2, dh2_b, loss, g_final = _ffn_down_loss(act, w_down, h1, target, nw["final_norm_w"].reshape(1, D_MODEL),
                                               "ffn_down_loss")
    grads = {"final_norm_w": g_final.reshape(D_MODEL)}
    grads_b = {}
    dgg, duu = _ffn_down_bwd(dh2_b, w_down, gg, uu, "ffn_down_bwd")
    grads["w_down"], grads_b["w_down"] = _matmul(act, dh2_b, "tn", F32, "g_w_down", tm=256, tn=D_MODEL, also_bf16=True)
    dh1, dh1_b, grads["ffn_norm_w"] = _matmul_norm_bwd([dgg, duu], [w_gate, w_up], h1, nw["ffn_norm_w"], "d_u2", F32,
                                                       residual=dh2, also_bf16=True, tm=256)
    grads["w_gate"], grads_b["w_gate"] = _matmul(dgg, u2, "tn", F32, "g_w_gate", tm=256, tn=D_MODEL, also_bf16=True)
    grads["w_up"], grads_b["w_up"] = _matmul(duu, u2, "tn", F32, "g_w_up", tm=256, tn=D_MODEL, also_bf16=True)
    dhm, doe = _d_out_proj(dh1_b, w_out_m, w_out_a, ha, "d_out_proj")
    grads["w_out"] = jnp.concatenate([_matmul(hm, dh1_b, "tn", F32, "g_w_out_m", tn=D_MODEL),
                                      _unpad_heads(_matmul(ha, dh1_b, "tn", F32, "g_w_out_a", tn=D_MODEL), A_V)], axis=0)

    late, doe = start_late_reduce(grads, grads_b, doe)

    dqr, dkr, dvv = _flash_bwd(qb, kr, vv, doe, "attn_bwd")
    dm4, dgates, small = _mlstm_bwd(m4, qk_t, gates, bias, mnorm, c_prev_all, nm_all, dhm, "mlstm_bwd")
    grads["mlstm_norm_w"] = small[:M_HEADS].reshape(1, M_HEADS, M_HEAD_DIM)
    grads["b_gates"] = small[M_HEADS:M_HEADS + 1, :2 * M_HEADS]
    dqpre, dgk = _rope_bwd(dqr, dkr, dgates, pos, "rope_bwd")
    grads["w_q_b"] = _unpad_heads(_matmul(dqpre, qa_n, "tn", F32, "g_w_q_b"), A_NOPE + A_ROPE)
    grads["w_kv_b"] = _merge_w_kv_b_grad_t(_matmul(dkr, kv_n, "tn", F32, "g_w_k"),
                                           _matmul(dvv, kv_n, "tn", F32, "g_w_v"))
    dqa, grads["q_a_norm_w"] = _matmul_norm_bwd([dqpre], [w_qb], qa, nw["q_a_norm_w"], "d_qa", BF16)
    dkva, grads["kv_a_norm_w"] = _matmul_norm_bwd([dkr, dvv], [w_k, w_v], kva, nw["kv_a_norm_w"], "d_kva", BF16)
    da2 = jnp.concatenate([dqa, dkva], axis=1)
    grads["w_in"] = _merge_w_in_grad_t(_matmul(dm4, u1, "tn", F32, "g_w_m4", tn=D_MODEL),
                                       _matmul(da2, u1, "tn", F32, "g_w_a2", tm=640, tn=D_MODEL),
                                       _matmul(dgk, u1, "tn", F32, "g_w_g", tn=D_MODEL))
    grad_x, grads["attn_norm_w"] = _matmul_norm_bwd([dm4, da2, dgk], [w_in, w_a2, w_g], x, nw["attn_norm_w"], "d_u1",
                                                    F32, residual=dh1, w_rows=[n_m4, None, None])
    return loss, grad_x, grads, grads_b, late


MESH = pl.DeviceIdType.MESH
N_CHIPS = 4
EARLY = ("w_in", "w_q_b", "w_kv_b")
LATE = ("w_out", "w_gate", "w_up", "w_down")
BIG = EARLY + LATE
TRANSPOSED = ("w_in", "w_q_b", "w_kv_b", "w_gate", "w_up")
LANE_HALVED = ("w_in", "w_out", "w_gate", "w_up", "w_down")
SMALL = ("attn_norm_w", "b_gates", "mlstm_norm_w", "q_a_norm_w", "kv_a_norm_w", "ffn_norm_w", "final_norm_w")
HBM_SPEC = pl.BlockSpec(memory_space=pltpu.HBM)


def _stored(name, a):
    return a[0].T if name in TRANSPOSED else a[0]


def _unstored(name, a):
    return (a.T if name in TRANSPOSED else a)[None]


def _half_shape(name, shape):
    rs, cs = shape
    return (rs, cs // 2) if name in LANE_HALVED else (rs // 2, cs)


def _half(ref, name, h, *lead):
    rs, cs = ref.shape[-2:]
    if name in LANE_HALVED:
        return ref.at[(*lead, slice(None), pl.ds(h * (cs // 2), cs // 2))]
    return ref.at[(*lead, pl.ds(h * (rs // 2), rs // 2), slice(None))]


def _place():
    x, y, c = lax.axis_index("x"), lax.axis_index("y"), lax.axis_index("c")
    others = [(1 - x, y), (x, 1 - y), (1 - x, 1 - y)]
    return x, y, c, others


def _gather_weights(names, shards):
    n = len(shards)

    def body(*refs):
        _gather_body(names, refs[:n], refs[n:2 * n], *refs[2 * n:])

    gathered = pl.pallas_call(
        body, name="gather_weights", in_specs=[HBM_SPEC] * n, out_specs=[HBM_SPEC] * n,
        out_shape=[jax.ShapeDtypeStruct((N_CHIPS,) + s.shape, s.dtype) for s in shards],
        scratch_shapes=[pltpu.SemaphoreType.DMA((6 * n,)), pltpu.SemaphoreType.DMA((6 * n,))],
    )(*shards)
    return _with_own_slab(gathered, shards)


def _with_own_slab(gathered, shards):
    me = 2 * lax.axis_index("x") + lax.axis_index("y")
    return [lax.dynamic_update_slice(g, s[None], (me, 0, 0)) for g, s in zip(gathered, shards)]


def _gather_body(names, ins, outs, send_sems, recv_sems):
    x, y, c, others = _place()
    me = 2 * x + y
    sibling = (x, y, 1 - c)

    def copy(w, k, slab, core, to, src=None):
        dst = _half(outs[w], names[w], core, slab)
        return pltpu.make_async_remote_copy(
            src_ref=dst if src is None else src, dst_ref=dst, send_sem=send_sems.at[w * 6 + k],
            recv_sem=recv_sems.at[w * 6 + k], device_id=to, device_id_type=MESH)

    sends = []
    for w in range(len(names)):
        for j, chip in enumerate(others):
            cp = copy(w, j, me, c, (*chip, c), src=_half(ins[w], names[w], c))
            cp.start()
            sends.append(cp)
    for w in range(len(names)):
        for j, (ox, oy) in enumerate(others):
            slab = 2 * ox + oy
            copy(w, j, slab, c, (x, y, c)).wait_recv()
            fwd = copy(w, 3 + j, slab, c, sibling)
            fwd.start()
            sends.append(fwd)
    for w in range(len(names)):
        for j, (ox, oy) in enumerate(others):
            copy(w, 3 + j, 2 * ox + oy, 1 - c, (x, y, c)).wait_recv()
    for cp in sends:
        cp.wait_send()


GATHER_LATE_COLLECTIVE_ID = 1


def _gather_weights_async(names, shards):
    n = len(shards)
    src = [jax.new_ref(s, memory_space=pltpu.MemorySpace.HBM) for s in shards]
    dst = [jax.empty_ref(jax.ShapeDtypeStruct((N_CHIPS,) + s.shape, s.dtype), memory_space=pltpu.MemorySpace.HBM)
           for s in shards]

    @pl.kernel(mesh=plsc.ScalarSubcoreMesh(axis_name="sequencer", num_cores=1), name="gather_weights_async",
               scratch_types=(pltpu.SemaphoreType.DMA((6 * n,)), pltpu.SemaphoreType.DMA((6 * n,))),
               compiler_params=pltpu.CompilerParams(collective_id=GATHER_LATE_COLLECTIVE_ID))
    def launch(send_sems, recv_sems):
        x, y, c, others = _place()
        peers = [(ox, oy, c) for ox, oy in others] + [(x, y, 1 - c)]
        barrier = pltpu.get_barrier_semaphore()
        for peer in peers:
            pl.semaphore_signal(barrier, inc=1, device_id=peer, device_id_type=MESH)
        pl.semaphore_wait(barrier, len(peers))
        _gather_body(names, src, dst, send_sems, recv_sems)

    launch()
    return _with_own_slab([d[...] for d in dst], shards)


def _exchange(arrays, out_shapes, plan, copies_per_array, name):
    n = len(arrays)

    def body(*refs):
        ins, outs = refs[:n], refs[n:2 * n]
        send_sems, recv_sems = refs[2 * n:]
        rem = [pltpu.make_async_remote_copy(src_ref=s, dst_ref=d, send_sem=send_sems.at[k], recv_sem=recv_sems.at[k],
                                            device_id=to, device_id_type=MESH)
               for k, (s, d, to) in enumerate(plan(ins, outs, _place()))]
        for cp in rem:
            cp.start()
        for cp in rem:
            cp.wait_recv()
        for cp in rem:
            cp.wait_send()

    return pl.pallas_call(
        body, name=name, in_specs=[HBM_SPEC] * n, out_specs=[HBM_SPEC] * n, out_shape=out_shapes,
        scratch_shapes=[pltpu.SemaphoreType.DMA((copies_per_array * n,)),
                        pltpu.SemaphoreType.DMA((copies_per_array * n,))],
    )(*arrays)


def _plan_to_sibling(ins, outs, place):
    x, y, c, _ = place
    return [(ins[w], outs[w], (x, y, 1 - c)) for w in range(len(ins))]


N_PEERS = 7


def _shard_sum_direct(g3, got7, name, where):
    hs = tuple(got7.shape[1:])
    if name in LANE_HALVED:
        g_in, g_spec = g3, pl.BlockSpec((1,) + hs, lambda i, p: (p[0], 0, p[1]))
    else:
        g_in, g_spec = g3.reshape((N_CHIPS, 2) + hs), pl.BlockSpec((1, 1) + hs, lambda i, p: (p[0], p[1], 0, 0))

    def body(p_ref, g_ref, r_ref, o_ref):
        acc = g_ref[(0,) * (len(g_ref.shape) - 2)]
        for k in range(N_PEERS):
            acc = acc + r_ref[k].astype(F32)
        o_ref[...] = acc

    return pl.pallas_call(
        body, name="shard_sum_%s" % name,
        grid_spec=pltpu.PrefetchScalarGridSpec(
            num_scalar_prefetch=1, grid=(1,),
            in_specs=[g_spec, pl.BlockSpec((N_PEERS,) + hs, lambda i, p: (0, 0, 0))],
            out_specs=pl.BlockSpec(hs, lambda i, p: (0, 0))),
        out_shape=jax.ShapeDtypeStruct(hs, F32), compiler_params=_cparams("arbitrary"),
    )(where, g_in, got7)


def _adamw_halves(w, m, v, mine, other, name, core):
    hs = tuple(mine.shape)
    full = pl.BlockSpec(hs, (lambda h, c_ref: (0, h)) if name in LANE_HALVED else (lambda h, c_ref: (h, 0)))
    half = pl.BlockSpec(hs, lambda h, c_ref: (0, 0))

    def body(c_ref, w_ref, m_ref, v_ref, a_ref, b_ref, g_ref, d_ref, mo_ref, vo_ref):
        g = jnp.where(pl.program_id(0) == c_ref[0], a_ref[...], b_ref[...])
        delta, m_new, v_new = _adamw_math(w_ref[...], g, m_ref[...], v_ref[...])
        g_ref[...] = g
        d_ref[...] = delta
        mo_ref[...] = m_new
        vo_ref[...] = v_new

    out = jax.ShapeDtypeStruct(w.shape, F32)
    return pl.pallas_call(
        body, name="adamw_%s" % name,
        grid_spec=pltpu.PrefetchScalarGridSpec(num_scalar_prefetch=1, grid=(2,), in_specs=[full, full, full, half, half],
                                               out_specs=[full] * 4),
        out_shape=[out] * 4, compiler_params=_cparams("parallel"),
    )(core, w, m, v, mine, other)


def _adamw_math(w, g, m, v):
    m = ADAM_B1 * m + (1.0 - ADAM_B1) * g
    v = ADAM_B2 * v + (1.0 - ADAM_B2) * (g * g)
    m_hat = m / (1.0 - ADAM_B1 ** ADAM_STEP)
    v_hat = v / (1.0 - ADAM_B2 ** ADAM_STEP)
    delta = -ADAM_LR * (m_hat / (jnp.sqrt(v_hat) + ADAM_EPS) + ADAM_WD * w)
    return delta, m, v


def _core_index():
    return lax.axis_index("c").astype(jnp.int32).reshape(1)


DIRECT_REDUCE_COLLECTIVE_ID = {"late": 2, "early": 3}


def _exchange_contributions_async(names, gb3, tag):
    n = len(gb3)
    src = [jax.new_ref(g, memory_space=pltpu.MemorySpace.HBM) for g in gb3]
    dst = [jax.empty_ref(jax.ShapeDtypeStruct((N_PEERS,) + _half_shape(nm, g.shape[1:]), g.dtype),
                         memory_space=pltpu.MemorySpace.HBM) for nm, g in zip(names, gb3)]

    @pl.kernel(mesh=plsc.ScalarSubcoreMesh(axis_name="sequencer", num_cores=1), name="contributions_async_" + tag,
               scratch_types=(pltpu.SemaphoreType.DMA((N_PEERS * n,)), pltpu.SemaphoreType.DMA((N_PEERS * n,))),
               compiler_params=pltpu.CompilerParams(collective_id=DIRECT_REDUCE_COLLECTIVE_ID[tag]))
    def launch(send_sems, recv_sems):
        x, y, c, _ = _place()
        peers = [(x ^ ((k >> 2) & 1), y ^ ((k >> 1) & 1), c ^ (k & 1)) for k in range(1, N_PEERS + 1)]
        barrier = pltpu.get_barrier_semaphore()
        for peer in peers:
            pl.semaphore_signal(barrier, inc=1, device_id=peer, device_id_type=MESH)
        pl.semaphore_wait(barrier, N_PEERS)
        rem = []
        for w in range(n):
            for k, (px, py, pc) in enumerate(peers):
                rem.append(pltpu.make_async_remote_copy(
                    src_ref=_half(src[w], names[w], pc, 2 * px + py), dst_ref=dst[w].at[k],
                    send_sem=send_sems.at[N_PEERS * w + k], recv_sem=recv_sems.at[N_PEERS * w + k],
                    device_id=(px, py, pc), device_id_type=MESH))
        for cp in rem:
            cp.start()
        for cp in rem:
            cp.wait_recv()
        for cp in rem:
            cp.wait_send()

    launch()
    return [d[...] for d in dst]


def _shard_index():
    return (2 * lax.axis_index("x") + lax.axis_index("y")).astype(jnp.int32).reshape(1)


def _with_sibling_half(names, mine, tag):
    other = _exchange(mine, [jax.ShapeDtypeStruct(m.shape, F32) for m in mine], _plan_to_sibling, 1,
                      "sibling_result_" + tag)
    return {n: (a, b) for n, a, b in zip(names, mine, other)}


SMALL_ROWS = 8
SMALL_LAYOUT = {"attn_norm_w": (0, 0, 1024), "ffn_norm_w": (1, 0, 1024), "final_norm_w": (2, 0, 1024),
                "q_a_norm_w": (3, 0, 384), "kv_a_norm_w": (3, 384, 256), "mlstm_norm_w": (4, 0, 512),
                "b_gates": (4, 512, 8)}
LOSS_SLOT = (5, 0)


def _small_allreduce_adamw(grads, loss, wts, mom, vel):
    shapes = {n: wts[n].shape for n in SMALL}
    flat = lambda d: [d[n].reshape(1, SMALL_LAYOUT[n][2]) for n in SMALL]
    ns = len(SMALL)

    def body(*refs):
        g_in, loss_ref = refs[:ns], refs[ns]
        w_in_, m_in, v_in = refs[ns + 1:2 * ns + 1], refs[2 * ns + 1:3 * ns + 1], refs[3 * ns + 1:4 * ns + 1]
        outs = refs[4 * ns + 1:8 * ns + 2]
        tile, slots, send_sems, recv_sems = refs[8 * ns + 2:]
        x, y, c, _ = _place()
        me = 4 * x + 2 * y + c
        tile[...] = jnp.zeros_like(tile)
        for n, g_ref in zip(SMALL, g_in):
            r, c0, width = SMALL_LAYOUT[n]
            tile[r:r + 1, c0:c0 + width] = g_ref[...]
        tile[LOSS_SLOT[0]:LOSS_SLOT[0] + 1, 0:LANES] = loss_ref[...]
        slots[me] = tile[...]
        copies = []
        for k in range(1, N_PEERS + 1):
            to = (x ^ ((k >> 2) & 1), y ^ ((k >> 1) & 1), c ^ (k & 1))
            cp = pltpu.make_async_remote_copy(src_ref=tile, dst_ref=slots.at[me], send_sem=send_sems.at[k - 1],
                                              recv_sem=recv_sems.at[k - 1], device_id=to, device_id_type=MESH)
            cp.start()
            copies.append(cp)
        for k in range(1, N_PEERS + 1):
            pltpu.make_async_remote_copy(src_ref=tile, dst_ref=slots.at[me ^ k], send_sem=send_sems.at[k - 1],
                                         recv_sem=recv_sems.at[k - 1], device_id=(x, y, c),
                                         device_id_type=MESH).wait_recv()
        for cp in copies:
            cp.wait_send()
        total = slots[0]
        for d in range(1, N_PEERS + 1):
            total = total + slots[d]
        for i, n in enumerate(SMALL):
            r, c0, width = SMALL_LAYOUT[n]
            g = total[r:r + 1, c0:c0 + width]
            delta, m_new, v_new = _adamw_math(w_in_[i][...], g, m_in[i][...], v_in[i][...])
            outs[i][...] = g
            outs[ns + i][...] = delta
            outs[2 * ns + i][...] = m_new
            outs[3 * ns + i][...] = v_new
        outs[4 * ns][...] = total[LOSS_SLOT[0]:LOSS_SLOT[0] + 1, 0:LANES]

    vm = pl.BlockSpec(memory_space=pltpu.VMEM)
    vec = [jax.ShapeDtypeStruct((1, SMALL_LAYOUT[n][2]), F32) for n in SMALL]
    res = pl.pallas_call(
        body, name="small_allreduce_adamw", in_specs=[vm] * (4 * ns + 1), out_specs=[vm] * (4 * ns + 1),
        out_shape=vec * 4 + [jax.ShapeDtypeStruct((1, LANES), F32)],
        scratch_shapes=[pltpu.VMEM((SMALL_ROWS, D_MODEL), F32), pltpu.VMEM((N_PEERS + 1, SMALL_ROWS, D_MODEL), F32),
                        pltpu.SemaphoreType.DMA((N_PEERS,)), pltpu.SemaphoreType.DMA((N_PEERS,))],
    )(*flat(grads), loss, *flat(wts), *flat(mom), *flat(vel))
    groups = [{n: r.reshape(shapes[n]) for n, r in zip(SMALL, res[i * ns:(i + 1) * ns])} for i in range(4)]
    return (*groups, res[4 * ns][0, 0])


def kernel(x, positions, attn_norm_w, w_in, b_gates, mlstm_norm_w, q_a_norm_w, w_q_b, kv_a_norm_w, w_kv_b, w_out, ffn_norm_w, w_gate, w_up, w_down, final_norm_w, loss_target, m_attn_norm_w, m_w_in, m_b_gates, m_mlstm_norm_w, m_q_a_norm_w, m_w_q_b, m_kv_a_norm_w, m_w_kv_b, m_w_out, m_ffn_norm_w, m_w_gate, m_w_up, m_w_down, m_final_norm_w, v_attn_norm_w, v_w_in, v_b_gates, v_mlstm_norm_w, v_q_a_norm_w, v_w_q_b, v_kv_a_norm_w, v_w_kv_b, v_w_out, v_ffn_norm_w, v_w_gate, v_w_up, v_w_down, v_final_norm_w):
    names = ("attn_norm_w", "w_in", "b_gates", "mlstm_norm_w", "q_a_norm_w", "w_q_b", "kv_a_norm_w", "w_kv_b", "w_out",
             "ffn_norm_w", "w_gate", "w_up", "w_down", "final_norm_w")
    wts = dict(zip(names, (attn_norm_w, w_in, b_gates, mlstm_norm_w, q_a_norm_w, w_q_b, kv_a_norm_w, w_kv_b, w_out,
                           ffn_norm_w, w_gate, w_up, w_down, final_norm_w)))
    mom = dict(zip(names, (m_attn_norm_w, m_w_in, m_b_gates, m_mlstm_norm_w, m_q_a_norm_w, m_w_q_b, m_kv_a_norm_w,
                           m_w_kv_b, m_w_out, m_ffn_norm_w, m_w_gate, m_w_up, m_w_down, m_final_norm_w)))
    vel = dict(zip(names, (v_attn_norm_w, v_w_in, v_b_gates, v_mlstm_norm_w, v_q_a_norm_w, v_w_q_b, v_kv_a_norm_w,
                           v_w_kv_b, v_w_out, v_ffn_norm_w, v_w_gate, v_w_up, v_w_down, v_final_norm_w)))
    t = x.shape[1]

    shards = {n: _stored(n, wts[n]).astype(BF16) for n in BIG}
    first, later = ("w_in",), tuple(n for n in BIG if n != "w_in")
    now = _gather_weights(first, [shards[n] for n in first])
    now, later_in = lax.optimization_barrier((now, [shards[n] for n in later]))
    behind = _gather_weights_async(later, later_in)
    full = {n: g.reshape(N_CHIPS * g.shape[1], g.shape[2]) for n, g in zip(first + later, list(now) + list(behind))}

    nw = {n: wts[n] for n in SMALL}
    by_shard = lambda g: g.reshape(N_CHIPS, g.shape[0] // N_CHIPS, g.shape[1])

    def start_reduce(group, tag, grads, grads_b, marker):
        g3 = [by_shard(grads[n]) for n in group]
        gb3 = [by_shard(grads_b[n]) if n in grads_b else g.astype(BF16) for n, g in zip(group, g3)]
        gb3, marker = lax.optimization_barrier((gb3, marker))
        return (g3, _exchange_contributions_async(group, gb3, tag)), marker

    loss, grad_x, grads, grads_b, (late_g3, late_got7) = _local_step(
        x[0], positions.reshape(t, 1), loss_target[0], nw, full, functools.partial(start_reduce, LATE, "late"))

    (early_g3, early_got7), late_got7 = start_reduce(EARLY, "early", grads, grads_b, list(late_got7))
    where = jnp.concatenate([_shard_index(), _core_index()])
    outs_g, outs_d, outs_m, outs_v = {}, {}, {}, {}

    def finish(group, g3, got7, tag):
        mine = [_shard_sum_direct(g, r7, n, where) for n, g, r7 in zip(group, g3, got7)]
        halves = _with_sibling_half(group, mine, tag)
        for n in group:
            res = _adamw_halves(_stored(n, wts[n]), _stored(n, mom[n]), _stored(n, vel[n]), *halves[n], n, where[1:])
            outs_g[n], outs_d[n], outs_m[n], outs_v[n] = [_unstored(n, r) for r in res]

    finish(LATE, late_g3, late_got7, "late")
    small_g, small_d, small_m, small_v, total_loss = _small_allreduce_adamw(grads, loss, wts, mom, vel)
    finish(EARLY, early_g3, early_got7, "early")
    outs_g.update(small_g)
    outs_d.update(small_d)
    outs_m.update(small_m)
    outs_v.update(small_v)
    return (total_loss, grad_x[None], *[outs_g[n] for n in names], *[outs_d[n] for n in names],
            *[outs_m[n] for n in names], *[outs_v[n] for n in names])
```

```python
import functools
import math

import jax
import jax.numpy as jnp
from jax import lax
from jax.experimental import pallas as pl
from jax.experimental.pallas import tpu as pltpu
from jax.experimental.pallas import tpu_sc as plsc

F32 = jnp.float32
BF16 = jnp.bfloat16

D_MODEL = 1024
M_HEADS = 4
M_HEAD_DIM = 128
M_WIDTH = M_HEADS * M_HEAD_DIM
M_CHUNK = 64
A_HEADS = 8
A_NOPE = 64
A_ROPE = 32
A_V = 64
A_QK_PAD = 128
Q_RANK = 384
KV_RANK = 256
ROPE_THETA = 10000.0
EPS = 1e-6
ATTN_SCALE = (A_NOPE + A_ROPE) ** -0.5
M_SCALE = M_HEAD_DIM ** -0.5

ADAM_LR = 0.001
ADAM_B1 = 0.9
ADAM_B2 = 0.999
ADAM_EPS = 1e-08
ADAM_WD = 0.01
ADAM_STEP = 10

VMEM_LIMIT_BYTES = 56 * 1024 * 1024
LANES = 128
NEG_INF = float("-inf")


def _cparams(*sem):
    return pltpu.CompilerParams(dimension_semantics=sem if sem else None, vmem_limit_bytes=VMEM_LIMIT_BYTES)


_DIMS = {"nn": (((1,), (0,)), ((), ())), "nt": (((1,), (1,)), ((), ())), "tn": (((0,), (0,)), ((), ()))}


def _matmul(a, b, mode, out_dtype, name, tm=None, tn=1024, residual=None, a_rows=None, b_rows=None, lead_scale=None,
            lead_row_scale=None, also_bf16=False, col_chunk=None):
    a_list = list(a) if isinstance(a, (list, tuple)) else [a]
    b_list = list(b) if isinstance(b, (list, tuple)) else [b]
    rows_list = list(b_rows) if isinstance(b_rows, (list, tuple)) else [b_rows] * len(b_list)
    assert len(a_list) == len(b_list) == len(rows_list)
    dims = _DIMS[mode]
    specs, m, n = [], None, None
    for aa, bb, rr in zip(a_list, b_list, rows_list):
        b_shape = bb.shape if rr is None else (rr, bb.shape[1])
        a_shape = aa.shape if a_rows is None else (a_rows, aa.shape[1])
        if mode == "nn":
            (m1, k), (k2, n1) = a_shape, b_shape
        elif mode == "nt":
            (m1, k), (n1, k2) = a_shape, b_shape
        else:
            (k, m1), (k2, n1) = aa.shape, b_shape
        assert k == k2 and (m is None or (m, n) == (m1, n1)), (aa.shape, bb.shape, mode)
        m, n = m1, n1
        specs.append(k)
    if tm is None:
        tm = 512 if mode == "tn" else 1024
    tm, tn = min(tm, m), min(tn, n)
    assert m % tm == 0 and n % tn == 0, (m, n, tm, tn)
    in_specs = []
    for k in specs:
        in_specs.append(pl.BlockSpec((k, tm), lambda i, j: (0, i)) if mode == "tn"
                        else pl.BlockSpec((tm, k), lambda i, j: (i, 0)))
        in_specs.append(pl.BlockSpec((tn, k), lambda i, j: (j, 0)) if mode == "nt"
                        else pl.BlockSpec((k, tn), lambda i, j: (0, j)))
    o_spec = pl.BlockSpec((tm, tn), lambda i, j: (i, j))
    n_pairs = len(specs)

    def body(*refs):
        acc = None
        for p in range(n_pairs):
            part = lax.dot_general(refs[2 * p][...].astype(BF16), refs[2 * p + 1][...].astype(BF16), dims,
                                   preferred_element_type=F32)
            acc = part if acc is None else acc + part
        if lead_scale is not None:
            col = pl.program_id(1) * tn + lax.broadcasted_iota(jnp.int32, (1, tn), 1)
            acc = acc * jnp.where(col < lead_scale[0], lead_scale[1], 1.0)
        if lead_row_scale is not None:
            row = pl.program_id(0) * tm + lax.broadcasted_iota(jnp.int32, (tm, 1), 0)
            acc = acc * jnp.where(row < lead_row_scale[0], lead_row_scale[1], 1.0)
        if residual is not None:
            acc = acc + refs[2 * n_pairs][...].astype(F32)
        outs = refs[2 * n_pairs + (residual is not None):]
        if col_chunk is not None:
            res = acc.astype(outs[0].dtype)
            for c in range(tn // col_chunk):
                outs[0][c] = res[:, c * col_chunk:(c + 1) * col_chunk]
            return
        outs[0][...] = acc.astype(outs[0].dtype)
        if also_bf16:
            outs[1][...] = acc.astype(BF16)

    ins = [x for pair in zip(a_list, b_list) for x in pair] + ([residual] if residual is not None else [])
    in_specs = in_specs + ([o_spec] if residual is not None else [])
    if col_chunk is not None:
        assert residual is None and not also_bf16 and tn % col_chunk == 0
        return pl.pallas_call(
            body, name=name, grid=(m // tm, n // tn), in_specs=in_specs,
            out_specs=pl.BlockSpec((tn // col_chunk, tm, col_chunk), lambda i, j: (j, i, 0)),
            out_shape=jax.ShapeDtypeStruct((n // col_chunk, m, col_chunk), out_dtype),
            compiler_params=_cparams("parallel", "parallel"),
        )(*ins)
    out_shape = jax.ShapeDtypeStruct((m, n), out_dtype)
    if also_bf16:
        return pl.pallas_call(
            body, name=name, grid=(m // tm, n // tn), in_specs=in_specs, out_specs=[o_spec, o_spec],
            out_shape=[out_shape, jax.ShapeDtypeStruct((m, n), BF16)],
            compiler_params=_cparams("parallel", "parallel"),
        )(*ins)
    return pl.pallas_call(
        body, name=name, grid=(m // tm, n // tn), in_specs=in_specs, out_specs=o_spec, out_shape=out_shape,
        compiler_params=_cparams("parallel", "parallel"),
    )(*ins)


def _rms(xf, w):
    return xf * lax.rsqrt(jnp.mean(xf * xf, axis=-1, keepdims=True) + EPS) * w


def _rms_bwd(dyf, xf, w):
    r = lax.rsqrt(jnp.mean(xf * xf, axis=-1, keepdims=True) + EPS)
    xh = xf * r
    dyh = dyf * w
    return r * (dyh - xh * jnp.mean(dyh * xh, axis=-1, keepdims=True)), dyf * xh


def _accumulate(ref, part):
    @pl.when(pl.program_id(0) == 0)
    def _():
        ref[...] = part

    @pl.when(pl.program_id(0) > 0)
    def _():
        ref[...] += part


def _rows(tm, width):
    return pl.BlockSpec((tm, width), lambda i: (i, 0))


def _whole(arr):
    return pl.BlockSpec(arr.shape, lambda i: (0,) * arr.ndim)


def _norm_proj_latents(x, w_norm, w_a2, w_g, w_qn, w_kvn, name, tm=512):
    t, d = x.shape
    tm = min(tm, t)
    n_a2 = Q_RANK + KV_RANK

    def body(x_ref, wn_ref, wa_ref, wg_ref, wq_ref, wkv_ref, u_ref, a2_ref, qn_ref, kvn_ref, g_ref):
        u = _rms(x_ref[...], wn_ref[...]).astype(BF16)
        u_ref[...] = u
        a2 = lax.dot_general(u, wa_ref[...], _DIMS["nt"], preferred_element_type=F32).astype(BF16)
        a2_ref[...] = a2
        qn_ref[...] = _rms(a2[:, :Q_RANK].astype(F32), wq_ref[...]).astype(BF16)
        kvn_ref[...] = _rms(a2[:, Q_RANK:].astype(F32), wkv_ref[...]).astype(BF16)
        g_ref[...] = lax.dot_general(u, wg_ref[...], _DIMS["nt"], preferred_element_type=F32)

    return pl.pallas_call(
        body, name=name, grid=(t // tm,),
        in_specs=[_rows(tm, d), _whole(w_norm), _whole(w_a2), _whole(w_g), _whole(w_qn), _whole(w_kvn)],
        out_specs=[_rows(tm, d), _rows(tm, n_a2), _rows(tm, Q_RANK), _rows(tm, KV_RANK), _rows(tm, LANES)],
        out_shape=[jax.ShapeDtypeStruct((t, d), BF16), jax.ShapeDtypeStruct((t, n_a2), BF16),
                   jax.ShapeDtypeStruct((t, Q_RANK), BF16), jax.ShapeDtypeStruct((t, KV_RANK), BF16),
                   jax.ShapeDtypeStruct((t, LANES), F32)],
        compiler_params=_cparams("parallel"),
    )(x, w_norm, w_a2, w_g, w_qn, w_kvn)


def _out_proj_norm(hm, ha, w_m, w_a, x, w_norm, name, tm=512):
    t = x.shape[0]
    tm = min(tm, t)

    def body(hm_ref, ha_ref, wm_ref, wa_ref, x_ref, wn_ref, h_ref, u_ref):
        h = (jnp.dot(hm_ref[...], wm_ref[...], preferred_element_type=F32)
             + jnp.dot(ha_ref[...], wa_ref[...], preferred_element_type=F32) + x_ref[...])
        h_ref[...] = h
        u_ref[...] = _rms(h, wn_ref[...]).astype(BF16)

    d = x.shape[1]
    return pl.pallas_call(
        body, name=name, grid=(t // tm,),
        in_specs=[_rows(tm, hm.shape[1]), _rows(tm, ha.shape[1]), _whole(w_m), _whole(w_a), _rows(tm, d), _whole(w_norm)],
        out_specs=[_rows(tm, d), _rows(tm, d)],
        out_shape=[jax.ShapeDtypeStruct((t, d), F32), jax.ShapeDtypeStruct((t, d), BF16)],
        compiler_params=_cparams("parallel"),
    )(hm, ha, w_m, w_a, x, w_norm)


def _ffn_down_loss(act, w_down, h1, target, w, name, tm=512):
    t, d = h1.shape
    tm = min(tm, t)

    def body(a_ref, wd_ref, h_ref, t_ref, w_ref, dh_ref, dhb_ref, loss_ref, dw_ref):
        xf = jnp.dot(a_ref[...], wd_ref[...], preferred_element_type=F32) + h_ref[...]
        r = lax.rsqrt(jnp.mean(xf * xf, axis=-1, keepdims=True) + EPS)
        xh = xf * r
        err = xh * w_ref[...] - t_ref[...]
        part_loss = 0.5 * jnp.sum(jnp.sum(err * err, axis=-1, keepdims=True), axis=0, keepdims=True) * (1.0 / d)
        dy = err * (1.0 / d)
        dyh = dy * w_ref[...]
        dh = r * (dyh - xh * jnp.mean(dyh * xh, axis=-1, keepdims=True))
        dh_ref[...] = dh
        dhb_ref[...] = dh.astype(BF16)
        _accumulate(dw_ref, jnp.sum(dy * xh, axis=0, keepdims=True))
        _accumulate(loss_ref, jnp.broadcast_to(part_loss, (1, LANES)))

    vec = pl.BlockSpec((1, d), lambda i: (0, 0))
    return pl.pallas_call(
        body, name=name, grid=(t // tm,),
        in_specs=[_rows(tm, act.shape[1]), _whole(w_down), _rows(tm, d), _rows(tm, d), vec],
        out_specs=[_rows(tm, d), _rows(tm, d), pl.BlockSpec((1, LANES), lambda i: (0, 0)), vec],
        out_shape=[jax.ShapeDtypeStruct((t, d), F32), jax.ShapeDtypeStruct((t, d), BF16),
                   jax.ShapeDtypeStruct((1, LANES), F32), jax.ShapeDtypeStruct((1, d), F32)],
        compiler_params=_cparams("arbitrary"),
    )(act, w_down, h1, target, w)


def _matmul_norm_bwd(das, ws, x, w_norm, name, out_dtype, residual=None, also_bf16=False, w_rows=None, tm=512):
    t, d = x.shape
    tm = min(tm, t)
    n = len(das)

    def body(*refs):
        x_ref, wn_ref = refs[2 * n], refs[2 * n + 1]
        outs = refs[2 * n + 2 + (residual is not None):]
        dy = None
        for p in range(n):
            wp = refs[n + p][...] if w_rows is None or w_rows[p] is None else refs[n + p][0:w_rows[p], :]
            part = jnp.dot(refs[p][...], wp, preferred_element_type=F32)
            dy = part if dy is None else dy + part
        dx, dw_terms = _rms_bwd(dy, x_ref[...].astype(F32), wn_ref[...])
        if residual is not None:
            dx = dx + refs[2 * n + 2][...]
        outs[0][...] = dx.astype(outs[0].dtype)
        if also_bf16:
            outs[1][...] = dx.astype(BF16)
        _accumulate(outs[-1], jnp.sum(dw_terms, axis=0, keepdims=True))

    extra = [jax.ShapeDtypeStruct((t, d), BF16)] if also_bf16 else []
    return pl.pallas_call(
        body, name=name, grid=(t // tm,),
        in_specs=([_rows(tm, a.shape[1]) for a in das] + [_whole(w) for w in ws] + [_rows(tm, d), _whole(w_norm)]
                  + ([_rows(tm, d)] if residual is not None else [])),
        out_specs=[_rows(tm, d)] * (1 + len(extra)) + [pl.BlockSpec((1, d), lambda i: (0, 0))],
        out_shape=[jax.ShapeDtypeStruct((t, d), out_dtype)] + extra + [jax.ShapeDtypeStruct((1, d), F32)],
        compiler_params=_cparams("arbitrary"),
    )(*das, *ws, x, w_norm, *([residual] if residual is not None else []))


FFN_TN = 1408


def _ffn_gate_up(u, w_gate, w_up, name, tm=1024):
    t, d = u.shape
    n = w_gate.shape[0]
    tm = min(tm, t)

    def body(u_ref, wg_ref, wu_ref, g_ref, up_ref, act_ref):
        uu = u_ref[...]
        g = lax.dot_general(uu, wg_ref[...], _DIMS["nt"], preferred_element_type=F32)
        up = lax.dot_general(uu, wu_ref[...], _DIMS["nt"], preferred_element_type=F32)
        g_b, up_b = g.astype(BF16), up.astype(BF16)
        g_ref[...] = g_b
        up_ref[...] = up_b
        gf, uf = g_b.astype(F32), up_b.astype(F32)
        act_ref[...] = (gf * jax.nn.sigmoid(gf) * uf).astype(act_ref.dtype)

    w_spec = pl.BlockSpec((FFN_TN, d), lambda i, j: (j, 0))
    o_spec = pl.BlockSpec((tm, FFN_TN), lambda i, j: (i, j))
    out = jax.ShapeDtypeStruct((t, n), BF16)
    return pl.pallas_call(
        body, name=name, grid=(t // tm, n // FFN_TN),
        in_specs=[pl.BlockSpec((tm, d), lambda i, j: (i, 0)), w_spec, w_spec], out_specs=[o_spec] * 3,
        out_shape=[out] * 3, compiler_params=_cparams("parallel", "parallel"),
    )(u, w_gate, w_up)


def _ffn_down_bwd(dh, w_down, g, up, name, tm=1024):
    t, d = dh.shape
    n = w_down.shape[0]
    tm = min(tm, t)

    def body(dh_ref, w_ref, g_ref, up_ref, dg_ref, du_ref):
        df = lax.dot_general(dh_ref[...], w_ref[...], _DIMS["nt"], preferred_element_type=F32).astype(BF16).astype(F32)
        gf = g_ref[...].astype(F32)
        uf = up_ref[...].astype(F32)
        s = jax.nn.sigmoid(gf)
        dg_ref[...] = (df * uf * s * (1.0 + gf * (1.0 - s))).astype(dg_ref.dtype)
        du_ref[...] = (df * gf * s).astype(du_ref.dtype)

    o_spec = pl.BlockSpec((tm, FFN_TN), lambda i, j: (i, j))
    out = jax.ShapeDtypeStruct((t, n), BF16)
    return pl.pallas_call(
        body, name=name, grid=(t // tm, n // FFN_TN),
        in_specs=[pl.BlockSpec((tm, d), lambda i, j: (i, 0)), pl.BlockSpec((FFN_TN, d), lambda i, j: (j, 0)),
                  o_spec, o_spec],
        out_specs=[o_spec, o_spec], out_shape=[out, out], compiler_params=_cparams("parallel", "parallel"),
    )(dh, w_down, g, up)


M_BLOCK = 512
M_CHUNKS_PER_BLOCK = M_BLOCK // M_CHUNK
M_UNROLL = 2


def _log_sigmoid(z):
    return jnp.minimum(z, 0.0) - jnp.log(1.0 + jnp.exp(-jnp.abs(z)))


def _per_head(fn):
    return jnp.stack([fn(hd) for hd in range(M_HEADS)])


def _mlstm_chunk_fwd(q, k, v, gc, c_prev, n_prev, m_prev):
    L, H = M_CHUNK, M_HEADS
    tt = lax.broadcasted_iota(jnp.int32, (L, L), 0)
    ss = lax.broadcasted_iota(jnp.int32, (L, L), 1)
    eye = tt == ss
    causal = ss <= tt
    gct = jnp.transpose(gc)
    i_col = _per_head(lambda hd: gc[:, hd:hd + 1])
    f_col = _per_head(lambda hd: gc[:, H + hd:H + hd + 1])
    i_row = _per_head(lambda hd: gct[hd:hd + 1, :])
    lf_col = _log_sigmoid(f_col)
    lf_row = _log_sigmoid(_per_head(lambda hd: gct[H + hd:H + hd + 1, :]))
    b_col = jnp.sum(jnp.where(causal, lf_row, 0.0), axis=2, keepdims=True)
    b_row = jnp.sum(jnp.where(tt <= ss, lf_col, 0.0), axis=1, keepdims=True)
    g = jnp.sum(lf_col, axis=1, keepdims=True)
    a_row = g - b_row + i_row
    a_col = g - b_col + i_col
    m_loc = jnp.max(a_row, axis=2, keepdims=True)
    d_log = jnp.where(causal, b_col - b_row + i_row, NEG_INF)
    inter = b_col + m_prev
    m_t = jnp.maximum(jnp.max(d_log, axis=2, keepdims=True), inter)
    dmat = jnp.exp(d_log - m_t)
    amat = _per_head(lambda hd: lax.dot_general(q[hd], k[hd], _DIMS["nt"], preferred_element_type=F32))
    p = dmat * amat
    sig = jnp.exp(inter - m_t)
    c_prev_b = [c.astype(BF16) for c in c_prev]
    qc = _per_head(lambda hd: jnp.dot(q[hd], c_prev_b[hd], preferred_element_type=F32))
    p_b = p.astype(BF16)
    num = _per_head(lambda hd: jnp.dot(p_b[hd], v[hd], preferred_element_type=F32)) + sig * qc
    qf = _per_head(lambda hd: q[hd].astype(F32))
    qn = jnp.sum(qf * n_prev, axis=2, keepdims=True)
    den_raw = jnp.sum(p, axis=2, keepdims=True) + sig * qn
    floor = jnp.exp(-m_t)
    den = jnp.maximum(jnp.abs(den_raw), floor)
    h = num / den
    m_new = jnp.maximum(g + m_prev, m_loc)
    w_col = jnp.exp(a_col - m_new)
    w_row = jnp.exp(a_row - m_new)
    alpha = jnp.exp(g + m_prev - m_new)
    return dict(eye=eye, causal=causal, tt=tt, ss=ss, i_col=i_col, f_col=f_col, dmat=dmat, amat=amat, p=p, p_b=p_b,
                sig=sig, qc=qc, qf=qf, qn=qn, num=num, den_raw=den_raw, floor=floor, den=den, h=h, m_new=m_new,
                w_col=w_col, w_row=w_row, alpha=alpha, c_prev_b=c_prev_b, b_row=b_row, b_col=b_col, m_t=m_t)


def _mlstm_fwd(m4, kt, gates, bias, norm_w, name):
    t = m4.shape[0]
    nblk = t // M_BLOCK
    nc = t // M_CHUNK
    L, dh = M_CHUNK, M_HEAD_DIM

    def body(m4_ref, kt_ref, g_ref, b_ref, w_ref, hm_ref, cp_ref, nm_ref, c_s, n_s, m_s):
        @pl.when(pl.program_id(0) == 0)
        def _():
            c_s[...] = jnp.zeros_like(c_s)
            n_s[...] = jnp.zeros_like(n_s)
            m_s[...] = jnp.zeros_like(m_s)

        row8 = lax.broadcasted_iota(jnp.int32, (8, dh), 0)

        def chunk(c, carry):
            rows = pl.ds(pl.multiple_of(c * L, L), L)
            gc = g_ref[rows, :] + b_ref[...]
            col = lambda part, hd: m4_ref[rows, part * M_WIDTH + hd * dh:part * M_WIDTH + (hd + 1) * dh]
            q = [col(0, hd) for hd in range(M_HEADS)]
            k = [col(1, hd) for hd in range(M_HEADS)]
            v = [col(2, hd) for hd in range(M_HEADS)]
            o = _per_head(lambda hd: col(3, hd).astype(F32))
            c_prev = [c_s[hd] for hd in range(M_HEADS)]
            n_prev = n_s[:, 0:1, :]
            m_prev = m_s[:, 0:1, 0:1]
            r = _mlstm_chunk_fwd(q, k, v, gc, c_prev, n_prev, m_prev)
            h = r["h"]
            hn = h * lax.rsqrt(jnp.mean(h * h, axis=-1, keepdims=True) + EPS) * w_ref[...]
            hm = (hn * jax.nn.sigmoid(o)).astype(hm_ref.dtype)
            nm = jnp.where(row8 == 0, n_prev, jnp.where(row8 == 1, m_prev, 0.0))
            kf = _per_head(lambda hd: k[hd].astype(F32))
            n_s[:, 0:1, :] = r["alpha"] * n_prev + jnp.sum(kf * r["w_col"], axis=1, keepdims=True)
            m_s[...] = jnp.broadcast_to(r["m_new"], (M_HEADS, 8, dh))
            for hd in range(M_HEADS):
                hm_ref[rows, hd * dh:(hd + 1) * dh] = hm[hd]
                cp_ref[hd, c] = r["c_prev_b"][hd]
                nm_ref[hd, c] = nm[hd]
                kw_t = (kt_ref[c, hd * dh:(hd + 1) * dh, :].astype(F32) * r["w_row"][hd]).astype(BF16)
                c_s[hd] = r["alpha"][hd] * c_prev[hd] + jnp.dot(kw_t, v[hd], preferred_element_type=F32)
            return carry

        lax.fori_loop(0, M_CHUNKS_PER_BLOCK, chunk, 0, unroll=M_UNROLL)

    return pl.pallas_call(
        body, name=name, grid=(nblk,),
        in_specs=[pl.BlockSpec((M_BLOCK, 4 * M_WIDTH), lambda i: (i, 0)),
                  pl.BlockSpec((M_CHUNKS_PER_BLOCK, M_WIDTH, L), lambda i: (i, 1, 0)),
                  pl.BlockSpec((M_BLOCK, LANES), lambda i: (i, 0)),
                  pl.BlockSpec((1, LANES), lambda i: (0, 0)),
                  pl.BlockSpec((M_HEADS, 1, dh), lambda i: (0, 0, 0))],
        out_specs=[pl.BlockSpec((M_BLOCK, M_WIDTH), lambda i: (i, 0)),
                   pl.BlockSpec((M_HEADS, M_CHUNKS_PER_BLOCK, dh, dh), lambda i: (0, i, 0, 0)),
                   pl.BlockSpec((M_HEADS, M_CHUNKS_PER_BLOCK, 8, dh), lambda i: (0, i, 0, 0))],
        out_shape=[jax.ShapeDtypeStruct((t, M_WIDTH), BF16),
                   jax.ShapeDtypeStruct((M_HEADS, nc, dh, dh), BF16),
                   jax.ShapeDtypeStruct((M_HEADS, nc, 8, dh), F32)],
        scratch_shapes=[pltpu.VMEM((M_HEADS, dh, dh), F32), pltpu.VMEM((M_HEADS, 8, dh), F32),
                        pltpu.VMEM((M_HEADS, 8, dh), F32)],
        compiler_params=_cparams("arbitrary"),
    )(m4, kt, gates, bias, norm_w)


def _mlstm_bwd(m4, qt, gates, bias, norm_w, c_prev_all, nm_all, dhm, name):
    t = m4.shape[0]
    nblk = t // M_BLOCK
    L, dh = M_CHUNK, M_HEAD_DIM

    def body(m4_ref, qt_ref, g_ref, b_ref, w_ref, cp_ref, nm_ref, dhm_ref, dm4_ref, dg_ref, small_ref, dc_s, dn_s):
        @pl.when(pl.program_id(0) == 0)
        def _():
            dc_s[...] = jnp.zeros_like(dc_s)
            dn_s[...] = jnp.zeros_like(dn_s)
            small_ref[...] = jnp.zeros_like(small_ref)

        lane = lax.broadcasted_iota(jnp.int32, (L, LANES), 1)
        row8 = lax.broadcasted_iota(jnp.int32, (8, LANES), 0)

        def chunk(ci, carry):
            c = M_CHUNKS_PER_BLOCK - 1 - ci
            rows = pl.ds(pl.multiple_of(c * L, L), L)
            gc = g_ref[rows, :] + b_ref[...]
            heads = range(M_HEADS)
            col = lambda part, hd: m4_ref[rows, part * M_WIDTH + hd * dh:part * M_WIDTH + (hd + 1) * dh]
            q = [col(0, hd) for hd in heads]
            k = [col(1, hd) for hd in heads]
            v = [col(2, hd) for hd in heads]
            o = _per_head(lambda hd: col(3, hd).astype(F32))
            q_t = [qt_ref[c, hd * dh:(hd + 1) * dh, :] for hd in heads]
            c_prev = [cp_ref[hd, c].astype(F32) for hd in heads]
            nm = _per_head(lambda hd: nm_ref[hd, c])
            n_prev = nm[:, 0:1, :]
            m_prev = nm[:, 1:2, 0:1]
            r = _mlstm_chunk_fwd(q, k, v, gc, c_prev, n_prev, m_prev)
            eye, tt, ss = r["eye"], r["tt"], r["ss"]
            h, den, sig, p = r["h"], r["den"], r["sig"], r["p"]
            w_col, alpha, c_prev_b = r["w_col"], r["alpha"], r["c_prev_b"]
            to_row = lambda colv: jnp.sum(jnp.where(eye, colv, 0.0), axis=1, keepdims=True)
            to_col = lambda rowv: jnp.sum(jnp.where(eye, rowv, 0.0), axis=2, keepdims=True)
            mm = _per_head

            w_h = w_ref[...]
            rn = lax.rsqrt(jnp.mean(h * h, axis=-1, keepdims=True) + EPS)
            hh = h * rn
            og = jax.nn.sigmoid(o)
            dhm_c = _per_head(lambda hd: dhm_ref[rows, hd * dh:(hd + 1) * dh].astype(F32))
            dhn = dhm_c * og
            d_o = dhm_c * hh * w_h * og * (1.0 - og)
            d_norm = jnp.sum(dhn * hh, axis=1, keepdims=True)
            dhh = dhn * w_h
            dh_ = rn * (dhh - hh * jnp.mean(dhh * hh, axis=-1, keepdims=True))

            dnum = dh_ / den
            dden = -jnp.sum(dh_ * h, axis=-1, keepdims=True) / den
            dden_raw = jnp.where(jnp.abs(r["den_raw"]) >= r["floor"], dden * jnp.sign(r["den_raw"]), 0.0)
            dnum_b = dnum.astype(BF16)
            dp = mm(lambda hd: lax.dot_general(dnum_b[hd], v[hd], _DIMS["nt"], preferred_element_type=F32)) + dden_raw
            dc = [dc_s[hd] for hd in heads]
            dn = dn_s[:, 0:1, :]
            dc_b = [d.astype(BF16) for d in dc]
            g2_b = (sig * dnum).astype(BF16)
            sd = sig * dden_raw
            da_mat = (dp * r["dmat"]).astype(BF16)
            dqs = mm(lambda hd: lax.dot_general(g2_b[hd], c_prev_b[hd], _DIMS["nt"], preferred_element_type=F32)
                     + jnp.dot(da_mat[hd], k[hd], preferred_element_type=F32)) + sd * n_prev
            r_mat = mm(lambda hd: lax.dot_general(v[hd], dc_b[hd], _DIMS["nt"], preferred_element_type=F32)) + dn
            kf = _per_head(lambda hd: k[hd].astype(F32))
            dmat_t = jnp.exp(jnp.where(tt <= ss, r["b_row"] - r["b_col"] + r["i_col"], NEG_INF) - to_row(r["m_t"]))
            p_t = (dmat_t * mm(lambda hd: jnp.dot(k[hd], q_t[hd], preferred_element_type=F32))).astype(BF16)
            dp_t = mm(lambda hd: lax.dot_general(v[hd], dnum_b[hd], _DIMS["nt"], preferred_element_type=F32))
            da_t = ((dp_t + to_row(dden_raw)) * dmat_t).astype(BF16)
            d_k = mm(lambda hd: jnp.dot(da_t[hd], q[hd], preferred_element_type=F32)) + w_col * r_mat
            d_v = (mm(lambda hd: jnp.dot(p_t[hd], dnum_b[hd], preferred_element_type=F32))
                   + w_col * mm(lambda hd: jnp.dot(k[hd], dc_b[hd], preferred_element_type=F32)))
            da_col = jnp.sum(kf * r_mat, axis=-1, keepdims=True) * w_col
            dsig = jnp.sum(dnum * r["qc"], axis=-1, keepdims=True) + dden_raw * r["qn"]
            pp = dp * p
            r1 = jnp.sum(pp, axis=2, keepdims=True)
            c1_col = to_col(jnp.sum(pp, axis=1, keepdims=True))
            dcc = _per_head(lambda hd: dc[hd] * c_prev[hd])
            d_alpha = (jnp.sum(jnp.sum(dcc, axis=2, keepdims=True), axis=1, keepdims=True)
                       + jnp.sum(dn * n_prev, axis=2, keepdims=True))
            dgl = jnp.sum(da_col, axis=1, keepdims=True) + d_alpha * alpha
            db_col = r1 + dsig * sig - c1_col - da_col
            dli_col = c1_col + da_col
            dlf_col = jnp.sum(jnp.where(ss >= tt, to_row(db_col), 0.0), axis=2, keepdims=True) + dgl
            df_col = dlf_col * jax.nn.sigmoid(-r["f_col"])

            dn_s[:, 0:1, :] = alpha * dn + jnp.sum(sd * r["qf"], axis=1, keepdims=True)
            dq_out = (dqs * M_SCALE).astype(dm4_ref.dtype)
            dk_out, dv_out, do_out = d_k.astype(dm4_ref.dtype), d_v.astype(dm4_ref.dtype), d_o.astype(dm4_ref.dtype)
            dg_tile = jnp.zeros((L, LANES), F32)
            small = jnp.zeros((8, LANES), F32)
            for hd in heads:
                dc_s[hd] = alpha[hd] * dc[hd] + jnp.dot(q_t[hd], g2_b[hd], preferred_element_type=F32)
                dm4_ref[rows, hd * dh:(hd + 1) * dh] = dq_out[hd]
                dm4_ref[rows, M_WIDTH + hd * dh:M_WIDTH + (hd + 1) * dh] = dk_out[hd]
                dm4_ref[rows, 2 * M_WIDTH + hd * dh:2 * M_WIDTH + (hd + 1) * dh] = dv_out[hd]
                dm4_ref[rows, 3 * M_WIDTH + hd * dh:3 * M_WIDTH + (hd + 1) * dh] = do_out[hd]
                dg_tile = (dg_tile + jnp.where(lane == hd, dli_col[hd], 0.0)
                           + jnp.where(lane == M_HEADS + hd, df_col[hd], 0.0))
                small = small + jnp.where(row8 == hd, d_norm[hd], 0.0)
            dg_ref[rows, :] = dg_tile
            small = small + jnp.where(row8 == M_HEADS, jnp.sum(dg_tile, axis=0, keepdims=True), 0.0)
            small_ref[...] += small
            return carry

        lax.fori_loop(0, M_CHUNKS_PER_BLOCK, chunk, 0, unroll=M_UNROLL)

    rev = lambda i: nblk - 1 - i
    return pl.pallas_call(
        body, name=name, grid=(nblk,),
        in_specs=[pl.BlockSpec((M_BLOCK, 4 * M_WIDTH), lambda i: (rev(i), 0)),
                  pl.BlockSpec((M_CHUNKS_PER_BLOCK, M_WIDTH, L), lambda i: (rev(i), 0, 0)),
                  pl.BlockSpec((M_BLOCK, LANES), lambda i: (rev(i), 0)),
                  pl.BlockSpec((1, LANES), lambda i: (0, 0)),
                  pl.BlockSpec((M_HEADS, 1, dh), lambda i: (0, 0, 0)),
                  pl.BlockSpec((M_HEADS, M_CHUNKS_PER_BLOCK, dh, dh), lambda i: (0, rev(i), 0, 0)),
                  pl.BlockSpec((M_HEADS, M_CHUNKS_PER_BLOCK, 8, dh), lambda i: (0, rev(i), 0, 0)),
                  pl.BlockSpec((M_BLOCK, M_WIDTH), lambda i: (rev(i), 0))],
        out_specs=[pl.BlockSpec((M_BLOCK, 4 * M_WIDTH), lambda i: (rev(i), 0)),
                   pl.BlockSpec((M_BLOCK, LANES), lambda i: (rev(i), 0)),
                   pl.BlockSpec((8, LANES), lambda i: (0, 0))],
        out_shape=[jax.ShapeDtypeStruct((t, 4 * M_WIDTH), BF16), jax.ShapeDtypeStruct((t, LANES), F32),
                   jax.ShapeDtypeStruct((8, LANES), F32)],
        scratch_shapes=[pltpu.VMEM((M_HEADS, dh, dh), F32), pltpu.VMEM((M_HEADS, 8, dh), F32)],
        compiler_params=_cparams("arbitrary"),
    )(m4, qt, gates, bias, norm_w, c_prev_all, nm_all, dhm)


def _rope_tables(pos_col, width):
    lane = lax.broadcasted_iota(jnp.int32, (1, width), 1) % A_QK_PAD
    half = A_ROPE // 2
    first = (lane >= A_NOPE) & (lane < A_NOPE + half)
    second = (lane >= A_NOPE + half) & (lane < A_NOPE + A_ROPE)
    idx = jnp.where(first, lane - A_NOPE, lane - A_NOPE - half).astype(F32)
    inv_freq = jnp.exp(idx * (-math.log(ROPE_THETA) / half))
    ang = pos_col.astype(F32) * inv_freq
    cos, sin = jnp.cos(ang), jnp.sin(ang)
    rot = first | second
    return jnp.where(rot, cos, 1.0), jnp.where(first, -sin, 0.0), jnp.where(second, sin, 0.0)


def _rope_apply(vv, cosf, s1, s2):
    half = A_ROPE // 2
    w = vv.shape[-1]
    return vv * cosf + pltpu.roll(vv, w - half, 1) * s1 + pltpu.roll(vv, half, 1) * s2


def _rope_apply_t(dd, cosf, s1, s2):
    half = A_ROPE // 2
    w = dd.shape[-1]
    return dd * cosf + pltpu.roll(dd * s1, half, 1) + pltpu.roll(dd * s2, w - half, 1)


A_BIAS_LANE_K = A_NOPE + A_ROPE
A_BIAS_LANE_V = A_V


def _hi_lo(val):
    hi = val.astype(BF16)
    return hi, (val - hi.astype(F32)).astype(BF16)


def _qkv_up_rope(qa_n, kv_n, w_qb, w_k, w_v, gates, pos, name, tm=512):
    t = qa_n.shape[0]
    w = A_HEADS * LANES

    def body(qa_ref, kv_ref, wq_ref, wk_ref, wv_ref, g_ref, p_ref, qo_ref, ko_ref, vo_ref):
        cosf, s1, s2 = _rope_tables(p_ref[...], LANES)
        lane = lax.broadcasted_iota(jnp.int32, (1, LANES), 1)
        kr = jnp.where((lane >= A_NOPE) & (lane < A_NOPE + A_ROPE), g_ref[...], 0.0)
        kr = _rope_apply(kr, cosf, s1, s2)
        kr = jnp.where((lane == A_BIAS_LANE_K) | (lane == A_BIAS_LANE_K + 1), 1.0, kr)
        v_one = (lane == A_BIAS_LANE_V) | (lane == A_BIAS_LANE_V + 1)
        qpre = lax.dot_general(qa_ref[...], wq_ref[...], _DIMS["nt"], preferred_element_type=F32)
        kv = kv_ref[...]
        kpre = lax.dot_general(kv, wk_ref[...], _DIMS["nt"], preferred_element_type=F32)
        vpre = lax.dot_general(kv, wv_ref[...], _DIMS["nt"], preferred_element_type=F32)
        for hd in range(A_HEADS):
            sl = slice(hd * LANES, (hd + 1) * LANES)
            qo_ref[:, sl] = (_rope_apply(qpre[:, sl], cosf, s1, s2) * ATTN_SCALE).astype(qo_ref.dtype)
            ko_ref[:, sl] = (kpre[:, sl] + kr).astype(ko_ref.dtype)
            vo_ref[:, sl] = jnp.where(v_one, 1.0, vpre[:, sl]).astype(vo_ref.dtype)

    rows = lambda width: pl.BlockSpec((tm, width), lambda i: (i, 0))
    whole = lambda arr: pl.BlockSpec(arr.shape, lambda i: (0, 0))
    out = jax.ShapeDtypeStruct((t, w), BF16)
    return pl.pallas_call(
        body, name=name, grid=(t // tm,),
        in_specs=[rows(qa_n.shape[1]), rows(kv_n.shape[1]), whole(w_qb), whole(w_k), whole(w_v), rows(LANES), rows(1)],
        out_specs=[rows(w)] * 3, out_shape=[out] * 3, compiler_params=_cparams("parallel"),
    )(qa_n, kv_n, w_qb, w_k, w_v, gates, pos)


def _rope_bwd(dq, dk, dgates, pos, name, tm=512):
    t, w = dq.shape

    def body(dq_ref, dk_ref, dg_ref, p_ref, dqo_ref, dgo_ref):
        cosf, s1, s2 = _rope_tables(p_ref[...], LANES)
        acc = jnp.zeros((tm, LANES), F32)
        for hd in range(A_HEADS):
            sl = slice(hd * LANES, (hd + 1) * LANES)
            dqo_ref[:, sl] = (_rope_apply_t(dq_ref[:, sl], cosf, s1, s2) * ATTN_SCALE).astype(dqo_ref.dtype)
            acc = acc + dk_ref[:, sl].astype(F32)
        lane = lax.broadcasted_iota(jnp.int32, (1, LANES), 1)
        dkr = _rope_apply_t(acc, cosf, s1, s2)
        dkr = jnp.where((lane >= A_NOPE) & (lane < A_NOPE + A_ROPE), dkr, 0.0)
        dgo_ref[...] = (dg_ref[...] + dkr).astype(dgo_ref.dtype)

    row = pl.BlockSpec((tm, w), lambda i: (i, 0))
    nar = pl.BlockSpec((tm, LANES), lambda i: (i, 0))
    return pl.pallas_call(
        body, name=name, grid=(t // tm,),
        in_specs=[row, row, nar, pl.BlockSpec((tm, 1), lambda i: (i, 0))],
        out_specs=[row, nar],
        out_shape=[jax.ShapeDtypeStruct((t, w), BF16), jax.ShapeDtypeStruct((t, LANES), BF16)],
        compiler_params=_cparams("parallel"),
    )(dq, dk, dgates, pos)


A_TQ = 512
A_TK = 512
A_TQ_FWD = 1024


A_HEADS_PER_STEP = 4
A_HEADS_PER_STEP_FWD = 4


def _flash_fwd(q, k, v, name):
    t = q.shape[0]
    tq, tk = min(A_TQ_FWD, t), A_TK
    nh = A_HEADS_PER_STEP_FWD
    wblk = nh * LANES

    def body(q_ref, k_ref, v_ref, o_ref, qb_ref):
        i = pl.program_id(1)
        lane = lax.broadcasted_iota(jnp.int32, (tk, LANES), 1)
        causal = lax.broadcasted_iota(jnp.int32, (tk, tk), 1) <= lax.broadcasted_iota(jnp.int32, (tk, tk), 0)
        head = lambda ref, rows, hh: ref[rows, hh * LANES:(hh + 1) * LANES]

        def tile(qh, kh, vh, m, acc, mask):
            s = lax.dot_general(qh, kh, _DIMS["nt"], preferred_element_type=F32)
            if mask is not None:
                s = jnp.where(mask, s, NEG_INF)
            m_new = jnp.maximum(m, jnp.max(s, axis=1, keepdims=True))
            p = jnp.exp(s - m_new)
            return m_new, jnp.exp(m - m_new) * acc + jnp.dot(p.astype(BF16), vh, preferred_element_type=F32)

        def step(kb, carry):
            rows = pl.ds(pl.multiple_of(kb * tk, tk), tk)
            return tuple(tile(head(q_ref, slice(None), hh), head(k_ref, rows, hh), head(v_ref, rows, hh), *carry[hh], None)
                         for hh in range(nh))

        carry = tuple((jnp.full((tq, 1), NEG_INF, F32), jnp.zeros((tq, LANES), F32)) for _ in range(nh))
        n_full = (i * tq) // tk
        carry = lax.fori_loop(0, n_full, step, carry)
        blocks = [slice(r * tk, (r + 1) * tk) for r in range(tq // tk)]
        parts = [[(m[b], acc[b]) for b in blocks] for m, acc in carry]
        for d in range(len(blocks)):
            rows = pl.ds(pl.multiple_of((n_full + d) * tk, tk), tk)
            for hh in range(nh):
                kh, vh = head(k_ref, rows, hh), head(v_ref, rows, hh)
                for r in range(d, len(blocks)):
                    parts[hh][r] = tile(head(q_ref, blocks[r], hh), kh, vh, *parts[hh][r], causal if r == d else None)
        for hh in range(nh):
            for b, (m, acc) in zip(blocks, parts[hh]):
                l = acc[:, A_BIAS_LANE_V:A_BIAS_LANE_V + 1]
                o_ref[b, hh * LANES:(hh + 1) * LANES] = (acc / l).astype(o_ref.dtype)
                hi, lo = _hi_lo(-(m + jnp.log(l)))
                qb_ref[b, hh * LANES:(hh + 1) * LANES] = jnp.where(
                    lane == A_BIAS_LANE_K, hi, jnp.where(lane == A_BIAS_LANE_K + 1, lo, head(q_ref, b, hh)))

    blk = pl.BlockSpec((tq, wblk), lambda j, i: (i, j))
    res = pl.BlockSpec((t, wblk), lambda j, i: (0, j))
    out = jax.ShapeDtypeStruct((t, A_HEADS * LANES), BF16)
    return pl.pallas_call(
        body, name=name, grid=(A_HEADS // nh, t // tq), in_specs=[blk, res, res], out_specs=[blk, blk],
        out_shape=[out, out], compiler_params=_cparams("parallel", "parallel"),
    )(q, k, v)


def _d_out_proj(dh, w_out_m, w_out_a, o, name, tm=512):
    t, w = o.shape
    tm = min(tm, t)

    def body(dh_ref, wm_ref, w_ref, o_ref, dhm_ref, out_ref):
        dh_blk = dh_ref[...]
        dhm_ref[...] = lax.dot_general(dh_blk, wm_ref[...], _DIMS["nt"], preferred_element_type=F32).astype(BF16)
        do = lax.dot_general(dh_blk, w_ref[...], _DIMS["nt"], preferred_element_type=F32).astype(BF16)
        lane = lax.broadcasted_iota(jnp.int32, (tm, LANES), 1)
        for hd in range(A_HEADS):
            sl = slice(hd * LANES, (hd + 1) * LANES)
            d = do[:, sl]
            delta = jnp.sum(jnp.where(lane < A_V, d.astype(F32) * o_ref[:, sl].astype(F32), 0.0), axis=1, keepdims=True)
            hi, lo = _hi_lo(-delta)
            out_ref[:, sl] = jnp.where(lane == A_BIAS_LANE_V, hi, jnp.where(lane == A_BIAS_LANE_V + 1, lo, d))

    return pl.pallas_call(
        body, name=name, grid=(t // tm,),
        in_specs=[_rows(tm, dh.shape[1]), _whole(w_out_m), _whole(w_out_a), _rows(tm, w)],
        out_specs=[_rows(tm, w_out_m.shape[0]), _rows(tm, w)],
        out_shape=[jax.ShapeDtypeStruct((t, w_out_m.shape[0]), BF16), jax.ShapeDtypeStruct((t, w), BF16)],
        compiler_params=_cparams("parallel"),
    )(dh, w_out_m, w_out_a, o)


def _flash_bwd(qb, k, v, doe, name):
    t = qb.shape[0]
    tq, tk = A_TQ, A_TK
    assert tq == tk
    nh = A_HEADS_PER_STEP
    wblk = nh * LANES
    nq = t // tq

    def body(q_ref, k_ref, v_ref, do_ref, dq_ref, dk_ref, dv_ref):
        kb = pl.program_id(1)

        @pl.when(kb == 0)
        def _():
            dq_ref[...] = jnp.zeros_like(dq_ref)

        kpos = kb * tk + lax.broadcasted_iota(jnp.int32, (tk, tq), 0)
        qofs = lax.broadcasted_iota(jnp.int32, (tk, tq), 1)
        def step(qi, carry, masked):
            rows = pl.ds(pl.multiple_of(qi * tq, tq), tq)
            new = []
            for hh in range(nh):
                dk_acc, dv_acc = carry[hh]
                kh = k_ref[:, hh * LANES:(hh + 1) * LANES]
                vh = v_ref[:, hh * LANES:(hh + 1) * LANES]
                qh = q_ref[rows, hh * LANES:(hh + 1) * LANES]
                doh = do_ref[rows, hh * LANES:(hh + 1) * LANES]
                p_t = jnp.exp(lax.dot_general(kh, qh, _DIMS["nt"], preferred_element_type=F32))
                if masked:
                    p_t = jnp.where(qi * tq + qofs >= kpos, p_t, 0.0)
                ds_t = (p_t * lax.dot_general(vh, doh, _DIMS["nt"], preferred_element_type=F32)).astype(BF16)
                dv_acc = dv_acc + jnp.dot(p_t.astype(BF16), doh, preferred_element_type=F32)
                dk_acc = dk_acc + jnp.dot(ds_t, qh, preferred_element_type=F32)
                dq_ref[rows, hh * LANES:(hh + 1) * LANES] += lax.dot_general(ds_t, kh, _DIMS["tn"],
                                                                             preferred_element_type=F32)
                new.append((dk_acc, dv_acc))
            return tuple(new)

        carry = tuple((jnp.zeros((tk, LANES), F32), jnp.zeros((tk, LANES), F32)) for _ in range(nh))
        carry = step(kb, carry, True)
        carry = lax.fori_loop(kb + 1, nq, functools.partial(step, masked=False), carry)
        for hh in range(nh):
            dk_ref[:, hh * LANES:(hh + 1) * LANES] = carry[hh][0].astype(dk_ref.dtype)
            dv_ref[:, hh * LANES:(hh + 1) * LANES] = carry[hh][1].astype(dv_ref.dtype)

    res = pl.BlockSpec((t, wblk), lambda j, kb: (0, j))
    blk = pl.BlockSpec((tk, wblk), lambda j, kb: (kb, j))
    wide = (t, A_HEADS * LANES)
    return pl.pallas_call(
        body, name=name, grid=(A_HEADS // nh, t // tk), in_specs=[res, blk, blk, res], out_specs=[res, blk, blk],
        out_shape=[jax.ShapeDtypeStruct(wide, F32), jax.ShapeDtypeStruct(wide, BF16), jax.ShapeDtypeStruct(wide, BF16)],
        compiler_params=_cparams("parallel", "arbitrary"),
    )(qb, k, v, doe)


_SPLITS = (M_WIDTH, M_WIDTH, M_WIDTH, M_WIDTH, M_HEADS, M_HEADS, Q_RANK, KV_RANK, A_ROPE)
_OFFS = tuple(sum(_SPLITS[:i]) for i in range(len(_SPLITS) + 1))
_GATE_BLOCK_KR = A_NOPE


def _split_w_in_t(wt):
    z = lambda n: jnp.zeros((n, wt.shape[1]), wt.dtype)
    w_g = jnp.concatenate([wt[_OFFS[4]:_OFFS[6]], z(_GATE_BLOCK_KR - 2 * M_HEADS), wt[_OFFS[8]:_OFFS[9]],
                           z(LANES - _GATE_BLOCK_KR - A_ROPE)], axis=0)
    return wt[_OFFS[6]:_OFFS[8]], w_g


def _merge_w_in_grad_t(g_m4, g_a2, g_g):
    return jnp.concatenate([g_m4, g_g[:2 * M_HEADS], g_a2, g_g[_GATE_BLOCK_KR:_GATE_BLOCK_KR + A_ROPE]], axis=0)


def _pad_heads(w, used):
    w3 = w.reshape(A_HEADS, used, w.shape[1])
    return jnp.pad(w3, ((0, 0), (0, LANES - used), (0, 0))).reshape(A_HEADS * LANES, w.shape[1])


def _unpad_heads(g, used):
    return g.reshape(A_HEADS, LANES, g.shape[1])[:, :used].reshape(A_HEADS * used, g.shape[1])


def _split_w_kv_b_t(wt):
    w3 = wt.reshape(A_HEADS, A_NOPE + A_V, wt.shape[1])
    pad = lambda part: jnp.pad(part, ((0, 0), (0, LANES - part.shape[1]), (0, 0))).reshape(A_HEADS * LANES, wt.shape[1])
    return pad(w3[:, :A_NOPE]), pad(w3[:, A_NOPE:])


def _merge_w_kv_b_grad_t(gk, gv):
    gk3 = gk.reshape(A_HEADS, LANES, gk.shape[1])[:, :A_NOPE]
    gv3 = gv.reshape(A_HEADS, LANES, gv.shape[1])[:, :A_V]
    return jnp.concatenate([gk3, gv3], axis=1).reshape(A_HEADS * (A_NOPE + A_V), gk.shape[1])


def _local_step(x, pos, target, nw, w, start_late_reduce):
    w_in = w["w_in"]
    w_a2, w_g = _split_w_in_t(w_in)
    w_qb = _pad_heads(w["w_q_b"], A_NOPE + A_ROPE)
    w_k, w_v = _split_w_kv_b_t(w["w_kv_b"])
    w_out_m, w_out_a = w["w_out"][:M_WIDTH], _pad_heads(w["w_out"][M_WIDTH:], A_V)
    w_gate, w_up, w_down = w["w_gate"], w["w_up"], w["w_down"]
    bias = jnp.pad(nw["b_gates"], ((0, 0), (0, LANES - 2 * M_HEADS)))
    mnorm = nw["mlstm_norm_w"].reshape(M_HEADS, 1, M_HEAD_DIM)
    n_m4 = 4 * M_WIDTH

    u1, a2, qa_n, kv_n, gates = _norm_proj_latents(x, nw["attn_norm_w"], w_a2, w_g, nw["q_a_norm_w"],
                                                   nw["kv_a_norm_w"], "norm_proj_latents")
    m4 = _matmul(u1, w_in, "nt", BF16, "proj_mlstm", b_rows=n_m4, lead_scale=(M_WIDTH, M_SCALE))
    qk_t = _matmul(w_in, u1, "nt", BF16, "proj_qk_t", a_rows=2 * M_WIDTH, lead_row_scale=(M_WIDTH, M_SCALE),
                   col_chunk=M_CHUNK)
    hm, c_prev_all, nm_all = _mlstm_fwd(m4, qk_t, gates, bias, mnorm, "mlstm_fwd")
    qa, kva = a2[:, :Q_RANK], a2[:, Q_RANK:]
    qr, kr, vv = _qkv_up_rope(qa_n, kv_n, w_qb, w_k, w_v, gates, pos, "qkv_up_rope")
    ha, qb = _flash_fwd(qr, kr, vv, "attn_fwd")
    h1, u2 = _out_proj_norm(hm, ha, w_out_m, w_out_a, x, nw["ffn_norm_w"], "out_proj")
    gg, uu, act = _ffn_gate_up(u2, w_gate, w_up, "ffn_gate_up")

    dh2, dh2_b, loss, g_final = _ffn_down_loss(act, w_down, h1, target, nw["final_norm_w"].reshape(1, D_MODEL),
                                               "ffn_down_loss")
    grads = {"final_norm_w": g_final.reshape(D_MODEL)}
    grads_b = {}
    dgg, duu = _ffn_down_bwd(dh2_b, w_down, gg, uu, "ffn_down_bwd")
    grads["w_down"], grads_b["w_down"] = _matmul(act, dh2_b, "tn", F32, "g_w_down", tm=256, tn=D_MODEL, also_bf16=True)
    dh1, dh1_b, grads["ffn_norm_w"] = _matmul_norm_bwd([dgg, duu], [w_gate, w_up], h1, nw["ffn_norm_w"], "d_u2", F32,
                                                       residual=dh2, also_bf16=True, tm=256)
    grads["w_gate"], grads_b["w_gate"] = _matmul(dgg, u2, "tn", F32, "g_w_gate", tm=256, tn=D_MODEL, also_bf16=True)
    grads["w_up"], grads_b["w_up"] = _matmul(duu, u2, "tn", F32, "g_w_up", tm=256, tn=D_MODEL, also_bf16=True)
    dhm, doe = _d_out_proj(dh1_b, w_out_m, w_out_a, ha, "d_out_proj")
    grads["w_out"] = jnp.concatenate([_matmul(hm, dh1_b, "tn", F32, "g_w_out_m", tn=D_MODEL),
                                      _unpad_heads(_matmul(ha, dh1_b, "tn", F32, "g_w_out_a", tn=D_MODEL), A_V)], axis=0)

    late, doe = start_late_reduce(grads, grads_b, doe)

    dqr, dkr, dvv = _flash_bwd(qb, kr, vv, doe, "attn_bwd")
    dm4, dgates, small = _mlstm_bwd(m4, qk_t, gates, bias, mnorm, c_prev_all, nm_all, dhm, "mlstm_bwd")
    grads["mlstm_norm_w"] = small[:M_HEADS].reshape(1, M_HEADS, M_HEAD_DIM)
    grads["b_gates"] = small[M_HEADS:M_HEADS + 1, :2 * M_HEADS]
    dqpre, dgk = _rope_bwd(dqr, dkr, dgates, pos, "rope_bwd")
    grads["w_q_b"] = _unpad_heads(_matmul(dqpre, qa_n, "tn", F32, "g_w_q_b"), A_NOPE + A_ROPE)
    grads["w_kv_b"] = _merge_w_kv_b_grad_t(_matmul(dkr, kv_n, "tn", F32, "g_w_k"),
                                           _matmul(dvv, kv_n, "tn", F32, "g_w_v"))
    dqa, grads["q_a_norm_w"] = _matmul_norm_bwd([dqpre], [w_qb], qa, nw["q_a_norm_w"], "d_qa", BF16)
    dkva, grads["kv_a_norm_w"] = _matmul_norm_bwd([dkr, dvv], [w_k, w_v], kva, nw["kv_a_norm_w"], "d_kva", BF16)
    da2 = jnp.concatenate([dqa, dkva], axis=1)
    grads["w_in"] = _merge_w_in_grad_t(_matmul(dm4, u1, "tn", F32, "g_w_m4", tn=D_MODEL),
                                       _matmul(da2, u1, "tn", F32, "g_w_a2", tm=640, tn=D_MODEL),
                                       _matmul(dgk, u1, "tn", F32, "g_w_g", tn=D_MODEL))
    grad_x, grads["attn_norm_w"] = _matmul_norm_bwd([dm4, da2, dgk], [w_in, w_a2, w_g], x, nw["attn_norm_w"], "d_u1",
                                                    F32, residual=dh1, w_rows=[n_m4, None, None])
    return loss, grad_x, grads, grads_b, late


MESH = pl.DeviceIdType.MESH
N_CHIPS = 4
EARLY = ("w_in", "w_q_b", "w_kv_b")
LATE = ("w_out", "w_gate", "w_up", "w_down")
BIG = EARLY + LATE
TRANSPOSED = ("w_in", "w_q_b", "w_kv_b", "w_gate", "w_up")
LANE_HALVED = ("w_in", "w_out", "w_gate", "w_up", "w_down")
SMALL = ("attn_norm_w", "b_gates", "mlstm_norm_w", "q_a_norm_w", "kv_a_norm_w", "ffn_norm_w", "final_norm_w")
HBM_SPEC = pl.BlockSpec(memory_space=pltpu.HBM)


def _stored(name, a):
    return a[0].T if name in TRANSPOSED else a[0]


def _unstored(name, a):
    return (a.T if name in TRANSPOSED else a)[None]


def _half_shape(name, shape):
    rs, cs = shape
    return (rs, cs // 2) if name in LANE_HALVED else (rs // 2, cs)


def _half(ref, name, h, *lead):
    rs, cs = ref.shape[-2:]
    if name in LANE_HALVED:
        return ref.at[(*lead, slice(None), pl.ds(h * (cs // 2), cs // 2))]
    return ref.at[(*lead, pl.ds(h * (rs // 2), rs // 2), slice(None))]


def _place():
    x, y, c = lax.axis_index("x"), lax.axis_index("y"), lax.axis_index("c")
    others = [(1 - x, y), (x, 1 - y), (1 - x, 1 - y)]
    return x, y, c, others


def _gather_weights(names, shards):
    n = len(shards)

    def body(*refs):
        _gather_body(names, refs[:n], refs[n:2 * n], *refs[2 * n:])

    gathered = pl.pallas_call(
        body, name="gather_weights", in_specs=[HBM_SPEC] * n, out_specs=[HBM_SPEC] * n,
        out_shape=[jax.ShapeDtypeStruct((N_CHIPS,) + s.shape, s.dtype) for s in shards],
        scratch_shapes=[pltpu.SemaphoreType.DMA((6 * n,)), pltpu.SemaphoreType.DMA((6 * n,))],
    )(*shards)
    return _with_own_slab(gathered, shards)


def _with_own_slab(gathered, shards):
    me = 2 * lax.axis_index("x") + lax.axis_index("y")
    return [lax.dynamic_update_slice(g, s[None], (me, 0, 0)) for g, s in zip(gathered, shards)]


def _gather_body(names, ins, outs, send_sems, recv_sems):
    x, y, c, others = _place()
    me = 2 * x + y
    sibling = (x, y, 1 - c)

    def copy(w, k, slab, core, to, src=None):
        dst = _half(outs[w], names[w], core, slab)
        return pltpu.make_async_remote_copy(
            src_ref=dst if src is None else src, dst_ref=dst, send_sem=send_sems.at[w * 6 + k],
            recv_sem=recv_sems.at[w * 6 + k], device_id=to, device_id_type=MESH)

    sends = []
    for w in range(len(names)):
        for j, chip in enumerate(others):
            cp = copy(w, j, me, c, (*chip, c), src=_half(ins[w], names[w], c))
            cp.start()
            sends.append(cp)
    for w in range(len(names)):
        for j, (ox, oy) in enumerate(others):
            slab = 2 * ox + oy
            copy(w, j, slab, c, (x, y, c)).wait_recv()
            fwd = copy(w, 3 + j, slab, c, sibling)
            fwd.start()
            sends.append(fwd)
    for w in range(len(names)):
        for j, (ox, oy) in enumerate(others):
            copy(w, 3 + j, 2 * ox + oy, 1 - c, (x, y, c)).wait_recv()
    for cp in sends:
        cp.wait_send()


GATHER_LATE_COLLECTIVE_ID = 1


def _gather_weights_async(names, shards):
    n = len(shards)
    src = [jax.new_ref(s, memory_space=pltpu.MemorySpace.HBM) for s in shards]
    dst = [jax.empty_ref(jax.ShapeDtypeStruct((N_CHIPS,) + s.shape, s.dtype), memory_space=pltpu.MemorySpace.HBM)
           for s in shards]

    @pl.kernel(mesh=plsc.ScalarSubcoreMesh(axis_name="sequencer", num_cores=1), name="gather_weights_async",
               scratch_types=(pltpu.SemaphoreType.DMA((6 * n,)), pltpu.SemaphoreType.DMA((6 * n,))),
               compiler_params=pltpu.CompilerParams(collective_id=GATHER_LATE_COLLECTIVE_ID))
    def launch(send_sems, recv_sems):
        x, y, c, others = _place()
        peers = [(ox, oy, c) for ox, oy in others] + [(x, y, 1 - c)]
        barrier = pltpu.get_barrier_semaphore()
        for peer in peers:
            pl.semaphore_signal(barrier, inc=1, device_id=peer, device_id_type=MESH)
        pl.semaphore_wait(barrier, len(peers))
        _gather_body(names, src, dst, send_sems, recv_sems)

    launch()
    return _with_own_slab([d[...] for d in dst], shards)


def _exchange(arrays, out_shapes, plan, copies_per_array, name):
    n = len(arrays)

    def body(*refs):
        ins, outs = refs[:n], refs[n:2 * n]
        send_sems, recv_sems = refs[2 * n:]
        rem = [pltpu.make_async_remote_copy(src_ref=s, dst_ref=d, send_sem=send_sems.at[k], recv_sem=recv_sems.at[k],
                                            device_id=to, device_id_type=MESH)
               for k, (s, d, to) in enumerate(plan(ins, outs, _place()))]
        for cp in rem:
            cp.start()
        for cp in rem:
            cp.wait_recv()
        for cp in rem:
            cp.wait_send()

    return pl.pallas_call(
        body, name=name, in_specs=[HBM_SPEC] * n, out_specs=[HBM_SPEC] * n, out_shape=out_shapes,
        scratch_shapes=[pltpu.SemaphoreType.DMA((copies_per_array * n,)),
                        pltpu.SemaphoreType.DMA((copies_per_array * n,))],
    )(*arrays)


def _plan_to_sibling(ins, outs, place):
    x, y, c, _ = place
    return [(ins[w], outs[w], (x, y, 1 - c)) for w in range(len(ins))]


N_PEERS = 7


def _shard_sum_direct(g3, got7, name, where):
    hs = tuple(got7.shape[1:])
    if name in LANE_HALVED:
        g_in, g_spec = g3, pl.BlockSpec((1,) + hs, lambda i, p: (p[0], 0, p[1]))
    else:
        g_in, g_spec = g3.reshape((N_CHIPS, 2) + hs), pl.BlockSpec((1, 1) + hs, lambda i, p: (p[0], p[1], 0, 0))

    def body(p_ref, g_ref, r_ref, o_ref):
        acc = g_ref[(0,) * (len(g_ref.shape) - 2)]
        for k in range(N_PEERS):
            acc = acc + r_ref[k].astype(F32)
        o_ref[...] = acc

    return pl.pallas_call(
        body, name="shard_sum_%s" % name,
        grid_spec=pltpu.PrefetchScalarGridSpec(
            num_scalar_prefetch=1, grid=(1,),
            in_specs=[g_spec, pl.BlockSpec((N_PEERS,) + hs, lambda i, p: (0, 0, 0))],
            out_specs=pl.BlockSpec(hs, lambda i, p: (0, 0))),
        out_shape=jax.ShapeDtypeStruct(hs, F32), compiler_params=_cparams("arbitrary"),
    )(where, g_in, got7)


def _adamw_halves(w, m, v, mine, other, name, core):
    hs = tuple(mine.shape)
    full = pl.BlockSpec(hs, (lambda h, c_ref: (0, h)) if name in LANE_HALVED else (lambda h, c_ref: (h, 0)))
    half = pl.BlockSpec(hs, lambda h, c_ref: (0, 0))

    def body(c_ref, w_ref, m_ref, v_ref, a_ref, b_ref, g_ref, d_ref, mo_ref, vo_ref):
        g = jnp.where(pl.program_id(0) == c_ref[0], a_ref[...], b_ref[...])
        delta, m_new, v_new = _adamw_math(w_ref[...], g, m_ref[...], v_ref[...])
        g_ref[...] = g
        d_ref[...] = delta
        mo_ref[...] = m_new
        vo_ref[...] = v_new

    out = jax.ShapeDtypeStruct(w.shape, F32)
    return pl.pallas_call(
        body, name="adamw_%s" % name,
        grid_spec=pltpu.PrefetchScalarGridSpec(num_scalar_prefetch=1, grid=(2,), in_specs=[full, full, full, half, half],
                                               out_specs=[full] * 4),
        out_shape=[out] * 4, compiler_params=_cparams("parallel"),
    )(core, w, m, v, mine, other)


def _adamw_math(w, g, m, v):
    m = ADAM_B1 * m + (1.0 - ADAM_B1) * g
    v = ADAM_B2 * v + (1.0 - ADAM_B2) * (g * g)
    m_hat = m / (1.0 - ADAM_B1 ** ADAM_STEP)
    v_hat = v / (1.0 - ADAM_B2 ** ADAM_STEP)
    delta = -ADAM_LR * (m_hat / (jnp.sqrt(v_hat) + ADAM_EPS) + ADAM_WD * w)
    return delta, m, v


def _core_index():
    return lax.axis_index("c").astype(jnp.int32).reshape(1)


DIRECT_REDUCE_COLLECTIVE_ID = {"late": 2, "early": 3}


def _exchange_contributions_async(names, gb3, tag):
    n = len(gb3)
    src = [jax.new_ref(g, memory_space=pltpu.MemorySpace.HBM) for g in gb3]
    dst = [jax.empty_ref(jax.ShapeDtypeStruct((N_PEERS,) + _half_shape(nm, g.shape[1:]), g.dtype),
                         memory_space=pltpu.MemorySpace.HBM) for nm, g in zip(names, gb3)]

    @pl.kernel(mesh=plsc.ScalarSubcoreMesh(axis_name="sequencer", num_cores=1), name="contributions_async_" + tag,
               scratch_types=(pltpu.SemaphoreType.DMA((N_PEERS * n,)), pltpu.SemaphoreType.DMA((N_PEERS * n,))),
               compiler_params=pltpu.CompilerParams(collective_id=DIRECT_REDUCE_COLLECTIVE_ID[tag]))
    def launch(send_sems, recv_sems):
        x, y, c, _ = _place()
        peers = [(x ^ ((k >> 2) & 1), y ^ ((k >> 1) & 1), c ^ (k & 1)) for k in range(1, N_PEERS + 1)]
        barrier = pltpu.get_barrier_semaphore()
        for peer in peers:
            pl.semaphore_signal(barrier, inc=1, device_id=peer, device_id_type=MESH)
        pl.semaphore_wait(barrier, N_PEERS)
        rem = []
        for w in range(n):
            for k, (px, py, pc) in enumerate(peers):
                rem.append(pltpu.make_async_remote_copy(
                    src_ref=_half(src[w], names[w], pc, 2 * px + py), dst_ref=dst[w].at[k],
                    send_sem=send_sems.at[N_PEERS * w + k], recv_sem=recv_sems.at[N_PEERS * w + k],
                    device_id=(px, py, pc), device_id_type=MESH))
        for cp in rem:
            cp.start()
        for cp in rem:
            cp.wait_recv()
        for cp in rem:
            cp.wait_send()

    launch()
    return [d[...] for d in dst]


def _shard_index():
    return (2 * lax.axis_index("x") + lax.axis_index("y")).astype(jnp.int32).reshape(1)


def _with_sibling_half(names, mine, tag):
    other = _exchange(mine, [jax.ShapeDtypeStruct(m.shape, F32) for m in mine], _plan_to_sibling, 1,
                      "sibling_result_" + tag)
    return {n: (a, b) for n, a, b in zip(names, mine, other)}


SMALL_ROWS = 8
SMALL_LAYOUT = {"attn_norm_w": (0, 0, 1024), "ffn_norm_w": (1, 0, 1024), "final_norm_w": (2, 0, 1024),
                "q_a_norm_w": (3, 0, 384), "kv_a_norm_w": (3, 384, 256), "mlstm_norm_w": (4, 0, 512),
                "b_gates": (4, 512, 8)}
LOSS_SLOT = (5, 0)


def _small_allreduce_adamw(grads, loss, wts, mom, vel):
    shapes = {n: wts[n].shape for n in SMALL}
    flat = lambda d: [d[n].reshape(1, SMALL_LAYOUT[n][2]) for n in SMALL]
    ns = len(SMALL)

    def body(*refs):
        g_in, loss_ref = refs[:ns], refs[ns]
        w_in_, m_in, v_in = refs[ns + 1:2 * ns + 1], refs[2 * ns + 1:3 * ns + 1], refs[3 * ns + 1:4 * ns + 1]
        outs = refs[4 * ns + 1:8 * ns + 2]
        tile, slots, send_sems, recv_sems = refs[8 * ns + 2:]
        x, y, c, _ = _place()
        me = 4 * x + 2 * y + c
        tile[...] = jnp.zeros_like(tile)
        for n, g_ref in zip(SMALL, g_in):
            r, c0, width = SMALL_LAYOUT[n]
            tile[r:r + 1, c0:c0 + width] = g_ref[...]
        tile[LOSS_SLOT[0]:LOSS_SLOT[0] + 1, 0:LANES] = loss_ref[...]
        slots[me] = tile[...]
        copies = []
        for k in range(1, N_PEERS + 1):
            to = (x ^ ((k >> 2) & 1), y ^ ((k >> 1) & 1), c ^ (k & 1))
            cp = pltpu.make_async_remote_copy(src_ref=tile, dst_ref=slots.at[me], send_sem=send_sems.at[k - 1],
                                              recv_sem=recv_sems.at[k - 1], device_id=to, device_id_type=MESH)
            cp.start()
            copies.append(cp)
        for k in range(1, N_PEERS + 1):
            pltpu.make_async_remote_copy(src_ref=tile, dst_ref=slots.at[me ^ k], send_sem=send_sems.at[k - 1],
                                         recv_sem=recv_sems.at[k - 1], device_id=(x, y, c),
                                         device_id_type=MESH).wait_recv()
        for cp in copies:
            cp.wait_send()
        total = slots[0]
        for d in range(1, N_PEERS + 1):
            total = total + slots[d]
        for i, n in enumerate(SMALL):
            r, c0, width = SMALL_LAYOUT[n]
            g = total[r:r + 1, c0:c0 + width]
            delta, m_new, v_new = _adamw_math(w_in_[i][...], g, m_in[i][...], v_in[i][...])
            outs[i][...] = g
            outs[ns + i][...] = delta
            outs[2 * ns + i][...] = m_new
            outs[3 * ns + i][...] = v_new
        outs[4 * ns][...] = total[LOSS_SLOT[0]:LOSS_SLOT[0] + 1, 0:LANES]

    vm = pl.BlockSpec(memory_space=pltpu.VMEM)
    vec = [jax.ShapeDtypeStruct((1, SMALL_LAYOUT[n][2]), F32) for n in SMALL]
    res = pl.pallas_call(
        body, name="small_allreduce_adamw", in_specs=[vm] * (4 * ns + 1), out_specs=[vm] * (4 * ns + 1),
        out_shape=vec * 4 + [jax.ShapeDtypeStruct((1, LANES), F32)],
        scratch_shapes=[pltpu.VMEM((SMALL_ROWS, D_MODEL), F32), pltpu.VMEM((N_PEERS + 1, SMALL_ROWS, D_MODEL), F32),
                        pltpu.SemaphoreType.DMA((N_PEERS,)), pltpu.SemaphoreType.DMA((N_PEERS,))],
    )(*flat(grads), loss, *flat(wts), *flat(mom), *flat(vel))
    groups = [{n: r.reshape(shapes[n]) for n, r in zip(SMALL, res[i * ns:(i + 1) * ns])} for i in range(4)]
    return (*groups, res[4 * ns][0, 0])


def kernel(x, positions, attn_norm_w, w_in, b_gates, mlstm_norm_w, q_a_norm_w, w_q_b, kv_a_norm_w, w_kv_b, w_out, ffn_norm_w, w_gate, w_up, w_down, final_norm_w, loss_target, m_attn_norm_w, m_w_in, m_b_gates, m_mlstm_norm_w, m_q_a_norm_w, m_w_q_b, m_kv_a_norm_w, m_w_kv_b, m_w_out, m_ffn_norm_w, m_w_gate, m_w_up, m_w_down, m_final_norm_w, v_attn_norm_w, v_w_in, v_b_gates, v_mlstm_norm_w, v_q_a_norm_w, v_w_q_b, v_kv_a_norm_w, v_w_kv_b, v_w_out, v_ffn_norm_w, v_w_gate, v_w_up, v_w_down, v_final_norm_w):
    names = ("attn_norm_w", "w_in", "b_gates", "mlstm_norm_w", "q_a_norm_w", "w_q_b", "kv_a_norm_w", "w_kv_b", "w_out",
             "ffn_norm_w", "w_gate", "w_up", "w_down", "final_norm_w")
    wts = dict(zip(names, (attn_norm_w, w_in, b_gates, mlstm_norm_w, q_a_norm_w, w_q_b, kv_a_norm_w, w_kv_b, w_out,
                           ffn_norm_w, w_gate, w_up, w_down, final_norm_w)))
    mom = dict(zip(names, (m_attn_norm_w, m_w_in, m_b_gates, m_mlstm_norm_w, m_q_a_norm_w, m_w_q_b, m_kv_a_norm_w,
                           m_w_kv_b, m_w_out, m_ffn_norm_w, m_w_gate, m_w_up, m_w_down, m_final_norm_w)))
    vel = dict(zip(names, (v_attn_norm_w, v_w_in, v_b_gates, v_mlstm_norm_w, v_q_a_norm_w, v_w_q_b, v_kv_a_norm_w,
                           v_w_kv_b, v_w_out, v_ffn_norm_w, v_w_gate, v_w_up, v_w_down, v_final_norm_w)))
    t = x.shape[1]

    shards = {n: _stored(n, wts[n]).astype(BF16) for n in BIG}
    first, later = ("w_in",), tuple(n for n in BIG if n != "w_in")
    now = _gather_weights(first, [shards[n] for n in first])
    now, later_in = lax.optimization_barrier((now, [shards[n] for n in later]))
    behind = _gather_weights_async(later, later_in)
    full = {n: g.reshape(N_CHIPS * g.shape[1], g.shape[2]) for n, g in zip(first + later, list(now) + list(behind))}

    nw = {n: wts[n] for n in SMALL}
    by_shard = lambda g: g.reshape(N_CHIPS, g.shape[0] // N_CHIPS, g.shape[1])

    def start_reduce(group, tag, grads, grads_b, marker):
        g3 = [by_shard(grads[n]) for n in group]
        gb3 = [by_shard(grads_b[n]) if n in grads_b else g.astype(BF16) for n, g in zip(group, g3)]
        gb3, marker = lax.optimization_barrier((gb3, marker))
        return (g3, _exchange_contributions_async(group, gb3, tag)), marker

    loss, grad_x, grads, grads_b, (late_g3, late_got7) = _local_step(
        x[0], positions.reshape(t, 1), loss_target[0], nw, full, functools.partial(start_reduce, LATE, "late"))

    (early_g3, early_got7), late_got7 = start_reduce(EARLY, "early", grads, grads_b, list(late_got7))
    where = jnp.concatenate([_shard_index(), _core_index()])
    outs_g, outs_d, outs_m, outs_v = {}, {}, {}, {}

    def finish(group, g3, got7, tag):
        mine = [_shard_sum_direct(g, r7, n, where) for n, g, r7 in zip(group, g3, got7)]
        halves = _with_sibling_half(group, mine, tag)
        for n in group:
            res = _adamw_halves(_stored(n, wts[n]), _stored(n, mom[n]), _stored(n, vel[n]), *halves[n], n, where[1:])
            outs_g[n], outs_d[n], outs_m[n], outs_v[n] = [_unstored(n, r) for r in res]

    finish(LATE, late_g3, late_got7, "late")
    small_g, small_d, small_m, small_v, total_loss = _small_allreduce_adamw(grads, loss, wts, mom, vel)
    finish(EARLY, early_g3, early_got7, "early")
    outs_g.update(small_g)
    outs_d.update(small_d)
    outs_m.update(small_m)
    outs_v.update(small_v)
    return (total_loss, grad_x[None], *[outs_g[n] for n in names], *[outs_d[n] for n in names],
            *[outs_m[n] for n in names], *[outs_v[n] for n in names])
```
